```python
import jax, jax.numpy as jnp
from jax import lax
import numpy as np

D_MODEL = 2048
BATCH = 8
SEQ = 8192
DEPTH = 1

POOL_WINDOWS = (2, 4, 8, 16)
POOL_GROUPS = len(POOL_WINDOWS)
POOL_GROUP_WIDTH = D_MODEL // 8
POOL_WIDTH = POOL_GROUPS * POOL_GROUP_WIDTH
LRU_WIDTH = D_MODEL
LRU_BLOCK_WIDTH = 256
LRU_BLOCKS = LRU_WIDTH // LRU_BLOCK_WIDTH
LRU_CONV_WIDTH = 4
LRU_C = 8.0
LRU_A_MIN = 0.9
LRU_A_MAX = 0.999
N_BRANCHES = 2
IN_WIDTH = POOL_WIDTH + 2 * LRU_WIDTH + N_BRANCHES * D_MODEL
D_FF = 3 * D_MODEL
FFN_CONV_WIDTH = 3
EPS = 1e-6

kernel_name = "hybrid_pool_rglru_gated_block"


def rms_norm(x, g):
    xf = x.astype(jnp.float32)
    y = xf * lax.rsqrt(jnp.mean(xf * xf, axis=-1, keepdims=True) + EPS)
    return (y * g.astype(jnp.float32)).astype(x.dtype)


def causal_depthwise_conv(x, w, b):
    K = w.shape[0]
    S = x.shape[1]
    xp = jnp.pad(x, ((0, 0), (K - 1, 0), (0, 0)))
    out = b
    for k in range(K):
        out = out + xp[:, k:k + S] * w[k]
    return out


def pool_mixer(u, w_pool, pool_scale):
    B, S, _ = u.shape
    uf = u.astype(jnp.float32)
    c = jnp.cumsum(uf, axis=1)
    pos = jnp.arange(1, S + 1, dtype=jnp.float32)
    means = []
    for g, w in enumerate(POOL_WINDOWS):
        cg = c[..., g * POOL_GROUP_WIDTH:(g + 1) * POOL_GROUP_WIDTH]
        shifted = jnp.pad(cg, ((0, 0), (w, 0), (0, 0)))[:, :S]
        count = jnp.minimum(pos, float(w))[None, :, None]
        means.append((cg - shifted) / count)
    mean = jnp.stack(means, axis=2)
    d = (mean - uf.reshape(B, S, POOL_GROUPS, POOL_GROUP_WIDTH)).astype(u.dtype)
    y = jnp.einsum('bsgc,gcd->bsgd', d, w_pool).reshape(B, S, POOL_WIDTH)
    return y * pool_scale


def rg_lru(x, w_a, b_a, w_i, b_i, lam):
    B, S, R = x.shape
    xb = x.reshape(B, S, LRU_BLOCKS, LRU_BLOCK_WIDTH)
    r = jax.nn.sigmoid(jnp.einsum('bshc,hcd->bshd', xb, w_a).reshape(B, S, R) + b_a)
    i = jax.nn.sigmoid(jnp.einsum('bshc,hcd->bshd', xb, w_i).reshape(B, S, R) + b_i)
    log_a = -LRU_C * r.astype(jnp.float32) * jax.nn.softplus(-lam.astype(jnp.float32))
    a = jnp.exp(log_a)
    mult = jnp.sqrt(-jnp.expm1(2.0 * log_a))
    bx = mult * (i * x).astype(jnp.float32)

    def combine(left, right):
        a1, b1 = left
        a2, b2 = right
        return a1 * a2, a2 * b1 + b2

    _, h = lax.associative_scan(combine, (a, bx), axis=1)
    return h.astype(x.dtype)


def _fwd_setup_inputs(seed: int = 0) -> dict:
    key = jax.random.key(seed)
    ks = jax.random.split(key, 24)
    f32 = jnp.float32

    def nrm(k, shape, fan_in):
        return jax.random.normal(k, shape, f32) * (fan_in ** -0.5)

    def gain(k, shape):
        return 1.0 + 0.02 * jax.random.normal(k, shape, f32)

    def bias(k, shape):
        return 0.01 * jax.random.normal(k, shape, f32)

    L = DEPTH
    u = jax.random.uniform(ks[12], (L, LRU_WIDTH), f32, LRU_A_MIN, LRU_A_MAX)
    s = u ** (1.0 / LRU_C)
    lru_lambda = jnp.log(s) - jnp.log1p(-s)
    return {
        "x": jax.random.normal(ks[0], (BATCH, SEQ, D_MODEL), f32),
        "g_mix": gain(ks[1], (L, D_MODEL)),
        "w_in": nrm(ks[2], (L, D_MODEL, IN_WIDTH), D_MODEL),
        "b_gate": bias(ks[3], (L, N_BRANCHES * D_MODEL)),
        "w_pool": nrm(ks[4], (L, POOL_GROUPS, POOL_GROUP_WIDTH, POOL_GROUP_WIDTH), POOL_GROUP_WIDTH),
        "pool_scale": gain(ks[5], (L, POOL_WIDTH)),
        "lru_conv_w": nrm(ks[6], (L, LRU_CONV_WIDTH, LRU_WIDTH), LRU_CONV_WIDTH),
        "lru_conv_b": bias(ks[7], (L, LRU_WIDTH)),
        "w_a": nrm(ks[8], (L, LRU_BLOCKS, LRU_BLOCK_WIDTH, LRU_BLOCK_WIDTH), LRU_BLOCK_WIDTH),
        "b_a": bias(ks[9], (L, LRU_WIDTH)),
        "w_i": nrm(ks[10], (L, LRU_BLOCKS, LRU_BLOCK_WIDTH, LRU_BLOCK_WIDTH), LRU_BLOCK_WIDTH),
        "b_i": bias(ks[11], (L, LRU_WIDTH)),
        "lru_lambda": lru_lambda,
        "w_pool_proj": nrm(ks[13], (L, POOL_WIDTH, D_MODEL), POOL_WIDTH),
        "w_lru_proj": nrm(ks[14], (L, LRU_WIDTH, D_MODEL), LRU_WIDTH),
        "w_out": nrm(ks[15], (L, D_MODEL, D_MODEL), D_MODEL),
        "g_mlp": gain(ks[16], (L, D_MODEL)),
        "w_up": nrm(ks[17], (L, D_MODEL, 2 * D_FF), D_MODEL),
        "ffn_conv_w": nrm(ks[18], (L, FFN_CONV_WIDTH, D_FF), FFN_CONV_WIDTH),
        "ffn_conv_b": bias(ks[19], (L, D_FF)),
        "w_down": nrm(ks[20], (L, D_FF, D_MODEL), D_FF),
        "g_final": gain(ks[21], (D_MODEL,)),
    }


def _fwd_reference(x, g_mix, w_in, b_gate, w_pool, pool_scale, lru_conv_w, lru_conv_b,
              w_a, b_a, w_i, b_i, lru_lambda, w_pool_proj, w_lru_proj, w_out,
              g_mlp, w_up, ffn_conv_w, ffn_conv_b, w_down, g_final):
    B, S, D = x.shape
    for l in range(DEPTH):
        h = rms_norm(x, g_mix[l])
        proj = h @ w_in[l]
        p0 = POOL_WIDTH
        p1 = p0 + LRU_WIDTH
        p2 = p1 + LRU_WIDTH
        u_pool = proj[..., :p0]
        u_lru = proj[..., p0:p1]
        u_gelu = proj[..., p1:p2]
        gates = jax.nn.sigmoid(proj[..., p2:] + b_gate[l]).reshape(B, S, N_BRANCHES, D)

        y_pool = pool_mixer(u_pool, w_pool[l], pool_scale[l])
        v = causal_depthwise_conv(u_lru, lru_conv_w[l], lru_conv_b[l])
        y_lru = rg_lru(v, w_a[l], b_a[l], w_i[l], b_i[l], lru_lambda[l]) * jax.nn.gelu(u_gelu)

        merged = (gates[:, :, 0] * (y_pool @ w_pool_proj[l])
                  + gates[:, :, 1] * (y_lru @ w_lru_proj[l]))
        x = x + merged @ w_out[l]

        h2 = rms_norm(x, g_mlp[l])
        up = h2 @ w_up[l]
        gate_pre = up[..., :D_FF]
        val = up[..., D_FF:]
        gate = jax.nn.gelu(causal_depthwise_conv(gate_pre, ffn_conv_w[l], ffn_conv_b[l]))
        x = x + (gate * val) @ w_down[l]
    return rms_norm(x, g_final)


import jax as _jax
import jax.numpy as _jnp

TWIN_FORMAT = 'train_step'
FWD_PARAMS = ['x', 'g_mix', 'w_in', 'b_gate', 'w_pool', 'pool_scale', 'lru_conv_w', 'lru_conv_b', 'w_a', 'b_a', 'w_i', 'b_i', 'lru_lambda', 'w_pool_proj', 'w_lru_proj', 'w_out', 'g_mlp', 'w_up', 'ffn_conv_w', 'ffn_conv_b', 'w_down', 'g_final']
TWIN_WEIGHTS = ['g_mix', 'w_in', 'b_gate', 'w_pool', 'pool_scale', 'lru_conv_w', 'lru_conv_b', 'w_a', 'b_a', 'w_i', 'b_i', 'lru_lambda', 'w_pool_proj', 'w_lru_proj', 'w_out', 'g_mlp', 'w_up', 'ffn_conv_w', 'ffn_conv_b', 'w_down', 'g_final']
TWIN_DIFF_INPUT = 'x'
TWIN_INPUTS = ['x', 'g_mix', 'w_in', 'b_gate', 'w_pool', 'pool_scale', 'lru_conv_w', 'lru_conv_b', 'w_a', 'b_a', 'w_i', 'b_i', 'lru_lambda', 'w_pool_proj', 'w_lru_proj', 'w_out', 'g_mlp', 'w_up', 'ffn_conv_w', 'ffn_conv_b', 'w_down', 'g_final', 'loss_target', 'm_g_mix', 'm_w_in', 'm_b_gate', 'm_w_pool', 'm_pool_scale', 'm_lru_conv_w', 'm_lru_conv_b', 'm_w_a', 'm_b_a', 'm_w_i', 'm_b_i', 'm_lru_lambda', 'm_w_pool_proj', 'm_w_lru_proj', 'm_w_out', 'm_g_mlp', 'm_w_up', 'm_ffn_conv_w', 'm_ffn_conv_b', 'm_w_down', 'm_g_final', 'v_g_mix', 'v_w_in', 'v_b_gate', 'v_w_pool', 'v_pool_scale', 'v_lru_conv_w', 'v_lru_conv_b', 'v_w_a', 'v_b_a', 'v_w_i', 'v_b_i', 'v_lru_lambda', 'v_w_pool_proj', 'v_w_lru_proj', 'v_w_out', 'v_g_mlp', 'v_w_up', 'v_ffn_conv_w', 'v_ffn_conv_b', 'v_w_down', 'v_g_final']
TWIN_OUTPUTS = ['loss', 'grad_x', 'grad_g_mix', 'grad_w_in', 'grad_b_gate', 'grad_w_pool', 'grad_pool_scale', 'grad_lru_conv_w', 'grad_lru_conv_b', 'grad_w_a', 'grad_b_a', 'grad_w_i', 'grad_b_i', 'grad_lru_lambda', 'grad_w_pool_proj', 'grad_w_lru_proj', 'grad_w_out', 'grad_g_mlp', 'grad_w_up', 'grad_ffn_conv_w', 'grad_ffn_conv_b', 'grad_w_down', 'grad_g_final', 'delta_g_mix', 'delta_w_in', 'delta_b_gate', 'delta_w_pool', 'delta_pool_scale', 'delta_lru_conv_w', 'delta_lru_conv_b', 'delta_w_a', 'delta_b_a', 'delta_w_i', 'delta_b_i', 'delta_lru_lambda', 'delta_w_pool_proj', 'delta_w_lru_proj', 'delta_w_out', 'delta_g_mlp', 'delta_w_up', 'delta_ffn_conv_w', 'delta_ffn_conv_b', 'delta_w_down', 'delta_g_final', 'new_m_g_mix', 'new_m_w_in', 'new_m_b_gate', 'new_m_w_pool', 'new_m_pool_scale', 'new_m_lru_conv_w', 'new_m_lru_conv_b', 'new_m_w_a', 'new_m_b_a', 'new_m_w_i', 'new_m_b_i', 'new_m_lru_lambda', 'new_m_w_pool_proj', 'new_m_w_lru_proj', 'new_m_w_out', 'new_m_g_mlp', 'new_m_w_up', 'new_m_ffn_conv_w', 'new_m_ffn_conv_b', 'new_m_w_down', 'new_m_g_final', 'new_v_g_mix', 'new_v_w_in', 'new_v_b_gate', 'new_v_w_pool', 'new_v_pool_scale', 'new_v_lru_conv_w', 'new_v_lru_conv_b', 'new_v_w_a', 'new_v_b_a', 'new_v_w_i', 'new_v_b_i', 'new_v_lru_lambda', 'new_v_w_pool_proj', 'new_v_w_lru_proj', 'new_v_w_out', 'new_v_g_mlp', 'new_v_w_up', 'new_v_ffn_conv_w', 'new_v_ffn_conv_b', 'new_v_w_down', 'new_v_g_final']
TWIN_LEAF_KINDS = {'loss': 'loss', 'grad_x': 'grad_x', 'grad_g_mix': 'grad_w', 'grad_w_in': 'grad_w', 'grad_b_gate': 'grad_w', 'grad_w_pool': 'grad_w', 'grad_pool_scale': 'grad_w', 'grad_lru_conv_w': 'grad_w', 'grad_lru_conv_b': 'grad_w', 'grad_w_a': 'grad_w', 'grad_b_a': 'grad_w', 'grad_w_i': 'grad_w', 'grad_b_i': 'grad_w', 'grad_lru_lambda': 'grad_w', 'grad_w_pool_proj': 'grad_w', 'grad_w_lru_proj': 'grad_w', 'grad_w_out': 'grad_w', 'grad_g_mlp': 'grad_w', 'grad_w_up': 'grad_w', 'grad_ffn_conv_w': 'grad_w', 'grad_ffn_conv_b': 'grad_w', 'grad_w_down': 'grad_w', 'grad_g_final': 'grad_w', 'delta_g_mix': 'delta_w', 'delta_w_in': 'delta_w', 'delta_b_gate': 'delta_w', 'delta_w_pool': 'delta_w', 'delta_pool_scale': 'delta_w', 'delta_lru_conv_w': 'delta_w', 'delta_lru_conv_b': 'delta_w', 'delta_w_a': 'delta_w', 'delta_b_a': 'delta_w', 'delta_w_i': 'delta_w', 'delta_b_i': 'delta_w', 'delta_lru_lambda': 'delta_w', 'delta_w_pool_proj': 'delta_w', 'delta_w_lru_proj': 'delta_w', 'delta_w_out': 'delta_w', 'delta_g_mlp': 'delta_w', 'delta_w_up': 'delta_w', 'delta_ffn_conv_w': 'delta_w', 'delta_ffn_conv_b': 'delta_w', 'delta_w_down': 'delta_w', 'delta_g_final': 'delta_w', 'new_m_g_mix': 'new_m', 'new_m_w_in': 'new_m', 'new_m_b_gate': 'new_m', 'new_m_w_pool': 'new_m', 'new_m_pool_scale': 'new_m', 'new_m_lru_conv_w': 'new_m', 'new_m_lru_conv_b': 'new_m', 'new_m_w_a': 'new_m', 'new_m_b_a': 'new_m', 'new_m_w_i': 'new_m', 'new_m_b_i': 'new_m', 'new_m_lru_lambda': 'new_m', 'new_m_w_pool_proj': 'new_m', 'new_m_w_lru_proj': 'new_m', 'new_m_w_out': 'new_m', 'new_m_g_mlp': 'new_m', 'new_m_w_up': 'new_m', 'new_m_ffn_conv_w': 'new_m', 'new_m_ffn_conv_b': 'new_m', 'new_m_w_down': 'new_m', 'new_m_g_final': 'new_m', 'new_v_g_mix': 'new_v', 'new_v_w_in': 'new_v', 'new_v_b_gate': 'new_v', 'new_v_w_pool': 'new_v', 'new_v_pool_scale': 'new_v', 'new_v_lru_conv_w': 'new_v', 'new_v_lru_conv_b': 'new_v', 'new_v_w_a': 'new_v', 'new_v_b_a': 'new_v', 'new_v_w_i': 'new_v', 'new_v_b_i': 'new_v', 'new_v_lru_lambda': 'new_v', 'new_v_w_pool_proj': 'new_v', 'new_v_w_lru_proj': 'new_v', 'new_v_w_out': 'new_v', 'new_v_g_mlp': 'new_v', 'new_v_w_up': 'new_v', 'new_v_ffn_conv_w': 'new_v', 'new_v_ffn_conv_b': 'new_v', 'new_v_w_down': 'new_v', 'new_v_g_final': 'new_v'}


def _forward(args):
    return _fwd_reference(*[args[k] for k in FWD_PARAMS])


def _output_shape():
    def fwd():
        inp = _fwd_setup_inputs(0)
        return _fwd_reference(*[inp[k] for k in FWD_PARAMS])
    out = _jax.eval_shape(fwd)
    return out.shape, out.dtype

N_MICROBATCH = 1
ADAM_LR = 0.001
ADAM_B1 = 0.9
ADAM_B2 = 0.999
ADAM_EPS = 1e-08
ADAM_WD = 0.01
ADAM_STEP = 10
PER_EXAMPLE_BATCH_AXIS = {'x': 0, 'loss_target': 0}
SHARED_INPUTS = []
_WEIGHT_DTYPES = {'g_mix': _jnp.float32, 'w_in': _jnp.float32, 'b_gate': _jnp.float32, 'w_pool': _jnp.float32, 'pool_scale': _jnp.float32, 'lru_conv_w': _jnp.float32, 'lru_conv_b': _jnp.float32, 'w_a': _jnp.float32, 'b_a': _jnp.float32, 'w_i': _jnp.float32, 'b_i': _jnp.float32, 'lru_lambda': _jnp.float32, 'w_pool_proj': _jnp.float32, 'w_lru_proj': _jnp.float32, 'w_out': _jnp.float32, 'g_mlp': _jnp.float32, 'w_up': _jnp.float32, 'ffn_conv_w': _jnp.float32, 'ffn_conv_b': _jnp.float32, 'w_down': _jnp.float32, 'g_final': _jnp.float32}
MOMENT_SCALE = {'g_mix': 7.459961e-02, 'w_in': 3.604692e-02, 'b_gate': 1.740161e-02, 'w_pool': 8.455350e-02, 'pool_scale': 8.752989e-02, 'lru_conv_w': 2.900141e-02, 'lru_conv_b': 3.757558e-01, 'w_a': 8.374683e-03, 'b_a': 8.013369e-03, 'w_i': 1.492753e-02, 'b_i': 1.028078e-02, 'lru_lambda': 1.620979e-02, 'w_pool_proj': 5.992098e-02, 'w_lru_proj': 2.828954e-02, 'w_out': 6.608047e-02, 'g_mlp': 9.310477e-02, 'w_up': 3.699005e-02, 'ffn_conv_w': 3.818488e-02, 'ffn_conv_b': 3.642914e-02, 'w_down': 6.310330e-02, 'g_final': 3.197187e+01}


def _to_microbatches(a, axis):
    t = _jnp.moveaxis(a, axis, 0)
    t = t.reshape((N_MICROBATCH, t.shape[0] // N_MICROBATCH) + t.shape[1:])
    return _jnp.moveaxis(t, 1, axis + 1)


def setup_inputs(seed: int = 0) -> dict:
    inp = _fwd_setup_inputs(seed)
    key = _jax.random.fold_in(_jax.random.key(seed), 7919)
    shape, _ = _output_shape()
    out = dict(inp)
    out["loss_target"] = _jax.random.normal(_jax.random.fold_in(key, 0), shape, _jnp.float32)
    for i, name in enumerate(TWIN_WEIGHTS):
        w = inp[name].astype(_jnp.float32)
        if MOMENT_SCALE is None:
            s = _jnp.sqrt(_jnp.mean(_jnp.square(w)) + 1e-30)
        else:
            s = MOMENT_SCALE[name]
        km, kv = _jax.random.split(_jax.random.fold_in(key, i + 1))
        out[name] = w
        out["m_" + name] = s * _jax.random.normal(km, w.shape, _jnp.float32)
        out["v_" + name] = (s * s) * _jax.random.uniform(kv, w.shape, _jnp.float32, 0.5, 1.5)
    if N_MICROBATCH > 1:
        for name, axis in PER_EXAMPLE_BATCH_AXIS.items():
            out[name] = _to_microbatches(out[name], axis)
    return {'x': out['x'], 'g_mix': out['g_mix'], 'w_in': out['w_in'], 'b_gate': out['b_gate'], 'w_pool': out['w_pool'], 'pool_scale': out['pool_scale'], 'lru_conv_w': out['lru_conv_w'], 'lru_conv_b': out['lru_conv_b'], 'w_a': out['w_a'], 'b_a': out['b_a'], 'w_i': out['w_i'], 'b_i': out['b_i'], 'lru_lambda': out['lru_lambda'], 'w_pool_proj': out['w_pool_proj'], 'w_lru_proj': out['w_lru_proj'], 'w_out': out['w_out'], 'g_mlp': out['g_mlp'], 'w_up': out['w_up'], 'ffn_conv_w': out['ffn_conv_w'], 'ffn_conv_b': out['ffn_conv_b'], 'w_down': out['w_down'], 'g_final': out['g_final'], 'loss_target': out['loss_target'], 'm_g_mix': out['m_g_mix'], 'm_w_in': out['m_w_in'], 'm_b_gate': out['m_b_gate'], 'm_w_pool': out['m_w_pool'], 'm_pool_scale': out['m_pool_scale'], 'm_lru_conv_w': out['m_lru_conv_w'], 'm_lru_conv_b': out['m_lru_conv_b'], 'm_w_a': out['m_w_a'], 'm_b_a': out['m_b_a'], 'm_w_i': out['m_w_i'], 'm_b_i': out['m_b_i'], 'm_lru_lambda': out['m_lru_lambda'], 'm_w_pool_proj': out['m_w_pool_proj'], 'm_w_lru_proj': out['m_w_lru_proj'], 'm_w_out': out['m_w_out'], 'm_g_mlp': out['m_g_mlp'], 'm_w_up': out['m_w_up'], 'm_ffn_conv_w': out['m_ffn_conv_w'], 'm_ffn_conv_b': out['m_ffn_conv_b'], 'm_w_down': out['m_w_down'], 'm_g_final': out['m_g_final'], 'v_g_mix': out['v_g_mix'], 'v_w_in': out['v_w_in'], 'v_b_gate': out['v_b_gate'], 'v_w_pool': out['v_w_pool'], 'v_pool_scale': out['v_pool_scale'], 'v_lru_conv_w': out['v_lru_conv_w'], 'v_lru_conv_b': out['v_lru_conv_b'], 'v_w_a': out['v_w_a'], 'v_b_a': out['v_b_a'], 'v_w_i': out['v_w_i'], 'v_b_i': out['v_b_i'], 'v_lru_lambda': out['v_lru_lambda'], 'v_w_pool_proj': out['v_w_pool_proj'], 'v_w_lru_proj': out['v_w_lru_proj'], 'v_w_out': out['v_w_out'], 'v_g_mlp': out['v_g_mlp'], 'v_w_up': out['v_w_up'], 'v_ffn_conv_w': out['v_ffn_conv_w'], 'v_ffn_conv_b': out['v_ffn_conv_b'], 'v_w_down': out['v_w_down'], 'v_g_final': out['v_g_final']}


def _loss(weights, diff, rest, loss_target):
    with _jax.named_scope("forward"):
        args = {**rest, TWIN_DIFF_INPUT: diff, **{k: w.astype(_WEIGHT_DTYPES[k]) for k, w in weights.items()}}
        y = _forward(args)
    with _jax.named_scope("loss_head"):
        err = _jnp.square(y.astype(_jnp.float32) - loss_target)
        return 0.5 * _jnp.sum(_jnp.mean(err, axis=-1)) if err.ndim else 0.5 * err


def _adamw(w, g, m, v):
    m = ADAM_B1 * m + (1.0 - ADAM_B1) * g
    v = ADAM_B2 * v + (1.0 - ADAM_B2) * _jnp.square(g)
    m_hat = m / (1.0 - ADAM_B1 ** ADAM_STEP)
    v_hat = v / (1.0 - ADAM_B2 ** ADAM_STEP)
    delta = -ADAM_LR * (m_hat / (_jnp.sqrt(v_hat) + ADAM_EPS) + ADAM_WD * w)
    return delta, m, v


def reference(x, g_mix, w_in, b_gate, w_pool, pool_scale, lru_conv_w, lru_conv_b, w_a, b_a, w_i, b_i, lru_lambda, w_pool_proj, w_lru_proj, w_out, g_mlp, w_up, ffn_conv_w, ffn_conv_b, w_down, g_final, loss_target, m_g_mix, m_w_in, m_b_gate, m_w_pool, m_pool_scale, m_lru_conv_w, m_lru_conv_b, m_w_a, m_b_a, m_w_i, m_b_i, m_lru_lambda, m_w_pool_proj, m_w_lru_proj, m_w_out, m_g_mlp, m_w_up, m_ffn_conv_w, m_ffn_conv_b, m_w_down, m_g_final, v_g_mix, v_w_in, v_b_gate, v_w_pool, v_pool_scale, v_lru_conv_w, v_lru_conv_b, v_w_a, v_b_a, v_w_i, v_b_i, v_lru_lambda, v_w_pool_proj, v_w_lru_proj, v_w_out, v_g_mlp, v_w_up, v_ffn_conv_w, v_ffn_conv_b, v_w_down, v_g_final):
    given = dict(x=x, g_mix=g_mix, w_in=w_in, b_gate=b_gate, w_pool=w_pool, pool_scale=pool_scale, lru_conv_w=lru_conv_w, lru_conv_b=lru_conv_b, w_a=w_a, b_a=b_a, w_i=w_i, b_i=b_i, lru_lambda=lru_lambda, w_pool_proj=w_pool_proj, w_lru_proj=w_lru_proj, w_out=w_out, g_mlp=g_mlp, w_up=w_up, ffn_conv_w=ffn_conv_w, ffn_conv_b=ffn_conv_b, w_down=w_down, g_final=g_final, loss_target=loss_target, m_g_mix=m_g_mix, m_w_in=m_w_in, m_b_gate=m_b_gate, m_w_pool=m_w_pool, m_pool_scale=m_pool_scale, m_lru_conv_w=m_lru_conv_w, m_lru_conv_b=m_lru_conv_b, m_w_a=m_w_a, m_b_a=m_b_a, m_w_i=m_w_i, m_b_i=m_b_i, m_lru_lambda=m_lru_lambda, m_w_pool_proj=m_w_pool_proj, m_w_lru_proj=m_w_lru_proj, m_w_out=m_w_out, m_g_mlp=m_g_mlp, m_w_up=m_w_up, m_ffn_conv_w=m_ffn_conv_w, m_ffn_conv_b=m_ffn_conv_b, m_w_down=m_w_down, m_g_final=m_g_final, v_g_mix=v_g_mix, v_w_in=v_w_in, v_b_gate=v_b_gate, v_w_pool=v_w_pool, v_pool_scale=v_pool_scale, v_lru_conv_w=v_lru_conv_w, v_lru_conv_b=v_lru_conv_b, v_w_a=v_w_a, v_b_a=v_b_a, v_w_i=v_w_i, v_b_i=v_b_i, v_lru_lambda=v_lru_lambda, v_w_pool_proj=v_w_pool_proj, v_w_lru_proj=v_w_lru_proj, v_w_out=v_w_out, v_g_mlp=v_g_mlp, v_w_up=v_w_up, v_ffn_conv_w=v_ffn_conv_w, v_ffn_conv_b=v_ffn_conv_b, v_w_down=v_w_down, v_g_final=v_g_final)
    weights = {n: given[n] for n in TWIN_WEIGHTS}
    shared = {n: given[n] for n in SHARED_INPUTS}
    per_example = {n: given[n] for n in ['x']}
    grad_fn = _jax.value_and_grad(_loss, argnums=(0, 1))

    def one_microbatch(ex, loss_target):
        ex = dict(ex)
        diff = ex.pop(TWIN_DIFF_INPUT)
        return grad_fn(weights, diff, {**shared, **ex}, loss_target)

    if N_MICROBATCH == 1:
        loss, (grad_w, grad_x) = one_microbatch(per_example, given["loss_target"])
    else:
        def body(carry, xs):
            loss_sum, grad_sum = carry
            l_k, (gw_k, gx_k) = one_microbatch(xs[0], xs[1])
            with _jax.named_scope("update"):
                return (loss_sum + l_k, _jax.tree.map(_jnp.add, grad_sum, gw_k)), gx_k

        init = (_jnp.zeros((), _jnp.float32), _jax.tree.map(_jnp.zeros_like, weights))
        (loss, grad_w), grad_x = _jax.lax.scan(body, init, (per_example, given["loss_target"]))
    with _jax.named_scope("update"):
        delta_w, new_m, new_v = {}, {}, {}
        for n in TWIN_WEIGHTS:
            delta_w[n], new_m[n], new_v[n] = _adamw(weights[n], grad_w[n], given["m_" + n], given["v_" + n])
    return (loss, grad_x, *[grad_w[n] for n in TWIN_WEIGHTS], *[delta_w[n] for n in TWIN_WEIGHTS],
            *[new_m[n] for n in TWIN_WEIGHTS], *[new_v[n] for n in TWIN_WEIGHTS])
```

```python
import functools

import jax
import jax.numpy as jnp
from jax import lax
from jax.experimental import pallas as pl
from jax.experimental.pallas import tpu as pltpu

f32 = jnp.float32
bf16 = jnp.bfloat16

_EPS = 1e-6
_LRU_C = 8.0
_POOL_WINDOWS = (2, 4, 8, 16)
_POOL_HALO = 16
_CONV_HALO = 8
_GELU_C0 = 0.7978845608028654
_GELU_C1 = 0.044715
_ADAM_LR, _ADAM_B1, _ADAM_B2, _ADAM_EPS, _ADAM_WD, _ADAM_STEP = 0.001, 0.9, 0.999, 1e-08, 0.01, 10
_N_DEV = 8
_LANE = 128
_SMALL_PAD = 8 * _LANE
_VMEM_LIMIT = 60 * 1024 * 1024

_TM = 512
_TM_SMALL = 256
_TC = 256
_TK_T = 512
_TN_MAX = 1536
_CW = 1024
_TK_DOWN = 768


def _cparams(*sem):
    return pltpu.CompilerParams(dimension_semantics=tuple(sem), vmem_limit_bytes=_VMEM_LIMIT)


def _tile(n, want, mult=1):
    if n <= want:
        return n
    t = want - want % mult
    while n % t:
        t -= mult
    return t


def _gelu(x):
    return 0.5 * x * (1.0 + jnp.tanh(_GELU_C0 * (x + _GELU_C1 * (x * x * x))))


def _gelu_grad(x):
    x2 = x * x
    t = jnp.tanh(_GELU_C0 * (x + _GELU_C1 * (x2 * x)))
    return 0.5 * (1.0 + t) + 0.5 * x * (1.0 - t * t) * (_GELU_C0 * (1.0 + 3.0 * _GELU_C1 * x2))


def _sigmoid(x):
    return jax.nn.sigmoid(x)


def _dot(a, b):
    return jnp.dot(a, b, preferred_element_type=f32)


def _dot_nt(a, b):
    return lax.dot_general(a, b, (((1,), (1,)), ((), ())), preferred_element_type=f32)


def _dot_tn(a, b):
    return lax.dot_general(a, b, (((0,), (0,)), ((), ())), preferred_element_type=f32)


def _rms_fwd(xf, g):
    r = lax.rsqrt(jnp.mean(xf * xf, axis=-1, keepdims=True) + _EPS)
    return xf * r * g


def _rms_bwd(xf, g, dy):
    r = lax.rsqrt(jnp.mean(xf * xf, axis=-1, keepdims=True) + _EPS)
    gy = dy * g
    dx = r * gy - xf * ((r * r * r) * jnp.mean(xf * gy, axis=-1, keepdims=True))
    dg = jnp.sum(dy * (xf * r), axis=0, keepdims=True)
    return dx, dg


def _matmul(name, mode, operands, in_specs, out_shape, out_spec, grid, acc_shape):
    dot = {"nn": _dot, "nt": _dot_nt, "tn": _dot_tn}[mode]
    nk = grid[2]

    def body(a_ref, b_ref, o_ref, acc_ref):
        k = pl.program_id(2)

        @pl.when(k == 0)
        def _():
            acc_ref[...] = jnp.zeros_like(acc_ref)

        acc_ref[...] += dot(a_ref[...], b_ref[...])

        @pl.when(k == nk - 1)
        def _():
            o_ref[...] = acc_ref[...].astype(o_ref.dtype)

    return pl.pallas_call(
        body, name=name, grid=grid, in_specs=in_specs, out_specs=out_spec, out_shape=out_shape,
        scratch_shapes=[pltpu.VMEM(acc_shape, f32)],
        compiler_params=_cparams("parallel", "parallel", "arbitrary"),
    )(*operands)


def _mm_nn(name, a, b, out_dtype):
    M, K = a.shape
    N = b.shape[1]
    tm, tn = _tile(M, _TM), _tile(N, _TN_MAX)
    return _matmul(
        name, "nn", (a, b),
        [pl.BlockSpec((tm, K), lambda i, j, k: (i, 0)), pl.BlockSpec((K, tn), lambda i, j, k: (0, j))],
        jax.ShapeDtypeStruct((M, N), out_dtype), pl.BlockSpec((tm, tn), lambda i, j, k: (i, j)),
        (M // tm, N // tn, 1), (tm, tn))


def _mm_nt(name, a, b, out_dtype):
    M, K = a.shape
    N = b.shape[0]
    tm, tn = _tile(M, _TM), _tile(N, _TN_MAX)
    return _matmul(
        name, "nt", (a, b),
        [pl.BlockSpec((tm, K), lambda i, j, k: (i, 0)), pl.BlockSpec((tn, K), lambda i, j, k: (j, 0))],
        jax.ShapeDtypeStruct((M, N), out_dtype), pl.BlockSpec((tm, tn), lambda i, j, k: (i, j)),
        (M // tm, N // tn, 1), (tm, tn))


def _mm_tn(name, a, b, out_dtype, tm_want=2048):
    T, M = a.shape
    N = b.shape[1]
    tm, tn, tk = _tile(M, tm_want), _tile(N, _TN_MAX), _tile(T, _TK_T)
    return _matmul(
        name, "tn", (a, b),
        [pl.BlockSpec((tk, tm), lambda i, j, k: (k, i)), pl.BlockSpec((tk, tn), lambda i, j, k: (k, j))],
        jax.ShapeDtypeStruct((M, N), out_dtype), pl.BlockSpec((tm, tn), lambda i, j, k: (i, j)),
        (M // tm, N // tn, T // tk), (tm, tn))


def _in_proj(x, g_mix, w_in):
    T, D = x.shape
    NI = w_in.shape[1]
    tm, tn = _tile(T, _TM), _tile(NI, _TN_MAX)

    def body(x_ref, g_ref, w_ref, h_ref, o_ref, h_scr):
        @pl.when(pl.program_id(1) == 0)
        def _():
            h = _rms_fwd(x_ref[...], g_ref[...]).astype(bf16)
            h_scr[...] = h
            h_ref[...] = h

        o_ref[...] = _dot(h_scr[...], w_ref[...])

    return pl.pallas_call(
        body, name="in_proj", grid=(T // tm, NI // tn),
        in_specs=[pl.BlockSpec((tm, D), lambda i, j: (i, 0)), pl.BlockSpec((1, D), lambda i, j: (0, 0)),
                  pl.BlockSpec((D, tn), lambda i, j: (0, j))],
        out_specs=[pl.BlockSpec((tm, D), lambda i, j: (i, 0)), pl.BlockSpec((tm, tn), lambda i, j: (i, j))],
        out_shape=[jax.ShapeDtypeStruct((T, D), bf16), jax.ShapeDtypeStruct((T, NI), f32)],
        scratch_shapes=[pltpu.VMEM((tm, D), bf16)],
        compiler_params=_cparams("parallel", "arbitrary"),
    )(x, g_mix, w_in)


def _window_means(ext_ref, row0, tc, gw):
    H = _POOL_HALO
    t_glob = row0 + lax.broadcasted_iota(jnp.int32, (tc, 1), 0)
    out = []
    for g, w in enumerate(_POOL_WINDOWS):
        s = ext_ref[:, g * gw:(g + 1) * gw]
        st = 1
        while st < w:
            s = s + pltpu.roll(s, st, 0)
            st *= 2
        cnt = jnp.minimum(t_glob + 1, w).astype(f32)
        out.append((s[H:, :] / cnt, ext_ref[pl.ds(H, tc), g * gw:(g + 1) * gw]))
    return out


def _pool_fwd(proj, w_pool, pool_scale):
    T = proj.shape[0]
    G, gw, _ = w_pool.shape
    PW = G * gw
    tc = _tile(T, _TC)
    H = _POOL_HALO

    def body(u_ref, w_ref, s_ref, y_ref, ext_ref):
        i = pl.program_id(0)

        @pl.when(i == 0)
        def _():
            ext_ref[pl.ds(0, H), :] = jnp.zeros((H, PW), f32)

        ext_ref[pl.ds(H, tc), :] = u_ref[...]
        for g, (m, u) in enumerate(_window_means(ext_ref, i * tc, tc, gw)):
            d = (m - u).astype(bf16)
            y = _dot(d, w_ref[g]) * s_ref[:, g * gw:(g + 1) * gw]
            y_ref[:, g * gw:(g + 1) * gw] = y.astype(bf16)
        ext_ref[pl.ds(0, H), :] = ext_ref[pl.ds(tc, H), :]

    return pl.pallas_call(
        body, name="pool_fwd", grid=(T // tc,),
        in_specs=[pl.BlockSpec((tc, PW), lambda i: (i, 0)), pl.BlockSpec((G, gw, gw), lambda i: (0, 0, 0)),
                  pl.BlockSpec((1, PW), lambda i: (0, 0))],
        out_specs=pl.BlockSpec((tc, PW), lambda i: (i, 0)),
        out_shape=jax.ShapeDtypeStruct((T, PW), bf16),
        scratch_shapes=[pltpu.VMEM((H + tc, PW), f32)],
        compiler_params=_cparams("arbitrary"),
    )(proj, w_pool, pool_scale)


def _lru_gates(v, wa, wi, ba, bi, sp):
    vb = v.astype(bf16)
    r = _sigmoid(_dot(vb, wa) + ba)
    i = _sigmoid(_dot(vb, wi) + bi)
    a = jnp.exp(-_LRU_C * r * sp)
    mult = jnp.sqrt(1.0 - a * a)
    return r, i, a, mult


def _softplus(z):
    return jnp.maximum(z, 0.0) + jnp.log1p(jnp.exp(-jnp.abs(z)))


def _causal_conv(ext_ref, cw_ref, cb_ref, tc, K, cols=slice(None)):
    H = _CONV_HALO
    v = cb_ref[:, cols] + cw_ref[K - 1:K, cols] * ext_ref[pl.ds(H, tc), cols]
    for k in range(K - 1):
        v = v + cw_ref[k:k + 1, cols] * ext_ref[pl.ds(H - (K - 1 - k), tc), cols]
    return v


def _lru_fwd(proj, D, PW, conv_w, conv_b, w_a, b_a, w_i, b_i, lam):
    T = proj.shape[0]
    NB, bw, _ = w_a.shape
    K = 4
    tc = _tile(T, _TC)
    H = _CONV_HALO
    hb = D // 2
    assert PW == hb and conv_w.shape[0] == 8

    def body(u0_ref, u1_ref, g0_ref, g1_ref, cw_ref, cb_ref, wa_ref, ba_ref, wi_ref, bi_ref, lam_ref,
             y_ref, hs_ref, ext_ref, a_scr, b_scr, hc_scr):
        c = pl.program_id(0)

        @pl.when(c == 0)
        def _():
            ext_ref[pl.ds(0, H), :] = jnp.zeros((H, D), f32)
            hc_scr[...] = jnp.zeros_like(hc_scr)

        ext_ref[pl.ds(H, tc), 0:hb] = u0_ref[...]
        ext_ref[pl.ds(H, tc), hb:D] = u1_ref[...]
        sp = _softplus(-lam_ref[...])
        for b in range(NB):
            cols = slice(b * bw, (b + 1) * bw)
            v = _causal_conv(ext_ref, cw_ref, cb_ref, tc, K, cols)
            _, i, a, mult = _lru_gates(v, wa_ref[b], wi_ref[b], ba_ref[:, cols], bi_ref[:, cols], sp[:, cols])
            a_scr[:, cols] = a
            b_scr[:, cols] = mult * (i * v)

        def step(t, h):
            h = a_scr[pl.ds(t, 1), :] * h + b_scr[pl.ds(t, 1), :]
            hs_ref[pl.ds(t, 1), :] = h
            return h

        h = lax.fori_loop(0, tc, step, hc_scr[0:1, :], unroll=8)
        hc_scr[0:1, :] = h
        y_ref[:, 0:hb] = (hs_ref[:, 0:hb] * _gelu(g0_ref[...])).astype(bf16)
        y_ref[:, hb:D] = (hs_ref[:, hb:D] * _gelu(g1_ref[...])).astype(bf16)
        ext_ref[pl.ds(0, H), :] = ext_ref[pl.ds(tc, H), :]

    vec = pl.BlockSpec((1, D), lambda c: (0, 0))
    wspec = pl.BlockSpec((NB, bw, bw), lambda c: (0, 0, 0))
    return pl.pallas_call(
        body, name="lru_fwd", grid=(T // tc,),
        in_specs=[pl.BlockSpec((tc, hb), lambda c: (c, 1)), pl.BlockSpec((tc, hb), lambda c: (c, 2)),
                  pl.BlockSpec((tc, hb), lambda c: (c, 3)), pl.BlockSpec((tc, hb), lambda c: (c, 4)),
                  pl.BlockSpec((8, D), lambda c: (0, 0)), vec, wspec, vec, wspec, vec, vec],
        out_specs=[pl.BlockSpec((tc, D), lambda c: (c, 0)), pl.BlockSpec((tc, D), lambda c: (c, 0))],
        out_shape=[jax.ShapeDtypeStruct((T, D), bf16), jax.ShapeDtypeStruct((T, D), f32)],
        scratch_shapes=[pltpu.VMEM((H + tc, D), f32), pltpu.VMEM((tc, D), f32), pltpu.VMEM((tc, D), f32),
                        pltpu.VMEM((8, D), f32)],
        compiler_params=_cparams("arbitrary"),
    )(proj, proj, proj, proj, conv_w, conv_b, w_a, b_a, w_i, b_i, lam)


def _merge_fwd(y_pool, y_lru, w_pp, w_lp, proj, b_gate):
    T, PW = y_pool.shape
    D = y_lru.shape[1]
    tm, tn = _tile(T, _TM), _tile(D, PW)
    nj = D // tn
    off = (PW + 2 * D) // tn

    def body(yp_ref, yl_ref, wp_ref, wl_ref, l0_ref, l1_ref, b0_ref, b1_ref, p_ref, q_ref, m_ref):
        p = _dot(yp_ref[...], wp_ref[...])
        q = _dot(yl_ref[...], wl_ref[...])
        p_ref[...] = p
        q_ref[...] = q
        g0 = _sigmoid(l0_ref[...] + b0_ref[...])
        g1 = _sigmoid(l1_ref[...] + b1_ref[...])
        m_ref[...] = (g0 * p + g1 * q).astype(bf16)

    tile = pl.BlockSpec((tm, tn), lambda i, j: (i, j))
    return pl.pallas_call(
        body, name="merge_fwd", grid=(T // tm, nj),
        in_specs=[pl.BlockSpec((tm, PW), lambda i, j: (i, 0)), pl.BlockSpec((tm, D), lambda i, j: (i, 0)),
                  pl.BlockSpec((PW, tn), lambda i, j: (0, j)), pl.BlockSpec((D, tn), lambda i, j: (0, j)),
                  pl.BlockSpec((tm, tn), lambda i, j: (i, off + j)), pl.BlockSpec((tm, tn), lambda i, j: (i, off + nj + j)),
                  pl.BlockSpec((1, tn), lambda i, j: (0, j)), pl.BlockSpec((1, tn), lambda i, j: (0, nj + j))],
        out_specs=[tile, tile, tile],
        out_shape=[jax.ShapeDtypeStruct((T, D), f32), jax.ShapeDtypeStruct((T, D), f32), jax.ShapeDtypeStruct((T, D), bf16)],
        compiler_params=_cparams("parallel", "arbitrary"),
    )(y_pool, y_lru, w_pp, w_lp, proj, proj, b_gate, b_gate)


def _out_proj(merged, w_out, x, g_mlp):
    T, D = x.shape
    tm = _tile(T, _TM_SMALL)

    def body(m_ref, w_ref, x_ref, g_ref, x2_ref, h2_ref):
        x2 = x_ref[...] + _dot(m_ref[...], w_ref[...])
        x2_ref[...] = x2
        h2_ref[...] = _rms_fwd(x2, g_ref[...]).astype(bf16)

    row = pl.BlockSpec((tm, D), lambda i: (i, 0))
    return pl.pallas_call(
        body, name="out_proj", grid=(T // tm,),
        in_specs=[row, pl.BlockSpec((D, D), lambda i: (0, 0)), row, pl.BlockSpec((1, D), lambda i: (0, 0))],
        out_specs=[row, row],
        out_shape=[jax.ShapeDtypeStruct((T, D), f32), jax.ShapeDtypeStruct((T, D), bf16)],
        compiler_params=_cparams("parallel"),
    )(merged, w_out, x, g_mlp)


def _ffn_fwd(up, conv_w, conv_b):
    T, F2 = up.shape
    F = F2 // 2
    K = 3
    tc = _tile(T, 2 * _TC)
    cw = _tile(F, _CW)
    ns = F // cw
    H = _CONV_HALO

    def body(gp_ref, val_ref, cw_ref, cb_ref, z_ref, ext_ref):
        @pl.when(pl.program_id(1) == 0)
        def _():
            ext_ref[pl.ds(0, H), :] = jnp.zeros((H, cw), f32)

        ext_ref[pl.ds(H, tc), :] = gp_ref[...]
        c = _causal_conv(ext_ref, cw_ref, cb_ref, tc, K)
        z_ref[...] = (_gelu(c) * val_ref[...]).astype(bf16)
        ext_ref[pl.ds(0, H), :] = ext_ref[pl.ds(tc, H), :]

    return pl.pallas_call(
        body, name="ffn_fwd", grid=(ns, T // tc),
        in_specs=[pl.BlockSpec((tc, cw), lambda s, c: (c, s)), pl.BlockSpec((tc, cw), lambda s, c: (c, ns + s)),
                  pl.BlockSpec((8, cw), lambda s, c: (0, s)), pl.BlockSpec((1, cw), lambda s, c: (0, s))],
        out_specs=pl.BlockSpec((tc, cw), lambda s, c: (c, s)),
        out_shape=jax.ShapeDtypeStruct((T, F), bf16),
        scratch_shapes=[pltpu.VMEM((H + tc, cw), f32)],
        compiler_params=_cparams("parallel", "arbitrary"),
    )(up, up, conv_w, conv_b)


def _down_loss(z, w_down, x2, target, g_final):
    T, F = z.shape
    D = x2.shape[1]
    tm, tk = _tile(T, _TM_SMALL), _tile(F, _TK_DOWN)
    nk = F // tk

    def body(z_ref, w_ref, x2_ref, t_ref, g_ref, dx_ref, dxb_ref, loss_ref, dg_ref, acc_ref):
        i, k = pl.program_id(0), pl.program_id(1)

        @pl.when(k == 0)
        def _():
            acc_ref[...] = x2_ref[...]

        @pl.when((i == 0) & (k == 0))
        def _():
            loss_ref[...] = jnp.zeros_like(loss_ref)
            dg_ref[...] = jnp.zeros_like(dg_ref)

        acc_ref[...] += _dot(z_ref[...], w_ref[...])

        @pl.when(k == nk - 1)
        def _():
            x3 = acc_ref[...]
            g = g_ref[...]
            e = _rms_fwd(x3, g) - t_ref[...]
            loss_ref[...] += 0.5 * jnp.sum(jnp.mean(e * e, axis=-1, keepdims=True))
            dx, dg = _rms_bwd(x3, g, e * (1.0 / D))
            dg_ref[...] += dg
            dx_ref[...] = dx
            dxb_ref[...] = dx.astype(bf16)

    row = pl.BlockSpec((tm, D), lambda i, k: (i, 0))
    vec = pl.BlockSpec((1, D), lambda i, k: (0, 0))
    return pl.pallas_call(
        body, name="down_loss", grid=(T // tm, nk),
        in_specs=[pl.BlockSpec((tm, tk), lambda i, k: (i, k)), pl.BlockSpec((tk, D), lambda i, k: (k, 0)), row, row, vec],
        out_specs=[row, row, pl.BlockSpec((8, _LANE), lambda i, k: (0, 0)), vec],
        out_shape=[jax.ShapeDtypeStruct((T, D), f32), jax.ShapeDtypeStruct((T, D), bf16),
                   jax.ShapeDtypeStruct((8, _LANE), f32), jax.ShapeDtypeStruct((1, D), f32)],
        scratch_shapes=[pltpu.VMEM((tm, D), f32)],
        compiler_params=_cparams("arbitrary", "arbitrary"),
    )(z, w_down, x2, target, g_final)


def _ffn_bwd(dz, up, conv_w, conv_b):
    T, F = dz.shape
    K = 3
    tc = _tile(T, 2 * _TC)
    cw = _tile(F, _CW)
    ns, nt = F // cw, T // tc
    H = _CONV_HALO

    def body(dz_ref, gp_ref, val_ref, gph_ref, cw_ref, cb_ref, dup_ref, dcw_ref, dcb_ref, ext_ref, dext_ref):
        j = pl.program_id(1)
        first = j == nt - 1

        @pl.when(j == 0)
        def _():
            dext_ref[pl.ds(tc, H), :] = jnp.zeros((H, cw), f32)
            dcw_ref[...] = jnp.zeros_like(dcw_ref)
            dcb_ref[...] = jnp.zeros_like(dcb_ref)

        ext_ref[pl.ds(0, H), :] = jnp.where(first, 0.0, gph_ref[...])
        gp = gp_ref[...]
        ext_ref[pl.ds(H, tc), :] = gp
        c = _causal_conv(ext_ref, cw_ref, cb_ref, tc, K)
        dzv = dz_ref[...]
        dup_ref[1] = (dzv * _gelu(c)).astype(bf16)
        dc = dzv * val_ref[...] * _gelu_grad(c)
        dext_ref[pl.ds(0, tc), :] = dc
        dcb_ref[...] += jnp.sum(dc, axis=0, keepdims=True)
        dgp = cw_ref[K - 1:K, :] * dc
        dcw_ref[K - 1:K, :] += jnp.sum(gp * dc, axis=0, keepdims=True)
        for k in range(K - 1):
            sh = dext_ref[pl.ds(K - 1 - k, tc), :]
            dgp = dgp + cw_ref[k:k + 1, :] * sh
            dcw_ref[k:k + 1, :] += jnp.sum(gp * sh, axis=0, keepdims=True)
        dup_ref[0] = dgp.astype(bf16)
        dext_ref[pl.ds(tc, H), :] = dext_ref[pl.ds(0, H), :]

    hblk = tc // H
    return pl.pallas_call(
        body, name="ffn_bwd", grid=(ns, nt),
        in_specs=[pl.BlockSpec((tc, cw), lambda s, j: (nt - 1 - j, s)),
                  pl.BlockSpec((tc, cw), lambda s, j: (nt - 1 - j, s)),
                  pl.BlockSpec((tc, cw), lambda s, j: (nt - 1 - j, ns + s)),
                  pl.BlockSpec((H, cw), lambda s, j: (jnp.maximum((nt - 1 - j) * hblk - 1, 0), s)),
                  pl.BlockSpec((8, cw), lambda s, j: (0, s)), pl.BlockSpec((1, cw), lambda s, j: (0, s))],
        out_specs=[pl.BlockSpec((2, tc, cw), lambda s, j: (0, nt - 1 - j, s)),
                   pl.BlockSpec((8, cw), lambda s, j: (0, s)), pl.BlockSpec((1, cw), lambda s, j: (0, s))],
        out_shape=[jax.ShapeDtypeStruct((2, T, F), bf16), jax.ShapeDtypeStruct((8, F), f32), jax.ShapeDtypeStruct((1, F), f32)],
        scratch_shapes=[pltpu.VMEM((H + tc, cw), f32), pltpu.VMEM((tc + H, cw), f32)],
        compiler_params=_cparams("parallel", "arbitrary"),
    )(dz, up, up, up, conv_w, conv_b)


def _norm_bwd_matmul(name, a, a_spec, nk, w, w_spec, x, g, dres):
    T, D = x.shape
    tm = a_spec.block_shape[-2]

    def body(a_ref, w_ref, x_ref, g_ref, dr_ref, dx_ref, dxb_ref, dg_ref, acc_ref):
        i, k = pl.program_id(0), pl.program_id(1)

        @pl.when(k == 0)
        def _():
            acc_ref[...] = jnp.zeros_like(acc_ref)

        @pl.when((i == 0) & (k == 0))
        def _():
            dg_ref[...] = jnp.zeros_like(dg_ref)

        acc_ref[...] += _dot_nt(a_ref[...], w_ref[...])

        @pl.when(k == nk - 1)
        def _():
            dxn, dg = _rms_bwd(x_ref[...], g_ref[...], acc_ref[...])
            dg_ref[...] += dg
            dx = dr_ref[...] + dxn
            dx_ref[...] = dx
            dxb_ref[...] = dx.astype(bf16)

    row = pl.BlockSpec((tm, D), lambda i, k: (i, 0))
    vec = pl.BlockSpec((1, D), lambda i, k: (0, 0))
    return pl.pallas_call(
        body, name=name, grid=(T // tm, nk),
        in_specs=[a_spec, w_spec, row, vec, row],
        out_specs=[row, row, vec],
        out_shape=[jax.ShapeDtypeStruct((T, D), f32), jax.ShapeDtypeStruct((T, D), bf16), jax.ShapeDtypeStruct((1, D), f32)],
        scratch_shapes=[pltpu.VMEM((tm, D), f32)],
        compiler_params=_cparams("arbitrary", "arbitrary"),
    )(a, w, x, g, dres)


def _merge_bwd(dx2b, w_out, p, q, proj, b_gate, PW):
    T, D = p.shape
    tm, tn = _tile(T, _TM), _tile(D, PW)
    nj = D // tn
    off = (PW + 2 * D) // tn

    def body(dx_ref, w_ref, p_ref, q_ref, l0_ref, l1_ref, b0_ref, b1_ref, dp_ref, dq_ref, dl0_ref, dl1_ref, db0_ref, db1_ref):
        @pl.when(pl.program_id(1) == 0)
        def _():
            db0_ref[...] = jnp.zeros_like(db0_ref)
            db1_ref[...] = jnp.zeros_like(db1_ref)

        dm = _dot_nt(dx_ref[...], w_ref[...])
        g0 = _sigmoid(l0_ref[...] + b0_ref[...])
        g1 = _sigmoid(l1_ref[...] + b1_ref[...])
        dp_ref[...] = (g0 * dm).astype(bf16)
        dq_ref[...] = (g1 * dm).astype(bf16)
        dl0 = dm * p_ref[...] * (g0 * (1.0 - g0))
        dl1 = dm * q_ref[...] * (g1 * (1.0 - g1))
        dl0_ref[...] = dl0.astype(bf16)
        dl1_ref[...] = dl1.astype(bf16)
        db0_ref[...] += jnp.sum(dl0, axis=0, keepdims=True)
        db1_ref[...] += jnp.sum(dl1, axis=0, keepdims=True)

    tile = pl.BlockSpec((tm, tn), lambda j, i: (i, j))
    vecj = pl.BlockSpec((1, tn), lambda j, i: (0, j))
    tb = jax.ShapeDtypeStruct((T, D), bf16)
    vb = jax.ShapeDtypeStruct((1, D), f32)
    return pl.pallas_call(
        body, name="merge_bwd", grid=(nj, T // tm),
        in_specs=[pl.BlockSpec((tm, D), lambda j, i: (i, 0)), pl.BlockSpec((tn, D), lambda j, i: (j, 0)), tile, tile,
                  pl.BlockSpec((tm, tn), lambda j, i: (i, off + j)), pl.BlockSpec((tm, tn), lambda j, i: (i, off + nj + j)),
                  vecj, pl.BlockSpec((1, tn), lambda j, i: (0, nj + j))],
        out_specs=[tile, tile, tile, tile, vecj, vecj],
        out_shape=[tb, tb, tb, tb, vb, vb],
        compiler_params=_cparams("parallel", "arbitrary"),
    )(dx2b, w_out, p, q, proj, proj, b_gate, b_gate)


def _lru_bwd(dy, proj, hs, D, conv_w, conv_b, w_a, b_a, w_i, b_i, lam):
    T = dy.shape[0]
    NB, bw, _ = w_a.shape
    K = 4
    tc = _tile(T, _TC)
    nt = T // tc
    H = _CONV_HALO
    hb = D // 2

    def body(dy_ref, u0_ref, u1_ref, g0_ref, g1_ref, hs_ref, uh0_ref, uh1_ref, hh_ref,
             cw_ref, cb_ref, wa_ref, ba_ref, wi_ref, bi_ref, lam_ref,
             du_ref, dg_ref, dwa_ref, dwi_ref, dcw_ref, dvec_ref,
             ext_ref, hext_ref, dext_ref, a_scr, r_scr, i_scr, v_scr, g_scr, car_scr):
        j = pl.program_id(0)
        first = j == nt - 1

        @pl.when(j == 0)
        def _():
            dext_ref[pl.ds(tc, H), :] = jnp.zeros((H, D), f32)
            car_scr[...] = jnp.zeros_like(car_scr)
            dwa_ref[...] = jnp.zeros_like(dwa_ref)
            dwi_ref[...] = jnp.zeros_like(dwi_ref)
            dcw_ref[...] = jnp.zeros_like(dcw_ref)
            dvec_ref[...] = jnp.zeros_like(dvec_ref)

        ext_ref[pl.ds(0, H), 0:hb] = jnp.where(first, 0.0, uh0_ref[...])
        ext_ref[pl.ds(0, H), hb:D] = jnp.where(first, 0.0, uh1_ref[...])
        ext_ref[pl.ds(H, tc), 0:hb] = u0_ref[...]
        ext_ref[pl.ds(H, tc), hb:D] = u1_ref[...]
        hext_ref[pl.ds(0, H), :] = jnp.where(first, 0.0, hh_ref[...])
        hext_ref[pl.ds(H, tc), :] = hs_ref[...]
        lamv = lam_ref[...]
        sp = _softplus(-lamv)

        for b in range(NB):
            cols = slice(b * bw, (b + 1) * bw)
            v = _causal_conv(ext_ref, cw_ref, cb_ref, tc, K, cols)
            r, i, a, _ = _lru_gates(v, wa_ref[b], wi_ref[b], ba_ref[:, cols], bi_ref[:, cols], sp[:, cols])
            v_scr[:, cols] = v
            r_scr[:, cols] = r
            i_scr[:, cols] = i
            a_scr[:, cols] = a
        for half, g_ref in enumerate((g0_ref, g1_ref)):
            cols = slice(half * hb, (half + 1) * hb)
            ug = g_ref[...]
            dyv = dy_ref[:, cols]
            g_scr[:, cols] = dyv * _gelu(ug)
            dg_ref[:, cols] = (dyv * hs_ref[:, cols] * _gelu_grad(ug)).astype(bf16)

        def step(s, carry):
            t = tc - 1 - s
            g = g_scr[pl.ds(t, 1), :] + carry
            g_scr[pl.ds(t, 1), :] = g
            return a_scr[pl.ds(t, 1), :] * g

        car_scr[0:1, :] = lax.fori_loop(0, tc, step, car_scr[0:1, :], unroll=8)

        for b in range(NB):
            cols = slice(b * bw, (b + 1) * bw)
            g = g_scr[:, cols]
            v, r, i, a = v_scr[:, cols], r_scr[:, cols], i_scr[:, cols], a_scr[:, cols]
            mult = jnp.sqrt(1.0 - a * a)
            h_prev = hext_ref[pl.ds(H - 1, tc), cols]
            dmult = g * (i * v)
            di = g * mult * v
            dv = g * mult * i
            da = g * h_prev - dmult * (a / mult)
            dlog_a = da * a
            spb = sp[:, cols]
            dr = dlog_a * (-_LRU_C * spb)
            dsp = jnp.sum(dlog_a * (-_LRU_C * r), axis=0, keepdims=True)
            dpr = dr * r * (1.0 - r)
            dpi = di * i * (1.0 - i)
            dprb, dpib, vb = dpr.astype(bf16), dpi.astype(bf16), v.astype(bf16)
            dv = dv + _dot_nt(dprb, wa_ref[b]) + _dot_nt(dpib, wi_ref[b])
            dwa_ref[b] += _dot_tn(vb, dprb)
            dwi_ref[b] += _dot_tn(vb, dpib)
            dext_ref[pl.ds(0, tc), cols] = dv
            dvec_ref[0:1, cols] += jnp.sum(dv, axis=0, keepdims=True)
            dvec_ref[1:2, cols] += jnp.sum(dpr, axis=0, keepdims=True)
            dvec_ref[2:3, cols] += jnp.sum(dpi, axis=0, keepdims=True)
            dvec_ref[3:4, cols] += dsp * (-_sigmoid(-lamv[:, cols]))

        for b in range(NB):
            cols = slice(b * bw, (b + 1) * bw)
            u = ext_ref[pl.ds(H, tc), cols]
            du = jnp.zeros((tc, bw), f32)
            for k in range(K):
                sh = dext_ref[pl.ds(K - 1 - k, tc), cols]
                du = du + cw_ref[k:k + 1, cols] * sh
                dcw_ref[k:k + 1, cols] += jnp.sum(u * sh, axis=0, keepdims=True)
            du_ref[:, cols] = du.astype(bf16)
        dext_ref[pl.ds(tc, H), :] = dext_ref[pl.ds(0, H), :]

    hblk = tc // H
    rev = lambda j: nt - 1 - j
    halo = lambda j: jnp.maximum((nt - 1 - j) * hblk - 1, 0)
    vec = pl.BlockSpec((1, D), lambda j: (0, 0))
    wspec = pl.BlockSpec((NB, bw, bw), lambda j: (0, 0, 0))
    acc8 = pl.BlockSpec((8, D), lambda j: (0, 0))
    big = pltpu.VMEM((tc, D), f32)
    return pl.pallas_call(
        body, name="lru_bwd", grid=(nt,),
        in_specs=[pl.BlockSpec((tc, D), lambda j: (rev(j), 0)),
                  pl.BlockSpec((tc, hb), lambda j: (rev(j), 1)), pl.BlockSpec((tc, hb), lambda j: (rev(j), 2)),
                  pl.BlockSpec((tc, hb), lambda j: (rev(j), 3)), pl.BlockSpec((tc, hb), lambda j: (rev(j), 4)),
                  pl.BlockSpec((tc, D), lambda j: (rev(j), 0)),
                  pl.BlockSpec((H, hb), lambda j: (halo(j), 1)), pl.BlockSpec((H, hb), lambda j: (halo(j), 2)),
                  pl.BlockSpec((H, D), lambda j: (halo(j), 0)),
                  acc8, vec, wspec, vec, wspec, vec, vec],
        out_specs=[pl.BlockSpec((tc, D), lambda j: (rev(j), 0)), pl.BlockSpec((tc, D), lambda j: (rev(j), 0)),
                   wspec, wspec, acc8, acc8],
        out_shape=[jax.ShapeDtypeStruct((T, D), bf16), jax.ShapeDtypeStruct((T, D), bf16),
                   jax.ShapeDtypeStruct((NB, bw, bw), f32), jax.ShapeDtypeStruct((NB, bw, bw), f32),
                   jax.ShapeDtypeStruct((8, D), f32), jax.ShapeDtypeStruct((8, D), f32)],
        scratch_shapes=[pltpu.VMEM((H + tc, D), f32), pltpu.VMEM((H + tc, D), f32), pltpu.VMEM((tc + H, D), f32),
                        big, big, big, big, big, pltpu.VMEM((8, D), f32)],
        compiler_params=_cparams("arbitrary"),
    )(dy, proj, proj, proj, proj, hs, proj, proj, hs, conv_w, conv_b, w_a, b_a, w_i, b_i, lam)


def _pool_bwd(dy, proj, w_pool, pool_scale):
    T, PW = dy.shape
    G, gw, _ = w_pool.shape
    tc = _tile(T, _TC)
    nt = T // tc
    H = _POOL_HALO

    def body(dy_ref, u_ref, uh_ref, w_ref, s_ref, du_ref, dw_ref, ds_ref, ext_ref, eext_ref):
        j = pl.program_id(0)
        first = j == nt - 1
        row0 = (nt - 1 - j) * tc

        @pl.when(j == 0)
        def _():
            eext_ref[pl.ds(tc, H), :] = jnp.zeros((H, PW), f32)
            dw_ref[...] = jnp.zeros_like(dw_ref)
            ds_ref[...] = jnp.zeros_like(ds_ref)

        ext_ref[pl.ds(0, H), :] = jnp.where(first, 0.0, uh_ref[...])
        ext_ref[pl.ds(H, tc), :] = u_ref[...]
        t_glob = row0 + lax.broadcasted_iota(jnp.int32, (tc, 1), 0)
        dds = []
        for g, (m, u) in enumerate(_window_means(ext_ref, row0, tc, gw)):
            cols = slice(g * gw, (g + 1) * gw)
            d = (m - u).astype(bf16)
            yraw = _dot(d, w_ref[g])
            dyv = dy_ref[:, cols]
            ds_ref[:, cols] += jnp.sum(dyv * yraw, axis=0, keepdims=True)
            dyr = (dyv * s_ref[:, cols]).astype(bf16)
            dd = _dot_nt(dyr, w_ref[g])
            dw_ref[g] += _dot_tn(d, dyr)
            cnt = jnp.minimum(t_glob + 1, _POOL_WINDOWS[g]).astype(f32)
            eext_ref[pl.ds(0, tc), cols] = dd / cnt
            dds.append(dd)
        n = tc + H
        for g, w in enumerate(_POOL_WINDOWS):
            cols = slice(g * gw, (g + 1) * gw)
            s = eext_ref[:, cols]
            st = 1
            while st < w:
                s = s + pltpu.roll(s, n - st, 0)
                st *= 2
            du_ref[:, cols] = (s[0:tc, :] - dds[g]).astype(bf16)
        eext_ref[pl.ds(tc, H), :] = eext_ref[pl.ds(0, H), :]

    hblk = tc // H
    return pl.pallas_call(
        body, name="pool_bwd", grid=(nt,),
        in_specs=[pl.BlockSpec((tc, PW), lambda j: (nt - 1 - j, 0)), pl.BlockSpec((tc, PW), lambda j: (nt - 1 - j, 0)),
                  pl.BlockSpec((H, PW), lambda j: (jnp.maximum((nt - 1 - j) * hblk - 1, 0), 0)),
                  pl.BlockSpec((G, gw, gw), lambda j: (0, 0, 0)), pl.BlockSpec((1, PW), lambda j: (0, 0))],
        out_specs=[pl.BlockSpec((tc, PW), lambda j: (nt - 1 - j, 0)), pl.BlockSpec((G, gw, gw), lambda j: (0, 0, 0)),
                   pl.BlockSpec((1, PW), lambda j: (0, 0))],
        out_shape=[jax.ShapeDtypeStruct((T, PW), bf16), jax.ShapeDtypeStruct((G, gw, gw), f32), jax.ShapeDtypeStruct((1, PW), f32)],
        scratch_shapes=[pltpu.VMEM((H + tc, PW), f32), pltpu.VMEM((tc + H, PW), f32)],
        compiler_params=_cparams("arbitrary"),
    )(dy, proj, proj, w_pool, pool_scale)


_MESH = pl.DeviceIdType.MESH
_HBM = pl.BlockSpec(memory_space=pltpu.HBM)


def _slab(ref, kind, blk, n):
    start = blk * n
    if n % _LANE == 0:
        start = pl.multiple_of(start, _LANE)
    if kind == "col":
        return ref.at[:, pl.ds(start, n)]
    if kind == "row":
        return ref.at[pl.ds(start, n), :]
    if kind == "mid":
        return ref.at[:, pl.ds(start, n), :]
    raise ValueError(kind)


def _my_place():
    x, y, c = lax.axis_index("x"), lax.axis_index("y"), lax.axis_index("c")
    return x, y, c


def _blk(px, py, pc):
    return 4 * px + 2 * py + pc


def _all_gather(shards, kinds, full_shapes):
    nt = len(shards)
    sizes = [s.shape[{"col": 1, "row": 0, "mid": 1}[k]] for s, k in zip(shards, kinds)]

    def body(*refs):
        ins, outs = refs[:nt], refs[nt:2 * nt]
        send_sems, recv_sems, local_sems = refs[2 * nt:]
        x, y, c = _my_place()
        me, sibling = (x, y, c), (x, y, 1 - c)
        chips = [(1 - x, y), (x, 1 - y), (1 - x, 1 - y)]

        def part(t, place):
            return _slab(outs[t], kinds[t], _blk(*place), sizes[t])

        def copy(t, k, block, to, src=None):
            return pltpu.make_async_remote_copy(
                src_ref=part(t, block) if src is None else src, dst_ref=part(t, block),
                send_sem=send_sems.at[7 * t + k], recv_sem=recv_sems.at[7 * t + k],
                device_id=to, device_id_type=_MESH)

        mine = [pltpu.make_async_copy(ins[t], part(t, me), local_sems.at[t]) for t in range(nt)]
        for cp in mine:
            cp.start()
        first = []
        for t in range(nt):
            first.append(copy(t, 0, me, sibling, src=ins[t]))
            first += [copy(t, 1 + j, me, (*chip, c), src=ins[t]) for j, chip in enumerate(chips)]
        for cp in first:
            cp.start()
        passed = []
        for t in range(nt):
            for j, chip in enumerate(chips):
                copy(t, 1 + j, (*chip, c), me).wait_recv()
                cp = copy(t, 4 + j, (*chip, c), sibling)
                cp.start()
                passed.append(cp)
        for t in range(nt):
            copy(t, 0, sibling, me).wait_recv()
            for j, chip in enumerate(chips):
                copy(t, 4 + j, (*chip, 1 - c), me).wait_recv()
        for cp in first + passed:
            cp.wait_send()
        for cp in mine:
            cp.wait()

    return pl.pallas_call(
        body, name="all_gather_weights",
        in_specs=[_HBM] * nt, out_specs=[_HBM] * nt,
        out_shape=[jax.ShapeDtypeStruct(fs, s.dtype) for fs, s in zip(full_shapes, shards)],
        scratch_shapes=[pltpu.SemaphoreType.DMA((7 * nt,)), pltpu.SemaphoreType.DMA((7 * nt,)), pltpu.SemaphoreType.DMA((nt,))],
        compiler_params=pltpu.CompilerParams(has_side_effects=True),
    )(*shards)


def _exchange(fulls, kinds, sizes, whole):
    arrays = list(fulls) + list(whole)
    nt, nf = len(arrays), len(fulls)

    def shard_shape(t):
        s = list(arrays[t].shape)
        if t < nf:
            s[{"col": 1, "row": 0, "mid": 1}[kinds[t]]] = sizes[t]
        return tuple(s)

    def body(*refs):
        ins, outs = refs[:nt], refs[nt:2 * nt]
        send_sems, recv_sems, local_sems = refs[2 * nt:]
        x, y, c = _my_place()
        me = _blk(x, y, c)

        def src(t, blk):
            return _slab(ins[t], kinds[t], blk, sizes[t]) if t < nf else ins[t]

        mine = [pltpu.make_async_copy(src(t, me), outs[t].at[me], local_sems.at[t]) for t in range(nt)]
        for cp in mine:
            cp.start()
        sent = []
        for t in range(nt):
            for k in range(1, 8):
                peer = (x ^ (k >> 2), y ^ ((k >> 1) & 1), c ^ (k & 1))
                pb = _blk(*peer)
                cp = pltpu.make_async_remote_copy(
                    src_ref=src(t, pb), dst_ref=outs[t].at[me],
                    send_sem=send_sems.at[7 * t + k - 1], recv_sem=recv_sems.at[7 * t + k - 1],
                    device_id=peer, device_id_type=_MESH)
                cp.start()
                sent.append((cp, t, k, pb))
        for cp, t, k, pb in sent:
            pltpu.make_async_remote_copy(
                src_ref=src(t, pb), dst_ref=outs[t].at[pb],
                send_sem=send_sems.at[7 * t + k - 1], recv_sem=recv_sems.at[7 * t + k - 1],
                device_id=(x, y, c), device_id_type=_MESH).wait_recv()
        for cp, _, _, _ in sent:
            cp.wait_send()
        for cp in mine:
            cp.wait()

    return pl.pallas_call(
        body, name="exchange_grads",
        in_specs=[_HBM] * nt, out_specs=[_HBM] * nt,
        out_shape=[jax.ShapeDtypeStruct((_N_DEV,) + shard_shape(t), arrays[t].dtype) for t in range(nt)],
        scratch_shapes=[pltpu.SemaphoreType.DMA((7 * nt,)), pltpu.SemaphoreType.DMA((7 * nt,)), pltpu.SemaphoreType.DMA((nt,))],
        compiler_params=pltpu.CompilerParams(has_side_effects=True),
    )(*arrays)


def _sum_parts(name, parts):
    n, R, C = parts.shape
    tr = _tile(R, 256, 8)

    def body(p_ref, o_ref):
        g = p_ref[0].astype(f32)
        for s in range(1, n):
            g = g + p_ref[s].astype(f32)
        o_ref[...] = g

    return pl.pallas_call(
        body, name=name, grid=(R // tr,),
        in_specs=[pl.BlockSpec((n, tr, C), lambda i: (0, i, 0))],
        out_specs=pl.BlockSpec((tr, C), lambda i: (i, 0)),
        out_shape=jax.ShapeDtypeStruct((R, C), f32),
        compiler_params=_cparams("parallel"),
    )(parts)


def _adamw(name, w, m, v, parts):
    R, C = w.shape
    n = parts.shape[0]
    tr = _tile(R, 256, 8)
    c1 = 1.0 - _ADAM_B1 ** _ADAM_STEP
    c2 = 1.0 - _ADAM_B2 ** _ADAM_STEP

    def body(w_ref, m_ref, v_ref, p_ref, g_ref, d_ref, nm_ref, nv_ref):
        g = p_ref[0].astype(f32)
        for s in range(1, n):
            g = g + p_ref[s].astype(f32)
        nm = _ADAM_B1 * m_ref[...] + (1.0 - _ADAM_B1) * g
        nv = _ADAM_B2 * v_ref[...] + (1.0 - _ADAM_B2) * (g * g)
        g_ref[...] = g
        nm_ref[...] = nm
        nv_ref[...] = nv
        d_ref[...] = -_ADAM_LR * ((nm / c1) / (jnp.sqrt(nv / c2) + _ADAM_EPS) + _ADAM_WD * w_ref[...])

    blk = pl.BlockSpec((tr, C), lambda i: (i, 0))
    sd = jax.ShapeDtypeStruct((R, C), f32)
    return pl.pallas_call(
        body, name=name, grid=(R // tr,),
        in_specs=[blk, blk, blk, pl.BlockSpec((n, tr, C), lambda i: (0, i, 0))],
        out_specs=[blk, blk, blk, blk], out_shape=[sd, sd, sd, sd],
        compiler_params=_cparams("parallel"),
    )(w, m, v, parts)


def _pack(vectors):
    flat = jnp.concatenate([a.reshape(-1).astype(f32) for a in vectors])
    pad = (-flat.shape[0]) % _SMALL_PAD
    return jnp.pad(flat, (0, pad)).reshape(-1, _LANE)


def _unpack(packed, shapes):
    flat = packed.reshape(-1)
    out, o = [], 0
    for s in shapes:
        n = 1
        for d in s:
            n *= d
        out.append(flat[o:o + n].reshape(s))
        o += n
    return out


def _pad_rows8(a):
    return jnp.pad(a, ((0, 8 - a.shape[0]), (0, 0)))


def _local_step(x, target, p):
    T, D = x.shape
    PW = p["w_pool"].shape[0] * p["w_pool"].shape[1]
    F = p["w_down"].shape[0]

    h1, proj = _in_proj(x, p["g_mix"], p["w_in"])
    y_pool = _pool_fwd(proj, p["w_pool"], p["pool_scale"])
    y_lru, hs = _lru_fwd(proj, D, PW, p["lru_conv_w"], p["lru_conv_b"], p["w_a"], p["b_a"], p["w_i"], p["b_i"], p["lru_lambda"])
    pp, qq, merged = _merge_fwd(y_pool, y_lru, p["w_pool_proj"], p["w_lru_proj"], proj, p["b_gate"])
    x2, h2 = _out_proj(merged, p["w_out"], x, p["g_mlp"])
    up = _mm_nn("up_proj", h2, p["w_up"], f32)
    z = _ffn_fwd(up, p["ffn_conv_w"], p["ffn_conv_b"])
    dx3, dx3b, loss_t, dg_final = _down_loss(z, p["w_down"], x2, target, p["g_final"])

    gm = {}
    gs = {"g_final": dg_final}
    gm["w_down"] = _mm_tn("dw_down", z, dx3b, bf16, tm_want=1536)
    dz = _mm_nt("dz", dx3b, p["w_down"], f32)
    dup, dcw_ffn, dcb_ffn = _ffn_bwd(dz, up, p["ffn_conv_w"], p["ffn_conv_b"])
    gs["ffn_conv_w"] = dcw_ffn[0:3]
    gs["ffn_conv_b"] = dcb_ffn

    tm = _tile(T, _TM)
    tk = _tile(F, _TN_MAX)
    nkh = F // tk
    dx2, dx2b, gs["g_mlp"] = _norm_bwd_matmul(
        "dh2", dup, pl.BlockSpec((None, tm, tk), lambda i, k: (k // nkh, i, k % nkh)), 2 * nkh,
        p["w_up"], pl.BlockSpec((D, tk), lambda i, k: (0, k)), x2, p["g_mlp"], dx3)

    tkt = _tile(T, _TK_T)
    gm["w_up"] = _matmul(
        "dw_up", "tn", (h2, dup),
        [pl.BlockSpec((tkt, D), lambda i, j, k: (k, 0)), pl.BlockSpec((None, tkt, tk), lambda i, j, k: (j // nkh, k, j % nkh))],
        jax.ShapeDtypeStruct((D, 2 * F), bf16), pl.BlockSpec((D, tk), lambda i, j, k: (0, j)),
        (1, 2 * nkh, T // tkt), (D, tk))

    gm["w_out"] = _mm_tn("dw_out", merged, dx2b, bf16)
    dP, dQ, dl0, dl1, db0, db1 = _merge_bwd(dx2b, p["w_out"], pp, qq, proj, p["b_gate"], PW)
    gs["b_gate"] = jnp.concatenate([db0, db1], axis=1)
    gm["w_pool_proj"] = _mm_tn("dw_pool_proj", y_pool, dP, bf16)
    gm["w_lru_proj"] = _mm_tn("dw_lru_proj", y_lru, dQ, bf16)
    dy_pool = _mm_nt("dy_pool", dP, p["w_pool_proj"], f32)
    dy_lru = _mm_nt("dy_lru", dQ, p["w_lru_proj"], f32)

    du_lru, du_gelu, dwa, dwi, dcw_lru, dvec = _lru_bwd(
        dy_lru, proj, hs, D, p["lru_conv_w"], p["lru_conv_b"], p["w_a"], p["b_a"], p["w_i"], p["b_i"], p["lru_lambda"])
    gm["w_a"], gm["w_i"] = dwa.astype(bf16), dwi.astype(bf16)
    gs["lru_conv_w"] = dcw_lru[0:4]
    gs["lru_conv_b"], gs["b_a"], gs["b_i"], gs["lru_lambda"] = dvec[0:1], dvec[1:2], dvec[2:3], dvec[3:4]
    du_pool, dwp, gs["pool_scale"] = _pool_bwd(dy_pool, proj, p["w_pool"], p["pool_scale"])
    gm["w_pool"] = dwp.astype(bf16)

    dproj = jnp.concatenate([du_pool, du_lru, du_gelu, dl0, dl1], axis=1)
    NI = dproj.shape[1]
    tki = _tile(NI, _TN_MAX)
    grad_x, _, gs["g_mix"] = _norm_bwd_matmul(
        "dh1", dproj, pl.BlockSpec((tm, tki), lambda i, k: (i, k)), NI // tki,
        p["w_in"], pl.BlockSpec((D, tki), lambda i, k: (0, k)), x, p["g_mix"], dx2)
    gm["w_in"] = _mm_tn("dw_in", h1, dproj, bf16)
    return loss_t[0, 0], grad_x, gm, gs


_MATRICES = {"w_in": "col", "w_pool": "mid", "w_a": "mid", "w_i": "mid", "w_pool_proj": "col", "w_lru_proj": "row",
             "w_out": "row", "w_up": "col", "w_down": "row"}
_CONVS = ("lru_conv_w", "ffn_conv_w")
_VECTORS = ("g_mix", "b_gate", "pool_scale", "lru_conv_b", "b_a", "b_i", "lru_lambda", "g_mlp", "ffn_conv_b", "g_final")
_WEIGHTS = ("g_mix", "w_in", "b_gate", "w_pool", "pool_scale", "lru_conv_w", "lru_conv_b", "w_a", "b_a", "w_i", "b_i",
            "lru_lambda", "w_pool_proj", "w_lru_proj", "w_out", "g_mlp", "w_up", "ffn_conv_w", "ffn_conv_b", "w_down", "g_final")


def _full_shape(shape, kind):
    s = list(shape)
    s[{"col": 1, "row": 0, "mid": 1}[kind]] *= _N_DEV
    return tuple(s)


def _step(x, target, w, m, v):
    x, target = x[0], target[0]
    me = _blk(*_my_place())

    names = list(_MATRICES) + list(_CONVS)
    shards = [w[n].astype(bf16) for n in _MATRICES] + [_pad_rows8(w[n]) for n in _CONVS]
    kinds = [_MATRICES[n] for n in _MATRICES] + ["col"] * len(_CONVS)
    fulls = _all_gather(shards, kinds, [_full_shape(s.shape, k) for s, k in zip(shards, kinds)])
    p = dict(zip(names, fulls))
    for n in _VECTORS:
        p[n] = w[n].reshape(1, -1)

    loss_t, grad_x, gm, gs = _local_step(x, target, p)
    loss = lax.psum(loss_t, ("x", "y", "c"))

    small_names = list(_VECTORS) + list(_CONVS)
    small_shapes = [tuple(gs[n].shape) for n in small_names]
    small = _pack([gs[n] for n in small_names])
    mats = list(_MATRICES)
    sizes = [w[n].shape[{"col": 1, "row": 0, "mid": 1}[_MATRICES[n]]] for n in mats]
    got = _exchange([gm[n] for n in mats], [_MATRICES[n] for n in mats], sizes, [small])

    out = {}
    for n, parts in zip(mats, got[:-1]):
        shp = w[n].shape
        r2 = (-1, shp[-1])
        res = _adamw("adamw_" + n, w[n].reshape(r2), m[n].reshape(r2), v[n].reshape(r2),
                     parts.reshape((_N_DEV,) + w[n].reshape(r2).shape))
        out[n] = [a.reshape(shp) for a in res]
    gsum = _unpack(_sum_parts("sum_small", got[-1]), small_shapes)
    gsmall = dict(zip(small_names, gsum))
    for n in _CONVS:
        cols = w[n].shape[1]
        gsmall[n] = lax.dynamic_slice_in_dim(gsmall[n], me * cols, cols, axis=1)
    pk = lambda d: _pack([d[n] for n in small_names])
    res = _adamw("adamw_small", pk(w), pk(m), pk(v), pk(gsmall)[None])
    shapes = [tuple(w[n].shape) for n in small_names]
    for k, arr in enumerate(res):
        for n, a in zip(small_names, _unpack(arr, shapes)):
            out.setdefault(n, [None] * 4)[k] = a
    return loss, grad_x[None], out


def kernel(x, g_mix, w_in, b_gate, w_pool, pool_scale, lru_conv_w, lru_conv_b, w_a, b_a, w_i, b_i, lru_lambda, w_pool_proj, w_lru_proj, w_out, g_mlp, w_up, ffn_conv_w, ffn_conv_b, w_down, g_final, loss_target, m_g_mix, m_w_in, m_b_gate, m_w_pool, m_pool_scale, m_lru_conv_w, m_lru_conv_b, m_w_a, m_b_a, m_w_i, m_b_i, m_lru_lambda, m_w_pool_proj, m_w_lru_proj, m_w_out, m_g_mlp, m_w_up, m_ffn_conv_w, m_ffn_conv_b, m_w_down, m_g_final, v_g_mix, v_w_in, v_b_gate, v_w_pool, v_pool_scale, v_lru_conv_w, v_lru_conv_b, v_w_a, v_b_a, v_w_i, v_b_i, v_lru_lambda, v_w_pool_proj, v_w_lru_proj, v_w_out, v_g_mlp, v_w_up, v_ffn_conv_w, v_ffn_conv_b, v_w_down, v_g_final):
    given = dict(locals())
    orig = {n: given[n].shape for n in _WEIGHTS}

    def squeeze(a, n):
        return a if n == "g_final" else a[0]

    w = {n: squeeze(given[n], n) for n in _WEIGHTS}
    m = {n: squeeze(given["m_" + n], n) for n in _WEIGHTS}
    v = {n: squeeze(given["v_" + n], n) for n in _WEIGHTS}
    for d in (w, m, v):
        d["g_final"] = d["g_final"].reshape(1, -1)
    loss, grad_x, out = _step(x, loss_target, w, m, v)
    res = [loss, grad_x]
    for k in range(4):
        res += [out[n][k].reshape(orig[n]) for n in _WEIGHTS]
    return tuple(res)
```

```python
import functools

import jax
import jax.numpy as jnp
from jax import lax
from jax.experimental import pallas as pl
from jax.experimental.pallas import tpu as pltpu

f32 = jnp.float32
bf16 = jnp.bfloat16

_EPS = 1e-6
_LRU_C = 8.0
_POOL_WINDOWS = (2, 4, 8, 16)
_POOL_HALO = 16
_CONV_HALO = 8
_GELU_C0 = 0.7978845608028654
_GELU_C1 = 0.044715
_ADAM_LR, _ADAM_B1, _ADAM_B2, _ADAM_EPS, _ADAM_WD, _ADAM_STEP = 0.001, 0.9, 0.999, 1e-08, 0.01, 10
_N_DEV = 8
_LANE = 128
_SMALL_PAD = 8 * _LANE
_VMEM_LIMIT = 60 * 1024 * 1024

_TM = 512
_TM_SMALL = 256
_TC = 256
_TK_T = 512
_TN_MAX = 1536
_CW = 1024
_TK_DOWN = 768


def _cparams(*sem):
    return pltpu.CompilerParams(dimension_semantics=tuple(sem), vmem_limit_bytes=_VMEM_LIMIT)


def _tile(n, want, mult=1):
    if n <= want:
        return n
    t = want - want % mult
    while n % t:
        t -= mult
    return t


def _gelu(x):
    return 0.5 * x * (1.0 + jnp.tanh(_GELU_C0 * (x + _GELU_C1 * (x * x * x))))


def _gelu_grad(x):
    x2 = x * x
    t = jnp.tanh(_GELU_C0 * (x + _GELU_C1 * (x2 * x)))
    return 0.5 * (1.0 + t) + 0.5 * x * (1.0 - t * t) * (_GELU_C0 * (1.0 + 3.0 * _GELU_C1 * x2))


def _sigmoid(x):
    return jax.nn.sigmoid(x)


def _dot(a, b):
    return jnp.dot(a, b, preferred_element_type=f32)


def _dot_nt(a, b):
    return lax.dot_general(a, b, (((1,), (1,)), ((), ())), preferred_element_type=f32)


def _dot_tn(a, b):
    return lax.dot_general(a, b, (((0,), (0,)), ((), ())), preferred_element_type=f32)


def _rms_fwd(xf, g):
    r = lax.rsqrt(jnp.mean(xf * xf, axis=-1, keepdims=True) + _EPS)
    return xf * r * g


def _rms_bwd(xf, g, dy):
    r = lax.rsqrt(jnp.mean(xf * xf, axis=-1, keepdims=True) + _EPS)
    gy = dy * g
    dx = r * gy - xf * ((r * r * r) * jnp.mean(xf * gy, axis=-1, keepdims=True))
    dg = jnp.sum(dy * (xf * r), axis=0, keepdims=True)
    return dx, dg


def _matmul(name, mode, operands, in_specs, out_shape, out_spec, grid, acc_shape):
    dot = {"nn": _dot, "nt": _dot_nt, "tn": _dot_tn}[mode]
    nk = grid[2]

    def body(a_ref, b_ref, o_ref, acc_ref):
        k = pl.program_id(2)

        @pl.when(k == 0)
        def _():
            acc_ref[...] = jnp.zeros_like(acc_ref)

        acc_ref[...] += dot(a_ref[...], b_ref[...])

        @pl.when(k == nk - 1)
        def _():
            o_ref[...] = acc_ref[...].astype(o_ref.dtype)

    return pl.pallas_call(
        body, name=name, grid=grid, in_specs=in_specs, out_specs=out_spec, out_shape=out_shape,
        scratch_shapes=[pltpu.VMEM(acc_shape, f32)],
        compiler_params=_cparams("parallel", "parallel", "arbitrary"),
    )(*operands)


def _mm_nn(name, a, b, out_dtype):
    M, K = a.shape
    N = b.shape[1]
    tm, tn = _tile(M, _TM), _tile(N, _TN_MAX)
    return _matmul(
        name, "nn", (a, b),
        [pl.BlockSpec((tm, K), lambda i, j, k: (i, 0)), pl.BlockSpec((K, tn), lambda i, j, k: (0, j))],
        jax.ShapeDtypeStruct((M, N), out_dtype), pl.BlockSpec((tm, tn), lambda i, j, k: (i, j)),
        (M // tm, N // tn, 1), (tm, tn))


def _mm_nt(name, a, b, out_dtype):
    M, K = a.shape
    N = b.shape[0]
    tm, tn = _tile(M, _TM), _tile(N, _TN_MAX)
    return _matmul(
        name, "nt", (a, b),
        [pl.BlockSpec((tm, K), lambda i, j, k: (i, 0)), pl.BlockSpec((tn, K), lambda i, j, k: (j, 0))],
        jax.ShapeDtypeStruct((M, N), out_dtype), pl.BlockSpec((tm, tn), lambda i, j, k: (i, j)),
        (M // tm, N // tn, 1), (tm, tn))


def _mm_tn(name, a, b, out_dtype, tm_want=2048):
    T, M = a.shape
    N = b.shape[1]
    tm, tn, tk = _tile(M, tm_want), _tile(N, _TN_MAX), _tile(T, _TK_T)
    return _matmul(
        name, "tn", (a, b),
        [pl.BlockSpec((tk, tm), lambda i, j, k: (k, i)), pl.BlockSpec((tk, tn), lambda i, j, k: (k, j))],
        jax.ShapeDtypeStruct((M, N), out_dtype), pl.BlockSpec((tm, tn), lambda i, j, k: (i, j)),
        (M // tm, N // tn, T // tk), (tm, tn))


def _in_proj(x, g_mix, w_in):
    T, D = x.shape
    NI = w_in.shape[1]
    tm, tn = _tile(T, _TM), _tile(NI, _TN_MAX)

    def body(x_ref, g_ref, w_ref, h_ref, o_ref, h_scr):
        @pl.when(pl.program_id(1) == 0)
        def _():
            h = _rms_fwd(x_ref[...], g_ref[...]).astype(bf16)
            h_scr[...] = h
            h_ref[...] = h

        o_ref[...] = _dot(h_scr[...], w_ref[...])

    return pl.pallas_call(
        body, name="in_proj", grid=(T // tm, NI // tn),
        in_specs=[pl.BlockSpec((tm, D), lambda i, j: (i, 0)), pl.BlockSpec((1, D), lambda i, j: (0, 0)),
                  pl.BlockSpec((D, tn), lambda i, j: (0, j))],
        out_specs=[pl.BlockSpec((tm, D), lambda i, j: (i, 0)), pl.BlockSpec((tm, tn), lambda i, j: (i, j))],
        out_shape=[jax.ShapeDtypeStruct((T, D), bf16), jax.ShapeDtypeStruct((T, NI), f32)],
        scratch_shapes=[pltpu.VMEM((tm, D), bf16)],
        compiler_params=_cparams("parallel", "arbitrary"),
    )(x, g_mix, w_in)


def _window_means(ext_ref, row0, tc, gw):
    H = _POOL_HALO
    t_glob = row0 + lax.broadcasted_iota(jnp.int32, (tc, 1), 0)
    out = []
    for g, w in enumerate(_POOL_WINDOWS):
        s = ext_ref[:, g * gw:(g + 1) * gw]
        st = 1
        while st < w:
            s = s + pltpu.roll(s, st, 0)
            st *= 2
        cnt = jnp.minimum(t_glob + 1, w).astype(f32)
        out.append((s[H:, :] / cnt, ext_ref[pl.ds(H, tc), g * gw:(g + 1) * gw]))
    return out


def _pool_fwd(proj, w_pool, pool_scale):
    T = proj.shape[0]
    G, gw, _ = w_pool.shape
    PW = G * gw
    tc = _tile(T, _TC)
    H = _POOL_HALO

    def body(u_ref, w_ref, s_ref, y_ref, ext_ref):
        i = pl.program_id(0)

        @pl.when(i == 0)
        def _():
            ext_ref[pl.ds(0, H), :] = jnp.zeros((H, PW), f32)

        ext_ref[pl.ds(H, tc), :] = u_ref[...]
        for g, (m, u) in enumerate(_window_means(ext_ref, i * tc, tc, gw)):
            d = (m - u).astype(bf16)
            y = _dot(d, w_ref[g]) * s_ref[:, g * gw:(g + 1) * gw]
            y_ref[:, g * gw:(g + 1) * gw] = y.astype(bf16)
        ext_ref[pl.ds(0, H), :] = ext_ref[pl.ds(tc, H), :]

    return pl.pallas_call(
        body, name="pool_fwd", grid=(T // tc,),
        in_specs=[pl.BlockSpec((tc, PW), lambda i: (i, 0)), pl.BlockSpec((G, gw, gw), lambda i: (0, 0, 0)),
                  pl.BlockSpec((1, PW), lambda i: (0, 0))],
        out_specs=pl.BlockSpec((tc, PW), lambda i: (i, 0)),
        out_shape=jax.ShapeDtypeStruct((T, PW), bf16),
        scratch_shapes=[pltpu.VMEM((H + tc, PW), f32)],
        compiler_params=_cparams("arbitrary"),
    )(proj, w_pool, pool_scale)


def _lru_gates(v, wa, wi, ba, bi, sp):
    vb = v.astype(bf16)
    r = _sigmoid(_dot(vb, wa) + ba)
    i = _sigmoid(_dot(vb, wi) + bi)
    a = jnp.exp(-_LRU_C * r * sp)
    mult = jnp.sqrt(1.0 - a * a)
    return r, i, a, mult


def _softplus(z):
    return jnp.maximum(z, 0.0) + jnp.log1p(jnp.exp(-jnp.abs(z)))


def _causal_conv(ext_ref, cw_ref, cb_ref, tc, K, cols=slice(None)):
    H = _CONV_HALO
    v = cb_ref[:, cols] + cw_ref[K - 1:K, cols] * ext_ref[pl.ds(H, tc), cols]
    for k in range(K - 1):
        v = v + cw_ref[k:k + 1, cols] * ext_ref[pl.ds(H - (K - 1 - k), tc), cols]
    return v


def _lru_fwd(proj, D, PW, conv_w, conv_b, w_a, b_a, w_i, b_i, lam):
    T = proj.shape[0]
    NB, bw, _ = w_a.shape
    K = 4
    tc = _tile(T, _TC)
    H = _CONV_HALO
    hb = D // 2
    assert PW == hb and conv_w.shape[0] == 8

    def body(u0_ref, u1_ref, g0_ref, g1_ref, cw_ref, cb_ref, wa_ref, ba_ref, wi_ref, bi_ref, lam_ref,
             y_ref, hs_ref, ext_ref, a_scr, b_scr, hc_scr):
        c = pl.program_id(0)

        @pl.when(c == 0)
        def _():
            ext_ref[pl.ds(0, H), :] = jnp.zeros((H, D), f32)
            hc_scr[...] = jnp.zeros_like(hc_scr)

        ext_ref[pl.ds(H, tc), 0:hb] = u0_ref[...]
        ext_ref[pl.ds(H, tc), hb:D] = u1_ref[...]
        sp = _softplus(-lam_ref[...])
        for b in range(NB):
            cols = slice(b * bw, (b + 1) * bw)
            v = _causal_conv(ext_ref, cw_ref, cb_ref, tc, K, cols)
            _, i, a, mult = _lru_gates(v, wa_ref[b], wi_ref[b], ba_ref[:, cols], bi_ref[:, cols], sp[:, cols])
            a_scr[:, cols] = a
            b_scr[:, cols] = mult * (i * v)

        def step(t, h):
            h = a_scr[pl.ds(t, 1), :] * h + b_scr[pl.ds(t, 1), :]
            hs_ref[pl.ds(t, 1), :] = h
            return h

        h = lax.fori_loop(0, tc, step, hc_scr[0:1, :], unroll=8)
        hc_scr[0:1, :] = h
        y_ref[:, 0:hb] = (hs_ref[:, 0:hb] * _gelu(g0_ref[...])).astype(bf16)
        y_ref[:, hb:D] = (hs_ref[:, hb:D] * _gelu(g1_ref[...])).astype(bf16)
        ext_ref[pl.ds(0, H), :] = ext_ref[pl.ds(tc, H), :]

    vec = pl.BlockSpec((1, D), lambda c: (0, 0))
    wspec = pl.BlockSpec((NB, bw, bw), lambda c: (0, 0, 0))
    return pl.pallas_call(
        body, name="lru_fwd", grid=(T // tc,),
        in_specs=[pl.BlockSpec((tc, hb), lambda c: (c, 1)), pl.BlockSpec((tc, hb), lambda c: (c, 2)),
                  pl.BlockSpec((tc, hb), lambda c: (c, 3)), pl.BlockSpec((tc, hb), lambda c: (c, 4)),
                  pl.BlockSpec((8, D), lambda c: (0, 0)), vec, wspec, vec, wspec, vec, vec],
        out_specs=[pl.BlockSpec((tc, D), lambda c: (c, 0)), pl.BlockSpec((tc, D), lambda c: (c, 0))],
        out_shape=[jax.ShapeDtypeStruct((T, D), bf16), jax.ShapeDtypeStruct((T, D), f32)],
        scratch_shapes=[pltpu.VMEM((H + tc, D), f32), pltpu.VMEM((tc, D), f32), pltpu.VMEM((tc, D), f32),
                        pltpu.VMEM((8, D), f32)],
        compiler_params=_cparams("arbitrary"),
    )(proj, proj, proj, proj, conv_w, conv_b, w_a, b_a, w_i, b_i, lam)


def _merge_fwd(y_pool, y_lru, w_pp, w_lp, proj, b_gate):
    T, PW = y_pool.shape
    D = y_lru.shape[1]
    tm, tn = _tile(T, _TM), _tile(D, PW)
    nj = D // tn
    off = (PW + 2 * D) // tn

    def body(yp_ref, yl_ref, wp_ref, wl_ref, l0_ref, l1_ref, b0_ref, b1_ref, p_ref, q_ref, m_ref):
        p = _dot(yp_ref[...], wp_ref[...])
        q = _dot(yl_ref[...], wl_ref[...])
        p_ref[...] = p
        q_ref[...] = q
        g0 = _sigmoid(l0_ref[...] + b0_ref[...])
        g1 = _sigmoid(l1_ref[...] + b1_ref[...])
        m_ref[...] = (g0 * p + g1 * q).astype(bf16)

    tile = pl.BlockSpec((tm, tn), lambda i, j: (i, j))
    return pl.pallas_call(
        body, name="merge_fwd", grid=(T // tm, nj),
        in_specs=[pl.BlockSpec((tm, PW), lambda i, j: (i, 0)), pl.BlockSpec((tm, D), lambda i, j: (i, 0)),
                  pl.BlockSpec((PW, tn), lambda i, j: (0, j)), pl.BlockSpec((D, tn), lambda i, j: (0, j)),
                  pl.BlockSpec((tm, tn), lambda i, j: (i, off + j)), pl.BlockSpec((tm, tn), lambda i, j: (i, off + nj + j)),
                  pl.BlockSpec((1, tn), lambda i, j: (0, j)), pl.BlockSpec((1, tn), lambda i, j: (0, nj + j))],
        out_specs=[tile, tile, tile],
        out_shape=[jax.ShapeDtypeStruct((T, D), f32), jax.ShapeDtypeStruct((T, D), f32), jax.ShapeDtypeStruct((T, D), bf16)],
        compiler_params=_cparams("parallel", "arbitrary"),
    )(y_pool, y_lru, w_pp, w_lp, proj, proj, b_gate, b_gate)


def _out_proj(merged, w_out, x, g_mlp):
    T, D = x.shape
    tm = _tile(T, _TM_SMALL)

    def body(m_ref, w_ref, x_ref, g_ref, x2_ref, h2_ref):
        x2 = x_ref[...] + _dot(m_ref[...], w_ref[...])
        x2_ref[...] = x2
        h2_ref[...] = _rms_fwd(x2, g_ref[...]).astype(bf16)

    row = pl.BlockSpec((tm, D), lambda i: (i, 0))
    return pl.pallas_call(
        body, name="out_proj", grid=(T // tm,),
        in_specs=[row, pl.BlockSpec((D, D), lambda i: (0, 0)), row, pl.BlockSpec((1, D), lambda i: (0, 0))],
        out_specs=[row, row],
        out_shape=[jax.ShapeDtypeStruct((T, D), f32), jax.ShapeDtypeStruct((T, D), bf16)],
        compiler_params=_cparams("parallel"),
    )(merged, w_out, x, g_mlp)


def _ffn_fwd(up, conv_w, conv_b):
    T, F2 = up.shape
    F = F2 // 2
    K = 3
    tc = _tile(T, 2 * _TC)
    cw = _tile(F, _CW)
    ns = F // cw
    H = _CONV_HALO

    def body(gp_ref, val_ref, cw_ref, cb_ref, z_ref, ext_ref):
        @pl.when(pl.program_id(1) == 0)
        def _():
            ext_ref[pl.ds(0, H), :] = jnp.zeros((H, cw), f32)

        ext_ref[pl.ds(H, tc), :] = gp_ref[...]
        c = _causal_conv(ext_ref, cw_ref, cb_ref, tc, K)
        z_ref[...] = (_gelu(c) * val_ref[...]).astype(bf16)
        ext_ref[pl.ds(0, H), :] = ext_ref[pl.ds(tc, H), :]

    return pl.pallas_call(
        body, name="ffn_fwd", grid=(ns, T // tc),
        in_specs=[pl.BlockSpec((tc, cw), lambda s, c: (c, s)), pl.BlockSpec((tc, cw), lambda s, c: (c, ns + s)),
                  pl.BlockSpec((8, cw), lambda s, c: (0, s)), pl.BlockSpec((1, cw), lambda s, c: (0, s))],
        out_specs=pl.BlockSpec((tc, cw), lambda s, c: (c, s)),
        out_shape=jax.ShapeDtypeStruct((T, F), bf16),
        scratch_shapes=[pltpu.VMEM((H + tc, cw), f32)],
        compiler_params=_cparams("parallel", "arbitrary"),
    )(up, up, conv_w, conv_b)


def _down_loss(z, w_down, x2, target, g_final):
    T, F = z.shape
    D = x2.shape[1]
    tm, tk = _tile(T, _TM_SMALL), _tile(F, _TK_DOWN)
    nk = F // tk

    def body(z_ref, w_ref, x2_ref, t_ref, g_ref, dx_ref, dxb_ref, loss_ref, dg_ref, acc_ref):
        i, k = pl.program_id(0), pl.program_id(1)

        @pl.when(k == 0)
        def _():
            acc_ref[...] = x2_ref[...]

        @pl.when((i == 0) & (k == 0))
        def _():
            loss_ref[...] = jnp.zeros_like(loss_ref)
            dg_ref[...] = jnp.zeros_like(dg_ref)

        acc_ref[...] += _dot(z_ref[...], w_ref[...])

        @pl.when(k == nk - 1)
        def _():
            x3 = acc_ref[...]
            g = g_ref[...]
            e = _rms_fwd(x3, g) - t_ref[...]
            loss_ref[...] += 0.5 * jnp.sum(jnp.mean(e * e, axis=-1, keepdims=True))
            dx, dg = _rms_bwd(x3, g, e * (1.0 / D))
            dg_ref[...] += dg
            dx_ref[...] = dx
            dxb_ref[...] = dx.astype(bf16)

    row = pl.BlockSpec((tm, D), lambda i, k: (i, 0))
    vec = pl.BlockSpec((1, D), lambda i, k: (0, 0))
    return pl.pallas_call(
        body, name="down_loss", grid=(T // tm, nk),
        in_specs=[pl.BlockSpec((tm, tk), lambda i, k: (i, k)), pl.BlockSpec((tk, D), lambda i, k: (k, 0)), row, row, vec],
        out_specs=[row, row, pl.BlockSpec((8, _LANE), lambda i, k: (0, 0)), vec],
        out_shape=[jax.ShapeDtypeStruct((T, D), f32), jax.ShapeDtypeStruct((T, D), bf16),
                   jax.ShapeDtypeStruct((8, _LANE), f32), jax.ShapeDtypeStruct((1, D), f32)],
        scratch_shapes=[pltpu.VMEM((tm, D), f32)],
        compiler_params=_cparams("arbitrary", "arbitrary"),
    )(z, w_down, x2, target, g_final)


def _ffn_bwd(dz, up, conv_w, conv_b):
    T, F = dz.shape
    K = 3
    tc = _tile(T, 2 * _TC)
    cw = _tile(F, _CW)
    ns, nt = F // cw, T // tc
    H = _CONV_HALO

    def body(dz_ref, gp_ref, val_ref, gph_ref, cw_ref, cb_ref, dup_ref, dcw_ref, dcb_ref, ext_ref, dext_ref):
        j = pl.program_id(1)
        first = j == nt - 1

        @pl.when(j == 0)
        def _():
            dext_ref[pl.ds(tc, H), :] = jnp.zeros((H, cw), f32)
            dcw_ref[...] = jnp.zeros_like(dcw_ref)
            dcb_ref[...] = jnp.zeros_like(dcb_ref)

        ext_ref[pl.ds(0, H), :] = jnp.where(first, 0.0, gph_ref[...])
        gp = gp_ref[...]
        ext_ref[pl.ds(H, tc), :] = gp
        c = _causal_conv(ext_ref, cw_ref, cb_ref, tc, K)
        dzv = dz_ref[...]
        dup_ref[1] = (dzv * _gelu(c)).astype(bf16)
        dc = dzv * val_ref[...] * _gelu_grad(c)
        dext_ref[pl.ds(0, tc), :] = dc
        dcb_ref[...] += jnp.sum(dc, axis=0, keepdims=True)
        dgp = cw_ref[K - 1:K, :] * dc
        dcw_ref[K - 1:K, :] += jnp.sum(gp * dc, axis=0, keepdims=True)
        for k in range(K - 1):
            sh = dext_ref[pl.ds(K - 1 - k, tc), :]
            dgp = dgp + cw_ref[k:k + 1, :] * sh
            dcw_ref[k:k + 1, :] += jnp.sum(gp * sh, axis=0, keepdims=True)
        dup_ref[0] = dgp.astype(bf16)
        dext_ref[pl.ds(tc, H), :] = dext_ref[pl.ds(0, H), :]

    hblk = tc // H
    return pl.pallas_call(
        body, name="ffn_bwd", grid=(ns, nt),
        in_specs=[pl.BlockSpec((tc, cw), lambda s, j: (nt - 1 - j, s)),
                  pl.BlockSpec((tc, cw), lambda s, j: (nt - 1 - j, s)),
                  pl.BlockSpec((tc, cw), lambda s, j: (nt - 1 - j, ns + s)),
                  pl.BlockSpec((H, cw), lambda s, j: (jnp.maximum((nt - 1 - j) * hblk - 1, 0), s)),
                  pl.BlockSpec((8, cw), lambda s, j: (0, s)), pl.BlockSpec((1, cw), lambda s, j: (0, s))],
        out_specs=[pl.BlockSpec((2, tc, cw), lambda s, j: (0, nt - 1 - j, s)),
                   pl.BlockSpec((8, cw), lambda s, j: (0, s)), pl.BlockSpec((1, cw), lambda s, j: (0, s))],
        out_shape=[jax.ShapeDtypeStruct((2, T, F), bf16), jax.ShapeDtypeStruct((8, F), f32), jax.ShapeDtypeStruct((1, F), f32)],
        scratch_shapes=[pltpu.VMEM((H + tc, cw), f32), pltpu.VMEM((tc + H, cw), f32)],
        compiler_params=_cparams("parallel", "arbitrary"),
    )(dz, up, up, up, conv_w, conv_b)


def _norm_bwd_matmul(name, a, a_spec, nk, w, w_spec, x, g, dres):
    T, D = x.shape
    tm = a_spec.block_shape[-2]

    def body(a_ref, w_ref, x_ref, g_ref, dr_ref, dx_ref, dxb_ref, dg_ref, acc_ref):
        i, k = pl.program_id(0), pl.program_id(1)

        @pl.when(k == 0)
        def _():
            acc_ref[...] = jnp.zeros_like(acc_ref)

        @pl.when((i == 0) & (k == 0))
        def _():
            dg_ref[...] = jnp.zeros_like(dg_ref)

        acc_ref[...] += _dot_nt(a_ref[...], w_ref[...])

        @pl.when(k == nk - 1)
        def _():
            dxn, dg = _rms_bwd(x_ref[...], g_ref[...], acc_ref[...])
            dg_ref[...] += dg
            dx = dr_ref[...] + dxn
            dx_ref[...] = dx
            dxb_ref[...] = dx.astype(bf16)

    row = pl.BlockSpec((tm, D), lambda i, k: (i, 0))
    vec = pl.BlockSpec((1, D), lambda i, k: (0, 0))
    return pl.pallas_call(
        body, name=name, grid=(T // tm, nk),
        in_specs=[a_spec, w_spec, row, vec, row],
        out_specs=[row, row, vec],
        out_shape=[jax.ShapeDtypeStruct((T, D), f32), jax.ShapeDtypeStruct((T, D), bf16), jax.ShapeDtypeStruct((1, D), f32)],
        scratch_shapes=[pltpu.VMEM((tm, D), f32)],
        compiler_params=_cparams("arbitrary", "arbitrary"),
    )(a, w, x, g, dres)


def _merge_bwd(dx2b, w_out, p, q, proj, b_gate, PW):
    T, D = p.shape
    tm, tn = _tile(T, _TM), _tile(D, PW)
    nj = D // tn
    off = (PW + 2 * D) // tn

    def body(dx_ref, w_ref, p_ref, q_ref, l0_ref, l1_ref, b0_ref, b1_ref, dp_ref, dq_ref, dl0_ref, dl1_ref, db0_ref, db1_ref):
        @pl.when(pl.program_id(1) == 0)
        def _():
            db0_ref[...] = jnp.zeros_like(db0_ref)
            db1_ref[...] = jnp.zeros_like(db1_ref)

        dm = _dot_nt(dx_ref[...], w_ref[...])
        g0 = _sigmoid(l0_ref[...] + b0_ref[...])
        g1 = _sigmoid(l1_ref[...] + b1_ref[...])
        dp_ref[...] = (g0 * dm).astype(bf16)
        dq_ref[...] = (g1 * dm).astype(bf16)
        dl0 = dm * p_ref[...] * (g0 * (1.0 - g0))
        dl1 = dm * q_ref[...] * (g1 * (1.0 - g1))
        dl0_ref[...] = dl0.astype(bf16)
        dl1_ref[...] = dl1.astype(bf16)
        db0_ref[...] += jnp.sum(dl0, axis=0, keepdims=True)
        db1_ref[...] += jnp.sum(dl1, axis=0, keepdims=True)

    tile = pl.BlockSpec((tm, tn), lambda j, i: (i, j))
    vecj = pl.BlockSpec((1, tn), lambda j, i: (0, j))
    tb = jax.ShapeDtypeStruct((T, D), bf16)
    vb = jax.ShapeDtypeStruct((1, D), f32)
    return pl.pallas_call(
        body, name="merge_bwd", grid=(nj, T // tm),
        in_specs=[pl.BlockSpec((tm, D), lambda j, i: (i, 0)), pl.BlockSpec((tn, D), lambda j, i: (j, 0)), tile, tile,
                  pl.BlockSpec((tm, tn), lambda j, i: (i, off + j)), pl.BlockSpec((tm, tn), lambda j, i: (i, off + nj + j)),
                  vecj, pl.BlockSpec((1, tn), lambda j, i: (0, nj + j))],
        out_specs=[tile, tile, tile, tile, vecj, vecj],
        out_shape=[tb, tb, tb, tb, vb, vb],
        compiler_params=_cparams("parallel", "arbitrary"),
    )(dx2b, w_out, p, q, proj, proj, b_gate, b_gate)


def _lru_bwd(dy, proj, hs, D, conv_w, conv_b, w_a, b_a, w_i, b_i, lam):
    T = dy.shape[0]
    NB, bw, _ = w_a.shape
    K = 4
    tc = _tile(T, _TC)
    nt = T // tc
    H = _CONV_HALO
    hb = D // 2

    def body(dy_ref, u0_ref, u1_ref, g0_ref, g1_ref, hs_ref, uh0_ref, uh1_ref, hh_ref,
             cw_ref, cb_ref, wa_ref, ba_ref, wi_ref, bi_ref, lam_ref,
             du_ref, dg_ref, dwa_ref, dwi_ref, dcw_ref, dvec_ref,
             ext_ref, hext_ref, dext_ref, a_scr, r_scr, i_scr, v_scr, g_scr, car_scr):
        j = pl.program_id(0)
        first = j == nt - 1

        @pl.when(j == 0)
        def _():
            dext_ref[pl.ds(tc, H), :] = jnp.zeros((H, D), f32)
            car_scr[...] = jnp.zeros_like(car_scr)
            dwa_ref[...] = jnp.zeros_like(dwa_ref)
            dwi_ref[...] = jnp.zeros_like(dwi_ref)
            dcw_ref[...] = jnp.zeros_like(dcw_ref)
            dvec_ref[...] = jnp.zeros_like(dvec_ref)

        ext_ref[pl.ds(0, H), 0:hb] = jnp.where(first, 0.0, uh0_ref[...])
        ext_ref[pl.ds(0, H), hb:D] = jnp.where(first, 0.0, uh1_ref[...])
        ext_ref[pl.ds(H, tc), 0:hb] = u0_ref[...]
        ext_ref[pl.ds(H, tc), hb:D] = u1_ref[...]
        hext_ref[pl.ds(0, H), :] = jnp.where(first, 0.0, hh_ref[...])
        hext_ref[pl.ds(H, tc), :] = hs_ref[...]
        lamv = lam_ref[...]
        sp = _softplus(-lamv)

        for b in range(NB):
            cols = slice(b * bw, (b + 1) * bw)
            v = _causal_conv(ext_ref, cw_ref, cb_ref, tc, K, cols)
            r, i, a, _ = _lru_gates(v, wa_ref[b], wi_ref[b], ba_ref[:, cols], bi_ref[:, cols], sp[:, cols])
            v_scr[:, cols] = v
            r_scr[:, cols] = r
            i_scr[:, cols] = i
            a_scr[:, cols] = a
        for half, g_ref in enumerate((g0_ref, g1_ref)):
            cols = slice(half * hb, (half + 1) * hb)
            ug = g_ref[...]
            dyv = dy_ref[:, cols]
            g_scr[:, cols] = dyv * _gelu(ug)
            dg_ref[:, cols] = (dyv * hs_ref[:, cols] * _gelu_grad(ug)).astype(bf16)

        def step(s, carry):
            t = tc - 1 - s
            g = g_scr[pl.ds(t, 1), :] + carry
            g_scr[pl.ds(t, 1), :] = g
            return a_scr[pl.ds(t, 1), :] * g

        car_scr[0:1, :] = lax.fori_loop(0, tc, step, car_scr[0:1, :], unroll=8)

        for b in range(NB):
            cols = slice(b * bw, (b + 1) * bw)
            g = g_scr[:, cols]
            v, r, i, a = v_scr[:, cols], r_scr[:, cols], i_scr[:, cols], a_scr[:, cols]
            mult = jnp.sqrt(1.0 - a * a)
            h_prev = hext_ref[pl.ds(H - 1, tc), cols]
            dmult = g * (i * v)
            di = g * mult * v
            dv = g * mult * i
            da = g * h_prev - dmult * (a / mult)
            dlog_a = da * a
            spb = sp[:, cols]
            dr = dlog_a * (-_LRU_C * spb)
            dsp = jnp.sum(dlog_a * (-_LRU_C * r), axis=0, keepdims=True)
            dpr = dr * r * (1.0 - r)
            dpi = di * i * (1.0 - i)
            dprb, dpib, vb = dpr.astype(bf16), dpi.astype(bf16), v.astype(bf16)
            dv = dv + _dot_nt(dprb, wa_ref[b]) + _dot_nt(dpib, wi_ref[b])
            dwa_ref[b] += _dot_tn(vb, dprb)
            dwi_ref[b] += _dot_tn(vb, dpib)
            dext_ref[pl.ds(0, tc), cols] = dv
            dvec_ref[0:1, cols] += jnp.sum(dv, axis=0, keepdims=True)
            dvec_ref[1:2, cols] += jnp.sum(dpr, axis=0, keepdims=True)
            dvec_ref[2:3, cols] += jnp.sum(dpi, axis=0, keepdims=True)
            dvec_ref[3:4, cols] += dsp * (-_sigmoid(-lamv[:, cols]))

        for b in range(NB):
            cols = slice(b * bw, (b + 1) * bw)
            u = ext_ref[pl.ds(H, tc), cols]
            du = jnp.zeros((tc, bw), f32)
            for k in range(K):
                sh = dext_ref[pl.ds(K - 1 - k, tc), cols]
                du = du + cw_ref[k:k + 1, cols] * sh
                dcw_ref[k:k + 1, cols] += jnp.sum(u * sh, axis=0, keepdims=True)
            du_ref[:, cols] = du.astype(bf16)
        dext_ref[pl.ds(tc, H), :] = dext_ref[pl.ds(0, H), :]

    hblk = tc // H
    rev = lambda j: nt - 1 - j
    halo = lambda j: jnp.maximum((nt - 1 - j) * hblk - 1, 0)
    vec = pl.BlockSpec((1, D), lambda j: (0, 0))
    wspec = pl.BlockSpec((NB, bw, bw), lambda j: (0, 0, 0))
    acc8 = pl.BlockSpec((8, D), lambda j: (0, 0))
    big = pltpu.VMEM((tc, D), f32)
    return pl.pallas_call(
        body, name="lru_bwd", grid=(nt,),
        in_specs=[pl.BlockSpec((tc, D), lambda j: (rev(j), 0)),
                  pl.BlockSpec((tc, hb), lambda j: (rev(j), 1)), pl.BlockSpec((tc, hb), lambda j: (rev(j), 2)),
                  pl.BlockSpec((tc, hb), lambda j: (rev(j), 3)), pl.BlockSpec((tc, hb), lambda j: (rev(j), 4)),
                  pl.BlockSpec((tc, D), lambda j: (rev(j), 0)),
                  pl.BlockSpec((H, hb), lambda j: (halo(j), 1)), pl.BlockSpec((H, hb), lambda j: (halo(j), 2)),
                  pl.BlockSpec((H, D), lambda j: (halo(j), 0)),
                  acc8, vec, wspec, vec, wspec, vec, vec],
        out_specs=[pl.BlockSpec((tc, D), lambda j: (rev(j), 0)), pl.BlockSpec((tc, D), lambda j: (rev(j), 0)),
                   wspec, wspec, acc8, acc8],
        out_shape=[jax.ShapeDtypeStruct((T, D), bf16), jax.ShapeDtypeStruct((T, D), bf16),
                   jax.ShapeDtypeStruct((NB, bw, bw), f32), jax.ShapeDtypeStruct((NB, bw, bw), f32),
                   jax.ShapeDtypeStruct((8, D), f32), jax.ShapeDtypeStruct((8, D), f32)],
        scratch_shapes=[pltpu.VMEM((H + tc, D), f32), pltpu.VMEM((H + tc, D), f32), pltpu.VMEM((tc + H, D), f32),
                        big, big, big, big, big, pltpu.VMEM((8, D), f32)],
        compiler_params=_cparams("arbitrary"),
    )(dy, proj, proj, proj, proj, hs, proj, proj, hs, conv_w, conv_b, w_a, b_a, w_i, b_i, lam)


def _pool_bwd(dy, proj, w_pool, pool_scale):
    T, PW = dy.shape
    G, gw, _ = w_pool.shape
    tc = _tile(T, _TC)
    nt = T // tc
    H = _POOL_HALO

    def body(dy_ref, u_ref, uh_ref, w_ref, s_ref, du_ref, dw_ref, ds_ref, ext_ref, eext_ref):
        j = pl.program_id(0)
        first = j == nt - 1
        row0 = (nt - 1 - j) * tc

        @pl.when(j == 0)
        def _():
            eext_ref[pl.ds(tc, H), :] = jnp.zeros((H, PW), f32)
            dw_ref[...] = jnp.zeros_like(dw_ref)
            ds_ref[...] = jnp.zeros_like(ds_ref)

        ext_ref[pl.ds(0, H), :] = jnp.where(first, 0.0, uh_ref[...])
        ext_ref[pl.ds(H, tc), :] = u_ref[...]
        t_glob = row0 + lax.broadcasted_iota(jnp.int32, (tc, 1), 0)
        dds = []
        for g, (m, u) in enumerate(_window_means(ext_ref, row0, tc, gw)):
            cols = slice(g * gw, (g + 1) * gw)
            d = (m - u).astype(bf16)
            yraw = _dot(d, w_ref[g])
            dyv = dy_ref[:, cols]
            ds_ref[:, cols] += jnp.sum(dyv * yraw, axis=0, keepdims=True)
            dyr = (dyv * s_ref[:, cols]).astype(bf16)
            dd = _dot_nt(dyr, w_ref[g])
            dw_ref[g] += _dot_tn(d, dyr)
            cnt = jnp.minimum(t_glob + 1, _POOL_WINDOWS[g]).astype(f32)
            eext_ref[pl.ds(0, tc), cols] = dd / cnt
            dds.append(dd)
        n = tc + H
        for g, w in enumerate(_POOL_WINDOWS):
            cols = slice(g * gw, (g + 1) * gw)
            s = eext_ref[:, cols]
            st = 1
            while st < w:
                s = s + pltpu.roll(s, n - st, 0)
                st *= 2
            du_ref[:, cols] = (s[0:tc, :] - dds[g]).astype(bf16)
        eext_ref[pl.ds(tc, H), :] = eext_ref[pl.ds(0, H), :]

    hblk = tc // H
    return pl.pallas_call(
        body, name="pool_bwd", grid=(nt,),
        in_specs=[pl.BlockSpec((tc, PW), lambda j: (nt - 1 - j, 0)), pl.BlockSpec((tc, PW), lambda j: (nt - 1 - j, 0)),
                  pl.BlockSpec((H, PW), lambda j: (jnp.maximum((nt - 1 - j) * hblk - 1, 0), 0)),
                  pl.BlockSpec((G, gw, gw), lambda j: (0, 0, 0)), pl.BlockSpec((1, PW), lambda j: (0, 0))],
        out_specs=[pl.BlockSpec((tc, PW), lambda j: (nt - 1 - j, 0)), pl.BlockSpec((G, gw, gw), lambda j: (0, 0, 0)),
                   pl.BlockSpec((1, PW), lambda j: (0, 0))],
        out_shape=[jax.ShapeDtypeStruct((T, PW), bf16), jax.ShapeDtypeStruct((G, gw, gw), f32), jax.ShapeDtypeStruct((1, PW), f32)],
        scratch_shapes=[pltpu.VMEM((H + tc, PW), f32), pltpu.VMEM((tc + H, PW), f32)],
        compiler_params=_cparams("arbitrary"),
    )(dy, proj, proj, w_pool, pool_scale)


_MESH = pl.DeviceIdType.MESH
_HBM = pl.BlockSpec(memory_space=pltpu.HBM)


def _slab(ref, kind, blk, n):
    start = blk * n
    if n % _LANE == 0:
        start = pl.multiple_of(start, _LANE)
    if kind == "col":
        return ref.at[:, pl.ds(start, n)]
    if kind == "row":
        return ref.at[pl.ds(start, n), :]
    if kind == "mid":
        return ref.at[:, pl.ds(start, n), :]
    raise ValueError(kind)


def _my_place():
    x, y, c = lax.axis_index("x"), lax.axis_index("y"), lax.axis_index("c")
    return x, y, c


def _blk(px, py, pc):
    return 4 * px + 2 * py + pc


_SEM = pl.BlockSpec(memory_space=pltpu.SEMAPHORE)
_EFFECT = pltpu.SideEffectType.DATAFLOW_SIDE_EFFECTING


def _peers(x, y, c):
    return [(k, (x ^ (k >> 2), y ^ ((k >> 1) & 1), c ^ (k & 1))) for k in range(1, 8)]


class _Route:
    def __init__(self, mode, kind, size):
        self.mode, self.kind, self.size = mode, kind, size

    def src(self, ref, peer_blk):
        return ref if self.mode == "gather" else _slab(ref, self.kind, peer_blk, self.size)

    def dst(self, ref, origin_blk):
        return _slab(ref, self.kind, origin_blk, self.size) if self.mode == "gather" else ref.at[origin_blk]


def _send_start(name, srcs, lands, routes, groups):
    nt, ng = len(srcs), len(groups)

    def body(*refs):
        src_refs, land_refs = refs[:nt], refs[nt:2 * nt]
        sems = refs[2 * nt:2 * nt + 2 * ng]
        token = refs[-1]
        x, y, c = _my_place()
        me = _blk(x, y, c)
        for gi, grp in enumerate(groups):
            for pos, t in enumerate(grp):
                for k, peer in _peers(x, y, c):
                    pltpu.make_async_remote_copy(
                        src_ref=routes[t].src(src_refs[t], _blk(*peer)), dst_ref=routes[t].dst(land_refs[t], me),
                        send_sem=sems[2 * gi].at[7 * pos + k - 1], recv_sem=sems[2 * gi + 1].at[7 * pos + k - 1],
                        device_id=peer, device_id_type=_MESH).start()
        token[...] = jnp.zeros_like(token)

    hbm = lambda a: pltpu.HBM(a.shape, a.dtype)
    out_shape = []
    for grp in groups:
        out_shape += [pltpu.SemaphoreType.DMA((7 * len(grp),)), pltpu.SemaphoreType.DMA((7 * len(grp),))]
    out_shape += [hbm(a) for a in srcs] + [hbm(a) for a in lands] + [jax.ShapeDtypeStruct((8, _LANE), f32)]
    res = pl.pallas_call(
        body, name=name, out_shape=out_shape,
        in_specs=[_HBM] * (2 * nt),
        out_specs=[_SEM] * (2 * ng) + [_HBM] * (2 * nt) + [pl.BlockSpec(memory_space=pltpu.VMEM)],
        input_output_aliases={t: 2 * ng + t for t in range(2 * nt)},
        compiler_params=pltpu.CompilerParams(has_side_effects=_EFFECT),
    )(*[pltpu.with_memory_space_constraint(a, pltpu.HBM) for a in list(srcs) + list(lands)])
    sems = [(res[2 * g], res[2 * g + 1]) for g in range(ng)]
    return sems, res[2 * ng:2 * ng + nt], res[2 * ng + nt:2 * ng + 2 * nt], res[-1]


def _send_wait(name, srcs, lands, routes, sems, after):
    n = len(srcs)

    def body(*refs):
        src_refs, land_refs = refs[:n], refs[n:2 * n]
        send_sems, recv_sems = refs[2 * n], refs[2 * n + 1]
        x, y, c = _my_place()
        for pos in range(n):
            for k, peer in _peers(x, y, c):
                pb = _blk(*peer)
                cp = pltpu.make_async_remote_copy(
                    src_ref=routes[pos].src(src_refs[pos], pb), dst_ref=routes[pos].dst(land_refs[pos], pb),
                    send_sem=send_sems.at[7 * pos + k - 1], recv_sem=recv_sems.at[7 * pos + k - 1],
                    device_id=peer, device_id_type=_MESH)
                cp.wait_send()
                cp.wait_recv()

    hbm = lambda a: pltpu.HBM(a.shape, a.dtype)
    res = pl.pallas_call(
        body, name=name, out_shape=[hbm(a) for a in srcs] + [hbm(a) for a in lands],
        in_specs=[_HBM] * (2 * n) + [_SEM, _SEM, pl.BlockSpec(memory_space=pl.ANY)],
        out_specs=[_HBM] * (2 * n),
        input_output_aliases={t: t for t in range(2 * n)},
        compiler_params=pltpu.CompilerParams(has_side_effects=_EFFECT),
    )(*srcs, *lands, sems[0], sems[1], after)
    return res[:n], res[n:]


def _place_own(name, srcs, lands, routes):
    n = len(srcs)

    def body(*refs):
        src_refs, land_refs = refs[:n], refs[2 * n:3 * n]
        sem = refs[-1]
        me = _blk(*_my_place())
        cps = [pltpu.make_async_copy(routes[t].src(src_refs[t], me), routes[t].dst(land_refs[t], me), sem.at[t])
               for t in range(n)]
        for cp in cps:
            cp.start()
        for cp in cps:
            cp.wait()

    return pl.pallas_call(
        body, name=name, out_shape=[jax.ShapeDtypeStruct(a.shape, a.dtype) for a in lands],
        in_specs=[_HBM] * (2 * n), out_specs=[_HBM] * n,
        input_output_aliases={n + t: t for t in range(n)},
        scratch_shapes=[pltpu.SemaphoreType.DMA((n,))],
        compiler_params=pltpu.CompilerParams(has_side_effects=True),
    )(*srcs, *lands)


def _exchange(fulls, kinds, sizes, whole):
    arrays = list(fulls) + list(whole)
    nt, nf = len(arrays), len(fulls)

    def shard_shape(t):
        s = list(arrays[t].shape)
        if t < nf:
            s[{"col": 1, "row": 0, "mid": 1}[kinds[t]]] = sizes[t]
        return tuple(s)

    def body(*refs):
        ins, outs = refs[:nt], refs[nt:2 * nt]
        send_sems, recv_sems, local_sems = refs[2 * nt:]
        x, y, c = _my_place()
        me = _blk(x, y, c)

        def src(t, blk):
            return _slab(ins[t], kinds[t], blk, sizes[t]) if t < nf else ins[t]

        mine = [pltpu.make_async_copy(src(t, me), outs[t].at[me], local_sems.at[t]) for t in range(nt)]
        for cp in mine:
            cp.start()
        sent = []
        for t in range(nt):
            for k in range(1, 8):
                peer = (x ^ (k >> 2), y ^ ((k >> 1) & 1), c ^ (k & 1))
                pb = _blk(*peer)
                cp = pltpu.make_async_remote_copy(
                    src_ref=src(t, pb), dst_ref=outs[t].at[me],
                    send_sem=send_sems.at[7 * t + k - 1], recv_sem=recv_sems.at[7 * t + k - 1],
                    device_id=peer, device_id_type=_MESH)
                cp.start()
                sent.append((cp, t, k, pb))
        for cp, t, k, pb in sent:
            pltpu.make_async_remote_copy(
                src_ref=src(t, pb), dst_ref=outs[t].at[pb],
                send_sem=send_sems.at[7 * t + k - 1], recv_sem=recv_sems.at[7 * t + k - 1],
                device_id=(x, y, c), device_id_type=_MESH).wait_recv()
        for cp, _, _, _ in sent:
            cp.wait_send()
        for cp in mine:
            cp.wait()

    return pl.pallas_call(
        body, name="exchange_grads",
        in_specs=[_HBM] * nt, out_specs=[_HBM] * nt,
        out_shape=[jax.ShapeDtypeStruct((_N_DEV,) + shard_shape(t), arrays[t].dtype) for t in range(nt)],
        scratch_shapes=[pltpu.SemaphoreType.DMA((7 * nt,)), pltpu.SemaphoreType.DMA((7 * nt,)), pltpu.SemaphoreType.DMA((nt,))],
        compiler_params=pltpu.CompilerParams(has_side_effects=True),
    )(*arrays)


def _sum_parts(name, parts):
    n, R, C = parts.shape
    tr = _tile(R, 256, 8)

    def body(p_ref, o_ref):
        g = p_ref[0].astype(f32)
        for s in range(1, n):
            g = g + p_ref[s].astype(f32)
        o_ref[...] = g

    return pl.pallas_call(
        body, name=name, grid=(R // tr,),
        in_specs=[pl.BlockSpec((n, tr, C), lambda i: (0, i, 0))],
        out_specs=pl.BlockSpec((tr, C), lambda i: (i, 0)),
        out_shape=jax.ShapeDtypeStruct((R, C), f32),
        compiler_params=_cparams("parallel"),
    )(parts)


def _adamw(name, w, m, v, parts):
    R, C = w.shape
    n = parts.shape[0]
    tr = _tile(R, 256, 8)
    c1 = 1.0 - _ADAM_B1 ** _ADAM_STEP
    c2 = 1.0 - _ADAM_B2 ** _ADAM_STEP

    def body(w_ref, m_ref, v_ref, p_ref, g_ref, d_ref, nm_ref, nv_ref):
        g = p_ref[0].astype(f32)
        for s in range(1, n):
            g = g + p_ref[s].astype(f32)
        nm = _ADAM_B1 * m_ref[...] + (1.0 - _ADAM_B1) * g
        nv = _ADAM_B2 * v_ref[...] + (1.0 - _ADAM_B2) * (g * g)
        g_ref[...] = g
        nm_ref[...] = nm
        nv_ref[...] = nv
        d_ref[...] = -_ADAM_LR * ((nm / c1) / (jnp.sqrt(nv / c2) + _ADAM_EPS) + _ADAM_WD * w_ref[...])

    blk = pl.BlockSpec((tr, C), lambda i: (i, 0))
    sd = jax.ShapeDtypeStruct((R, C), f32)
    return pl.pallas_call(
        body, name=name, grid=(R // tr,),
        in_specs=[blk, blk, blk, pl.BlockSpec((n, tr, C), lambda i: (0, i, 0))],
        out_specs=[blk, blk, blk, blk], out_shape=[sd, sd, sd, sd],
        compiler_params=_cparams("parallel"),
    )(w, m, v, parts)


def _pack(vectors):
    flat = jnp.concatenate([a.reshape(-1).astype(f32) for a in vectors])
    pad = (-flat.shape[0]) % _SMALL_PAD
    return jnp.pad(flat, (0, pad)).reshape(-1, _LANE)


def _unpack(packed, shapes):
    flat = packed.reshape(-1)
    out, o = [], 0
    for s in shapes:
        n = 1
        for d in s:
            n *= d
        out.append(flat[o:o + n].reshape(s))
        o += n
    return out


def _pad_rows8(a):
    return jnp.pad(a, ((0, 8 - a.shape[0]), (0, 0)))


def _local_step(x, target, p, get, emit):
    T, D = x.shape

    def tie(a, *tokens):
        for tok in tokens:
            if tok is not None:
                a = a + tok[0, 0]
        return a

    w_in = get("w_in", x)
    h1, proj = _in_proj(x, p["g_mix"], w_in)
    w_pool = get("w_pool", proj)
    PW = w_pool.shape[0] * w_pool.shape[1]
    y_pool = _pool_fwd(proj, w_pool, p["pool_scale"])
    lru_conv_w, w_a, w_i = get("lru_conv_w", proj), get("w_a", proj), get("w_i", proj)
    y_lru, hs = _lru_fwd(proj, D, PW, lru_conv_w, p["lru_conv_b"], w_a, p["b_a"], w_i, p["b_i"], p["lru_lambda"])
    w_pp, w_lp = get("w_pool_proj", proj), get("w_lru_proj", proj)
    pp, qq, merged = _merge_fwd(y_pool, y_lru, w_pp, w_lp, proj, p["b_gate"])
    w_out = get("w_out", proj)
    x2, h2 = _out_proj(merged, w_out, x, p["g_mlp"])
    w_up = get("w_up", x2)
    up = _mm_nn("up_proj", h2, w_up, f32)
    ffn_conv_w = get("ffn_conv_w", proj)
    z = _ffn_fwd(up, ffn_conv_w, p["ffn_conv_b"])
    w_down = get("w_down", up)
    F = w_down.shape[0]
    dx3, dx3b, loss_t, dg_final = _down_loss(z, w_down, x2, target, p["g_final"])

    gs = {"g_final": dg_final}
    tok = emit("w_down", _mm_tn("dw_down", z, dx3b, bf16, tm_want=1536))
    dz = _mm_nt("dz", dx3b, w_down, f32)
    dup, dcw_ffn, dcb_ffn = _ffn_bwd(dz, up, ffn_conv_w, tie(p["ffn_conv_b"], tok))
    gs["ffn_conv_w"] = dcw_ffn[0:3]
    gs["ffn_conv_b"] = dcb_ffn

    tm = _tile(T, _TM)
    tk = _tile(F, _TN_MAX)
    nkh = F // tk
    tkt = _tile(T, _TK_T)
    tok = emit("w_up", _matmul(
        "dw_up", "tn", (h2, dup),
        [pl.BlockSpec((tkt, D), lambda i, j, k: (k, 0)), pl.BlockSpec((None, tkt, tk), lambda i, j, k: (j // nkh, k, j % nkh))],
        jax.ShapeDtypeStruct((D, 2 * F), bf16), pl.BlockSpec((D, tk), lambda i, j, k: (0, j)),
        (1, 2 * nkh, T // tkt), (D, tk)))

    dx2, dx2b, gs["g_mlp"] = _norm_bwd_matmul(
        "dh2", dup, pl.BlockSpec((None, tm, tk), lambda i, k: (k // nkh, i, k % nkh)), 2 * nkh,
        w_up, pl.BlockSpec((D, tk), lambda i, k: (0, k)), x2, tie(p["g_mlp"], tok), dx3)

    tok = emit("w_out", _mm_tn("dw_out", merged, dx2b, bf16))
    dP, dQ, dl0, dl1, db0, db1 = _merge_bwd(dx2b, w_out, pp, qq, proj, tie(p["b_gate"], tok), PW)
    gs["b_gate"] = jnp.concatenate([db0, db1], axis=1)
    tok = emit("w_pool_proj", _mm_tn("dw_pool_proj", y_pool, dP, bf16))
    tok2 = emit("w_lru_proj", _mm_tn("dw_lru_proj", y_lru, dQ, bf16))
    dy_pool = _mm_nt("dy_pool", dP, w_pp, f32)
    dy_lru = _mm_nt("dy_lru", dQ, w_lp, f32)

    du_lru, du_gelu, dwa, dwi, dcw_lru, dvec = _lru_bwd(
        dy_lru, proj, hs, D, lru_conv_w, tie(p["lru_conv_b"], tok, tok2), w_a, p["b_a"], w_i, p["b_i"], p["lru_lambda"])
    tok = emit("w_a", dwa.astype(bf16))
    tok2 = emit("w_i", dwi.astype(bf16))
    gs["lru_conv_w"] = dcw_lru[0:4]
    gs["lru_conv_b"], gs["b_a"], gs["b_i"], gs["lru_lambda"] = dvec[0:1], dvec[1:2], dvec[2:3], dvec[3:4]
    du_pool, dwp, gs["pool_scale"] = _pool_bwd(dy_pool, proj, w_pool, tie(p["pool_scale"], tok, tok2))
    tok = emit("w_pool", dwp.astype(bf16))

    dproj = jnp.concatenate([du_pool, du_lru, du_gelu, dl0, dl1], axis=1)
    tok2 = emit("w_in", _mm_tn("dw_in", h1, dproj, bf16))
    NI = dproj.shape[1]
    tki = _tile(NI, _TN_MAX)
    grad_x, _, gs["g_mix"] = _norm_bwd_matmul(
        "dh1", dproj, pl.BlockSpec((tm, tki), lambda i, k: (i, k)), NI // tki,
        w_in, pl.BlockSpec((D, tki), lambda i, k: (0, k)), x, tie(p["g_mix"], tok, tok2), dx2)
    return loss_t[0, 0], grad_x, gs


_MATRICES = {"w_in": "col", "w_pool": "mid", "w_a": "mid", "w_i": "mid", "w_pool_proj": "col", "w_lru_proj": "row",
             "w_out": "row", "w_up": "col", "w_down": "row"}
_CONVS = ("lru_conv_w", "ffn_conv_w")
_GATHER_GROUPS = (("w_in",), ("w_pool", "lru_conv_w", "w_a", "w_i"), ("w_pool_proj", "w_lru_proj", "w_out", "ffn_conv_w"),
                  ("w_up",), ("w_down",))
_VECTORS = ("g_mix", "b_gate", "pool_scale", "lru_conv_b", "b_a", "b_i", "lru_lambda", "g_mlp", "ffn_conv_b", "g_final")
_WEIGHTS = ("g_mix", "w_in", "b_gate", "w_pool", "pool_scale", "lru_conv_w", "lru_conv_b", "w_a", "b_a", "w_i", "b_i",
            "lru_lambda", "w_pool_proj", "w_lru_proj", "w_out", "g_mlp", "w_up", "ffn_conv_w", "ffn_conv_b", "w_down", "g_final")


def _full_shape(shape, kind):
    s = list(shape)
    s[{"col": 1, "row": 0, "mid": 1}[kind]] *= _N_DEV
    return tuple(s)


def _step(x, target, w, m, v):
    x, target = x[0], target[0]
    me = _blk(*_my_place())

    axis = {"col": 1, "row": 0, "mid": 1}
    kind = dict(_MATRICES, **{n: "col" for n in _CONVS})
    shard = {n: w[n].astype(bf16) for n in _MATRICES}
    shard.update({n: _pad_rows8(w[n]) for n in _CONVS})
    order = [n for grp in _GATHER_GROUPS for n in grp]
    index = {n: t for t, n in enumerate(order)}
    groups = [[index[n] for n in grp] for grp in _GATHER_GROUPS]
    g_routes = [_Route("gather", kind[n], shard[n].shape[axis[kind[n]]]) for n in order]
    lands = [lax.empty(_full_shape(shard[n].shape, kind[n]), shard[n].dtype) for n in order]
    g_sems, g_srcs, g_lands, _ = _send_start("gather_start", [shard[n] for n in order], lands, g_routes, groups)
    gathered = {}

    def get(name, after):
        if name not in gathered:
            gi = next(i for i, grp in enumerate(_GATHER_GROUPS) if name in grp)
            ts = groups[gi]
            routes = [g_routes[t] for t in ts]
            srcs = [g_srcs[t] for t in ts]
            srcs, got = _send_wait(f"gather_wait_{gi}", srcs, [g_lands[t] for t in ts], routes, g_sems[gi], after)
            full = _place_own(f"gather_own_{gi}", srcs, got, routes)
            gathered.update(zip(_GATHER_GROUPS[gi], full))
        return gathered[name]

    sent = {}

    def emit(name, grad):
        k = _MATRICES[name]
        size = w[name].shape[axis[k]]
        route = _Route("scatter", k, size)
        shp = list(grad.shape)
        shp[axis[k]] = size
        land = lax.empty((_N_DEV, *shp), grad.dtype)
        sems, srcs, lnds, token = _send_start("grad_start_" + name, [grad], [land], [route], [[0]])
        sent[name] = (srcs, lnds, [route], sems[0])
        return token

    p = {n: w[n].reshape(1, -1) for n in _VECTORS}
    loss_t, grad_x, gs = _local_step(x, target, p, get, emit)
    loss = lax.psum(loss_t, ("x", "y", "c"))

    small_names = list(_VECTORS) + list(_CONVS)
    small_shapes = [tuple(gs[n].shape) for n in small_names]
    small_parts = _exchange([], [], [], [_pack([gs[n] for n in small_names])])[0]

    out = {}
    mats = list(_MATRICES)
    for n in mats:
        srcs, lnds, routes, sems = sent[n]
        srcs, got = _send_wait("grad_wait_" + n, srcs, lnds, routes, sems, grad_x)
        parts = _place_own("grad_own_" + n, srcs, got, routes)[0]
        shp = w[n].shape
        r2 = (-1, shp[-1])
        res = _adamw("adamw_" + n, w[n].reshape(r2), m[n].reshape(r2), v[n].reshape(r2),
                     parts.reshape((_N_DEV,) + w[n].reshape(r2).shape))
        out[n] = [a.reshape(shp) for a in res]
    gsum = _unpack(_sum_parts("sum_small", small_parts), small_shapes)
    gsmall = dict(zip(small_names, gsum))
    for n in _CONVS:
        cols = w[n].shape[1]
        gsmall[n] = lax.dynamic_slice_in_dim(gsmall[n], me * cols, cols, axis=1)
    pk = lambda d: _pack([d[n] for n in small_names])
    res = _adamw("adamw_small", pk(w), pk(m), pk(v), pk(gsmall)[None])
    shapes = [tuple(w[n].shape) for n in small_names]
    for k, arr in enumerate(res):
        for n, a in zip(small_names, _unpack(arr, shapes)):
            out.setdefault(n, [None] * 4)[k] = a
    return loss, grad_x[None], out


def kernel(x, g_mix, w_in, b_gate, w_pool, pool_scale, lru_conv_w, lru_conv_b, w_a, b_a, w_i, b_i, lru_lambda, w_pool_proj, w_lru_proj, w_out, g_mlp, w_up, ffn_conv_w, ffn_conv_b, w_down, g_final, loss_target, m_g_mix, m_w_in, m_b_gate, m_w_pool, m_pool_scale, m_lru_conv_w, m_lru_conv_b, m_w_a, m_b_a, m_w_i, m_b_i, m_lru_lambda, m_w_pool_proj, m_w_lru_proj, m_w_out, m_g_mlp, m_w_up, m_ffn_conv_w, m_ffn_conv_b, m_w_down, m_g_final, v_g_mix, v_w_in, v_b_gate, v_w_pool, v_pool_scale, v_lru_conv_w, v_lru_conv_b, v_w_a, v_b_a, v_w_i, v_b_i, v_lru_lambda, v_w_pool_proj, v_w_lru_proj, v_w_out, v_g_mlp, v_w_up, v_ffn_conv_w, v_ffn_conv_b, v_w_down, v_g_final):
    given = dict(locals())
    orig = {n: given[n].shape for n in _WEIGHTS}

    def squeeze(a, n):
        return a if n == "g_final" else a[0]

    w = {n: squeeze(given[n], n) for n in _WEIGHTS}
    m = {n: squeeze(given["m_" + n], n) for n in _WEIGHTS}
    v = {n: squeeze(given["v_" + n], n) for n in _WEIGHTS}
    for d in (w, m, v):
        d["g_final"] = d["g_final"].reshape(1, -1)
    loss, grad_x, out = _step(x, loss_target, w, m, v)
    res = [loss, grad_x]
    for k in range(4):
        res += [out[n][k].reshape(orig[n]) for n in _WEIGHTS]
    return tuple(res)
```

```python
import functools

import jax
import jax.numpy as jnp
from jax import lax
from jax.experimental import pallas as pl
from jax.experimental.pallas import tpu as pltpu

f32 = jnp.float32
bf16 = jnp.bfloat16

_EPS = 1e-6
_LRU_C = 8.0
_POOL_WINDOWS = (2, 4, 8, 16)
_POOL_HALO = 16
_CONV_HALO = 8
_GELU_C0 = 0.7978845608028654
_GELU_C1 = 0.044715
_ADAM_LR, _ADAM_B1, _ADAM_B2, _ADAM_EPS, _ADAM_WD, _ADAM_STEP = 0.001, 0.9, 0.999, 1e-08, 0.01, 10
_N_DEV = 8
_LANE = 128
_SMALL_PAD = 8 * _LANE
_VMEM_LIMIT = 60 * 1024 * 1024

_TM = 512
_TM_SMALL = 256
_TC = 256
_TK_T = 512
_TN_MAX = 1536
_CW = 1024
_TK_DOWN = 768


def _cparams(*sem):
    return pltpu.CompilerParams(dimension_semantics=tuple(sem), vmem_limit_bytes=_VMEM_LIMIT)


def _tile(n, want, mult=1):
    if n <= want:
        return n
    t = want - want % mult
    while n % t:
        t -= mult
    return t


def _gelu(x):
    return 0.5 * x * (1.0 + jnp.tanh(_GELU_C0 * (x + _GELU_C1 * (x * x * x))))


def _gelu_grad(x):
    x2 = x * x
    t = jnp.tanh(_GELU_C0 * (x + _GELU_C1 * (x2 * x)))
    return 0.5 * (1.0 + t) + 0.5 * x * (1.0 - t * t) * (_GELU_C0 * (1.0 + 3.0 * _GELU_C1 * x2))


def _gelu_both(x):
    x2 = x * x
    t = jnp.tanh(_GELU_C0 * (x + _GELU_C1 * (x2 * x)))
    h = 0.5 * (1.0 + t)
    return x * h, h + 0.5 * x * (1.0 - t * t) * (_GELU_C0 * (1.0 + 3.0 * _GELU_C1 * x2))


def _fold8(x):
    out = x[0:8]
    for r in range(8, x.shape[0], 8):
        out = out + x[r:r + 8]
    return out


def _sigmoid(x):
    return jax.nn.sigmoid(x)


def _dot(a, b):
    return jnp.dot(a, b, preferred_element_type=f32)


def _dot_nt(a, b):
    return lax.dot_general(a, b, (((1,), (1,)), ((), ())), preferred_element_type=f32)


def _dot_tn(a, b):
    return lax.dot_general(a, b, (((0,), (0,)), ((), ())), preferred_element_type=f32)


def _rms_fwd(xf, g):
    r = lax.rsqrt(jnp.mean(xf * xf, axis=-1, keepdims=True) + _EPS)
    return xf * r * g


def _rms_bwd(xf, g, dy):
    r = lax.rsqrt(jnp.mean(xf * xf, axis=-1, keepdims=True) + _EPS)
    gy = dy * g
    dx = r * gy - xf * ((r * r * r) * jnp.mean(xf * gy, axis=-1, keepdims=True))
    dg = jnp.sum(dy * (xf * r), axis=0, keepdims=True)
    return dx, dg


def _matmul(name, mode, operands, in_specs, out_shape, out_spec, grid, acc_shape):
    dot = {"nn": _dot, "nt": _dot_nt, "tn": _dot_tn}[mode]
    nk = grid[2]

    def body_whole(a_ref, b_ref, o_ref):
        o_ref[...] = dot(a_ref[...], b_ref[...]).astype(o_ref.dtype)

    def body(a_ref, b_ref, o_ref, acc_ref):
        k = pl.program_id(2)

        @pl.when(k == 0)
        def _():
            acc_ref[...] = dot(a_ref[...], b_ref[...])

        @pl.when((k > 0) & (k < nk - 1))
        def _():
            acc_ref[...] += dot(a_ref[...], b_ref[...])

        @pl.when(k == nk - 1)
        def _():
            o_ref[...] = (acc_ref[...] + dot(a_ref[...], b_ref[...])).astype(o_ref.dtype)

    return pl.pallas_call(
        body_whole if nk == 1 else body, name=name, grid=grid, in_specs=in_specs, out_specs=out_spec, out_shape=out_shape,
        scratch_shapes=[] if nk == 1 else [pltpu.VMEM(acc_shape, f32)],
        compiler_params=_cparams("parallel", "parallel", "arbitrary"),
    )(*operands)


def _mm_nn(name, a, b, out_dtype, tm_want=None):
    M, K = a.shape
    N = b.shape[1]
    tm, tn = _tile(M, tm_want or _TM), _tile(N, _TN_MAX)
    return _matmul(
        name, "nn", (a, b),
        [pl.BlockSpec((tm, K), lambda i, j, k: (i, 0)), pl.BlockSpec((K, tn), lambda i, j, k: (0, j))],
        jax.ShapeDtypeStruct((M, N), out_dtype), pl.BlockSpec((tm, tn), lambda i, j, k: (i, j)),
        (M // tm, N // tn, 1), (tm, tn))


def _mm_nt(name, a, b, out_dtype):
    M, K = a.shape
    N = b.shape[0]
    tm, tn = _tile(M, _TM), _tile(N, _TN_MAX)
    return _matmul(
        name, "nt", (a, b),
        [pl.BlockSpec((tm, K), lambda i, j, k: (i, 0)), pl.BlockSpec((tn, K), lambda i, j, k: (j, 0))],
        jax.ShapeDtypeStruct((M, N), out_dtype), pl.BlockSpec((tm, tn), lambda i, j, k: (i, j)),
        (M // tm, N // tn, 1), (tm, tn))


def _mm_tn(name, a, b, out_dtype, tm_want=2048):
    T, M = a.shape
    N = b.shape[1]
    tm, tn, tk = _tile(M, tm_want), _tile(N, _TN_MAX), _tile(T, _TK_T)
    return _matmul(
        name, "tn", (a, b),
        [pl.BlockSpec((tk, tm), lambda i, j, k: (k, i)), pl.BlockSpec((tk, tn), lambda i, j, k: (k, j))],
        jax.ShapeDtypeStruct((M, N), out_dtype), pl.BlockSpec((tm, tn), lambda i, j, k: (i, j)),
        (M // tm, N // tn, T // tk), (tm, tn))


def _in_proj(x, g_mix, w_in):
    T, D = x.shape
    NI = w_in.shape[1]
    tm, tn = _tile(T, _TM), _tile(NI, _TN_MAX)

    def body(x_ref, g_ref, w_ref, h_ref, o_ref, h_scr):
        @pl.when(pl.program_id(1) == 0)
        def _():
            h = _rms_fwd(x_ref[...], g_ref[...]).astype(bf16)
            h_scr[...] = h
            h_ref[...] = h

        o_ref[...] = _dot(h_scr[...], w_ref[...])

    return pl.pallas_call(
        body, name="in_proj", grid=(T // tm, NI // tn),
        in_specs=[pl.BlockSpec((tm, D), lambda i, j: (i, 0)), pl.BlockSpec((1, D), lambda i, j: (0, 0)),
                  pl.BlockSpec((D, tn), lambda i, j: (0, j))],
        out_specs=[pl.BlockSpec((tm, D), lambda i, j: (i, 0)), pl.BlockSpec((tm, tn), lambda i, j: (i, j))],
        out_shape=[jax.ShapeDtypeStruct((T, D), bf16), jax.ShapeDtypeStruct((T, NI), f32)],
        scratch_shapes=[pltpu.VMEM((tm, D), bf16)],
        compiler_params=_cparams("parallel", "arbitrary"),
    )(x, g_mix, w_in)


def _window_means(ext_ref, row0, tc, gw):
    H = _POOL_HALO
    t_glob = row0 + lax.broadcasted_iota(jnp.int32, (tc, 1), 0)
    out = []
    for g, w in enumerate(_POOL_WINDOWS):
        s = ext_ref[:, g * gw:(g + 1) * gw]
        st = 1
        while st < w:
            s = s + pltpu.roll(s, st, 0)
            st *= 2
        cnt = jnp.minimum(t_glob + 1, w).astype(f32)
        out.append((s[H:, :] / cnt, ext_ref[pl.ds(H, tc), g * gw:(g + 1) * gw]))
    return out


def _pool_fwd(proj, w_pool, pool_scale):
    T = proj.shape[0]
    G, gw, _ = w_pool.shape
    PW = G * gw
    tc = _tile(T, _TC)
    H = _POOL_HALO

    def body(u_ref, w_ref, s_ref, y_ref, ext_ref):
        i = pl.program_id(0)

        @pl.when(i == 0)
        def _():
            ext_ref[pl.ds(0, H), :] = jnp.zeros((H, PW), f32)

        ext_ref[pl.ds(H, tc), :] = u_ref[...]
        for g, (m, u) in enumerate(_window_means(ext_ref, i * tc, tc, gw)):
            d = (m - u).astype(bf16)
            y = _dot(d, w_ref[g]) * s_ref[:, g * gw:(g + 1) * gw]
            y_ref[:, g * gw:(g + 1) * gw] = y.astype(bf16)
        ext_ref[pl.ds(0, H), :] = ext_ref[pl.ds(tc, H), :]

    return pl.pallas_call(
        body, name="pool_fwd", grid=(T // tc,),
        in_specs=[pl.BlockSpec((tc, PW), lambda i: (i, 0)), pl.BlockSpec((G, gw, gw), lambda i: (0, 0, 0)),
                  pl.BlockSpec((1, PW), lambda i: (0, 0))],
        out_specs=pl.BlockSpec((tc, PW), lambda i: (i, 0)),
        out_shape=jax.ShapeDtypeStruct((T, PW), bf16),
        scratch_shapes=[pltpu.VMEM((H + tc, PW), f32)],
        compiler_params=_cparams("arbitrary"),
    )(proj, w_pool, pool_scale)


def _lru_gates(v, wa, wi, ba, bi, sp):
    vb = v.astype(bf16)
    r = _sigmoid(_dot(vb, wa) + ba)
    i = _sigmoid(_dot(vb, wi) + bi)
    a = jnp.exp(-_LRU_C * r * sp)
    mult = jnp.sqrt(1.0 - a * a)
    return r, i, a, mult


def _softplus(z):
    return jnp.maximum(z, 0.0) + jnp.log1p(jnp.exp(-jnp.abs(z)))


def _causal_conv(ext_ref, cw_ref, cb_ref, n, K, cols=slice(None), r0=0):
    H = _CONV_HALO
    v = cb_ref[:, cols] + cw_ref[K - 1:K, cols] * ext_ref[pl.ds(H + r0, n), cols]
    for k in range(K - 1):
        v = v + cw_ref[k:k + 1, cols] * ext_ref[pl.ds(H + r0 - (K - 1 - k), n), cols]
    return v


_ROWS = 8
_SLAB_ROWS = 16
_SLAB_COLS = 512


def _slabs(n_rows, n_cols, reverse=False):
    cg = _tile(n_cols, _SLAB_COLS)
    rr = _tile(n_rows, _SLAB_ROWS)
    starts = range(0, n_rows, rr)
    for c0 in range(0, n_cols, cg):
        for r0 in (reversed(starts) if reverse else starts):
            yield slice(c0, c0 + cg), pl.ds(r0, rr)


def _lru_fwd(proj, D, PW, conv_w, conv_b, w_a, b_a, w_i, b_i, lam):
    T = proj.shape[0]
    NB, bw, _ = w_a.shape
    K = 4
    tc = _tile(T, _TC)
    H = _CONV_HALO
    hb = D // 2
    assert PW == hb and conv_w.shape[0] == 8

    def body(u0_ref, u1_ref, g0_ref, g1_ref, cw_ref, cb_ref, wa_ref, ba_ref, wi_ref, bi_ref, lam_ref,
             y_ref, hs_ref, ext_ref, a_scr, b_scr, hc_scr):
        c = pl.program_id(0)

        @pl.when(c == 0)
        def _():
            ext_ref[pl.ds(0, H), :] = jnp.zeros((H, D), f32)
            hc_scr[...] = jnp.zeros_like(hc_scr)

        ext_ref[pl.ds(H, tc), 0:hb] = u0_ref[...]
        ext_ref[pl.ds(H, tc), hb:D] = u1_ref[...]
        sp = _softplus(-lam_ref[...])
        for b in range(NB):
            cols = slice(b * bw, (b + 1) * bw)
            v = _causal_conv(ext_ref, cw_ref, cb_ref, tc, K, cols)
            _, i, a, mult = _lru_gates(v, wa_ref[b], wi_ref[b], ba_ref[:, cols], bi_ref[:, cols], sp[:, cols])
            a_scr[:, cols] = a
            b_scr[:, cols] = mult * (i * v)

        def step(t, h):
            h = a_scr[pl.ds(t, 1), :] * h + b_scr[pl.ds(t, 1), :]
            hs_ref[pl.ds(t, 1), :] = h
            return h

        h = lax.fori_loop(0, tc, step, hc_scr[0:1, :], unroll=8)
        hc_scr[0:1, :] = h
        y_ref[:, 0:hb] = (hs_ref[:, 0:hb] * _gelu(g0_ref[...])).astype(bf16)
        y_ref[:, hb:D] = (hs_ref[:, hb:D] * _gelu(g1_ref[...])).astype(bf16)
        ext_ref[pl.ds(0, H), :] = ext_ref[pl.ds(tc, H), :]

    vec = pl.BlockSpec((1, D), lambda c: (0, 0))
    wspec = pl.BlockSpec((NB, bw, bw), lambda c: (0, 0, 0))
    return pl.pallas_call(
        body, name="lru_fwd", grid=(T // tc,),
        in_specs=[pl.BlockSpec((tc, hb), lambda c: (c, 1)), pl.BlockSpec((tc, hb), lambda c: (c, 2)),
                  pl.BlockSpec((tc, hb), lambda c: (c, 3)), pl.BlockSpec((tc, hb), lambda c: (c, 4)),
                  pl.BlockSpec((8, D), lambda c: (0, 0)), vec, wspec, vec, wspec, vec, vec],
        out_specs=[pl.BlockSpec((tc, D), lambda c: (c, 0)), pl.BlockSpec((tc, D), lambda c: (c, 0))],
        out_shape=[jax.ShapeDtypeStruct((T, D), bf16), jax.ShapeDtypeStruct((T, D), f32)],
        scratch_shapes=[pltpu.VMEM((H + tc, D), f32), pltpu.VMEM((tc, D), f32), pltpu.VMEM((tc, D), f32),
                        pltpu.VMEM((8, D), f32)],
        compiler_params=_cparams("arbitrary"),
    )(proj, proj, proj, proj, conv_w, conv_b, w_a, b_a, w_i, b_i, lam)


def _merge_fwd(y_pool, y_lru, w_pp, w_lp, proj, b_gate):
    T, PW = y_pool.shape
    D = y_lru.shape[1]
    tm, tn = _tile(T, _TM), _tile(D, PW)
    nj = D // tn
    off = (PW + 2 * D) // tn

    def body(yp_ref, yl_ref, wp_ref, wl_ref, l0_ref, l1_ref, b0_ref, b1_ref, p_ref, q_ref, m_ref):
        p = _dot(yp_ref[...], wp_ref[...])
        q = _dot(yl_ref[...], wl_ref[...])
        p_ref[...] = p
        q_ref[...] = q
        g0 = _sigmoid(l0_ref[...] + b0_ref[...])
        g1 = _sigmoid(l1_ref[...] + b1_ref[...])
        m_ref[...] = (g0 * p + g1 * q).astype(bf16)

    tile = pl.BlockSpec((tm, tn), lambda i, j: (i, j))
    return pl.pallas_call(
        body, name="merge_fwd", grid=(T // tm, nj),
        in_specs=[pl.BlockSpec((tm, PW), lambda i, j: (i, 0)), pl.BlockSpec((tm, D), lambda i, j: (i, 0)),
                  pl.BlockSpec((PW, tn), lambda i, j: (0, j)), pl.BlockSpec((D, tn), lambda i, j: (0, j)),
                  pl.BlockSpec((tm, tn), lambda i, j: (i, off + j)), pl.BlockSpec((tm, tn), lambda i, j: (i, off + nj + j)),
                  pl.BlockSpec((1, tn), lambda i, j: (0, j)), pl.BlockSpec((1, tn), lambda i, j: (0, nj + j))],
        out_specs=[tile, tile, tile],
        out_shape=[jax.ShapeDtypeStruct((T, D), f32), jax.ShapeDtypeStruct((T, D), f32), jax.ShapeDtypeStruct((T, D), bf16)],
        compiler_params=_cparams("parallel", "arbitrary"),
    )(y_pool, y_lru, w_pp, w_lp, proj, proj, b_gate, b_gate)


def _out_proj(merged, w_out, x, g_mlp):
    T, D = x.shape
    tm = _tile(T, _TM_SMALL)

    def body(m_ref, w_ref, x_ref, g_ref, x2_ref, h2_ref):
        x2 = x_ref[...] + _dot(m_ref[...], w_ref[...])
        x2_ref[...] = x2
        h2_ref[...] = _rms_fwd(x2, g_ref[...]).astype(bf16)

    row = pl.BlockSpec((tm, D), lambda i: (i, 0))
    return pl.pallas_call(
        body, name="out_proj", grid=(T // tm,),
        in_specs=[row, pl.BlockSpec((D, D), lambda i: (0, 0)), row, pl.BlockSpec((1, D), lambda i: (0, 0))],
        out_specs=[row, row],
        out_shape=[jax.ShapeDtypeStruct((T, D), f32), jax.ShapeDtypeStruct((T, D), bf16)],
        compiler_params=_cparams("parallel"),
    )(merged, w_out, x, g_mlp)


def _ffn_fwd(up, conv_w, conv_b):
    T, F2 = up.shape
    F = F2 // 2
    K = 3
    tc = _tile(T, 2 * _TC)
    cw = _tile(F, _CW)
    ns = F // cw
    H = _CONV_HALO

    def body(gp_ref, val_ref, cw_ref, cb_ref, z_ref, ext_ref):
        @pl.when(pl.program_id(1) == 0)
        def _():
            ext_ref[pl.ds(0, H), :] = jnp.zeros((H, cw), f32)

        ext_ref[pl.ds(H, tc), :] = gp_ref[...]
        for cols, rows in _slabs(tc, cw):
            c = _causal_conv(ext_ref, cw_ref, cb_ref, rows.size, K, cols, rows.start)
            z_ref[rows, cols] = (_gelu(c) * val_ref[rows, cols]).astype(bf16)
        ext_ref[pl.ds(0, H), :] = ext_ref[pl.ds(tc, H), :]

    return pl.pallas_call(
        body, name="ffn_fwd", grid=(ns, T // tc),
        in_specs=[pl.BlockSpec((tc, cw), lambda s, c: (c, s)), pl.BlockSpec((tc, cw), lambda s, c: (c, ns + s)),
                  pl.BlockSpec((8, cw), lambda s, c: (0, s)), pl.BlockSpec((1, cw), lambda s, c: (0, s))],
        out_specs=pl.BlockSpec((tc, cw), lambda s, c: (c, s)),
        out_shape=jax.ShapeDtypeStruct((T, F), bf16),
        scratch_shapes=[pltpu.VMEM((H + tc, cw), f32)],
        compiler_params=_cparams("parallel", "arbitrary"),
    )(up, up, conv_w, conv_b)


def _down_loss(z, w_down, x2, target, g_final):
    T, F = z.shape
    D = x2.shape[1]
    tm, tk = _tile(T, _TM), _tile(F, _TK_DOWN)
    nk = F // tk

    def body(z_ref, w_ref, x2_ref, t_ref, g_ref, dx_ref, dxb_ref, loss_ref, dg_ref, acc_ref):
        i, k = pl.program_id(0), pl.program_id(1)

        @pl.when(k == 0)
        def _():
            acc_ref[...] = x2_ref[...]

        @pl.when((i == 0) & (k == 0))
        def _():
            loss_ref[...] = jnp.zeros_like(loss_ref)
            dg_ref[...] = jnp.zeros_like(dg_ref)

        acc_ref[...] += _dot(z_ref[...], w_ref[...])

        @pl.when(k == nk - 1)
        def _():
            g = g_ref[...]
            sq = jnp.zeros((_ROWS, 1), f32)
            dgs = jnp.zeros((_ROWS, D), f32)
            for r0 in range(0, tm, _ROWS):
                rows = pl.ds(r0, _ROWS)
                x3 = acc_ref[rows, :]
                r = lax.rsqrt(jnp.mean(x3 * x3, axis=-1, keepdims=True) + _EPS)
                xr = x3 * r
                e = xr * g - t_ref[rows, :]
                sq = sq + jnp.sum(e * e, axis=-1, keepdims=True)
                dy = e * (1.0 / D)
                gy = dy * g
                dx = r * gy - x3 * ((r * r * r) * jnp.mean(x3 * gy, axis=-1, keepdims=True))
                dgs = dgs + dy * xr
                dx_ref[rows, :] = dx
                dxb_ref[rows, :] = dx.astype(bf16)
            loss_ref[...] += (0.5 / D) * jnp.sum(sq)
            dg_ref[...] += jnp.sum(dgs, axis=0, keepdims=True)

    row = pl.BlockSpec((tm, D), lambda i, k: (i, 0))
    vec = pl.BlockSpec((1, D), lambda i, k: (0, 0))
    return pl.pallas_call(
        body, name="down_loss", grid=(T // tm, nk),
        in_specs=[pl.BlockSpec((tm, tk), lambda i, k: (i, k)), pl.BlockSpec((tk, D), lambda i, k: (k, 0)), row, row, vec],
        out_specs=[row, row, pl.BlockSpec((8, _LANE), lambda i, k: (0, 0)), vec],
        out_shape=[jax.ShapeDtypeStruct((T, D), f32), jax.ShapeDtypeStruct((T, D), bf16),
                   jax.ShapeDtypeStruct((8, _LANE), f32), jax.ShapeDtypeStruct((1, D), f32)],
        scratch_shapes=[pltpu.VMEM((tm, D), f32)],
        compiler_params=_cparams("arbitrary", "arbitrary"),
    )(z, w_down, x2, target, g_final)


def _ffn_bwd(dz, up, conv_w, conv_b):
    T, F = dz.shape
    K = 3
    tc = _tile(T, 2 * _TC)
    cw = _tile(F, _CW)
    ns, nt = F // cw, T // tc
    H = _CONV_HALO

    def body(dz_ref, gp_ref, val_ref, gph_ref, cw_ref, cb_ref, dup_ref, dcw_ref, dcb_ref, ext_ref, dext_ref):
        j = pl.program_id(1)
        first = j == nt - 1

        @pl.when(j == 0)
        def _():
            dext_ref[pl.ds(tc, H), :] = jnp.zeros((H, cw), f32)
            dcw_ref[...] = jnp.zeros_like(dcw_ref)
            dcb_ref[...] = jnp.zeros_like(dcb_ref)

        ext_ref[pl.ds(0, H), :] = jnp.where(first, 0.0, gph_ref[...])
        ext_ref[pl.ds(H, tc), :] = gp_ref[...]
        sums, cols_of = None, None
        for cols, rows in list(_slabs(tc, cw, reverse=True)) + [(None, None)]:
            if cols != cols_of:
                if cols_of is not None:
                    dcb_ref[:, cols_of] += jnp.sum(sums[K], axis=0, keepdims=True)
                    for k in range(K):
                        dcw_ref[k:k + 1, cols_of] += jnp.sum(sums[k], axis=0, keepdims=True)
                if cols is None:
                    break
                cols_of = cols
                sums = [jnp.zeros((8, cols.stop - cols.start), f32) for _ in range(K + 1)]
            r0, n = rows.start, rows.size
            gp = ext_ref[pl.ds(H + r0, n), cols]
            c = _causal_conv(ext_ref, cw_ref, cb_ref, n, K, cols, r0)
            ge, gg = _gelu_both(c)
            dzv = dz_ref[rows, cols]
            dup_ref[1, rows, cols] = (dzv * ge).astype(bf16)
            dc = dzv * val_ref[rows, cols] * gg
            dext_ref[rows, cols] = dc
            sums[K] = sums[K] + _fold8(dc)
            dgp = cw_ref[K - 1:K, cols] * dc
            sums[K - 1] = sums[K - 1] + _fold8(gp * dc)
            for k in range(K - 1):
                sh = dext_ref[pl.ds(r0 + K - 1 - k, n), cols]
                dgp = dgp + cw_ref[k:k + 1, cols] * sh
                sums[k] = sums[k] + _fold8(gp * sh)
            dup_ref[0, rows, cols] = dgp.astype(bf16)
        dext_ref[pl.ds(tc, H), :] = dext_ref[pl.ds(0, H), :]

    hblk = tc // H
    return pl.pallas_call(
        body, name="ffn_bwd", grid=(ns, nt),
        in_specs=[pl.BlockSpec((tc, cw), lambda s, j: (nt - 1 - j, s)),
                  pl.BlockSpec((tc, cw), lambda s, j: (nt - 1 - j, s)),
                  pl.BlockSpec((tc, cw), lambda s, j: (nt - 1 - j, ns + s)),
                  pl.BlockSpec((H, cw), lambda s, j: (jnp.maximum((nt - 1 - j) * hblk - 1, 0), s)),
                  pl.BlockSpec((8, cw), lambda s, j: (0, s)), pl.BlockSpec((1, cw), lambda s, j: (0, s))],
        out_specs=[pl.BlockSpec((2, tc, cw), lambda s, j: (0, nt - 1 - j, s)),
                   pl.BlockSpec((8, cw), lambda s, j: (0, s)), pl.BlockSpec((1, cw), lambda s, j: (0, s))],
        out_shape=[jax.ShapeDtypeStruct((2, T, F), bf16), jax.ShapeDtypeStruct((8, F), f32), jax.ShapeDtypeStruct((1, F), f32)],
        scratch_shapes=[pltpu.VMEM((H + tc, cw), f32), pltpu.VMEM((tc + H, cw), f32)],
        compiler_params=_cparams("parallel", "arbitrary"),
    )(dz, up, up, up, conv_w, conv_b)


def _norm_bwd_matmul(name, a, a_spec, nk, w, w_spec, x, g, dres):
    T, D = x.shape
    tm = a_spec.block_shape[-2]

    def body(a_ref, w_ref, x_ref, g_ref, dr_ref, dx_ref, dxb_ref, dg_ref, acc_ref):
        i, k = pl.program_id(0), pl.program_id(1)

        @pl.when((i == 0) & (k == 0))
        def _():
            dg_ref[...] = jnp.zeros_like(dg_ref)

        @pl.when(k == 0)
        def _():
            acc_ref[...] = _dot_nt(a_ref[...], w_ref[...])

        @pl.when(k > 0)
        def _():
            acc_ref[...] += _dot_nt(a_ref[...], w_ref[...])

        @pl.when(k == nk - 1)
        def _():
            g = g_ref[...]
            dgs = jnp.zeros((_ROWS, D), f32)
            for r0 in range(0, tm, _ROWS):
                rows = pl.ds(r0, _ROWS)
                x = x_ref[rows, :]
                dh = acc_ref[rows, :]
                r = lax.rsqrt(jnp.mean(x * x, axis=-1, keepdims=True) + _EPS)
                gy = dh * g
                dx = dr_ref[rows, :] + (r * gy - x * ((r * r * r) * jnp.mean(x * gy, axis=-1, keepdims=True)))
                dgs = dgs + dh * (x * r)
                dx_ref[rows, :] = dx
                dxb_ref[rows, :] = dx.astype(bf16)
            dg_ref[...] += jnp.sum(dgs, axis=0, keepdims=True)

    row = pl.BlockSpec((tm, D), lambda i, k: (i, 0))
    vec = pl.BlockSpec((1, D), lambda i, k: (0, 0))
    return pl.pallas_call(
        body, name=name, grid=(T // tm, nk),
        in_specs=[a_spec, w_spec, row, vec, row],
        out_specs=[row, row, vec],
        out_shape=[jax.ShapeDtypeStruct((T, D), f32), jax.ShapeDtypeStruct((T, D), bf16), jax.ShapeDtypeStruct((1, D), f32)],
        scratch_shapes=[pltpu.VMEM((tm, D), f32)],
        compiler_params=_cparams("arbitrary", "arbitrary"),
    )(a, w, x, g, dres)


def _merge_bwd(dx2b, w_out, p, q, proj, b_gate, PW):
    T, D = p.shape
    tm, tn = _tile(T, _TM), _tile(D, PW)
    nj = D // tn
    off = (PW + 2 * D) // tn

    def body(dx_ref, w_ref, p_ref, q_ref, l0_ref, l1_ref, b0_ref, b1_ref, dp_ref, dq_ref, dl0_ref, dl1_ref, db0_ref, db1_ref):
        @pl.when(pl.program_id(1) == 0)
        def _():
            db0_ref[...] = jnp.zeros_like(db0_ref)
            db1_ref[...] = jnp.zeros_like(db1_ref)

        dm = _dot_nt(dx_ref[...], w_ref[...])
        g0 = _sigmoid(l0_ref[...] + b0_ref[...])
        g1 = _sigmoid(l1_ref[...] + b1_ref[...])
        dp_ref[...] = (g0 * dm).astype(bf16)
        dq_ref[...] = (g1 * dm).astype(bf16)
        dl0 = dm * p_ref[...] * (g0 * (1.0 - g0))
        dl1 = dm * q_ref[...] * (g1 * (1.0 - g1))
        dl0_ref[...] = dl0.astype(bf16)
        dl1_ref[...] = dl1.astype(bf16)
        db0_ref[...] += jnp.sum(dl0, axis=0, keepdims=True)
        db1_ref[...] += jnp.sum(dl1, axis=0, keepdims=True)

    tile = pl.BlockSpec((tm, tn), lambda j, i: (i, j))
    vecj = pl.BlockSpec((1, tn), lambda j, i: (0, j))
    tb = jax.ShapeDtypeStruct((T, D), bf16)
    vb = jax.ShapeDtypeStruct((1, D), f32)
    return pl.pallas_call(
        body, name="merge_bwd", grid=(nj, T // tm),
        in_specs=[pl.BlockSpec((tm, D), lambda j, i: (i, 0)), pl.BlockSpec((tn, D), lambda j, i: (j, 0)), tile, tile,
                  pl.BlockSpec((tm, tn), lambda j, i: (i, off + j)), pl.BlockSpec((tm, tn), lambda j, i: (i, off + nj + j)),
                  vecj, pl.BlockSpec((1, tn), lambda j, i: (0, nj + j))],
        out_specs=[tile, tile, tile, tile, vecj, vecj],
        out_shape=[tb, tb, tb, tb, vb, vb],
        compiler_params=_cparams("parallel", "arbitrary"),
    )(dx2b, w_out, p, q, proj, proj, b_gate, b_gate)


def _lru_bwd(dy, proj, hs, D, conv_w, conv_b, w_a, b_a, w_i, b_i, lam):
    T = dy.shape[0]
    NB, bw, _ = w_a.shape
    K = 4
    tc = _tile(T, _TC)
    nt = T // tc
    H = _CONV_HALO
    hb = D // 2

    def body(dy_ref, u0_ref, u1_ref, g0_ref, g1_ref, hs_ref, uh0_ref, uh1_ref, hh_ref,
             cw_ref, cb_ref, wa_ref, ba_ref, wi_ref, bi_ref, lam_ref,
             du_ref, dg_ref, dwa_ref, dwi_ref, dcw_ref, dvec_ref,
             ext_ref, hext_ref, dext_ref, a_scr, r_scr, i_scr, v_scr, g_scr, car_scr):
        j = pl.program_id(0)
        first = j == nt - 1

        @pl.when(j == 0)
        def _():
            dext_ref[pl.ds(tc, H), :] = jnp.zeros((H, D), f32)
            car_scr[...] = jnp.zeros_like(car_scr)
            dwa_ref[...] = jnp.zeros_like(dwa_ref)
            dwi_ref[...] = jnp.zeros_like(dwi_ref)
            dcw_ref[...] = jnp.zeros_like(dcw_ref)
            dvec_ref[...] = jnp.zeros_like(dvec_ref)

        ext_ref[pl.ds(0, H), 0:hb] = jnp.where(first, 0.0, uh0_ref[...])
        ext_ref[pl.ds(0, H), hb:D] = jnp.where(first, 0.0, uh1_ref[...])
        ext_ref[pl.ds(H, tc), 0:hb] = u0_ref[...]
        ext_ref[pl.ds(H, tc), hb:D] = u1_ref[...]
        hext_ref[pl.ds(0, H), :] = jnp.where(first, 0.0, hh_ref[...])
        hext_ref[pl.ds(H, tc), :] = hs_ref[...]
        lamv = lam_ref[...]
        sp = _softplus(-lamv)

        for b in range(NB):
            cols = slice(b * bw, (b + 1) * bw)
            v = _causal_conv(ext_ref, cw_ref, cb_ref, tc, K, cols)
            r, i, a, _ = _lru_gates(v, wa_ref[b], wi_ref[b], ba_ref[:, cols], bi_ref[:, cols], sp[:, cols])
            v_scr[:, cols] = v
            r_scr[:, cols] = r
            i_scr[:, cols] = i
            a_scr[:, cols] = a
        for half, g_ref in enumerate((g0_ref, g1_ref)):
            cols = slice(half * hb, (half + 1) * hb)
            ug = g_ref[...]
            dyv = dy_ref[:, cols]
            g_scr[:, cols] = dyv * _gelu(ug)
            dg_ref[:, cols] = (dyv * hs_ref[:, cols] * _gelu_grad(ug)).astype(bf16)

        def step(s, carry):
            t = tc - 1 - s
            g = g_scr[pl.ds(t, 1), :] + carry
            g_scr[pl.ds(t, 1), :] = g
            return a_scr[pl.ds(t, 1), :] * g

        car_scr[0:1, :] = lax.fori_loop(0, tc, step, car_scr[0:1, :], unroll=8)

        for b in range(NB):
            cols = slice(b * bw, (b + 1) * bw)
            g = g_scr[:, cols]
            v, r, i, a = v_scr[:, cols], r_scr[:, cols], i_scr[:, cols], a_scr[:, cols]
            mult = jnp.sqrt(1.0 - a * a)
            h_prev = hext_ref[pl.ds(H - 1, tc), cols]
            dmult = g * (i * v)
            di = g * mult * v
            dv = g * mult * i
            da = g * h_prev - dmult * (a / mult)
            dlog_a = da * a
            spb = sp[:, cols]
            dr = dlog_a * (-_LRU_C * spb)
            dsp = jnp.sum(dlog_a * (-_LRU_C * r), axis=0, keepdims=True)
            dpr = dr * r * (1.0 - r)
            dpi = di * i * (1.0 - i)
            dprb, dpib, vb = dpr.astype(bf16), dpi.astype(bf16), v.astype(bf16)
            dv = dv + _dot_nt(dprb, wa_ref[b]) + _dot_nt(dpib, wi_ref[b])
            dwa_ref[b] += _dot_tn(vb, dprb)
            dwi_ref[b] += _dot_tn(vb, dpib)
            dext_ref[pl.ds(0, tc), cols] = dv
            dvec_ref[0:1, cols] += jnp.sum(dv, axis=0, keepdims=True)
            dvec_ref[1:2, cols] += jnp.sum(dpr, axis=0, keepdims=True)
            dvec_ref[2:3, cols] += jnp.sum(dpi, axis=0, keepdims=True)
            dvec_ref[3:4, cols] += dsp * (-_sigmoid(-lamv[:, cols]))

        for b in range(NB):
            cols = slice(b * bw, (b + 1) * bw)
            u = ext_ref[pl.ds(H, tc), cols]
            du = jnp.zeros((tc, bw), f32)
            for k in range(K):
                sh = dext_ref[pl.ds(K - 1 - k, tc), cols]
                du = du + cw_ref[k:k + 1, cols] * sh
                dcw_ref[k:k + 1, cols] += jnp.sum(u * sh, axis=0, keepdims=True)
            du_ref[:, cols] = du.astype(bf16)
        dext_ref[pl.ds(tc, H), :] = dext_ref[pl.ds(0, H), :]

    hblk = tc // H
    rev = lambda j: nt - 1 - j
    halo = lambda j: jnp.maximum((nt - 1 - j) * hblk - 1, 0)
    vec = pl.BlockSpec((1, D), lambda j: (0, 0))
    wspec = pl.BlockSpec((NB, bw, bw), lambda j: (0, 0, 0))
    acc8 = pl.BlockSpec((8, D), lambda j: (0, 0))
    big = pltpu.VMEM((tc, D), f32)
    return pl.pallas_call(
        body, name="lru_bwd", grid=(nt,),
        in_specs=[pl.BlockSpec((tc, D), lambda j: (rev(j), 0)),
                  pl.BlockSpec((tc, hb), lambda j: (rev(j), 1)), pl.BlockSpec((tc, hb), lambda j: (rev(j), 2)),
                  pl.BlockSpec((tc, hb), lambda j: (rev(j), 3)), pl.BlockSpec((tc, hb), lambda j: (rev(j), 4)),
                  pl.BlockSpec((tc, D), lambda j: (rev(j), 0)),
                  pl.BlockSpec((H, hb), lambda j: (halo(j), 1)), pl.BlockSpec((H, hb), lambda j: (halo(j), 2)),
                  pl.BlockSpec((H, D), lambda j: (halo(j), 0)),
                  acc8, vec, wspec, vec, wspec, vec, vec],
        out_specs=[pl.BlockSpec((tc, D), lambda j: (rev(j), 0)), pl.BlockSpec((tc, D), lambda j: (rev(j), 0)),
                   wspec, wspec, acc8, acc8],
        out_shape=[jax.ShapeDtypeStruct((T, D), bf16), jax.ShapeDtypeStruct((T, D), bf16),
                   jax.ShapeDtypeStruct((NB, bw, bw), f32), jax.ShapeDtypeStruct((NB, bw, bw), f32),
                   jax.ShapeDtypeStruct((8, D), f32), jax.ShapeDtypeStruct((8, D), f32)],
        scratch_shapes=[pltpu.VMEM((H + tc, D), f32), pltpu.VMEM((H + tc, D), f32), pltpu.VMEM((tc + H, D), f32),
                        big, big, big, big, big, pltpu.VMEM((8, D), f32)],
        compiler_params=_cparams("arbitrary"),
    )(dy, proj, proj, proj, proj, hs, proj, proj, hs, conv_w, conv_b, w_a, b_a, w_i, b_i, lam)


def _pool_bwd(dy, proj, w_pool, pool_scale):
    T, PW = dy.shape
    G, gw, _ = w_pool.shape
    tc = _tile(T, _TC)
    nt = T // tc
    H = _POOL_HALO

    def body(dy_ref, u_ref, uh_ref, w_ref, s_ref, du_ref, dw_ref, ds_ref, ext_ref, eext_ref):
        j = pl.program_id(0)
        first = j == nt - 1
        row0 = (nt - 1 - j) * tc

        @pl.when(j == 0)
        def _():
            eext_ref[pl.ds(tc, H), :] = jnp.zeros((H, PW), f32)
            dw_ref[...] = jnp.zeros_like(dw_ref)
            ds_ref[...] = jnp.zeros_like(ds_ref)

        ext_ref[pl.ds(0, H), :] = jnp.where(first, 0.0, uh_ref[...])
        ext_ref[pl.ds(H, tc), :] = u_ref[...]
        t_glob = row0 + lax.broadcasted_iota(jnp.int32, (tc, 1), 0)
        dds = []
        for g, (m, u) in enumerate(_window_means(ext_ref, row0, tc, gw)):
            cols = slice(g * gw, (g + 1) * gw)
            d = (m - u).astype(bf16)
            yraw = _dot(d, w_ref[g])
            dyv = dy_ref[:, cols]
            ds_ref[:, cols] += jnp.sum(dyv * yraw, axis=0, keepdims=True)
            dyr = (dyv * s_ref[:, cols]).astype(bf16)
            dd = _dot_nt(dyr, w_ref[g])
            dw_ref[g] += _dot_tn(d, dyr)
            cnt = jnp.minimum(t_glob + 1, _POOL_WINDOWS[g]).astype(f32)
            eext_ref[pl.ds(0, tc), cols] = dd / cnt
            dds.append(dd)
        n = tc + H
        for g, w in enumerate(_POOL_WINDOWS):
            cols = slice(g * gw, (g + 1) * gw)
            s = eext_ref[:, cols]
            st = 1
            while st < w:
                s = s + pltpu.roll(s, n - st, 0)
                st *= 2
            du_ref[:, cols] = (s[0:tc, :] - dds[g]).astype(bf16)
        eext_ref[pl.ds(tc, H), :] = eext_ref[pl.ds(0, H), :]

    hblk = tc // H
    return pl.pallas_call(
        body, name="pool_bwd", grid=(nt,),
        in_specs=[pl.BlockSpec((tc, PW), lambda j: (nt - 1 - j, 0)), pl.BlockSpec((tc, PW), lambda j: (nt - 1 - j, 0)),
                  pl.BlockSpec((H, PW), lambda j: (jnp.maximum((nt - 1 - j) * hblk - 1, 0), 0)),
                  pl.BlockSpec((G, gw, gw), lambda j: (0, 0, 0)), pl.BlockSpec((1, PW), lambda j: (0, 0))],
        out_specs=[pl.BlockSpec((tc, PW), lambda j: (nt - 1 - j, 0)), pl.BlockSpec((G, gw, gw), lambda j: (0, 0, 0)),
                   pl.BlockSpec((1, PW), lambda j: (0, 0))],
        out_shape=[jax.ShapeDtypeStruct((T, PW), bf16), jax.ShapeDtypeStruct((G, gw, gw), f32), jax.ShapeDtypeStruct((1, PW), f32)],
        scratch_shapes=[pltpu.VMEM((H + tc, PW), f32), pltpu.VMEM((tc + H, PW), f32)],
        compiler_params=_cparams("arbitrary"),
    )(dy, proj, proj, w_pool, pool_scale)


_MESH = pl.DeviceIdType.MESH
_HBM = pl.BlockSpec(memory_space=pltpu.HBM)


def _slab(ref, kind, blk, n):
    start = blk * n
    if n % _LANE == 0:
        start = pl.multiple_of(start, _LANE)
    if kind == "col":
        return ref.at[:, pl.ds(start, n)]
    if kind == "row":
        return ref.at[pl.ds(start, n), :]
    if kind == "mid":
        return ref.at[:, pl.ds(start, n), :]
    raise ValueError(kind)


def _my_place():
    x, y, c = lax.axis_index("x"), lax.axis_index("y"), lax.axis_index("c")
    return x, y, c


def _blk(px, py, pc):
    return 4 * px + 2 * py + pc


_SEM = pl.BlockSpec(memory_space=pltpu.SEMAPHORE)
_EFFECT = pltpu.SideEffectType.DATAFLOW_SIDE_EFFECTING


def _peers(x, y, c):
    return [(k, (x ^ (k >> 2), y ^ ((k >> 1) & 1), c ^ (k & 1))) for k in range(1, 8)]


class _Route:
    def __init__(self, mode, kind, size):
        self.mode, self.kind, self.size = mode, kind, size

    def src(self, ref, peer_blk):
        return ref if self.mode == "gather" else _slab(ref, self.kind, peer_blk, self.size)

    def dst(self, ref, origin_blk):
        return _slab(ref, self.kind, origin_blk, self.size) if self.mode == "gather" else ref.at[origin_blk]


def _send_start(name, srcs, lands, routes, groups):
    nt, ng = len(srcs), len(groups)

    def body(*refs):
        src_refs, land_refs = refs[:nt], refs[nt:2 * nt]
        sems = refs[2 * nt:2 * nt + 2 * ng]
        token = refs[-1]
        x, y, c = _my_place()
        me = _blk(x, y, c)
        for gi, grp in enumerate(groups):
            for pos, t in enumerate(grp):
                for k, peer in _peers(x, y, c):
                    pltpu.make_async_remote_copy(
                        src_ref=routes[t].src(src_refs[t], _blk(*peer)), dst_ref=routes[t].dst(land_refs[t], me),
                        send_sem=sems[2 * gi].at[7 * pos + k - 1], recv_sem=sems[2 * gi + 1].at[7 * pos + k - 1],
                        device_id=peer, device_id_type=_MESH).start()
        token[...] = jnp.zeros_like(token)

    hbm = lambda a: pltpu.HBM(a.shape, a.dtype)
    out_shape = []
    for grp in groups:
        out_shape += [pltpu.SemaphoreType.DMA((7 * len(grp),)), pltpu.SemaphoreType.DMA((7 * len(grp),))]
    out_shape += [hbm(a) for a in srcs] + [hbm(a) for a in lands] + [jax.ShapeDtypeStruct((8, _LANE), f32)]
    res = pl.pallas_call(
        body, name=name, out_shape=out_shape,
        in_specs=[_HBM] * (2 * nt),
        out_specs=[_SEM] * (2 * ng) + [_HBM] * (2 * nt) + [pl.BlockSpec(memory_space=pltpu.VMEM)],
        input_output_aliases={t: 2 * ng + t for t in range(2 * nt)},
        compiler_params=pltpu.CompilerParams(has_side_effects=_EFFECT),
    )(*[pltpu.with_memory_space_constraint(a, pltpu.HBM) for a in list(srcs) + list(lands)])
    sems = [(res[2 * g], res[2 * g + 1]) for g in range(ng)]
    return sems, res[2 * ng:2 * ng + nt], res[2 * ng + nt:2 * ng + 2 * nt], res[-1]


def _send_wait(name, srcs, lands, routes, sems, after):
    n = len(srcs)

    def body(*refs):
        src_refs, land_refs = refs[:n], refs[n:2 * n]
        send_sems, recv_sems = refs[2 * n], refs[2 * n + 1]
        x, y, c = _my_place()
        for pos in range(n):
            for k, peer in _peers(x, y, c):
                pb = _blk(*peer)
                cp = pltpu.make_async_remote_copy(
                    src_ref=routes[pos].src(src_refs[pos], pb), dst_ref=routes[pos].dst(land_refs[pos], pb),
                    send_sem=send_sems.at[7 * pos + k - 1], recv_sem=recv_sems.at[7 * pos + k - 1],
                    device_id=peer, device_id_type=_MESH)
                cp.wait_send()
                cp.wait_recv()

    hbm = lambda a: pltpu.HBM(a.shape, a.dtype)
    res = pl.pallas_call(
        body, name=name, out_shape=[hbm(a) for a in srcs] + [hbm(a) for a in lands],
        in_specs=[_HBM] * (2 * n) + [_SEM, _SEM, pl.BlockSpec(memory_space=pl.ANY)],
        out_specs=[_HBM] * (2 * n),
        input_output_aliases={t: t for t in range(2 * n)},
        compiler_params=pltpu.CompilerParams(has_side_effects=_EFFECT),
    )(*srcs, *lands, sems[0], sems[1], after)
    return res[:n], res[n:]


def _copy_own(name, src, land, route, me):
    gather = route.mode == "gather"
    shard = src.shape if gather else land.shape[1:]
    lead = () if gather else (None,)

    if route.kind == "mid":
        grid = (1,)
        at_full = lambda i, me: (0, me[0], 0)
        at_shard = lambda i, me: (0, 0, 0)
        block = tuple(shard)
    else:
        rows, width = shard
        tr = _tile(rows, 512, 16)
        grid = (rows // tr,)
        block = (tr, width)
        if route.kind == "col":
            at_full = lambda i, me: (i, me[0])
        else:
            at_full = lambda i, me: (me[0] * grid[0] + i, 0)
        at_shard = lambda i, me: (i, 0)
    if gather:
        in_map, out_map = at_shard, at_full
    else:
        in_map, out_map = at_full, (lambda i, me: (me[0], *at_shard(i, me)))

    def body(me_ref, src_ref, land_ref, out_ref):
        out_ref[...] = src_ref[...]

    return pl.pallas_call(
        body, name=name, out_shape=jax.ShapeDtypeStruct(land.shape, land.dtype),
        grid_spec=pltpu.PrefetchScalarGridSpec(
            num_scalar_prefetch=1, grid=grid,
            in_specs=[pl.BlockSpec(block, in_map), pl.BlockSpec(memory_space=pl.ANY)],
            out_specs=pl.BlockSpec(lead + block, out_map)),
        input_output_aliases={2: 0},
        compiler_params=_cparams("arbitrary"),
    )(me, src, land)


def _place_own(name, srcs, lands, routes):
    me = _blk(*_my_place()).astype(jnp.int32).reshape(1)
    return [_copy_own(f"{name}_{t}", s, l, r, me) for t, (s, l, r) in enumerate(zip(srcs, lands, routes))]


def _exchange(fulls, kinds, sizes, whole):
    arrays = list(fulls) + list(whole)
    nt, nf = len(arrays), len(fulls)

    def shard_shape(t):
        s = list(arrays[t].shape)
        if t < nf:
            s[{"col": 1, "row": 0, "mid": 1}[kinds[t]]] = sizes[t]
        return tuple(s)

    def body(*refs):
        ins, outs = refs[:nt], refs[nt:2 * nt]
        send_sems, recv_sems, local_sems = refs[2 * nt:]
        x, y, c = _my_place()
        me = _blk(x, y, c)

        def src(t, blk):
            return _slab(ins[t], kinds[t], blk, sizes[t]) if t < nf else ins[t]

        mine = [pltpu.make_async_copy(src(t, me), outs[t].at[me], local_sems.at[t]) for t in range(nt)]
        for cp in mine:
            cp.start()
        sent = []
        for t in range(nt):
            for k in range(1, 8):
                peer = (x ^ (k >> 2), y ^ ((k >> 1) & 1), c ^ (k & 1))
                pb = _blk(*peer)
                cp = pltpu.make_async_remote_copy(
                    src_ref=src(t, pb), dst_ref=outs[t].at[me],
                    send_sem=send_sems.at[7 * t + k - 1], recv_sem=recv_sems.at[7 * t + k - 1],
                    device_id=peer, device_id_type=_MESH)
                cp.start()
                sent.append((cp, t, k, pb))
        for cp, t, k, pb in sent:
            pltpu.make_async_remote_copy(
                src_ref=src(t, pb), dst_ref=outs[t].at[pb],
                send_sem=send_sems.at[7 * t + k - 1], recv_sem=recv_sems.at[7 * t + k - 1],
                device_id=(x, y, c), device_id_type=_MESH).wait_recv()
        for cp, _, _, _ in sent:
            cp.wait_send()
        for cp in mine:
            cp.wait()

    return pl.pallas_call(
        body, name="exchange_grads",
        in_specs=[_HBM] * nt, out_specs=[_HBM] * nt,
        out_shape=[jax.ShapeDtypeStruct((_N_DEV,) + shard_shape(t), arrays[t].dtype) for t in range(nt)],
        scratch_shapes=[pltpu.SemaphoreType.DMA((7 * nt,)), pltpu.SemaphoreType.DMA((7 * nt,)), pltpu.SemaphoreType.DMA((nt,))],
        compiler_params=pltpu.CompilerParams(has_side_effects=True),
    )(*arrays)


def _sum_parts(name, parts):
    n, R, C = parts.shape
    tr = _tile(R, 256, 8)

    def body(p_ref, o_ref):
        g = p_ref[0].astype(f32)
        for s in range(1, n):
            g = g + p_ref[s].astype(f32)
        o_ref[...] = g

    return pl.pallas_call(
        body, name=name, grid=(R // tr,),
        in_specs=[pl.BlockSpec((n, tr, C), lambda i: (0, i, 0))],
        out_specs=pl.BlockSpec((tr, C), lambda i: (i, 0)),
        out_shape=jax.ShapeDtypeStruct((R, C), f32),
        compiler_params=_cparams("parallel"),
    )(parts)


def _adamw(name, w, m, v, parts):
    R, C = w.shape
    n = parts.shape[0]
    tr = _tile(R, 256, 8)
    c1 = 1.0 - _ADAM_B1 ** _ADAM_STEP
    c2 = 1.0 - _ADAM_B2 ** _ADAM_STEP

    def body(w_ref, m_ref, v_ref, p_ref, g_ref, d_ref, nm_ref, nv_ref):
        g = p_ref[0].astype(f32)
        for s in range(1, n):
            g = g + p_ref[s].astype(f32)
        nm = _ADAM_B1 * m_ref[...] + (1.0 - _ADAM_B1) * g
        nv = _ADAM_B2 * v_ref[...] + (1.0 - _ADAM_B2) * (g * g)
        g_ref[...] = g
        nm_ref[...] = nm
        nv_ref[...] = nv
        d_ref[...] = -_ADAM_LR * ((nm / c1) / (jnp.sqrt(nv / c2) + _ADAM_EPS) + _ADAM_WD * w_ref[...])

    blk = pl.BlockSpec((tr, C), lambda i: (i, 0))
    sd = jax.ShapeDtypeStruct((R, C), f32)
    return pl.pallas_call(
        body, name=name, grid=(R // tr,),
        in_specs=[blk, blk, blk, pl.BlockSpec((n, tr, C), lambda i: (0, i, 0))],
        out_specs=[blk, blk, blk, blk], out_shape=[sd, sd, sd, sd],
        compiler_params=_cparams("parallel"),
    )(w, m, v, parts)


def _pack(vectors):
    flat = jnp.concatenate([a.reshape(-1).astype(f32) for a in vectors])
    pad = (-flat.shape[0]) % _SMALL_PAD
    return jnp.pad(flat, (0, pad)).reshape(-1, _LANE)


def _unpack(packed, shapes):
    flat = packed.reshape(-1)
    out, o = [], 0
    for s in shapes:
        n = 1
        for d in s:
            n *= d
        out.append(flat[o:o + n].reshape(s))
        o += n
    return out


def _pad_rows8(a):
    return jnp.pad(a, ((0, 8 - a.shape[0]), (0, 0)))


def _local_step(x, target, p, get, emit):
    T, D = x.shape

    def tie(a, *tokens):
        for tok in tokens:
            if tok is not None:
                a = a + tok[0, 0]
        return a

    w_in = get("w_in", x)
    h1, proj = _in_proj(x, p["g_mix"], w_in)
    w_pool = get("w_pool", proj)
    PW = w_pool.shape[0] * w_pool.shape[1]
    y_pool = _pool_fwd(proj, w_pool, p["pool_scale"])
    lru_conv_w, w_a, w_i = get("lru_conv_w", proj), get("w_a", proj), get("w_i", proj)
    y_lru, hs = _lru_fwd(proj, D, PW, lru_conv_w, p["lru_conv_b"], w_a, p["b_a"], w_i, p["b_i"], p["lru_lambda"])
    w_pp, w_lp = get("w_pool_proj", proj), get("w_lru_proj", proj)
    pp, qq, merged = _merge_fwd(y_pool, y_lru, w_pp, w_lp, proj, p["b_gate"])
    w_out = get("w_out", proj)
    x2, h2 = _out_proj(merged, w_out, x, p["g_mlp"])
    w_up = get("w_up", x2)
    up = _mm_nn("up_proj", h2, w_up, f32, tm_want=2 * _TM)
    ffn_conv_w = get("ffn_conv_w", proj)
    z = _ffn_fwd(up, ffn_conv_w, p["ffn_conv_b"])
    w_down = get("w_down", up)
    F = w_down.shape[0]
    dx3, dx3b, loss_t, dg_final = _down_loss(z, w_down, x2, target, p["g_final"])

    gs = {"g_final": dg_final}
    tok = emit("w_down", _mm_tn("dw_down", z, dx3b, bf16, tm_want=1536))
    dz = _mm_nt("dz", dx3b, w_down, f32)
    dup, dcw_ffn, dcb_ffn = _ffn_bwd(dz, up, ffn_conv_w, tie(p["ffn_conv_b"], tok))
    gs["ffn_conv_w"] = dcw_ffn[0:3]
    gs["ffn_conv_b"] = dcb_ffn

    tm = _tile(T, _TM)
    tk = _tile(F, _TN_MAX)
    nkh = F // tk
    tkt = _tile(T, _TK_T)
    tok = emit("w_up", _matmul(
        "dw_up", "tn", (h2, dup),
        [pl.BlockSpec((tkt, D), lambda i, j, k: (k, 0)), pl.BlockSpec((None, tkt, tk), lambda i, j, k: (j // nkh, k, j % nkh))],
        jax.ShapeDtypeStruct((D, 2 * F), bf16), pl.BlockSpec((D, tk), lambda i, j, k: (0, j)),
        (1, 2 * nkh, T // tkt), (D, tk)))

    dx2, dx2b, gs["g_mlp"] = _norm_bwd_matmul(
        "dh2", dup, pl.BlockSpec((None, tm, tk), lambda i, k: (k // nkh, i, k % nkh)), 2 * nkh,
        w_up, pl.BlockSpec((D, tk), lambda i, k: (0, k)), x2, tie(p["g_mlp"], tok), dx3)

    tok = emit("w_out", _mm_tn("dw_out", merged, dx2b, bf16))
    dP, dQ, dl0, dl1, db0, db1 = _merge_bwd(dx2b, w_out, pp, qq, proj, tie(p["b_gate"], tok), PW)
    gs["b_gate"] = jnp.concatenate([db0, db1], axis=1)
    tok = emit("w_pool_proj", _mm_tn("dw_pool_proj", y_pool, dP, bf16))
    tok2 = emit("w_lru_proj", _mm_tn("dw_lru_proj", y_lru, dQ, bf16))
    dy_pool = _mm_nt("dy_pool", dP, w_pp, f32)
    dy_lru = _mm_nt("dy_lru", dQ, w_lp, f32)

    du_lru, du_gelu, dwa, dwi, dcw_lru, dvec = _lru_bwd(
        dy_lru, proj, hs, D, lru_conv_w, tie(p["lru_conv_b"], tok, tok2), w_a, p["b_a"], w_i, p["b_i"], p["lru_lambda"])
    tok = emit("w_a", dwa.astype(bf16))
    tok2 = emit("w_i", dwi.astype(bf16))
    gs["lru_conv_w"] = dcw_lru[0:4]
    gs["lru_conv_b"], gs["b_a"], gs["b_i"], gs["lru_lambda"] = dvec[0:1], dvec[1:2], dvec[2:3], dvec[3:4]
    du_pool, dwp, gs["pool_scale"] = _pool_bwd(dy_pool, proj, w_pool, tie(p["pool_scale"], tok, tok2))
    tok = emit("w_pool", dwp.astype(bf16))

    dproj = jnp.concatenate([du_pool, du_lru, du_gelu, dl0, dl1], axis=1)
    tok2 = emit("w_in", _mm_tn("dw_in", h1, dproj, bf16))
    NI = dproj.shape[1]
    tki = _tile(NI, _TN_MAX)
    grad_x, _, gs["g_mix"] = _norm_bwd_matmul(
        "dh1", dproj, pl.BlockSpec((tm, tki), lambda i, k: (i, k)), NI // tki,
        w_in, pl.BlockSpec((D, tki), lambda i, k: (0, k)), x, tie(p["g_mix"], tok, tok2), dx2)
    return loss_t[0, 0], grad_x, gs


_MATRICES = {"w_in": "col", "w_pool": "mid", "w_a": "mid", "w_i": "mid", "w_pool_proj": "col", "w_lru_proj": "row",
             "w_out": "row", "w_up": "col", "w_down": "row"}
_CONVS = ("lru_conv_w", "ffn_conv_w")
_GATHER_GROUPS = (("w_in",), ("w_pool", "lru_conv_w", "w_a", "w_i"), ("w_pool_proj", "w_lru_proj", "w_out", "ffn_conv_w"),
                  ("w_up",), ("w_down",))
_VECTORS = ("g_mix", "b_gate", "pool_scale", "lru_conv_b", "b_a", "b_i", "lru_lambda", "g_mlp", "ffn_conv_b", "g_final")
_WEIGHTS = ("g_mix", "w_in", "b_gate", "w_pool", "pool_scale", "lru_conv_w", "lru_conv_b", "w_a", "b_a", "w_i", "b_i",
            "lru_lambda", "w_pool_proj", "w_lru_proj", "w_out", "g_mlp", "w_up", "ffn_conv_w", "ffn_conv_b", "w_down", "g_final")


def _full_shape(shape, kind):
    s = list(shape)
    s[{"col": 1, "row": 0, "mid": 1}[kind]] *= _N_DEV
    return tuple(s)


def _step(x, target, w, m, v):
    x, target = x[0], target[0]
    me = _blk(*_my_place())

    axis = {"col": 1, "row": 0, "mid": 1}
    kind = dict(_MATRICES, **{n: "col" for n in _CONVS})
    shard = {n: w[n].astype(bf16) for n in _MATRICES}
    shard.update({n: _pad_rows8(w[n]) for n in _CONVS})
    order = [n for grp in _GATHER_GROUPS for n in grp]
    index = {n: t for t, n in enumerate(order)}
    groups = [[index[n] for n in grp] for grp in _GATHER_GROUPS]
    g_routes = [_Route("gather", kind[n], shard[n].shape[axis[kind[n]]]) for n in order]
    lands = [lax.empty(_full_shape(shard[n].shape, kind[n]), shard[n].dtype) for n in order]
    g_sems, g_srcs, g_lands, _ = _send_start("gather_start", [shard[n] for n in order], lands, g_routes, groups)
    gathered = {}

    def get(name, after):
        if name not in gathered:
            gi = next(i for i, grp in enumerate(_GATHER_GROUPS) if name in grp)
            ts = groups[gi]
            routes = [g_routes[t] for t in ts]
            srcs = [g_srcs[t] for t in ts]
            srcs, got = _send_wait(f"gather_wait_{gi}", srcs, [g_lands[t] for t in ts], routes, g_sems[gi], after)
            full = _place_own(f"gather_own_{gi}", srcs, got, routes)
            gathered.update(zip(_GATHER_GROUPS[gi], full))
        return gathered[name]

    sent = {}

    def emit(name, grad):
        k = _MATRICES[name]
        size = w[name].shape[axis[k]]
        route = _Route("scatter", k, size)
        shp = list(grad.shape)
        shp[axis[k]] = size
        land = lax.empty((_N_DEV, *shp), grad.dtype)
        sems, srcs, lnds, token = _send_start("grad_start_" + name, [grad], [land], [route], [[0]])
        sent[name] = (srcs, lnds, [route], sems[0])
        return token

    p = {n: w[n].reshape(1, -1) for n in _VECTORS}
    loss_t, grad_x, gs = _local_step(x, target, p, get, emit)
    loss = lax.psum(loss_t, ("x", "y", "c"))

    small_names = list(_VECTORS) + list(_CONVS)
    small_shapes = [tuple(gs[n].shape) for n in small_names]
    small_parts = _exchange([], [], [], [_pack([gs[n] for n in small_names])])[0]

    out = {}
    mats = list(_MATRICES)
    for n in mats:
        srcs, lnds, routes, sems = sent[n]
        srcs, got = _send_wait("grad_wait_" + n, srcs, lnds, routes, sems, grad_x)
        parts = _place_own("grad_own_" + n, srcs, got, routes)[0]
        shp = w[n].shape
        r2 = (-1, shp[-1])
        res = _adamw("adamw_" + n, w[n].reshape(r2), m[n].reshape(r2), v[n].reshape(r2),
                     parts.reshape((_N_DEV,) + w[n].reshape(r2).shape))
        out[n] = [a.reshape(shp) for a in res]
    gsum = _unpack(_sum_parts("sum_small", small_parts), small_shapes)
    gsmall = dict(zip(small_names, gsum))
    for n in _CONVS:
        cols = w[n].shape[1]
        gsmall[n] = lax.dynamic_slice_in_dim(gsmall[n], me * cols, cols, axis=1)
    pk = lambda d: _pack([d[n] for n in small_names])
    res = _adamw("adamw_small", pk(w), pk(m), pk(v), pk(gsmall)[None])
    shapes = [tuple(w[n].shape) for n in small_names]
    for k, arr in enumerate(res):
        for n, a in zip(small_names, _unpack(arr, shapes)):
            out.setdefault(n, [None] * 4)[k] = a
    return loss, grad_x[None], out


def kernel(x, g_mix, w_in, b_gate, w_pool, pool_scale, lru_conv_w, lru_conv_b, w_a, b_a, w_i, b_i, lru_lambda, w_pool_proj, w_lru_proj, w_out, g_mlp, w_up, ffn_conv_w, ffn_conv_b, w_down, g_final, loss_target, m_g_mix, m_w_in, m_b_gate, m_w_pool, m_pool_scale, m_lru_conv_w, m_lru_conv_b, m_w_a, m_b_a, m_w_i, m_b_i, m_lru_lambda, m_w_pool_proj, m_w_lru_proj, m_w_out, m_g_mlp, m_w_up, m_ffn_conv_w, m_ffn_conv_b, m_w_down, m_g_final, v_g_mix, v_w_in, v_b_gate, v_w_pool, v_pool_scale, v_lru_conv_w, v_lru_conv_b, v_w_a, v_b_a, v_w_i, v_b_i, v_lru_lambda, v_w_pool_proj, v_w_lru_proj, v_w_out, v_g_mlp, v_w_up, v_ffn_conv_w, v_ffn_conv_b, v_w_down, v_g_final):
    given = dict(locals())
    orig = {n: given[n].shape for n in _WEIGHTS}

    def squeeze(a, n):
        return a if n == "g_final" else a[0]

    w = {n: squeeze(given[n], n) for n in _WEIGHTS}
    m = {n: squeeze(given["m_" + n], n) for n in _WEIGHTS}
    v = {n: squeeze(given["v_" + n], n) for n in _WEIGHTS}
    for d in (w, m, v):
        d["g_final"] = d["g_final"].reshape(1, -1)
    loss, grad_x, out = _step(x, loss_target, w, m, v)
    res = [loss, grad_x]
    for k in range(4):
        res += [out[n][k].reshape(orig[n]) for n in _WEIGHTS]
    return tuple(res)
```

```python
import functools

import jax
import jax.numpy as jnp
from jax import lax
from jax.experimental import pallas as pl
from jax.experimental.pallas import tpu as pltpu

f32 = jnp.float32
bf16 = jnp.bfloat16

_EPS = 1e-6
_LRU_C = 8.0
_POOL_WINDOWS = (2, 4, 8, 16)
_POOL_HALO = 16
_CONV_HALO = 8
_GELU_C0 = 0.7978845608028654
_GELU_C1 = 0.044715
_ADAM_LR, _ADAM_B1, _ADAM_B2, _ADAM_EPS, _ADAM_WD, _ADAM_STEP = 0.001, 0.9, 0.999, 1e-08, 0.01, 10
_N_DEV = 8
_LANE = 128
_SMALL_PAD = 8 * _LANE
_VMEM_LIMIT = 60 * 1024 * 1024

_TM = 512
_TM_SMALL = 256
_TC = 256
_TK_T = 512
_TN_MAX = 1536
_CW = 1024
_TK_DOWN = 768


def _cparams(*sem):
    return pltpu.CompilerParams(dimension_semantics=tuple(sem), vmem_limit_bytes=_VMEM_LIMIT)


def _tile(n, want, mult=1):
    if n <= want:
        return n
    t = want - want % mult
    while n % t:
        t -= mult
    return t


def _gelu(x):
    return 0.5 * x * (1.0 + jnp.tanh(_GELU_C0 * (x + _GELU_C1 * (x * x * x))))


def _gelu_both(x):
    x2 = x * x
    t = jnp.tanh(_GELU_C0 * (x + _GELU_C1 * (x2 * x)))
    h = 0.5 * (1.0 + t)
    return x * h, h + 0.5 * x * (1.0 - t * t) * (_GELU_C0 * (1.0 + 3.0 * _GELU_C1 * x2))


def _fold8(x):
    out = x[0:8]
    for r in range(8, x.shape[0], 8):
        out = out + x[r:r + 8]
    return out


def _sigmoid(x):
    return jax.nn.sigmoid(x)


def _dot(a, b):
    return jnp.dot(a, b, preferred_element_type=f32)


def _dot_nt(a, b):
    return lax.dot_general(a, b, (((1,), (1,)), ((), ())), preferred_element_type=f32)


def _dot_tn(a, b):
    return lax.dot_general(a, b, (((0,), (0,)), ((), ())), preferred_element_type=f32)


def _rms_rows(x_ref, g_ref, h_ref, n_rows, n_cols):
    rr = _tile(n_rows, _SLAB_ROWS)
    cg = _tile(n_cols, _SLAB_COLS)
    for r0 in range(0, n_rows, rr):
        rows = pl.ds(r0, rr)
        x = x_ref[rows, :]
        r = lax.rsqrt(jnp.mean(x * x, axis=-1, keepdims=True) + _EPS)
        for c0 in range(0, n_cols, cg):
            cols = slice(c0, c0 + cg)
            h_ref[rows, cols] = (x_ref[rows, cols] * r * g_ref[:, cols]).astype(bf16)


def _matmul(name, mode, operands, in_specs, out_shape, out_spec, grid, acc_shape):
    dot = {"nn": _dot, "nt": _dot_nt, "tn": _dot_tn}[mode]
    nk = grid[2]

    def body_whole(a_ref, b_ref, o_ref):
        o_ref[...] = dot(a_ref[...], b_ref[...]).astype(o_ref.dtype)

    def body(a_ref, b_ref, o_ref, acc_ref):
        k = pl.program_id(2)

        @pl.when(k == 0)
        def _():
            acc_ref[...] = dot(a_ref[...], b_ref[...])

        @pl.when((k > 0) & (k < nk - 1))
        def _():
            acc_ref[...] += dot(a_ref[...], b_ref[...])

        @pl.when(k == nk - 1)
        def _():
            o_ref[...] = (acc_ref[...] + dot(a_ref[...], b_ref[...])).astype(o_ref.dtype)

    return pl.pallas_call(
        body_whole if nk == 1 else body, name=name, grid=grid, in_specs=in_specs, out_specs=out_spec, out_shape=out_shape,
        scratch_shapes=[] if nk == 1 else [pltpu.VMEM(acc_shape, f32)],
        compiler_params=_cparams("parallel", "parallel", "arbitrary"),
    )(*operands)


def _mm_nn(name, a, b, out_dtype, tm_want=None):
    M, K = a.shape
    N = b.shape[1]
    tm, tn = _tile(M, tm_want or _TM), _tile(N, _TN_MAX)
    return _matmul(
        name, "nn", (a, b),
        [pl.BlockSpec((tm, K), lambda i, j, k: (i, 0)), pl.BlockSpec((K, tn), lambda i, j, k: (0, j))],
        jax.ShapeDtypeStruct((M, N), out_dtype), pl.BlockSpec((tm, tn), lambda i, j, k: (i, j)),
        (M // tm, N // tn, 1), (tm, tn))


def _mm_nt(name, a, b, out_dtype):
    M, K = a.shape
    N = b.shape[0]
    tm, tn = _tile(M, _TM), _tile(N, _TN_MAX)
    return _matmul(
        name, "nt", (a, b),
        [pl.BlockSpec((tm, K), lambda i, j, k: (i, 0)), pl.BlockSpec((tn, K), lambda i, j, k: (j, 0))],
        jax.ShapeDtypeStruct((M, N), out_dtype), pl.BlockSpec((tm, tn), lambda i, j, k: (i, j)),
        (M // tm, N // tn, 1), (tm, tn))


def _mm_tn(name, a, b, out_dtype, tm_want=2048):
    T, M = a.shape
    N = b.shape[1]
    tm, tn, tk = _tile(M, tm_want), _tile(N, _TN_MAX), _tile(T, _TK_T)
    return _matmul(
        name, "tn", (a, b),
        [pl.BlockSpec((tk, tm), lambda i, j, k: (k, i)), pl.BlockSpec((tk, tn), lambda i, j, k: (k, j))],
        jax.ShapeDtypeStruct((M, N), out_dtype), pl.BlockSpec((tm, tn), lambda i, j, k: (i, j)),
        (M // tm, N // tn, T // tk), (tm, tn))


def _in_proj(x, g_mix, w_in):
    T, D = x.shape
    NI = w_in.shape[1]
    tm, tn = _tile(T, 2 * _TM), _tile(NI, _TN_MAX)

    def body(x_ref, g_ref, w_ref, h_ref, o_ref):
        @pl.when(pl.program_id(1) == 0)
        def _():
            _rms_rows(x_ref, g_ref, h_ref, tm, D)

        o_ref[...] = _dot(h_ref[...], w_ref[...])

    return pl.pallas_call(
        body, name="in_proj", grid=(T // tm, NI // tn),
        in_specs=[pl.BlockSpec((tm, D), lambda i, j: (i, 0)), pl.BlockSpec((1, D), lambda i, j: (0, 0)),
                  pl.BlockSpec((D, tn), lambda i, j: (0, j))],
        out_specs=[pl.BlockSpec((tm, D), lambda i, j: (i, 0)), pl.BlockSpec((tm, tn), lambda i, j: (i, j))],
        out_shape=[jax.ShapeDtypeStruct((T, D), bf16), jax.ShapeDtypeStruct((T, NI), f32)],
        compiler_params=_cparams("parallel", "arbitrary"),
    )(x, g_mix, w_in)


def _window_means(ext_ref, row0, tc, gw):
    H = _POOL_HALO
    t_glob = row0 + lax.broadcasted_iota(jnp.int32, (tc, 1), 0)
    out = []
    for g, w in enumerate(_POOL_WINDOWS):
        s = ext_ref[:, g * gw:(g + 1) * gw]
        st = 1
        while st < w:
            s = s + pltpu.roll(s, st, 0)
            st *= 2
        cnt = jnp.minimum(t_glob + 1, w).astype(f32)
        out.append((s[H:, :] / cnt, ext_ref[pl.ds(H, tc), g * gw:(g + 1) * gw]))
    return out


def _pool_fwd(proj, w_pool, pool_scale):
    T = proj.shape[0]
    G, gw, _ = w_pool.shape
    PW = G * gw
    tc = _tile(T, _TC)
    H = _POOL_HALO

    def body(u_ref, w_ref, s_ref, y_ref, ext_ref):
        i = pl.program_id(0)

        @pl.when(i == 0)
        def _():
            ext_ref[pl.ds(0, H), :] = jnp.zeros((H, PW), f32)

        ext_ref[pl.ds(H, tc), :] = u_ref[...]
        for g, (m, u) in enumerate(_window_means(ext_ref, i * tc, tc, gw)):
            d = (m - u).astype(bf16)
            y = _dot(d, w_ref[g]) * s_ref[:, g * gw:(g + 1) * gw]
            y_ref[:, g * gw:(g + 1) * gw] = y.astype(bf16)
        ext_ref[pl.ds(0, H), :] = ext_ref[pl.ds(tc, H), :]

    return pl.pallas_call(
        body, name="pool_fwd", grid=(T // tc,),
        in_specs=[pl.BlockSpec((tc, PW), lambda i: (i, 0)), pl.BlockSpec((G, gw, gw), lambda i: (0, 0, 0)),
                  pl.BlockSpec((1, PW), lambda i: (0, 0))],
        out_specs=pl.BlockSpec((tc, PW), lambda i: (i, 0)),
        out_shape=jax.ShapeDtypeStruct((T, PW), bf16),
        scratch_shapes=[pltpu.VMEM((H + tc, PW), f32)],
        compiler_params=_cparams("arbitrary"),
    )(proj, w_pool, pool_scale)


def _softplus(z):
    return jnp.maximum(z, 0.0) + jnp.log1p(jnp.exp(-jnp.abs(z)))


def _causal_conv(ext_ref, cw_ref, cb_ref, n, K, cols=slice(None), r0=0):
    H = _CONV_HALO
    v = cb_ref[:, cols] + cw_ref[K - 1:K, cols] * ext_ref[pl.ds(H + r0, n), cols]
    for k in range(K - 1):
        v = v + cw_ref[k:k + 1, cols] * ext_ref[pl.ds(H + r0 - (K - 1 - k), n), cols]
    return v


_ROWS = 8
_SLAB_ROWS = 16
_SLAB_COLS = 512


def _slabs(n_rows, n_cols, reverse=False):
    cg = _tile(n_cols, _SLAB_COLS)
    rr = _tile(n_rows, _SLAB_ROWS)
    starts = range(0, n_rows, rr)
    for c0 in range(0, n_cols, cg):
        for r0 in (reversed(starts) if reverse else starts):
            yield slice(c0, c0 + cg), pl.ds(r0, rr)


def _slabs_with_sums(n_rows, n_cols, n_sums, visit, flush, reverse=False):
    cg = _tile(n_cols, _SLAB_COLS)
    rr = _tile(n_rows, _SLAB_ROWS)
    starts = list(range(0, n_rows, rr))
    for c0 in range(0, n_cols, cg):
        cols = slice(c0, c0 + cg)
        sums = [jnp.zeros((8, cg), f32) for _ in range(n_sums)]
        for r0 in (reversed(starts) if reverse else starts):
            sums = visit(cols, pl.ds(r0, rr), sums)
        flush(cols, [jnp.sum(s, axis=0, keepdims=True) for s in sums])


def _lru_fwd(proj, D, PW, conv_w, conv_b, w_a, b_a, w_i, b_i, lam):
    T = proj.shape[0]
    NB, bw, _ = w_a.shape
    K = 4
    tc = _tile(T, _TC)
    H = _CONV_HALO
    hb = D // 2
    assert PW == hb and conv_w.shape[0] == 8

    def body(u0_ref, u1_ref, g0_ref, g1_ref, cw_ref, cb_ref, wa_ref, ba_ref, wi_ref, bi_ref, lam_ref,
             y_ref, hs_ref, ext_ref, a_scr, b_scr, v_scr, hc_scr):
        c = pl.program_id(0)

        @pl.when(c == 0)
        def _():
            ext_ref[pl.ds(0, H), :] = jnp.zeros((H, D), f32)
            hc_scr[...] = jnp.zeros_like(hc_scr)

        ext_ref[pl.ds(H, tc), 0:hb] = u0_ref[...]
        ext_ref[pl.ds(H, tc), hb:D] = u1_ref[...]
        sp = _softplus(-lam_ref[...])
        for cols, rows in _slabs(tc, D):
            v_scr[rows, cols] = _causal_conv(ext_ref, cw_ref, cb_ref, rows.size, K, cols, rows.start)
        for b in range(NB):
            cols = slice(b * bw, (b + 1) * bw)
            vb = v_scr[:, cols].astype(bf16)
            a_scr[:, cols] = _dot(vb, wa_ref[b])
            b_scr[:, cols] = _dot(vb, wi_ref[b])
        for cols, rows in _slabs(tc, D):
            r = _sigmoid(a_scr[rows, cols] + ba_ref[:, cols])
            i = _sigmoid(b_scr[rows, cols] + bi_ref[:, cols])
            a = jnp.exp(-_LRU_C * r * sp[:, cols])
            a_scr[rows, cols] = a
            b_scr[rows, cols] = jnp.sqrt(1.0 - a * a) * (i * v_scr[rows, cols])

        def step(t, h):
            h = a_scr[pl.ds(t, 1), :] * h + b_scr[pl.ds(t, 1), :]
            hs_ref[pl.ds(t, 1), :] = h
            return h

        h = lax.fori_loop(0, tc, step, hc_scr[0:1, :], unroll=8)
        hc_scr[0:1, :] = h
        for cols, rows in _slabs(tc, D):
            g_ref, gcols = (g0_ref, cols) if cols.start < hb else (g1_ref, slice(cols.start - hb, cols.stop - hb))
            y_ref[rows, cols] = (hs_ref[rows, cols] * _gelu(g_ref[rows, gcols])).astype(bf16)
        ext_ref[pl.ds(0, H), :] = ext_ref[pl.ds(tc, H), :]

    vec = pl.BlockSpec((1, D), lambda c: (0, 0))
    wspec = pl.BlockSpec((NB, bw, bw), lambda c: (0, 0, 0))
    return pl.pallas_call(
        body, name="lru_fwd", grid=(T // tc,),
        in_specs=[pl.BlockSpec((tc, hb), lambda c: (c, 1)), pl.BlockSpec((tc, hb), lambda c: (c, 2)),
                  pl.BlockSpec((tc, hb), lambda c: (c, 3)), pl.BlockSpec((tc, hb), lambda c: (c, 4)),
                  pl.BlockSpec((8, D), lambda c: (0, 0)), vec, wspec, vec, wspec, vec, vec],
        out_specs=[pl.BlockSpec((tc, D), lambda c: (c, 0)), pl.BlockSpec((tc, D), lambda c: (c, 0))],
        out_shape=[jax.ShapeDtypeStruct((T, D), bf16), jax.ShapeDtypeStruct((T, D), f32)],
        scratch_shapes=[pltpu.VMEM((H + tc, D), f32), pltpu.VMEM((tc, D), f32), pltpu.VMEM((tc, D), f32),
                        pltpu.VMEM((tc, D), f32), pltpu.VMEM((8, D), f32)],
        compiler_params=_cparams("arbitrary"),
    )(proj, proj, proj, proj, conv_w, conv_b, w_a, b_a, w_i, b_i, lam)


def _merge_fwd(y_pool, y_lru, w_pp, w_lp, proj, b_gate):
    T, PW = y_pool.shape
    D = y_lru.shape[1]
    tm, tn = _tile(T, _TM), _tile(D, PW)
    nj = D // tn
    off = (PW + 2 * D) // tn

    def body(yp_ref, yl_ref, wp_ref, wl_ref, l0_ref, l1_ref, b0_ref, b1_ref, p_ref, q_ref, m_ref, p_scr, q_scr):
        p_scr[...] = _dot(yp_ref[...], wp_ref[...])
        q_scr[...] = _dot(yl_ref[...], wl_ref[...])
        for cols, rows in _slabs(tm, tn):
            p, q = p_scr[rows, cols], q_scr[rows, cols]
            g0 = _sigmoid(l0_ref[rows, cols] + b0_ref[:, cols])
            g1 = _sigmoid(l1_ref[rows, cols] + b1_ref[:, cols])
            m_ref[rows, cols] = (g0 * p + g1 * q).astype(bf16)
            p_ref[rows, cols] = p.astype(bf16)
            q_ref[rows, cols] = q.astype(bf16)

    tile = pl.BlockSpec((tm, tn), lambda j, i: (i, j))
    return pl.pallas_call(
        body, name="merge_fwd", grid=(nj, T // tm),
        in_specs=[pl.BlockSpec((tm, PW), lambda j, i: (i, 0)), pl.BlockSpec((tm, D), lambda j, i: (i, 0)),
                  pl.BlockSpec((PW, tn), lambda j, i: (0, j)), pl.BlockSpec((D, tn), lambda j, i: (0, j)),
                  pl.BlockSpec((tm, tn), lambda j, i: (i, off + j)), pl.BlockSpec((tm, tn), lambda j, i: (i, off + nj + j)),
                  pl.BlockSpec((1, tn), lambda j, i: (0, j)), pl.BlockSpec((1, tn), lambda j, i: (0, nj + j))],
        out_specs=[tile, tile, tile],
        out_shape=[jax.ShapeDtypeStruct((T, D), bf16), jax.ShapeDtypeStruct((T, D), bf16), jax.ShapeDtypeStruct((T, D), bf16)],
        scratch_shapes=[pltpu.VMEM((tm, tn), f32), pltpu.VMEM((tm, tn), f32)],
        compiler_params=_cparams("parallel", "arbitrary"),
    )(y_pool, y_lru, w_pp, w_lp, proj, proj, b_gate, b_gate)


def _out_proj(merged, w_out, x, g_mlp):
    T, D = x.shape
    tm = _tile(T, _TM_SMALL)

    def body(m_ref, w_ref, x_ref, g_ref, x2_ref, h2_ref):
        x2_ref[...] = x_ref[...] + _dot(m_ref[...], w_ref[...])
        _rms_rows(x2_ref, g_ref, h2_ref, tm, D)

    row = pl.BlockSpec((tm, D), lambda i: (i, 0))
    return pl.pallas_call(
        body, name="out_proj", grid=(T // tm,),
        in_specs=[row, pl.BlockSpec((D, D), lambda i: (0, 0)), row, pl.BlockSpec((1, D), lambda i: (0, 0))],
        out_specs=[row, row],
        out_shape=[jax.ShapeDtypeStruct((T, D), f32), jax.ShapeDtypeStruct((T, D), bf16)],
        compiler_params=_cparams("parallel"),
    )(merged, w_out, x, g_mlp)


def _ffn_fwd(up, conv_w, conv_b):
    T, F2 = up.shape
    F = F2 // 2
    K = 3
    tc = _tile(T, 2 * _TC)
    cw = _tile(F, _CW)
    ns = F // cw
    H = _CONV_HALO

    def body(gp_ref, val_ref, cw_ref, cb_ref, z_ref, ext_ref):
        @pl.when(pl.program_id(1) == 0)
        def _():
            ext_ref[pl.ds(0, H), :] = jnp.zeros((H, cw), f32)

        ext_ref[pl.ds(H, tc), :] = gp_ref[...]
        for cols, rows in _slabs(tc, cw):
            c = _causal_conv(ext_ref, cw_ref, cb_ref, rows.size, K, cols, rows.start)
            z_ref[rows, cols] = (_gelu(c) * val_ref[rows, cols]).astype(bf16)
        ext_ref[pl.ds(0, H), :] = ext_ref[pl.ds(tc, H), :]

    return pl.pallas_call(
        body, name="ffn_fwd", grid=(ns, T // tc),
        in_specs=[pl.BlockSpec((tc, cw), lambda s, c: (c, s)), pl.BlockSpec((tc, cw), lambda s, c: (c, ns + s)),
                  pl.BlockSpec((8, cw), lambda s, c: (0, s)), pl.BlockSpec((1, cw), lambda s, c: (0, s))],
        out_specs=pl.BlockSpec((tc, cw), lambda s, c: (c, s)),
        out_shape=jax.ShapeDtypeStruct((T, F), bf16),
        scratch_shapes=[pltpu.VMEM((H + tc, cw), f32)],
        compiler_params=_cparams("parallel", "arbitrary"),
    )(up, up, conv_w, conv_b)


def _down_loss(z, w_down, x2, target, g_final):
    T, F = z.shape
    D = x2.shape[1]
    tm, tk = _tile(T, 2 * _TM), _tile(F, _TK_DOWN)
    nk = F // tk

    def body(z_ref, w_ref, x2_ref, t_ref, g_ref, dx_ref, dxb_ref, loss_ref, dg_ref, acc_ref):
        i, k = pl.program_id(0), pl.program_id(1)

        @pl.when(k == 0)
        def _():
            acc_ref[...] = x2_ref[...]

        @pl.when((i == 0) & (k == 0))
        def _():
            loss_ref[...] = jnp.zeros_like(loss_ref)
            dg_ref[...] = jnp.zeros_like(dg_ref)

        acc_ref[...] += _dot(z_ref[...], w_ref[...])

        @pl.when(k == nk - 1)
        def _():
            g = g_ref[...]
            sq = jnp.zeros((_ROWS, 1), f32)
            dgs = jnp.zeros((_ROWS, D), f32)
            for r0 in range(0, tm, _ROWS):
                rows = pl.ds(r0, _ROWS)
                x3 = acc_ref[rows, :]
                r = lax.rsqrt(jnp.mean(x3 * x3, axis=-1, keepdims=True) + _EPS)
                xr = x3 * r
                e = xr * g - t_ref[rows, :]
                sq = sq + jnp.sum(e * e, axis=-1, keepdims=True)
                dy = e * (1.0 / D)
                gy = dy * g
                dx = r * gy - x3 * ((r * r * r) * jnp.mean(x3 * gy, axis=-1, keepdims=True))
                dgs = dgs + dy * xr
                dx_ref[rows, :] = dx
                dxb_ref[rows, :] = dx.astype(bf16)
            loss_ref[...] += (0.5 / D) * jnp.sum(sq)
            dg_ref[...] += jnp.sum(dgs, axis=0, keepdims=True)

    row = pl.BlockSpec((tm, D), lambda i, k: (i, 0), pipeline_mode=pl.Buffered(1))
    vec = pl.BlockSpec((1, D), lambda i, k: (0, 0))
    return pl.pallas_call(
        body, name="down_loss", grid=(T // tm, nk),
        in_specs=[pl.BlockSpec((tm, tk), lambda i, k: (i, k)), pl.BlockSpec((tk, D), lambda i, k: (k, 0)), row, row, vec],
        out_specs=[row, row, pl.BlockSpec((8, _LANE), lambda i, k: (0, 0)), vec],
        out_shape=[jax.ShapeDtypeStruct((T, D), f32), jax.ShapeDtypeStruct((T, D), bf16),
                   jax.ShapeDtypeStruct((8, _LANE), f32), jax.ShapeDtypeStruct((1, D), f32)],
        scratch_shapes=[pltpu.VMEM((tm, D), f32)],
        compiler_params=_cparams("arbitrary", "arbitrary"),
    )(z, w_down, x2, target, g_final)


def _ffn_bwd(dz, up, conv_w, conv_b):
    T, F = dz.shape
    K = 3
    tc = _tile(T, 2 * _TC)
    cw = _tile(F, _CW)
    ns, nt = F // cw, T // tc
    H = _CONV_HALO

    def body(dz_ref, gp_ref, val_ref, gph_ref, cw_ref, cb_ref, dup_ref, dcw_ref, dcb_ref, ext_ref, dext_ref):
        j = pl.program_id(1)
        first = j == nt - 1

        @pl.when(j == 0)
        def _():
            dext_ref[pl.ds(tc, H), :] = jnp.zeros((H, cw), f32)
            dcw_ref[...] = jnp.zeros_like(dcw_ref)
            dcb_ref[...] = jnp.zeros_like(dcb_ref)

        ext_ref[pl.ds(0, H), :] = jnp.where(first, 0.0, gph_ref[...])
        ext_ref[pl.ds(H, tc), :] = gp_ref[...]
        def visit(cols, rows, sums):
            r0, n = rows.start, rows.size
            gp = ext_ref[pl.ds(H + r0, n), cols]
            c = _causal_conv(ext_ref, cw_ref, cb_ref, n, K, cols, r0)
            ge, gg = _gelu_both(c)
            dzv = dz_ref[rows, cols].astype(f32)
            dup_ref[1, rows, cols] = (dzv * ge).astype(bf16)
            dc = dzv * val_ref[rows, cols] * gg
            dext_ref[rows, cols] = dc
            dgp = cw_ref[K - 1:K, cols] * dc
            new = [None] * K + [sums[K] + _fold8(dc)]
            new[K - 1] = sums[K - 1] + _fold8(gp * dc)
            for k in range(K - 1):
                sh = dext_ref[pl.ds(r0 + K - 1 - k, n), cols]
                dgp = dgp + cw_ref[k:k + 1, cols] * sh
                new[k] = sums[k] + _fold8(gp * sh)
            dup_ref[0, rows, cols] = dgp.astype(bf16)
            return new

        def flush(cols, totals):
            for k in range(K):
                dcw_ref[k:k + 1, cols] += totals[k]
            dcb_ref[:, cols] += totals[K]

        _slabs_with_sums(tc, cw, K + 1, visit, flush, reverse=True)
        dext_ref[pl.ds(tc, H), :] = dext_ref[pl.ds(0, H), :]

    hblk = tc // H
    return pl.pallas_call(
        body, name="ffn_bwd", grid=(ns, nt),
        in_specs=[pl.BlockSpec((tc, cw), lambda s, j: (nt - 1 - j, s)),
                  pl.BlockSpec((tc, cw), lambda s, j: (nt - 1 - j, s)),
                  pl.BlockSpec((tc, cw), lambda s, j: (nt - 1 - j, ns + s)),
                  pl.BlockSpec((H, cw), lambda s, j: (jnp.maximum((nt - 1 - j) * hblk - 1, 0), s)),
                  pl.BlockSpec((8, cw), lambda s, j: (0, s)), pl.BlockSpec((1, cw), lambda s, j: (0, s))],
        out_specs=[pl.BlockSpec((2, tc, cw), lambda s, j: (0, nt - 1 - j, s)),
                   pl.BlockSpec((8, cw), lambda s, j: (0, s)), pl.BlockSpec((1, cw), lambda s, j: (0, s))],
        out_shape=[jax.ShapeDtypeStruct((2, T, F), bf16), jax.ShapeDtypeStruct((8, F), f32), jax.ShapeDtypeStruct((1, F), f32)],
        scratch_shapes=[pltpu.VMEM((H + tc, cw), f32), pltpu.VMEM((tc + H, cw), f32)],
        compiler_params=_cparams("parallel", "arbitrary"),
    )(dz, up, up, up, conv_w, conv_b)


def _norm_bwd_matmul(name, a, a_spec, nk, w, w_spec, x, g, dres):
    T, D = x.shape
    tm = a_spec.block_shape[-2]

    def body(a_ref, w_ref, x_ref, g_ref, dr_ref, dx_ref, dxb_ref, dg_ref, acc_ref):
        i, k = pl.program_id(0), pl.program_id(1)

        @pl.when((i == 0) & (k == 0))
        def _():
            dg_ref[...] = jnp.zeros_like(dg_ref)

        @pl.when(k == 0)
        def _():
            acc_ref[...] = _dot_nt(a_ref[...], w_ref[...])

        @pl.when(k > 0)
        def _():
            acc_ref[...] += _dot_nt(a_ref[...], w_ref[...])

        @pl.when(k == nk - 1)
        def _():
            g = g_ref[...]
            dgs = jnp.zeros((_ROWS, D), f32)
            for r0 in range(0, tm, _ROWS):
                rows = pl.ds(r0, _ROWS)
                x = x_ref[rows, :]
                dh = acc_ref[rows, :]
                r = lax.rsqrt(jnp.mean(x * x, axis=-1, keepdims=True) + _EPS)
                gy = dh * g
                dx = dr_ref[rows, :] + (r * gy - x * ((r * r * r) * jnp.mean(x * gy, axis=-1, keepdims=True)))
                dgs = dgs + dh * (x * r)
                dx_ref[rows, :] = dx
                dxb_ref[rows, :] = dx.astype(bf16)
            dg_ref[...] += jnp.sum(dgs, axis=0, keepdims=True)

    row = pl.BlockSpec((tm, D), lambda i, k: (i, 0), pipeline_mode=pl.Buffered(1))
    vec = pl.BlockSpec((1, D), lambda i, k: (0, 0))
    return pl.pallas_call(
        body, name=name, grid=(T // tm, nk),
        in_specs=[a_spec, w_spec, row, vec, row],
        out_specs=[row, row, vec],
        out_shape=[jax.ShapeDtypeStruct((T, D), f32), jax.ShapeDtypeStruct((T, D), bf16), jax.ShapeDtypeStruct((1, D), f32)],
        scratch_shapes=[pltpu.VMEM((tm, D), f32)],
        compiler_params=_cparams("arbitrary", "arbitrary"),
    )(a, w, x, g, dres)


def _merge_bwd(dx2b, w_out, p, q, proj, b_gate, PW):
    T, D = p.shape
    tm, tn = _tile(T, _TM), _tile(D, PW)
    nj = D // tn
    off = (PW + 2 * D) // tn

    def body(dx_ref, w_ref, p_ref, q_ref, l0_ref, l1_ref, b0_ref, b1_ref, dp_ref, dq_ref, dl0_ref, dl1_ref, db0_ref, db1_ref,
             dm_ref):
        @pl.when(pl.program_id(1) == 0)
        def _():
            db0_ref[...] = jnp.zeros_like(db0_ref)
            db1_ref[...] = jnp.zeros_like(db1_ref)

        dm_ref[...] = _dot_nt(dx_ref[...], w_ref[...])

        def visit(cols, rows, sums):
            dm = dm_ref[rows, cols]
            g0 = _sigmoid(l0_ref[rows, cols] + b0_ref[:, cols])
            g1 = _sigmoid(l1_ref[rows, cols] + b1_ref[:, cols])
            dp_ref[rows, cols] = (g0 * dm).astype(bf16)
            dq_ref[rows, cols] = (g1 * dm).astype(bf16)
            dl0 = dm * p_ref[rows, cols].astype(f32) * (g0 * (1.0 - g0))
            dl1 = dm * q_ref[rows, cols].astype(f32) * (g1 * (1.0 - g1))
            dl0_ref[rows, cols] = dl0.astype(bf16)
            dl1_ref[rows, cols] = dl1.astype(bf16)
            return [sums[0] + _fold8(dl0), sums[1] + _fold8(dl1)]

        def flush(cols, totals):
            db0_ref[:, cols] += totals[0]
            db1_ref[:, cols] += totals[1]

        _slabs_with_sums(tm, tn, 2, visit, flush)

    tile = pl.BlockSpec((tm, tn), lambda j, i: (i, j))
    vecj = pl.BlockSpec((1, tn), lambda j, i: (0, j))
    tb = jax.ShapeDtypeStruct((T, D), bf16)
    vb = jax.ShapeDtypeStruct((1, D), f32)
    return pl.pallas_call(
        body, name="merge_bwd", grid=(nj, T // tm),
        in_specs=[pl.BlockSpec((tm, D), lambda j, i: (i, 0)), pl.BlockSpec((tn, D), lambda j, i: (j, 0)), tile, tile,
                  pl.BlockSpec((tm, tn), lambda j, i: (i, off + j)), pl.BlockSpec((tm, tn), lambda j, i: (i, off + nj + j)),
                  vecj, pl.BlockSpec((1, tn), lambda j, i: (0, nj + j))],
        out_specs=[tile, tile, tile, tile, vecj, vecj],
        out_shape=[tb, tb, tb, tb, vb, vb],
        scratch_shapes=[pltpu.VMEM((tm, tn), f32)],
        compiler_params=_cparams("parallel", "arbitrary"),
    )(dx2b, w_out, p, q, proj, proj, b_gate, b_gate)


def _lru_bwd(dy, proj, hs, D, conv_w, conv_b, w_a, b_a, w_i, b_i, lam):
    T = dy.shape[0]
    NB, bw, _ = w_a.shape
    K = 4
    tc = _tile(T, _TC)
    nt = T // tc
    H = _CONV_HALO
    hb = D // 2

    def body(dy_ref, u0_ref, u1_ref, g0_ref, g1_ref, hs_ref, uh0_ref, uh1_ref, hh_ref,
             cw_ref, cb_ref, wa_ref, ba_ref, wi_ref, bi_ref, lam_ref,
             du_ref, dg_ref, dwa_ref, dwi_ref, dcw_ref, dvec_ref,
             ext_ref, hext_ref, dext_ref, a_scr, r_scr, i_scr, v_scr, g_scr, car_scr):
        j = pl.program_id(0)
        first = j == nt - 1

        @pl.when(j == 0)
        def _():
            dext_ref[pl.ds(tc, H), :] = jnp.zeros((H, D), f32)
            car_scr[...] = jnp.zeros_like(car_scr)
            dwa_ref[...] = jnp.zeros_like(dwa_ref)
            dwi_ref[...] = jnp.zeros_like(dwi_ref)
            dcw_ref[...] = jnp.zeros_like(dcw_ref)
            dvec_ref[...] = jnp.zeros_like(dvec_ref)

        ext_ref[pl.ds(0, H), 0:hb] = jnp.where(first, 0.0, uh0_ref[...])
        ext_ref[pl.ds(0, H), hb:D] = jnp.where(first, 0.0, uh1_ref[...])
        ext_ref[pl.ds(H, tc), 0:hb] = u0_ref[...]
        ext_ref[pl.ds(H, tc), hb:D] = u1_ref[...]
        hext_ref[pl.ds(0, H), :] = jnp.where(first, 0.0, hh_ref[...])
        hext_ref[pl.ds(H, tc), :] = hs_ref[...]
        lamv = lam_ref[...]
        sp = _softplus(-lamv)

        for cols, rows in _slabs(tc, D):
            v_scr[rows, cols] = _causal_conv(ext_ref, cw_ref, cb_ref, rows.size, K, cols, rows.start)
        for b in range(NB):
            cols = slice(b * bw, (b + 1) * bw)
            vb = v_scr[:, cols].astype(bf16)
            r_scr[:, cols] = _dot(vb, wa_ref[b])
            i_scr[:, cols] = _dot(vb, wi_ref[b])
        for cols, rows in _slabs(tc, D):
            r = _sigmoid(r_scr[rows, cols] + ba_ref[:, cols])
            r_scr[rows, cols] = r
            i_scr[rows, cols] = _sigmoid(i_scr[rows, cols] + bi_ref[:, cols])
            a_scr[rows, cols] = jnp.exp(-_LRU_C * r * sp[:, cols])
            g_ref, gcols = (g0_ref, cols) if cols.start < hb else (g1_ref, slice(cols.start - hb, cols.stop - hb))
            ge, gg = _gelu_both(g_ref[rows, gcols])
            dyv = dy_ref[rows, cols]
            g_scr[rows, cols] = dyv * ge
            dg_ref[rows, cols] = (dyv * hs_ref[rows, cols] * gg).astype(bf16)

        def step(s, carry):
            t = tc - 1 - s
            g = g_scr[pl.ds(t, 1), :] + carry
            g_scr[pl.ds(t, 1), :] = g
            return a_scr[pl.ds(t, 1), :] * g

        car_scr[0:1, :] = lax.fori_loop(0, tc, step, car_scr[0:1, :], unroll=8)

        def gates(cols, rows, sums):
            g, v, r, i, a = g_scr[rows, cols], v_scr[rows, cols], r_scr[rows, cols], i_scr[rows, cols], a_scr[rows, cols]
            mult = jnp.sqrt(1.0 - a * a)
            h_prev = hext_ref[pl.ds(H - 1 + rows.start, rows.size), cols]
            gm = g * mult
            dext_ref[rows, cols] = gm * i
            dlog_a = (g * h_prev - g * (i * v) * (a / mult)) * a
            dpr = dlog_a * (-_LRU_C * sp[:, cols]) * (r * (1.0 - r))
            dpi = gm * v * (i * (1.0 - i))
            r_scr[rows, cols] = dpr
            i_scr[rows, cols] = dpi
            return [sums[0] + _fold8(dpr), sums[1] + _fold8(dpi), sums[2] + _fold8(dlog_a * (-_LRU_C * r))]

        def gates_flush(cols, totals):
            dvec_ref[1:2, cols] += totals[0]
            dvec_ref[2:3, cols] += totals[1]
            dvec_ref[3:4, cols] += totals[2] * (-_sigmoid(-lamv[:, cols]))

        _slabs_with_sums(tc, D, 3, gates, gates_flush)
        for b in range(NB):
            cols = slice(b * bw, (b + 1) * bw)
            dprb, dpib, vb = r_scr[:, cols].astype(bf16), i_scr[:, cols].astype(bf16), v_scr[:, cols].astype(bf16)
            dext_ref[pl.ds(0, tc), cols] += _dot_nt(dprb, wa_ref[b]) + _dot_nt(dpib, wi_ref[b])
            dwa_ref[b] += _dot_tn(vb, dprb)
            dwi_ref[b] += _dot_tn(vb, dpib)

        def conv_t(cols, rows, sums):
            r0, n = rows.start, rows.size
            u = ext_ref[pl.ds(H + r0, n), cols]
            dv = dext_ref[rows, cols]
            du = cw_ref[K - 1:K, cols] * dv
            new = [None] * K + [sums[K] + _fold8(dv)]
            new[K - 1] = sums[K - 1] + _fold8(u * dv)
            for k in range(K - 1):
                sh = dext_ref[pl.ds(r0 + K - 1 - k, n), cols]
                du = du + cw_ref[k:k + 1, cols] * sh
                new[k] = sums[k] + _fold8(u * sh)
            du_ref[rows, cols] = du.astype(bf16)
            return new

        def conv_t_flush(cols, totals):
            for k in range(K):
                dcw_ref[k:k + 1, cols] += totals[k]
            dvec_ref[0:1, cols] += totals[K]

        _slabs_with_sums(tc, D, K + 1, conv_t, conv_t_flush)
        dext_ref[pl.ds(tc, H), :] = dext_ref[pl.ds(0, H), :]

    hblk = tc // H
    rev = lambda j: nt - 1 - j
    halo = lambda j: jnp.maximum((nt - 1 - j) * hblk - 1, 0)
    vec = pl.BlockSpec((1, D), lambda j: (0, 0))
    wspec = pl.BlockSpec((NB, bw, bw), lambda j: (0, 0, 0))
    acc8 = pl.BlockSpec((8, D), lambda j: (0, 0))
    big = pltpu.VMEM((tc, D), f32)
    return pl.pallas_call(
        body, name="lru_bwd", grid=(nt,),
        in_specs=[pl.BlockSpec((tc, D), lambda j: (rev(j), 0)),
                  pl.BlockSpec((tc, hb), lambda j: (rev(j), 1)), pl.BlockSpec((tc, hb), lambda j: (rev(j), 2)),
                  pl.BlockSpec((tc, hb), lambda j: (rev(j), 3)), pl.BlockSpec((tc, hb), lambda j: (rev(j), 4)),
                  pl.BlockSpec((tc, D), lambda j: (rev(j), 0)),
                  pl.BlockSpec((H, hb), lambda j: (halo(j), 1)), pl.BlockSpec((H, hb), lambda j: (halo(j), 2)),
                  pl.BlockSpec((H, D), lambda j: (halo(j), 0)),
                  acc8, vec, wspec, vec, wspec, vec, vec],
        out_specs=[pl.BlockSpec((tc, D), lambda j: (rev(j), 0)), pl.BlockSpec((tc, D), lambda j: (rev(j), 0)),
                   wspec, wspec, acc8, acc8],
        out_shape=[jax.ShapeDtypeStruct((T, D), bf16), jax.ShapeDtypeStruct((T, D), bf16),
                   jax.ShapeDtypeStruct((NB, bw, bw), f32), jax.ShapeDtypeStruct((NB, bw, bw), f32),
                   jax.ShapeDtypeStruct((8, D), f32), jax.ShapeDtypeStruct((8, D), f32)],
        scratch_shapes=[pltpu.VMEM((H + tc, D), f32), pltpu.VMEM((H + tc, D), f32), pltpu.VMEM((tc + H, D), f32),
                        big, big, big, big, big, pltpu.VMEM((8, D), f32)],
        compiler_params=_cparams("arbitrary"),
    )(dy, proj, proj, proj, proj, hs, proj, proj, hs, conv_w, conv_b, w_a, b_a, w_i, b_i, lam)


def _pool_bwd(dy, proj, w_pool, pool_scale):
    T, PW = dy.shape
    G, gw, _ = w_pool.shape
    tc = _tile(T, _TC)
    nt = T // tc
    H = _POOL_HALO

    def body(dy_ref, u_ref, uh_ref, w_ref, s_ref, du_ref, dw_ref, ds_ref, ext_ref, eext_ref):
        j = pl.program_id(0)
        first = j == nt - 1
        row0 = (nt - 1 - j) * tc

        @pl.when(j == 0)
        def _():
            eext_ref[pl.ds(tc, H), :] = jnp.zeros((H, PW), f32)
            dw_ref[...] = jnp.zeros_like(dw_ref)
            ds_ref[...] = jnp.zeros_like(ds_ref)

        ext_ref[pl.ds(0, H), :] = jnp.where(first, 0.0, uh_ref[...])
        ext_ref[pl.ds(H, tc), :] = u_ref[...]
        t_glob = row0 + lax.broadcasted_iota(jnp.int32, (tc, 1), 0)
        dds = []
        for g, (m, u) in enumerate(_window_means(ext_ref, row0, tc, gw)):
            cols = slice(g * gw, (g + 1) * gw)
            d = (m - u).astype(bf16)
            yraw = _dot(d, w_ref[g])
            dyv = dy_ref[:, cols]
            ds_ref[:, cols] += jnp.sum(dyv * yraw, axis=0, keepdims=True)
            dyr = (dyv * s_ref[:, cols]).astype(bf16)
            dd = _dot_nt(dyr, w_ref[g])
            dw_ref[g] += _dot_tn(d, dyr)
            cnt = jnp.minimum(t_glob + 1, _POOL_WINDOWS[g]).astype(f32)
            eext_ref[pl.ds(0, tc), cols] = dd / cnt
            dds.append(dd)
        n = tc + H
        for g, w in enumerate(_POOL_WINDOWS):
            cols = slice(g * gw, (g + 1) * gw)
            s = eext_ref[:, cols]
            st = 1
            while st < w:
                s = s + pltpu.roll(s, n - st, 0)
                st *= 2
            du_ref[:, cols] = (s[0:tc, :] - dds[g]).astype(bf16)
        eext_ref[pl.ds(tc, H), :] = eext_ref[pl.ds(0, H), :]

    hblk = tc // H
    return pl.pallas_call(
        body, name="pool_bwd", grid=(nt,),
        in_specs=[pl.BlockSpec((tc, PW), lambda j: (nt - 1 - j, 0)), pl.BlockSpec((tc, PW), lambda j: (nt - 1 - j, 0)),
                  pl.BlockSpec((H, PW), lambda j: (jnp.maximum((nt - 1 - j) * hblk - 1, 0), 0)),
                  pl.BlockSpec((G, gw, gw), lambda j: (0, 0, 0)), pl.BlockSpec((1, PW), lambda j: (0, 0))],
        out_specs=[pl.BlockSpec((tc, PW), lambda j: (nt - 1 - j, 0)), pl.BlockSpec((G, gw, gw), lambda j: (0, 0, 0)),
                   pl.BlockSpec((1, PW), lambda j: (0, 0))],
        out_shape=[jax.ShapeDtypeStruct((T, PW), bf16), jax.ShapeDtypeStruct((G, gw, gw), f32), jax.ShapeDtypeStruct((1, PW), f32)],
        scratch_shapes=[pltpu.VMEM((H + tc, PW), f32), pltpu.VMEM((tc + H, PW), f32)],
        compiler_params=_cparams("arbitrary"),
    )(dy, proj, proj, w_pool, pool_scale)


_MESH = pl.DeviceIdType.MESH
_HBM = pl.BlockSpec(memory_space=pltpu.HBM)


def _slab(ref, kind, blk, n):
    start = blk * n
    if n % _LANE == 0:
        start = pl.multiple_of(start, _LANE)
    if kind == "col":
        return ref.at[:, pl.ds(start, n)]
    if kind == "row":
        return ref.at[pl.ds(start, n), :]
    if kind == "mid":
        return ref.at[:, pl.ds(start, n), :]
    raise ValueError(kind)


def _my_place():
    x, y, c = lax.axis_index("x"), lax.axis_index("y"), lax.axis_index("c")
    return x, y, c


def _blk(px, py, pc):
    return 4 * px + 2 * py + pc


_SEM = pl.BlockSpec(memory_space=pltpu.SEMAPHORE)
_EFFECT = pltpu.SideEffectType.DATAFLOW_SIDE_EFFECTING


def _peers(x, y, c):
    return [(k, (x ^ (k >> 2), y ^ ((k >> 1) & 1), c ^ (k & 1))) for k in range(1, 8)]


class _Route:
    def __init__(self, mode, kind, size):
        self.mode, self.kind, self.size = mode, kind, size

    def src(self, ref, peer_blk):
        return ref if self.mode == "gather" else _slab(ref, self.kind, peer_blk, self.size)

    def dst(self, ref, origin_blk):
        return _slab(ref, self.kind, origin_blk, self.size) if self.mode == "gather" else ref.at[origin_blk]


def _send_start(name, srcs, lands, routes, groups):
    nt, ng = len(srcs), len(groups)

    def body(*refs):
        src_refs, land_refs = refs[:nt], refs[nt:2 * nt]
        sems = refs[2 * nt:2 * nt + 2 * ng]
        token = refs[-1]
        x, y, c = _my_place()
        me = _blk(x, y, c)
        for gi, grp in enumerate(groups):
            for pos, t in enumerate(grp):
                for k, peer in _peers(x, y, c):
                    pltpu.make_async_remote_copy(
                        src_ref=routes[t].src(src_refs[t], _blk(*peer)), dst_ref=routes[t].dst(land_refs[t], me),
                        send_sem=sems[2 * gi].at[7 * pos + k - 1], recv_sem=sems[2 * gi + 1].at[7 * pos + k - 1],
                        device_id=peer, device_id_type=_MESH).start()
        token[...] = jnp.zeros_like(token)

    hbm = lambda a: pltpu.HBM(a.shape, a.dtype)
    out_shape = []
    for grp in groups:
        out_shape += [pltpu.SemaphoreType.DMA((7 * len(grp),)), pltpu.SemaphoreType.DMA((7 * len(grp),))]
    out_shape += [hbm(a) for a in srcs] + [hbm(a) for a in lands] + [jax.ShapeDtypeStruct((8, _LANE), f32)]
    res = pl.pallas_call(
        body, name=name, out_shape=out_shape,
        in_specs=[_HBM] * (2 * nt),
        out_specs=[_SEM] * (2 * ng) + [_HBM] * (2 * nt) + [pl.BlockSpec(memory_space=pltpu.VMEM)],
        input_output_aliases={t: 2 * ng + t for t in range(2 * nt)},
        compiler_params=pltpu.CompilerParams(has_side_effects=_EFFECT),
    )(*[pltpu.with_memory_space_constraint(a, pltpu.HBM) for a in list(srcs) + list(lands)])
    sems = [(res[2 * g], res[2 * g + 1]) for g in range(ng)]
    return sems, res[2 * ng:2 * ng + nt], res[2 * ng + nt:2 * ng + 2 * nt], res[-1]


def _send_wait(name, srcs, lands, routes, sems, after):
    n = len(srcs)

    def body(*refs):
        src_refs, land_refs = refs[:n], refs[n:2 * n]
        send_sems, recv_sems = refs[2 * n], refs[2 * n + 1]
        x, y, c = _my_place()
        for pos in range(n):
            for k, peer in _peers(x, y, c):
                pb = _blk(*peer)
                cp = pltpu.make_async_remote_copy(
                    src_ref=routes[pos].src(src_refs[pos], pb), dst_ref=routes[pos].dst(land_refs[pos], pb),
                    send_sem=send_sems.at[7 * pos + k - 1], recv_sem=recv_sems.at[7 * pos + k - 1],
                    device_id=peer, device_id_type=_MESH)
                cp.wait_send()
                cp.wait_recv()

    hbm = lambda a: pltpu.HBM(a.shape, a.dtype)
    res = pl.pallas_call(
        body, name=name, out_shape=[hbm(a) for a in srcs] + [hbm(a) for a in lands],
        in_specs=[_HBM] * (2 * n) + [_SEM, _SEM, pl.BlockSpec(memory_space=pl.ANY)],
        out_specs=[_HBM] * (2 * n),
        input_output_aliases={t: t for t in range(2 * n)},
        compiler_params=pltpu.CompilerParams(has_side_effects=_EFFECT),
    )(*srcs, *lands, sems[0], sems[1], after)
    return res[:n], res[n:]


def _copy_own(name, src, land, route, me):
    gather = route.mode == "gather"
    shard = src.shape if gather else land.shape[1:]
    lead = () if gather else (None,)

    if route.kind == "mid":
        grid = (1,)
        at_full = lambda i, me: (0, me[0], 0)
        at_shard = lambda i, me: (0, 0, 0)
        block = tuple(shard)
    else:
        rows, width = shard
        tr = _tile(rows, 512, 16)
        grid = (rows // tr,)
        block = (tr, width)
        if route.kind == "col":
            at_full = lambda i, me: (i, me[0])
        else:
            at_full = lambda i, me: (me[0] * grid[0] + i, 0)
        at_shard = lambda i, me: (i, 0)
    if gather:
        in_map, out_map = at_shard, at_full
    else:
        in_map, out_map = at_full, (lambda i, me: (me[0], *at_shard(i, me)))

    def body(me_ref, src_ref, land_ref, out_ref):
        out_ref[...] = src_ref[...]

    return pl.pallas_call(
        body, name=name, out_shape=jax.ShapeDtypeStruct(land.shape, land.dtype),
        grid_spec=pltpu.PrefetchScalarGridSpec(
            num_scalar_prefetch=1, grid=grid,
            in_specs=[pl.BlockSpec(block, in_map), pl.BlockSpec(memory_space=pl.ANY)],
            out_specs=pl.BlockSpec(lead + block, out_map)),
        input_output_aliases={2: 0},
        compiler_params=_cparams("arbitrary"),
    )(me, src, land)


def _place_own(name, srcs, lands, routes):
    me = _blk(*_my_place()).astype(jnp.int32).reshape(1)
    return [_copy_own(f"{name}_{t}", s, l, r, me) for t, (s, l, r) in enumerate(zip(srcs, lands, routes))]


def _exchange(fulls, kinds, sizes, whole):
    arrays = list(fulls) + list(whole)
    nt, nf = len(arrays), len(fulls)

    def shard_shape(t):
        s = list(arrays[t].shape)
        if t < nf:
            s[{"col": 1, "row": 0, "mid": 1}[kinds[t]]] = sizes[t]
        return tuple(s)

    def body(*refs):
        ins, outs = refs[:nt], refs[nt:2 * nt]
        send_sems, recv_sems, local_sems = refs[2 * nt:]
        x, y, c = _my_place()
        me = _blk(x, y, c)

        def src(t, blk):
            return _slab(ins[t], kinds[t], blk, sizes[t]) if t < nf else ins[t]

        mine = [pltpu.make_async_copy(src(t, me), outs[t].at[me], local_sems.at[t]) for t in range(nt)]
        for cp in mine:
            cp.start()
        sent = []
        for t in range(nt):
            for k in range(1, 8):
                peer = (x ^ (k >> 2), y ^ ((k >> 1) & 1), c ^ (k & 1))
                pb = _blk(*peer)
                cp = pltpu.make_async_remote_copy(
                    src_ref=src(t, pb), dst_ref=outs[t].at[me],
                    send_sem=send_sems.at[7 * t + k - 1], recv_sem=recv_sems.at[7 * t + k - 1],
                    device_id=peer, device_id_type=_MESH)
                cp.start()
                sent.append((cp, t, k, pb))
        for cp, t, k, pb in sent:
            pltpu.make_async_remote_copy(
                src_ref=src(t, pb), dst_ref=outs[t].at[pb],
                send_sem=send_sems.at[7 * t + k - 1], recv_sem=recv_sems.at[7 * t + k - 1],
                device_id=(x, y, c), device_id_type=_MESH).wait_recv()
        for cp, _, _, _ in sent:
            cp.wait_send()
        for cp in mine:
            cp.wait()

    return pl.pallas_call(
        body, name="exchange_grads",
        in_specs=[_HBM] * nt, out_specs=[_HBM] * nt,
        out_shape=[jax.ShapeDtypeStruct((_N_DEV,) + shard_shape(t), arrays[t].dtype) for t in range(nt)],
        scratch_shapes=[pltpu.SemaphoreType.DMA((7 * nt,)), pltpu.SemaphoreType.DMA((7 * nt,)), pltpu.SemaphoreType.DMA((nt,))],
        compiler_params=pltpu.CompilerParams(has_side_effects=True),
    )(*arrays)


def _sum_parts(name, parts):
    n, R, C = parts.shape
    tr = _tile(R, 256, 8)

    def body(p_ref, o_ref):
        g = p_ref[0].astype(f32)
        for s in range(1, n):
            g = g + p_ref[s].astype(f32)
        o_ref[...] = g

    return pl.pallas_call(
        body, name=name, grid=(R // tr,),
        in_specs=[pl.BlockSpec((n, tr, C), lambda i: (0, i, 0))],
        out_specs=pl.BlockSpec((tr, C), lambda i: (i, 0)),
        out_shape=jax.ShapeDtypeStruct((R, C), f32),
        compiler_params=_cparams("parallel"),
    )(parts)


def _adamw(name, w, m, v, parts):
    R, C = w.shape
    n = parts.shape[0]
    tr = _tile(R, 256, 8)
    c1 = 1.0 - _ADAM_B1 ** _ADAM_STEP
    c2 = 1.0 - _ADAM_B2 ** _ADAM_STEP

    def body(w_ref, m_ref, v_ref, p_ref, g_ref, d_ref, nm_ref, nv_ref):
        g = p_ref[0].astype(f32)
        for s in range(1, n):
            g = g + p_ref[s].astype(f32)
        nm = _ADAM_B1 * m_ref[...] + (1.0 - _ADAM_B1) * g
        nv = _ADAM_B2 * v_ref[...] + (1.0 - _ADAM_B2) * (g * g)
        g_ref[...] = g
        nm_ref[...] = nm
        nv_ref[...] = nv
        d_ref[...] = -_ADAM_LR * ((nm / c1) / (jnp.sqrt(nv / c2) + _ADAM_EPS) + _ADAM_WD * w_ref[...])

    blk = pl.BlockSpec((tr, C), lambda i: (i, 0))
    sd = jax.ShapeDtypeStruct((R, C), f32)
    return pl.pallas_call(
        body, name=name, grid=(R // tr,),
        in_specs=[blk, blk, blk, pl.BlockSpec((n, tr, C), lambda i: (0, i, 0))],
        out_specs=[blk, blk, blk, blk], out_shape=[sd, sd, sd, sd],
        compiler_params=_cparams("parallel"),
    )(w, m, v, parts)


def _pack(vectors):
    flat = jnp.concatenate([a.reshape(-1).astype(f32) for a in vectors])
    pad = (-flat.shape[0]) % _SMALL_PAD
    return jnp.pad(flat, (0, pad)).reshape(-1, _LANE)


def _unpack(packed, shapes):
    flat = packed.reshape(-1)
    out, o = [], 0
    for s in shapes:
        n = 1
        for d in s:
            n *= d
        out.append(flat[o:o + n].reshape(s))
        o += n
    return out


def _pad_rows8(a):
    return jnp.pad(a, ((0, 8 - a.shape[0]), (0, 0)))


def _local_step(x, target, p, get, emit):
    T, D = x.shape

    def tie(a, *tokens):
        for tok in tokens:
            if tok is not None:
                a = a + tok[0, 0]
        return a

    w_in = get("w_in", x)
    h1, proj = _in_proj(x, p["g_mix"], w_in)
    w_pool = get("w_pool", proj)
    PW = w_pool.shape[0] * w_pool.shape[1]
    y_pool = _pool_fwd(proj, w_pool, p["pool_scale"])
    lru_conv_w, w_a, w_i = get("lru_conv_w", proj), get("w_a", proj), get("w_i", proj)
    y_lru, hs = _lru_fwd(proj, D, PW, lru_conv_w, p["lru_conv_b"], w_a, p["b_a"], w_i, p["b_i"], p["lru_lambda"])
    w_pp, w_lp = get("w_pool_proj", proj), get("w_lru_proj", proj)
    pp, qq, merged = _merge_fwd(y_pool, y_lru, w_pp, w_lp, proj, p["b_gate"])
    w_out = get("w_out", proj)
    x2, h2 = _out_proj(merged, w_out, x, p["g_mlp"])
    w_up = get("w_up", x2)
    up = _mm_nn("up_proj", h2, w_up, f32, tm_want=2 * _TM)
    ffn_conv_w = get("ffn_conv_w", proj)
    z = _ffn_fwd(up, ffn_conv_w, p["ffn_conv_b"])
    w_down = get("w_down", up)
    F = w_down.shape[0]
    dx3, dx3b, loss_t, dg_final = _down_loss(z, w_down, x2, target, p["g_final"])

    gs = {"g_final": dg_final}
    tok = emit("w_down", _mm_tn("dw_down", z, dx3b, bf16, tm_want=1536))
    dz = _mm_nt("dz", dx3b, w_down, bf16)
    dup, dcw_ffn, dcb_ffn = _ffn_bwd(dz, up, ffn_conv_w, tie(p["ffn_conv_b"], tok))
    gs["ffn_conv_w"] = dcw_ffn[0:3]
    gs["ffn_conv_b"] = dcb_ffn

    tm = _tile(T, 2 * _TM)
    tk = _tile(F, _TN_MAX)
    nkh = F // tk
    tkt = _tile(T, _TK_T)
    tok = emit("w_up", _matmul(
        "dw_up", "tn", (h2, dup),
        [pl.BlockSpec((tkt, D), lambda i, j, k: (k, 0)), pl.BlockSpec((None, tkt, tk), lambda i, j, k: (j // nkh, k, j % nkh))],
        jax.ShapeDtypeStruct((D, 2 * F), bf16), pl.BlockSpec((D, tk), lambda i, j, k: (0, j)),
        (1, 2 * nkh, T // tkt), (D, tk)))

    tkc = _tile(F, _TK_DOWN)
    nkc = F // tkc
    dx2, dx2b, gs["g_mlp"] = _norm_bwd_matmul(
        "dh2", dup, pl.BlockSpec((None, tm, tkc), lambda i, k: (k // nkc, i, k % nkc)), 2 * nkc,
        w_up, pl.BlockSpec((D, tkc), lambda i, k: (0, k)), x2, tie(p["g_mlp"], tok), dx3)

    tok = emit("w_out", _mm_tn("dw_out", merged, dx2b, bf16))
    dP, dQ, dl0, dl1, db0, db1 = _merge_bwd(dx2b, w_out, pp, qq, proj, tie(p["b_gate"], tok), PW)
    gs["b_gate"] = jnp.concatenate([db0, db1], axis=1)
    tok = emit("w_pool_proj", _mm_tn("dw_pool_proj", y_pool, dP, bf16))
    tok2 = emit("w_lru_proj", _mm_tn("dw_lru_proj", y_lru, dQ, bf16))
    dy_pool = _mm_nt("dy_pool", dP, w_pp, f32)
    dy_lru = _mm_nt("dy_lru", dQ, w_lp, f32)

    du_lru, du_gelu, dwa, dwi, dcw_lru, dvec = _lru_bwd(
        dy_lru, proj, hs, D, lru_conv_w, tie(p["lru_conv_b"], tok, tok2), w_a, p["b_a"], w_i, p["b_i"], p["lru_lambda"])
    tok = emit("w_a", dwa.astype(bf16))
    tok2 = emit("w_i", dwi.astype(bf16))
    gs["lru_conv_w"] = dcw_lru[0:4]
    gs["lru_conv_b"], gs["b_a"], gs["b_i"], gs["lru_lambda"] = dvec[0:1], dvec[1:2], dvec[2:3], dvec[3:4]
    du_pool, dwp, gs["pool_scale"] = _pool_bwd(dy_pool, proj, w_pool, tie(p["pool_scale"], tok, tok2))
    tok = emit("w_pool", dwp.astype(bf16))

    dproj = jnp.concatenate([du_pool, du_lru, du_gelu, dl0, dl1], axis=1)
    tok2 = emit("w_in", _mm_tn("dw_in", h1, dproj, bf16))
    NI = dproj.shape[1]
    tki = _tile(NI, _TK_DOWN)
    grad_x, _, gs["g_mix"] = _norm_bwd_matmul(
        "dh1", dproj, pl.BlockSpec((tm, tki), lambda i, k: (i, k)), NI // tki,
        w_in, pl.BlockSpec((D, tki), lambda i, k: (0, k)), x, tie(p["g_mix"], tok, tok2), dx2)
    return loss_t[0, 0], grad_x, gs


_MATRICES = {"w_in": "col", "w_pool": "mid", "w_a": "mid", "w_i": "mid", "w_pool_proj": "col", "w_lru_proj": "row",
             "w_out": "row", "w_up": "col", "w_down": "row"}
_CONVS = ("lru_conv_w", "ffn_conv_w")
_GATHER_GROUPS = (("w_in",), ("w_pool", "lru_conv_w", "w_a", "w_i"), ("w_pool_proj", "w_lru_proj", "w_out", "ffn_conv_w"),
                  ("w_up",), ("w_down",))
_VECTORS = ("g_mix", "b_gate", "pool_scale", "lru_conv_b", "b_a", "b_i", "lru_lambda", "g_mlp", "ffn_conv_b", "g_final")
_WEIGHTS = ("g_mix", "w_in", "b_gate", "w_pool", "pool_scale", "lru_conv_w", "lru_conv_b", "w_a", "b_a", "w_i", "b_i",
            "lru_lambda", "w_pool_proj", "w_lru_proj", "w_out", "g_mlp", "w_up", "ffn_conv_w", "ffn_conv_b", "w_down", "g_final")


def _full_shape(shape, kind):
    s = list(shape)
    s[{"col": 1, "row": 0, "mid": 1}[kind]] *= _N_DEV
    return tuple(s)


def _step(x, target, w, m, v):
    x, target = x[0], target[0]
    me = _blk(*_my_place())

    axis = {"col": 1, "row": 0, "mid": 1}
    kind = dict(_MATRICES, **{n: "col" for n in _CONVS})
    shard = {n: w[n].astype(bf16) for n in _MATRICES}
    shard.update({n: _pad_rows8(w[n]) for n in _CONVS})
    order = [n for grp in _GATHER_GROUPS for n in grp]
    index = {n: t for t, n in enumerate(order)}
    groups = [[index[n] for n in grp] for grp in _GATHER_GROUPS]
    g_routes = [_Route("gather", kind[n], shard[n].shape[axis[kind[n]]]) for n in order]
    lands = [lax.empty(_full_shape(shard[n].shape, kind[n]), shard[n].dtype) for n in order]
    g_sems, g_srcs, g_lands, _ = _send_start("gather_start", [shard[n] for n in order], lands, g_routes, groups)
    gathered = {}

    def get(name, after):
        if name not in gathered:
            gi = next(i for i, grp in enumerate(_GATHER_GROUPS) if name in grp)
            ts = groups[gi]
            routes = [g_routes[t] for t in ts]
            srcs = [g_srcs[t] for t in ts]
            srcs, got = _send_wait(f"gather_wait_{gi}", srcs, [g_lands[t] for t in ts], routes, g_sems[gi], after)
            full = _place_own(f"gather_own_{gi}", srcs, got, routes)
            gathered.update(zip(_GATHER_GROUPS[gi], full))
        return gathered[name]

    sent = {}

    def emit(name, grad):
        k = _MATRICES[name]
        size = w[name].shape[axis[k]]
        route = _Route("scatter", k, size)
        shp = list(grad.shape)
        shp[axis[k]] = size
        land = lax.empty((_N_DEV, *shp), grad.dtype)
        sems, srcs, lnds, token = _send_start("grad_start_" + name, [grad], [land], [route], [[0]])
        sent[name] = (srcs, lnds, [route], sems[0])
        return token

    p = {n: w[n].reshape(1, -1) for n in _VECTORS}
    loss_t, grad_x, gs = _local_step(x, target, p, get, emit)
    loss = lax.psum(loss_t, ("x", "y", "c"))

    small_names = list(_VECTORS) + list(_CONVS)
    small_shapes = [tuple(gs[n].shape) for n in small_names]
    small_parts = _exchange([], [], [], [_pack([gs[n] for n in small_names])])[0]

    out = {}
    mats = list(_MATRICES)
    for n in mats:
        srcs, lnds, routes, sems = sent[n]
        srcs, got = _send_wait("grad_wait_" + n, srcs, lnds, routes, sems, grad_x)
        parts = _place_own("grad_own_" + n, srcs, got, routes)[0]
        shp = w[n].shape
        r2 = (-1, shp[-1])
        res = _adamw("adamw_" + n, w[n].reshape(r2), m[n].reshape(r2), v[n].reshape(r2),
                     parts.reshape((_N_DEV,) + w[n].reshape(r2).shape))
        out[n] = [a.reshape(shp) for a in res]
    gsum = _unpack(_sum_parts("sum_small", small_parts), small_shapes)
    gsmall = dict(zip(small_names, gsum))
    for n in _CONVS:
        cols = w[n].shape[1]
        gsmall[n] = lax.dynamic_slice_in_dim(gsmall[n], me * cols, cols, axis=1)
    pk = lambda d: _pack([d[n] for n in small_names])
    res = _adamw("adamw_small", pk(w), pk(m), pk(v), pk(gsmall)[None])
    shapes = [tuple(w[n].shape) for n in small_names]
    for k, arr in enumerate(res):
        for n, a in zip(small_names, _unpack(arr, shapes)):
            out.setdefault(n, [None] * 4)[k] = a
    return loss, grad_x[None], out


def kernel(x, g_mix, w_in, b_gate, w_pool, pool_scale, lru_conv_w, lru_conv_b, w_a, b_a, w_i, b_i, lru_lambda, w_pool_proj, w_lru_proj, w_out, g_mlp, w_up, ffn_conv_w, ffn_conv_b, w_down, g_final, loss_target, m_g_mix, m_w_in, m_b_gate, m_w_pool, m_pool_scale, m_lru_conv_w, m_lru_conv_b, m_w_a, m_b_a, m_w_i, m_b_i, m_lru_lambda, m_w_pool_proj, m_w_lru_proj, m_w_out, m_g_mlp, m_w_up, m_ffn_conv_w, m_ffn_conv_b, m_w_down, m_g_final, v_g_mix, v_w_in, v_b_gate, v_w_pool, v_pool_scale, v_lru_conv_w, v_lru_conv_b, v_w_a, v_b_a, v_w_i, v_b_i, v_lru_lambda, v_w_pool_proj, v_w_lru_proj, v_w_out, v_g_mlp, v_w_up, v_ffn_conv_w, v_ffn_conv_b, v_w_down, v_g_final):
    given = dict(locals())
    orig = {n: given[n].shape for n in _WEIGHTS}

    def squeeze(a, n):
        return a if n == "g_final" else a[0]

    w = {n: squeeze(given[n], n) for n in _WEIGHTS}
    m = {n: squeeze(given["m_" + n], n) for n in _WEIGHTS}
    v = {n: squeeze(given["v_" + n], n) for n in _WEIGHTS}
    for d in (w, m, v):
        d["g_final"] = d["g_final"].reshape(1, -1)
    loss, grad_x, out = _step(x, loss_target, w, m, v)
    res = [loss, grad_x]
    for k in range(4):
        res += [out[n][k].reshape(orig[n]) for n in _WEIGHTS]
    return tuple(res)
```

```python
import functools

import jax
import jax.numpy as jnp
from jax import lax
from jax.experimental import pallas as pl
from jax.experimental.pallas import tpu as pltpu

f32 = jnp.float32
bf16 = jnp.bfloat16

_EPS = 1e-6
_LRU_C = 8.0
_POOL_WINDOWS = (2, 4, 8, 16)
_POOL_HALO = 16
_CONV_HALO = 8
_GELU_C0 = 0.7978845608028654
_GELU_C1 = 0.044715
_ADAM_LR, _ADAM_B1, _ADAM_B2, _ADAM_EPS, _ADAM_WD, _ADAM_STEP = 0.001, 0.9, 0.999, 1e-08, 0.01, 10
_N_DEV = 8
_LANE = 128
_SMALL_PAD = 8 * _LANE
_VMEM_LIMIT = 60 * 1024 * 1024

_TM = 512
_TM_SMALL = 256
_TC = 256
_TK_T = 512
_TN_MAX = 1536
_CW = 1024
_TK_DOWN = 768
_TK_UP = 2048


def _cparams(*sem):
    return pltpu.CompilerParams(dimension_semantics=tuple(sem), vmem_limit_bytes=_VMEM_LIMIT)


def _tile(n, want, mult=1):
    if n <= want:
        return n
    t = want - want % mult
    while n % t:
        t -= mult
    return t


def _gelu(x):
    return 0.5 * x * (1.0 + jnp.tanh(_GELU_C0 * (x + _GELU_C1 * (x * x * x))))


def _gelu_both(x):
    x2 = x * x
    t = jnp.tanh(_GELU_C0 * (x + _GELU_C1 * (x2 * x)))
    h = 0.5 * (1.0 + t)
    return x * h, h + 0.5 * x * (1.0 - t * t) * (_GELU_C0 * (1.0 + 3.0 * _GELU_C1 * x2))


def _fold8(x):
    out = x[0:8]
    for r in range(8, x.shape[0], 8):
        out = out + x[r:r + 8]
    return out


def _sigmoid(x):
    return jax.nn.sigmoid(x)


def _dot(a, b):
    return jnp.dot(a, b, preferred_element_type=f32)


def _dot_nt(a, b):
    return lax.dot_general(a, b, (((1,), (1,)), ((), ())), preferred_element_type=f32)


def _dot_tn(a, b):
    return lax.dot_general(a, b, (((0,), (0,)), ((), ())), preferred_element_type=f32)


def _rms_rows(x_ref, g_ref, h_ref, n_rows, n_cols):
    rr = _tile(n_rows, _SLAB_ROWS)
    cg = _tile(n_cols, _SLAB_COLS)
    for r0 in range(0, n_rows, rr):
        rows = pl.ds(r0, rr)
        x = x_ref[rows, :]
        r = lax.rsqrt(jnp.mean(x * x, axis=-1, keepdims=True) + _EPS)
        for c0 in range(0, n_cols, cg):
            cols = slice(c0, c0 + cg)
            h_ref[rows, cols] = (x_ref[rows, cols] * r * g_ref[:, cols]).astype(bf16)


def _matmul(name, mode, operands, in_specs, out_shape, out_spec, grid, acc_shape):
    dot = {"nn": _dot, "nt": _dot_nt, "tn": _dot_tn}[mode]
    nk = grid[2]

    def body_whole(a_ref, b_ref, o_ref):
        o_ref[...] = dot(a_ref[...], b_ref[...]).astype(o_ref.dtype)

    def body(a_ref, b_ref, o_ref, acc_ref):
        k = pl.program_id(2)

        @pl.when(k == 0)
        def _():
            acc_ref[...] = dot(a_ref[...], b_ref[...])

        @pl.when((k > 0) & (k < nk - 1))
        def _():
            acc_ref[...] += dot(a_ref[...], b_ref[...])

        @pl.when(k == nk - 1)
        def _():
            o_ref[...] = (acc_ref[...] + dot(a_ref[...], b_ref[...])).astype(o_ref.dtype)

    return pl.pallas_call(
        body_whole if nk == 1 else body, name=name, grid=grid, in_specs=in_specs, out_specs=out_spec, out_shape=out_shape,
        scratch_shapes=[] if nk == 1 else [pltpu.VMEM(acc_shape, f32)],
        compiler_params=_cparams("parallel", "parallel", "arbitrary"),
    )(*operands)


def _mm_nn(name, a, b, out_dtype, tm_want=None):
    M, K = a.shape
    N = b.shape[1]
    tm, tn = _tile(M, tm_want or _TM), _tile(N, _TN_MAX)
    return _matmul(
        name, "nn", (a, b),
        [pl.BlockSpec((tm, K), lambda i, j, k: (i, 0)), pl.BlockSpec((K, tn), lambda i, j, k: (0, j))],
        jax.ShapeDtypeStruct((M, N), out_dtype), pl.BlockSpec((tm, tn), lambda i, j, k: (i, j)),
        (M // tm, N // tn, 1), (tm, tn))


def _mm_nt(name, a, b, out_dtype):
    M, K = a.shape
    N = b.shape[0]
    tm, tn = _tile(M, _TM), _tile(N, _TN_MAX)
    return _matmul(
        name, "nt", (a, b),
        [pl.BlockSpec((tm, K), lambda i, j, k: (i, 0)), pl.BlockSpec((tn, K), lambda i, j, k: (j, 0))],
        jax.ShapeDtypeStruct((M, N), out_dtype), pl.BlockSpec((tm, tn), lambda i, j, k: (i, j)),
        (M // tm, N // tn, 1), (tm, tn))


def _mm_tn(name, a, b, out_dtype, tm_want=2048):
    T, M = a.shape
    N = b.shape[1]
    tm, tn, tk = _tile(M, tm_want), _tile(N, _TN_MAX), _tile(T, _TK_T)
    return _matmul(
        name, "tn", (a, b),
        [pl.BlockSpec((tk, tm), lambda i, j, k: (k, i)), pl.BlockSpec((tk, tn), lambda i, j, k: (k, j))],
        jax.ShapeDtypeStruct((M, N), out_dtype), pl.BlockSpec((tm, tn), lambda i, j, k: (i, j)),
        (M // tm, N // tn, T // tk), (tm, tn))


def _in_proj_first(x, g_mix, w_in, base, n_tiles):
    T, D = x.shape
    NI = w_in.shape[1]
    tm, tn = _tile(T, 2 * _TM), NI // _N_DEV

    def body(base_ref, x_ref, g_ref, w_ref, h_ref, o_ref):
        @pl.when(pl.program_id(1) == 0)
        def _():
            _rms_rows(x_ref, g_ref, h_ref, tm, D)

        o_ref[...] = _dot(h_ref[...], w_ref[...])

    return pl.pallas_call(
        body, name="in_proj_0",
        grid_spec=pltpu.PrefetchScalarGridSpec(
            num_scalar_prefetch=1, grid=(T // tm, n_tiles),
            in_specs=[pl.BlockSpec((tm, D), lambda i, j, b: (i, 0)), pl.BlockSpec((1, D), lambda i, j, b: (0, 0)),
                      pl.BlockSpec((D, tn), lambda i, j, b: (0, b[0] + j))],
            out_specs=[pl.BlockSpec((tm, D), lambda i, j, b: (i, 0)), pl.BlockSpec((tm, tn), lambda i, j, b: (i, b[0] + j))]),
        out_shape=[jax.ShapeDtypeStruct((T, D), bf16), jax.ShapeDtypeStruct((T, NI), f32)],
        compiler_params=_cparams("parallel", "arbitrary"),
    )(base, x, g_mix, w_in)


def _in_proj_more(name, h1, w_in, proj, base, n_tiles):
    T, D = h1.shape
    NI = w_in.shape[1]
    tm, tn = _tile(T, 2 * _TM), NI // _N_DEV

    def body(base_ref, h_ref, w_ref, proj_ref, o_ref):
        o_ref[...] = _dot(h_ref[...], w_ref[...])

    return pl.pallas_call(
        body, name=name,
        grid_spec=pltpu.PrefetchScalarGridSpec(
            num_scalar_prefetch=1, grid=(T // tm, n_tiles),
            in_specs=[pl.BlockSpec((tm, D), lambda i, j, b: (i, 0)), pl.BlockSpec((D, tn), lambda i, j, b: (0, b[0] + j)),
                      pl.BlockSpec(memory_space=pl.ANY)],
            out_specs=pl.BlockSpec((tm, tn), lambda i, j, b: (i, b[0] + j))),
        out_shape=jax.ShapeDtypeStruct((T, NI), f32),
        input_output_aliases={3: 0},
        compiler_params=_cparams("parallel", "arbitrary"),
    )(base, h1, w_in, proj)


def _window_means(ext_ref, row0, tc, gw):
    H = _POOL_HALO
    t_glob = row0 + lax.broadcasted_iota(jnp.int32, (tc, 1), 0)
    out = []
    for g, w in enumerate(_POOL_WINDOWS):
        s = ext_ref[:, g * gw:(g + 1) * gw]
        st = 1
        while st < w:
            s = s + pltpu.roll(s, st, 0)
            st *= 2
        cnt = jnp.minimum(t_glob + 1, w).astype(f32)
        out.append((s[H:, :] / cnt, ext_ref[pl.ds(H, tc), g * gw:(g + 1) * gw]))
    return out


def _pool_fwd(proj, w_pool, pool_scale):
    T = proj.shape[0]
    G, gw, _ = w_pool.shape
    PW = G * gw
    tc = _tile(T, _TC)
    H = _POOL_HALO

    def body(u_ref, w_ref, s_ref, y_ref, ext_ref):
        i = pl.program_id(0)

        @pl.when(i == 0)
        def _():
            ext_ref[pl.ds(0, H), :] = jnp.zeros((H, PW), f32)

        ext_ref[pl.ds(H, tc), :] = u_ref[...]
        for g, (m, u) in enumerate(_window_means(ext_ref, i * tc, tc, gw)):
            d = (m - u).astype(bf16)
            y = _dot(d, w_ref[g]) * s_ref[:, g * gw:(g + 1) * gw]
            y_ref[:, g * gw:(g + 1) * gw] = y.astype(bf16)
        ext_ref[pl.ds(0, H), :] = ext_ref[pl.ds(tc, H), :]

    return pl.pallas_call(
        body, name="pool_fwd", grid=(T // tc,),
        in_specs=[pl.BlockSpec((tc, PW), lambda i: (i, 0)), pl.BlockSpec((G, gw, gw), lambda i: (0, 0, 0)),
                  pl.BlockSpec((1, PW), lambda i: (0, 0))],
        out_specs=pl.BlockSpec((tc, PW), lambda i: (i, 0)),
        out_shape=jax.ShapeDtypeStruct((T, PW), bf16),
        scratch_shapes=[pltpu.VMEM((H + tc, PW), f32)],
        compiler_params=_cparams("arbitrary"),
    )(proj, w_pool, pool_scale)


def _softplus(z):
    return jnp.maximum(z, 0.0) + jnp.log1p(jnp.exp(-jnp.abs(z)))


def _causal_conv(ext_ref, cw_ref, cb_ref, n, K, cols=slice(None), r0=0):
    H = _CONV_HALO
    v = cb_ref[:, cols] + cw_ref[K - 1:K, cols] * ext_ref[pl.ds(H + r0, n), cols]
    for k in range(K - 1):
        v = v + cw_ref[k:k + 1, cols] * ext_ref[pl.ds(H + r0 - (K - 1 - k), n), cols]
    return v


_ROWS = 8
_SLAB_ROWS = 16
_SLAB_COLS = 512


def _slabs(n_rows, n_cols, reverse=False):
    cg = _tile(n_cols, _SLAB_COLS)
    rr = _tile(n_rows, _SLAB_ROWS)
    starts = range(0, n_rows, rr)
    for c0 in range(0, n_cols, cg):
        for r0 in (reversed(starts) if reverse else starts):
            yield slice(c0, c0 + cg), pl.ds(r0, rr)


def _slabs_with_sums(n_rows, n_cols, n_sums, visit, flush, reverse=False):
    cg = _tile(n_cols, _SLAB_COLS)
    rr = _tile(n_rows, _SLAB_ROWS)
    starts = list(range(0, n_rows, rr))
    for c0 in range(0, n_cols, cg):
        cols = slice(c0, c0 + cg)
        sums = [jnp.zeros((8, cg), f32) for _ in range(n_sums)]
        for r0 in (reversed(starts) if reverse else starts):
            sums = visit(cols, pl.ds(r0, rr), sums)
        flush(cols, [jnp.sum(s, axis=0, keepdims=True) for s in sums])


def _lru_fwd(proj, D, PW, conv_w, conv_b, w_a, b_a, w_i, b_i, lam):
    T = proj.shape[0]
    NB, bw, _ = w_a.shape
    K = 4
    tc = _tile(T, _TC)
    H = _CONV_HALO
    hb = D // 2
    assert PW == hb and conv_w.shape[0] == 8

    def body(u0_ref, u1_ref, g0_ref, g1_ref, cw_ref, cb_ref, wa_ref, ba_ref, wi_ref, bi_ref, lam_ref,
             y_ref, hs_ref, ext_ref, a_scr, b_scr, v_scr, hc_scr):
        c = pl.program_id(0)

        @pl.when(c == 0)
        def _():
            ext_ref[pl.ds(0, H), :] = jnp.zeros((H, D), f32)
            hc_scr[...] = jnp.zeros_like(hc_scr)

        ext_ref[pl.ds(H, tc), 0:hb] = u0_ref[...]
        ext_ref[pl.ds(H, tc), hb:D] = u1_ref[...]
        sp = _softplus(-lam_ref[...])
        for cols, rows in _slabs(tc, D):
            v_scr[rows, cols] = _causal_conv(ext_ref, cw_ref, cb_ref, rows.size, K, cols, rows.start)
        for b in range(NB):
            cols = slice(b * bw, (b + 1) * bw)
            vb = v_scr[:, cols].astype(bf16)
            a_scr[:, cols] = _dot(vb, wa_ref[b])
            b_scr[:, cols] = _dot(vb, wi_ref[b])
        for cols, rows in _slabs(tc, D):
            r = _sigmoid(a_scr[rows, cols] + ba_ref[:, cols])
            i = _sigmoid(b_scr[rows, cols] + bi_ref[:, cols])
            a = jnp.exp(-_LRU_C * r * sp[:, cols])
            a_scr[rows, cols] = a
            b_scr[rows, cols] = jnp.sqrt(1.0 - a * a) * (i * v_scr[rows, cols])

        def step(t, h):
            h = a_scr[pl.ds(t, 1), :] * h + b_scr[pl.ds(t, 1), :]
            hs_ref[pl.ds(t, 1), :] = h
            return h

        h = lax.fori_loop(0, tc, step, hc_scr[0:1, :], unroll=8)
        hc_scr[0:1, :] = h
        for cols, rows in _slabs(tc, D):
            g_ref, gcols = (g0_ref, cols) if cols.start < hb else (g1_ref, slice(cols.start - hb, cols.stop - hb))
            y_ref[rows, cols] = (hs_ref[rows, cols] * _gelu(g_ref[rows, gcols])).astype(bf16)
        ext_ref[pl.ds(0, H), :] = ext_ref[pl.ds(tc, H), :]

    vec = pl.BlockSpec((1, D), lambda c: (0, 0))
    wspec = pl.BlockSpec((NB, bw, bw), lambda c: (0, 0, 0))
    return pl.pallas_call(
        body, name="lru_fwd", grid=(T // tc,),
        in_specs=[pl.BlockSpec((tc, hb), lambda c: (c, 1)), pl.BlockSpec((tc, hb), lambda c: (c, 2)),
                  pl.BlockSpec((tc, hb), lambda c: (c, 3)), pl.BlockSpec((tc, hb), lambda c: (c, 4)),
                  pl.BlockSpec((8, D), lambda c: (0, 0)), vec, wspec, vec, wspec, vec, vec],
        out_specs=[pl.BlockSpec((tc, D), lambda c: (c, 0)), pl.BlockSpec((tc, D), lambda c: (c, 0))],
        out_shape=[jax.ShapeDtypeStruct((T, D), bf16), jax.ShapeDtypeStruct((T, D), f32)],
        scratch_shapes=[pltpu.VMEM((H + tc, D), f32), pltpu.VMEM((tc, D), f32), pltpu.VMEM((tc, D), f32),
                        pltpu.VMEM((tc, D), f32), pltpu.VMEM((8, D), f32)],
        compiler_params=_cparams("arbitrary"),
    )(proj, proj, proj, proj, conv_w, conv_b, w_a, b_a, w_i, b_i, lam)


def _merge_fwd(y_pool, y_lru, w_pp, w_lp, proj, b_gate):
    T, PW = y_pool.shape
    D = y_lru.shape[1]
    tm, tn = _tile(T, _TM), _tile(D, PW)
    nj = D // tn
    off = (PW + 2 * D) // tn

    def body(yp_ref, yl_ref, wp_ref, wl_ref, l0_ref, l1_ref, b0_ref, b1_ref, p_ref, q_ref, m_ref, p_scr, q_scr):
        p_scr[...] = _dot(yp_ref[...], wp_ref[...])
        q_scr[...] = _dot(yl_ref[...], wl_ref[...])
        for cols, rows in _slabs(tm, tn):
            p, q = p_scr[rows, cols], q_scr[rows, cols]
            g0 = _sigmoid(l0_ref[rows, cols] + b0_ref[:, cols])
            g1 = _sigmoid(l1_ref[rows, cols] + b1_ref[:, cols])
            m_ref[rows, cols] = (g0 * p + g1 * q).astype(bf16)
            p_ref[rows, cols] = p.astype(bf16)
            q_ref[rows, cols] = q.astype(bf16)

    tile = pl.BlockSpec((tm, tn), lambda j, i: (i, j))
    return pl.pallas_call(
        body, name="merge_fwd", grid=(nj, T // tm),
        in_specs=[pl.BlockSpec((tm, PW), lambda j, i: (i, 0)), pl.BlockSpec((tm, D), lambda j, i: (i, 0)),
                  pl.BlockSpec((PW, tn), lambda j, i: (0, j)), pl.BlockSpec((D, tn), lambda j, i: (0, j)),
                  pl.BlockSpec((tm, tn), lambda j, i: (i, off + j)), pl.BlockSpec((tm, tn), lambda j, i: (i, off + nj + j)),
                  pl.BlockSpec((1, tn), lambda j, i: (0, j)), pl.BlockSpec((1, tn), lambda j, i: (0, nj + j))],
        out_specs=[tile, tile, tile],
        out_shape=[jax.ShapeDtypeStruct((T, D), bf16), jax.ShapeDtypeStruct((T, D), bf16), jax.ShapeDtypeStruct((T, D), bf16)],
        scratch_shapes=[pltpu.VMEM((tm, tn), f32), pltpu.VMEM((tm, tn), f32)],
        compiler_params=_cparams("parallel", "arbitrary"),
    )(y_pool, y_lru, w_pp, w_lp, proj, proj, b_gate, b_gate)


def _out_proj(merged, w_out, x, g_mlp):
    T, D = x.shape
    tm = _tile(T, _TM_SMALL)

    def body(m_ref, w_ref, x_ref, g_ref, x2_ref, h2_ref):
        x2_ref[...] = x_ref[...] + _dot(m_ref[...], w_ref[...])
        _rms_rows(x2_ref, g_ref, h2_ref, tm, D)

    row = pl.BlockSpec((tm, D), lambda i: (i, 0))
    return pl.pallas_call(
        body, name="out_proj", grid=(T // tm,),
        in_specs=[row, pl.BlockSpec((D, D), lambda i: (0, 0)), row, pl.BlockSpec((1, D), lambda i: (0, 0))],
        out_specs=[row, row],
        out_shape=[jax.ShapeDtypeStruct((T, D), f32), jax.ShapeDtypeStruct((T, D), bf16)],
        compiler_params=_cparams("parallel"),
    )(merged, w_out, x, g_mlp)


def _ffn_fwd(up, conv_w, conv_b):
    T, F2 = up.shape
    F = F2 // 2
    K = 3
    tc = _tile(T, 2 * _TC)
    cw = _tile(F, _CW)
    ns = F // cw
    H = _CONV_HALO

    def body(gp_ref, val_ref, cw_ref, cb_ref, z_ref, ext_ref):
        @pl.when(pl.program_id(1) == 0)
        def _():
            ext_ref[pl.ds(0, H), :] = jnp.zeros((H, cw), f32)

        ext_ref[pl.ds(H, tc), :] = gp_ref[...]
        for cols, rows in _slabs(tc, cw):
            c = _causal_conv(ext_ref, cw_ref, cb_ref, rows.size, K, cols, rows.start)
            z_ref[rows, cols] = (_gelu(c) * val_ref[rows, cols]).astype(bf16)
        ext_ref[pl.ds(0, H), :] = ext_ref[pl.ds(tc, H), :]

    return pl.pallas_call(
        body, name="ffn_fwd", grid=(ns, T // tc),
        in_specs=[pl.BlockSpec((tc, cw), lambda s, c: (c, s)), pl.BlockSpec((tc, cw), lambda s, c: (c, ns + s)),
                  pl.BlockSpec((8, cw), lambda s, c: (0, s)), pl.BlockSpec((1, cw), lambda s, c: (0, s))],
        out_specs=pl.BlockSpec((tc, cw), lambda s, c: (c, s)),
        out_shape=jax.ShapeDtypeStruct((T, F), bf16),
        scratch_shapes=[pltpu.VMEM((H + tc, cw), f32)],
        compiler_params=_cparams("parallel", "arbitrary"),
    )(up, up, conv_w, conv_b)


def _down_loss(z, w_down, x2, target, g_final):
    T, F = z.shape
    D = x2.shape[1]
    tm, tk = _tile(T, _TM), _tile(F, _TK_DOWN)
    nk = F // tk

    def body(z_ref, w_ref, x2_ref, t_ref, g_ref, dx_ref, dxb_ref, loss_ref, dg_ref, acc_ref):
        i, k = pl.program_id(0), pl.program_id(1)

        @pl.when(k == 0)
        def _():
            acc_ref[...] = x2_ref[...]

        @pl.when((i == 0) & (k == 0))
        def _():
            loss_ref[...] = jnp.zeros_like(loss_ref)
            dg_ref[...] = jnp.zeros_like(dg_ref)

        acc_ref[...] += _dot(z_ref[...], w_ref[...])

        @pl.when(k == nk - 1)
        def _():
            g = g_ref[...]
            sq = jnp.zeros((_ROWS, 1), f32)
            dgs = jnp.zeros((_ROWS, D), f32)
            for r0 in range(0, tm, _ROWS):
                rows = pl.ds(r0, _ROWS)
                x3 = acc_ref[rows, :]
                r = lax.rsqrt(jnp.mean(x3 * x3, axis=-1, keepdims=True) + _EPS)
                xr = x3 * r
                e = xr * g - t_ref[rows, :]
                sq = sq + jnp.sum(e * e, axis=-1, keepdims=True)
                dy = e * (1.0 / D)
                gy = dy * g
                dx = r * gy - x3 * ((r * r * r) * jnp.mean(x3 * gy, axis=-1, keepdims=True))
                dgs = dgs + dy * xr
                dx_ref[rows, :] = dx
                dxb_ref[rows, :] = dx.astype(bf16)
            loss_ref[...] += (0.5 / D) * jnp.sum(sq)
            dg_ref[...] += jnp.sum(dgs, axis=0, keepdims=True)

    row = pl.BlockSpec((tm, D), lambda i, k: (i, 0))
    vec = pl.BlockSpec((1, D), lambda i, k: (0, 0))
    return pl.pallas_call(
        body, name="down_loss", grid=(T // tm, nk),
        in_specs=[pl.BlockSpec((tm, tk), lambda i, k: (i, k)), pl.BlockSpec((tk, D), lambda i, k: (k, 0)), row, row, vec],
        out_specs=[row, row, pl.BlockSpec((8, _LANE), lambda i, k: (0, 0)), vec],
        out_shape=[jax.ShapeDtypeStruct((T, D), f32), jax.ShapeDtypeStruct((T, D), bf16),
                   jax.ShapeDtypeStruct((8, _LANE), f32), jax.ShapeDtypeStruct((1, D), f32)],
        scratch_shapes=[pltpu.VMEM((tm, D), f32)],
        compiler_params=_cparams("arbitrary", "arbitrary"),
    )(z, w_down, x2, target, g_final)


def _ffn_bwd(dz, up, conv_w, conv_b):
    T, F = dz.shape
    K = 3
    tc = _tile(T, 2 * _TC)
    cw = _tile(F, _CW)
    ns, nt = F // cw, T // tc
    H = _CONV_HALO

    def body(dz_ref, gp_ref, val_ref, gph_ref, cw_ref, cb_ref, dup_ref, dcw_ref, dcb_ref, ext_ref, dext_ref):
        j = pl.program_id(1)
        first = j == nt - 1

        @pl.when(j == 0)
        def _():
            dext_ref[pl.ds(tc, H), :] = jnp.zeros((H, cw), f32)
            dcw_ref[...] = jnp.zeros_like(dcw_ref)
            dcb_ref[...] = jnp.zeros_like(dcb_ref)

        ext_ref[pl.ds(0, H), :] = jnp.where(first, 0.0, gph_ref[...])
        ext_ref[pl.ds(H, tc), :] = gp_ref[...]
        def visit(cols, rows, sums):
            r0, n = rows.start, rows.size
            gp = ext_ref[pl.ds(H + r0, n), cols]
            c = _causal_conv(ext_ref, cw_ref, cb_ref, n, K, cols, r0)
            ge, gg = _gelu_both(c)
            dzv = dz_ref[rows, cols].astype(f32)
            dup_ref[1, rows, cols] = (dzv * ge).astype(bf16)
            dc = dzv * val_ref[rows, cols] * gg
            dext_ref[rows, cols] = dc
            dgp = cw_ref[K - 1:K, cols] * dc
            new = [None] * K + [sums[K] + _fold8(dc)]
            new[K - 1] = sums[K - 1] + _fold8(gp * dc)
            for k in range(K - 1):
                sh = dext_ref[pl.ds(r0 + K - 1 - k, n), cols]
                dgp = dgp + cw_ref[k:k + 1, cols] * sh
                new[k] = sums[k] + _fold8(gp * sh)
            dup_ref[0, rows, cols] = dgp.astype(bf16)
            return new

        def flush(cols, totals):
            for k in range(K):
                dcw_ref[k:k + 1, cols] += totals[k]
            dcb_ref[:, cols] += totals[K]

        _slabs_with_sums(tc, cw, K + 1, visit, flush, reverse=True)
        dext_ref[pl.ds(tc, H), :] = dext_ref[pl.ds(0, H), :]

    hblk = tc // H
    return pl.pallas_call(
        body, name="ffn_bwd", grid=(ns, nt),
        in_specs=[pl.BlockSpec((tc, cw), lambda s, j: (nt - 1 - j, s)),
                  pl.BlockSpec((tc, cw), lambda s, j: (nt - 1 - j, s)),
                  pl.BlockSpec((tc, cw), lambda s, j: (nt - 1 - j, ns + s)),
                  pl.BlockSpec((H, cw), lambda s, j: (jnp.maximum((nt - 1 - j) * hblk - 1, 0), s)),
                  pl.BlockSpec((8, cw), lambda s, j: (0, s)), pl.BlockSpec((1, cw), lambda s, j: (0, s))],
        out_specs=[pl.BlockSpec((2, tc, cw), lambda s, j: (0, nt - 1 - j, s)),
                   pl.BlockSpec((8, cw), lambda s, j: (0, s)), pl.BlockSpec((1, cw), lambda s, j: (0, s))],
        out_shape=[jax.ShapeDtypeStruct((2, T, F), bf16), jax.ShapeDtypeStruct((8, F), f32), jax.ShapeDtypeStruct((1, F), f32)],
        scratch_shapes=[pltpu.VMEM((H + tc, cw), f32), pltpu.VMEM((tc + H, cw), f32)],
        compiler_params=_cparams("parallel", "arbitrary"),
    )(dz, up, up, up, conv_w, conv_b)


def _norm_bwd_matmul(name, a, a_spec, nk, w, w_spec, x, g, dres):
    T, D = x.shape
    tm = a_spec.block_shape[-2]

    def body(a_ref, w_ref, x_ref, g_ref, dr_ref, dx_ref, dxb_ref, dg_ref, acc_ref):
        i, k = pl.program_id(0), pl.program_id(1)

        @pl.when((i == 0) & (k == 0))
        def _():
            dg_ref[...] = jnp.zeros_like(dg_ref)

        @pl.when(k == 0)
        def _():
            acc_ref[...] = _dot_nt(a_ref[...], w_ref[...])

        @pl.when(k > 0)
        def _():
            acc_ref[...] += _dot_nt(a_ref[...], w_ref[...])

        @pl.when(k == nk - 1)
        def _():
            g = g_ref[...]
            dgs = jnp.zeros((_ROWS, D), f32)
            for r0 in range(0, tm, _ROWS):
                rows = pl.ds(r0, _ROWS)
                x = x_ref[rows, :]
                dh = acc_ref[rows, :]
                r = lax.rsqrt(jnp.mean(x * x, axis=-1, keepdims=True) + _EPS)
                gy = dh * g
                dx = dr_ref[rows, :] + (r * gy - x * ((r * r * r) * jnp.mean(x * gy, axis=-1, keepdims=True)))
                dgs = dgs + dh * (x * r)
                dx_ref[rows, :] = dx
                dxb_ref[rows, :] = dx.astype(bf16)
            dg_ref[...] += jnp.sum(dgs, axis=0, keepdims=True)

    row = pl.BlockSpec((tm, D), lambda i, k: (i, 0))
    vec = pl.BlockSpec((1, D), lambda i, k: (0, 0))
    return pl.pallas_call(
        body, name=name, grid=(T // tm, nk),
        in_specs=[a_spec, w_spec, row, vec, row],
        out_specs=[row, row, vec],
        out_shape=[jax.ShapeDtypeStruct((T, D), f32), jax.ShapeDtypeStruct((T, D), bf16), jax.ShapeDtypeStruct((1, D), f32)],
        scratch_shapes=[pltpu.VMEM((tm, D), f32)],
        compiler_params=_cparams("arbitrary", "arbitrary"),
    )(a, w, x, g, dres)


def _merge_bwd(dx2b, w_out, p, q, proj, b_gate, PW):
    T, D = p.shape
    tm, tn = _tile(T, _TM), _tile(D, PW)
    nj = D // tn
    off = (PW + 2 * D) // tn

    def body(dx_ref, w_ref, p_ref, q_ref, l0_ref, l1_ref, b0_ref, b1_ref, dp_ref, dq_ref, dl0_ref, dl1_ref, db0_ref, db1_ref,
             dm_ref):
        @pl.when(pl.program_id(1) == 0)
        def _():
            db0_ref[...] = jnp.zeros_like(db0_ref)
            db1_ref[...] = jnp.zeros_like(db1_ref)

        dm_ref[...] = _dot_nt(dx_ref[...], w_ref[...])

        def visit(cols, rows, sums):
            dm = dm_ref[rows, cols]
            g0 = _sigmoid(l0_ref[rows, cols] + b0_ref[:, cols])
            g1 = _sigmoid(l1_ref[rows, cols] + b1_ref[:, cols])
            dp_ref[rows, cols] = (g0 * dm).astype(bf16)
            dq_ref[rows, cols] = (g1 * dm).astype(bf16)
            dl0 = dm * p_ref[rows, cols].astype(f32) * (g0 * (1.0 - g0))
            dl1 = dm * q_ref[rows, cols].astype(f32) * (g1 * (1.0 - g1))
            dl0_ref[rows, cols] = dl0.astype(bf16)
            dl1_ref[rows, cols] = dl1.astype(bf16)
            return [sums[0] + _fold8(dl0), sums[1] + _fold8(dl1)]

        def flush(cols, totals):
            db0_ref[:, cols] += totals[0]
            db1_ref[:, cols] += totals[1]

        _slabs_with_sums(tm, tn, 2, visit, flush)

    tile = pl.BlockSpec((tm, tn), lambda j, i: (i, j))
    vecj = pl.BlockSpec((1, tn), lambda j, i: (0, j))
    tb = jax.ShapeDtypeStruct((T, D), bf16)
    vb = jax.ShapeDtypeStruct((1, D), f32)
    return pl.pallas_call(
        body, name="merge_bwd", grid=(nj, T // tm),
        in_specs=[pl.BlockSpec((tm, D), lambda j, i: (i, 0)), pl.BlockSpec((tn, D), lambda j, i: (j, 0)), tile, tile,
                  pl.BlockSpec((tm, tn), lambda j, i: (i, off + j)), pl.BlockSpec((tm, tn), lambda j, i: (i, off + nj + j)),
                  vecj, pl.BlockSpec((1, tn), lambda j, i: (0, nj + j))],
        out_specs=[tile, tile, tile, tile, vecj, vecj],
        out_shape=[tb, tb, tb, tb, vb, vb],
        scratch_shapes=[pltpu.VMEM((tm, tn), f32)],
        compiler_params=_cparams("parallel", "arbitrary"),
    )(dx2b, w_out, p, q, proj, proj, b_gate, b_gate)


def _lru_bwd(dy, proj, hs, D, conv_w, conv_b, w_a, b_a, w_i, b_i, lam):
    T = dy.shape[0]
    NB, bw, _ = w_a.shape
    K = 4
    tc = _tile(T, _TC)
    nt = T // tc
    H = _CONV_HALO
    hb = D // 2

    def body(dy_ref, u0_ref, u1_ref, g0_ref, g1_ref, hs_ref, uh0_ref, uh1_ref, hh_ref,
             cw_ref, cb_ref, wa_ref, ba_ref, wi_ref, bi_ref, lam_ref,
             du_ref, dg_ref, dwa_ref, dwi_ref, dcw_ref, dvec_ref,
             ext_ref, hext_ref, dext_ref, a_scr, r_scr, i_scr, v_scr, g_scr, car_scr):
        j = pl.program_id(0)
        first = j == nt - 1

        @pl.when(j == 0)
        def _():
            dext_ref[pl.ds(tc, H), :] = jnp.zeros((H, D), f32)
            car_scr[...] = jnp.zeros_like(car_scr)
            dwa_ref[...] = jnp.zeros_like(dwa_ref)
            dwi_ref[...] = jnp.zeros_like(dwi_ref)
            dcw_ref[...] = jnp.zeros_like(dcw_ref)
            dvec_ref[...] = jnp.zeros_like(dvec_ref)

        ext_ref[pl.ds(0, H), 0:hb] = jnp.where(first, 0.0, uh0_ref[...])
        ext_ref[pl.ds(0, H), hb:D] = jnp.where(first, 0.0, uh1_ref[...])
        ext_ref[pl.ds(H, tc), 0:hb] = u0_ref[...]
        ext_ref[pl.ds(H, tc), hb:D] = u1_ref[...]
        hext_ref[pl.ds(0, H), :] = jnp.where(first, 0.0, hh_ref[...])
        hext_ref[pl.ds(H, tc), :] = hs_ref[...]
        lamv = lam_ref[...]
        sp = _softplus(-lamv)

        for cols, rows in _slabs(tc, D):
            v_scr[rows, cols] = _causal_conv(ext_ref, cw_ref, cb_ref, rows.size, K, cols, rows.start)
        for b in range(NB):
            cols = slice(b * bw, (b + 1) * bw)
            vb = v_scr[:, cols].astype(bf16)
            r_scr[:, cols] = _dot(vb, wa_ref[b])
            i_scr[:, cols] = _dot(vb, wi_ref[b])
        for cols, rows in _slabs(tc, D):
            r = _sigmoid(r_scr[rows, cols] + ba_ref[:, cols])
            r_scr[rows, cols] = r
            i_scr[rows, cols] = _sigmoid(i_scr[rows, cols] + bi_ref[:, cols])
            a_scr[rows, cols] = jnp.exp(-_LRU_C * r * sp[:, cols])
            g_ref, gcols = (g0_ref, cols) if cols.start < hb else (g1_ref, slice(cols.start - hb, cols.stop - hb))
            ge, gg = _gelu_both(g_ref[rows, gcols])
            dyv = dy_ref[rows, cols]
            g_scr[rows, cols] = dyv * ge
            dg_ref[rows, cols] = (dyv * hs_ref[rows, cols] * gg).astype(bf16)

        def step(s, carry):
            t = tc - 1 - s
            g = g_scr[pl.ds(t, 1), :] + carry
            g_scr[pl.ds(t, 1), :] = g
            return a_scr[pl.ds(t, 1), :] * g

        car_scr[0:1, :] = lax.fori_loop(0, tc, step, car_scr[0:1, :], unroll=8)

        def gates(cols, rows, sums):
            g, v, r, i, a = g_scr[rows, cols], v_scr[rows, cols], r_scr[rows, cols], i_scr[rows, cols], a_scr[rows, cols]
            mult = jnp.sqrt(1.0 - a * a)
            h_prev = hext_ref[pl.ds(H - 1 + rows.start, rows.size), cols]
            gm = g * mult
            dext_ref[rows, cols] = gm * i
            dlog_a = (g * h_prev - g * (i * v) * (a / mult)) * a
            dpr = dlog_a * (-_LRU_C * sp[:, cols]) * (r * (1.0 - r))
            dpi = gm * v * (i * (1.0 - i))
            r_scr[rows, cols] = dpr
            i_scr[rows, cols] = dpi
            return [sums[0] + _fold8(dpr), sums[1] + _fold8(dpi), sums[2] + _fold8(dlog_a * (-_LRU_C * r))]

        def gates_flush(cols, totals):
            dvec_ref[1:2, cols] += totals[0]
            dvec_ref[2:3, cols] += totals[1]
            dvec_ref[3:4, cols] += totals[2] * (-_sigmoid(-lamv[:, cols]))

        _slabs_with_sums(tc, D, 3, gates, gates_flush)
        for b in range(NB):
            cols = slice(b * bw, (b + 1) * bw)
            dprb, dpib, vb = r_scr[:, cols].astype(bf16), i_scr[:, cols].astype(bf16), v_scr[:, cols].astype(bf16)
            dext_ref[pl.ds(0, tc), cols] += _dot_nt(dprb, wa_ref[b]) + _dot_nt(dpib, wi_ref[b])
            dwa_ref[b] += _dot_tn(vb, dprb)
            dwi_ref[b] += _dot_tn(vb, dpib)

        def conv_t(cols, rows, sums):
            r0, n = rows.start, rows.size
            u = ext_ref[pl.ds(H + r0, n), cols]
            dv = dext_ref[rows, cols]
            du = cw_ref[K - 1:K, cols] * dv
            new = [None] * K + [sums[K] + _fold8(dv)]
            new[K - 1] = sums[K - 1] + _fold8(u * dv)
            for k in range(K - 1):
                sh = dext_ref[pl.ds(r0 + K - 1 - k, n), cols]
                du = du + cw_ref[k:k + 1, cols] * sh
                new[k] = sums[k] + _fold8(u * sh)
            du_ref[rows, cols] = du.astype(bf16)
            return new

        def conv_t_flush(cols, totals):
            for k in range(K):
                dcw_ref[k:k + 1, cols] += totals[k]
            dvec_ref[0:1, cols] += totals[K]

        _slabs_with_sums(tc, D, K + 1, conv_t, conv_t_flush)
        dext_ref[pl.ds(tc, H), :] = dext_ref[pl.ds(0, H), :]

    hblk = tc // H
    rev = lambda j: nt - 1 - j
    halo = lambda j: jnp.maximum((nt - 1 - j) * hblk - 1, 0)
    vec = pl.BlockSpec((1, D), lambda j: (0, 0))
    wspec = pl.BlockSpec((NB, bw, bw), lambda j: (0, 0, 0))
    acc8 = pl.BlockSpec((8, D), lambda j: (0, 0))
    big = pltpu.VMEM((tc, D), f32)
    return pl.pallas_call(
        body, name="lru_bwd", grid=(nt,),
        in_specs=[pl.BlockSpec((tc, D), lambda j: (rev(j), 0)),
                  pl.BlockSpec((tc, hb), lambda j: (rev(j), 1)), pl.BlockSpec((tc, hb), lambda j: (rev(j), 2)),
                  pl.BlockSpec((tc, hb), lambda j: (rev(j), 3)), pl.BlockSpec((tc, hb), lambda j: (rev(j), 4)),
                  pl.BlockSpec((tc, D), lambda j: (rev(j), 0)),
                  pl.BlockSpec((H, hb), lambda j: (halo(j), 1)), pl.BlockSpec((H, hb), lambda j: (halo(j), 2)),
                  pl.BlockSpec((H, D), lambda j: (halo(j), 0)),
                  acc8, vec, wspec, vec, wspec, vec, vec],
        out_specs=[pl.BlockSpec((tc, D), lambda j: (rev(j), 0)), pl.BlockSpec((tc, D), lambda j: (rev(j), 0)),
                   wspec, wspec, acc8, acc8],
        out_shape=[jax.ShapeDtypeStruct((T, D), bf16), jax.ShapeDtypeStruct((T, D), bf16),
                   jax.ShapeDtypeStruct((NB, bw, bw), f32), jax.ShapeDtypeStruct((NB, bw, bw), f32),
                   jax.ShapeDtypeStruct((8, D), f32), jax.ShapeDtypeStruct((8, D), f32)],
        scratch_shapes=[pltpu.VMEM((H + tc, D), f32), pltpu.VMEM((H + tc, D), f32), pltpu.VMEM((tc + H, D), f32),
                        big, big, big, big, big, pltpu.VMEM((8, D), f32)],
        compiler_params=_cparams("arbitrary"),
    )(dy, proj, proj, proj, proj, hs, proj, proj, hs, conv_w, conv_b, w_a, b_a, w_i, b_i, lam)


def _pool_bwd(dy, proj, w_pool, pool_scale):
    T, PW = dy.shape
    G, gw, _ = w_pool.shape
    tc = _tile(T, _TC)
    nt = T // tc
    H = _POOL_HALO

    def body(dy_ref, u_ref, uh_ref, w_ref, s_ref, du_ref, dw_ref, ds_ref, ext_ref, eext_ref):
        j = pl.program_id(0)
        first = j == nt - 1
        row0 = (nt - 1 - j) * tc

        @pl.when(j == 0)
        def _():
            eext_ref[pl.ds(tc, H), :] = jnp.zeros((H, PW), f32)
            dw_ref[...] = jnp.zeros_like(dw_ref)
            ds_ref[...] = jnp.zeros_like(ds_ref)

        ext_ref[pl.ds(0, H), :] = jnp.where(first, 0.0, uh_ref[...])
        ext_ref[pl.ds(H, tc), :] = u_ref[...]
        t_glob = row0 + lax.broadcasted_iota(jnp.int32, (tc, 1), 0)
        dds = []
        for g, (m, u) in enumerate(_window_means(ext_ref, row0, tc, gw)):
            cols = slice(g * gw, (g + 1) * gw)
            d = (m - u).astype(bf16)
            yraw = _dot(d, w_ref[g])
            dyv = dy_ref[:, cols]
            ds_ref[:, cols] += jnp.sum(dyv * yraw, axis=0, keepdims=True)
            dyr = (dyv * s_ref[:, cols]).astype(bf16)
            dd = _dot_nt(dyr, w_ref[g])
            dw_ref[g] += _dot_tn(d, dyr)
            cnt = jnp.minimum(t_glob + 1, _POOL_WINDOWS[g]).astype(f32)
            eext_ref[pl.ds(0, tc), cols] = dd / cnt
            dds.append(dd)
        n = tc + H
        for g, w in enumerate(_POOL_WINDOWS):
            cols = slice(g * gw, (g + 1) * gw)
            s = eext_ref[:, cols]
            st = 1
            while st < w:
                s = s + pltpu.roll(s, n - st, 0)
                st *= 2
            du_ref[:, cols] = (s[0:tc, :] - dds[g]).astype(bf16)
        eext_ref[pl.ds(tc, H), :] = eext_ref[pl.ds(0, H), :]

    hblk = tc // H
    return pl.pallas_call(
        body, name="pool_bwd", grid=(nt,),
        in_specs=[pl.BlockSpec((tc, PW), lambda j: (nt - 1 - j, 0)), pl.BlockSpec((tc, PW), lambda j: (nt - 1 - j, 0)),
                  pl.BlockSpec((H, PW), lambda j: (jnp.maximum((nt - 1 - j) * hblk - 1, 0), 0)),
                  pl.BlockSpec((G, gw, gw), lambda j: (0, 0, 0)), pl.BlockSpec((1, PW), lambda j: (0, 0))],
        out_specs=[pl.BlockSpec((tc, PW), lambda j: (nt - 1 - j, 0)), pl.BlockSpec((G, gw, gw), lambda j: (0, 0, 0)),
                   pl.BlockSpec((1, PW), lambda j: (0, 0))],
        out_shape=[jax.ShapeDtypeStruct((T, PW), bf16), jax.ShapeDtypeStruct((G, gw, gw), f32), jax.ShapeDtypeStruct((1, PW), f32)],
        scratch_shapes=[pltpu.VMEM((H + tc, PW), f32), pltpu.VMEM((tc + H, PW), f32)],
        compiler_params=_cparams("arbitrary"),
    )(dy, proj, proj, w_pool, pool_scale)


_MESH = pl.DeviceIdType.MESH
_HBM = pl.BlockSpec(memory_space=pltpu.HBM)


def _slab(ref, kind, blk, n):
    start = blk * n
    if n % _LANE == 0:
        start = pl.multiple_of(start, _LANE)
    if kind == "col":
        return ref.at[:, pl.ds(start, n)]
    if kind == "row":
        return ref.at[pl.ds(start, n), :]
    if kind == "mid":
        return ref.at[:, pl.ds(start, n), :]
    raise ValueError(kind)


def _my_place():
    x, y, c = lax.axis_index("x"), lax.axis_index("y"), lax.axis_index("c")
    return x, y, c


def _blk(px, py, pc):
    return 4 * px + 2 * py + pc


_SEM = pl.BlockSpec(memory_space=pltpu.SEMAPHORE)
_EFFECT = pltpu.SideEffectType.DATAFLOW_SIDE_EFFECTING


_ALL_PEERS = (1, 2, 3, 4, 5, 6, 7)


def _peers(x, y, c):
    return [(k, (x ^ (k >> 2), y ^ ((k >> 1) & 1), c ^ (k & 1))) for k in range(1, 8)]


class _Route:
    def __init__(self, mode, kind, size):
        self.mode, self.kind, self.size = mode, kind, size

    def src(self, ref, peer_blk):
        return ref if self.mode == "gather" else _slab(ref, self.kind, peer_blk, self.size)

    def dst(self, ref, origin_blk):
        return _slab(ref, self.kind, origin_blk, self.size) if self.mode == "gather" else ref.at[origin_blk]


def _send_start(name, srcs, lands, routes, groups):
    nt, ng = len(srcs), len(groups)

    def body(*refs):
        src_refs, land_refs = refs[:nt], refs[nt:2 * nt]
        sems = refs[2 * nt:2 * nt + 2 * ng]
        token = refs[-1]
        x, y, c = _my_place()
        me = _blk(x, y, c)
        for gi, (grp, ks) in enumerate(groups):
            for pos, t in enumerate(grp):
                for k, peer in _peers(x, y, c):
                    if k in ks:
                        s = len(ks) * pos + ks.index(k)
                        pltpu.make_async_remote_copy(
                            src_ref=routes[t].src(src_refs[t], _blk(*peer)), dst_ref=routes[t].dst(land_refs[t], me),
                            send_sem=sems[2 * gi].at[s], recv_sem=sems[2 * gi + 1].at[s],
                            device_id=peer, device_id_type=_MESH).start()
        token[...] = jnp.zeros_like(token)

    hbm = lambda a: pltpu.HBM(a.shape, a.dtype)
    out_shape = []
    for grp, ks in groups:
        out_shape += [pltpu.SemaphoreType.DMA((len(ks) * len(grp),)), pltpu.SemaphoreType.DMA((len(ks) * len(grp),))]
    out_shape += [hbm(a) for a in srcs] + [hbm(a) for a in lands] + [jax.ShapeDtypeStruct((8, _LANE), f32)]
    res = pl.pallas_call(
        body, name=name, out_shape=out_shape,
        in_specs=[_HBM] * (2 * nt),
        out_specs=[_SEM] * (2 * ng) + [_HBM] * (2 * nt) + [pl.BlockSpec(memory_space=pltpu.VMEM)],
        input_output_aliases={t: 2 * ng + t for t in range(2 * nt)},
        compiler_params=pltpu.CompilerParams(has_side_effects=_EFFECT),
    )(*[pltpu.with_memory_space_constraint(a, pltpu.HBM) for a in list(srcs) + list(lands)])
    sems = [(res[2 * g], res[2 * g + 1]) for g in range(ng)]
    return sems, res[2 * ng:2 * ng + nt], res[2 * ng + nt:2 * ng + 2 * nt], res[-1]


def _send_wait(name, srcs, lands, routes, sems, after, ks=_ALL_PEERS):
    n = len(srcs)

    def body(*refs):
        src_refs, land_refs = refs[:n], refs[n:2 * n]
        send_sems, recv_sems = refs[2 * n], refs[2 * n + 1]
        x, y, c = _my_place()
        for pos in range(n):
            for k, peer in _peers(x, y, c):
                if k in ks:
                    pb = _blk(*peer)
                    s = len(ks) * pos + ks.index(k)
                    cp = pltpu.make_async_remote_copy(
                        src_ref=routes[pos].src(src_refs[pos], pb), dst_ref=routes[pos].dst(land_refs[pos], pb),
                        send_sem=send_sems.at[s], recv_sem=recv_sems.at[s], device_id=peer, device_id_type=_MESH)
                    cp.wait_send()
                    cp.wait_recv()

    hbm = lambda a: pltpu.HBM(a.shape, a.dtype)
    res = pl.pallas_call(
        body, name=name, out_shape=[hbm(a) for a in srcs] + [hbm(a) for a in lands],
        in_specs=[_HBM] * (2 * n) + [_SEM, _SEM, pl.BlockSpec(memory_space=pl.ANY)],
        out_specs=[_HBM] * (2 * n),
        input_output_aliases={t: t for t in range(2 * n)},
        compiler_params=pltpu.CompilerParams(has_side_effects=_EFFECT),
    )(*srcs, *lands, sems[0], sems[1], after)
    return res[:n], res[n:]


def _copy_own(name, src, land, route, me):
    gather = route.mode == "gather"
    shard = src.shape if gather else land.shape[1:]
    lead = () if gather else (None,)

    if route.kind == "mid":
        grid = (1,)
        at_full = lambda i, me: (0, me[0], 0)
        at_shard = lambda i, me: (0, 0, 0)
        block = tuple(shard)
    else:
        rows, width = shard
        tr = _tile(rows, 512, 16)
        grid = (rows // tr,)
        block = (tr, width)
        if route.kind == "col":
            at_full = lambda i, me: (i, me[0])
        else:
            at_full = lambda i, me: (me[0] * grid[0] + i, 0)
        at_shard = lambda i, me: (i, 0)
    if gather:
        in_map, out_map = at_shard, at_full
    else:
        in_map, out_map = at_full, (lambda i, me: (me[0], *at_shard(i, me)))

    def body(me_ref, src_ref, land_ref, out_ref):
        out_ref[...] = src_ref[...]

    return pl.pallas_call(
        body, name=name, out_shape=jax.ShapeDtypeStruct(land.shape, land.dtype),
        grid_spec=pltpu.PrefetchScalarGridSpec(
            num_scalar_prefetch=1, grid=grid,
            in_specs=[pl.BlockSpec(block, in_map), pl.BlockSpec(memory_space=pl.ANY)],
            out_specs=pl.BlockSpec(lead + block, out_map)),
        input_output_aliases={2: 0},
        compiler_params=_cparams("arbitrary"),
    )(me, src, land)


def _place_own(name, srcs, lands, routes):
    me = _blk(*_my_place()).astype(jnp.int32).reshape(1)
    return [_copy_own(f"{name}_{t}", s, l, r, me) for t, (s, l, r) in enumerate(zip(srcs, lands, routes))]


def _exchange(fulls, kinds, sizes, whole):
    arrays = list(fulls) + list(whole)
    nt, nf = len(arrays), len(fulls)

    def shard_shape(t):
        s = list(arrays[t].shape)
        if t < nf:
            s[{"col": 1, "row": 0, "mid": 1}[kinds[t]]] = sizes[t]
        return tuple(s)

    def body(*refs):
        ins, outs = refs[:nt], refs[nt:2 * nt]
        send_sems, recv_sems, local_sems = refs[2 * nt:]
        x, y, c = _my_place()
        me = _blk(x, y, c)

        def src(t, blk):
            return _slab(ins[t], kinds[t], blk, sizes[t]) if t < nf else ins[t]

        mine = [pltpu.make_async_copy(src(t, me), outs[t].at[me], local_sems.at[t]) for t in range(nt)]
        for cp in mine:
            cp.start()
        sent = []
        for t in range(nt):
            for k in range(1, 8):
                peer = (x ^ (k >> 2), y ^ ((k >> 1) & 1), c ^ (k & 1))
                pb = _blk(*peer)
                cp = pltpu.make_async_remote_copy(
                    src_ref=src(t, pb), dst_ref=outs[t].at[me],
                    send_sem=send_sems.at[7 * t + k - 1], recv_sem=recv_sems.at[7 * t + k - 1],
                    device_id=peer, device_id_type=_MESH)
                cp.start()
                sent.append((cp, t, k, pb))
        for cp, t, k, pb in sent:
            pltpu.make_async_remote_copy(
                src_ref=src(t, pb), dst_ref=outs[t].at[pb],
                send_sem=send_sems.at[7 * t + k - 1], recv_sem=recv_sems.at[7 * t + k - 1],
                device_id=(x, y, c), device_id_type=_MESH).wait_recv()
        for cp, _, _, _ in sent:
            cp.wait_send()
        for cp in mine:
            cp.wait()

    return pl.pallas_call(
        body, name="exchange_grads",
        in_specs=[_HBM] * nt, out_specs=[_HBM] * nt,
        out_shape=[jax.ShapeDtypeStruct((_N_DEV,) + shard_shape(t), arrays[t].dtype) for t in range(nt)],
        scratch_shapes=[pltpu.SemaphoreType.DMA((7 * nt,)), pltpu.SemaphoreType.DMA((7 * nt,)), pltpu.SemaphoreType.DMA((nt,))],
        compiler_params=pltpu.CompilerParams(has_side_effects=True),
    )(*arrays)


def _sum_parts(name, parts):
    n, R, C = parts.shape
    tr = _tile(R, 256, 8)

    def body(p_ref, o_ref):
        g = p_ref[0].astype(f32)
        for s in range(1, n):
            g = g + p_ref[s].astype(f32)
        o_ref[...] = g

    return pl.pallas_call(
        body, name=name, grid=(R // tr,),
        in_specs=[pl.BlockSpec((n, tr, C), lambda i: (0, i, 0))],
        out_specs=pl.BlockSpec((tr, C), lambda i: (i, 0)),
        out_shape=jax.ShapeDtypeStruct((R, C), f32),
        compiler_params=_cparams("parallel"),
    )(parts)


def _adamw(name, w, m, v, parts):
    R, C = w.shape
    n = parts.shape[0]
    tr = _tile(R, 256, 8)
    c1 = 1.0 - _ADAM_B1 ** _ADAM_STEP
    c2 = 1.0 - _ADAM_B2 ** _ADAM_STEP

    def body(w_ref, m_ref, v_ref, p_ref, g_ref, d_ref, nm_ref, nv_ref):
        g = p_ref[0].astype(f32)
        for s in range(1, n):
            g = g + p_ref[s].astype(f32)
        nm = _ADAM_B1 * m_ref[...] + (1.0 - _ADAM_B1) * g
        nv = _ADAM_B2 * v_ref[...] + (1.0 - _ADAM_B2) * (g * g)
        g_ref[...] = g
        nm_ref[...] = nm
        nv_ref[...] = nv
        d_ref[...] = -_ADAM_LR * ((nm / c1) / (jnp.sqrt(nv / c2) + _ADAM_EPS) + _ADAM_WD * w_ref[...])

    blk = pl.BlockSpec((tr, C), lambda i: (i, 0))
    sd = jax.ShapeDtypeStruct((R, C), f32)
    return pl.pallas_call(
        body, name=name, grid=(R // tr,),
        in_specs=[blk, blk, blk, pl.BlockSpec((n, tr, C), lambda i: (0, i, 0))],
        out_specs=[blk, blk, blk, blk], out_shape=[sd, sd, sd, sd],
        compiler_params=_cparams("parallel"),
    )(w, m, v, parts)


def _pack(vectors):
    flat = jnp.concatenate([a.reshape(-1).astype(f32) for a in vectors])
    pad = (-flat.shape[0]) % _SMALL_PAD
    return jnp.pad(flat, (0, pad)).reshape(-1, _LANE)


def _unpack(packed, shapes):
    flat = packed.reshape(-1)
    out, o = [], 0
    for s in shapes:
        n = 1
        for d in s:
            n *= d
        out.append(flat[o:o + n].reshape(s))
        o += n
    return out


def _pad_rows8(a):
    return jnp.pad(a, ((0, 8 - a.shape[0]), (0, 0)))


def _local_step(x, target, p, get, emit):
    T, D = x.shape

    def tie(a, *tokens):
        for tok in tokens:
            if tok is not None:
                a = a + tok[0, 0]
        return a

    n_parts = _N_DEV // _W_IN_PART
    w_in, base = get("w_in", x, 0)
    h1, proj = _in_proj_first(x, p["g_mix"], w_in, base, _W_IN_PART)
    for part in range(1, n_parts):
        w_in, base = get("w_in", proj, part)
        proj = _in_proj_more(f"in_proj_{part}", h1, w_in, proj, base, _W_IN_PART)
    w_pool = get("w_pool", proj)
    PW = w_pool.shape[0] * w_pool.shape[1]
    y_pool = _pool_fwd(proj, w_pool, p["pool_scale"])
    lru_conv_w, w_a, w_i = get("lru_conv_w", proj), get("w_a", proj), get("w_i", proj)
    y_lru, hs = _lru_fwd(proj, D, PW, lru_conv_w, p["lru_conv_b"], w_a, p["b_a"], w_i, p["b_i"], p["lru_lambda"])
    w_pp, w_lp = get("w_pool_proj", proj), get("w_lru_proj", proj)
    pp, qq, merged = _merge_fwd(y_pool, y_lru, w_pp, w_lp, proj, p["b_gate"])
    w_out = get("w_out", proj)
    x2, h2 = _out_proj(merged, w_out, x, p["g_mlp"])
    w_up = get("w_up", x2)
    up = _mm_nn("up_proj", h2, w_up, f32, tm_want=2 * _TM)
    ffn_conv_w = get("ffn_conv_w", proj)
    z = _ffn_fwd(up, ffn_conv_w, p["ffn_conv_b"])
    w_down = get("w_down", up)
    F = w_down.shape[0]
    dx3, dx3b, loss_t, dg_final = _down_loss(z, w_down, x2, target, p["g_final"])

    gs = {"g_final": dg_final}
    tok = emit("w_down", _mm_tn("dw_down", z, dx3b, bf16, tm_want=1536))
    dz = _mm_nt("dz", dx3b, w_down, bf16)
    dup, dcw_ffn, dcb_ffn = _ffn_bwd(dz, up, ffn_conv_w, tie(p["ffn_conv_b"], tok))
    gs["ffn_conv_w"] = dcw_ffn[0:3]
    gs["ffn_conv_b"] = dcb_ffn

    tm = _tile(T, _TM)
    tk = _tile(F, _TN_MAX)
    nkh = F // tk
    tkt = _tile(T, _TK_T)
    tok = emit("w_up", _matmul(
        "dw_up", "tn", (h2, dup),
        [pl.BlockSpec((tkt, D), lambda i, j, k: (k, 0)), pl.BlockSpec((None, tkt, tk), lambda i, j, k: (j // nkh, k, j % nkh))],
        jax.ShapeDtypeStruct((D, 2 * F), bf16), pl.BlockSpec((D, tk), lambda i, j, k: (0, j)),
        (1, 2 * nkh, T // tkt), (D, tk)))

    tkc = _tile(F, _TK_UP)
    nkc = F // tkc
    dx2, dx2b, gs["g_mlp"] = _norm_bwd_matmul(
        "dh2", dup, pl.BlockSpec((None, tm, tkc), lambda i, k: (k // nkc, i, k % nkc)), 2 * nkc,
        w_up, pl.BlockSpec((D, tkc), lambda i, k: (0, k)), x2, tie(p["g_mlp"], tok), dx3)

    tok = emit("w_out", _mm_tn("dw_out", merged, dx2b, bf16))
    dP, dQ, dl0, dl1, db0, db1 = _merge_bwd(dx2b, w_out, pp, qq, proj, tie(p["b_gate"], tok), PW)
    gs["b_gate"] = jnp.concatenate([db0, db1], axis=1)
    tok = emit("w_pool_proj", _mm_tn("dw_pool_proj", y_pool, dP, bf16))
    tok2 = emit("w_lru_proj", _mm_tn("dw_lru_proj", y_lru, dQ, bf16))
    dy_pool = _mm_nt("dy_pool", dP, w_pp, f32)
    dy_lru = _mm_nt("dy_lru", dQ, w_lp, f32)

    du_lru, du_gelu, dwa, dwi, dcw_lru, dvec = _lru_bwd(
        dy_lru, proj, hs, D, lru_conv_w, tie(p["lru_conv_b"], tok, tok2), w_a, p["b_a"], w_i, p["b_i"], p["lru_lambda"])
    tok = emit("w_a", dwa.astype(bf16))
    tok2 = emit("w_i", dwi.astype(bf16))
    gs["lru_conv_w"] = dcw_lru[0:4]
    gs["lru_conv_b"], gs["b_a"], gs["b_i"], gs["lru_lambda"] = dvec[0:1], dvec[1:2], dvec[2:3], dvec[3:4]
    du_pool, dwp, gs["pool_scale"] = _pool_bwd(dy_pool, proj, w_pool, tie(p["pool_scale"], tok, tok2))
    tok = emit("w_pool", dwp.astype(bf16))

    dproj = jnp.concatenate([du_pool, du_lru, du_gelu, dl0, dl1], axis=1)
    tok2 = emit("w_in", _mm_tn("dw_in", h1, dproj, bf16))
    NI = dproj.shape[1]
    tki = _tile(NI, _TN_MAX)
    grad_x, _, gs["g_mix"] = _norm_bwd_matmul(
        "dh1", dproj, pl.BlockSpec((tm, tki), lambda i, k: (i, k)), NI // tki,
        w_in, pl.BlockSpec((D, tki), lambda i, k: (0, k)), x, tie(p["g_mix"], tok, tok2), dx2)
    return loss_t[0, 0], grad_x, gs


_MATRICES = {"w_in": "col", "w_pool": "mid", "w_a": "mid", "w_i": "mid", "w_pool_proj": "col", "w_lru_proj": "row",
             "w_out": "row", "w_up": "col", "w_down": "row"}
_CONVS = ("lru_conv_w", "ffn_conv_w")
_GATHER_GROUPS = (("w_pool", "lru_conv_w", "w_a", "w_i"), ("w_pool_proj", "w_lru_proj", "w_out", "ffn_conv_w"),
                  ("w_up",), ("w_down",))
_W_IN_PARTS = ((1,), (4, 5), (2, 3), (6, 7))
_W_IN_PART = 2
_VECTORS = ("g_mix", "b_gate", "pool_scale", "lru_conv_b", "b_a", "b_i", "lru_lambda", "g_mlp", "ffn_conv_b", "g_final")
_WEIGHTS = ("g_mix", "w_in", "b_gate", "w_pool", "pool_scale", "lru_conv_w", "lru_conv_b", "w_a", "b_a", "w_i", "b_i",
            "lru_lambda", "w_pool_proj", "w_lru_proj", "w_out", "g_mlp", "w_up", "ffn_conv_w", "ffn_conv_b", "w_down", "g_final")


def _full_shape(shape, kind):
    s = list(shape)
    s[{"col": 1, "row": 0, "mid": 1}[kind]] *= _N_DEV
    return tuple(s)


def _step(x, target, w, m, v):
    x, target = x[0], target[0]
    me = _blk(*_my_place())

    axis = {"col": 1, "row": 0, "mid": 1}
    kind = dict(_MATRICES, **{n: "col" for n in _CONVS})
    shard = {n: w[n].astype(bf16) for n in _MATRICES}
    shard.update({n: _pad_rows8(w[n]) for n in _CONVS})
    order = ["w_in"] + [n for grp in _GATHER_GROUPS for n in grp]
    index = {n: t for t, n in enumerate(order)}
    n_parts = len(_W_IN_PARTS)
    groups = [([0], ks) for ks in _W_IN_PARTS] + [([index[n] for n in grp], _ALL_PEERS) for grp in _GATHER_GROUPS]
    g_routes = [_Route("gather", kind[n], shard[n].shape[axis[kind[n]]]) for n in order]
    lands = [lax.empty(_full_shape(shard[n].shape, kind[n]), shard[n].dtype) for n in order]
    g_sems, g_srcs, g_lands, _ = _send_start("gather_start", [shard[n] for n in order], lands, g_routes, groups)
    gathered = {}
    x_, y_, _c = _my_place()
    w_in_state = [[g_srcs[0]], [g_lands[0]]]

    def get(name, after, part=None):
        if name == "w_in":
            ks = _W_IN_PARTS[part]
            srcs, got = _send_wait(f"gather_wait_w_in_{part}", *w_in_state, g_routes[:1], g_sems[part], after, ks)
            if part == 0:
                got = _place_own("gather_own_w_in", srcs, got, g_routes[:1])
            w_in_state[:] = [srcs, got]
            k = ks[-1]
            base = 4 * (x_ ^ (k >> 2)) + 2 * (y_ ^ ((k >> 1) & 1))
            return got[0], base.astype(jnp.int32).reshape(1)
        if name not in gathered:
            gi = next(i for i, grp in enumerate(_GATHER_GROUPS) if name in grp)
            ts = groups[n_parts + gi][0]
            routes = [g_routes[t] for t in ts]
            srcs = [g_srcs[t] for t in ts]
            srcs, got = _send_wait(f"gather_wait_{gi}", srcs, [g_lands[t] for t in ts], routes, g_sems[n_parts + gi], after)
            full = _place_own(f"gather_own_{gi}", srcs, got, routes)
            gathered.update(zip(_GATHER_GROUPS[gi], full))
        return gathered[name]

    sent = {}

    def emit(name, grad):
        k = _MATRICES[name]
        size = w[name].shape[axis[k]]
        route = _Route("scatter", k, size)
        shp = list(grad.shape)
        shp[axis[k]] = size
        land = lax.empty((_N_DEV, *shp), grad.dtype)
        sems, srcs, lnds, token = _send_start("grad_start_" + name, [grad], [land], [route], [([0], _ALL_PEERS)])
        sent[name] = (srcs, lnds, [route], sems[0])
        return token

    p = {n: w[n].reshape(1, -1) for n in _VECTORS}
    loss_t, grad_x, gs = _local_step(x, target, p, get, emit)
    loss = lax.psum(loss_t, ("x", "y", "c"))

    small_names = list(_VECTORS) + list(_CONVS)
    small_shapes = [tuple(gs[n].shape) for n in small_names]
    small_parts = _exchange([], [], [], [_pack([gs[n] for n in small_names])])[0]

    out = {}
    mats = list(_MATRICES)
    for n in mats:
        srcs, lnds, routes, sems = sent[n]
        srcs, got = _send_wait("grad_wait_" + n, srcs, lnds, routes, sems, grad_x)
        parts = _place_own("grad_own_" + n, srcs, got, routes)[0]
        shp = w[n].shape
        r2 = (-1, shp[-1])
        res = _adamw("adamw_" + n, w[n].reshape(r2), m[n].reshape(r2), v[n].reshape(r2),
                     parts.reshape((_N_DEV,) + w[n].reshape(r2).shape))
        out[n] = [a.reshape(shp) for a in res]
    gsum = _unpack(_sum_parts("sum_small", small_parts), small_shapes)
    gsmall = dict(zip(small_names, gsum))
    for n in _CONVS:
        cols = w[n].shape[1]
        gsmall[n] = lax.dynamic_slice_in_dim(gsmall[n], me * cols, cols, axis=1)
    pk = lambda d: _pack([d[n] for n in small_names])
    res = _adamw("adamw_small", pk(w), pk(m), pk(v), pk(gsmall)[None])
    shapes = [tuple(w[n].shape) for n in small_names]
    for k, arr in enumerate(res):
        for n, a in zip(small_names, _unpack(arr, shapes)):
            out.setdefault(n, [None] * 4)[k] = a
    return loss, grad_x[None], out


def kernel(x, g_mix, w_in, b_gate, w_pool, pool_scale, lru_conv_w, lru_conv_b, w_a, b_a, w_i, b_i, lru_lambda, w_pool_proj, w_lru_proj, w_out, g_mlp, w_up, ffn_conv_w, ffn_conv_b, w_down, g_final, loss_target, m_g_mix, m_w_in, m_b_gate, m_w_pool, m_pool_scale, m_lru_conv_w, m_lru_conv_b, m_w_a, m_b_a, m_w_i, m_b_i, m_lru_lambda, m_w_pool_proj, m_w_lru_proj, m_w_out, m_g_mlp, m_w_up, m_ffn_conv_w, m_ffn_conv_b, m_w_down, m_g_final, v_g_mix, v_w_in, v_b_gate, v_w_pool, v_pool_scale, v_lru_conv_w, v_lru_conv_b, v_w_a, v_b_a, v_w_i, v_b_i, v_lru_lambda, v_w_pool_proj, v_w_lru_proj, v_w_out, v_g_mlp, v_w_up, v_ffn_conv_w, v_ffn_conv_b, v_w_down, v_g_final):
    given = dict(locals())
    orig = {n: given[n].shape for n in _WEIGHTS}

    def squeeze(a, n):
        return a if n == "g_final" else a[0]

    w = {n: squeeze(given[n], n) for n in _WEIGHTS}
    m = {n: squeeze(given["m_" + n], n) for n in _WEIGHTS}
    v = {n: squeeze(given["v_" + n], n) for n in _WEIGHTS}
    for d in (w, m, v):
        d["g_final"] = d["g_final"].reshape(1, -1)
    loss, grad_x, out = _step(x, loss_target, w, m, v)
    res = [loss, grad_x]
    for k in range(4):
        res += [out[n][k].reshape(orig[n]) for n in _WEIGHTS]
    return tuple(res)
```

```python
import functools

import jax
import jax.numpy as jnp
from jax import lax
from jax.experimental import pallas as pl
from jax.experimental.pallas import tpu as pltpu

f32 = jnp.float32
bf16 = jnp.bfloat16

_EPS = 1e-6
_LRU_C = 8.0
_POOL_WINDOWS = (2, 4, 8, 16)
_POOL_HALO = 16
_CONV_HALO = 8
_GELU_C0 = 0.7978845608028654
_GELU_C1 = 0.044715
_ADAM_LR, _ADAM_B1, _ADAM_B2, _ADAM_EPS, _ADAM_WD, _ADAM_STEP = 0.001, 0.9, 0.999, 1e-08, 0.01, 10
_N_DEV = 8
_LANE = 128
_SMALL_PAD = 8 * _LANE
_VMEM_LIMIT = 60 * 1024 * 1024

_TM = 512
_TM_SMALL = 256
_TC = 256
_TK_T = 512
_TN_MAX = 1536
_CW = 1024
_TK_DOWN = 768
_TK_UP = 2048


def _cparams(*sem):
    return pltpu.CompilerParams(dimension_semantics=tuple(sem), vmem_limit_bytes=_VMEM_LIMIT)


def _tile(n, want, mult=1):
    if n <= want:
        return n
    t = want - want % mult
    while n % t:
        t -= mult
    return t


def _gelu(x):
    return 0.5 * x * (1.0 + jnp.tanh(_GELU_C0 * (x + _GELU_C1 * (x * x * x))))


def _gelu_both(x):
    x2 = x * x
    t = jnp.tanh(_GELU_C0 * (x + _GELU_C1 * (x2 * x)))
    h = 0.5 * (1.0 + t)
    return x * h, h + 0.5 * x * (1.0 - t * t) * (_GELU_C0 * (1.0 + 3.0 * _GELU_C1 * x2))


def _fold8(x):
    out = x[0:8]
    for r in range(8, x.shape[0], 8):
        out = out + x[r:r + 8]
    return out


def _sigmoid(x):
    return jax.nn.sigmoid(x)


def _dot(a, b):
    return jnp.dot(a, b, preferred_element_type=f32)


def _dot_nt(a, b):
    return lax.dot_general(a, b, (((1,), (1,)), ((), ())), preferred_element_type=f32)


def _dot_tn(a, b):
    return lax.dot_general(a, b, (((0,), (0,)), ((), ())), preferred_element_type=f32)


def _rms_rows(x_ref, g_ref, h_ref, n_rows, n_cols):
    rr = _tile(n_rows, _SLAB_ROWS)
    cg = _tile(n_cols, _SLAB_COLS)
    for r0 in range(0, n_rows, rr):
        rows = pl.ds(r0, rr)
        x = x_ref[rows, :]
        r = lax.rsqrt(jnp.mean(x * x, axis=-1, keepdims=True) + _EPS)
        for c0 in range(0, n_cols, cg):
            cols = slice(c0, c0 + cg)
            h_ref[rows, cols] = (x_ref[rows, cols] * r * g_ref[:, cols]).astype(bf16)


def _matmul(name, mode, operands, in_specs, out_shape, out_spec, grid, acc_shape):
    dot = {"nn": _dot, "nt": _dot_nt, "tn": _dot_tn}[mode]
    nk = grid[2]

    def body_whole(a_ref, b_ref, o_ref):
        o_ref[...] = dot(a_ref[...], b_ref[...]).astype(o_ref.dtype)

    def body(a_ref, b_ref, o_ref, acc_ref):
        k = pl.program_id(2)

        @pl.when(k == 0)
        def _():
            acc_ref[...] = dot(a_ref[...], b_ref[...])

        @pl.when((k > 0) & (k < nk - 1))
        def _():
            acc_ref[...] += dot(a_ref[...], b_ref[...])

        @pl.when(k == nk - 1)
        def _():
            o_ref[...] = (acc_ref[...] + dot(a_ref[...], b_ref[...])).astype(o_ref.dtype)

    return pl.pallas_call(
        body_whole if nk == 1 else body, name=name, grid=grid, in_specs=in_specs, out_specs=out_spec, out_shape=out_shape,
        scratch_shapes=[] if nk == 1 else [pltpu.VMEM(acc_shape, f32)],
        compiler_params=_cparams("parallel", "parallel", "arbitrary"),
    )(*operands)


def _mm_nn(name, a, b, out_dtype, tm_want=None):
    M, K = a.shape
    N = b.shape[1]
    tm, tn = _tile(M, tm_want or _TM), _tile(N, _TN_MAX)
    return _matmul(
        name, "nn", (a, b),
        [pl.BlockSpec((tm, K), lambda i, j, k: (i, 0)), pl.BlockSpec((K, tn), lambda i, j, k: (0, j))],
        jax.ShapeDtypeStruct((M, N), out_dtype), pl.BlockSpec((tm, tn), lambda i, j, k: (i, j)),
        (M // tm, N // tn, 1), (tm, tn))


def _mm_nt(name, a, b, out_dtype):
    M, K = a.shape
    N = b.shape[0]
    tm, tn = _tile(M, _TM), _tile(N, _TN_MAX)
    return _matmul(
        name, "nt", (a, b),
        [pl.BlockSpec((tm, K), lambda i, j, k: (i, 0)), pl.BlockSpec((tn, K), lambda i, j, k: (j, 0))],
        jax.ShapeDtypeStruct((M, N), out_dtype), pl.BlockSpec((tm, tn), lambda i, j, k: (i, j)),
        (M // tm, N // tn, 1), (tm, tn))


def _mm_tn(name, a, b, out_dtype, tm_want=2048):
    T, M = a.shape
    N = b.shape[1]
    tm, tn, tk = _tile(M, tm_want), _tile(N, _TN_MAX), _tile(T, _TK_T)
    return _matmul(
        name, "tn", (a, b),
        [pl.BlockSpec((tk, tm), lambda i, j, k: (k, i)), pl.BlockSpec((tk, tn), lambda i, j, k: (k, j))],
        jax.ShapeDtypeStruct((M, N), out_dtype), pl.BlockSpec((tm, tn), lambda i, j, k: (i, j)),
        (M // tm, N // tn, T // tk), (tm, tn))


def _in_proj_first(x, g_mix, w_in, base, n_tiles):
    T, D = x.shape
    NI = w_in.shape[1]
    tm, tn = _tile(T, 2 * _TM), NI // _N_DEV

    def body(base_ref, x_ref, g_ref, w_ref, h_ref, o_ref):
        @pl.when(pl.program_id(1) == 0)
        def _():
            _rms_rows(x_ref, g_ref, h_ref, tm, D)

        o_ref[...] = _dot(h_ref[...], w_ref[...])

    return pl.pallas_call(
        body, name="in_proj_0",
        grid_spec=pltpu.PrefetchScalarGridSpec(
            num_scalar_prefetch=1, grid=(T // tm, n_tiles),
            in_specs=[pl.BlockSpec((tm, D), lambda i, j, b: (i, 0)), pl.BlockSpec((1, D), lambda i, j, b: (0, 0)),
                      pl.BlockSpec((D, tn), lambda i, j, b: (0, b[0] + j))],
            out_specs=[pl.BlockSpec((tm, D), lambda i, j, b: (i, 0)), pl.BlockSpec((tm, tn), lambda i, j, b: (i, b[0] + j))]),
        out_shape=[jax.ShapeDtypeStruct((T, D), bf16), jax.ShapeDtypeStruct((T, NI), f32)],
        compiler_params=_cparams("parallel", "arbitrary"),
    )(base, x, g_mix, w_in)


def _in_proj_more(name, h1, w_in, proj, base, n_tiles):
    T, D = h1.shape
    NI = w_in.shape[1]
    tm, tn = _tile(T, 2 * _TM), NI // _N_DEV

    def body(base_ref, h_ref, w_ref, proj_ref, o_ref):
        o_ref[...] = _dot(h_ref[...], w_ref[...])

    return pl.pallas_call(
        body, name=name,
        grid_spec=pltpu.PrefetchScalarGridSpec(
            num_scalar_prefetch=1, grid=(T // tm, n_tiles),
            in_specs=[pl.BlockSpec((tm, D), lambda i, j, b: (i, 0)), pl.BlockSpec((D, tn), lambda i, j, b: (0, b[0] + j)),
                      pl.BlockSpec(memory_space=pl.ANY)],
            out_specs=pl.BlockSpec((tm, tn), lambda i, j, b: (i, b[0] + j))),
        out_shape=jax.ShapeDtypeStruct((T, NI), f32),
        input_output_aliases={3: 0},
        compiler_params=_cparams("parallel", "arbitrary"),
    )(base, h1, w_in, proj)


def _window_means(ext_ref, row0, tc, gw):
    H = _POOL_HALO
    t_glob = row0 + lax.broadcasted_iota(jnp.int32, (tc, 1), 0)
    out = []
    for g, w in enumerate(_POOL_WINDOWS):
        s = ext_ref[:, g * gw:(g + 1) * gw]
        st = 1
        while st < w:
            s = s + pltpu.roll(s, st, 0)
            st *= 2
        cnt = jnp.minimum(t_glob + 1, w).astype(f32)
        out.append((s[H:, :] / cnt, ext_ref[pl.ds(H, tc), g * gw:(g + 1) * gw]))
    return out


def _pool_fwd(proj, w_pool, pool_scale):
    T = proj.shape[0]
    G, gw, _ = w_pool.shape
    PW = G * gw
    tc = _tile(T, _TC)
    H = _POOL_HALO

    def body(u_ref, w_ref, s_ref, y_ref, ext_ref):
        i = pl.program_id(0)

        @pl.when(i == 0)
        def _():
            ext_ref[pl.ds(0, H), :] = jnp.zeros((H, PW), f32)

        ext_ref[pl.ds(H, tc), :] = u_ref[...]
        for g, (m, u) in enumerate(_window_means(ext_ref, i * tc, tc, gw)):
            d = (m - u).astype(bf16)
            y = _dot(d, w_ref[g]) * s_ref[:, g * gw:(g + 1) * gw]
            y_ref[:, g * gw:(g + 1) * gw] = y.astype(bf16)
        ext_ref[pl.ds(0, H), :] = ext_ref[pl.ds(tc, H), :]

    return pl.pallas_call(
        body, name="pool_fwd", grid=(T // tc,),
        in_specs=[pl.BlockSpec((tc, PW), lambda i: (i, 0)), pl.BlockSpec((G, gw, gw), lambda i: (0, 0, 0)),
                  pl.BlockSpec((1, PW), lambda i: (0, 0))],
        out_specs=pl.BlockSpec((tc, PW), lambda i: (i, 0)),
        out_shape=jax.ShapeDtypeStruct((T, PW), bf16),
        scratch_shapes=[pltpu.VMEM((H + tc, PW), f32)],
        compiler_params=_cparams("arbitrary"),
    )(proj, w_pool, pool_scale)


def _softplus(z):
    return jnp.maximum(z, 0.0) + jnp.log1p(jnp.exp(-jnp.abs(z)))


def _causal_conv(ext_ref, cw_ref, cb_ref, n, K, cols=slice(None), r0=0):
    H = _CONV_HALO
    v = cb_ref[:, cols] + cw_ref[K - 1:K, cols] * ext_ref[pl.ds(H + r0, n), cols]
    for k in range(K - 1):
        v = v + cw_ref[k:k + 1, cols] * ext_ref[pl.ds(H + r0 - (K - 1 - k), n), cols]
    return v


_ROWS = 8
_SLAB_ROWS = 16
_SLAB_COLS = 512


def _slabs(n_rows, n_cols, reverse=False):
    cg = _tile(n_cols, _SLAB_COLS)
    rr = _tile(n_rows, _SLAB_ROWS)
    starts = range(0, n_rows, rr)
    for c0 in range(0, n_cols, cg):
        for r0 in (reversed(starts) if reverse else starts):
            yield slice(c0, c0 + cg), pl.ds(r0, rr)


def _slabs_with_sums(n_rows, n_cols, n_sums, visit, flush, reverse=False):
    cg = _tile(n_cols, _SLAB_COLS)
    rr = _tile(n_rows, _SLAB_ROWS)
    starts = list(range(0, n_rows, rr))
    for c0 in range(0, n_cols, cg):
        cols = slice(c0, c0 + cg)
        sums = [jnp.zeros((8, cg), f32) for _ in range(n_sums)]
        for r0 in (reversed(starts) if reverse else starts):
            sums = visit(cols, pl.ds(r0, rr), sums)
        flush(cols, [jnp.sum(s, axis=0, keepdims=True) for s in sums])


def _lru_fwd(proj, D, PW, conv_w, conv_b, w_a, b_a, w_i, b_i, lam):
    T = proj.shape[0]
    NB, bw, _ = w_a.shape
    K = 4
    tc = _tile(T, _TC)
    H = _CONV_HALO
    hb = D // 2
    assert PW == hb and conv_w.shape[0] == 8

    def body(u0_ref, u1_ref, g0_ref, g1_ref, cw_ref, cb_ref, wa_ref, ba_ref, wi_ref, bi_ref, lam_ref,
             y_ref, hs_ref, ext_ref, a_scr, b_scr, v_scr, hc_scr):
        c = pl.program_id(0)

        @pl.when(c == 0)
        def _():
            ext_ref[pl.ds(0, H), :] = jnp.zeros((H, D), f32)
            hc_scr[...] = jnp.zeros_like(hc_scr)

        ext_ref[pl.ds(H, tc), 0:hb] = u0_ref[...]
        ext_ref[pl.ds(H, tc), hb:D] = u1_ref[...]
        sp = _softplus(-lam_ref[...])
        for cols, rows in _slabs(tc, D):
            v_scr[rows, cols] = _causal_conv(ext_ref, cw_ref, cb_ref, rows.size, K, cols, rows.start)
        for b in range(NB):
            cols = slice(b * bw, (b + 1) * bw)
            vb = v_scr[:, cols].astype(bf16)
            a_scr[:, cols] = _dot(vb, wa_ref[b])
            b_scr[:, cols] = _dot(vb, wi_ref[b])
        for cols, rows in _slabs(tc, D):
            r = _sigmoid(a_scr[rows, cols] + ba_ref[:, cols])
            i = _sigmoid(b_scr[rows, cols] + bi_ref[:, cols])
            a = jnp.exp(-_LRU_C * r * sp[:, cols])
            a_scr[rows, cols] = a
            b_scr[rows, cols] = jnp.sqrt(1.0 - a * a) * (i * v_scr[rows, cols])

        def step(t, h):
            h = a_scr[pl.ds(t, 1), :] * h + b_scr[pl.ds(t, 1), :]
            hs_ref[pl.ds(t, 1), :] = h
            return h

        h = lax.fori_loop(0, tc, step, hc_scr[0:1, :], unroll=8)
        hc_scr[0:1, :] = h
        for cols, rows in _slabs(tc, D):
            g_ref, gcols = (g0_ref, cols) if cols.start < hb else (g1_ref, slice(cols.start - hb, cols.stop - hb))
            y_ref[rows, cols] = (hs_ref[rows, cols] * _gelu(g_ref[rows, gcols])).astype(bf16)
        ext_ref[pl.ds(0, H), :] = ext_ref[pl.ds(tc, H), :]

    vec = pl.BlockSpec((1, D), lambda c: (0, 0))
    wspec = pl.BlockSpec((NB, bw, bw), lambda c: (0, 0, 0))
    return pl.pallas_call(
        body, name="lru_fwd", grid=(T // tc,),
        in_specs=[pl.BlockSpec((tc, hb), lambda c: (c, 1)), pl.BlockSpec((tc, hb), lambda c: (c, 2)),
                  pl.BlockSpec((tc, hb), lambda c: (c, 3)), pl.BlockSpec((tc, hb), lambda c: (c, 4)),
                  pl.BlockSpec((8, D), lambda c: (0, 0)), vec, wspec, vec, wspec, vec, vec],
        out_specs=[pl.BlockSpec((tc, D), lambda c: (c, 0)), pl.BlockSpec((tc, D), lambda c: (c, 0))],
        out_shape=[jax.ShapeDtypeStruct((T, D), bf16), jax.ShapeDtypeStruct((T, D), f32)],
        scratch_shapes=[pltpu.VMEM((H + tc, D), f32), pltpu.VMEM((tc, D), f32), pltpu.VMEM((tc, D), f32),
                        pltpu.VMEM((tc, D), f32), pltpu.VMEM((8, D), f32)],
        compiler_params=_cparams("arbitrary"),
    )(proj, proj, proj, proj, conv_w, conv_b, w_a, b_a, w_i, b_i, lam)


def _merge_fwd(y_pool, y_lru, w_pp, w_lp, proj, b_gate):
    T, PW = y_pool.shape
    D = y_lru.shape[1]
    tm, tn = _tile(T, _TM), _tile(D, PW)
    nj = D // tn
    off = (PW + 2 * D) // tn

    def body(yp_ref, yl_ref, wp_ref, wl_ref, l0_ref, l1_ref, b0_ref, b1_ref, p_ref, q_ref, m_ref, p_scr, q_scr):
        p_scr[...] = _dot(yp_ref[...], wp_ref[...])
        q_scr[...] = _dot(yl_ref[...], wl_ref[...])
        for cols, rows in _slabs(tm, tn):
            p, q = p_scr[rows, cols], q_scr[rows, cols]
            g0 = _sigmoid(l0_ref[rows, cols] + b0_ref[:, cols])
            g1 = _sigmoid(l1_ref[rows, cols] + b1_ref[:, cols])
            m_ref[rows, cols] = (g0 * p + g1 * q).astype(bf16)
            p_ref[rows, cols] = p.astype(bf16)
            q_ref[rows, cols] = q.astype(bf16)

    tile = pl.BlockSpec((tm, tn), lambda j, i: (i, j))
    return pl.pallas_call(
        body, name="merge_fwd", grid=(nj, T // tm),
        in_specs=[pl.BlockSpec((tm, PW), lambda j, i: (i, 0)), pl.BlockSpec((tm, D), lambda j, i: (i, 0)),
                  pl.BlockSpec((PW, tn), lambda j, i: (0, j)), pl.BlockSpec((D, tn), lambda j, i: (0, j)),
                  pl.BlockSpec((tm, tn), lambda j, i: (i, off + j)), pl.BlockSpec((tm, tn), lambda j, i: (i, off + nj + j)),
                  pl.BlockSpec((1, tn), lambda j, i: (0, j)), pl.BlockSpec((1, tn), lambda j, i: (0, nj + j))],
        out_specs=[tile, tile, tile],
        out_shape=[jax.ShapeDtypeStruct((T, D), bf16), jax.ShapeDtypeStruct((T, D), bf16), jax.ShapeDtypeStruct((T, D), bf16)],
        scratch_shapes=[pltpu.VMEM((tm, tn), f32), pltpu.VMEM((tm, tn), f32)],
        compiler_params=_cparams("parallel", "arbitrary"),
    )(y_pool, y_lru, w_pp, w_lp, proj, proj, b_gate, b_gate)


def _out_proj(merged, w_out, x, g_mlp):
    T, D = x.shape
    tm = _tile(T, _TM_SMALL)

    def body(m_ref, w_ref, x_ref, g_ref, x2_ref, h2_ref):
        x2_ref[...] = x_ref[...] + _dot(m_ref[...], w_ref[...])
        _rms_rows(x2_ref, g_ref, h2_ref, tm, D)

    row = pl.BlockSpec((tm, D), lambda i: (i, 0))
    return pl.pallas_call(
        body, name="out_proj", grid=(T // tm,),
        in_specs=[row, pl.BlockSpec((D, D), lambda i: (0, 0)), row, pl.BlockSpec((1, D), lambda i: (0, 0))],
        out_specs=[row, row],
        out_shape=[jax.ShapeDtypeStruct((T, D), f32), jax.ShapeDtypeStruct((T, D), bf16)],
        compiler_params=_cparams("parallel"),
    )(merged, w_out, x, g_mlp)


def _ffn_fwd(up, conv_w, conv_b):
    T, F2 = up.shape
    F = F2 // 2
    K = 3
    tc = _tile(T, 2 * _TC)
    cw = _tile(F, _CW)
    ns = F // cw
    H = _CONV_HALO

    def body(gp_ref, val_ref, cw_ref, cb_ref, z_ref, ext_ref):
        @pl.when(pl.program_id(1) == 0)
        def _():
            ext_ref[pl.ds(0, H), :] = jnp.zeros((H, cw), f32)

        ext_ref[pl.ds(H, tc), :] = gp_ref[...]
        for cols, rows in _slabs(tc, cw):
            c = _causal_conv(ext_ref, cw_ref, cb_ref, rows.size, K, cols, rows.start)
            z_ref[rows, cols] = (_gelu(c) * val_ref[rows, cols]).astype(bf16)
        ext_ref[pl.ds(0, H), :] = ext_ref[pl.ds(tc, H), :]

    return pl.pallas_call(
        body, name="ffn_fwd", grid=(ns, T // tc),
        in_specs=[pl.BlockSpec((tc, cw), lambda s, c: (c, s)), pl.BlockSpec((tc, cw), lambda s, c: (c, ns + s)),
                  pl.BlockSpec((8, cw), lambda s, c: (0, s)), pl.BlockSpec((1, cw), lambda s, c: (0, s))],
        out_specs=pl.BlockSpec((tc, cw), lambda s, c: (c, s)),
        out_shape=jax.ShapeDtypeStruct((T, F), bf16),
        scratch_shapes=[pltpu.VMEM((H + tc, cw), f32)],
        compiler_params=_cparams("parallel", "arbitrary"),
    )(up, up, conv_w, conv_b)


def _down_loss(z, w_down, x2, target, g_final):
    T, F = z.shape
    D = x2.shape[1]
    tm, tk = _tile(T, _TM), _tile(F, _TK_DOWN)
    nk = F // tk

    def body(z_ref, w_ref, x2_ref, t_ref, g_ref, dx_ref, dxb_ref, loss_ref, dg_ref, acc_ref):
        i, k = pl.program_id(0), pl.program_id(1)

        @pl.when(k == 0)
        def _():
            acc_ref[...] = x2_ref[...]

        @pl.when((i == 0) & (k == 0))
        def _():
            loss_ref[...] = jnp.zeros_like(loss_ref)
            dg_ref[...] = jnp.zeros_like(dg_ref)

        acc_ref[...] += _dot(z_ref[...], w_ref[...])

        @pl.when(k == nk - 1)
        def _():
            g = g_ref[...]
            sq = jnp.zeros((_ROWS, 1), f32)
            dgs = jnp.zeros((_ROWS, D), f32)
            for r0 in range(0, tm, _ROWS):
                rows = pl.ds(r0, _ROWS)
                x3 = acc_ref[rows, :]
                r = lax.rsqrt(jnp.mean(x3 * x3, axis=-1, keepdims=True) + _EPS)
                xr = x3 * r
                e = xr * g - t_ref[rows, :]
                sq = sq + jnp.sum(e * e, axis=-1, keepdims=True)
                dy = e * (1.0 / D)
                gy = dy * g
                dx = r * gy - x3 * ((r * r * r) * jnp.mean(x3 * gy, axis=-1, keepdims=True))
                dgs = dgs + dy * xr
                dx_ref[rows, :] = dx
                dxb_ref[rows, :] = dx.astype(bf16)
            loss_ref[...] += (0.5 / D) * jnp.sum(sq)
            dg_ref[...] += jnp.sum(dgs, axis=0, keepdims=True)

    row = pl.BlockSpec((tm, D), lambda i, k: (i, 0))
    vec = pl.BlockSpec((1, D), lambda i, k: (0, 0))
    return pl.pallas_call(
        body, name="down_loss", grid=(T // tm, nk),
        in_specs=[pl.BlockSpec((tm, tk), lambda i, k: (i, k)), pl.BlockSpec((tk, D), lambda i, k: (k, 0)), row, row, vec],
        out_specs=[row, row, pl.BlockSpec((8, _LANE), lambda i, k: (0, 0)), vec],
        out_shape=[jax.ShapeDtypeStruct((T, D), f32), jax.ShapeDtypeStruct((T, D), bf16),
                   jax.ShapeDtypeStruct((8, _LANE), f32), jax.ShapeDtypeStruct((1, D), f32)],
        scratch_shapes=[pltpu.VMEM((tm, D), f32)],
        compiler_params=_cparams("arbitrary", "arbitrary"),
    )(z, w_down, x2, target, g_final)


def _ffn_bwd(dz, up, conv_w, conv_b):
    T, F = dz.shape
    K = 3
    tc = _tile(T, 2 * _TC)
    cw = _tile(F, _CW)
    ns, nt = F // cw, T // tc
    H = _CONV_HALO

    def body(dz_ref, gp_ref, val_ref, gph_ref, cw_ref, cb_ref, dup_ref, dcw_ref, dcb_ref, ext_ref, dext_ref):
        j = pl.program_id(1)
        first = j == nt - 1

        @pl.when(j == 0)
        def _():
            dext_ref[pl.ds(tc, H), :] = jnp.zeros((H, cw), f32)
            dcw_ref[...] = jnp.zeros_like(dcw_ref)
            dcb_ref[...] = jnp.zeros_like(dcb_ref)

        ext_ref[pl.ds(0, H), :] = jnp.where(first, 0.0, gph_ref[...])
        ext_ref[pl.ds(H, tc), :] = gp_ref[...]
        def visit(cols, rows, sums):
            r0, n = rows.start, rows.size
            gp = ext_ref[pl.ds(H + r0, n), cols]
            c = _causal_conv(ext_ref, cw_ref, cb_ref, n, K, cols, r0)
            ge, gg = _gelu_both(c)
            dzv = dz_ref[rows, cols].astype(f32)
            dup_ref[1, rows, cols] = (dzv * ge).astype(bf16)
            dc = dzv * val_ref[rows, cols] * gg
            dext_ref[rows, cols] = dc
            dgp = cw_ref[K - 1:K, cols] * dc
            new = [None] * K + [sums[K] + _fold8(dc)]
            new[K - 1] = sums[K - 1] + _fold8(gp * dc)
            for k in range(K - 1):
                sh = dext_ref[pl.ds(r0 + K - 1 - k, n), cols]
                dgp = dgp + cw_ref[k:k + 1, cols] * sh
                new[k] = sums[k] + _fold8(gp * sh)
            dup_ref[0, rows, cols] = dgp.astype(bf16)
            return new

        def flush(cols, totals):
            for k in range(K):
                dcw_ref[k:k + 1, cols] += totals[k]
            dcb_ref[:, cols] += totals[K]

        _slabs_with_sums(tc, cw, K + 1, visit, flush, reverse=True)
        dext_ref[pl.ds(tc, H), :] = dext_ref[pl.ds(0, H), :]

    hblk = tc // H
    return pl.pallas_call(
        body, name="ffn_bwd", grid=(ns, nt),
        in_specs=[pl.BlockSpec((tc, cw), lambda s, j: (nt - 1 - j, s)),
                  pl.BlockSpec((tc, cw), lambda s, j: (nt - 1 - j, s)),
                  pl.BlockSpec((tc, cw), lambda s, j: (nt - 1 - j, ns + s)),
                  pl.BlockSpec((H, cw), lambda s, j: (jnp.maximum((nt - 1 - j) * hblk - 1, 0), s)),
                  pl.BlockSpec((8, cw), lambda s, j: (0, s)), pl.BlockSpec((1, cw), lambda s, j: (0, s))],
        out_specs=[pl.BlockSpec((2, tc, cw), lambda s, j: (0, nt - 1 - j, s)),
                   pl.BlockSpec((8, cw), lambda s, j: (0, s)), pl.BlockSpec((1, cw), lambda s, j: (0, s))],
        out_shape=[jax.ShapeDtypeStruct((2, T, F), bf16), jax.ShapeDtypeStruct((8, F), f32), jax.ShapeDtypeStruct((1, F), f32)],
        scratch_shapes=[pltpu.VMEM((H + tc, cw), f32), pltpu.VMEM((tc + H, cw), f32)],
        compiler_params=_cparams("parallel", "arbitrary"),
    )(dz, up, up, up, conv_w, conv_b)


def _norm_bwd_matmul(name, a, a_spec, nk, w, w_spec, x, g, dres):
    T, D = x.shape
    tm = a_spec.block_shape[-2]

    def body(a_ref, w_ref, x_ref, g_ref, dr_ref, dx_ref, dxb_ref, dg_ref, acc_ref):
        i, k = pl.program_id(0), pl.program_id(1)

        @pl.when((i == 0) & (k == 0))
        def _():
            dg_ref[...] = jnp.zeros_like(dg_ref)

        @pl.when(k == 0)
        def _():
            acc_ref[...] = _dot_nt(a_ref[...], w_ref[...])

        @pl.when(k > 0)
        def _():
            acc_ref[...] += _dot_nt(a_ref[...], w_ref[...])

        @pl.when(k == nk - 1)
        def _():
            g = g_ref[...]
            dgs = jnp.zeros((_ROWS, D), f32)
            for r0 in range(0, tm, _ROWS):
                rows = pl.ds(r0, _ROWS)
                x = x_ref[rows, :]
                dh = acc_ref[rows, :]
                r = lax.rsqrt(jnp.mean(x * x, axis=-1, keepdims=True) + _EPS)
                gy = dh * g
                dx = dr_ref[rows, :] + (r * gy - x * ((r * r * r) * jnp.mean(x * gy, axis=-1, keepdims=True)))
                dgs = dgs + dh * (x * r)
                dx_ref[rows, :] = dx
                dxb_ref[rows, :] = dx.astype(bf16)
            dg_ref[...] += jnp.sum(dgs, axis=0, keepdims=True)

    row = pl.BlockSpec((tm, D), lambda i, k: (i, 0))
    vec = pl.BlockSpec((1, D), lambda i, k: (0, 0))
    return pl.pallas_call(
        body, name=name, grid=(T // tm, nk),
        in_specs=[a_spec, w_spec, row, vec, row],
        out_specs=[row, row, vec],
        out_shape=[jax.ShapeDtypeStruct((T, D), f32), jax.ShapeDtypeStruct((T, D), bf16), jax.ShapeDtypeStruct((1, D), f32)],
        scratch_shapes=[pltpu.VMEM((tm, D), f32)],
        compiler_params=_cparams("arbitrary", "arbitrary"),
    )(a, w, x, g, dres)


def _merge_bwd(dx2b, w_out, p, q, proj, b_gate, PW):
    T, D = p.shape
    tm, tn = _tile(T, _TM), _tile(D, PW)
    nj = D // tn
    off = (PW + 2 * D) // tn

    def body(dx_ref, w_ref, p_ref, q_ref, l0_ref, l1_ref, b0_ref, b1_ref, dp_ref, dq_ref, dl0_ref, dl1_ref, db0_ref, db1_ref,
             dm_ref):
        @pl.when(pl.program_id(1) == 0)
        def _():
            db0_ref[...] = jnp.zeros_like(db0_ref)
            db1_ref[...] = jnp.zeros_like(db1_ref)

        dm_ref[...] = _dot_nt(dx_ref[...], w_ref[...])

        def visit(cols, rows, sums):
            dm = dm_ref[rows, cols]
            g0 = _sigmoid(l0_ref[rows, cols] + b0_ref[:, cols])
            g1 = _sigmoid(l1_ref[rows, cols] + b1_ref[:, cols])
            dp_ref[rows, cols] = (g0 * dm).astype(bf16)
            dq_ref[rows, cols] = (g1 * dm).astype(bf16)
            dl0 = dm * p_ref[rows, cols].astype(f32) * (g0 * (1.0 - g0))
            dl1 = dm * q_ref[rows, cols].astype(f32) * (g1 * (1.0 - g1))
            dl0_ref[rows, cols] = dl0.astype(bf16)
            dl1_ref[rows, cols] = dl1.astype(bf16)
            return [sums[0] + _fold8(dl0), sums[1] + _fold8(dl1)]

        def flush(cols, totals):
            db0_ref[:, cols] += totals[0]
            db1_ref[:, cols] += totals[1]

        _slabs_with_sums(tm, tn, 2, visit, flush)

    tile = pl.BlockSpec((tm, tn), lambda j, i: (i, j))
    vecj = pl.BlockSpec((1, tn), lambda j, i: (0, j))
    tb = jax.ShapeDtypeStruct((T, D), bf16)
    vb = jax.ShapeDtypeStruct((1, D), f32)
    return pl.pallas_call(
        body, name="merge_bwd", grid=(nj, T // tm),
        in_specs=[pl.BlockSpec((tm, D), lambda j, i: (i, 0)), pl.BlockSpec((tn, D), lambda j, i: (j, 0)), tile, tile,
                  pl.BlockSpec((tm, tn), lambda j, i: (i, off + j)), pl.BlockSpec((tm, tn), lambda j, i: (i, off + nj + j)),
                  vecj, pl.BlockSpec((1, tn), lambda j, i: (0, nj + j))],
        out_specs=[tile, tile, tile, tile, vecj, vecj],
        out_shape=[tb, tb, tb, tb, vb, vb],
        scratch_shapes=[pltpu.VMEM((tm, tn), f32)],
        compiler_params=_cparams("parallel", "arbitrary"),
    )(dx2b, w_out, p, q, proj, proj, b_gate, b_gate)


def _lru_bwd(dy, proj, hs, D, conv_w, conv_b, w_a, b_a, w_i, b_i, lam):
    T = dy.shape[0]
    NB, bw, _ = w_a.shape
    K = 4
    tc = _tile(T, _TC)
    nt = T // tc
    H = _CONV_HALO
    hb = D // 2

    def body(dy_ref, u0_ref, u1_ref, g0_ref, g1_ref, hs_ref, uh0_ref, uh1_ref, hh_ref,
             cw_ref, cb_ref, wa_ref, ba_ref, wi_ref, bi_ref, lam_ref,
             du_ref, dg_ref, dwa_ref, dwi_ref, dcw_ref, dvec_ref,
             ext_ref, hext_ref, dext_ref, a_scr, r_scr, i_scr, v_scr, g_scr, car_scr):
        j = pl.program_id(0)
        first = j == nt - 1

        @pl.when(j == 0)
        def _():
            dext_ref[pl.ds(tc, H), :] = jnp.zeros((H, D), f32)
            car_scr[...] = jnp.zeros_like(car_scr)
            dwa_ref[...] = jnp.zeros_like(dwa_ref)
            dwi_ref[...] = jnp.zeros_like(dwi_ref)
            dcw_ref[...] = jnp.zeros_like(dcw_ref)
            dvec_ref[...] = jnp.zeros_like(dvec_ref)

        ext_ref[pl.ds(0, H), 0:hb] = jnp.where(first, 0.0, uh0_ref[...])
        ext_ref[pl.ds(0, H), hb:D] = jnp.where(first, 0.0, uh1_ref[...])
        ext_ref[pl.ds(H, tc), 0:hb] = u0_ref[...]
        ext_ref[pl.ds(H, tc), hb:D] = u1_ref[...]
        hext_ref[pl.ds(0, H), :] = jnp.where(first, 0.0, hh_ref[...])
        hext_ref[pl.ds(H, tc), :] = hs_ref[...]
        lamv = lam_ref[...]
        sp = _softplus(-lamv)

        for cols, rows in _slabs(tc, D):
            v_scr[rows, cols] = _causal_conv(ext_ref, cw_ref, cb_ref, rows.size, K, cols, rows.start)
        for b in range(NB):
            cols = slice(b * bw, (b + 1) * bw)
            vb = v_scr[:, cols].astype(bf16)
            r_scr[:, cols] = _dot(vb, wa_ref[b])
            i_scr[:, cols] = _dot(vb, wi_ref[b])
        for cols, rows in _slabs(tc, D):
            r = _sigmoid(r_scr[rows, cols] + ba_ref[:, cols])
            r_scr[rows, cols] = r
            i_scr[rows, cols] = _sigmoid(i_scr[rows, cols] + bi_ref[:, cols])
            a_scr[rows, cols] = jnp.exp(-_LRU_C * r * sp[:, cols])
            g_ref, gcols = (g0_ref, cols) if cols.start < hb else (g1_ref, slice(cols.start - hb, cols.stop - hb))
            ge, gg = _gelu_both(g_ref[rows, gcols])
            dyv = dy_ref[rows, cols]
            g_scr[rows, cols] = dyv * ge
            dg_ref[rows, cols] = (dyv * hs_ref[rows, cols] * gg).astype(bf16)

        def step(s, carry):
            t = tc - 1 - s
            g = g_scr[pl.ds(t, 1), :] + carry
            g_scr[pl.ds(t, 1), :] = g
            return a_scr[pl.ds(t, 1), :] * g

        car_scr[0:1, :] = lax.fori_loop(0, tc, step, car_scr[0:1, :], unroll=8)

        def gates(cols, rows, sums):
            g, v, r, i, a = g_scr[rows, cols], v_scr[rows, cols], r_scr[rows, cols], i_scr[rows, cols], a_scr[rows, cols]
            mult = jnp.sqrt(1.0 - a * a)
            h_prev = hext_ref[pl.ds(H - 1 + rows.start, rows.size), cols]
            gm = g * mult
            dext_ref[rows, cols] = gm * i
            dlog_a = (g * h_prev - g * (i * v) * (a / mult)) * a
            dpr = dlog_a * (-_LRU_C * sp[:, cols]) * (r * (1.0 - r))
            dpi = gm * v * (i * (1.0 - i))
            r_scr[rows, cols] = dpr
            i_scr[rows, cols] = dpi
            return [sums[0] + _fold8(dpr), sums[1] + _fold8(dpi), sums[2] + _fold8(dlog_a * (-_LRU_C * r))]

        def gates_flush(cols, totals):
            dvec_ref[1:2, cols] += totals[0]
            dvec_ref[2:3, cols] += totals[1]
            dvec_ref[3:4, cols] += totals[2] * (-_sigmoid(-lamv[:, cols]))

        _slabs_with_sums(tc, D, 3, gates, gates_flush)
        for b in range(NB):
            cols = slice(b * bw, (b + 1) * bw)
            dprb, dpib, vb = r_scr[:, cols].astype(bf16), i_scr[:, cols].astype(bf16), v_scr[:, cols].astype(bf16)
            dext_ref[pl.ds(0, tc), cols] += _dot_nt(dprb, wa_ref[b]) + _dot_nt(dpib, wi_ref[b])
            dwa_ref[b] += _dot_tn(vb, dprb)
            dwi_ref[b] += _dot_tn(vb, dpib)

        def conv_t(cols, rows, sums):
            r0, n = rows.start, rows.size
            u = ext_ref[pl.ds(H + r0, n), cols]
            dv = dext_ref[rows, cols]
            du = cw_ref[K - 1:K, cols] * dv
            new = [None] * K + [sums[K] + _fold8(dv)]
            new[K - 1] = sums[K - 1] + _fold8(u * dv)
            for k in range(K - 1):
                sh = dext_ref[pl.ds(r0 + K - 1 - k, n), cols]
                du = du + cw_ref[k:k + 1, cols] * sh
                new[k] = sums[k] + _fold8(u * sh)
            du_ref[rows, cols] = du.astype(bf16)
            return new

        def conv_t_flush(cols, totals):
            for k in range(K):
                dcw_ref[k:k + 1, cols] += totals[k]
            dvec_ref[0:1, cols] += totals[K]

        _slabs_with_sums(tc, D, K + 1, conv_t, conv_t_flush)
        dext_ref[pl.ds(tc, H), :] = dext_ref[pl.ds(0, H), :]

    hblk = tc // H
    rev = lambda j: nt - 1 - j
    halo = lambda j: jnp.maximum((nt - 1 - j) * hblk - 1, 0)
    vec = pl.BlockSpec((1, D), lambda j: (0, 0))
    wspec = pl.BlockSpec((NB, bw, bw), lambda j: (0, 0, 0))
    acc8 = pl.BlockSpec((8, D), lambda j: (0, 0))
    big = pltpu.VMEM((tc, D), f32)
    return pl.pallas_call(
        body, name="lru_bwd", grid=(nt,),
        in_specs=[pl.BlockSpec((tc, D), lambda j: (rev(j), 0)),
                  pl.BlockSpec((tc, hb), lambda j: (rev(j), 1)), pl.BlockSpec((tc, hb), lambda j: (rev(j), 2)),
                  pl.BlockSpec((tc, hb), lambda j: (rev(j), 3)), pl.BlockSpec((tc, hb), lambda j: (rev(j), 4)),
                  pl.BlockSpec((tc, D), lambda j: (rev(j), 0)),
                  pl.BlockSpec((H, hb), lambda j: (halo(j), 1)), pl.BlockSpec((H, hb), lambda j: (halo(j), 2)),
                  pl.BlockSpec((H, D), lambda j: (halo(j), 0)),
                  acc8, vec, wspec, vec, wspec, vec, vec],
        out_specs=[pl.BlockSpec((tc, D), lambda j: (rev(j), 0)), pl.BlockSpec((tc, D), lambda j: (rev(j), 0)),
                   wspec, wspec, acc8, acc8],
        out_shape=[jax.ShapeDtypeStruct((T, D), bf16), jax.ShapeDtypeStruct((T, D), bf16),
                   jax.ShapeDtypeStruct((NB, bw, bw), f32), jax.ShapeDtypeStruct((NB, bw, bw), f32),
                   jax.ShapeDtypeStruct((8, D), f32), jax.ShapeDtypeStruct((8, D), f32)],
        scratch_shapes=[pltpu.VMEM((H + tc, D), f32), pltpu.VMEM((H + tc, D), f32), pltpu.VMEM((tc + H, D), f32),
                        big, big, big, big, big, pltpu.VMEM((8, D), f32)],
        compiler_params=_cparams("arbitrary"),
    )(dy, proj, proj, proj, proj, hs, proj, proj, hs, conv_w, conv_b, w_a, b_a, w_i, b_i, lam)


def _pool_bwd(dy, proj, w_pool, pool_scale):
    T, PW = dy.shape
    G, gw, _ = w_pool.shape
    tc = _tile(T, _TC)
    nt = T // tc
    H = _POOL_HALO

    def body(dy_ref, u_ref, uh_ref, w_ref, s_ref, du_ref, dw_ref, ds_ref, ext_ref, eext_ref):
        j = pl.program_id(0)
        first = j == nt - 1
        row0 = (nt - 1 - j) * tc

        @pl.when(j == 0)
        def _():
            eext_ref[pl.ds(tc, H), :] = jnp.zeros((H, PW), f32)
            dw_ref[...] = jnp.zeros_like(dw_ref)
            ds_ref[...] = jnp.zeros_like(ds_ref)

        ext_ref[pl.ds(0, H), :] = jnp.where(first, 0.0, uh_ref[...])
        ext_ref[pl.ds(H, tc), :] = u_ref[...]
        t_glob = row0 + lax.broadcasted_iota(jnp.int32, (tc, 1), 0)
        dds = []
        for g, (m, u) in enumerate(_window_means(ext_ref, row0, tc, gw)):
            cols = slice(g * gw, (g + 1) * gw)
            d = (m - u).astype(bf16)
            yraw = _dot(d, w_ref[g])
            dyv = dy_ref[:, cols]
            ds_ref[:, cols] += jnp.sum(dyv * yraw, axis=0, keepdims=True)
            dyr = (dyv * s_ref[:, cols]).astype(bf16)
            dd = _dot_nt(dyr, w_ref[g])
            dw_ref[g] += _dot_tn(d, dyr)
            cnt = jnp.minimum(t_glob + 1, _POOL_WINDOWS[g]).astype(f32)
            eext_ref[pl.ds(0, tc), cols] = dd / cnt
            dds.append(dd)
        n = tc + H
        for g, w in enumerate(_POOL_WINDOWS):
            cols = slice(g * gw, (g + 1) * gw)
            s = eext_ref[:, cols]
            st = 1
            while st < w:
                s = s + pltpu.roll(s, n - st, 0)
                st *= 2
            du_ref[:, cols] = (s[0:tc, :] - dds[g]).astype(bf16)
        eext_ref[pl.ds(tc, H), :] = eext_ref[pl.ds(0, H), :]

    hblk = tc // H
    return pl.pallas_call(
        body, name="pool_bwd", grid=(nt,),
        in_specs=[pl.BlockSpec((tc, PW), lambda j: (nt - 1 - j, 0)), pl.BlockSpec((tc, PW), lambda j: (nt - 1 - j, 0)),
                  pl.BlockSpec((H, PW), lambda j: (jnp.maximum((nt - 1 - j) * hblk - 1, 0), 0)),
                  pl.BlockSpec((G, gw, gw), lambda j: (0, 0, 0)), pl.BlockSpec((1, PW), lambda j: (0, 0))],
        out_specs=[pl.BlockSpec((tc, PW), lambda j: (nt - 1 - j, 0)), pl.BlockSpec((G, gw, gw), lambda j: (0, 0, 0)),
                   pl.BlockSpec((1, PW), lambda j: (0, 0))],
        out_shape=[jax.ShapeDtypeStruct((T, PW), bf16), jax.ShapeDtypeStruct((G, gw, gw), f32), jax.ShapeDtypeStruct((1, PW), f32)],
        scratch_shapes=[pltpu.VMEM((H + tc, PW), f32), pltpu.VMEM((tc + H, PW), f32)],
        compiler_params=_cparams("arbitrary"),
    )(dy, proj, proj, w_pool, pool_scale)


_MESH = pl.DeviceIdType.MESH
_HBM = pl.BlockSpec(memory_space=pltpu.HBM)


def _slab(ref, kind, blk, n):
    start = blk * n
    if n % _LANE == 0:
        start = pl.multiple_of(start, _LANE)
    if kind == "col":
        return ref.at[:, pl.ds(start, n)]
    if kind == "row":
        return ref.at[pl.ds(start, n), :]
    if kind == "mid":
        return ref.at[:, pl.ds(start, n), :]
    raise ValueError(kind)


def _my_place():
    x, y, c = lax.axis_index("x"), lax.axis_index("y"), lax.axis_index("c")
    return x, y, c


def _blk(px, py, pc):
    return 4 * px + 2 * py + pc


_SEM = pl.BlockSpec(memory_space=pltpu.SEMAPHORE)
_EFFECT = pltpu.SideEffectType.DATAFLOW_SIDE_EFFECTING


_ALL_PEERS = (1, 2, 3, 4, 5, 6, 7)


def _peers(x, y, c):
    return [(k, (x ^ (k >> 2), y ^ ((k >> 1) & 1), c ^ (k & 1))) for k in range(1, 8)]


class _Route:
    def __init__(self, mode, kind, size):
        self.mode, self.kind, self.size = mode, kind, size

    def src(self, ref, peer_blk):
        return ref if self.mode == "gather" else _slab(ref, self.kind, peer_blk, self.size)

    def dst(self, ref, origin_blk):
        return _slab(ref, self.kind, origin_blk, self.size) if self.mode == "gather" else ref.at[origin_blk]


def _send_start(name, srcs, lands, routes, groups):
    nt, ng = len(srcs), len(groups)

    def body(*refs):
        src_refs, land_refs = refs[:nt], refs[nt:2 * nt]
        sems = refs[2 * nt:2 * nt + 2 * ng]
        token = refs[-1]
        x, y, c = _my_place()
        me = _blk(x, y, c)
        for gi, (grp, ks) in enumerate(groups):
            for pos, t in enumerate(grp):
                for k, peer in _peers(x, y, c):
                    if k in ks:
                        s = len(ks) * pos + ks.index(k)
                        pltpu.make_async_remote_copy(
                            src_ref=routes[t].src(src_refs[t], _blk(*peer)), dst_ref=routes[t].dst(land_refs[t], me),
                            send_sem=sems[2 * gi].at[s], recv_sem=sems[2 * gi + 1].at[s],
                            device_id=peer, device_id_type=_MESH).start()
        token[...] = jnp.zeros_like(token)

    hbm = lambda a: pltpu.HBM(a.shape, a.dtype)
    out_shape = []
    for grp, ks in groups:
        out_shape += [pltpu.SemaphoreType.DMA((len(ks) * len(grp),)), pltpu.SemaphoreType.DMA((len(ks) * len(grp),))]
    out_shape += [hbm(a) for a in srcs] + [hbm(a) for a in lands] + [jax.ShapeDtypeStruct((8, _LANE), f32)]
    res = pl.pallas_call(
        body, name=name, out_shape=out_shape,
        in_specs=[_HBM] * (2 * nt),
        out_specs=[_SEM] * (2 * ng) + [_HBM] * (2 * nt) + [pl.BlockSpec(memory_space=pltpu.VMEM)],
        input_output_aliases={t: 2 * ng + t for t in range(2 * nt)},
        compiler_params=pltpu.CompilerParams(has_side_effects=_EFFECT),
    )(*[pltpu.with_memory_space_constraint(a, pltpu.HBM) for a in list(srcs) + list(lands)])
    sems = [(res[2 * g], res[2 * g + 1]) for g in range(ng)]
    return sems, res[2 * ng:2 * ng + nt], res[2 * ng + nt:2 * ng + 2 * nt], res[-1]


def _send_wait(name, srcs, lands, routes, sems, after, ks=_ALL_PEERS):
    n = len(srcs)

    def body(*refs):
        src_refs, land_refs = refs[:n], refs[n:2 * n]
        send_sems, recv_sems = refs[2 * n], refs[2 * n + 1]
        x, y, c = _my_place()
        for pos in range(n):
            for k, peer in _peers(x, y, c):
                if k in ks:
                    pb = _blk(*peer)
                    s = len(ks) * pos + ks.index(k)
                    cp = pltpu.make_async_remote_copy(
                        src_ref=routes[pos].src(src_refs[pos], pb), dst_ref=routes[pos].dst(land_refs[pos], pb),
                        send_sem=send_sems.at[s], recv_sem=recv_sems.at[s], device_id=peer, device_id_type=_MESH)
                    cp.wait_send()
                    cp.wait_recv()

    hbm = lambda a: pltpu.HBM(a.shape, a.dtype)
    res = pl.pallas_call(
        body, name=name, out_shape=[hbm(a) for a in srcs] + [hbm(a) for a in lands],
        in_specs=[_HBM] * (2 * n) + [_SEM, _SEM, pl.BlockSpec(memory_space=pl.ANY)],
        out_specs=[_HBM] * (2 * n),
        input_output_aliases={t: t for t in range(2 * n)},
        compiler_params=pltpu.CompilerParams(has_side_effects=_EFFECT),
    )(*srcs, *lands, sems[0], sems[1], after)
    return res[:n], res[n:]


def _forward_start(name, lands, routes, ks):
    n = len(lands)

    def body(*refs):
        land_refs, send_sems, recv_sems, token = refs[:n], refs[n], refs[n + 1], refs[-1]
        x, y, c = _my_place()
        for pos in range(n):
            for i, k in enumerate(ks):
                part = routes[pos].dst(land_refs[pos], _blk(x ^ (k >> 2), y ^ ((k >> 1) & 1), c))
                pltpu.make_async_remote_copy(
                    src_ref=part, dst_ref=part, send_sem=send_sems.at[len(ks) * pos + i],
                    recv_sem=recv_sems.at[len(ks) * pos + i], device_id=(x, y, 1 - c), device_id_type=_MESH).start()
        token[...] = jnp.zeros_like(token)

    hbm = lambda a: pltpu.HBM(a.shape, a.dtype)
    sem = pltpu.SemaphoreType.DMA((len(ks) * n,))
    res = pl.pallas_call(
        body, name=name, out_shape=[sem, sem] + [hbm(a) for a in lands] + [jax.ShapeDtypeStruct((8, _LANE), f32)],
        in_specs=[_HBM] * n, out_specs=[_SEM, _SEM] + [_HBM] * n + [pl.BlockSpec(memory_space=pltpu.VMEM)],
        input_output_aliases={t: 2 + t for t in range(n)},
        compiler_params=pltpu.CompilerParams(has_side_effects=_EFFECT),
    )(*[pltpu.with_memory_space_constraint(a, pltpu.HBM) for a in lands])
    return (res[0], res[1]), res[2:2 + n], res[-1]


def _forward_wait(name, lands, routes, sems, after, ks):
    n = len(lands)

    def body(*refs):
        land_refs, send_sems, recv_sems = refs[:n], refs[n], refs[n + 1]
        x, y, c = _my_place()
        for pos in range(n):
            for i, k in enumerate(ks):
                px, py = x ^ (k >> 2), y ^ ((k >> 1) & 1)
                cp = pltpu.make_async_remote_copy(
                    src_ref=routes[pos].dst(land_refs[pos], _blk(px, py, c)),
                    dst_ref=routes[pos].dst(land_refs[pos], _blk(px, py, 1 - c)),
                    send_sem=send_sems.at[len(ks) * pos + i], recv_sem=recv_sems.at[len(ks) * pos + i],
                    device_id=(x, y, 1 - c), device_id_type=_MESH)
                cp.wait_send()
                cp.wait_recv()

    hbm = lambda a: pltpu.HBM(a.shape, a.dtype)
    return pl.pallas_call(
        body, name=name, out_shape=[hbm(a) for a in lands],
        in_specs=[_HBM] * n + [_SEM, _SEM, pl.BlockSpec(memory_space=pl.ANY)], out_specs=[_HBM] * n,
        input_output_aliases={t: t for t in range(n)},
        compiler_params=pltpu.CompilerParams(has_side_effects=_EFFECT),
    )(*lands, sems[0], sems[1], after)


def _copy_own(name, src, land, route, me):
    gather = route.mode == "gather"
    shard = src.shape if gather else land.shape[1:]
    lead = () if gather else (None,)

    if route.kind == "mid":
        grid = (1,)
        at_full = lambda i, me: (0, me[0], 0)
        at_shard = lambda i, me: (0, 0, 0)
        block = tuple(shard)
    else:
        rows, width = shard
        tr = _tile(rows, 512, 16)
        grid = (rows // tr,)
        block = (tr, width)
        if route.kind == "col":
            at_full = lambda i, me: (i, me[0])
        else:
            at_full = lambda i, me: (me[0] * grid[0] + i, 0)
        at_shard = lambda i, me: (i, 0)
    if gather:
        in_map, out_map = at_shard, at_full
    else:
        in_map, out_map = at_full, (lambda i, me: (me[0], *at_shard(i, me)))

    def body(me_ref, src_ref, land_ref, out_ref):
        out_ref[...] = src_ref[...]

    return pl.pallas_call(
        body, name=name, out_shape=jax.ShapeDtypeStruct(land.shape, land.dtype),
        grid_spec=pltpu.PrefetchScalarGridSpec(
            num_scalar_prefetch=1, grid=grid,
            in_specs=[pl.BlockSpec(block, in_map), pl.BlockSpec(memory_space=pl.ANY)],
            out_specs=pl.BlockSpec(lead + block, out_map)),
        input_output_aliases={2: 0},
        compiler_params=_cparams("arbitrary"),
    )(me, src, land)


def _place_own(name, srcs, lands, routes):
    me = _blk(*_my_place()).astype(jnp.int32).reshape(1)
    return [_copy_own(f"{name}_{t}", s, l, r, me) for t, (s, l, r) in enumerate(zip(srcs, lands, routes))]


def _exchange(fulls, kinds, sizes, whole):
    arrays = list(fulls) + list(whole)
    nt, nf = len(arrays), len(fulls)

    def shard_shape(t):
        s = list(arrays[t].shape)
        if t < nf:
            s[{"col": 1, "row": 0, "mid": 1}[kinds[t]]] = sizes[t]
        return tuple(s)

    def body(*refs):
        ins, outs = refs[:nt], refs[nt:2 * nt]
        send_sems, recv_sems, local_sems = refs[2 * nt:]
        x, y, c = _my_place()
        me = _blk(x, y, c)

        def src(t, blk):
            return _slab(ins[t], kinds[t], blk, sizes[t]) if t < nf else ins[t]

        mine = [pltpu.make_async_copy(src(t, me), outs[t].at[me], local_sems.at[t]) for t in range(nt)]
        for cp in mine:
            cp.start()
        sent = []
        for t in range(nt):
            for k in range(1, 8):
                peer = (x ^ (k >> 2), y ^ ((k >> 1) & 1), c ^ (k & 1))
                pb = _blk(*peer)
                cp = pltpu.make_async_remote_copy(
                    src_ref=src(t, pb), dst_ref=outs[t].at[me],
                    send_sem=send_sems.at[7 * t + k - 1], recv_sem=recv_sems.at[7 * t + k - 1],
                    device_id=peer, device_id_type=_MESH)
                cp.start()
                sent.append((cp, t, k, pb))
        for cp, t, k, pb in sent:
            pltpu.make_async_remote_copy(
                src_ref=src(t, pb), dst_ref=outs[t].at[pb],
                send_sem=send_sems.at[7 * t + k - 1], recv_sem=recv_sems.at[7 * t + k - 1],
                device_id=(x, y, c), device_id_type=_MESH).wait_recv()
        for cp, _, _, _ in sent:
            cp.wait_send()
        for cp in mine:
            cp.wait()

    return pl.pallas_call(
        body, name="exchange_grads",
        in_specs=[_HBM] * nt, out_specs=[_HBM] * nt,
        out_shape=[jax.ShapeDtypeStruct((_N_DEV,) + shard_shape(t), arrays[t].dtype) for t in range(nt)],
        scratch_shapes=[pltpu.SemaphoreType.DMA((7 * nt,)), pltpu.SemaphoreType.DMA((7 * nt,)), pltpu.SemaphoreType.DMA((nt,))],
        compiler_params=pltpu.CompilerParams(has_side_effects=True),
    )(*arrays)


def _sum_parts(name, parts):
    n, R, C = parts.shape
    tr = _tile(R, 256, 8)

    def body(p_ref, o_ref):
        g = p_ref[0].astype(f32)
        for s in range(1, n):
            g = g + p_ref[s].astype(f32)
        o_ref[...] = g

    return pl.pallas_call(
        body, name=name, grid=(R // tr,),
        in_specs=[pl.BlockSpec((n, tr, C), lambda i: (0, i, 0))],
        out_specs=pl.BlockSpec((tr, C), lambda i: (i, 0)),
        out_shape=jax.ShapeDtypeStruct((R, C), f32),
        compiler_params=_cparams("parallel"),
    )(parts)


def _adamw(name, w, m, v, parts):
    R, C = w.shape
    n = parts.shape[0]
    tr = _tile(R, 256, 8)
    c1 = 1.0 - _ADAM_B1 ** _ADAM_STEP
    c2 = 1.0 - _ADAM_B2 ** _ADAM_STEP

    def body(w_ref, m_ref, v_ref, p_ref, g_ref, d_ref, nm_ref, nv_ref):
        g = p_ref[0].astype(f32)
        for s in range(1, n):
            g = g + p_ref[s].astype(f32)
        nm = _ADAM_B1 * m_ref[...] + (1.0 - _ADAM_B1) * g
        nv = _ADAM_B2 * v_ref[...] + (1.0 - _ADAM_B2) * (g * g)
        g_ref[...] = g
        nm_ref[...] = nm
        nv_ref[...] = nv
        d_ref[...] = -_ADAM_LR * ((nm / c1) / (jnp.sqrt(nv / c2) + _ADAM_EPS) + _ADAM_WD * w_ref[...])

    blk = pl.BlockSpec((tr, C), lambda i: (i, 0))
    sd = jax.ShapeDtypeStruct((R, C), f32)
    return pl.pallas_call(
        body, name=name, grid=(R // tr,),
        in_specs=[blk, blk, blk, pl.BlockSpec((n, tr, C), lambda i: (0, i, 0))],
        out_specs=[blk, blk, blk, blk], out_shape=[sd, sd, sd, sd],
        compiler_params=_cparams("parallel"),
    )(w, m, v, parts)


def _pack(vectors):
    flat = jnp.concatenate([a.reshape(-1).astype(f32) for a in vectors])
    pad = (-flat.shape[0]) % _SMALL_PAD
    return jnp.pad(flat, (0, pad)).reshape(-1, _LANE)


def _unpack(packed, shapes):
    flat = packed.reshape(-1)
    out, o = [], 0
    for s in shapes:
        n = 1
        for d in s:
            n *= d
        out.append(flat[o:o + n].reshape(s))
        o += n
    return out


def _pad_rows8(a):
    return jnp.pad(a, ((0, 8 - a.shape[0]), (0, 0)))


def _local_step(x, target, p, get, emit, hint):
    T, D = x.shape

    def tie(a, *tokens):
        for tok in tokens:
            if tok is not None:
                a = a + tok[0, 0]
        return a

    n_parts = _N_DEV // _W_IN_PART
    w_in, base = get("w_in", x, 0)
    h1, proj = _in_proj_first(x, p["g_mix"], w_in, base, _W_IN_PART)
    hint("w_in", proj)
    for part in range(1, n_parts):
        w_in, base = get("w_in", proj, part)
        proj = _in_proj_more(f"in_proj_{part}", h1, w_in, proj, base, _W_IN_PART)
    w_pool = get("w_pool", proj)
    PW = w_pool.shape[0] * w_pool.shape[1]
    y_pool = _pool_fwd(proj, w_pool, p["pool_scale"])
    tok = hint("w_pool_proj", y_pool)
    lru_conv_w, w_a, w_i = get("lru_conv_w", proj), get("w_a", proj), get("w_i", proj)
    y_lru, hs = _lru_fwd(proj, D, PW, lru_conv_w, tie(p["lru_conv_b"], tok), w_a, p["b_a"], w_i, p["b_i"], p["lru_lambda"])
    tok = hint("w_up", y_lru)
    w_pp, w_lp = get("w_pool_proj", y_lru), get("w_lru_proj", y_lru)
    pp, qq, merged = _merge_fwd(y_pool, y_lru, w_pp, w_lp, proj, tie(p["b_gate"], tok))
    w_out = get("w_out", y_lru)
    x2, h2 = _out_proj(merged, w_out, x, p["g_mlp"])
    w_up = get("w_up", x2)
    up = _mm_nn("up_proj", h2, w_up, f32, tm_want=2 * _TM)
    tok = hint("w_down", up)
    ffn_conv_w = get("ffn_conv_w", y_lru)
    z = _ffn_fwd(up, ffn_conv_w, tie(p["ffn_conv_b"], tok))
    w_down = get("w_down", z)
    F = w_down.shape[0]
    dx3, dx3b, loss_t, dg_final = _down_loss(z, w_down, x2, target, p["g_final"])

    gs = {"g_final": dg_final}
    tok = emit("w_down", _mm_tn("dw_down", z, dx3b, bf16, tm_want=1536))
    dz = _mm_nt("dz", dx3b, w_down, bf16)
    dup, dcw_ffn, dcb_ffn = _ffn_bwd(dz, up, ffn_conv_w, tie(p["ffn_conv_b"], tok))
    gs["ffn_conv_w"] = dcw_ffn[0:3]
    gs["ffn_conv_b"] = dcb_ffn

    tm = _tile(T, _TM)
    tk = _tile(F, _TN_MAX)
    nkh = F // tk
    tkt = _tile(T, _TK_T)
    tok = emit("w_up", _matmul(
        "dw_up", "tn", (h2, dup),
        [pl.BlockSpec((tkt, D), lambda i, j, k: (k, 0)), pl.BlockSpec((None, tkt, tk), lambda i, j, k: (j // nkh, k, j % nkh))],
        jax.ShapeDtypeStruct((D, 2 * F), bf16), pl.BlockSpec((D, tk), lambda i, j, k: (0, j)),
        (1, 2 * nkh, T // tkt), (D, tk)))

    tkc = _tile(F, _TK_UP)
    nkc = F // tkc
    dx2, dx2b, gs["g_mlp"] = _norm_bwd_matmul(
        "dh2", dup, pl.BlockSpec((None, tm, tkc), lambda i, k: (k // nkc, i, k % nkc)), 2 * nkc,
        w_up, pl.BlockSpec((D, tkc), lambda i, k: (0, k)), x2, tie(p["g_mlp"], tok), dx3)

    tok = emit("w_out", _mm_tn("dw_out", merged, dx2b, bf16))
    dP, dQ, dl0, dl1, db0, db1 = _merge_bwd(dx2b, w_out, pp, qq, proj, tie(p["b_gate"], tok), PW)
    gs["b_gate"] = jnp.concatenate([db0, db1], axis=1)
    tok = emit("w_pool_proj", _mm_tn("dw_pool_proj", y_pool, dP, bf16))
    tok2 = emit("w_lru_proj", _mm_tn("dw_lru_proj", y_lru, dQ, bf16))
    dy_pool = _mm_nt("dy_pool", dP, w_pp, f32)
    dy_lru = _mm_nt("dy_lru", dQ, w_lp, f32)

    du_lru, du_gelu, dwa, dwi, dcw_lru, dvec = _lru_bwd(
        dy_lru, proj, hs, D, lru_conv_w, tie(p["lru_conv_b"], tok, tok2), w_a, p["b_a"], w_i, p["b_i"], p["lru_lambda"])
    tok = emit("w_a", dwa.astype(bf16))
    tok2 = emit("w_i", dwi.astype(bf16))
    gs["lru_conv_w"] = dcw_lru[0:4]
    gs["lru_conv_b"], gs["b_a"], gs["b_i"], gs["lru_lambda"] = dvec[0:1], dvec[1:2], dvec[2:3], dvec[3:4]
    du_pool, dwp, gs["pool_scale"] = _pool_bwd(dy_pool, proj, w_pool, tie(p["pool_scale"], tok, tok2))
    tok = emit("w_pool", dwp.astype(bf16))

    dproj = jnp.concatenate([du_pool, du_lru, du_gelu, dl0, dl1], axis=1)
    tok2 = emit("w_in", _mm_tn("dw_in", h1, dproj, bf16))
    NI = dproj.shape[1]
    tki = _tile(NI, _TN_MAX)
    grad_x, _, gs["g_mix"] = _norm_bwd_matmul(
        "dh1", dproj, pl.BlockSpec((tm, tki), lambda i, k: (i, k)), NI // tki,
        w_in, pl.BlockSpec((D, tki), lambda i, k: (0, k)), x, tie(p["g_mix"], tok, tok2), dx2)
    return loss_t[0, 0], grad_x, gs


_MATRICES = {"w_in": "col", "w_pool": "mid", "w_a": "mid", "w_i": "mid", "w_pool_proj": "col", "w_lru_proj": "row",
             "w_out": "row", "w_up": "col", "w_down": "row"}
_CONVS = ("lru_conv_w", "ffn_conv_w")
_GATHER_GROUPS = (("w_pool", "lru_conv_w", "w_a", "w_i"), ("w_pool_proj", "w_lru_proj", "w_out", "ffn_conv_w"),
                  ("w_up",), ("w_down",))
_GROUP_TWO_LEVEL = (False, True, True, True)
_SAME_CORE = (2, 4, 6)
_TWO_LEVEL = (1, 2, 4, 6)
_W_IN_PARTS = ((1,), (4,), (2,), (6,))
_W_IN_PART = 2
_VECTORS = ("g_mix", "b_gate", "pool_scale", "lru_conv_b", "b_a", "b_i", "lru_lambda", "g_mlp", "ffn_conv_b", "g_final")
_WEIGHTS = ("g_mix", "w_in", "b_gate", "w_pool", "pool_scale", "lru_conv_w", "lru_conv_b", "w_a", "b_a", "w_i", "b_i",
            "lru_lambda", "w_pool_proj", "w_lru_proj", "w_out", "g_mlp", "w_up", "ffn_conv_w", "ffn_conv_b", "w_down", "g_final")


def _full_shape(shape, kind):
    s = list(shape)
    s[{"col": 1, "row": 0, "mid": 1}[kind]] *= _N_DEV
    return tuple(s)


def _step(x, target, w, m, v):
    x, target = x[0], target[0]
    me = _blk(*_my_place())

    axis = {"col": 1, "row": 0, "mid": 1}
    kind = dict(_MATRICES, **{n: "col" for n in _CONVS})
    shard = {n: w[n].astype(bf16) for n in _MATRICES}
    shard.update({n: _pad_rows8(w[n]) for n in _CONVS})
    order = ["w_in"] + [n for grp in _GATHER_GROUPS for n in grp]
    index = {n: t for t, n in enumerate(order)}
    n_parts = len(_W_IN_PARTS)
    groups = [([0], ks) for ks in _W_IN_PARTS]
    groups += [([index[n] for n in grp], _TWO_LEVEL if two else _ALL_PEERS) for grp, two in zip(_GATHER_GROUPS, _GROUP_TWO_LEVEL)]
    g_routes = [_Route("gather", kind[n], shard[n].shape[axis[kind[n]]]) for n in order]
    lands = [lax.empty(_full_shape(shard[n].shape, kind[n]), shard[n].dtype) for n in order]
    g_sems, g_srcs, g_lands, _ = _send_start("gather_start", [shard[n] for n in order], lands, g_routes, groups)
    gathered, passing = {}, {}
    x_, y_, _c = _my_place()
    w_in_state = [[g_srcs[0]], [g_lands[0]]]

    def hint(name, after):
        if name == "w_in":
            srcs, got = w_in_state
            tok = None
            for part in range(1, n_parts):
                ks = _W_IN_PARTS[part]
                srcs, got = _send_wait(f"gather_wait_w_in_{part}", srcs, got, g_routes[:1], g_sems[part], after, ks)
                sems, got, tok = _forward_start(f"gather_pass_w_in_{part}", got, g_routes[:1], ks)
                passing[name, part] = sems
            w_in_state[:] = [srcs, got]
            return tok
        gi = next(i for i, grp in enumerate(_GATHER_GROUPS) if name in grp)
        if not _GROUP_TWO_LEVEL[gi] or gi in passing or _GATHER_GROUPS[gi][0] in gathered:
            return None
        ts = groups[n_parts + gi][0]
        routes = [g_routes[t] for t in ts]
        srcs, got = _send_wait(f"gather_wait_{gi}", [g_srcs[t] for t in ts], [g_lands[t] for t in ts], routes,
                               g_sems[n_parts + gi], after, _TWO_LEVEL)
        got = _place_own(f"gather_own_{gi}", srcs, got, routes)
        sems, got, tok = _forward_start(f"gather_pass_{gi}", got, routes, _SAME_CORE)
        passing[gi] = (sems, got, routes)
        return tok

    def get(name, after, part=None):
        if name == "w_in":
            ks = _W_IN_PARTS[part]
            if part == 0:
                srcs, got = _send_wait("gather_wait_w_in_0", *w_in_state, g_routes[:1], g_sems[0], after, ks)
                got = _place_own("gather_own_w_in", srcs, got, g_routes[:1])
                w_in_state[:] = [srcs, got]
            else:
                if (name, part) not in passing:
                    hint(name, after)
                got = _forward_wait(f"gather_got_w_in_{part}", w_in_state[1], g_routes[:1], passing[name, part], after, ks)
                w_in_state[1] = got
            k = ks[-1]
            base = 4 * (x_ ^ (k >> 2)) + 2 * (y_ ^ ((k >> 1) & 1))
            return got[0], base.astype(jnp.int32).reshape(1)
        if name not in gathered:
            gi = next(i for i, grp in enumerate(_GATHER_GROUPS) if name in grp)
            if _GROUP_TWO_LEVEL[gi]:
                hint(name, after)
                sems, got, routes = passing[gi]
                full = _forward_wait(f"gather_got_{gi}", got, routes, sems, after, _SAME_CORE)
            else:
                ts = groups[n_parts + gi][0]
                routes = [g_routes[t] for t in ts]
                srcs, got = _send_wait(f"gather_wait_{gi}", [g_srcs[t] for t in ts], [g_lands[t] for t in ts], routes,
                                       g_sems[n_parts + gi], after)
                full = _place_own(f"gather_own_{gi}", srcs, got, routes)
            gathered.update(zip(_GATHER_GROUPS[gi], full))
        return gathered[name]

    sent = {}

    def emit(name, grad):
        k = _MATRICES[name]
        size = w[name].shape[axis[k]]
        route = _Route("scatter", k, size)
        shp = list(grad.shape)
        shp[axis[k]] = size
        land = lax.empty((_N_DEV, *shp), grad.dtype)
        sems, srcs, lnds, token = _send_start("grad_start_" + name, [grad], [land], [route], [([0], _ALL_PEERS)])
        sent[name] = (srcs, lnds, [route], sems[0])
        return token

    p = {n: w[n].reshape(1, -1) for n in _VECTORS}
    loss_t, grad_x, gs = _local_step(x, target, p, get, emit, hint)
    loss = lax.psum(loss_t, ("x", "y", "c"))

    small_names = list(_VECTORS) + list(_CONVS)
    small_shapes = [tuple(gs[n].shape) for n in small_names]
    small_parts = _exchange([], [], [], [_pack([gs[n] for n in small_names])])[0]

    out = {}
    mats = list(_MATRICES)
    for n in mats:
        srcs, lnds, routes, sems = sent[n]
        srcs, got = _send_wait("grad_wait_" + n, srcs, lnds, routes, sems, grad_x)
        parts = _place_own("grad_own_" + n, srcs, got, routes)[0]
        shp = w[n].shape
        r2 = (-1, shp[-1])
        res = _adamw("adamw_" + n, w[n].reshape(r2), m[n].reshape(r2), v[n].reshape(r2),
                     parts.reshape((_N_DEV,) + w[n].reshape(r2).shape))
        out[n] = [a.reshape(shp) for a in res]
    gsum = _unpack(_sum_parts("sum_small", small_parts), small_shapes)
    gsmall = dict(zip(small_names, gsum))
    for n in _CONVS:
        cols = w[n].shape[1]
        gsmall[n] = lax.dynamic_slice_in_dim(gsmall[n], me * cols, cols, axis=1)
    pk = lambda d: _pack([d[n] for n in small_names])
    res = _adamw("adamw_small", pk(w), pk(m), pk(v), pk(gsmall)[None])
    shapes = [tuple(w[n].shape) for n in small_names]
    for k, arr in enumerate(res):
        for n, a in zip(small_names, _unpack(arr, shapes)):
            out.setdefault(n, [None] * 4)[k] = a
    return loss, grad_x[None], out


def kernel(x, g_mix, w_in, b_gate, w_pool, pool_scale, lru_conv_w, lru_conv_b, w_a, b_a, w_i, b_i, lru_lambda, w_pool_proj, w_lru_proj, w_out, g_mlp, w_up, ffn_conv_w, ffn_conv_b, w_down, g_final, loss_target, m_g_mix, m_w_in, m_b_gate, m_w_pool, m_pool_scale, m_lru_conv_w, m_lru_conv_b, m_w_a, m_b_a, m_w_i, m_b_i, m_lru_lambda, m_w_pool_proj, m_w_lru_proj, m_w_out, m_g_mlp, m_w_up, m_ffn_conv_w, m_ffn_conv_b, m_w_down, m_g_final, v_g_mix, v_w_in, v_b_gate, v_w_pool, v_pool_scale, v_lru_conv_w, v_lru_conv_b, v_w_a, v_b_a, v_w_i, v_b_i, v_lru_lambda, v_w_pool_proj, v_w_lru_proj, v_w_out, v_g_mlp, v_w_up, v_ffn_conv_w, v_ffn_conv_b, v_w_down, v_g_final):
    given = dict(locals())
    orig = {n: given[n].shape for n in _WEIGHTS}

    def squeeze(a, n):
        return a if n == "g_final" else a[0]

    w = {n: squeeze(given[n], n) for n in _WEIGHTS}
    m = {n: squeeze(given["m_" + n], n) for n in _WEIGHTS}
    v = {n: squeeze(given["v_" + n], n) for n in _WEIGHTS}
    for d in (w, m, v):
        d["g_final"] = d["g_final"].reshape(1, -1)
    loss, grad_x, out = _step(x, loss_target, w, m, v)
    res = [loss, grad_x]
    for k in range(4):
        res += [out[n][k].reshape(orig[n]) for n in _WEIGHTS]
    return tuple(res)
```

```python
import functools

import jax
import jax.numpy as jnp
from jax import lax
from jax.experimental import pallas as pl
from jax.experimental.pallas import tpu as pltpu

f32 = jnp.float32
bf16 = jnp.bfloat16

_EPS = 1e-6
_LRU_C = 8.0
_POOL_WINDOWS = (2, 4, 8, 16)
_POOL_HALO = 16
_CONV_HALO = 8
_GELU_C0 = 0.7978845608028654
_GELU_C1 = 0.044715
_ADAM_LR, _ADAM_B1, _ADAM_B2, _ADAM_EPS, _ADAM_WD, _ADAM_STEP = 0.001, 0.9, 0.999, 1e-08, 0.01, 10
_N_DEV = 8
_LANE = 128
_VMEM_LIMIT = 60 * 1024 * 1024

_TM = 512
_TM_SMALL = 256
_TC = 256
_TK_T = 1024
_TN_MAX = 1536
_CW = 1024
_TK_DOWN = 1536
_TK_UP = 2048


def _cparams(*sem):
    return pltpu.CompilerParams(dimension_semantics=tuple(sem), vmem_limit_bytes=_VMEM_LIMIT)


def _tile(n, want, mult=1):
    if n <= want:
        return n
    t = want - want % mult
    while n % t:
        t -= mult
    return t


def _gelu(x):
    t = jnp.tanh(x * (_GELU_C0 + (_GELU_C0 * _GELU_C1) * (x * x)))
    return x * (0.5 + 0.5 * t)


def _gelu_both(x):
    x2 = x * x
    t = jnp.tanh(x * (_GELU_C0 + (_GELU_C0 * _GELU_C1) * x2))
    h = 0.5 + 0.5 * t
    return x * h, h + (x * (1.0 - t * t)) * (0.5 * _GELU_C0 + (1.5 * _GELU_C0 * _GELU_C1) * x2)


def _fold8(x):
    out = x[0:8]
    for r in range(8, x.shape[0], 8):
        out = out + x[r:r + 8]
    return out


def _sigmoid(x):
    return jax.nn.sigmoid(x)


def _dot(a, b):
    return jnp.dot(a, b, preferred_element_type=f32)


def _dot_nt(a, b):
    return lax.dot_general(a, b, (((1,), (1,)), ((), ())), preferred_element_type=f32)


def _dot_tn(a, b):
    return lax.dot_general(a, b, (((0,), (0,)), ((), ())), preferred_element_type=f32)


def _rms_rows(x_ref, g_ref, h_ref, n_rows, n_cols):
    rr = _tile(n_rows, _SLAB_ROWS)
    cg = _tile(n_cols, _SLAB_COLS)
    for r0 in range(0, n_rows, rr):
        rows = pl.ds(r0, rr)
        x = x_ref[rows, :]
        r = lax.rsqrt(jnp.mean(x * x, axis=-1, keepdims=True) + _EPS)
        for c0 in range(0, n_cols, cg):
            cols = slice(c0, c0 + cg)
            h_ref[rows, cols] = (x_ref[rows, cols] * r * g_ref[:, cols]).astype(bf16)


def _matmul(name, mode, operands, in_specs, out_shape, out_spec, grid, acc_shape):
    dot = {"nn": _dot, "nt": _dot_nt, "tn": _dot_tn}[mode]
    nk = grid[2]

    def body_whole(a_ref, b_ref, o_ref):
        o_ref[...] = dot(a_ref[...], b_ref[...]).astype(o_ref.dtype)

    def body(a_ref, b_ref, o_ref, acc_ref):
        k = pl.program_id(2)

        @pl.when(k == 0)
        def _():
            acc_ref[...] = dot(a_ref[...], b_ref[...])

        @pl.when((k > 0) & (k < nk - 1))
        def _():
            acc_ref[...] += dot(a_ref[...], b_ref[...])

        @pl.when(k == nk - 1)
        def _():
            o_ref[...] = (acc_ref[...] + dot(a_ref[...], b_ref[...])).astype(o_ref.dtype)

    return pl.pallas_call(
        body_whole if nk == 1 else body, name=name, grid=grid, in_specs=in_specs, out_specs=out_spec, out_shape=out_shape,
        scratch_shapes=[] if nk == 1 else [pltpu.VMEM(acc_shape, f32)],
        compiler_params=_cparams("parallel", "parallel", "arbitrary"),
    )(*operands)


def _mm_nn(name, a, b, out_dtype, tm_want=None):
    M, K = a.shape
    N = b.shape[1]
    tm, tn = _tile(M, tm_want or _TM), _tile(N, _TN_MAX)
    return _matmul(
        name, "nn", (a, b),
        [pl.BlockSpec((tm, K), lambda i, j, k: (i, 0)), pl.BlockSpec((K, tn), lambda i, j, k: (0, j))],
        jax.ShapeDtypeStruct((M, N), out_dtype), pl.BlockSpec((tm, tn), lambda i, j, k: (i, j)),
        (M // tm, N // tn, 1), (tm, tn))


def _mm_nt(name, a, b, out_dtype):
    M, K = a.shape
    N = b.shape[0]
    tm, tn = _tile(M, _TM), _tile(N, _TN_MAX)
    return _matmul(
        name, "nt", (a, b),
        [pl.BlockSpec((tm, K), lambda i, j, k: (i, 0)), pl.BlockSpec((tn, K), lambda i, j, k: (j, 0))],
        jax.ShapeDtypeStruct((M, N), out_dtype), pl.BlockSpec((tm, tn), lambda i, j, k: (i, j)),
        (M // tm, N // tn, 1), (tm, tn))


def _mm_tn(name, a, b, out_dtype, tm_want=2048):
    T, M = a.shape
    N = b.shape[1]
    tm, tn, tk = _tile(M, tm_want), _tile(N, _TN_MAX), _tile(T, _TK_T)
    return _matmul(
        name, "tn", (a, b),
        [pl.BlockSpec((tk, tm), lambda i, j, k: (k, i)), pl.BlockSpec((tk, tn), lambda i, j, k: (k, j))],
        jax.ShapeDtypeStruct((M, N), out_dtype), pl.BlockSpec((tm, tn), lambda i, j, k: (i, j)),
        (M // tm, N // tn, T // tk), (tm, tn))


def _in_proj_first(x, g_mix, w_in, base, n_tiles):
    T, D = x.shape
    NI = w_in.shape[1]
    tm, tn = _tile(T, 2 * _TM), NI // _N_DEV

    def body(base_ref, x_ref, g_ref, w_ref, h_ref, o_ref):
        @pl.when(pl.program_id(1) == 0)
        def _():
            _rms_rows(x_ref, g_ref, h_ref, tm, D)

        o_ref[...] = _dot(h_ref[...], w_ref[...])

    return pl.pallas_call(
        body, name="in_proj_0",
        grid_spec=pltpu.PrefetchScalarGridSpec(
            num_scalar_prefetch=1, grid=(T // tm, n_tiles),
            in_specs=[pl.BlockSpec((tm, D), lambda i, j, b: (i, 0)), pl.BlockSpec((1, D), lambda i, j, b: (0, 0)),
                      pl.BlockSpec((D, tn), lambda i, j, b: (0, b[0] + j))],
            out_specs=[pl.BlockSpec((tm, D), lambda i, j, b: (i, 0)), pl.BlockSpec((tm, tn), lambda i, j, b: (i, b[0] + j))]),
        out_shape=[jax.ShapeDtypeStruct((T, D), bf16), jax.ShapeDtypeStruct((T, NI), f32)],
        compiler_params=_cparams("parallel", "arbitrary"),
    )(base, x, g_mix, w_in)


def _in_proj_more(name, h1, w_in, proj, base, n_tiles):
    T, D = h1.shape
    NI = w_in.shape[1]
    tm, tn = _tile(T, 2 * _TM), NI // _N_DEV

    def body(base_ref, h_ref, w_ref, proj_ref, o_ref):
        o_ref[...] = _dot(h_ref[...], w_ref[...])

    return pl.pallas_call(
        body, name=name,
        grid_spec=pltpu.PrefetchScalarGridSpec(
            num_scalar_prefetch=1, grid=(T // tm, n_tiles),
            in_specs=[pl.BlockSpec((tm, D), lambda i, j, b: (i, 0)), pl.BlockSpec((D, tn), lambda i, j, b: (0, b[0] + j)),
                      pl.BlockSpec(memory_space=pl.ANY)],
            out_specs=pl.BlockSpec((tm, tn), lambda i, j, b: (i, b[0] + j))),
        out_shape=jax.ShapeDtypeStruct((T, NI), f32),
        input_output_aliases={3: 0},
        compiler_params=_cparams("parallel", "arbitrary"),
    )(base, h1, w_in, proj)


def _window_means(ext_ref, row0, tc, gw):
    H = _POOL_HALO
    t_glob = row0 + lax.broadcasted_iota(jnp.int32, (tc, 1), 0)
    out = []
    for g, w in enumerate(_POOL_WINDOWS):
        s = ext_ref[:, g * gw:(g + 1) * gw]
        st = 1
        while st < w:
            s = s + pltpu.roll(s, st, 0)
            st *= 2
        cnt = jnp.minimum(t_glob + 1, w).astype(f32)
        out.append((s[H:, :] / cnt, ext_ref[pl.ds(H, tc), g * gw:(g + 1) * gw]))
    return out


def _pool_fwd(proj, w_pool, pool_scale):
    T = proj.shape[0]
    G, gw, _ = w_pool.shape
    PW = G * gw
    tc = _tile(T, _TC)
    H = _POOL_HALO

    def body(u_ref, w_ref, s_ref, y_ref, ext_ref):
        i = pl.program_id(0)

        @pl.when(i == 0)
        def _():
            ext_ref[pl.ds(0, H), :] = jnp.zeros((H, PW), f32)

        ext_ref[pl.ds(H, tc), :] = u_ref[...]
        for g, (m, u) in enumerate(_window_means(ext_ref, i * tc, tc, gw)):
            d = (m - u).astype(bf16)
            y = _dot(d, w_ref[g]) * s_ref[:, g * gw:(g + 1) * gw]
            y_ref[:, g * gw:(g + 1) * gw] = y.astype(bf16)
        ext_ref[pl.ds(0, H), :] = ext_ref[pl.ds(tc, H), :]

    return pl.pallas_call(
        body, name="pool_fwd", grid=(T // tc,),
        in_specs=[pl.BlockSpec((tc, PW), lambda i: (i, 0)), pl.BlockSpec((G, gw, gw), lambda i: (0, 0, 0)),
                  pl.BlockSpec((1, PW), lambda i: (0, 0))],
        out_specs=pl.BlockSpec((tc, PW), lambda i: (i, 0)),
        out_shape=jax.ShapeDtypeStruct((T, PW), bf16),
        scratch_shapes=[pltpu.VMEM((H + tc, PW), f32)],
        compiler_params=_cparams("arbitrary"),
    )(proj, w_pool, pool_scale)


def _softplus(z):
    return jnp.maximum(z, 0.0) + jnp.log1p(jnp.exp(-jnp.abs(z)))


def _causal_conv(ext_ref, cw_ref, cb_ref, n, K, cols=slice(None), r0=0):
    H = _CONV_HALO
    v = cb_ref[:, cols] + cw_ref[K - 1:K, cols] * ext_ref[pl.ds(H + r0, n), cols]
    for k in range(K - 1):
        v = v + cw_ref[k:k + 1, cols] * ext_ref[pl.ds(H + r0 - (K - 1 - k), n), cols]
    return v


_ROWS = 8
_SLAB_ROWS = 16
_SLAB_COLS = 512


def _slabs(n_rows, n_cols, reverse=False):
    cg = _tile(n_cols, _SLAB_COLS)
    rr = _tile(n_rows, _SLAB_ROWS)
    starts = range(0, n_rows, rr)
    for c0 in range(0, n_cols, cg):
        for r0 in (reversed(starts) if reverse else starts):
            yield slice(c0, c0 + cg), pl.ds(r0, rr)


def _slabs_with_sums(n_rows, n_cols, n_sums, visit, flush, reverse=False):
    cg = _tile(n_cols, _SLAB_COLS)
    rr = _tile(n_rows, _SLAB_ROWS)
    starts = list(range(0, n_rows, rr))
    for c0 in range(0, n_cols, cg):
        cols = slice(c0, c0 + cg)
        sums = [jnp.zeros((8, cg), f32) for _ in range(n_sums)]
        for r0 in (reversed(starts) if reverse else starts):
            sums = visit(cols, pl.ds(r0, rr), sums)
        flush(cols, [jnp.sum(s, axis=0, keepdims=True) for s in sums])


def _lru_fwd(proj, D, PW, conv_w, conv_b, w_a, b_a, w_i, b_i, lam):
    T = proj.shape[0]
    NB, bw, _ = w_a.shape
    K = 4
    tc = _tile(T, _TC)
    H = _CONV_HALO
    hb = D // 2
    assert PW == hb and conv_w.shape[0] == 8

    def body(u0_ref, u1_ref, g0_ref, g1_ref, cw_ref, cb_ref, wa_ref, ba_ref, wi_ref, bi_ref, lam_ref,
             y_ref, hs_ref, ext_ref, a_scr, b_scr, v_scr, hc_scr):
        c = pl.program_id(0)

        @pl.when(c == 0)
        def _():
            ext_ref[pl.ds(0, H), :] = jnp.zeros((H, D), f32)
            hc_scr[...] = jnp.zeros_like(hc_scr)

        ext_ref[pl.ds(H, tc), 0:hb] = u0_ref[...]
        ext_ref[pl.ds(H, tc), hb:D] = u1_ref[...]
        sp = _softplus(-lam_ref[...])
        for cols, rows in _slabs(tc, D):
            v_scr[rows, cols] = _causal_conv(ext_ref, cw_ref, cb_ref, rows.size, K, cols, rows.start)
        for b in range(NB):
            cols = slice(b * bw, (b + 1) * bw)
            vb = v_scr[:, cols].astype(bf16)
            a_scr[:, cols] = _dot(vb, wa_ref[b])
            b_scr[:, cols] = _dot(vb, wi_ref[b])
        for cols, rows in _slabs(tc, D):
            r = _sigmoid(a_scr[rows, cols] + ba_ref[:, cols])
            i = _sigmoid(b_scr[rows, cols] + bi_ref[:, cols])
            a = jnp.exp(-_LRU_C * r * sp[:, cols])
            a_scr[rows, cols] = a
            b_scr[rows, cols] = jnp.sqrt(1.0 - a * a) * (i * v_scr[rows, cols])

        def step(t, h):
            h = a_scr[pl.ds(t, 1), :] * h + b_scr[pl.ds(t, 1), :]
            hs_ref[pl.ds(t, 1), :] = h
            return h

        h = lax.fori_loop(0, tc, step, hc_scr[0:1, :], unroll=8)
        hc_scr[0:1, :] = h
        for cols, rows in _slabs(tc, D):
            g_ref, gcols = (g0_ref, cols) if cols.start < hb else (g1_ref, slice(cols.start - hb, cols.stop - hb))
            y_ref[rows, cols] = (hs_ref[rows, cols] * _gelu(g_ref[rows, gcols])).astype(bf16)
        ext_ref[pl.ds(0, H), :] = ext_ref[pl.ds(tc, H), :]

    vec = pl.BlockSpec((1, D), lambda c: (0, 0))
    wspec = pl.BlockSpec((NB, bw, bw), lambda c: (0, 0, 0))
    return pl.pallas_call(
        body, name="lru_fwd", grid=(T // tc,),
        in_specs=[pl.BlockSpec((tc, hb), lambda c: (c, 1)), pl.BlockSpec((tc, hb), lambda c: (c, 2)),
                  pl.BlockSpec((tc, hb), lambda c: (c, 3)), pl.BlockSpec((tc, hb), lambda c: (c, 4)),
                  pl.BlockSpec((8, D), lambda c: (0, 0)), vec, wspec, vec, wspec, vec, vec],
        out_specs=[pl.BlockSpec((tc, D), lambda c: (c, 0)), pl.BlockSpec((tc, D), lambda c: (c, 0))],
        out_shape=[jax.ShapeDtypeStruct((T, D), bf16), jax.ShapeDtypeStruct((T, D), f32)],
        scratch_shapes=[pltpu.VMEM((H + tc, D), f32), pltpu.VMEM((tc, D), f32), pltpu.VMEM((tc, D), f32),
                        pltpu.VMEM((tc, D), f32), pltpu.VMEM((8, D), f32)],
        compiler_params=_cparams("arbitrary"),
    )(proj, proj, proj, proj, conv_w, conv_b, w_a, b_a, w_i, b_i, lam)


def _merge_fwd(y_pool, y_lru, w_pp, w_lp, proj, b_gate):
    T, PW = y_pool.shape
    D = y_lru.shape[1]
    tm, tn = _tile(T, _TM), _tile(D, PW)
    nj = D // tn
    off = (PW + 2 * D) // tn

    def body(yp_ref, yl_ref, wp_ref, wl_ref, l0_ref, l1_ref, b0_ref, b1_ref, p_ref, q_ref, m_ref, p_scr, q_scr):
        p_scr[...] = _dot(yp_ref[...], wp_ref[...])
        q_scr[...] = _dot(yl_ref[...], wl_ref[...])
        for cols, rows in _slabs(tm, tn):
            p, q = p_scr[rows, cols], q_scr[rows, cols]
            g0 = _sigmoid(l0_ref[rows, cols] + b0_ref[:, cols])
            g1 = _sigmoid(l1_ref[rows, cols] + b1_ref[:, cols])
            m_ref[rows, cols] = (g0 * p + g1 * q).astype(bf16)
            p_ref[rows, cols] = p.astype(bf16)
            q_ref[rows, cols] = q.astype(bf16)

    tile = pl.BlockSpec((tm, tn), lambda j, i: (i, j))
    return pl.pallas_call(
        body, name="merge_fwd", grid=(nj, T // tm),
        in_specs=[pl.BlockSpec((tm, PW), lambda j, i: (i, 0)), pl.BlockSpec((tm, D), lambda j, i: (i, 0)),
                  pl.BlockSpec((PW, tn), lambda j, i: (0, j)), pl.BlockSpec((D, tn), lambda j, i: (0, j)),
                  pl.BlockSpec((tm, tn), lambda j, i: (i, off + j)), pl.BlockSpec((tm, tn), lambda j, i: (i, off + nj + j)),
                  pl.BlockSpec((1, tn), lambda j, i: (0, j)), pl.BlockSpec((1, tn), lambda j, i: (0, nj + j))],
        out_specs=[tile, tile, tile],
        out_shape=[jax.ShapeDtypeStruct((T, D), bf16), jax.ShapeDtypeStruct((T, D), bf16), jax.ShapeDtypeStruct((T, D), bf16)],
        scratch_shapes=[pltpu.VMEM((tm, tn), f32), pltpu.VMEM((tm, tn), f32)],
        compiler_params=_cparams("parallel", "arbitrary"),
    )(y_pool, y_lru, w_pp, w_lp, proj, proj, b_gate, b_gate)


def _out_proj(merged, w_out, x, g_mlp):
    T, D = x.shape
    tm = _tile(T, _TM_SMALL)

    def body(m_ref, w_ref, x_ref, g_ref, x2_ref, h2_ref):
        x2_ref[...] = x_ref[...] + _dot(m_ref[...], w_ref[...])
        _rms_rows(x2_ref, g_ref, h2_ref, tm, D)

    row = pl.BlockSpec((tm, D), lambda i: (i, 0))
    return pl.pallas_call(
        body, name="out_proj", grid=(T // tm,),
        in_specs=[row, pl.BlockSpec((D, D), lambda i: (0, 0)), row, pl.BlockSpec((1, D), lambda i: (0, 0))],
        out_specs=[row, row],
        out_shape=[jax.ShapeDtypeStruct((T, D), f32), jax.ShapeDtypeStruct((T, D), bf16)],
        compiler_params=_cparams("parallel"),
    )(merged, w_out, x, g_mlp)


def _ffn_fwd(up, conv_w, conv_b):
    T, F2 = up.shape
    F = F2 // 2
    K = 3
    tc = _tile(T, 2 * _TC)
    cw = _tile(F, _CW)
    ns = F // cw
    H = _CONV_HALO

    def body(gp_ref, val_ref, cw_ref, cb_ref, z_ref, ext_ref):
        @pl.when(pl.program_id(1) == 0)
        def _():
            ext_ref[pl.ds(0, H), :] = jnp.zeros((H, cw), f32)

        ext_ref[pl.ds(H, tc), :] = gp_ref[...]
        for cols, rows in _slabs(tc, cw):
            c = _causal_conv(ext_ref, cw_ref, cb_ref, rows.size, K, cols, rows.start)
            z_ref[rows, cols] = (_gelu(c) * val_ref[rows, cols]).astype(bf16)
        ext_ref[pl.ds(0, H), :] = ext_ref[pl.ds(tc, H), :]

    return pl.pallas_call(
        body, name="ffn_fwd", grid=(ns, T // tc),
        in_specs=[pl.BlockSpec((tc, cw), lambda s, c: (c, s)), pl.BlockSpec((tc, cw), lambda s, c: (c, ns + s)),
                  pl.BlockSpec((8, cw), lambda s, c: (0, s)), pl.BlockSpec((1, cw), lambda s, c: (0, s))],
        out_specs=pl.BlockSpec((tc, cw), lambda s, c: (c, s)),
        out_shape=jax.ShapeDtypeStruct((T, F), bf16),
        scratch_shapes=[pltpu.VMEM((H + tc, cw), f32)],
        compiler_params=_cparams("parallel", "arbitrary"),
    )(up, up, conv_w, conv_b)


def _down_loss(z, w_down, x2, target, g_final):
    T, F = z.shape
    D = x2.shape[1]
    tm, tk = _tile(T, _TM), _tile(F, _TK_DOWN)
    nk = F // tk

    def body(z_ref, w_ref, x2_ref, t_ref, g_ref, dx_ref, dxb_ref, loss_ref, dg_ref, acc_ref):
        i, k = pl.program_id(0), pl.program_id(1)

        @pl.when(k == 0)
        def _():
            acc_ref[...] = x2_ref[...]

        @pl.when((i == 0) & (k == 0))
        def _():
            loss_ref[...] = jnp.zeros_like(loss_ref)
            dg_ref[...] = jnp.zeros_like(dg_ref)

        acc_ref[...] += _dot(z_ref[...], w_ref[...])

        @pl.when(k == nk - 1)
        def _():
            g = g_ref[...]
            sq = jnp.zeros((_ROWS, 1), f32)
            dgs = jnp.zeros((_ROWS, D), f32)
            for r0 in range(0, tm, _ROWS):
                rows = pl.ds(r0, _ROWS)
                x3 = acc_ref[rows, :]
                r = lax.rsqrt(jnp.mean(x3 * x3, axis=-1, keepdims=True) + _EPS)
                xr = x3 * r
                e = xr * g - t_ref[rows, :]
                sq = sq + jnp.sum(e * e, axis=-1, keepdims=True)
                dy = e * (1.0 / D)
                gy = dy * g
                dx = r * gy - x3 * ((r * r * r) * jnp.mean(x3 * gy, axis=-1, keepdims=True))
                dgs = dgs + dy * xr
                dx_ref[rows, :] = dx
                dxb_ref[rows, :] = dx.astype(bf16)
            loss_ref[...] += (0.5 / D) * jnp.sum(sq)
            dg_ref[...] += jnp.sum(dgs, axis=0, keepdims=True)

    row = pl.BlockSpec((tm, D), lambda i, k: (i, 0))
    vec = pl.BlockSpec((1, D), lambda i, k: (0, 0))
    return pl.pallas_call(
        body, name="down_loss", grid=(T // tm, nk),
        in_specs=[pl.BlockSpec((tm, tk), lambda i, k: (i, k)), pl.BlockSpec((tk, D), lambda i, k: (k, 0)), row, row, vec],
        out_specs=[row, row, pl.BlockSpec((8, _LANE), lambda i, k: (0, 0)), vec],
        out_shape=[jax.ShapeDtypeStruct((T, D), f32), jax.ShapeDtypeStruct((T, D), bf16),
                   jax.ShapeDtypeStruct((8, _LANE), f32), jax.ShapeDtypeStruct((1, D), f32)],
        scratch_shapes=[pltpu.VMEM((tm, D), f32)],
        compiler_params=_cparams("arbitrary", "arbitrary"),
    )(z, w_down, x2, target, g_final)


def _ffn_bwd(dz, up, conv_w, conv_b):
    T, F = dz.shape
    K = 3
    tc = _tile(T, 2 * _TC)
    cw = _tile(F, _CW)
    ns, nt = F // cw, T // tc
    H = _CONV_HALO

    def body(dz_ref, gp_ref, val_ref, gph_ref, cw_ref, cb_ref, dup_ref, dcw_ref, dcb_ref, ext_ref, dext_ref):
        j = pl.program_id(1)
        first = j == nt - 1

        @pl.when(j == 0)
        def _():
            dext_ref[pl.ds(tc, H), :] = jnp.zeros((H, cw), f32)
            dcw_ref[...] = jnp.zeros_like(dcw_ref)
            dcb_ref[...] = jnp.zeros_like(dcb_ref)

        ext_ref[pl.ds(0, H), :] = jnp.where(first, 0.0, gph_ref[...])
        ext_ref[pl.ds(H, tc), :] = gp_ref[...]
        def visit(cols, rows, sums):
            r0, n = rows.start, rows.size
            gp = ext_ref[pl.ds(H + r0, n), cols]
            c = _causal_conv(ext_ref, cw_ref, cb_ref, n, K, cols, r0)
            ge, gg = _gelu_both(c)
            dzv = dz_ref[rows, cols].astype(f32)
            dup_ref[1, rows, cols] = (dzv * ge).astype(bf16)
            dc = dzv * val_ref[rows, cols] * gg
            dext_ref[rows, cols] = dc
            dgp = cw_ref[K - 1:K, cols] * dc
            new = [None] * K + [sums[K] + _fold8(dc)]
            new[K - 1] = sums[K - 1] + _fold8(gp * dc)
            for k in range(K - 1):
                sh = dext_ref[pl.ds(r0 + K - 1 - k, n), cols]
                dgp = dgp + cw_ref[k:k + 1, cols] * sh
                new[k] = sums[k] + _fold8(gp * sh)
            dup_ref[0, rows, cols] = dgp.astype(bf16)
            return new

        def flush(cols, totals):
            for k in range(K):
                dcw_ref[k:k + 1, cols] += totals[k]
            dcb_ref[:, cols] += totals[K]

        _slabs_with_sums(tc, cw, K + 1, visit, flush, reverse=True)
        dext_ref[pl.ds(tc, H), :] = dext_ref[pl.ds(0, H), :]

    hblk = tc // H
    return pl.pallas_call(
        body, name="ffn_bwd", grid=(ns, nt),
        in_specs=[pl.BlockSpec((tc, cw), lambda s, j: (nt - 1 - j, s)),
                  pl.BlockSpec((tc, cw), lambda s, j: (nt - 1 - j, s)),
                  pl.BlockSpec((tc, cw), lambda s, j: (nt - 1 - j, ns + s)),
                  pl.BlockSpec((H, cw), lambda s, j: (jnp.maximum((nt - 1 - j) * hblk - 1, 0), s)),
                  pl.BlockSpec((8, cw), lambda s, j: (0, s)), pl.BlockSpec((1, cw), lambda s, j: (0, s))],
        out_specs=[pl.BlockSpec((2, tc, cw), lambda s, j: (0, nt - 1 - j, s)),
                   pl.BlockSpec((8, cw), lambda s, j: (0, s)), pl.BlockSpec((1, cw), lambda s, j: (0, s))],
        out_shape=[jax.ShapeDtypeStruct((2, T, F), bf16), jax.ShapeDtypeStruct((8, F), f32), jax.ShapeDtypeStruct((1, F), f32)],
        scratch_shapes=[pltpu.VMEM((H + tc, cw), f32), pltpu.VMEM((tc + H, cw), f32)],
        compiler_params=_cparams("parallel", "arbitrary"),
    )(dz, up, up, up, conv_w, conv_b)


def _norm_bwd_matmul(name, a, a_spec, nk, w, w_spec, x, g, dres):
    T, D = x.shape
    tm = a_spec.block_shape[-2]

    def body(a_ref, w_ref, x_ref, g_ref, dr_ref, dx_ref, dxb_ref, dg_ref, acc_ref):
        i, k = pl.program_id(0), pl.program_id(1)

        @pl.when((i == 0) & (k == 0))
        def _():
            dg_ref[...] = jnp.zeros_like(dg_ref)

        @pl.when(k == 0)
        def _():
            acc_ref[...] = _dot_nt(a_ref[...], w_ref[...])

        @pl.when(k > 0)
        def _():
            acc_ref[...] += _dot_nt(a_ref[...], w_ref[...])

        @pl.when(k == nk - 1)
        def _():
            g = g_ref[...]
            dgs = jnp.zeros((_ROWS, D), f32)
            for r0 in range(0, tm, _ROWS):
                rows = pl.ds(r0, _ROWS)
                x = x_ref[rows, :]
                dh = acc_ref[rows, :]
                r = lax.rsqrt(jnp.mean(x * x, axis=-1, keepdims=True) + _EPS)
                gy = dh * g
                dx = dr_ref[rows, :] + (r * gy - x * ((r * r * r) * jnp.mean(x * gy, axis=-1, keepdims=True)))
                dgs = dgs + dh * (x * r)
                dx_ref[rows, :] = dx
                dxb_ref[rows, :] = dx.astype(bf16)
            dg_ref[...] += jnp.sum(dgs, axis=0, keepdims=True)

    row = pl.BlockSpec((tm, D), lambda i, k: (i, 0))
    vec = pl.BlockSpec((1, D), lambda i, k: (0, 0))
    return pl.pallas_call(
        body, name=name, grid=(T // tm, nk),
        in_specs=[a_spec, w_spec, row, vec, row],
        out_specs=[row, row, vec],
        out_shape=[jax.ShapeDtypeStruct((T, D), f32), jax.ShapeDtypeStruct((T, D), bf16), jax.ShapeDtypeStruct((1, D), f32)],
        scratch_shapes=[pltpu.VMEM((tm, D), f32)],
        compiler_params=_cparams("arbitrary", "arbitrary"),
    )(a, w, x, g, dres)


def _merge_bwd(dx2b, w_out, p, q, proj, b_gate, PW):
    T, D = p.shape
    tm, tn = _tile(T, _TM), _tile(D, PW)
    nj = D // tn
    off = (PW + 2 * D) // tn

    def body(dx_ref, w_ref, p_ref, q_ref, l0_ref, l1_ref, b0_ref, b1_ref, dp_ref, dq_ref, dl0_ref, dl1_ref, db0_ref, db1_ref,
             dm_ref):
        @pl.when(pl.program_id(1) == 0)
        def _():
            db0_ref[...] = jnp.zeros_like(db0_ref)
            db1_ref[...] = jnp.zeros_like(db1_ref)

        dm_ref[...] = _dot_nt(dx_ref[...], w_ref[...])

        def visit(cols, rows, sums):
            dm = dm_ref[rows, cols]
            g0 = _sigmoid(l0_ref[rows, cols] + b0_ref[:, cols])
            g1 = _sigmoid(l1_ref[rows, cols] + b1_ref[:, cols])
            dp_ref[rows, cols] = (g0 * dm).astype(bf16)
            dq_ref[rows, cols] = (g1 * dm).astype(bf16)
            dl0 = dm * p_ref[rows, cols].astype(f32) * (g0 * (1.0 - g0))
            dl1 = dm * q_ref[rows, cols].astype(f32) * (g1 * (1.0 - g1))
            dl0_ref[rows, cols] = dl0.astype(bf16)
            dl1_ref[rows, cols] = dl1.astype(bf16)
            return [sums[0] + _fold8(dl0), sums[1] + _fold8(dl1)]

        def flush(cols, totals):
            db0_ref[:, cols] += totals[0]
            db1_ref[:, cols] += totals[1]

        _slabs_with_sums(tm, tn, 2, visit, flush)

    tile = pl.BlockSpec((tm, tn), lambda j, i: (i, j))
    vecj = pl.BlockSpec((1, tn), lambda j, i: (0, j))
    tb = jax.ShapeDtypeStruct((T, D), bf16)
    vb = jax.ShapeDtypeStruct((1, D), f32)
    return pl.pallas_call(
        body, name="merge_bwd", grid=(nj, T // tm),
        in_specs=[pl.BlockSpec((tm, D), lambda j, i: (i, 0)), pl.BlockSpec((tn, D), lambda j, i: (j, 0)), tile, tile,
                  pl.BlockSpec((tm, tn), lambda j, i: (i, off + j)), pl.BlockSpec((tm, tn), lambda j, i: (i, off + nj + j)),
                  vecj, pl.BlockSpec((1, tn), lambda j, i: (0, nj + j))],
        out_specs=[tile, tile, tile, tile, vecj, vecj],
        out_shape=[tb, tb, tb, tb, vb, vb],
        scratch_shapes=[pltpu.VMEM((tm, tn), f32)],
        compiler_params=_cparams("parallel", "arbitrary"),
    )(dx2b, w_out, p, q, proj, proj, b_gate, b_gate)


def _lru_bwd(dy, proj, hs, D, conv_w, conv_b, w_a, b_a, w_i, b_i, lam):
    T = dy.shape[0]
    NB, bw, _ = w_a.shape
    K = 4
    tc = _tile(T, _TC)
    nt = T // tc
    H = _CONV_HALO
    hb = D // 2

    def body(dy_ref, u0_ref, u1_ref, g0_ref, g1_ref, hs_ref, uh0_ref, uh1_ref, hh_ref,
             cw_ref, cb_ref, wa_ref, ba_ref, wi_ref, bi_ref, lam_ref,
             du_ref, dg_ref, dwa_ref, dwi_ref, dcw_ref, dvec_ref,
             ext_ref, hext_ref, dext_ref, a_scr, r_scr, i_scr, v_scr, g_scr, car_scr):
        j = pl.program_id(0)
        first = j == nt - 1

        @pl.when(j == 0)
        def _():
            dext_ref[pl.ds(tc, H), :] = jnp.zeros((H, D), f32)
            car_scr[...] = jnp.zeros_like(car_scr)
            dwa_ref[...] = jnp.zeros_like(dwa_ref)
            dwi_ref[...] = jnp.zeros_like(dwi_ref)
            dcw_ref[...] = jnp.zeros_like(dcw_ref)
            dvec_ref[...] = jnp.zeros_like(dvec_ref)

        ext_ref[pl.ds(0, H), 0:hb] = jnp.where(first, 0.0, uh0_ref[...])
        ext_ref[pl.ds(0, H), hb:D] = jnp.where(first, 0.0, uh1_ref[...])
        ext_ref[pl.ds(H, tc), 0:hb] = u0_ref[...]
        ext_ref[pl.ds(H, tc), hb:D] = u1_ref[...]
        hext_ref[pl.ds(0, H), :] = jnp.where(first, 0.0, hh_ref[...])
        hext_ref[pl.ds(H, tc), :] = hs_ref[...]
        lamv = lam_ref[...]
        sp = _softplus(-lamv)

        for cols, rows in _slabs(tc, D):
            v_scr[rows, cols] = _causal_conv(ext_ref, cw_ref, cb_ref, rows.size, K, cols, rows.start)
        for b in range(NB):
            cols = slice(b * bw, (b + 1) * bw)
            vb = v_scr[:, cols].astype(bf16)
            r_scr[:, cols] = _dot(vb, wa_ref[b])
            i_scr[:, cols] = _dot(vb, wi_ref[b])
        for cols, rows in _slabs(tc, D):
            r = _sigmoid(r_scr[rows, cols] + ba_ref[:, cols])
            r_scr[rows, cols] = r
            i_scr[rows, cols] = _sigmoid(i_scr[rows, cols] + bi_ref[:, cols])
            a_scr[rows, cols] = jnp.exp(-_LRU_C * r * sp[:, cols])
            g_ref, gcols = (g0_ref, cols) if cols.start < hb else (g1_ref, slice(cols.start - hb, cols.stop - hb))
            ge, gg = _gelu_both(g_ref[rows, gcols])
            dyv = dy_ref[rows, cols]
            g_scr[rows, cols] = dyv * ge
            dg_ref[rows, cols] = (dyv * hs_ref[rows, cols] * gg).astype(bf16)

        def step(s, carry):
            t = tc - 1 - s
            g = g_scr[pl.ds(t, 1), :] + carry
            g_scr[pl.ds(t, 1), :] = g
            return a_scr[pl.ds(t, 1), :] * g

        car_scr[0:1, :] = lax.fori_loop(0, tc, step, car_scr[0:1, :], unroll=8)

        def gates(cols, rows, sums):
            g, v, r, i, a = g_scr[rows, cols], v_scr[rows, cols], r_scr[rows, cols], i_scr[rows, cols], a_scr[rows, cols]
            mult = jnp.sqrt(1.0 - a * a)
            h_prev = hext_ref[pl.ds(H - 1 + rows.start, rows.size), cols]
            gm = g * mult
            dext_ref[rows, cols] = gm * i
            dlog_a = (g * h_prev - g * (i * v) * (a / mult)) * a
            dpr = dlog_a * (-_LRU_C * sp[:, cols]) * (r * (1.0 - r))
            dpi = gm * v * (i * (1.0 - i))
            r_scr[rows, cols] = dpr
            i_scr[rows, cols] = dpi
            return [sums[0] + _fold8(dpr), sums[1] + _fold8(dpi), sums[2] + _fold8(dlog_a * (-_LRU_C * r))]

        def gates_flush(cols, totals):
            dvec_ref[1:2, cols] += totals[0]
            dvec_ref[2:3, cols] += totals[1]
            dvec_ref[3:4, cols] += totals[2] * (-_sigmoid(-lamv[:, cols]))

        _slabs_with_sums(tc, D, 3, gates, gates_flush)
        for b in range(NB):
            cols = slice(b * bw, (b + 1) * bw)
            dprb, dpib, vb = r_scr[:, cols].astype(bf16), i_scr[:, cols].astype(bf16), v_scr[:, cols].astype(bf16)
            dext_ref[pl.ds(0, tc), cols] += _dot_nt(dprb, wa_ref[b]) + _dot_nt(dpib, wi_ref[b])
            dwa_ref[b] += _dot_tn(vb, dprb)
            dwi_ref[b] += _dot_tn(vb, dpib)

        def conv_t(cols, rows, sums):
            r0, n = rows.start, rows.size
            u = ext_ref[pl.ds(H + r0, n), cols]
            dv = dext_ref[rows, cols]
            du = cw_ref[K - 1:K, cols] * dv
            new = [None] * K + [sums[K] + _fold8(dv)]
            new[K - 1] = sums[K - 1] + _fold8(u * dv)
            for k in range(K - 1):
                sh = dext_ref[pl.ds(r0 + K - 1 - k, n), cols]
                du = du + cw_ref[k:k + 1, cols] * sh
                new[k] = sums[k] + _fold8(u * sh)
            du_ref[rows, cols] = du.astype(bf16)
            return new

        def conv_t_flush(cols, totals):
            for k in range(K):
                dcw_ref[k:k + 1, cols] += totals[k]
            dvec_ref[0:1, cols] += totals[K]

        _slabs_with_sums(tc, D, K + 1, conv_t, conv_t_flush)
        dext_ref[pl.ds(tc, H), :] = dext_ref[pl.ds(0, H), :]

    hblk = tc // H
    rev = lambda j: nt - 1 - j
    halo = lambda j: jnp.maximum((nt - 1 - j) * hblk - 1, 0)
    vec = pl.BlockSpec((1, D), lambda j: (0, 0))
    wspec = pl.BlockSpec((NB, bw, bw), lambda j: (0, 0, 0))
    acc8 = pl.BlockSpec((8, D), lambda j: (0, 0))
    big = pltpu.VMEM((tc, D), f32)
    return pl.pallas_call(
        body, name="lru_bwd", grid=(nt,),
        in_specs=[pl.BlockSpec((tc, D), lambda j: (rev(j), 0)),
                  pl.BlockSpec((tc, hb), lambda j: (rev(j), 1)), pl.BlockSpec((tc, hb), lambda j: (rev(j), 2)),
                  pl.BlockSpec((tc, hb), lambda j: (rev(j), 3)), pl.BlockSpec((tc, hb), lambda j: (rev(j), 4)),
                  pl.BlockSpec((tc, D), lambda j: (rev(j), 0)),
                  pl.BlockSpec((H, hb), lambda j: (halo(j), 1)), pl.BlockSpec((H, hb), lambda j: (halo(j), 2)),
                  pl.BlockSpec((H, D), lambda j: (halo(j), 0)),
                  acc8, vec, wspec, vec, wspec, vec, vec],
        out_specs=[pl.BlockSpec((tc, D), lambda j: (rev(j), 0)), pl.BlockSpec((tc, D), lambda j: (rev(j), 0)),
                   wspec, wspec, acc8, acc8],
        out_shape=[jax.ShapeDtypeStruct((T, D), bf16), jax.ShapeDtypeStruct((T, D), bf16),
                   jax.ShapeDtypeStruct((NB, bw, bw), f32), jax.ShapeDtypeStruct((NB, bw, bw), f32),
                   jax.ShapeDtypeStruct((8, D), f32), jax.ShapeDtypeStruct((8, D), f32)],
        scratch_shapes=[pltpu.VMEM((H + tc, D), f32), pltpu.VMEM((H + tc, D), f32), pltpu.VMEM((tc + H, D), f32),
                        big, big, big, big, big, pltpu.VMEM((8, D), f32)],
        compiler_params=_cparams("arbitrary"),
    )(dy, proj, proj, proj, proj, hs, proj, proj, hs, conv_w, conv_b, w_a, b_a, w_i, b_i, lam)


def _pool_bwd(dy, proj, w_pool, pool_scale):
    T, PW = dy.shape
    G, gw, _ = w_pool.shape
    tc = _tile(T, _TC)
    nt = T // tc
    H = _POOL_HALO

    def body(dy_ref, u_ref, uh_ref, w_ref, s_ref, du_ref, dw_ref, ds_ref, ext_ref, eext_ref):
        j = pl.program_id(0)
        first = j == nt - 1
        row0 = (nt - 1 - j) * tc

        @pl.when(j == 0)
        def _():
            eext_ref[pl.ds(tc, H), :] = jnp.zeros((H, PW), f32)
            dw_ref[...] = jnp.zeros_like(dw_ref)
            ds_ref[...] = jnp.zeros_like(ds_ref)

        ext_ref[pl.ds(0, H), :] = jnp.where(first, 0.0, uh_ref[...])
        ext_ref[pl.ds(H, tc), :] = u_ref[...]
        t_glob = row0 + lax.broadcasted_iota(jnp.int32, (tc, 1), 0)
        dds = []
        for g, (m, u) in enumerate(_window_means(ext_ref, row0, tc, gw)):
            cols = slice(g * gw, (g + 1) * gw)
            d = (m - u).astype(bf16)
            yraw = _dot(d, w_ref[g])
            dyv = dy_ref[:, cols]
            ds_ref[:, cols] += jnp.sum(dyv * yraw, axis=0, keepdims=True)
            dyr = (dyv * s_ref[:, cols]).astype(bf16)
            dd = _dot_nt(dyr, w_ref[g])
            dw_ref[g] += _dot_tn(d, dyr)
            cnt = jnp.minimum(t_glob + 1, _POOL_WINDOWS[g]).astype(f32)
            eext_ref[pl.ds(0, tc), cols] = dd / cnt
            dds.append(dd)
        n = tc + H
        for g, w in enumerate(_POOL_WINDOWS):
            cols = slice(g * gw, (g + 1) * gw)
            s = eext_ref[:, cols]
            st = 1
            while st < w:
                s = s + pltpu.roll(s, n - st, 0)
                st *= 2
            du_ref[:, cols] = (s[0:tc, :] - dds[g]).astype(bf16)
        eext_ref[pl.ds(tc, H), :] = eext_ref[pl.ds(0, H), :]

    hblk = tc // H
    return pl.pallas_call(
        body, name="pool_bwd", grid=(nt,),
        in_specs=[pl.BlockSpec((tc, PW), lambda j: (nt - 1 - j, 0)), pl.BlockSpec((tc, PW), lambda j: (nt - 1 - j, 0)),
                  pl.BlockSpec((H, PW), lambda j: (jnp.maximum((nt - 1 - j) * hblk - 1, 0), 0)),
                  pl.BlockSpec((G, gw, gw), lambda j: (0, 0, 0)), pl.BlockSpec((1, PW), lambda j: (0, 0))],
        out_specs=[pl.BlockSpec((tc, PW), lambda j: (nt - 1 - j, 0)), pl.BlockSpec((G, gw, gw), lambda j: (0, 0, 0)),
                   pl.BlockSpec((1, PW), lambda j: (0, 0))],
        out_shape=[jax.ShapeDtypeStruct((T, PW), bf16), jax.ShapeDtypeStruct((G, gw, gw), f32), jax.ShapeDtypeStruct((1, PW), f32)],
        scratch_shapes=[pltpu.VMEM((H + tc, PW), f32), pltpu.VMEM((tc + H, PW), f32)],
        compiler_params=_cparams("arbitrary"),
    )(dy, proj, proj, w_pool, pool_scale)


_MESH = pl.DeviceIdType.MESH
_HBM = pl.BlockSpec(memory_space=pltpu.HBM)


def _slab(ref, kind, blk, n):
    start = blk * n
    if n % _LANE == 0:
        start = pl.multiple_of(start, _LANE)
    if kind == "col":
        return ref.at[:, pl.ds(start, n)]
    if kind == "row":
        return ref.at[pl.ds(start, n), :]
    if kind == "mid":
        return ref.at[:, pl.ds(start, n), :]
    raise ValueError(kind)


def _my_place():
    x, y, c = lax.axis_index("x"), lax.axis_index("y"), lax.axis_index("c")
    return x, y, c


def _blk(px, py, pc):
    return 4 * px + 2 * py + pc


_SEM = pl.BlockSpec(memory_space=pltpu.SEMAPHORE)
_EFFECT = pltpu.SideEffectType.DATAFLOW_SIDE_EFFECTING


_ALL_PEERS = (1, 2, 3, 4, 5, 6, 7)


def _peers(x, y, c):
    return [(k, (x ^ (k >> 2), y ^ ((k >> 1) & 1), c ^ (k & 1))) for k in range(1, 8)]


class _Route:
    def __init__(self, mode, kind, size):
        self.mode, self.kind, self.size = mode, kind, size

    def src(self, ref, peer_blk):
        return ref if self.mode == "gather" else _slab(ref, self.kind, peer_blk, self.size)

    def dst(self, ref, origin_blk):
        return _slab(ref, self.kind, origin_blk, self.size) if self.mode == "gather" else ref.at[origin_blk]


def _send_start(name, srcs, lands, routes, groups):
    nt, ng = len(srcs), len(groups)

    def body(*refs):
        src_refs, land_refs = refs[:nt], refs[nt:2 * nt]
        sems = refs[2 * nt:2 * nt + 2 * ng]
        token = refs[-1]
        x, y, c = _my_place()
        me = _blk(x, y, c)
        for gi, (grp, ks) in enumerate(groups):
            for pos, t in enumerate(grp):
                for k, peer in _peers(x, y, c):
                    if k in ks:
                        s = len(ks) * pos + ks.index(k)
                        pltpu.make_async_remote_copy(
                            src_ref=routes[t].src(src_refs[t], _blk(*peer)), dst_ref=routes[t].dst(land_refs[t], me),
                            send_sem=sems[2 * gi].at[s], recv_sem=sems[2 * gi + 1].at[s],
                            device_id=peer, device_id_type=_MESH).start()
        token[...] = jnp.zeros_like(token)

    hbm = lambda a: pltpu.HBM(a.shape, a.dtype)
    out_shape = []
    for grp, ks in groups:
        out_shape += [pltpu.SemaphoreType.DMA((len(ks) * len(grp),)), pltpu.SemaphoreType.DMA((len(ks) * len(grp),))]
    out_shape += [hbm(a) for a in srcs] + [hbm(a) for a in lands] + [jax.ShapeDtypeStruct((8, _LANE), f32)]
    res = pl.pallas_call(
        body, name=name, out_shape=out_shape,
        in_specs=[_HBM] * (2 * nt),
        out_specs=[_SEM] * (2 * ng) + [_HBM] * (2 * nt) + [pl.BlockSpec(memory_space=pltpu.VMEM)],
        input_output_aliases={t: 2 * ng + t for t in range(2 * nt)},
        compiler_params=pltpu.CompilerParams(has_side_effects=_EFFECT),
    )(*[pltpu.with_memory_space_constraint(a, pltpu.HBM) for a in list(srcs) + list(lands)])
    sems = [(res[2 * g], res[2 * g + 1]) for g in range(ng)]
    return sems, res[2 * ng:2 * ng + nt], res[2 * ng + nt:2 * ng + 2 * nt], res[-1]


def _send_wait(name, srcs, lands, routes, sems, after, ks=_ALL_PEERS):
    n = len(srcs)

    def body(*refs):
        src_refs, land_refs = refs[:n], refs[n:2 * n]
        send_sems, recv_sems = refs[2 * n], refs[2 * n + 1]
        x, y, c = _my_place()
        for pos in range(n):
            for k, peer in _peers(x, y, c):
                if k in ks:
                    pb = _blk(*peer)
                    s = len(ks) * pos + ks.index(k)
                    cp = pltpu.make_async_remote_copy(
                        src_ref=routes[pos].src(src_refs[pos], pb), dst_ref=routes[pos].dst(land_refs[pos], pb),
                        send_sem=send_sems.at[s], recv_sem=recv_sems.at[s], device_id=peer, device_id_type=_MESH)
                    cp.wait_send()
                    cp.wait_recv()

    hbm = lambda a: pltpu.HBM(a.shape, a.dtype)
    res = pl.pallas_call(
        body, name=name, out_shape=[hbm(a) for a in srcs] + [hbm(a) for a in lands],
        in_specs=[_HBM] * (2 * n) + [_SEM, _SEM, pl.BlockSpec(memory_space=pl.ANY)],
        out_specs=[_HBM] * (2 * n),
        input_output_aliases={t: t for t in range(2 * n)},
        compiler_params=pltpu.CompilerParams(has_side_effects=_EFFECT),
    )(*srcs, *lands, sems[0], sems[1], after)
    return res[:n], res[n:]


def _forward_start(name, lands, routes, ks):
    n = len(lands)

    def body(*refs):
        land_refs, send_sems, recv_sems, token = refs[:n], refs[n], refs[n + 1], refs[-1]
        x, y, c = _my_place()
        for pos in range(n):
            for i, k in enumerate(ks):
                part = routes[pos].dst(land_refs[pos], _blk(x ^ (k >> 2), y ^ ((k >> 1) & 1), c))
                pltpu.make_async_remote_copy(
                    src_ref=part, dst_ref=part, send_sem=send_sems.at[len(ks) * pos + i],
                    recv_sem=recv_sems.at[len(ks) * pos + i], device_id=(x, y, 1 - c), device_id_type=_MESH).start()
        token[...] = jnp.zeros_like(token)

    hbm = lambda a: pltpu.HBM(a.shape, a.dtype)
    sem = pltpu.SemaphoreType.DMA((len(ks) * n,))
    res = pl.pallas_call(
        body, name=name, out_shape=[sem, sem] + [hbm(a) for a in lands] + [jax.ShapeDtypeStruct((8, _LANE), f32)],
        in_specs=[_HBM] * n, out_specs=[_SEM, _SEM] + [_HBM] * n + [pl.BlockSpec(memory_space=pltpu.VMEM)],
        input_output_aliases={t: 2 + t for t in range(n)},
        compiler_params=pltpu.CompilerParams(has_side_effects=_EFFECT),
    )(*[pltpu.with_memory_space_constraint(a, pltpu.HBM) for a in lands])
    return (res[0], res[1]), res[2:2 + n], res[-1]


def _forward_wait(name, lands, routes, sems, after, ks):
    n = len(lands)

    def body(*refs):
        land_refs, send_sems, recv_sems = refs[:n], refs[n], refs[n + 1]
        x, y, c = _my_place()
        for pos in range(n):
            for i, k in enumerate(ks):
                px, py = x ^ (k >> 2), y ^ ((k >> 1) & 1)
                cp = pltpu.make_async_remote_copy(
                    src_ref=routes[pos].dst(land_refs[pos], _blk(px, py, c)),
                    dst_ref=routes[pos].dst(land_refs[pos], _blk(px, py, 1 - c)),
                    send_sem=send_sems.at[len(ks) * pos + i], recv_sem=recv_sems.at[len(ks) * pos + i],
                    device_id=(x, y, 1 - c), device_id_type=_MESH)
                cp.wait_send()
                cp.wait_recv()

    hbm = lambda a: pltpu.HBM(a.shape, a.dtype)
    return pl.pallas_call(
        body, name=name, out_shape=[hbm(a) for a in lands],
        in_specs=[_HBM] * n + [_SEM, _SEM, pl.BlockSpec(memory_space=pl.ANY)], out_specs=[_HBM] * n,
        input_output_aliases={t: t for t in range(n)},
        compiler_params=pltpu.CompilerParams(has_side_effects=_EFFECT),
    )(*lands, sems[0], sems[1], after)


def _copy_own(name, src, land, route, me):
    gather = route.mode == "gather"
    shard = src.shape if gather else land.shape[1:]
    lead = () if gather else (None,)

    if route.kind == "mid":
        grid = (1,)
        at_full = lambda i, me: (0, me[0], 0)
        at_shard = lambda i, me: (0, 0, 0)
        block = tuple(shard)
    else:
        rows, width = shard
        tr = _tile(rows, 512, 16)
        grid = (rows // tr,)
        block = (tr, width)
        if route.kind == "col":
            at_full = lambda i, me: (i, me[0])
        else:
            at_full = lambda i, me: (me[0] * grid[0] + i, 0)
        at_shard = lambda i, me: (i, 0)
    if gather:
        in_map, out_map = at_shard, at_full
    else:
        in_map, out_map = at_full, (lambda i, me: (me[0], *at_shard(i, me)))

    def body(me_ref, src_ref, land_ref, out_ref):
        out_ref[...] = src_ref[...]

    return pl.pallas_call(
        body, name=name, out_shape=jax.ShapeDtypeStruct(land.shape, land.dtype),
        grid_spec=pltpu.PrefetchScalarGridSpec(
            num_scalar_prefetch=1, grid=grid,
            in_specs=[pl.BlockSpec(block, in_map), pl.BlockSpec(memory_space=pl.ANY)],
            out_specs=pl.BlockSpec(lead + block, out_map)),
        input_output_aliases={2: 0},
        compiler_params=_cparams("arbitrary"),
    )(me, src, land)


def _place_own(name, srcs, lands, routes):
    me = _blk(*_my_place()).astype(jnp.int32).reshape(1)
    return [_copy_own(f"{name}_{t}", s, l, r, me) for t, (s, l, r) in enumerate(zip(srcs, lands, routes))]


def _exchange(fulls, kinds, sizes, whole):
    arrays = list(fulls) + list(whole)
    nt, nf = len(arrays), len(fulls)

    def shard_shape(t):
        s = list(arrays[t].shape)
        if t < nf:
            s[{"col": 1, "row": 0, "mid": 1}[kinds[t]]] = sizes[t]
        return tuple(s)

    def body(*refs):
        ins, outs = refs[:nt], refs[nt:2 * nt]
        send_sems, recv_sems, local_sems = refs[2 * nt:]
        x, y, c = _my_place()
        me = _blk(x, y, c)

        def src(t, blk):
            return _slab(ins[t], kinds[t], blk, sizes[t]) if t < nf else ins[t]

        mine = [pltpu.make_async_copy(src(t, me), outs[t].at[me], local_sems.at[t]) for t in range(nt)]
        for cp in mine:
            cp.start()
        sent = []
        for t in range(nt):
            for k in range(1, 8):
                peer = (x ^ (k >> 2), y ^ ((k >> 1) & 1), c ^ (k & 1))
                pb = _blk(*peer)
                cp = pltpu.make_async_remote_copy(
                    src_ref=src(t, pb), dst_ref=outs[t].at[me],
                    send_sem=send_sems.at[7 * t + k - 1], recv_sem=recv_sems.at[7 * t + k - 1],
                    device_id=peer, device_id_type=_MESH)
                cp.start()
                sent.append((cp, t, k, pb))
        for cp, t, k, pb in sent:
            pltpu.make_async_remote_copy(
                src_ref=src(t, pb), dst_ref=outs[t].at[pb],
                send_sem=send_sems.at[7 * t + k - 1], recv_sem=recv_sems.at[7 * t + k - 1],
                device_id=(x, y, c), device_id_type=_MESH).wait_recv()
        for cp, _, _, _ in sent:
            cp.wait_send()
        for cp in mine:
            cp.wait()

    return pl.pallas_call(
        body, name="exchange_grads",
        in_specs=[_HBM] * nt, out_specs=[_HBM] * nt,
        out_shape=[jax.ShapeDtypeStruct((_N_DEV,) + shard_shape(t), arrays[t].dtype) for t in range(nt)],
        scratch_shapes=[pltpu.SemaphoreType.DMA((7 * nt,)), pltpu.SemaphoreType.DMA((7 * nt,)), pltpu.SemaphoreType.DMA((nt,))],
        compiler_params=pltpu.CompilerParams(has_side_effects=True),
    )(*arrays)


def _adamw_update(w_ref, m_ref, v_ref, g, g_ref, d_ref, nm_ref, nv_ref):
    c1 = 1.0 - _ADAM_B1 ** _ADAM_STEP
    c2 = 1.0 - _ADAM_B2 ** _ADAM_STEP
    nm = _ADAM_B1 * m_ref[...] + (1.0 - _ADAM_B1) * g
    nv = _ADAM_B2 * v_ref[...] + (1.0 - _ADAM_B2) * (g * g)
    g_ref[...] = g
    nm_ref[...] = nm
    nv_ref[...] = nv
    d_ref[...] = -_ADAM_LR * ((nm / c1) / (jnp.sqrt(nv / c2) + _ADAM_EPS) + _ADAM_WD * w_ref[...])


def _adamw_small(me, ws, ms, vs, parts):
    n = len(ws)

    def body(me_ref, *refs):
        ins, outs = refs[:4 * n], refs[4 * n:]
        for t in range(n):
            w_ref, m_ref, v_ref, p_ref = ins[4 * t:4 * t + 4]
            r = w_ref.shape[0]
            g = p_ref[0, 0:r, :]
            for s in range(1, _N_DEV):
                g = g + p_ref[s, 0:r, :]
            _adamw_update(w_ref, m_ref, v_ref, g, *outs[4 * t:4 * t + 4])

    whole = lambda a: pl.BlockSpec(a.shape, lambda i, me, nd=a.ndim: (0,) * nd)
    in_specs, operands, out_specs, out_shape = [], [], [], []
    for w, m, v, p in zip(ws, ms, vs, parts):
        c = w.shape[1]
        mine = whole(p) if p.shape[2] == c else pl.BlockSpec((_N_DEV, p.shape[1], c), lambda i, me: (0, 0, me[0]))
        in_specs += [whole(w), whole(m), whole(v), mine]
        operands += [w, m, v, p]
        out_specs += [whole(w)] * 4
        out_shape += [jax.ShapeDtypeStruct(w.shape, f32)] * 4
    return pl.pallas_call(
        body, name="adamw_small", out_shape=out_shape,
        grid_spec=pltpu.PrefetchScalarGridSpec(num_scalar_prefetch=1, grid=(1,), in_specs=in_specs, out_specs=out_specs),
        compiler_params=_cparams("arbitrary"),
    )(me, *operands)


def _adamw(name, w, m, v, parts):
    R, C = w.shape
    n = parts.shape[0]
    tr = _tile(R, 256, 8)

    def body(w_ref, m_ref, v_ref, p_ref, g_ref, d_ref, nm_ref, nv_ref):
        g = p_ref[0].astype(f32)
        for s in range(1, n):
            g = g + p_ref[s].astype(f32)
        _adamw_update(w_ref, m_ref, v_ref, g, g_ref, d_ref, nm_ref, nv_ref)

    blk = pl.BlockSpec((tr, C), lambda i: (i, 0))
    sd = jax.ShapeDtypeStruct((R, C), f32)
    return pl.pallas_call(
        body, name=name, grid=(R // tr,),
        in_specs=[blk, blk, blk, pl.BlockSpec((n, tr, C), lambda i: (0, i, 0))],
        out_specs=[blk, blk, blk, blk], out_shape=[sd, sd, sd, sd],
        compiler_params=_cparams("parallel"),
    )(w, m, v, parts)


def _pad_rows8(a):
    return jnp.pad(a, ((0, 8 - a.shape[0]), (0, 0)))


def _local_step(x, target, p, get, emit, hint):
    T, D = x.shape

    def tie(a, *tokens):
        for tok in tokens:
            if tok is not None:
                a = a + tok[0, 0]
        return a

    n_parts = _N_DEV // _W_IN_PART
    w_in, base = get("w_in", x, 0)
    h1, proj = _in_proj_first(x, p["g_mix"], w_in, base, _W_IN_PART)
    hint("w_in", proj)
    for part in range(1, n_parts):
        w_in, base = get("w_in", proj, part)
        proj = _in_proj_more(f"in_proj_{part}", h1, w_in, proj, base, _W_IN_PART)
    w_pool = get("w_pool", proj)
    PW = w_pool.shape[0] * w_pool.shape[1]
    y_pool = _pool_fwd(proj, w_pool, p["pool_scale"])
    tok = hint("w_pool_proj", y_pool)
    lru_conv_w, w_a, w_i = get("lru_conv_w", proj), get("w_a", proj), get("w_i", proj)
    y_lru, hs = _lru_fwd(proj, D, PW, lru_conv_w, tie(p["lru_conv_b"], tok), w_a, p["b_a"], w_i, p["b_i"], p["lru_lambda"])
    tok = hint("w_up", y_lru)
    w_pp, w_lp = get("w_pool_proj", y_lru), get("w_lru_proj", y_lru)
    pp, qq, merged = _merge_fwd(y_pool, y_lru, w_pp, w_lp, proj, tie(p["b_gate"], tok))
    w_out = get("w_out", y_lru)
    x2, h2 = _out_proj(merged, w_out, x, p["g_mlp"])
    w_up = get("w_up", x2)
    up = _mm_nn("up_proj", h2, w_up, f32, tm_want=2 * _TM)
    tok = hint("w_down", up)
    ffn_conv_w = get("ffn_conv_w", y_lru)
    z = _ffn_fwd(up, ffn_conv_w, tie(p["ffn_conv_b"], tok))
    w_down = get("w_down", z)
    F = w_down.shape[0]
    dx3, dx3b, loss_t, dg_final = _down_loss(z, w_down, x2, target, p["g_final"])

    gs = {"g_final": dg_final}
    tok = emit("w_down", _mm_tn("dw_down", z, dx3b, bf16, tm_want=1536))
    dz = _mm_nt("dz", dx3b, w_down, bf16)
    dup, dcw_ffn, dcb_ffn = _ffn_bwd(dz, up, ffn_conv_w, tie(p["ffn_conv_b"], tok))
    gs["ffn_conv_w"] = dcw_ffn
    gs["ffn_conv_b"] = dcb_ffn

    tm = _tile(T, _TM)
    tk = _tile(F, _TN_MAX)
    nkh = F // tk
    tkt = _tile(T, _TK_T)
    tok = emit("w_up", _matmul(
        "dw_up", "tn", (h2, dup),
        [pl.BlockSpec((tkt, D), lambda i, j, k: (k, 0)), pl.BlockSpec((None, tkt, tk), lambda i, j, k: (j // nkh, k, j % nkh))],
        jax.ShapeDtypeStruct((D, 2 * F), bf16), pl.BlockSpec((D, tk), lambda i, j, k: (0, j)),
        (1, 2 * nkh, T // tkt), (D, tk)))

    tkc = _tile(F, _TK_UP)
    nkc = F // tkc
    dx2, dx2b, gs["g_mlp"] = _norm_bwd_matmul(
        "dh2", dup, pl.BlockSpec((None, tm, tkc), lambda i, k: (k // nkc, i, k % nkc)), 2 * nkc,
        w_up, pl.BlockSpec((D, tkc), lambda i, k: (0, k)), x2, tie(p["g_mlp"], tok), dx3)

    tok = emit("w_out", _mm_tn("dw_out", merged, dx2b, bf16))
    dP, dQ, dl0, dl1, db0, db1 = _merge_bwd(dx2b, w_out, pp, qq, proj, tie(p["b_gate"], tok), PW)
    gs["b_gate"] = jnp.concatenate([db0, db1], axis=1)
    tok = emit("w_pool_proj", _mm_tn("dw_pool_proj", y_pool, dP, bf16))
    tok2 = emit("w_lru_proj", _mm_tn("dw_lru_proj", y_lru, dQ, bf16))
    dy_pool = _mm_nt("dy_pool", dP, w_pp, f32)
    dy_lru = _mm_nt("dy_lru", dQ, w_lp, f32)

    du_lru, du_gelu, dwa, dwi, dcw_lru, dvec = _lru_bwd(
        dy_lru, proj, hs, D, lru_conv_w, tie(p["lru_conv_b"], tok, tok2), w_a, p["b_a"], w_i, p["b_i"], p["lru_lambda"])
    tok = emit("w_a", dwa.astype(bf16))
    tok2 = emit("w_i", dwi.astype(bf16))
    gs["lru_conv_w"] = dcw_lru
    gs["lru_conv_b"], gs["b_a"], gs["b_i"], gs["lru_lambda"] = dvec[0:1], dvec[1:2], dvec[2:3], dvec[3:4]
    du_pool, dwp, gs["pool_scale"] = _pool_bwd(dy_pool, proj, w_pool, tie(p["pool_scale"], tok, tok2))
    tok = emit("w_pool", dwp.astype(bf16))

    dproj = jnp.concatenate([du_pool, du_lru, du_gelu, dl0, dl1], axis=1)
    tok2 = emit("w_in", _mm_tn("dw_in", h1, dproj, bf16))
    NI = dproj.shape[1]
    tki = _tile(NI, _TN_MAX)
    grad_x, _, gs["g_mix"] = _norm_bwd_matmul(
        "dh1", dproj, pl.BlockSpec((tm, tki), lambda i, k: (i, k)), NI // tki,
        w_in, pl.BlockSpec((D, tki), lambda i, k: (0, k)), x, tie(p["g_mix"], tok, tok2), dx2)
    return loss_t[0, 0], grad_x, gs


_MATRICES = {"w_in": "col", "w_pool": "mid", "w_a": "mid", "w_i": "mid", "w_pool_proj": "col", "w_lru_proj": "row",
             "w_out": "row", "w_up": "col", "w_down": "row"}
_CONVS = ("lru_conv_w", "ffn_conv_w")
_GATHER_GROUPS = (("w_pool", "lru_conv_w", "w_a", "w_i"), ("w_pool_proj", "w_lru_proj", "w_out", "ffn_conv_w"),
                  ("w_up",), ("w_down",))
_GROUP_TWO_LEVEL = (False, True, True, True)
_SAME_CORE = (2, 4, 6)
_TWO_LEVEL = (1, 2, 4, 6)
_W_IN_PARTS = ((1,), (4,), (2,), (6,))
_W_IN_PART = 2
_VECTORS = ("g_mix", "b_gate", "pool_scale", "lru_conv_b", "b_a", "b_i", "lru_lambda", "g_mlp", "ffn_conv_b", "g_final")
_WEIGHTS = ("g_mix", "w_in", "b_gate", "w_pool", "pool_scale", "lru_conv_w", "lru_conv_b", "w_a", "b_a", "w_i", "b_i",
            "lru_lambda", "w_pool_proj", "w_lru_proj", "w_out", "g_mlp", "w_up", "ffn_conv_w", "ffn_conv_b", "w_down", "g_final")


def _full_shape(shape, kind):
    s = list(shape)
    s[{"col": 1, "row": 0, "mid": 1}[kind]] *= _N_DEV
    return tuple(s)


def _step(x, target, w, m, v):
    x, target = x[0], target[0]
    me = _blk(*_my_place())

    axis = {"col": 1, "row": 0, "mid": 1}
    kind = dict(_MATRICES, **{n: "col" for n in _CONVS})
    shard = {n: w[n].astype(bf16) for n in _MATRICES}
    shard.update({n: _pad_rows8(w[n]) for n in _CONVS})
    order = ["w_in"] + [n for grp in _GATHER_GROUPS for n in grp]
    index = {n: t for t, n in enumerate(order)}
    n_parts = len(_W_IN_PARTS)
    groups = [([0], ks) for ks in _W_IN_PARTS]
    groups += [([index[n] for n in grp], _TWO_LEVEL if two else _ALL_PEERS) for grp, two in zip(_GATHER_GROUPS, _GROUP_TWO_LEVEL)]
    g_routes = [_Route("gather", kind[n], shard[n].shape[axis[kind[n]]]) for n in order]
    lands = [lax.empty(_full_shape(shard[n].shape, kind[n]), shard[n].dtype) for n in order]
    g_sems, g_srcs, g_lands, _ = _send_start("gather_start", [shard[n] for n in order], lands, g_routes, groups)
    gathered, passing = {}, {}
    x_, y_, _c = _my_place()
    w_in_state = [[g_srcs[0]], [g_lands[0]]]

    def hint(name, after):
        if name == "w_in":
            srcs, got = w_in_state
            tok = None
            for part in range(1, n_parts):
                ks = _W_IN_PARTS[part]
                srcs, got = _send_wait(f"gather_wait_w_in_{part}", srcs, got, g_routes[:1], g_sems[part], after, ks)
                sems, got, tok = _forward_start(f"gather_pass_w_in_{part}", got, g_routes[:1], ks)
                passing[name, part] = sems
            w_in_state[:] = [srcs, got]
            return tok
        gi = next(i for i, grp in enumerate(_GATHER_GROUPS) if name in grp)
        if not _GROUP_TWO_LEVEL[gi] or gi in passing or _GATHER_GROUPS[gi][0] in gathered:
            return None
        ts = groups[n_parts + gi][0]
        routes = [g_routes[t] for t in ts]
        srcs, got = _send_wait(f"gather_wait_{gi}", [g_srcs[t] for t in ts], [g_lands[t] for t in ts], routes,
                               g_sems[n_parts + gi], after, _TWO_LEVEL)
        got = _place_own(f"gather_own_{gi}", srcs, got, routes)
        sems, got, tok = _forward_start(f"gather_pass_{gi}", got, routes, _SAME_CORE)
        passing[gi] = (sems, got, routes)
        return tok

    def get(name, after, part=None):
        if name == "w_in":
            ks = _W_IN_PARTS[part]
            if part == 0:
                srcs, got = _send_wait("gather_wait_w_in_0", *w_in_state, g_routes[:1], g_sems[0], after, ks)
                got = _place_own("gather_own_w_in", srcs, got, g_routes[:1])
                w_in_state[:] = [srcs, got]
            else:
                if (name, part) not in passing:
                    hint(name, after)
                got = _forward_wait(f"gather_got_w_in_{part}", w_in_state[1], g_routes[:1], passing[name, part], after, ks)
                w_in_state[1] = got
            k = ks[-1]
            base = 4 * (x_ ^ (k >> 2)) + 2 * (y_ ^ ((k >> 1) & 1))
            return got[0], base.astype(jnp.int32).reshape(1)
        if name not in gathered:
            gi = next(i for i, grp in enumerate(_GATHER_GROUPS) if name in grp)
            if _GROUP_TWO_LEVEL[gi]:
                hint(name, after)
                sems, got, routes = passing[gi]
                full = _forward_wait(f"gather_got_{gi}", got, routes, sems, after, _SAME_CORE)
            else:
                ts = groups[n_parts + gi][0]
                routes = [g_routes[t] for t in ts]
                srcs, got = _send_wait(f"gather_wait_{gi}", [g_srcs[t] for t in ts], [g_lands[t] for t in ts], routes,
                                       g_sems[n_parts + gi], after)
                full = _place_own(f"gather_own_{gi}", srcs, got, routes)
            gathered.update(zip(_GATHER_GROUPS[gi], full))
        return gathered[name]

    sent = {}

    def emit(name, grad):
        k = _MATRICES[name]
        size = w[name].shape[axis[k]]
        route = _Route("scatter", k, size)
        shp = list(grad.shape)
        shp[axis[k]] = size
        land = lax.empty((_N_DEV, *shp), grad.dtype)
        sems, srcs, lnds, token = _send_start("grad_start_" + name, [grad], [land], [route], [([0], _ALL_PEERS)])
        sent[name] = (srcs, lnds, [route], sems[0])
        return token

    p = {n: w[n].reshape(1, -1) for n in _VECTORS}
    loss_t, grad_x, gs = _local_step(x, target, p, get, emit, hint)
    loss = lax.psum(loss_t, ("x", "y", "c"))

    small_names = list(_VECTORS) + list(_CONVS)
    small_parts = _exchange([], [], [], [gs[n] for n in small_names])

    out = {}
    mats = list(_MATRICES)
    for n in mats:
        srcs, lnds, routes, sems = sent[n]
        srcs, got = _send_wait("grad_wait_" + n, srcs, lnds, routes, sems, grad_x)
        parts = _place_own("grad_own_" + n, srcs, got, routes)[0]
        shp = w[n].shape
        r2 = (-1, shp[-1])
        res = _adamw("adamw_" + n, w[n].reshape(r2), m[n].reshape(r2), v[n].reshape(r2),
                     parts.reshape((_N_DEV,) + w[n].reshape(r2).shape))
        out[n] = [a.reshape(shp) for a in res]
    two_d = lambda a: a.reshape(-1, a.shape[-1])
    res = _adamw_small(me.astype(jnp.int32).reshape(1), [two_d(w[n]) for n in small_names], [two_d(m[n]) for n in small_names],
                       [two_d(v[n]) for n in small_names], small_parts)
    for t, n in enumerate(small_names):
        out[n] = [a.reshape(w[n].shape) for a in res[4 * t:4 * t + 4]]
    return loss, grad_x[None], out


def kernel(x, g_mix, w_in, b_gate, w_pool, pool_scale, lru_conv_w, lru_conv_b, w_a, b_a, w_i, b_i, lru_lambda, w_pool_proj, w_lru_proj, w_out, g_mlp, w_up, ffn_conv_w, ffn_conv_b, w_down, g_final, loss_target, m_g_mix, m_w_in, m_b_gate, m_w_pool, m_pool_scale, m_lru_conv_w, m_lru_conv_b, m_w_a, m_b_a, m_w_i, m_b_i, m_lru_lambda, m_w_pool_proj, m_w_lru_proj, m_w_out, m_g_mlp, m_w_up, m_ffn_conv_w, m_ffn_conv_b, m_w_down, m_g_final, v_g_mix, v_w_in, v_b_gate, v_w_pool, v_pool_scale, v_lru_conv_w, v_lru_conv_b, v_w_a, v_b_a, v_w_i, v_b_i, v_lru_lambda, v_w_pool_proj, v_w_lru_proj, v_w_out, v_g_mlp, v_w_up, v_ffn_conv_w, v_ffn_conv_b, v_w_down, v_g_final):
    given = dict(locals())
    orig = {n: given[n].shape for n in _WEIGHTS}

    def squeeze(a, n):
        return a if n == "g_final" else a[0]

    w = {n: squeeze(given[n], n) for n in _WEIGHTS}
    m = {n: squeeze(given["m_" + n], n) for n in _WEIGHTS}
    v = {n: squeeze(given["v_" + n], n) for n in _WEIGHTS}
    for d in (w, m, v):
        d["g_final"] = d["g_final"].reshape(1, -1)
    loss, grad_x, out = _step(x, loss_target, w, m, v)
    res = [loss, grad_x]
    for k in range(4):
        res += [out[n][k].reshape(orig[n]) for n in _WEIGHTS]
    return tuple(res)
```

```python
import functools

import jax
import jax.numpy as jnp
from jax import lax
from jax.experimental import pallas as pl
from jax.experimental.pallas import tpu as pltpu

f32 = jnp.float32
bf16 = jnp.bfloat16

_EPS = 1e-6
_LRU_C = 8.0
_POOL_WINDOWS = (2, 4, 8, 16)
_POOL_HALO = 16
_CONV_HALO = 8
_GELU_C0 = 0.7978845608028654
_GELU_C1 = 0.044715
_ADAM_LR, _ADAM_B1, _ADAM_B2, _ADAM_EPS, _ADAM_WD, _ADAM_STEP = 0.001, 0.9, 0.999, 1e-08, 0.01, 10
_N_DEV = 8
_LANE = 128
_VMEM_LIMIT = 60 * 1024 * 1024

_TM = 512
_TM_SMALL = 256
_TC = 256
_TK_T = 1024
_TN_MAX = 1536
_CW = 1024
_TK_DOWN = 1536
_TK_UP = 2048


def _cparams(*sem):
    return pltpu.CompilerParams(dimension_semantics=tuple(sem), vmem_limit_bytes=_VMEM_LIMIT)


def _tile(n, want, mult=1):
    if n <= want:
        return n
    t = want - want % mult
    while n % t:
        t -= mult
    return t


def _gelu(x):
    t = jnp.tanh(x * (_GELU_C0 + (_GELU_C0 * _GELU_C1) * (x * x)))
    return x * (0.5 + 0.5 * t)


def _gelu_both(x):
    x2 = x * x
    t = jnp.tanh(x * (_GELU_C0 + (_GELU_C0 * _GELU_C1) * x2))
    h = 0.5 + 0.5 * t
    return x * h, h + (x * (1.0 - t * t)) * (0.5 * _GELU_C0 + (1.5 * _GELU_C0 * _GELU_C1) * x2)


def _fold8(x):
    out = x[0:8]
    for r in range(8, x.shape[0], 8):
        out = out + x[r:r + 8]
    return out


def _sigmoid(x):
    return jax.nn.sigmoid(x)


def _dot(a, b):
    return jnp.dot(a, b, preferred_element_type=f32)


def _dot_nt(a, b):
    return lax.dot_general(a, b, (((1,), (1,)), ((), ())), preferred_element_type=f32)


def _dot_tn(a, b):
    return lax.dot_general(a, b, (((0,), (0,)), ((), ())), preferred_element_type=f32)


def _rms_rows(x_ref, g_ref, h_ref, n_rows, n_cols):
    rr = _tile(n_rows, _SLAB_ROWS)
    cg = _tile(n_cols, _SLAB_COLS)
    for r0 in range(0, n_rows, rr):
        rows = pl.ds(r0, rr)
        x = x_ref[rows, :]
        r = lax.rsqrt(jnp.mean(x * x, axis=-1, keepdims=True) + _EPS)
        for c0 in range(0, n_cols, cg):
            cols = slice(c0, c0 + cg)
            h_ref[rows, cols] = (x_ref[rows, cols] * r * g_ref[:, cols]).astype(bf16)


def _matmul(name, mode, operands, in_specs, out_shape, out_spec, grid, acc_shape):
    dot = {"nn": _dot, "nt": _dot_nt, "tn": _dot_tn}[mode]
    nk = grid[2]

    def body_whole(a_ref, b_ref, o_ref):
        o_ref[...] = dot(a_ref[...], b_ref[...]).astype(o_ref.dtype)

    def body(a_ref, b_ref, o_ref, acc_ref):
        k = pl.program_id(2)

        @pl.when(k == 0)
        def _():
            acc_ref[...] = dot(a_ref[...], b_ref[...])

        @pl.when((k > 0) & (k < nk - 1))
        def _():
            acc_ref[...] += dot(a_ref[...], b_ref[...])

        @pl.when(k == nk - 1)
        def _():
            o_ref[...] = (acc_ref[...] + dot(a_ref[...], b_ref[...])).astype(o_ref.dtype)

    return pl.pallas_call(
        body_whole if nk == 1 else body, name=name, grid=grid, in_specs=in_specs, out_specs=out_spec, out_shape=out_shape,
        scratch_shapes=[] if nk == 1 else [pltpu.VMEM(acc_shape, f32)],
        compiler_params=_cparams("parallel", "parallel", "arbitrary"),
    )(*operands)


def _mm_nn(name, a, b, out_dtype, tm_want=None):
    M, K = a.shape
    N = b.shape[1]
    tm, tn = _tile(M, tm_want or _TM), _tile(N, _TN_MAX)
    return _matmul(
        name, "nn", (a, b),
        [pl.BlockSpec((tm, K), lambda i, j, k: (i, 0)), pl.BlockSpec((K, tn), lambda i, j, k: (0, j))],
        jax.ShapeDtypeStruct((M, N), out_dtype), pl.BlockSpec((tm, tn), lambda i, j, k: (i, j)),
        (M // tm, N // tn, 1), (tm, tn))


def _mm_nt(name, a, b, out_dtype):
    M, K = a.shape
    N = b.shape[0]
    tm, tn = _tile(M, _TM), _tile(N, _TN_MAX)
    return _matmul(
        name, "nt", (a, b),
        [pl.BlockSpec((tm, K), lambda i, j, k: (i, 0)), pl.BlockSpec((tn, K), lambda i, j, k: (j, 0))],
        jax.ShapeDtypeStruct((M, N), out_dtype), pl.BlockSpec((tm, tn), lambda i, j, k: (i, j)),
        (M // tm, N // tn, 1), (tm, tn))


def _mm_tn(name, a, b, out_dtype, tm_want=2048):
    T, M = a.shape
    N = b.shape[1]
    tm, tn, tk = _tile(M, tm_want), _tile(N, _TN_MAX), _tile(T, _TK_T)
    return _matmul(
        name, "tn", (a, b),
        [pl.BlockSpec((tk, tm), lambda i, j, k: (k, i)), pl.BlockSpec((tk, tn), lambda i, j, k: (k, j))],
        jax.ShapeDtypeStruct((M, N), out_dtype), pl.BlockSpec((tm, tn), lambda i, j, k: (i, j)),
        (M // tm, N // tn, T // tk), (tm, tn))


def _in_proj_first(x, g_mix, w_in, base, n_tiles):
    T, D = x.shape
    NI = w_in.shape[1]
    tm, tn = _tile(T, 2 * _TM), NI // _N_DEV

    def body(base_ref, x_ref, g_ref, w_ref, h_ref, o_ref):
        @pl.when(pl.program_id(1) == 0)
        def _():
            _rms_rows(x_ref, g_ref, h_ref, tm, D)

        o_ref[...] = _dot(h_ref[...], w_ref[...])

    return pl.pallas_call(
        body, name="in_proj_0",
        grid_spec=pltpu.PrefetchScalarGridSpec(
            num_scalar_prefetch=1, grid=(T // tm, n_tiles),
            in_specs=[pl.BlockSpec((tm, D), lambda i, j, b: (i, 0)), pl.BlockSpec((1, D), lambda i, j, b: (0, 0)),
                      pl.BlockSpec((D, tn), lambda i, j, b: (0, b[0] + j))],
            out_specs=[pl.BlockSpec((tm, D), lambda i, j, b: (i, 0)), pl.BlockSpec((tm, tn), lambda i, j, b: (i, b[0] + j))]),
        out_shape=[jax.ShapeDtypeStruct((T, D), bf16), jax.ShapeDtypeStruct((T, NI), f32)],
        compiler_params=_cparams("parallel", "arbitrary"),
    )(base, x, g_mix, w_in)


def _in_proj_more(name, h1, w_in, proj, base, n_tiles):
    T, D = h1.shape
    NI = w_in.shape[1]
    tm, tn = _tile(T, 2 * _TM), NI // _N_DEV

    def body(base_ref, h_ref, w_ref, proj_ref, o_ref):
        o_ref[...] = _dot(h_ref[...], w_ref[...])

    return pl.pallas_call(
        body, name=name,
        grid_spec=pltpu.PrefetchScalarGridSpec(
            num_scalar_prefetch=1, grid=(T // tm, n_tiles),
            in_specs=[pl.BlockSpec((tm, D), lambda i, j, b: (i, 0)), pl.BlockSpec((D, tn), lambda i, j, b: (0, b[0] + j)),
                      pl.BlockSpec(memory_space=pl.ANY)],
            out_specs=pl.BlockSpec((tm, tn), lambda i, j, b: (i, b[0] + j))),
        out_shape=jax.ShapeDtypeStruct((T, NI), f32),
        input_output_aliases={3: 0},
        compiler_params=_cparams("parallel", "arbitrary"),
    )(base, h1, w_in, proj)


def _window_means(ext_ref, row0, tc, gw):
    H = _POOL_HALO
    t_glob = row0 + lax.broadcasted_iota(jnp.int32, (tc, 1), 0)
    out = []
    for g, w in enumerate(_POOL_WINDOWS):
        s = ext_ref[:, g * gw:(g + 1) * gw]
        st = 1
        while st < w:
            s = s + pltpu.roll(s, st, 0)
            st *= 2
        cnt = jnp.minimum(t_glob + 1, w).astype(f32)
        out.append((s[H:, :] / cnt, ext_ref[pl.ds(H, tc), g * gw:(g + 1) * gw]))
    return out


def _pool_fwd(proj, w_pool, pool_scale):
    T = proj.shape[0]
    G, gw, _ = w_pool.shape
    PW = G * gw
    tc = _tile(T, _TC)
    H = _POOL_HALO

    def body(u_ref, w_ref, s_ref, y_ref, ext_ref):
        i = pl.program_id(0)

        @pl.when(i == 0)
        def _():
            ext_ref[pl.ds(0, H), :] = jnp.zeros((H, PW), f32)

        ext_ref[pl.ds(H, tc), :] = u_ref[...]
        for g, (m, u) in enumerate(_window_means(ext_ref, i * tc, tc, gw)):
            d = (m - u).astype(bf16)
            y = _dot(d, w_ref[g]) * s_ref[:, g * gw:(g + 1) * gw]
            y_ref[:, g * gw:(g + 1) * gw] = y.astype(bf16)
        ext_ref[pl.ds(0, H), :] = ext_ref[pl.ds(tc, H), :]

    return pl.pallas_call(
        body, name="pool_fwd", grid=(T // tc,),
        in_specs=[pl.BlockSpec((tc, PW), lambda i: (i, 0)), pl.BlockSpec((G, gw, gw), lambda i: (0, 0, 0)),
                  pl.BlockSpec((1, PW), lambda i: (0, 0))],
        out_specs=pl.BlockSpec((tc, PW), lambda i: (i, 0)),
        out_shape=jax.ShapeDtypeStruct((T, PW), bf16),
        scratch_shapes=[pltpu.VMEM((H + tc, PW), f32)],
        compiler_params=_cparams("arbitrary"),
    )(proj, w_pool, pool_scale)


def _softplus(z):
    return jnp.maximum(z, 0.0) + jnp.log1p(jnp.exp(-jnp.abs(z)))


def _causal_conv(ext_ref, cw_ref, cb_ref, n, K, cols=slice(None), r0=0):
    H = _CONV_HALO
    v = cb_ref[:, cols] + cw_ref[K - 1:K, cols] * ext_ref[pl.ds(H + r0, n), cols]
    for k in range(K - 1):
        v = v + cw_ref[k:k + 1, cols] * ext_ref[pl.ds(H + r0 - (K - 1 - k), n), cols]
    return v


_ROWS = 8
_SLAB_ROWS = 16
_SLAB_COLS = 512


def _slabs(n_rows, n_cols, reverse=False):
    cg = _tile(n_cols, _SLAB_COLS)
    rr = _tile(n_rows, _SLAB_ROWS)
    starts = range(0, n_rows, rr)
    for c0 in range(0, n_cols, cg):
        for r0 in (reversed(starts) if reverse else starts):
            yield slice(c0, c0 + cg), pl.ds(r0, rr)


def _scan_rows(a_ref, b_ref, out_ref, carry_ref, n_rows, n_cols, reverse=False):
    cg = _tile(n_cols, _SLAB_COLS)
    row = lax.broadcasted_iota(jnp.int32, (8, cg), 0)
    steps = [(8 - sh, row < 8 - sh) if reverse else (sh, row >= sh) for sh in (1, 2, 4)]
    tiles = list(range(0, n_rows, 8))
    for c0 in range(0, n_cols, cg):
        cols = slice(c0, c0 + cg)
        h_in = carry_ref[0:1, cols]
        for r0 in (reversed(tiles) if reverse else tiles):
            rows = pl.ds(r0, 8)
            a, b = a_ref[rows, cols], b_ref[rows, cols]
            for shift, inside in steps:
                b = jnp.where(inside, a * pltpu.roll(b, shift, 0) + b, b)
                a = jnp.where(inside, a * pltpu.roll(a, shift, 0), a)
            h = a * h_in + b
            out_ref[rows, cols] = h
            h_in = h[0:1, :] if reverse else h[7:8, :]
        carry_ref[0:1, cols] = h_in


def _slabs_with_sums(n_rows, n_cols, n_sums, visit, flush, reverse=False):
    cg = _tile(n_cols, _SLAB_COLS)
    rr = _tile(n_rows, _SLAB_ROWS)
    starts = list(range(0, n_rows, rr))
    for c0 in range(0, n_cols, cg):
        cols = slice(c0, c0 + cg)
        sums = [jnp.zeros((8, cg), f32) for _ in range(n_sums)]
        for r0 in (reversed(starts) if reverse else starts):
            sums = visit(cols, pl.ds(r0, rr), sums)
        flush(cols, [jnp.sum(s, axis=0, keepdims=True) for s in sums])


def _lru_fwd(proj, D, PW, conv_w, conv_b, w_a, b_a, w_i, b_i, lam):
    T = proj.shape[0]
    NB, bw, _ = w_a.shape
    K = 4
    tc = _tile(T, _TC)
    H = _CONV_HALO
    hb = D // 2
    assert PW == hb and conv_w.shape[0] == 8

    def body(u0_ref, u1_ref, g0_ref, g1_ref, cw_ref, cb_ref, wa_ref, ba_ref, wi_ref, bi_ref, lam_ref,
             y_ref, hs_ref, ext_ref, a_scr, b_scr, v_scr, hc_scr):
        c = pl.program_id(0)

        @pl.when(c == 0)
        def _():
            ext_ref[pl.ds(0, H), :] = jnp.zeros((H, D), f32)
            hc_scr[...] = jnp.zeros_like(hc_scr)

        ext_ref[pl.ds(H, tc), 0:hb] = u0_ref[...]
        ext_ref[pl.ds(H, tc), hb:D] = u1_ref[...]
        sp = _softplus(-lam_ref[...])
        for cols, rows in _slabs(tc, D):
            v_scr[rows, cols] = _causal_conv(ext_ref, cw_ref, cb_ref, rows.size, K, cols, rows.start)
        for b in range(NB):
            cols = slice(b * bw, (b + 1) * bw)
            vb = v_scr[:, cols].astype(bf16)
            a_scr[:, cols] = _dot(vb, wa_ref[b])
            b_scr[:, cols] = _dot(vb, wi_ref[b])
        for cols, rows in _slabs(tc, D):
            r = _sigmoid(a_scr[rows, cols] + ba_ref[:, cols])
            i = _sigmoid(b_scr[rows, cols] + bi_ref[:, cols])
            a = jnp.exp(-_LRU_C * r * sp[:, cols])
            a_scr[rows, cols] = a
            b_scr[rows, cols] = jnp.sqrt(1.0 - a * a) * (i * v_scr[rows, cols])

        _scan_rows(a_scr, b_scr, hs_ref, hc_scr, tc, D)
        for cols, rows in _slabs(tc, D):
            g_ref, gcols = (g0_ref, cols) if cols.start < hb else (g1_ref, slice(cols.start - hb, cols.stop - hb))
            y_ref[rows, cols] = (hs_ref[rows, cols] * _gelu(g_ref[rows, gcols])).astype(bf16)
        ext_ref[pl.ds(0, H), :] = ext_ref[pl.ds(tc, H), :]

    vec = pl.BlockSpec((1, D), lambda c: (0, 0))
    wspec = pl.BlockSpec((NB, bw, bw), lambda c: (0, 0, 0))
    return pl.pallas_call(
        body, name="lru_fwd", grid=(T // tc,),
        in_specs=[pl.BlockSpec((tc, hb), lambda c: (c, 1)), pl.BlockSpec((tc, hb), lambda c: (c, 2)),
                  pl.BlockSpec((tc, hb), lambda c: (c, 3)), pl.BlockSpec((tc, hb), lambda c: (c, 4)),
                  pl.BlockSpec((8, D), lambda c: (0, 0)), vec, wspec, vec, wspec, vec, vec],
        out_specs=[pl.BlockSpec((tc, D), lambda c: (c, 0)), pl.BlockSpec((tc, D), lambda c: (c, 0))],
        out_shape=[jax.ShapeDtypeStruct((T, D), bf16), jax.ShapeDtypeStruct((T, D), f32)],
        scratch_shapes=[pltpu.VMEM((H + tc, D), f32), pltpu.VMEM((tc, D), f32), pltpu.VMEM((tc, D), f32),
                        pltpu.VMEM((tc, D), f32), pltpu.VMEM((8, D), f32)],
        compiler_params=_cparams("arbitrary"),
    )(proj, proj, proj, proj, conv_w, conv_b, w_a, b_a, w_i, b_i, lam)


def _merge_fwd(y_pool, y_lru, w_pp, w_lp, proj, b_gate):
    T, PW = y_pool.shape
    D = y_lru.shape[1]
    tm, tn = _tile(T, _TM), _tile(D, PW)
    nj = D // tn
    off = (PW + 2 * D) // tn

    def body(yp_ref, yl_ref, wp_ref, wl_ref, l0_ref, l1_ref, b0_ref, b1_ref, p_ref, q_ref, m_ref, p_scr, q_scr):
        p_scr[...] = _dot(yp_ref[...], wp_ref[...])
        q_scr[...] = _dot(yl_ref[...], wl_ref[...])
        for cols, rows in _slabs(tm, tn):
            p, q = p_scr[rows, cols], q_scr[rows, cols]
            g0 = _sigmoid(l0_ref[rows, cols] + b0_ref[:, cols])
            g1 = _sigmoid(l1_ref[rows, cols] + b1_ref[:, cols])
            m_ref[rows, cols] = (g0 * p + g1 * q).astype(bf16)
            p_ref[rows, cols] = p.astype(bf16)
            q_ref[rows, cols] = q.astype(bf16)

    tile = pl.BlockSpec((tm, tn), lambda j, i: (i, j))
    return pl.pallas_call(
        body, name="merge_fwd", grid=(nj, T // tm),
        in_specs=[pl.BlockSpec((tm, PW), lambda j, i: (i, 0)), pl.BlockSpec((tm, D), lambda j, i: (i, 0)),
                  pl.BlockSpec((PW, tn), lambda j, i: (0, j)), pl.BlockSpec((D, tn), lambda j, i: (0, j)),
                  pl.BlockSpec((tm, tn), lambda j, i: (i, off + j)), pl.BlockSpec((tm, tn), lambda j, i: (i, off + nj + j)),
                  pl.BlockSpec((1, tn), lambda j, i: (0, j)), pl.BlockSpec((1, tn), lambda j, i: (0, nj + j))],
        out_specs=[tile, tile, tile],
        out_shape=[jax.ShapeDtypeStruct((T, D), bf16), jax.ShapeDtypeStruct((T, D), bf16), jax.ShapeDtypeStruct((T, D), bf16)],
        scratch_shapes=[pltpu.VMEM((tm, tn), f32), pltpu.VMEM((tm, tn), f32)],
        compiler_params=_cparams("parallel", "arbitrary"),
    )(y_pool, y_lru, w_pp, w_lp, proj, proj, b_gate, b_gate)


def _out_proj(merged, w_out, x, g_mlp):
    T, D = x.shape
    tm = _tile(T, _TM_SMALL)

    def body(m_ref, w_ref, x_ref, g_ref, x2_ref, h2_ref):
        x2_ref[...] = x_ref[...] + _dot(m_ref[...], w_ref[...])
        _rms_rows(x2_ref, g_ref, h2_ref, tm, D)

    row = pl.BlockSpec((tm, D), lambda i: (i, 0))
    return pl.pallas_call(
        body, name="out_proj", grid=(T // tm,),
        in_specs=[row, pl.BlockSpec((D, D), lambda i: (0, 0)), row, pl.BlockSpec((1, D), lambda i: (0, 0))],
        out_specs=[row, row],
        out_shape=[jax.ShapeDtypeStruct((T, D), f32), jax.ShapeDtypeStruct((T, D), bf16)],
        compiler_params=_cparams("parallel"),
    )(merged, w_out, x, g_mlp)


def _ffn_fwd(up, conv_w, conv_b):
    T, F2 = up.shape
    F = F2 // 2
    K = 3
    tc = _tile(T, 2 * _TC)
    cw = _tile(F, _CW)
    ns = F // cw
    H = _CONV_HALO

    def body(gp_ref, val_ref, cw_ref, cb_ref, z_ref, ext_ref):
        @pl.when(pl.program_id(1) == 0)
        def _():
            ext_ref[pl.ds(0, H), :] = jnp.zeros((H, cw), f32)

        ext_ref[pl.ds(H, tc), :] = gp_ref[...]
        for cols, rows in _slabs(tc, cw):
            c = _causal_conv(ext_ref, cw_ref, cb_ref, rows.size, K, cols, rows.start)
            z_ref[rows, cols] = (_gelu(c) * val_ref[rows, cols]).astype(bf16)
        ext_ref[pl.ds(0, H), :] = ext_ref[pl.ds(tc, H), :]

    return pl.pallas_call(
        body, name="ffn_fwd", grid=(ns, T // tc),
        in_specs=[pl.BlockSpec((tc, cw), lambda s, c: (c, s)), pl.BlockSpec((tc, cw), lambda s, c: (c, ns + s)),
                  pl.BlockSpec((8, cw), lambda s, c: (0, s)), pl.BlockSpec((1, cw), lambda s, c: (0, s))],
        out_specs=pl.BlockSpec((tc, cw), lambda s, c: (c, s)),
        out_shape=jax.ShapeDtypeStruct((T, F), bf16),
        scratch_shapes=[pltpu.VMEM((H + tc, cw), f32)],
        compiler_params=_cparams("parallel", "arbitrary"),
    )(up, up, conv_w, conv_b)


def _down_loss(z, w_down, x2, target, g_final):
    T, F = z.shape
    D = x2.shape[1]
    tm, tk = _tile(T, _TM), _tile(F, _TK_DOWN)
    nk = F // tk

    def body(z_ref, w_ref, x2_ref, t_ref, g_ref, dx_ref, dxb_ref, loss_ref, dg_ref, acc_ref):
        i, k = pl.program_id(0), pl.program_id(1)

        @pl.when(k == 0)
        def _():
            acc_ref[...] = x2_ref[...]

        @pl.when((i == 0) & (k == 0))
        def _():
            loss_ref[...] = jnp.zeros_like(loss_ref)
            dg_ref[...] = jnp.zeros_like(dg_ref)

        acc_ref[...] += _dot(z_ref[...], w_ref[...])

        @pl.when(k == nk - 1)
        def _():
            g = g_ref[...]
            sq = jnp.zeros((_ROWS, 1), f32)
            dgs = jnp.zeros((_ROWS, D), f32)
            for r0 in range(0, tm, _ROWS):
                rows = pl.ds(r0, _ROWS)
                x3 = acc_ref[rows, :]
                r = lax.rsqrt(jnp.mean(x3 * x3, axis=-1, keepdims=True) + _EPS)
                xr = x3 * r
                e = xr * g - t_ref[rows, :]
                sq = sq + jnp.sum(e * e, axis=-1, keepdims=True)
                dy = e * (1.0 / D)
                gy = dy * g
                dx = r * gy - x3 * ((r * r * r) * jnp.mean(x3 * gy, axis=-1, keepdims=True))
                dgs = dgs + dy * xr
                dx_ref[rows, :] = dx
                dxb_ref[rows, :] = dx.astype(bf16)
            loss_ref[...] += (0.5 / D) * jnp.sum(sq)
            dg_ref[...] += jnp.sum(dgs, axis=0, keepdims=True)

    row = pl.BlockSpec((tm, D), lambda i, k: (i, 0))
    vec = pl.BlockSpec((1, D), lambda i, k: (0, 0))
    return pl.pallas_call(
        body, name="down_loss", grid=(T // tm, nk),
        in_specs=[pl.BlockSpec((tm, tk), lambda i, k: (i, k)), pl.BlockSpec((tk, D), lambda i, k: (k, 0)), row, row, vec],
        out_specs=[row, row, pl.BlockSpec((8, _LANE), lambda i, k: (0, 0)), vec],
        out_shape=[jax.ShapeDtypeStruct((T, D), f32), jax.ShapeDtypeStruct((T, D), bf16),
                   jax.ShapeDtypeStruct((8, _LANE), f32), jax.ShapeDtypeStruct((1, D), f32)],
        scratch_shapes=[pltpu.VMEM((tm, D), f32)],
        compiler_params=_cparams("arbitrary", "arbitrary"),
    )(z, w_down, x2, target, g_final)


def _ffn_bwd(dz, up, conv_w, conv_b):
    T, F = dz.shape
    K = 3
    tc = _tile(T, 2 * _TC)
    cw = _tile(F, _CW)
    ns, nt = F // cw, T // tc
    H = _CONV_HALO

    def body(dz_ref, gp_ref, val_ref, gph_ref, cw_ref, cb_ref, dup_ref, dcw_ref, dcb_ref, ext_ref, dext_ref):
        j = pl.program_id(1)
        first = j == nt - 1

        @pl.when(j == 0)
        def _():
            dext_ref[pl.ds(tc, H), :] = jnp.zeros((H, cw), f32)
            dcw_ref[...] = jnp.zeros_like(dcw_ref)
            dcb_ref[...] = jnp.zeros_like(dcb_ref)

        ext_ref[pl.ds(0, H), :] = jnp.where(first, 0.0, gph_ref[...])
        ext_ref[pl.ds(H, tc), :] = gp_ref[...]
        def visit(cols, rows, sums):
            r0, n = rows.start, rows.size
            gp = ext_ref[pl.ds(H + r0, n), cols]
            c = _causal_conv(ext_ref, cw_ref, cb_ref, n, K, cols, r0)
            ge, gg = _gelu_both(c)
            dzv = dz_ref[rows, cols].astype(f32)
            dup_ref[1, rows, cols] = (dzv * ge).astype(bf16)
            dc = dzv * val_ref[rows, cols] * gg
            dext_ref[rows, cols] = dc
            dgp = cw_ref[K - 1:K, cols] * dc
            new = [None] * K + [sums[K] + _fold8(dc)]
            new[K - 1] = sums[K - 1] + _fold8(gp * dc)
            for k in range(K - 1):
                sh = dext_ref[pl.ds(r0 + K - 1 - k, n), cols]
                dgp = dgp + cw_ref[k:k + 1, cols] * sh
                new[k] = sums[k] + _fold8(gp * sh)
            dup_ref[0, rows, cols] = dgp.astype(bf16)
            return new

        def flush(cols, totals):
            for k in range(K):
                dcw_ref[k:k + 1, cols] += totals[k]
            dcb_ref[:, cols] += totals[K]

        _slabs_with_sums(tc, cw, K + 1, visit, flush, reverse=True)
        dext_ref[pl.ds(tc, H), :] = dext_ref[pl.ds(0, H), :]

    hblk = tc // H
    return pl.pallas_call(
        body, name="ffn_bwd", grid=(ns, nt),
        in_specs=[pl.BlockSpec((tc, cw), lambda s, j: (nt - 1 - j, s)),
                  pl.BlockSpec((tc, cw), lambda s, j: (nt - 1 - j, s)),
                  pl.BlockSpec((tc, cw), lambda s, j: (nt - 1 - j, ns + s)),
                  pl.BlockSpec((H, cw), lambda s, j: (jnp.maximum((nt - 1 - j) * hblk - 1, 0), s)),
                  pl.BlockSpec((8, cw), lambda s, j: (0, s)), pl.BlockSpec((1, cw), lambda s, j: (0, s))],
        out_specs=[pl.BlockSpec((2, tc, cw), lambda s, j: (0, nt - 1 - j, s)),
                   pl.BlockSpec((8, cw), lambda s, j: (0, s)), pl.BlockSpec((1, cw), lambda s, j: (0, s))],
        out_shape=[jax.ShapeDtypeStruct((2, T, F), bf16), jax.ShapeDtypeStruct((8, F), f32), jax.ShapeDtypeStruct((1, F), f32)],
        scratch_shapes=[pltpu.VMEM((H + tc, cw), f32), pltpu.VMEM((tc + H, cw), f32)],
        compiler_params=_cparams("parallel", "arbitrary"),
    )(dz, up, up, up, conv_w, conv_b)


def _norm_bwd_matmul(name, a, a_spec, nk, w, w_spec, x, g, dres):
    T, D = x.shape
    tm = a_spec.block_shape[-2]

    def body(a_ref, w_ref, x_ref, g_ref, dr_ref, dx_ref, dxb_ref, dg_ref, acc_ref):
        i, k = pl.program_id(0), pl.program_id(1)

        @pl.when((i == 0) & (k == 0))
        def _():
            dg_ref[...] = jnp.zeros_like(dg_ref)

        @pl.when(k == 0)
        def _():
            acc_ref[...] = _dot_nt(a_ref[...], w_ref[...])

        @pl.when(k > 0)
        def _():
            acc_ref[...] += _dot_nt(a_ref[...], w_ref[...])

        @pl.when(k == nk - 1)
        def _():
            g = g_ref[...]
            dgs = jnp.zeros((_ROWS, D), f32)
            for r0 in range(0, tm, _ROWS):
                rows = pl.ds(r0, _ROWS)
                x = x_ref[rows, :]
                dh = acc_ref[rows, :]
                r = lax.rsqrt(jnp.mean(x * x, axis=-1, keepdims=True) + _EPS)
                gy = dh * g
                dx = dr_ref[rows, :] + (r * gy - x * ((r * r * r) * jnp.mean(x * gy, axis=-1, keepdims=True)))
                dgs = dgs + dh * (x * r)
                dx_ref[rows, :] = dx
                dxb_ref[rows, :] = dx.astype(bf16)
            dg_ref[...] += jnp.sum(dgs, axis=0, keepdims=True)

    row = pl.BlockSpec((tm, D), lambda i, k: (i, 0))
    vec = pl.BlockSpec((1, D), lambda i, k: (0, 0))
    return pl.pallas_call(
        body, name=name, grid=(T // tm, nk),
        in_specs=[a_spec, w_spec, row, vec, row],
        out_specs=[row, row, vec],
        out_shape=[jax.ShapeDtypeStruct((T, D), f32), jax.ShapeDtypeStruct((T, D), bf16), jax.ShapeDtypeStruct((1, D), f32)],
        scratch_shapes=[pltpu.VMEM((tm, D), f32)],
        compiler_params=_cparams("arbitrary", "arbitrary"),
    )(a, w, x, g, dres)


def _merge_bwd(dx2b, w_out, p, q, proj, b_gate, PW):
    T, D = p.shape
    tm, tn = _tile(T, _TM), _tile(D, PW)
    nj = D // tn
    off = (PW + 2 * D) // tn

    def body(dx_ref, w_ref, p_ref, q_ref, l0_ref, l1_ref, b0_ref, b1_ref, dp_ref, dq_ref, dl0_ref, dl1_ref, db0_ref, db1_ref,
             dm_ref):
        @pl.when(pl.program_id(1) == 0)
        def _():
            db0_ref[...] = jnp.zeros_like(db0_ref)
            db1_ref[...] = jnp.zeros_like(db1_ref)

        dm_ref[...] = _dot_nt(dx_ref[...], w_ref[...])

        def visit(cols, rows, sums):
            dm = dm_ref[rows, cols]
            g0 = _sigmoid(l0_ref[rows, cols] + b0_ref[:, cols])
            g1 = _sigmoid(l1_ref[rows, cols] + b1_ref[:, cols])
            dp_ref[rows, cols] = (g0 * dm).astype(bf16)
            dq_ref[rows, cols] = (g1 * dm).astype(bf16)
            dl0 = dm * p_ref[rows, cols].astype(f32) * (g0 * (1.0 - g0))
            dl1 = dm * q_ref[rows, cols].astype(f32) * (g1 * (1.0 - g1))
            dl0_ref[rows, cols] = dl0.astype(bf16)
            dl1_ref[rows, cols] = dl1.astype(bf16)
            return [sums[0] + _fold8(dl0), sums[1] + _fold8(dl1)]

        def flush(cols, totals):
            db0_ref[:, cols] += totals[0]
            db1_ref[:, cols] += totals[1]

        _slabs_with_sums(tm, tn, 2, visit, flush)

    tile = pl.BlockSpec((tm, tn), lambda j, i: (i, j))
    vecj = pl.BlockSpec((1, tn), lambda j, i: (0, j))
    tb = jax.ShapeDtypeStruct((T, D), bf16)
    vb = jax.ShapeDtypeStruct((1, D), f32)
    return pl.pallas_call(
        body, name="merge_bwd", grid=(nj, T // tm),
        in_specs=[pl.BlockSpec((tm, D), lambda j, i: (i, 0)), pl.BlockSpec((tn, D), lambda j, i: (j, 0)), tile, tile,
                  pl.BlockSpec((tm, tn), lambda j, i: (i, off + j)), pl.BlockSpec((tm, tn), lambda j, i: (i, off + nj + j)),
                  vecj, pl.BlockSpec((1, tn), lambda j, i: (0, nj + j))],
        out_specs=[tile, tile, tile, tile, vecj, vecj],
        out_shape=[tb, tb, tb, tb, vb, vb],
        scratch_shapes=[pltpu.VMEM((tm, tn), f32)],
        compiler_params=_cparams("parallel", "arbitrary"),
    )(dx2b, w_out, p, q, proj, proj, b_gate, b_gate)


def _lru_bwd(dy, proj, hs, D, conv_w, conv_b, w_a, b_a, w_i, b_i, lam):
    T = dy.shape[0]
    NB, bw, _ = w_a.shape
    K = 4
    tc = _tile(T, _TC)
    nt = T // tc
    H = _CONV_HALO
    hb = D // 2

    def body(dy_ref, u0_ref, u1_ref, g0_ref, g1_ref, hs_ref, uh0_ref, uh1_ref, hh_ref,
             cw_ref, cb_ref, wa_ref, ba_ref, wi_ref, bi_ref, lam_ref,
             du_ref, dg_ref, dwa_ref, dwi_ref, dcw_ref, dvec_ref,
             ext_ref, hext_ref, dext_ref, q_ref, a_scr, r_scr, i_scr, v_scr, g_scr, car_scr):
        j = pl.program_id(0)
        first = j == nt - 1

        @pl.when(j == 0)
        def _():
            dext_ref[pl.ds(tc, H), :] = jnp.zeros((H, D), f32)
            car_scr[...] = jnp.zeros_like(car_scr)
            dwa_ref[...] = jnp.zeros_like(dwa_ref)
            dwi_ref[...] = jnp.zeros_like(dwi_ref)
            dcw_ref[...] = jnp.zeros_like(dcw_ref)
            dvec_ref[...] = jnp.zeros_like(dvec_ref)

        ext_ref[pl.ds(0, H), 0:hb] = jnp.where(first, 0.0, uh0_ref[...])
        ext_ref[pl.ds(0, H), hb:D] = jnp.where(first, 0.0, uh1_ref[...])
        ext_ref[pl.ds(H, tc), 0:hb] = u0_ref[...]
        ext_ref[pl.ds(H, tc), hb:D] = u1_ref[...]
        hext_ref[pl.ds(0, H), :] = jnp.where(first, 0.0, hh_ref[...])
        hext_ref[pl.ds(H, tc), :] = hs_ref[...]
        lamv = lam_ref[...]
        sp = _softplus(-lamv)

        for cols, rows in _slabs(tc, D):
            v_scr[rows, cols] = _causal_conv(ext_ref, cw_ref, cb_ref, rows.size, K, cols, rows.start)
        for b in range(NB):
            cols = slice(b * bw, (b + 1) * bw)
            vb = v_scr[:, cols].astype(bf16)
            r_scr[:, cols] = _dot(vb, wa_ref[b])
            i_scr[:, cols] = _dot(vb, wi_ref[b])
        for cols, rows in _slabs(tc, D):
            r = _sigmoid(r_scr[rows, cols] + ba_ref[:, cols])
            r_scr[rows, cols] = r
            i_scr[rows, cols] = _sigmoid(i_scr[rows, cols] + bi_ref[:, cols])
            a = jnp.exp(-_LRU_C * r * sp[:, cols])
            a_scr[rows, cols] = a
            g_ref, gcols = (g0_ref, cols) if cols.start < hb else (g1_ref, slice(cols.start - hb, cols.stop - hb))
            ge, gg = _gelu_both(g_ref[rows, gcols])
            dyv = dy_ref[rows, cols]
            dho = dyv * ge
            g_scr[rows, cols] = dho
            q_ref[rows, cols] = a * dho
            dg_ref[rows, cols] = (dyv * hs_ref[rows, cols] * gg).astype(bf16)

        q_ref[pl.ds(tc, 1), :] = car_scr[0:1, :]
        _scan_rows(a_scr, q_ref, q_ref, car_scr, tc, D, reverse=True)

        def gates(cols, rows, sums):
            g = g_scr[rows, cols] + q_ref[pl.ds(rows.start + 1, rows.size), cols]
            v, r, i, a = v_scr[rows, cols], r_scr[rows, cols], i_scr[rows, cols], a_scr[rows, cols]
            mult = jnp.sqrt(1.0 - a * a)
            h_prev = hext_ref[pl.ds(H - 1 + rows.start, rows.size), cols]
            gm = g * mult
            dext_ref[rows, cols] = gm * i
            dlog_a = (g * h_prev - g * (i * v) * (a / mult)) * a
            dpr = dlog_a * (-_LRU_C * sp[:, cols]) * (r * (1.0 - r))
            dpi = gm * v * (i * (1.0 - i))
            r_scr[rows, cols] = dpr
            i_scr[rows, cols] = dpi
            return [sums[0] + _fold8(dpr), sums[1] + _fold8(dpi), sums[2] + _fold8(dlog_a * (-_LRU_C * r))]

        def gates_flush(cols, totals):
            dvec_ref[1:2, cols] += totals[0]
            dvec_ref[2:3, cols] += totals[1]
            dvec_ref[3:4, cols] += totals[2] * (-_sigmoid(-lamv[:, cols]))

        _slabs_with_sums(tc, D, 3, gates, gates_flush)
        for b in range(NB):
            cols = slice(b * bw, (b + 1) * bw)
            dprb, dpib, vb = r_scr[:, cols].astype(bf16), i_scr[:, cols].astype(bf16), v_scr[:, cols].astype(bf16)
            dext_ref[pl.ds(0, tc), cols] += _dot_nt(dprb, wa_ref[b]) + _dot_nt(dpib, wi_ref[b])
            dwa_ref[b] += _dot_tn(vb, dprb)
            dwi_ref[b] += _dot_tn(vb, dpib)

        def conv_t(cols, rows, sums):
            r0, n = rows.start, rows.size
            u = ext_ref[pl.ds(H + r0, n), cols]
            dv = dext_ref[rows, cols]
            du = cw_ref[K - 1:K, cols] * dv
            new = [None] * K + [sums[K] + _fold8(dv)]
            new[K - 1] = sums[K - 1] + _fold8(u * dv)
            for k in range(K - 1):
                sh = dext_ref[pl.ds(r0 + K - 1 - k, n), cols]
                du = du + cw_ref[k:k + 1, cols] * sh
                new[k] = sums[k] + _fold8(u * sh)
            du_ref[rows, cols] = du.astype(bf16)
            return new

        def conv_t_flush(cols, totals):
            for k in range(K):
                dcw_ref[k:k + 1, cols] += totals[k]
            dvec_ref[0:1, cols] += totals[K]

        _slabs_with_sums(tc, D, K + 1, conv_t, conv_t_flush)
        dext_ref[pl.ds(tc, H), :] = dext_ref[pl.ds(0, H), :]

    hblk = tc // H
    rev = lambda j: nt - 1 - j
    halo = lambda j: jnp.maximum((nt - 1 - j) * hblk - 1, 0)
    vec = pl.BlockSpec((1, D), lambda j: (0, 0))
    wspec = pl.BlockSpec((NB, bw, bw), lambda j: (0, 0, 0))
    acc8 = pl.BlockSpec((8, D), lambda j: (0, 0))
    big = pltpu.VMEM((tc, D), f32)
    return pl.pallas_call(
        body, name="lru_bwd", grid=(nt,),
        in_specs=[pl.BlockSpec((tc, D), lambda j: (rev(j), 0)),
                  pl.BlockSpec((tc, hb), lambda j: (rev(j), 1)), pl.BlockSpec((tc, hb), lambda j: (rev(j), 2)),
                  pl.BlockSpec((tc, hb), lambda j: (rev(j), 3)), pl.BlockSpec((tc, hb), lambda j: (rev(j), 4)),
                  pl.BlockSpec((tc, D), lambda j: (rev(j), 0)),
                  pl.BlockSpec((H, hb), lambda j: (halo(j), 1)), pl.BlockSpec((H, hb), lambda j: (halo(j), 2)),
                  pl.BlockSpec((H, D), lambda j: (halo(j), 0)),
                  acc8, vec, wspec, vec, wspec, vec, vec],
        out_specs=[pl.BlockSpec((tc, D), lambda j: (rev(j), 0)), pl.BlockSpec((tc, D), lambda j: (rev(j), 0)),
                   wspec, wspec, acc8, acc8],
        out_shape=[jax.ShapeDtypeStruct((T, D), bf16), jax.ShapeDtypeStruct((T, D), bf16),
                   jax.ShapeDtypeStruct((NB, bw, bw), f32), jax.ShapeDtypeStruct((NB, bw, bw), f32),
                   jax.ShapeDtypeStruct((8, D), f32), jax.ShapeDtypeStruct((8, D), f32)],
        scratch_shapes=[pltpu.VMEM((H + tc, D), f32), pltpu.VMEM((H + tc, D), f32), pltpu.VMEM((tc + H, D), f32),
                        pltpu.VMEM((tc + H, D), f32), big, big, big, big, big, pltpu.VMEM((8, D), f32)],
        compiler_params=_cparams("arbitrary"),
    )(dy, proj, proj, proj, proj, hs, proj, proj, hs, conv_w, conv_b, w_a, b_a, w_i, b_i, lam)


def _pool_bwd(dy, proj, w_pool, pool_scale):
    T, PW = dy.shape
    G, gw, _ = w_pool.shape
    tc = _tile(T, _TC)
    nt = T // tc
    H = _POOL_HALO

    def body(dy_ref, u_ref, uh_ref, w_ref, s_ref, du_ref, dw_ref, ds_ref, ext_ref, eext_ref):
        j = pl.program_id(0)
        first = j == nt - 1
        row0 = (nt - 1 - j) * tc

        @pl.when(j == 0)
        def _():
            eext_ref[pl.ds(tc, H), :] = jnp.zeros((H, PW), f32)
            dw_ref[...] = jnp.zeros_like(dw_ref)
            ds_ref[...] = jnp.zeros_like(ds_ref)

        ext_ref[pl.ds(0, H), :] = jnp.where(first, 0.0, uh_ref[...])
        ext_ref[pl.ds(H, tc), :] = u_ref[...]
        t_glob = row0 + lax.broadcasted_iota(jnp.int32, (tc, 1), 0)
        dds = []
        for g, (m, u) in enumerate(_window_means(ext_ref, row0, tc, gw)):
            cols = slice(g * gw, (g + 1) * gw)
            d = (m - u).astype(bf16)
            yraw = _dot(d, w_ref[g])
            dyv = dy_ref[:, cols]
            ds_ref[:, cols] += jnp.sum(dyv * yraw, axis=0, keepdims=True)
            dyr = (dyv * s_ref[:, cols]).astype(bf16)
            dd = _dot_nt(dyr, w_ref[g])
            dw_ref[g] += _dot_tn(d, dyr)
            cnt = jnp.minimum(t_glob + 1, _POOL_WINDOWS[g]).astype(f32)
            eext_ref[pl.ds(0, tc), cols] = dd / cnt
            dds.append(dd)
        n = tc + H
        for g, w in enumerate(_POOL_WINDOWS):
            cols = slice(g * gw, (g + 1) * gw)
            s = eext_ref[:, cols]
            st = 1
            while st < w:
                s = s + pltpu.roll(s, n - st, 0)
                st *= 2
            du_ref[:, cols] = (s[0:tc, :] - dds[g]).astype(bf16)
        eext_ref[pl.ds(tc, H), :] = eext_ref[pl.ds(0, H), :]

    hblk = tc // H
    return pl.pallas_call(
        body, name="pool_bwd", grid=(nt,),
        in_specs=[pl.BlockSpec((tc, PW), lambda j: (nt - 1 - j, 0)), pl.BlockSpec((tc, PW), lambda j: (nt - 1 - j, 0)),
                  pl.BlockSpec((H, PW), lambda j: (jnp.maximum((nt - 1 - j) * hblk - 1, 0), 0)),
                  pl.BlockSpec((G, gw, gw), lambda j: (0, 0, 0)), pl.BlockSpec((1, PW), lambda j: (0, 0))],
        out_specs=[pl.BlockSpec((tc, PW), lambda j: (nt - 1 - j, 0)), pl.BlockSpec((G, gw, gw), lambda j: (0, 0, 0)),
                   pl.BlockSpec((1, PW), lambda j: (0, 0))],
        out_shape=[jax.ShapeDtypeStruct((T, PW), bf16), jax.ShapeDtypeStruct((G, gw, gw), f32), jax.ShapeDtypeStruct((1, PW), f32)],
        scratch_shapes=[pltpu.VMEM((H + tc, PW), f32), pltpu.VMEM((tc + H, PW), f32)],
        compiler_params=_cparams("arbitrary"),
    )(dy, proj, proj, w_pool, pool_scale)


_MESH = pl.DeviceIdType.MESH
_HBM = pl.BlockSpec(memory_space=pltpu.HBM)


def _slab(ref, kind, blk, n):
    start = blk * n
    if n % _LANE == 0:
        start = pl.multiple_of(start, _LANE)
    if kind == "col":
        return ref.at[:, pl.ds(start, n)]
    if kind == "row":
        return ref.at[pl.ds(start, n), :]
    if kind == "mid":
        return ref.at[:, pl.ds(start, n), :]
    raise ValueError(kind)


def _my_place():
    x, y, c = lax.axis_index("x"), lax.axis_index("y"), lax.axis_index("c")
    return x, y, c


def _blk(px, py, pc):
    return 4 * px + 2 * py + pc


_SEM = pl.BlockSpec(memory_space=pltpu.SEMAPHORE)
_EFFECT = pltpu.SideEffectType.DATAFLOW_SIDE_EFFECTING


_ALL_PEERS = (1, 2, 3, 4, 5, 6, 7)


def _peers(x, y, c):
    return [(k, (x ^ (k >> 2), y ^ ((k >> 1) & 1), c ^ (k & 1))) for k in range(1, 8)]


class _Route:
    def __init__(self, mode, kind, size):
        self.mode, self.kind, self.size = mode, kind, size

    def src(self, ref, peer_blk):
        return ref if self.mode == "gather" else _slab(ref, self.kind, peer_blk, self.size)

    def dst(self, ref, origin_blk):
        return _slab(ref, self.kind, origin_blk, self.size) if self.mode == "gather" else ref.at[origin_blk]


def _send_start(name, srcs, lands, routes, groups):
    nt, ng = len(srcs), len(groups)

    def body(*refs):
        src_refs, land_refs = refs[:nt], refs[nt:2 * nt]
        sems = refs[2 * nt:2 * nt + 2 * ng]
        token = refs[-1]
        x, y, c = _my_place()
        me = _blk(x, y, c)
        for gi, (grp, ks) in enumerate(groups):
            for pos, t in enumerate(grp):
                for k, peer in _peers(x, y, c):
                    if k in ks:
                        s = len(ks) * pos + ks.index(k)
                        pltpu.make_async_remote_copy(
                            src_ref=routes[t].src(src_refs[t], _blk(*peer)), dst_ref=routes[t].dst(land_refs[t], me),
                            send_sem=sems[2 * gi].at[s], recv_sem=sems[2 * gi + 1].at[s],
                            device_id=peer, device_id_type=_MESH).start()
        token[...] = jnp.zeros_like(token)

    hbm = lambda a: pltpu.HBM(a.shape, a.dtype)
    out_shape = []
    for grp, ks in groups:
        out_shape += [pltpu.SemaphoreType.DMA((len(ks) * len(grp),)), pltpu.SemaphoreType.DMA((len(ks) * len(grp),))]
    out_shape += [hbm(a) for a in srcs] + [hbm(a) for a in lands] + [jax.ShapeDtypeStruct((8, _LANE), f32)]
    res = pl.pallas_call(
        body, name=name, out_shape=out_shape,
        in_specs=[_HBM] * (2 * nt),
        out_specs=[_SEM] * (2 * ng) + [_HBM] * (2 * nt) + [pl.BlockSpec(memory_space=pltpu.VMEM)],
        input_output_aliases={t: 2 * ng + t for t in range(2 * nt)},
        compiler_params=pltpu.CompilerParams(has_side_effects=_EFFECT),
    )(*[pltpu.with_memory_space_constraint(a, pltpu.HBM) for a in list(srcs) + list(lands)])
    sems = [(res[2 * g], res[2 * g + 1]) for g in range(ng)]
    return sems, res[2 * ng:2 * ng + nt], res[2 * ng + nt:2 * ng + 2 * nt], res[-1]


def _send_wait(name, srcs, lands, routes, sems, after, ks=_ALL_PEERS):
    n = len(srcs)

    def body(*refs):
        src_refs, land_refs = refs[:n], refs[n:2 * n]
        send_sems, recv_sems = refs[2 * n], refs[2 * n + 1]
        x, y, c = _my_place()
        for pos in range(n):
            for k, peer in _peers(x, y, c):
                if k in ks:
                    pb = _blk(*peer)
                    s = len(ks) * pos + ks.index(k)
                    cp = pltpu.make_async_remote_copy(
                        src_ref=routes[pos].src(src_refs[pos], pb), dst_ref=routes[pos].dst(land_refs[pos], pb),
                        send_sem=send_sems.at[s], recv_sem=recv_sems.at[s], device_id=peer, device_id_type=_MESH)
                    cp.wait_send()
                    cp.wait_recv()

    hbm = lambda a: pltpu.HBM(a.shape, a.dtype)
    res = pl.pallas_call(
        body, name=name, out_shape=[hbm(a) for a in srcs] + [hbm(a) for a in lands],
        in_specs=[_HBM] * (2 * n) + [_SEM, _SEM, pl.BlockSpec(memory_space=pl.ANY)],
        out_specs=[_HBM] * (2 * n),
        input_output_aliases={t: t for t in range(2 * n)},
        compiler_params=pltpu.CompilerParams(has_side_effects=_EFFECT),
    )(*srcs, *lands, sems[0], sems[1], after)
    return res[:n], res[n:]


def _forward_start(name, lands, routes, ks):
    n = len(lands)

    def body(*refs):
        land_refs, send_sems, recv_sems, token = refs[:n], refs[n], refs[n + 1], refs[-1]
        x, y, c = _my_place()
        for pos in range(n):
            for i, k in enumerate(ks):
                part = routes[pos].dst(land_refs[pos], _blk(x ^ (k >> 2), y ^ ((k >> 1) & 1), c))
                pltpu.make_async_remote_copy(
                    src_ref=part, dst_ref=part, send_sem=send_sems.at[len(ks) * pos + i],
                    recv_sem=recv_sems.at[len(ks) * pos + i], device_id=(x, y, 1 - c), device_id_type=_MESH).start()
        token[...] = jnp.zeros_like(token)

    hbm = lambda a: pltpu.HBM(a.shape, a.dtype)
    sem = pltpu.SemaphoreType.DMA((len(ks) * n,))
    res = pl.pallas_call(
        body, name=name, out_shape=[sem, sem] + [hbm(a) for a in lands] + [jax.ShapeDtypeStruct((8, _LANE), f32)],
        in_specs=[_HBM] * n, out_specs=[_SEM, _SEM] + [_HBM] * n + [pl.BlockSpec(memory_space=pltpu.VMEM)],
        input_output_aliases={t: 2 + t for t in range(n)},
        compiler_params=pltpu.CompilerParams(has_side_effects=_EFFECT),
    )(*[pltpu.with_memory_space_constraint(a, pltpu.HBM) for a in lands])
    return (res[0], res[1]), res[2:2 + n], res[-1]


def _forward_wait(name, lands, routes, sems, after, ks):
    n = len(lands)

    def body(*refs):
        land_refs, send_sems, recv_sems = refs[:n], refs[n], refs[n + 1]
        x, y, c = _my_place()
        for pos in range(n):
            for i, k in enumerate(ks):
                px, py = x ^ (k >> 2), y ^ ((k >> 1) & 1)
                cp = pltpu.make_async_remote_copy(
                    src_ref=routes[pos].dst(land_refs[pos], _blk(px, py, c)),
                    dst_ref=routes[pos].dst(land_refs[pos], _blk(px, py, 1 - c)),
                    send_sem=send_sems.at[len(ks) * pos + i], recv_sem=recv_sems.at[len(ks) * pos + i],
                    device_id=(x, y, 1 - c), device_id_type=_MESH)
                cp.wait_send()
                cp.wait_recv()

    hbm = lambda a: pltpu.HBM(a.shape, a.dtype)
    return pl.pallas_call(
        body, name=name, out_shape=[hbm(a) for a in lands],
        in_specs=[_HBM] * n + [_SEM, _SEM, pl.BlockSpec(memory_space=pl.ANY)], out_specs=[_HBM] * n,
        input_output_aliases={t: t for t in range(n)},
        compiler_params=pltpu.CompilerParams(has_side_effects=_EFFECT),
    )(*lands, sems[0], sems[1], after)


def _copy_own(name, src, land, route, me):
    gather = route.mode == "gather"
    shard = src.shape if gather else land.shape[1:]
    lead = () if gather else (None,)

    if route.kind == "mid":
        grid = (1,)
        at_full = lambda i, me: (0, me[0], 0)
        at_shard = lambda i, me: (0, 0, 0)
        block = tuple(shard)
    else:
        rows, width = shard
        tr = _tile(rows, 512, 16)
        grid = (rows // tr,)
        block = (tr, width)
        if route.kind == "col":
            at_full = lambda i, me: (i, me[0])
        else:
            at_full = lambda i, me: (me[0] * grid[0] + i, 0)
        at_shard = lambda i, me: (i, 0)
    if gather:
        in_map, out_map = at_shard, at_full
    else:
        in_map, out_map = at_full, (lambda i, me: (me[0], *at_shard(i, me)))

    def body(me_ref, src_ref, land_ref, out_ref):
        out_ref[...] = src_ref[...]

    return pl.pallas_call(
        body, name=name, out_shape=jax.ShapeDtypeStruct(land.shape, land.dtype),
        grid_spec=pltpu.PrefetchScalarGridSpec(
            num_scalar_prefetch=1, grid=grid,
            in_specs=[pl.BlockSpec(block, in_map), pl.BlockSpec(memory_space=pl.ANY)],
            out_specs=pl.BlockSpec(lead + block, out_map)),
        input_output_aliases={2: 0},
        compiler_params=_cparams("arbitrary"),
    )(me, src, land)


def _place_own(name, srcs, lands, routes):
    me = _blk(*_my_place()).astype(jnp.int32).reshape(1)
    return [_copy_own(f"{name}_{t}", s, l, r, me) for t, (s, l, r) in enumerate(zip(srcs, lands, routes))]


def _exchange(fulls, kinds, sizes, whole):
    arrays = list(fulls) + list(whole)
    nt, nf = len(arrays), len(fulls)

    def shard_shape(t):
        s = list(arrays[t].shape)
        if t < nf:
            s[{"col": 1, "row": 0, "mid": 1}[kinds[t]]] = sizes[t]
        return tuple(s)

    def body(*refs):
        ins, outs = refs[:nt], refs[nt:2 * nt]
        send_sems, recv_sems, local_sems = refs[2 * nt:]
        x, y, c = _my_place()
        me = _blk(x, y, c)

        def src(t, blk):
            return _slab(ins[t], kinds[t], blk, sizes[t]) if t < nf else ins[t]

        mine = [pltpu.make_async_copy(src(t, me), outs[t].at[me], local_sems.at[t]) for t in range(nt)]
        for cp in mine:
            cp.start()
        sent = []
        for t in range(nt):
            for k in range(1, 8):
                peer = (x ^ (k >> 2), y ^ ((k >> 1) & 1), c ^ (k & 1))
                pb = _blk(*peer)
                cp = pltpu.make_async_remote_copy(
                    src_ref=src(t, pb), dst_ref=outs[t].at[me],
                    send_sem=send_sems.at[7 * t + k - 1], recv_sem=recv_sems.at[7 * t + k - 1],
                    device_id=peer, device_id_type=_MESH)
                cp.start()
                sent.append((cp, t, k, pb))
        for cp, t, k, pb in sent:
            pltpu.make_async_remote_copy(
                src_ref=src(t, pb), dst_ref=outs[t].at[pb],
                send_sem=send_sems.at[7 * t + k - 1], recv_sem=recv_sems.at[7 * t + k - 1],
                device_id=(x, y, c), device_id_type=_MESH).wait_recv()
        for cp, _, _, _ in sent:
            cp.wait_send()
        for cp in mine:
            cp.wait()

    return pl.pallas_call(
        body, name="exchange_grads",
        in_specs=[_HBM] * nt, out_specs=[_HBM] * nt,
        out_shape=[jax.ShapeDtypeStruct((_N_DEV,) + shard_shape(t), arrays[t].dtype) for t in range(nt)],
        scratch_shapes=[pltpu.SemaphoreType.DMA((7 * nt,)), pltpu.SemaphoreType.DMA((7 * nt,)), pltpu.SemaphoreType.DMA((nt,))],
        compiler_params=pltpu.CompilerParams(has_side_effects=True),
    )(*arrays)


def _adamw_update(w_ref, m_ref, v_ref, g, g_ref, d_ref, nm_ref, nv_ref):
    c1 = 1.0 - _ADAM_B1 ** _ADAM_STEP
    c2 = 1.0 - _ADAM_B2 ** _ADAM_STEP
    nm = _ADAM_B1 * m_ref[...] + (1.0 - _ADAM_B1) * g
    nv = _ADAM_B2 * v_ref[...] + (1.0 - _ADAM_B2) * (g * g)
    g_ref[...] = g
    nm_ref[...] = nm
    nv_ref[...] = nv
    d_ref[...] = -_ADAM_LR * ((nm / c1) / (jnp.sqrt(nv / c2) + _ADAM_EPS) + _ADAM_WD * w_ref[...])


def _adamw_small(me, ws, ms, vs, parts):
    n = len(ws)

    def body(me_ref, *refs):
        ins, outs = refs[:4 * n], refs[4 * n:]
        for t in range(n):
            w_ref, m_ref, v_ref, p_ref = ins[4 * t:4 * t + 4]
            r = w_ref.shape[0]
            g = p_ref[0, 0:r, :]
            for s in range(1, _N_DEV):
                g = g + p_ref[s, 0:r, :]
            _adamw_update(w_ref, m_ref, v_ref, g, *outs[4 * t:4 * t + 4])

    whole = lambda a: pl.BlockSpec(a.shape, lambda i, me, nd=a.ndim: (0,) * nd)
    in_specs, operands, out_specs, out_shape = [], [], [], []
    for w, m, v, p in zip(ws, ms, vs, parts):
        c = w.shape[1]
        mine = whole(p) if p.shape[2] == c else pl.BlockSpec((_N_DEV, p.shape[1], c), lambda i, me: (0, 0, me[0]))
        in_specs += [whole(w), whole(m), whole(v), mine]
        operands += [w, m, v, p]
        out_specs += [whole(w)] * 4
        out_shape += [jax.ShapeDtypeStruct(w.shape, f32)] * 4
    return pl.pallas_call(
        body, name="adamw_small", out_shape=out_shape,
        grid_spec=pltpu.PrefetchScalarGridSpec(num_scalar_prefetch=1, grid=(1,), in_specs=in_specs, out_specs=out_specs),
        compiler_params=_cparams("arbitrary"),
    )(me, *operands)


def _adamw(name, w, m, v, parts):
    R, C = w.shape
    n = parts.shape[0]
    tr = _tile(R, 256, 8)

    def body(w_ref, m_ref, v_ref, p_ref, g_ref, d_ref, nm_ref, nv_ref):
        g = p_ref[0].astype(f32)
        for s in range(1, n):
            g = g + p_ref[s].astype(f32)
        _adamw_update(w_ref, m_ref, v_ref, g, g_ref, d_ref, nm_ref, nv_ref)

    blk = pl.BlockSpec((tr, C), lambda i: (i, 0))
    sd = jax.ShapeDtypeStruct((R, C), f32)
    return pl.pallas_call(
        body, name=name, grid=(R // tr,),
        in_specs=[blk, blk, blk, pl.BlockSpec((n, tr, C), lambda i: (0, i, 0))],
        out_specs=[blk, blk, blk, blk], out_shape=[sd, sd, sd, sd],
        compiler_params=_cparams("parallel"),
    )(w, m, v, parts)


def _pad_rows8(a):
    return jnp.pad(a, ((0, 8 - a.shape[0]), (0, 0)))


def _local_step(x, target, p, get, emit, hint):
    T, D = x.shape

    def tie(a, *tokens):
        for tok in tokens:
            if tok is not None:
                a = a + tok[0, 0]
        return a

    n_parts = _N_DEV // _W_IN_PART
    w_in, base = get("w_in", x, 0)
    h1, proj = _in_proj_first(x, p["g_mix"], w_in, base, _W_IN_PART)
    hint("w_in", proj)
    for part in range(1, n_parts):
        w_in, base = get("w_in", proj, part)
        proj = _in_proj_more(f"in_proj_{part}", h1, w_in, proj, base, _W_IN_PART)
    w_pool = get("w_pool", proj)
    PW = w_pool.shape[0] * w_pool.shape[1]
    y_pool = _pool_fwd(proj, w_pool, p["pool_scale"])
    tok = hint("w_pool_proj", y_pool)
    lru_conv_w, w_a, w_i = get("lru_conv_w", proj), get("w_a", proj), get("w_i", proj)
    y_lru, hs = _lru_fwd(proj, D, PW, lru_conv_w, tie(p["lru_conv_b"], tok), w_a, p["b_a"], w_i, p["b_i"], p["lru_lambda"])
    tok = hint("w_up", y_lru)
    w_pp, w_lp = get("w_pool_proj", y_lru), get("w_lru_proj", y_lru)
    pp, qq, merged = _merge_fwd(y_pool, y_lru, w_pp, w_lp, proj, tie(p["b_gate"], tok))
    w_out = get("w_out", y_lru)
    x2, h2 = _out_proj(merged, w_out, x, p["g_mlp"])
    w_up = get("w_up", x2)
    up = _mm_nn("up_proj", h2, w_up, f32, tm_want=2 * _TM)
    tok = hint("w_down", up)
    ffn_conv_w = get("ffn_conv_w", y_lru)
    z = _ffn_fwd(up, ffn_conv_w, tie(p["ffn_conv_b"], tok))
    w_down = get("w_down", z)
    F = w_down.shape[0]
    dx3, dx3b, loss_t, dg_final = _down_loss(z, w_down, x2, target, p["g_final"])

    gs = {"g_final": dg_final}
    tok = emit("w_down", _mm_tn("dw_down", z, dx3b, bf16, tm_want=1536))
    dz = _mm_nt("dz", dx3b, w_down, bf16)
    dup, dcw_ffn, dcb_ffn = _ffn_bwd(dz, up, ffn_conv_w, tie(p["ffn_conv_b"], tok))
    gs["ffn_conv_w"] = dcw_ffn
    gs["ffn_conv_b"] = dcb_ffn

    tm = _tile(T, _TM)
    tk = _tile(F, _TN_MAX)
    nkh = F // tk
    tkt = _tile(T, _TK_T)
    tok = emit("w_up", _matmul(
        "dw_up", "tn", (h2, dup),
        [pl.BlockSpec((tkt, D), lambda i, j, k: (k, 0)), pl.BlockSpec((None, tkt, tk), lambda i, j, k: (j // nkh, k, j % nkh))],
        jax.ShapeDtypeStruct((D, 2 * F), bf16), pl.BlockSpec((D, tk), lambda i, j, k: (0, j)),
        (1, 2 * nkh, T // tkt), (D, tk)))

    tkc = _tile(F, _TK_UP)
    nkc = F // tkc
    dx2, dx2b, gs["g_mlp"] = _norm_bwd_matmul(
        "dh2", dup, pl.BlockSpec((None, tm, tkc), lambda i, k: (k // nkc, i, k % nkc)), 2 * nkc,
        w_up, pl.BlockSpec((D, tkc), lambda i, k: (0, k)), x2, tie(p["g_mlp"], tok), dx3)

    tok = emit("w_out", _mm_tn("dw_out", merged, dx2b, bf16))
    dP, dQ, dl0, dl1, db0, db1 = _merge_bwd(dx2b, w_out, pp, qq, proj, tie(p["b_gate"], tok), PW)
    gs["b_gate"] = jnp.concatenate([db0, db1], axis=1)
    tok = emit("w_pool_proj", _mm_tn("dw_pool_proj", y_pool, dP, bf16))
    tok2 = emit("w_lru_proj", _mm_tn("dw_lru_proj", y_lru, dQ, bf16))
    dy_pool = _mm_nt("dy_pool", dP, w_pp, f32)
    dy_lru = _mm_nt("dy_lru", dQ, w_lp, f32)

    du_lru, du_gelu, dwa, dwi, dcw_lru, dvec = _lru_bwd(
        dy_lru, proj, hs, D, lru_conv_w, tie(p["lru_conv_b"], tok, tok2), w_a, p["b_a"], w_i, p["b_i"], p["lru_lambda"])
    tok = emit("w_a", dwa.astype(bf16))
    tok2 = emit("w_i", dwi.astype(bf16))
    gs["lru_conv_w"] = dcw_lru
    gs["lru_conv_b"], gs["b_a"], gs["b_i"], gs["lru_lambda"] = dvec[0:1], dvec[1:2], dvec[2:3], dvec[3:4]
    du_pool, dwp, gs["pool_scale"] = _pool_bwd(dy_pool, proj, w_pool, tie(p["pool_scale"], tok, tok2))
    tok = emit("w_pool", dwp.astype(bf16))

    dproj = jnp.concatenate([du_pool, du_lru, du_gelu, dl0, dl1], axis=1)
    tok2 = emit("w_in", _mm_tn("dw_in", h1, dproj, bf16))
    NI = dproj.shape[1]
    tki = _tile(NI, _TN_MAX)
    grad_x, _, gs["g_mix"] = _norm_bwd_matmul(
        "dh1", dproj, pl.BlockSpec((tm, tki), lambda i, k: (i, k)), NI // tki,
        w_in, pl.BlockSpec((D, tki), lambda i, k: (0, k)), x, tie(p["g_mix"], tok, tok2), dx2)
    return loss_t[0, 0], grad_x, gs


_MATRICES = {"w_in": "col", "w_pool": "mid", "w_a": "mid", "w_i": "mid", "w_pool_proj": "col", "w_lru_proj": "row",
             "w_out": "row", "w_up": "col", "w_down": "row"}
_CONVS = ("lru_conv_w", "ffn_conv_w")
_GATHER_GROUPS = (("w_pool", "lru_conv_w", "w_a", "w_i"), ("w_pool_proj", "w_lru_proj", "w_out", "ffn_conv_w"),
                  ("w_up",), ("w_down",))
_GROUP_TWO_LEVEL = (False, True, True, True)
_SAME_CORE = (2, 4, 6)
_TWO_LEVEL = (1, 2, 4, 6)
_W_IN_PARTS = ((1,), (4,), (2,), (6,))
_W_IN_PART = 2
_VECTORS = ("g_mix", "b_gate", "pool_scale", "lru_conv_b", "b_a", "b_i", "lru_lambda", "g_mlp", "ffn_conv_b", "g_final")
_WEIGHTS = ("g_mix", "w_in", "b_gate", "w_pool", "pool_scale", "lru_conv_w", "lru_conv_b", "w_a", "b_a", "w_i", "b_i",
            "lru_lambda", "w_pool_proj", "w_lru_proj", "w_out", "g_mlp", "w_up", "ffn_conv_w", "ffn_conv_b", "w_down", "g_final")


def _full_shape(shape, kind):
    s = list(shape)
    s[{"col": 1, "row": 0, "mid": 1}[kind]] *= _N_DEV
    return tuple(s)


def _step(x, target, w, m, v):
    x, target = x[0], target[0]
    me = _blk(*_my_place())

    axis = {"col": 1, "row": 0, "mid": 1}
    kind = dict(_MATRICES, **{n: "col" for n in _CONVS})
    shard = {n: w[n].astype(bf16) for n in _MATRICES}
    shard.update({n: _pad_rows8(w[n]) for n in _CONVS})
    order = ["w_in"] + [n for grp in _GATHER_GROUPS for n in grp]
    index = {n: t for t, n in enumerate(order)}
    n_parts = len(_W_IN_PARTS)
    groups = [([0], ks) for ks in _W_IN_PARTS]
    groups += [([index[n] for n in grp], _TWO_LEVEL if two else _ALL_PEERS) for grp, two in zip(_GATHER_GROUPS, _GROUP_TWO_LEVEL)]
    g_routes = [_Route("gather", kind[n], shard[n].shape[axis[kind[n]]]) for n in order]
    lands = [lax.empty(_full_shape(shard[n].shape, kind[n]), shard[n].dtype) for n in order]
    g_sems, g_srcs, g_lands, _ = _send_start("gather_start", [shard[n] for n in order], lands, g_routes, groups)
    gathered, passing = {}, {}
    x_, y_, _c = _my_place()
    w_in_state = [[g_srcs[0]], [g_lands[0]]]

    def hint(name, after):
        if name == "w_in":
            srcs, got = w_in_state
            tok = None
            for part in range(1, n_parts):
                ks = _W_IN_PARTS[part]
                srcs, got = _send_wait(f"gather_wait_w_in_{part}", srcs, got, g_routes[:1], g_sems[part], after, ks)
                sems, got, tok = _forward_start(f"gather_pass_w_in_{part}", got, g_routes[:1], ks)
                passing[name, part] = sems
            w_in_state[:] = [srcs, got]
            return tok
        gi = next(i for i, grp in enumerate(_GATHER_GROUPS) if name in grp)
        if not _GROUP_TWO_LEVEL[gi] or gi in passing or _GATHER_GROUPS[gi][0] in gathered:
            return None
        ts = groups[n_parts + gi][0]
        routes = [g_routes[t] for t in ts]
        srcs, got = _send_wait(f"gather_wait_{gi}", [g_srcs[t] for t in ts], [g_lands[t] for t in ts], routes,
                               g_sems[n_parts + gi], after, _TWO_LEVEL)
        got = _place_own(f"gather_own_{gi}", srcs, got, routes)
        sems, got, tok = _forward_start(f"gather_pass_{gi}", got, routes, _SAME_CORE)
        passing[gi] = (sems, got, routes)
        return tok

    def get(name, after, part=None):
        if name == "w_in":
            ks = _W_IN_PARTS[part]
            if part == 0:
                srcs, got = _send_wait("gather_wait_w_in_0", *w_in_state, g_routes[:1], g_sems[0], after, ks)
                got = _place_own("gather_own_w_in", srcs, got, g_routes[:1])
                w_in_state[:] = [srcs, got]
            else:
                if (name, part) not in passing:
                    hint(name, after)
                got = _forward_wait(f"gather_got_w_in_{part}", w_in_state[1], g_routes[:1], passing[name, part], after, ks)
                w_in_state[1] = got
            k = ks[-1]
            base = 4 * (x_ ^ (k >> 2)) + 2 * (y_ ^ ((k >> 1) & 1))
            return got[0], base.astype(jnp.int32).reshape(1)
        if name not in gathered:
            gi = next(i for i, grp in enumerate(_GATHER_GROUPS) if name in grp)
            if _GROUP_TWO_LEVEL[gi]:
                hint(name, after)
                sems, got, routes = passing[gi]
                full = _forward_wait(f"gather_got_{gi}", got, routes, sems, after, _SAME_CORE)
            else:
                ts = groups[n_parts + gi][0]
                routes = [g_routes[t] for t in ts]
                srcs, got = _send_wait(f"gather_wait_{gi}", [g_srcs[t] for t in ts], [g_lands[t] for t in ts], routes,
                                       g_sems[n_parts + gi], after)
                full = _place_own(f"gather_own_{gi}", srcs, got, routes)
            gathered.update(zip(_GATHER_GROUPS[gi], full))
        return gathered[name]

    sent = {}

    def emit(name, grad):
        k = _MATRICES[name]
        size = w[name].shape[axis[k]]
        route = _Route("scatter", k, size)
        shp = list(grad.shape)
        shp[axis[k]] = size
        land = lax.empty((_N_DEV, *shp), grad.dtype)
        sems, srcs, lnds, token = _send_start("grad_start_" + name, [grad], [land], [route], [([0], _ALL_PEERS)])
        sent[name] = (srcs, lnds, [route], sems[0])
        return token

    p = {n: w[n].reshape(1, -1) for n in _VECTORS}
    loss_t, grad_x, gs = _local_step(x, target, p, get, emit, hint)
    loss = lax.psum(loss_t, ("x", "y", "c"))

    small_names = list(_VECTORS) + list(_CONVS)
    small_parts = _exchange([], [], [], [gs[n] for n in small_names])

    out = {}
    mats = list(_MATRICES)
    for n in mats:
        srcs, lnds, routes, sems = sent[n]
        srcs, got = _send_wait("grad_wait_" + n, srcs, lnds, routes, sems, grad_x)
        parts = _place_own("grad_own_" + n, srcs, got, routes)[0]
        shp = w[n].shape
        r2 = (-1, shp[-1])
        res = _adamw("adamw_" + n, w[n].reshape(r2), m[n].reshape(r2), v[n].reshape(r2),
                     parts.reshape((_N_DEV,) + w[n].reshape(r2).shape))
        out[n] = [a.reshape(shp) for a in res]
    two_d = lambda a: a.reshape(-1, a.shape[-1])
    res = _adamw_small(me.astype(jnp.int32).reshape(1), [two_d(w[n]) for n in small_names], [two_d(m[n]) for n in small_names],
                       [two_d(v[n]) for n in small_names], small_parts)
    for t, n in enumerate(small_names):
        out[n] = [a.reshape(w[n].shape) for a in res[4 * t:4 * t + 4]]
    return loss, grad_x[None], out


def kernel(x, g_mix, w_in, b_gate, w_pool, pool_scale, lru_conv_w, lru_conv_b, w_a, b_a, w_i, b_i, lru_lambda, w_pool_proj, w_lru_proj, w_out, g_mlp, w_up, ffn_conv_w, ffn_conv_b, w_down, g_final, loss_target, m_g_mix, m_w_in, m_b_gate, m_w_pool, m_pool_scale, m_lru_conv_w, m_lru_conv_b, m_w_a, m_b_a, m_w_i, m_b_i, m_lru_lambda, m_w_pool_proj, m_w_lru_proj, m_w_out, m_g_mlp, m_w_up, m_ffn_conv_w, m_ffn_conv_b, m_w_down, m_g_final, v_g_mix, v_w_in, v_b_gate, v_w_pool, v_pool_scale, v_lru_conv_w, v_lru_conv_b, v_w_a, v_b_a, v_w_i, v_b_i, v_lru_lambda, v_w_pool_proj, v_w_lru_proj, v_w_out, v_g_mlp, v_w_up, v_ffn_conv_w, v_ffn_conv_b, v_w_down, v_g_final):
    given = dict(locals())
    orig = {n: given[n].shape for n in _WEIGHTS}

    def squeeze(a, n):
        return a if n == "g_final" else a[0]

    w = {n: squeeze(given[n], n) for n in _WEIGHTS}
    m = {n: squeeze(given["m_" + n], n) for n in _WEIGHTS}
    v = {n: squeeze(given["v_" + n], n) for n in _WEIGHTS}
    for d in (w, m, v):
        d["g_final"] = d["g_final"].reshape(1, -1)
    loss, grad_x, out = _step(x, loss_target, w, m, v)
    res = [loss, grad_x]
    for k in range(4):
        res += [out[n][k].reshape(orig[n]) for n in _WEIGHTS]
    return tuple(res)
```

```python
import functools

import jax
import jax.numpy as jnp
from jax import lax
from jax.experimental import pallas as pl
from jax.experimental.pallas import tpu as pltpu

f32 = jnp.float32
bf16 = jnp.bfloat16

_EPS = 1e-6
_LRU_C = 8.0
_POOL_WINDOWS = (2, 4, 8, 16)
_POOL_HALO = 16
_CONV_HALO = 8
_GELU_C0 = 0.7978845608028654
_GELU_C1 = 0.044715
_ADAM_LR, _ADAM_B1, _ADAM_B2, _ADAM_EPS, _ADAM_WD, _ADAM_STEP = 0.001, 0.9, 0.999, 1e-08, 0.01, 10
_N_DEV = 8
_LANE = 128
_VMEM_LIMIT = 60 * 1024 * 1024

_TM = 512
_TM_SMALL = 256
_TC = 256
_TK_T = 1024
_TN_MAX = 1536
_CW = 1024
_TK_DOWN = 1536
_TK_UP = 2048


def _cparams(*sem):
    return pltpu.CompilerParams(dimension_semantics=tuple(sem), vmem_limit_bytes=_VMEM_LIMIT)


def _tile(n, want, mult=1):
    if n <= want:
        return n
    t = want - want % mult
    while n % t:
        t -= mult
    return t


def _gelu(x):
    t = jnp.tanh(x * (_GELU_C0 + (_GELU_C0 * _GELU_C1) * (x * x)))
    return x * (0.5 + 0.5 * t)


def _gelu_both(x):
    x2 = x * x
    t = jnp.tanh(x * (_GELU_C0 + (_GELU_C0 * _GELU_C1) * x2))
    h = 0.5 + 0.5 * t
    return x * h, h + (x * (1.0 - t * t)) * (0.5 * _GELU_C0 + (1.5 * _GELU_C0 * _GELU_C1) * x2)


def _fold8(x):
    out = x[0:8]
    for r in range(8, x.shape[0], 8):
        out = out + x[r:r + 8]
    return out


def _sigmoid(x):
    return jax.nn.sigmoid(x)


def _dot(a, b):
    return jnp.dot(a, b, preferred_element_type=f32)


def _dot_nt(a, b):
    return lax.dot_general(a, b, (((1,), (1,)), ((), ())), preferred_element_type=f32)


def _dot_tn(a, b):
    return lax.dot_general(a, b, (((0,), (0,)), ((), ())), preferred_element_type=f32)


def _rms_rows(x_ref, g_ref, h_ref, n_rows, n_cols):
    rr = _tile(n_rows, _SLAB_ROWS)
    cg = _tile(n_cols, _SLAB_COLS)
    for r0 in range(0, n_rows, rr):
        rows = pl.ds(r0, rr)
        x = x_ref[rows, :]
        r = lax.rsqrt(jnp.mean(x * x, axis=-1, keepdims=True) + _EPS)
        for c0 in range(0, n_cols, cg):
            cols = slice(c0, c0 + cg)
            h_ref[rows, cols] = (x_ref[rows, cols] * r * g_ref[:, cols]).astype(bf16)


def _matmul(name, mode, operands, in_specs, out_shape, out_spec, grid, acc_shape):
    dot = {"nn": _dot, "nt": _dot_nt, "tn": _dot_tn}[mode]
    nk = grid[2]

    def body_whole(a_ref, b_ref, o_ref):
        o_ref[...] = dot(a_ref[...], b_ref[...]).astype(o_ref.dtype)

    def body(a_ref, b_ref, o_ref, acc_ref):
        k = pl.program_id(2)

        @pl.when(k == 0)
        def _():
            acc_ref[...] = dot(a_ref[...], b_ref[...])

        @pl.when((k > 0) & (k < nk - 1))
        def _():
            acc_ref[...] += dot(a_ref[...], b_ref[...])

        @pl.when(k == nk - 1)
        def _():
            o_ref[...] = (acc_ref[...] + dot(a_ref[...], b_ref[...])).astype(o_ref.dtype)

    return pl.pallas_call(
        body_whole if nk == 1 else body, name=name, grid=grid, in_specs=in_specs, out_specs=out_spec, out_shape=out_shape,
        scratch_shapes=[] if nk == 1 else [pltpu.VMEM(acc_shape, f32)],
        compiler_params=_cparams("parallel", "parallel", "arbitrary"),
    )(*operands)


def _mm_nn(name, a, b, out_dtype, tm_want=None):
    M, K = a.shape
    N = b.shape[1]
    tm, tn = _tile(M, tm_want or _TM), _tile(N, _TN_MAX)
    return _matmul(
        name, "nn", (a, b),
        [pl.BlockSpec((tm, K), lambda i, j, k: (i, 0)), pl.BlockSpec((K, tn), lambda i, j, k: (0, j))],
        jax.ShapeDtypeStruct((M, N), out_dtype), pl.BlockSpec((tm, tn), lambda i, j, k: (i, j)),
        (M // tm, N // tn, 1), (tm, tn))


def _mm_nt(name, a, b, out_dtype):
    M, K = a.shape
    N = b.shape[0]
    tm, tn = _tile(M, 2 * _TM), _tile(N, _TN_MAX)
    return _matmul(
        name, "nt", (a, b),
        [pl.BlockSpec((tm, K), lambda i, j, k: (i, 0)), pl.BlockSpec((tn, K), lambda i, j, k: (j, 0))],
        jax.ShapeDtypeStruct((M, N), out_dtype), pl.BlockSpec((tm, tn), lambda i, j, k: (i, j)),
        (M // tm, N // tn, 1), (tm, tn))


def _chunked(parts, width, at):
    out, lo = [], 0
    for p in parts:
        n = p.shape[1] // width

        def index_map(*ids, lo=lo, n=n):
            return at(ids, lambda c: jnp.clip(c - lo, 0, n - 1))

        out.append((index_map, lo, lo + n))
        lo += n
    return out


def _mm_tn_parts(name, a, parts, width, out_dtype):
    T, M = a.shape
    tk = _tile(T, _TK_T)
    nk = T // tk
    maps = _chunked(parts, width, lambda ids, c: (ids[2], c(ids[1])))
    n_tiles = maps[-1][2]

    def body(*refs):
        a_ref, b_refs, o_ref, acc_ref = refs[0], refs[1:1 + len(parts)], refs[-2], refs[-1]
        j, k = pl.program_id(1), pl.program_id(2)
        for b_ref, (_, lo, hi) in zip(b_refs, maps):
            mine = (j >= lo) & (j < hi)

            @pl.when(mine & (k == 0))
            def _(b_ref=b_ref):
                acc_ref[...] = _dot_tn(a_ref[...], b_ref[...])

            @pl.when(mine & (k > 0))
            def _(b_ref=b_ref):
                acc_ref[...] += _dot_tn(a_ref[...], b_ref[...])

        @pl.when(k == nk - 1)
        def _():
            o_ref[...] = acc_ref[...].astype(o_ref.dtype)

    return pl.pallas_call(
        body, name=name, grid=(1, n_tiles, nk),
        in_specs=[pl.BlockSpec((tk, M), lambda i, j, k: (k, 0))] + [pl.BlockSpec((tk, width), mp) for mp, _, _ in maps],
        out_specs=pl.BlockSpec((M, width), lambda i, j, k: (0, j)),
        out_shape=jax.ShapeDtypeStruct((M, n_tiles * width), out_dtype),
        scratch_shapes=[pltpu.VMEM((M, width), f32)],
        compiler_params=_cparams("parallel", "parallel", "arbitrary"),
    )(a, *parts)


def _mm_tn(name, a, b, out_dtype, tm_want=2048):
    T, M = a.shape
    N = b.shape[1]
    tm, tn, tk = _tile(M, tm_want), _tile(N, _TN_MAX), _tile(T, _TK_T)
    return _matmul(
        name, "tn", (a, b),
        [pl.BlockSpec((tk, tm), lambda i, j, k: (k, i)), pl.BlockSpec((tk, tn), lambda i, j, k: (k, j))],
        jax.ShapeDtypeStruct((M, N), out_dtype), pl.BlockSpec((tm, tn), lambda i, j, k: (i, j)),
        (M // tm, N // tn, T // tk), (tm, tn))


def _in_proj_first(x, g_mix, w_in, base, n_tiles):
    T, D = x.shape
    NI = w_in.shape[1]
    tm, tn = _tile(T, 2 * _TM), NI // _N_DEV

    def body(base_ref, x_ref, g_ref, w_ref, h_ref, o_ref):
        @pl.when(pl.program_id(1) == 0)
        def _():
            _rms_rows(x_ref, g_ref, h_ref, tm, D)

        o_ref[...] = _dot(h_ref[...], w_ref[...])

    return pl.pallas_call(
        body, name="in_proj_0",
        grid_spec=pltpu.PrefetchScalarGridSpec(
            num_scalar_prefetch=1, grid=(T // tm, n_tiles),
            in_specs=[pl.BlockSpec((tm, D), lambda i, j, b: (i, 0)), pl.BlockSpec((1, D), lambda i, j, b: (0, 0)),
                      pl.BlockSpec((D, tn), lambda i, j, b: (0, b[0] + j))],
            out_specs=[pl.BlockSpec((tm, D), lambda i, j, b: (i, 0)), pl.BlockSpec((tm, tn), lambda i, j, b: (i, b[0] + j))]),
        out_shape=[jax.ShapeDtypeStruct((T, D), bf16), jax.ShapeDtypeStruct((T, NI), f32)],
        compiler_params=_cparams("parallel", "arbitrary"),
    )(base, x, g_mix, w_in)


def _in_proj_more(name, h1, w_in, proj, base, n_tiles):
    T, D = h1.shape
    NI = w_in.shape[1]
    tm, tn = _tile(T, 2 * _TM), NI // _N_DEV

    def body(base_ref, h_ref, w_ref, proj_ref, o_ref):
        o_ref[...] = _dot(h_ref[...], w_ref[...])

    return pl.pallas_call(
        body, name=name,
        grid_spec=pltpu.PrefetchScalarGridSpec(
            num_scalar_prefetch=1, grid=(T // tm, n_tiles),
            in_specs=[pl.BlockSpec((tm, D), lambda i, j, b: (i, 0)), pl.BlockSpec((D, tn), lambda i, j, b: (0, b[0] + j)),
                      pl.BlockSpec(memory_space=pl.ANY)],
            out_specs=pl.BlockSpec((tm, tn), lambda i, j, b: (i, b[0] + j))),
        out_shape=jax.ShapeDtypeStruct((T, NI), f32),
        input_output_aliases={3: 0},
        compiler_params=_cparams("parallel", "arbitrary"),
    )(base, h1, w_in, proj)


def _window_means(ext_ref, row0, tc, gw):
    H = _POOL_HALO
    t_glob = row0 + lax.broadcasted_iota(jnp.int32, (tc, 1), 0)
    out = []
    for g, w in enumerate(_POOL_WINDOWS):
        s = ext_ref[:, g * gw:(g + 1) * gw]
        st = 1
        while st < w:
            s = s + pltpu.roll(s, st, 0)
            st *= 2
        cnt = jnp.minimum(t_glob + 1, w).astype(f32)
        out.append((s[H:, :] / cnt, ext_ref[pl.ds(H, tc), g * gw:(g + 1) * gw]))
    return out


def _pool_fwd(proj, w_pool, pool_scale):
    T = proj.shape[0]
    G, gw, _ = w_pool.shape
    PW = G * gw
    tc = _tile(T, _TC)
    H = _POOL_HALO

    def body(u_ref, w_ref, s_ref, y_ref, ext_ref):
        i = pl.program_id(0)

        @pl.when(i == 0)
        def _():
            ext_ref[pl.ds(0, H), :] = jnp.zeros((H, PW), f32)

        ext_ref[pl.ds(H, tc), :] = u_ref[...]
        for g, (m, u) in enumerate(_window_means(ext_ref, i * tc, tc, gw)):
            d = (m - u).astype(bf16)
            y = _dot(d, w_ref[g]) * s_ref[:, g * gw:(g + 1) * gw]
            y_ref[:, g * gw:(g + 1) * gw] = y.astype(bf16)
        ext_ref[pl.ds(0, H), :] = ext_ref[pl.ds(tc, H), :]

    return pl.pallas_call(
        body, name="pool_fwd", grid=(T // tc,),
        in_specs=[pl.BlockSpec((tc, PW), lambda i: (i, 0)), pl.BlockSpec((G, gw, gw), lambda i: (0, 0, 0)),
                  pl.BlockSpec((1, PW), lambda i: (0, 0))],
        out_specs=pl.BlockSpec((tc, PW), lambda i: (i, 0)),
        out_shape=jax.ShapeDtypeStruct((T, PW), bf16),
        scratch_shapes=[pltpu.VMEM((H + tc, PW), f32)],
        compiler_params=_cparams("arbitrary"),
    )(proj, w_pool, pool_scale)


def _softplus(z):
    return jnp.maximum(z, 0.0) + jnp.log1p(jnp.exp(-jnp.abs(z)))


def _causal_conv(ext_ref, cw_ref, cb_ref, n, K, cols=slice(None), r0=0):
    H = _CONV_HALO
    v = cb_ref[:, cols] + cw_ref[K - 1:K, cols] * ext_ref[pl.ds(H + r0, n), cols]
    for k in range(K - 1):
        v = v + cw_ref[k:k + 1, cols] * ext_ref[pl.ds(H + r0 - (K - 1 - k), n), cols]
    return v


_ROWS = 8
_SLAB_ROWS = 16
_SLAB_COLS = 512


def _slabs(n_rows, n_cols, reverse=False):
    cg = _tile(n_cols, _SLAB_COLS)
    rr = _tile(n_rows, _SLAB_ROWS)
    starts = range(0, n_rows, rr)
    for c0 in range(0, n_cols, cg):
        for r0 in (reversed(starts) if reverse else starts):
            yield slice(c0, c0 + cg), pl.ds(r0, rr)


def _scan_rows(a_ref, b_ref, out_ref, carry_ref, n_rows, n_cols, reverse=False):
    cg = _tile(n_cols, _SLAB_COLS)
    row = lax.broadcasted_iota(jnp.int32, (8, cg), 0)
    steps = [(8 - sh, row < 8 - sh) if reverse else (sh, row >= sh) for sh in (1, 2, 4)]
    tiles = list(range(0, n_rows, 8))
    for c0 in range(0, n_cols, cg):
        cols = slice(c0, c0 + cg)
        h_in = carry_ref[0:1, cols]
        for r0 in (reversed(tiles) if reverse else tiles):
            rows = pl.ds(r0, 8)
            a, b = a_ref[rows, cols], b_ref[rows, cols]
            for shift, inside in steps:
                b = jnp.where(inside, a * pltpu.roll(b, shift, 0) + b, b)
                a = jnp.where(inside, a * pltpu.roll(a, shift, 0), a)
            h = a * h_in + b
            out_ref[rows, cols] = h
            h_in = h[0:1, :] if reverse else h[7:8, :]
        carry_ref[0:1, cols] = h_in


def _slabs_with_sums(n_rows, n_cols, n_sums, visit, flush, reverse=False):
    cg = _tile(n_cols, _SLAB_COLS)
    rr = _tile(n_rows, _SLAB_ROWS)
    starts = list(range(0, n_rows, rr))
    for c0 in range(0, n_cols, cg):
        cols = slice(c0, c0 + cg)
        sums = [jnp.zeros((8, cg), f32) for _ in range(n_sums)]
        for r0 in (reversed(starts) if reverse else starts):
            sums = visit(cols, pl.ds(r0, rr), sums)
        flush(cols, [jnp.sum(s, axis=0, keepdims=True) for s in sums])


def _lru_fwd(proj, D, PW, conv_w, conv_b, w_a, b_a, w_i, b_i, lam):
    T = proj.shape[0]
    NB, bw, _ = w_a.shape
    K = 4
    tc = _tile(T, _TC)
    H = _CONV_HALO
    hb = D // 2
    assert PW == hb and conv_w.shape[0] == 8

    def body(u0_ref, u1_ref, g0_ref, g1_ref, cw_ref, cb_ref, wa_ref, ba_ref, wi_ref, bi_ref, lam_ref,
             y_ref, hs_ref, ext_ref, a_scr, b_scr, v_scr, hc_scr):
        c = pl.program_id(0)

        @pl.when(c == 0)
        def _():
            ext_ref[pl.ds(0, H), :] = jnp.zeros((H, D), f32)
            hc_scr[...] = jnp.zeros_like(hc_scr)

        ext_ref[pl.ds(H, tc), 0:hb] = u0_ref[...]
        ext_ref[pl.ds(H, tc), hb:D] = u1_ref[...]
        sp = _softplus(-lam_ref[...])
        for cols, rows in _slabs(tc, D):
            v_scr[rows, cols] = _causal_conv(ext_ref, cw_ref, cb_ref, rows.size, K, cols, rows.start)
        for b in range(NB):
            cols = slice(b * bw, (b + 1) * bw)
            vb = v_scr[:, cols].astype(bf16)
            a_scr[:, cols] = _dot(vb, wa_ref[b])
            b_scr[:, cols] = _dot(vb, wi_ref[b])
        for cols, rows in _slabs(tc, D):
            r = _sigmoid(a_scr[rows, cols] + ba_ref[:, cols])
            i = _sigmoid(b_scr[rows, cols] + bi_ref[:, cols])
            a = jnp.exp(-_LRU_C * r * sp[:, cols])
            a_scr[rows, cols] = a
            b_scr[rows, cols] = jnp.sqrt(1.0 - a * a) * (i * v_scr[rows, cols])

        _scan_rows(a_scr, b_scr, hs_ref, hc_scr, tc, D)
        for cols, rows in _slabs(tc, D):
            g_ref, gcols = (g0_ref, cols) if cols.start < hb else (g1_ref, slice(cols.start - hb, cols.stop - hb))
            y_ref[rows, cols] = (hs_ref[rows, cols] * _gelu(g_ref[rows, gcols])).astype(bf16)
        ext_ref[pl.ds(0, H), :] = ext_ref[pl.ds(tc, H), :]

    vec = pl.BlockSpec((1, D), lambda c: (0, 0))
    wspec = pl.BlockSpec((NB, bw, bw), lambda c: (0, 0, 0))
    return pl.pallas_call(
        body, name="lru_fwd", grid=(T // tc,),
        in_specs=[pl.BlockSpec((tc, hb), lambda c: (c, 1)), pl.BlockSpec((tc, hb), lambda c: (c, 2)),
                  pl.BlockSpec((tc, hb), lambda c: (c, 3)), pl.BlockSpec((tc, hb), lambda c: (c, 4)),
                  pl.BlockSpec((8, D), lambda c: (0, 0)), vec, wspec, vec, wspec, vec, vec],
        out_specs=[pl.BlockSpec((tc, D), lambda c: (c, 0)), pl.BlockSpec((tc, D), lambda c: (c, 0))],
        out_shape=[jax.ShapeDtypeStruct((T, D), bf16), jax.ShapeDtypeStruct((T, D), f32)],
        scratch_shapes=[pltpu.VMEM((H + tc, D), f32), pltpu.VMEM((tc, D), f32), pltpu.VMEM((tc, D), f32),
                        pltpu.VMEM((tc, D), f32), pltpu.VMEM((8, D), f32)],
        compiler_params=_cparams("arbitrary"),
    )(proj, proj, proj, proj, conv_w, conv_b, w_a, b_a, w_i, b_i, lam)


def _merge_fwd(y_pool, y_lru, w_pp, w_lp, proj, b_gate):
    T, PW = y_pool.shape
    D = y_lru.shape[1]
    tm, tn = _tile(T, _TM), _tile(D, PW)
    nj = D // tn
    off = (PW + 2 * D) // tn

    def body(yp_ref, yl_ref, wp_ref, wl_ref, l0_ref, l1_ref, b0_ref, b1_ref, p_ref, q_ref, m_ref, p_scr, q_scr):
        p_scr[...] = _dot(yp_ref[...], wp_ref[...])
        q_scr[...] = _dot(yl_ref[...], wl_ref[...])
        for cols, rows in _slabs(tm, tn):
            p, q = p_scr[rows, cols], q_scr[rows, cols]
            g0 = _sigmoid(l0_ref[rows, cols] + b0_ref[:, cols])
            g1 = _sigmoid(l1_ref[rows, cols] + b1_ref[:, cols])
            m_ref[rows, cols] = (g0 * p + g1 * q).astype(bf16)
            p_ref[rows, cols] = p.astype(bf16)
            q_ref[rows, cols] = q.astype(bf16)

    tile = pl.BlockSpec((tm, tn), lambda j, i: (i, j))
    return pl.pallas_call(
        body, name="merge_fwd", grid=(nj, T // tm),
        in_specs=[pl.BlockSpec((tm, PW), lambda j, i: (i, 0)), pl.BlockSpec((tm, D), lambda j, i: (i, 0)),
                  pl.BlockSpec((PW, tn), lambda j, i: (0, j)), pl.BlockSpec((D, tn), lambda j, i: (0, j)),
                  pl.BlockSpec((tm, tn), lambda j, i: (i, off + j)), pl.BlockSpec((tm, tn), lambda j, i: (i, off + nj + j)),
                  pl.BlockSpec((1, tn), lambda j, i: (0, j)), pl.BlockSpec((1, tn), lambda j, i: (0, nj + j))],
        out_specs=[tile, tile, tile],
        out_shape=[jax.ShapeDtypeStruct((T, D), bf16), jax.ShapeDtypeStruct((T, D), bf16), jax.ShapeDtypeStruct((T, D), bf16)],
        scratch_shapes=[pltpu.VMEM((tm, tn), f32), pltpu.VMEM((tm, tn), f32)],
        compiler_params=_cparams("parallel", "arbitrary"),
    )(y_pool, y_lru, w_pp, w_lp, proj, proj, b_gate, b_gate)


def _out_proj(merged, w_out, x, g_mlp):
    T, D = x.shape
    tm = _tile(T, _TM_SMALL)

    def body(m_ref, w_ref, x_ref, g_ref, x2_ref, h2_ref):
        x2_ref[...] = x_ref[...] + _dot(m_ref[...], w_ref[...])
        _rms_rows(x2_ref, g_ref, h2_ref, tm, D)

    row = pl.BlockSpec((tm, D), lambda i: (i, 0))
    return pl.pallas_call(
        body, name="out_proj", grid=(T // tm,),
        in_specs=[row, pl.BlockSpec((D, D), lambda i: (0, 0)), row, pl.BlockSpec((1, D), lambda i: (0, 0))],
        out_specs=[row, row],
        out_shape=[jax.ShapeDtypeStruct((T, D), f32), jax.ShapeDtypeStruct((T, D), bf16)],
        compiler_params=_cparams("parallel"),
    )(merged, w_out, x, g_mlp)


def _ffn_fwd(up, conv_w, conv_b):
    T, F2 = up.shape
    F = F2 // 2
    K = 3
    tc = _tile(T, 2 * _TC)
    cw = _tile(F, _CW)
    ns = F // cw
    H = _CONV_HALO

    def body(gp_ref, val_ref, cw_ref, cb_ref, z_ref, ext_ref):
        @pl.when(pl.program_id(1) == 0)
        def _():
            ext_ref[pl.ds(0, H), :] = jnp.zeros((H, cw), f32)

        ext_ref[pl.ds(H, tc), :] = gp_ref[...]
        for cols, rows in _slabs(tc, cw):
            c = _causal_conv(ext_ref, cw_ref, cb_ref, rows.size, K, cols, rows.start)
            z_ref[rows, cols] = (_gelu(c) * val_ref[rows, cols]).astype(bf16)
        ext_ref[pl.ds(0, H), :] = ext_ref[pl.ds(tc, H), :]

    return pl.pallas_call(
        body, name="ffn_fwd", grid=(ns, T // tc),
        in_specs=[pl.BlockSpec((tc, cw), lambda s, c: (c, s)), pl.BlockSpec((tc, cw), lambda s, c: (c, ns + s)),
                  pl.BlockSpec((8, cw), lambda s, c: (0, s)), pl.BlockSpec((1, cw), lambda s, c: (0, s))],
        out_specs=pl.BlockSpec((tc, cw), lambda s, c: (c, s)),
        out_shape=jax.ShapeDtypeStruct((T, F), bf16),
        scratch_shapes=[pltpu.VMEM((H + tc, cw), f32)],
        compiler_params=_cparams("parallel", "arbitrary"),
    )(up, up, conv_w, conv_b)


def _down_loss(z, w_down, x2, target, g_final):
    T, F = z.shape
    D = x2.shape[1]
    tm, tk = _tile(T, _TM), _tile(F, _TK_DOWN)
    nk = F // tk

    def body(z_ref, w_ref, x2_ref, t_ref, g_ref, dx_ref, dxb_ref, loss_ref, dg_ref, acc_ref):
        i, k = pl.program_id(0), pl.program_id(1)

        @pl.when(k == 0)
        def _():
            acc_ref[...] = x2_ref[...]

        @pl.when((i == 0) & (k == 0))
        def _():
            loss_ref[...] = jnp.zeros_like(loss_ref)
            dg_ref[...] = jnp.zeros_like(dg_ref)

        acc_ref[...] += _dot(z_ref[...], w_ref[...])

        @pl.when(k == nk - 1)
        def _():
            g = g_ref[...]
            sq = jnp.zeros((_ROWS, 1), f32)
            dgs = jnp.zeros((_ROWS, D), f32)
            for r0 in range(0, tm, _ROWS):
                rows = pl.ds(r0, _ROWS)
                x3 = acc_ref[rows, :]
                r = lax.rsqrt(jnp.mean(x3 * x3, axis=-1, keepdims=True) + _EPS)
                xr = x3 * r
                e = xr * g - t_ref[rows, :]
                sq = sq + jnp.sum(e * e, axis=-1, keepdims=True)
                dy = e * (1.0 / D)
                gy = dy * g
                dx = r * gy - x3 * ((r * r * r) * jnp.mean(x3 * gy, axis=-1, keepdims=True))
                dgs = dgs + dy * xr
                dx_ref[rows, :] = dx
                dxb_ref[rows, :] = dx.astype(bf16)
            loss_ref[...] += (0.5 / D) * jnp.sum(sq)
            dg_ref[...] += jnp.sum(dgs, axis=0, keepdims=True)

    row = pl.BlockSpec((tm, D), lambda i, k: (i, 0))
    vec = pl.BlockSpec((1, D), lambda i, k: (0, 0))
    return pl.pallas_call(
        body, name="down_loss", grid=(T // tm, nk),
        in_specs=[pl.BlockSpec((tm, tk), lambda i, k: (i, k)), pl.BlockSpec((tk, D), lambda i, k: (k, 0)), row, row, vec],
        out_specs=[row, row, pl.BlockSpec((8, _LANE), lambda i, k: (0, 0)), vec],
        out_shape=[jax.ShapeDtypeStruct((T, D), f32), jax.ShapeDtypeStruct((T, D), bf16),
                   jax.ShapeDtypeStruct((8, _LANE), f32), jax.ShapeDtypeStruct((1, D), f32)],
        scratch_shapes=[pltpu.VMEM((tm, D), f32)],
        compiler_params=_cparams("arbitrary", "arbitrary"),
    )(z, w_down, x2, target, g_final)


def _ffn_bwd(dz, up, conv_w, conv_b):
    T, F = dz.shape
    K = 3
    tc = _tile(T, 2 * _TC)
    cw = _tile(F, _CW)
    ns, nt = F // cw, T // tc
    H = _CONV_HALO

    def body(dz_ref, gp_ref, val_ref, gph_ref, cw_ref, cb_ref, dup_ref, dcw_ref, dcb_ref, ext_ref, dext_ref):
        j = pl.program_id(1)
        first = j == nt - 1

        @pl.when(j == 0)
        def _():
            dext_ref[pl.ds(tc, H), :] = jnp.zeros((H, cw), f32)
            dcw_ref[...] = jnp.zeros_like(dcw_ref)
            dcb_ref[...] = jnp.zeros_like(dcb_ref)

        ext_ref[pl.ds(0, H), :] = jnp.where(first, 0.0, gph_ref[...])
        ext_ref[pl.ds(H, tc), :] = gp_ref[...]
        def visit(cols, rows, sums):
            r0, n = rows.start, rows.size
            gp = ext_ref[pl.ds(H + r0, n), cols]
            c = _causal_conv(ext_ref, cw_ref, cb_ref, n, K, cols, r0)
            ge, gg = _gelu_both(c)
            dzv = dz_ref[rows, cols].astype(f32)
            dup_ref[1, rows, cols] = (dzv * ge).astype(bf16)
            dc = dzv * val_ref[rows, cols] * gg
            dext_ref[rows, cols] = dc
            dgp = cw_ref[K - 1:K, cols] * dc
            new = [None] * K + [sums[K] + _fold8(dc)]
            new[K - 1] = sums[K - 1] + _fold8(gp * dc)
            for k in range(K - 1):
                sh = dext_ref[pl.ds(r0 + K - 1 - k, n), cols]
                dgp = dgp + cw_ref[k:k + 1, cols] * sh
                new[k] = sums[k] + _fold8(gp * sh)
            dup_ref[0, rows, cols] = dgp.astype(bf16)
            return new

        def flush(cols, totals):
            for k in range(K):
                dcw_ref[k:k + 1, cols] += totals[k]
            dcb_ref[:, cols] += totals[K]

        _slabs_with_sums(tc, cw, K + 1, visit, flush, reverse=True)
        dext_ref[pl.ds(tc, H), :] = dext_ref[pl.ds(0, H), :]

    hblk = tc // H
    return pl.pallas_call(
        body, name="ffn_bwd", grid=(ns, nt),
        in_specs=[pl.BlockSpec((tc, cw), lambda s, j: (nt - 1 - j, s)),
                  pl.BlockSpec((tc, cw), lambda s, j: (nt - 1 - j, s)),
                  pl.BlockSpec((tc, cw), lambda s, j: (nt - 1 - j, ns + s)),
                  pl.BlockSpec((H, cw), lambda s, j: (jnp.maximum((nt - 1 - j) * hblk - 1, 0), s)),
                  pl.BlockSpec((8, cw), lambda s, j: (0, s)), pl.BlockSpec((1, cw), lambda s, j: (0, s))],
        out_specs=[pl.BlockSpec((2, tc, cw), lambda s, j: (0, nt - 1 - j, s)),
                   pl.BlockSpec((8, cw), lambda s, j: (0, s)), pl.BlockSpec((1, cw), lambda s, j: (0, s))],
        out_shape=[jax.ShapeDtypeStruct((2, T, F), bf16), jax.ShapeDtypeStruct((8, F), f32), jax.ShapeDtypeStruct((1, F), f32)],
        scratch_shapes=[pltpu.VMEM((H + tc, cw), f32), pltpu.VMEM((tc + H, cw), f32)],
        compiler_params=_cparams("parallel", "arbitrary"),
    )(dz, up, up, up, conv_w, conv_b)


def _norm_bwd_matmul(name, a, a_spec, nk, w, w_spec, x, g, dres):
    T, D = x.shape
    a_list, ranges = ([a], [(0, nk)]) if not isinstance(a, (list, tuple)) else (list(a), [(lo, hi) for _, lo, hi in a_spec])
    a_specs = [a_spec] if not isinstance(a, (list, tuple)) else [sp for sp, _, _ in a_spec]
    na = len(a_list)
    tm = a_specs[0].block_shape[-2]

    def body(*refs):
        a_refs = refs[:na]
        w_ref, x_ref, g_ref, dr_ref, dx_ref, dxb_ref, dg_ref, acc_ref = refs[na:]
        i, k = pl.program_id(0), pl.program_id(1)

        @pl.when((i == 0) & (k == 0))
        def _():
            dg_ref[...] = jnp.zeros_like(dg_ref)

        for a_ref, (lo, hi) in zip(a_refs, ranges):
            if lo == 0:
                @pl.when(k == 0)
                def _(a_ref=a_ref):
                    acc_ref[...] = _dot_nt(a_ref[...], w_ref[...])

            @pl.when((k >= max(lo, 1)) & (k < hi))
            def _(a_ref=a_ref):
                acc_ref[...] += _dot_nt(a_ref[...], w_ref[...])

        @pl.when(k == nk - 1)
        def _():
            g = g_ref[...]
            dgs = jnp.zeros((_ROWS, D), f32)
            for r0 in range(0, tm, _ROWS):
                rows = pl.ds(r0, _ROWS)
                x = x_ref[rows, :]
                dh = acc_ref[rows, :]
                r = lax.rsqrt(jnp.mean(x * x, axis=-1, keepdims=True) + _EPS)
                gy = dh * g
                dx = dr_ref[rows, :] + (r * gy - x * ((r * r * r) * jnp.mean(x * gy, axis=-1, keepdims=True)))
                dgs = dgs + dh * (x * r)
                dx_ref[rows, :] = dx
                dxb_ref[rows, :] = dx.astype(bf16)
            dg_ref[...] += jnp.sum(dgs, axis=0, keepdims=True)

    row = pl.BlockSpec((tm, D), lambda i, k: (i, 0))
    vec = pl.BlockSpec((1, D), lambda i, k: (0, 0))
    return pl.pallas_call(
        body, name=name, grid=(T // tm, nk),
        in_specs=a_specs + [w_spec, row, vec, row],
        out_specs=[row, row, vec],
        out_shape=[jax.ShapeDtypeStruct((T, D), f32), jax.ShapeDtypeStruct((T, D), bf16), jax.ShapeDtypeStruct((1, D), f32)],
        scratch_shapes=[pltpu.VMEM((tm, D), f32)],
        compiler_params=_cparams("arbitrary", "arbitrary"),
    )(*a_list, w, x, g, dres)


def _merge_bwd(dx2b, w_out, p, q, proj, b_gate, PW):
    T, D = p.shape
    tm, tn = _tile(T, _TM), _tile(D, PW)
    nj = D // tn
    off = (PW + 2 * D) // tn

    def body(dx_ref, w_ref, p_ref, q_ref, l0_ref, l1_ref, b0_ref, b1_ref, dp_ref, dq_ref, dl0_ref, dl1_ref, db0_ref, db1_ref,
             dm_ref):
        @pl.when(pl.program_id(1) == 0)
        def _():
            db0_ref[...] = jnp.zeros_like(db0_ref)
            db1_ref[...] = jnp.zeros_like(db1_ref)

        dm_ref[...] = _dot_nt(dx_ref[...], w_ref[...])

        def visit(cols, rows, sums):
            dm = dm_ref[rows, cols]
            g0 = _sigmoid(l0_ref[rows, cols] + b0_ref[:, cols])
            g1 = _sigmoid(l1_ref[rows, cols] + b1_ref[:, cols])
            dp_ref[rows, cols] = (g0 * dm).astype(bf16)
            dq_ref[rows, cols] = (g1 * dm).astype(bf16)
            dl0 = dm * p_ref[rows, cols].astype(f32) * (g0 * (1.0 - g0))
            dl1 = dm * q_ref[rows, cols].astype(f32) * (g1 * (1.0 - g1))
            dl0_ref[rows, cols] = dl0.astype(bf16)
            dl1_ref[rows, cols] = dl1.astype(bf16)
            return [sums[0] + _fold8(dl0), sums[1] + _fold8(dl1)]

        def flush(cols, totals):
            db0_ref[:, cols] += totals[0]
            db1_ref[:, cols] += totals[1]

        _slabs_with_sums(tm, tn, 2, visit, flush)

    tile = pl.BlockSpec((tm, tn), lambda j, i: (i, j))
    vecj = pl.BlockSpec((1, tn), lambda j, i: (0, j))
    tb = jax.ShapeDtypeStruct((T, D), bf16)
    vb = jax.ShapeDtypeStruct((1, D), f32)
    return pl.pallas_call(
        body, name="merge_bwd", grid=(nj, T // tm),
        in_specs=[pl.BlockSpec((tm, D), lambda j, i: (i, 0)), pl.BlockSpec((tn, D), lambda j, i: (j, 0)), tile, tile,
                  pl.BlockSpec((tm, tn), lambda j, i: (i, off + j)), pl.BlockSpec((tm, tn), lambda j, i: (i, off + nj + j)),
                  vecj, pl.BlockSpec((1, tn), lambda j, i: (0, nj + j))],
        out_specs=[tile, tile, tile, tile, vecj, vecj],
        out_shape=[tb, tb, tb, tb, vb, vb],
        scratch_shapes=[pltpu.VMEM((tm, tn), f32)],
        compiler_params=_cparams("parallel", "arbitrary"),
    )(dx2b, w_out, p, q, proj, proj, b_gate, b_gate)


def _lru_bwd(dy, proj, hs, D, conv_w, conv_b, w_a, b_a, w_i, b_i, lam):
    T = dy.shape[0]
    NB, bw, _ = w_a.shape
    K = 4
    tc = _tile(T, _TC)
    nt = T // tc
    H = _CONV_HALO
    hb = D // 2

    def body(dy_ref, u0_ref, u1_ref, g0_ref, g1_ref, hs_ref, uh0_ref, uh1_ref, hh_ref,
             cw_ref, cb_ref, wa_ref, ba_ref, wi_ref, bi_ref, lam_ref,
             du_ref, dg_ref, dwa_ref, dwi_ref, dcw_ref, dvec_ref,
             ext_ref, hext_ref, dext_ref, q_ref, a_scr, r_scr, i_scr, v_scr, g_scr, car_scr):
        j = pl.program_id(0)
        first = j == nt - 1

        @pl.when(j == 0)
        def _():
            dext_ref[pl.ds(tc, H), :] = jnp.zeros((H, D), f32)
            car_scr[...] = jnp.zeros_like(car_scr)
            dwa_ref[...] = jnp.zeros_like(dwa_ref)
            dwi_ref[...] = jnp.zeros_like(dwi_ref)
            dcw_ref[...] = jnp.zeros_like(dcw_ref)
            dvec_ref[...] = jnp.zeros_like(dvec_ref)

        ext_ref[pl.ds(0, H), 0:hb] = jnp.where(first, 0.0, uh0_ref[...])
        ext_ref[pl.ds(0, H), hb:D] = jnp.where(first, 0.0, uh1_ref[...])
        ext_ref[pl.ds(H, tc), 0:hb] = u0_ref[...]
        ext_ref[pl.ds(H, tc), hb:D] = u1_ref[...]
        hext_ref[pl.ds(0, H), :] = jnp.where(first, 0.0, hh_ref[...])
        hext_ref[pl.ds(H, tc), :] = hs_ref[...]
        lamv = lam_ref[...]
        sp = _softplus(-lamv)

        for cols, rows in _slabs(tc, D):
            v_scr[rows, cols] = _causal_conv(ext_ref, cw_ref, cb_ref, rows.size, K, cols, rows.start)
        for b in range(NB):
            cols = slice(b * bw, (b + 1) * bw)
            vb = v_scr[:, cols].astype(bf16)
            r_scr[:, cols] = _dot(vb, wa_ref[b])
            i_scr[:, cols] = _dot(vb, wi_ref[b])
        for cols, rows in _slabs(tc, D):
            r = _sigmoid(r_scr[rows, cols] + ba_ref[:, cols])
            r_scr[rows, cols] = r
            i_scr[rows, cols] = _sigmoid(i_scr[rows, cols] + bi_ref[:, cols])
            a = jnp.exp(-_LRU_C * r * sp[:, cols])
            a_scr[rows, cols] = a
            g_ref, gcols = (g0_ref, cols) if cols.start < hb else (g1_ref, slice(cols.start - hb, cols.stop - hb))
            ge, gg = _gelu_both(g_ref[rows, gcols])
            dyv = dy_ref[rows, cols]
            dho = dyv * ge
            g_scr[rows, cols] = dho
            q_ref[rows, cols] = a * dho
            dg_ref[rows, cols] = (dyv * hs_ref[rows, cols] * gg).astype(bf16)

        q_ref[pl.ds(tc, 1), :] = car_scr[0:1, :]
        _scan_rows(a_scr, q_ref, q_ref, car_scr, tc, D, reverse=True)

        def gates(cols, rows, sums):
            g = g_scr[rows, cols] + q_ref[pl.ds(rows.start + 1, rows.size), cols]
            v, r, i, a = v_scr[rows, cols], r_scr[rows, cols], i_scr[rows, cols], a_scr[rows, cols]
            mult = jnp.sqrt(1.0 - a * a)
            h_prev = hext_ref[pl.ds(H - 1 + rows.start, rows.size), cols]
            gm = g * mult
            dext_ref[rows, cols] = gm * i
            dlog_a = (g * h_prev - g * (i * v) * (a / mult)) * a
            dpr = dlog_a * (-_LRU_C * sp[:, cols]) * (r * (1.0 - r))
            dpi = gm * v * (i * (1.0 - i))
            r_scr[rows, cols] = dpr
            i_scr[rows, cols] = dpi
            return [sums[0] + _fold8(dpr), sums[1] + _fold8(dpi), sums[2] + _fold8(dlog_a * (-_LRU_C * r))]

        def gates_flush(cols, totals):
            dvec_ref[1:2, cols] += totals[0]
            dvec_ref[2:3, cols] += totals[1]
            dvec_ref[3:4, cols] += totals[2] * (-_sigmoid(-lamv[:, cols]))

        _slabs_with_sums(tc, D, 3, gates, gates_flush)
        for b in range(NB):
            cols = slice(b * bw, (b + 1) * bw)
            dprb, dpib, vb = r_scr[:, cols].astype(bf16), i_scr[:, cols].astype(bf16), v_scr[:, cols].astype(bf16)
            dext_ref[pl.ds(0, tc), cols] += _dot_nt(dprb, wa_ref[b]) + _dot_nt(dpib, wi_ref[b])
            dwa_ref[b] += _dot_tn(vb, dprb)
            dwi_ref[b] += _dot_tn(vb, dpib)

        def conv_t(cols, rows, sums):
            r0, n = rows.start, rows.size
            u = ext_ref[pl.ds(H + r0, n), cols]
            dv = dext_ref[rows, cols]
            du = cw_ref[K - 1:K, cols] * dv
            new = [None] * K + [sums[K] + _fold8(dv)]
            new[K - 1] = sums[K - 1] + _fold8(u * dv)
            for k in range(K - 1):
                sh = dext_ref[pl.ds(r0 + K - 1 - k, n), cols]
                du = du + cw_ref[k:k + 1, cols] * sh
                new[k] = sums[k] + _fold8(u * sh)
            du_ref[rows, cols] = du.astype(bf16)
            return new

        def conv_t_flush(cols, totals):
            for k in range(K):
                dcw_ref[k:k + 1, cols] += totals[k]
            dvec_ref[0:1, cols] += totals[K]

        _slabs_with_sums(tc, D, K + 1, conv_t, conv_t_flush)
        dext_ref[pl.ds(tc, H), :] = dext_ref[pl.ds(0, H), :]

    hblk = tc // H
    rev = lambda j: nt - 1 - j
    halo = lambda j: jnp.maximum((nt - 1 - j) * hblk - 1, 0)
    vec = pl.BlockSpec((1, D), lambda j: (0, 0))
    wspec = pl.BlockSpec((NB, bw, bw), lambda j: (0, 0, 0))
    acc8 = pl.BlockSpec((8, D), lambda j: (0, 0))
    big = pltpu.VMEM((tc, D), f32)
    return pl.pallas_call(
        body, name="lru_bwd", grid=(nt,),
        in_specs=[pl.BlockSpec((tc, D), lambda j: (rev(j), 0)),
                  pl.BlockSpec((tc, hb), lambda j: (rev(j), 1)), pl.BlockSpec((tc, hb), lambda j: (rev(j), 2)),
                  pl.BlockSpec((tc, hb), lambda j: (rev(j), 3)), pl.BlockSpec((tc, hb), lambda j: (rev(j), 4)),
                  pl.BlockSpec((tc, D), lambda j: (rev(j), 0)),
                  pl.BlockSpec((H, hb), lambda j: (halo(j), 1)), pl.BlockSpec((H, hb), lambda j: (halo(j), 2)),
                  pl.BlockSpec((H, D), lambda j: (halo(j), 0)),
                  acc8, vec, wspec, vec, wspec, vec, vec],
        out_specs=[pl.BlockSpec((tc, D), lambda j: (rev(j), 0)), pl.BlockSpec((tc, D), lambda j: (rev(j), 0)),
                   wspec, wspec, acc8, acc8],
        out_shape=[jax.ShapeDtypeStruct((T, D), bf16), jax.ShapeDtypeStruct((T, D), bf16),
                   jax.ShapeDtypeStruct((NB, bw, bw), f32), jax.ShapeDtypeStruct((NB, bw, bw), f32),
                   jax.ShapeDtypeStruct((8, D), f32), jax.ShapeDtypeStruct((8, D), f32)],
        scratch_shapes=[pltpu.VMEM((H + tc, D), f32), pltpu.VMEM((H + tc, D), f32), pltpu.VMEM((tc + H, D), f32),
                        pltpu.VMEM((tc + H, D), f32), big, big, big, big, big, pltpu.VMEM((8, D), f32)],
        compiler_params=_cparams("arbitrary"),
    )(dy, proj, proj, proj, proj, hs, proj, proj, hs, conv_w, conv_b, w_a, b_a, w_i, b_i, lam)


def _pool_bwd(dy, proj, w_pool, pool_scale):
    T, PW = dy.shape
    G, gw, _ = w_pool.shape
    tc = _tile(T, _TC)
    nt = T // tc
    H = _POOL_HALO

    def body(dy_ref, u_ref, uh_ref, w_ref, s_ref, du_ref, dw_ref, ds_ref, ext_ref, eext_ref):
        j = pl.program_id(0)
        first = j == nt - 1
        row0 = (nt - 1 - j) * tc

        @pl.when(j == 0)
        def _():
            eext_ref[pl.ds(tc, H), :] = jnp.zeros((H, PW), f32)
            dw_ref[...] = jnp.zeros_like(dw_ref)
            ds_ref[...] = jnp.zeros_like(ds_ref)

        ext_ref[pl.ds(0, H), :] = jnp.where(first, 0.0, uh_ref[...])
        ext_ref[pl.ds(H, tc), :] = u_ref[...]
        t_glob = row0 + lax.broadcasted_iota(jnp.int32, (tc, 1), 0)
        dds = []
        for g, (m, u) in enumerate(_window_means(ext_ref, row0, tc, gw)):
            cols = slice(g * gw, (g + 1) * gw)
            d = (m - u).astype(bf16)
            yraw = _dot(d, w_ref[g])
            dyv = dy_ref[:, cols]
            ds_ref[:, cols] += jnp.sum(dyv * yraw, axis=0, keepdims=True)
            dyr = (dyv * s_ref[:, cols]).astype(bf16)
            dd = _dot_nt(dyr, w_ref[g])
            dw_ref[g] += _dot_tn(d, dyr)
            cnt = jnp.minimum(t_glob + 1, _POOL_WINDOWS[g]).astype(f32)
            eext_ref[pl.ds(0, tc), cols] = dd / cnt
            dds.append(dd)
        n = tc + H
        for g, w in enumerate(_POOL_WINDOWS):
            cols = slice(g * gw, (g + 1) * gw)
            s = eext_ref[:, cols]
            st = 1
            while st < w:
                s = s + pltpu.roll(s, n - st, 0)
                st *= 2
            du_ref[:, cols] = (s[0:tc, :] - dds[g]).astype(bf16)
        eext_ref[pl.ds(tc, H), :] = eext_ref[pl.ds(0, H), :]

    hblk = tc // H
    return pl.pallas_call(
        body, name="pool_bwd", grid=(nt,),
        in_specs=[pl.BlockSpec((tc, PW), lambda j: (nt - 1 - j, 0)), pl.BlockSpec((tc, PW), lambda j: (nt - 1 - j, 0)),
                  pl.BlockSpec((H, PW), lambda j: (jnp.maximum((nt - 1 - j) * hblk - 1, 0), 0)),
                  pl.BlockSpec((G, gw, gw), lambda j: (0, 0, 0)), pl.BlockSpec((1, PW), lambda j: (0, 0))],
        out_specs=[pl.BlockSpec((tc, PW), lambda j: (nt - 1 - j, 0)), pl.BlockSpec((G, gw, gw), lambda j: (0, 0, 0)),
                   pl.BlockSpec((1, PW), lambda j: (0, 0))],
        out_shape=[jax.ShapeDtypeStruct((T, PW), bf16), jax.ShapeDtypeStruct((G, gw, gw), f32), jax.ShapeDtypeStruct((1, PW), f32)],
        scratch_shapes=[pltpu.VMEM((H + tc, PW), f32), pltpu.VMEM((tc + H, PW), f32)],
        compiler_params=_cparams("arbitrary"),
    )(dy, proj, proj, w_pool, pool_scale)


_MESH = pl.DeviceIdType.MESH
_HBM = pl.BlockSpec(memory_space=pltpu.HBM)


def _slab(ref, kind, blk, n):
    start = blk * n
    if n % _LANE == 0:
        start = pl.multiple_of(start, _LANE)
    if kind == "col":
        return ref.at[:, pl.ds(start, n)]
    if kind == "row":
        return ref.at[pl.ds(start, n), :]
    if kind == "mid":
        return ref.at[:, pl.ds(start, n), :]
    raise ValueError(kind)


def _my_place():
    x, y, c = lax.axis_index("x"), lax.axis_index("y"), lax.axis_index("c")
    return x, y, c


def _blk(px, py, pc):
    return 4 * px + 2 * py + pc


_SEM = pl.BlockSpec(memory_space=pltpu.SEMAPHORE)
_EFFECT = pltpu.SideEffectType.DATAFLOW_SIDE_EFFECTING


_ALL_PEERS = (1, 2, 3, 4, 5, 6, 7)


def _peers(x, y, c):
    return [(k, (x ^ (k >> 2), y ^ ((k >> 1) & 1), c ^ (k & 1))) for k in range(1, 8)]


class _Route:
    def __init__(self, mode, kind, size):
        self.mode, self.kind, self.size = mode, kind, size

    def src(self, ref, peer_blk):
        return ref if self.mode == "gather" else _slab(ref, self.kind, peer_blk, self.size)

    def dst(self, ref, origin_blk):
        return _slab(ref, self.kind, origin_blk, self.size) if self.mode == "gather" else ref.at[origin_blk]


def _send_start(name, srcs, lands, routes, groups):
    nt, ng = len(srcs), len(groups)

    def body(*refs):
        src_refs, land_refs = refs[:nt], refs[nt:2 * nt]
        sems = refs[2 * nt:2 * nt + 2 * ng]
        token = refs[-1]
        x, y, c = _my_place()
        me = _blk(x, y, c)
        for gi, (grp, ks) in enumerate(groups):
            for pos, t in enumerate(grp):
                for k, peer in _peers(x, y, c):
                    if k in ks:
                        s = len(ks) * pos + ks.index(k)
                        pltpu.make_async_remote_copy(
                            src_ref=routes[t].src(src_refs[t], _blk(*peer)), dst_ref=routes[t].dst(land_refs[t], me),
                            send_sem=sems[2 * gi].at[s], recv_sem=sems[2 * gi + 1].at[s],
                            device_id=peer, device_id_type=_MESH).start()
        token[...] = jnp.zeros_like(token)

    hbm = lambda a: pltpu.HBM(a.shape, a.dtype)
    out_shape = []
    for grp, ks in groups:
        out_shape += [pltpu.SemaphoreType.DMA((len(ks) * len(grp),)), pltpu.SemaphoreType.DMA((len(ks) * len(grp),))]
    out_shape += [hbm(a) for a in srcs] + [hbm(a) for a in lands] + [jax.ShapeDtypeStruct((8, _LANE), f32)]
    res = pl.pallas_call(
        body, name=name, out_shape=out_shape,
        in_specs=[_HBM] * (2 * nt),
        out_specs=[_SEM] * (2 * ng) + [_HBM] * (2 * nt) + [pl.BlockSpec(memory_space=pltpu.VMEM)],
        input_output_aliases={t: 2 * ng + t for t in range(2 * nt)},
        compiler_params=pltpu.CompilerParams(has_side_effects=_EFFECT),
    )(*[pltpu.with_memory_space_constraint(a, pltpu.HBM) for a in list(srcs) + list(lands)])
    sems = [(res[2 * g], res[2 * g + 1]) for g in range(ng)]
    return sems, res[2 * ng:2 * ng + nt], res[2 * ng + nt:2 * ng + 2 * nt], res[-1]


def _send_wait(name, srcs, lands, routes, sems, after, ks=_ALL_PEERS):
    n = len(srcs)

    def body(*refs):
        src_refs, land_refs = refs[:n], refs[n:2 * n]
        send_sems, recv_sems = refs[2 * n], refs[2 * n + 1]
        x, y, c = _my_place()
        for pos in range(n):
            for k, peer in _peers(x, y, c):
                if k in ks:
                    pb = _blk(*peer)
                    s = len(ks) * pos + ks.index(k)
                    cp = pltpu.make_async_remote_copy(
                        src_ref=routes[pos].src(src_refs[pos], pb), dst_ref=routes[pos].dst(land_refs[pos], pb),
                        send_sem=send_sems.at[s], recv_sem=recv_sems.at[s], device_id=peer, device_id_type=_MESH)
                    cp.wait_send()
                    cp.wait_recv()

    hbm = lambda a: pltpu.HBM(a.shape, a.dtype)
    res = pl.pallas_call(
        body, name=name, out_shape=[hbm(a) for a in srcs] + [hbm(a) for a in lands],
        in_specs=[_HBM] * (2 * n) + [_SEM, _SEM, pl.BlockSpec(memory_space=pl.ANY)],
        out_specs=[_HBM] * (2 * n),
        input_output_aliases={t: t for t in range(2 * n)},
        compiler_params=pltpu.CompilerParams(has_side_effects=_EFFECT),
    )(*srcs, *lands, sems[0], sems[1], after)
    return res[:n], res[n:]


def _forward_start(name, lands, routes, ks):
    n = len(lands)

    def body(*refs):
        land_refs, send_sems, recv_sems, token = refs[:n], refs[n], refs[n + 1], refs[-1]
        x, y, c = _my_place()
        for pos in range(n):
            for i, k in enumerate(ks):
                part = routes[pos].dst(land_refs[pos], _blk(x ^ (k >> 2), y ^ ((k >> 1) & 1), c))
                pltpu.make_async_remote_copy(
                    src_ref=part, dst_ref=part, send_sem=send_sems.at[len(ks) * pos + i],
                    recv_sem=recv_sems.at[len(ks) * pos + i], device_id=(x, y, 1 - c), device_id_type=_MESH).start()
        token[...] = jnp.zeros_like(token)

    hbm = lambda a: pltpu.HBM(a.shape, a.dtype)
    sem = pltpu.SemaphoreType.DMA((len(ks) * n,))
    res = pl.pallas_call(
        body, name=name, out_shape=[sem, sem] + [hbm(a) for a in lands] + [jax.ShapeDtypeStruct((8, _LANE), f32)],
        in_specs=[_HBM] * n, out_specs=[_SEM, _SEM] + [_HBM] * n + [pl.BlockSpec(memory_space=pltpu.VMEM)],
        input_output_aliases={t: 2 + t for t in range(n)},
        compiler_params=pltpu.CompilerParams(has_side_effects=_EFFECT),
    )(*[pltpu.with_memory_space_constraint(a, pltpu.HBM) for a in lands])
    return (res[0], res[1]), res[2:2 + n], res[-1]


def _forward_wait(name, lands, routes, sems, after, ks):
    n = len(lands)

    def body(*refs):
        land_refs, send_sems, recv_sems = refs[:n], refs[n], refs[n + 1]
        x, y, c = _my_place()
        for pos in range(n):
            for i, k in enumerate(ks):
                px, py = x ^ (k >> 2), y ^ ((k >> 1) & 1)
                cp = pltpu.make_async_remote_copy(
                    src_ref=routes[pos].dst(land_refs[pos], _blk(px, py, c)),
                    dst_ref=routes[pos].dst(land_refs[pos], _blk(px, py, 1 - c)),
                    send_sem=send_sems.at[len(ks) * pos + i], recv_sem=recv_sems.at[len(ks) * pos + i],
                    device_id=(x, y, 1 - c), device_id_type=_MESH)
                cp.wait_send()
                cp.wait_recv()

    hbm = lambda a: pltpu.HBM(a.shape, a.dtype)
    return pl.pallas_call(
        body, name=name, out_shape=[hbm(a) for a in lands],
        in_specs=[_HBM] * n + [_SEM, _SEM, pl.BlockSpec(memory_space=pl.ANY)], out_specs=[_HBM] * n,
        input_output_aliases={t: t for t in range(n)},
        compiler_params=pltpu.CompilerParams(has_side_effects=_EFFECT),
    )(*lands, sems[0], sems[1], after)


def _copy_own(name, src, land, route, me):
    gather = route.mode == "gather"
    shard = src.shape if gather else land.shape[1:]
    lead = () if gather else (None,)

    if route.kind == "mid":
        grid = (1,)
        at_full = lambda i, me: (0, me[0], 0)
        at_shard = lambda i, me: (0, 0, 0)
        block = tuple(shard)
    else:
        rows, width = shard
        tr = _tile(rows, 512, 16)
        grid = (rows // tr,)
        block = (tr, width)
        if route.kind == "col":
            at_full = lambda i, me: (i, me[0])
        else:
            at_full = lambda i, me: (me[0] * grid[0] + i, 0)
        at_shard = lambda i, me: (i, 0)
    if gather:
        in_map, out_map = at_shard, at_full
    else:
        in_map, out_map = at_full, (lambda i, me: (me[0], *at_shard(i, me)))

    def body(me_ref, src_ref, land_ref, out_ref):
        out_ref[...] = src_ref[...]

    return pl.pallas_call(
        body, name=name, out_shape=jax.ShapeDtypeStruct(land.shape, land.dtype),
        grid_spec=pltpu.PrefetchScalarGridSpec(
            num_scalar_prefetch=1, grid=grid,
            in_specs=[pl.BlockSpec(block, in_map), pl.BlockSpec(memory_space=pl.ANY)],
            out_specs=pl.BlockSpec(lead + block, out_map)),
        input_output_aliases={2: 0},
        compiler_params=_cparams("arbitrary"),
    )(me, src, land)


def _place_own(name, srcs, lands, routes):
    me = _blk(*_my_place()).astype(jnp.int32).reshape(1)
    return [_copy_own(f"{name}_{t}", s, l, r, me) for t, (s, l, r) in enumerate(zip(srcs, lands, routes))]


def _exchange(fulls, kinds, sizes, whole):
    arrays = list(fulls) + list(whole)
    nt, nf = len(arrays), len(fulls)

    def shard_shape(t):
        s = list(arrays[t].shape)
        if t < nf:
            s[{"col": 1, "row": 0, "mid": 1}[kinds[t]]] = sizes[t]
        return tuple(s)

    def body(*refs):
        ins, outs = refs[:nt], refs[nt:2 * nt]
        send_sems, recv_sems, local_sems = refs[2 * nt:]
        x, y, c = _my_place()
        me = _blk(x, y, c)

        def src(t, blk):
            return _slab(ins[t], kinds[t], blk, sizes[t]) if t < nf else ins[t]

        mine = [pltpu.make_async_copy(src(t, me), outs[t].at[me], local_sems.at[t]) for t in range(nt)]
        for cp in mine:
            cp.start()
        sent = []
        for t in range(nt):
            for k in range(1, 8):
                peer = (x ^ (k >> 2), y ^ ((k >> 1) & 1), c ^ (k & 1))
                pb = _blk(*peer)
                cp = pltpu.make_async_remote_copy(
                    src_ref=src(t, pb), dst_ref=outs[t].at[me],
                    send_sem=send_sems.at[7 * t + k - 1], recv_sem=recv_sems.at[7 * t + k - 1],
                    device_id=peer, device_id_type=_MESH)
                cp.start()
                sent.append((cp, t, k, pb))
        for cp, t, k, pb in sent:
            pltpu.make_async_remote_copy(
                src_ref=src(t, pb), dst_ref=outs[t].at[pb],
                send_sem=send_sems.at[7 * t + k - 1], recv_sem=recv_sems.at[7 * t + k - 1],
                device_id=(x, y, c), device_id_type=_MESH).wait_recv()
        for cp, _, _, _ in sent:
            cp.wait_send()
        for cp in mine:
            cp.wait()

    return pl.pallas_call(
        body, name="exchange_grads",
        in_specs=[_HBM] * nt, out_specs=[_HBM] * nt,
        out_shape=[jax.ShapeDtypeStruct((_N_DEV,) + shard_shape(t), arrays[t].dtype) for t in range(nt)],
        scratch_shapes=[pltpu.SemaphoreType.DMA((7 * nt,)), pltpu.SemaphoreType.DMA((7 * nt,)), pltpu.SemaphoreType.DMA((nt,))],
        compiler_params=pltpu.CompilerParams(has_side_effects=True),
    )(*arrays)


def _adamw_update(w_ref, m_ref, v_ref, g, g_ref, d_ref, nm_ref, nv_ref):
    c1 = 1.0 - _ADAM_B1 ** _ADAM_STEP
    c2 = 1.0 - _ADAM_B2 ** _ADAM_STEP
    nm = _ADAM_B1 * m_ref[...] + (1.0 - _ADAM_B1) * g
    nv = _ADAM_B2 * v_ref[...] + (1.0 - _ADAM_B2) * (g * g)
    g_ref[...] = g
    nm_ref[...] = nm
    nv_ref[...] = nv
    d_ref[...] = -_ADAM_LR * ((nm / c1) / (jnp.sqrt(nv / c2) + _ADAM_EPS) + _ADAM_WD * w_ref[...])


def _adamw_small(me, ws, ms, vs, parts):
    n = len(ws)

    def body(me_ref, *refs):
        ins, outs = refs[:4 * n], refs[4 * n:]
        for t in range(n):
            w_ref, m_ref, v_ref, p_ref = ins[4 * t:4 * t + 4]
            r = w_ref.shape[0]
            g = p_ref[0, 0:r, :]
            for s in range(1, _N_DEV):
                g = g + p_ref[s, 0:r, :]
            _adamw_update(w_ref, m_ref, v_ref, g, *outs[4 * t:4 * t + 4])

    whole = lambda a: pl.BlockSpec(a.shape, lambda i, me, nd=a.ndim: (0,) * nd)
    in_specs, operands, out_specs, out_shape = [], [], [], []
    for w, m, v, p in zip(ws, ms, vs, parts):
        c = w.shape[1]
        mine = whole(p) if p.shape[2] == c else pl.BlockSpec((_N_DEV, p.shape[1], c), lambda i, me: (0, 0, me[0]))
        in_specs += [whole(w), whole(m), whole(v), mine]
        operands += [w, m, v, p]
        out_specs += [whole(w)] * 4
        out_shape += [jax.ShapeDtypeStruct(w.shape, f32)] * 4
    return pl.pallas_call(
        body, name="adamw_small", out_shape=out_shape,
        grid_spec=pltpu.PrefetchScalarGridSpec(num_scalar_prefetch=1, grid=(1,), in_specs=in_specs, out_specs=out_specs),
        compiler_params=_cparams("arbitrary"),
    )(me, *operands)


def _adamw(name, w, m, v, parts):
    R, C = w.shape
    n = parts.shape[0]
    tr = _tile(R, 256, 8)

    def body(w_ref, m_ref, v_ref, p_ref, g_ref, d_ref, nm_ref, nv_ref):
        g = p_ref[0].astype(f32)
        for s in range(1, n):
            g = g + p_ref[s].astype(f32)
        _adamw_update(w_ref, m_ref, v_ref, g, g_ref, d_ref, nm_ref, nv_ref)

    blk = pl.BlockSpec((tr, C), lambda i: (i, 0))
    sd = jax.ShapeDtypeStruct((R, C), f32)
    return pl.pallas_call(
        body, name=name, grid=(R // tr,),
        in_specs=[blk, blk, blk, pl.BlockSpec((n, tr, C), lambda i: (0, i, 0))],
        out_specs=[blk, blk, blk, blk], out_shape=[sd, sd, sd, sd],
        compiler_params=_cparams("parallel"),
    )(w, m, v, parts)


def _pad_rows8(a):
    return jnp.pad(a, ((0, 8 - a.shape[0]), (0, 0)))


def _local_step(x, target, p, get, emit, hint):
    T, D = x.shape

    def tie(a, *tokens):
        for tok in tokens:
            if tok is not None:
                a = a + tok[0, 0]
        return a

    n_parts = _N_DEV // _W_IN_PART
    w_in, base = get("w_in", x, 0)
    h1, proj = _in_proj_first(x, p["g_mix"], w_in, base, _W_IN_PART)
    hint("w_in", proj)
    for part in range(1, n_parts):
        w_in, base = get("w_in", proj, part)
        proj = _in_proj_more(f"in_proj_{part}", h1, w_in, proj, base, _W_IN_PART)
    w_pool = get("w_pool", proj)
    PW = w_pool.shape[0] * w_pool.shape[1]
    y_pool = _pool_fwd(proj, w_pool, p["pool_scale"])
    tok = hint("w_pool_proj", y_pool)
    lru_conv_w, w_a, w_i = get("lru_conv_w", proj), get("w_a", proj), get("w_i", proj)
    y_lru, hs = _lru_fwd(proj, D, PW, lru_conv_w, tie(p["lru_conv_b"], tok), w_a, p["b_a"], w_i, p["b_i"], p["lru_lambda"])
    tok = hint("w_up", y_lru)
    w_pp, w_lp = get("w_pool_proj", y_lru), get("w_lru_proj", y_lru)
    pp, qq, merged = _merge_fwd(y_pool, y_lru, w_pp, w_lp, proj, tie(p["b_gate"], tok))
    w_out = get("w_out", y_lru)
    x2, h2 = _out_proj(merged, w_out, x, p["g_mlp"])
    w_up = get("w_up", x2)
    up = _mm_nn("up_proj", h2, w_up, f32, tm_want=2 * _TM)
    tok = hint("w_down", up)
    ffn_conv_w = get("ffn_conv_w", y_lru)
    z = _ffn_fwd(up, ffn_conv_w, tie(p["ffn_conv_b"], tok))
    w_down = get("w_down", z)
    F = w_down.shape[0]
    dx3, dx3b, loss_t, dg_final = _down_loss(z, w_down, x2, target, p["g_final"])

    gs = {"g_final": dg_final}
    tok = emit("w_down", _mm_tn("dw_down", z, dx3b, bf16, tm_want=1536))
    dz = _mm_nt("dz", dx3b, w_down, bf16)
    dup, dcw_ffn, dcb_ffn = _ffn_bwd(dz, up, ffn_conv_w, tie(p["ffn_conv_b"], tok))
    gs["ffn_conv_w"] = dcw_ffn
    gs["ffn_conv_b"] = dcb_ffn

    tm = _tile(T, _TM)
    tk = _tile(F, _TN_MAX)
    nkh = F // tk
    tkt = _tile(T, _TK_T)
    tok = emit("w_up", _matmul(
        "dw_up", "tn", (h2, dup),
        [pl.BlockSpec((tkt, D), lambda i, j, k: (k, 0)), pl.BlockSpec((None, tkt, tk), lambda i, j, k: (j // nkh, k, j % nkh))],
        jax.ShapeDtypeStruct((D, 2 * F), bf16), pl.BlockSpec((D, tk), lambda i, j, k: (0, j)),
        (1, 2 * nkh, T // tkt), (D, tk)))

    tkc = _tile(F, _TK_UP)
    nkc = F // tkc
    dx2, dx2b, gs["g_mlp"] = _norm_bwd_matmul(
        "dh2", dup, pl.BlockSpec((None, tm, tkc), lambda i, k: (k // nkc, i, k % nkc)), 2 * nkc,
        w_up, pl.BlockSpec((D, tkc), lambda i, k: (0, k)), x2, tie(p["g_mlp"], tok), dx3)

    tok = emit("w_out", _mm_tn("dw_out", merged, dx2b, bf16))
    dP, dQ, dl0, dl1, db0, db1 = _merge_bwd(dx2b, w_out, pp, qq, proj, tie(p["b_gate"], tok), PW)
    gs["b_gate"] = jnp.concatenate([db0, db1], axis=1)
    tok = emit("w_pool_proj", _mm_tn("dw_pool_proj", y_pool, dP, bf16))
    tok2 = emit("w_lru_proj", _mm_tn("dw_lru_proj", y_lru, dQ, bf16))
    dy_pool = _mm_nt("dy_pool", dP, w_pp, f32)
    dy_lru = _mm_nt("dy_lru", dQ, w_lp, f32)

    du_lru, du_gelu, dwa, dwi, dcw_lru, dvec = _lru_bwd(
        dy_lru, proj, hs, D, lru_conv_w, tie(p["lru_conv_b"], tok, tok2), w_a, p["b_a"], w_i, p["b_i"], p["lru_lambda"])
    tok = emit("w_a", dwa.astype(bf16))
    tok2 = emit("w_i", dwi.astype(bf16))
    gs["lru_conv_w"] = dcw_lru
    gs["lru_conv_b"], gs["b_a"], gs["b_i"], gs["lru_lambda"] = dvec[0:1], dvec[1:2], dvec[2:3], dvec[3:4]
    du_pool, dwp, gs["pool_scale"] = _pool_bwd(dy_pool, proj, w_pool, tie(p["pool_scale"], tok, tok2))
    tok = emit("w_pool", dwp.astype(bf16))

    dproj = [du_pool, du_lru, du_gelu, dl0, dl1]
    tok2 = emit("w_in", _mm_tn_parts("dw_in", h1, dproj, PW, bf16))
    a_specs = [(pl.BlockSpec((tm, PW), mp), lo, hi)
               for mp, lo, hi in _chunked(dproj, PW, lambda ids, own: (ids[0], own(ids[1])))]
    grad_x, _, gs["g_mix"] = _norm_bwd_matmul(
        "dh1", dproj, a_specs, a_specs[-1][2],
        w_in, pl.BlockSpec((D, PW), lambda i, k: (0, k)), x, tie(p["g_mix"], tok, tok2), dx2)
    return loss_t[0, 0], grad_x, gs


_MATRICES = {"w_in": "col", "w_pool": "mid", "w_a": "mid", "w_i": "mid", "w_pool_proj": "col", "w_lru_proj": "row",
             "w_out": "row", "w_up": "col", "w_down": "row"}
_CONVS = ("lru_conv_w", "ffn_conv_w")
_GATHER_GROUPS = (("w_pool", "lru_conv_w", "w_a", "w_i"), ("w_pool_proj", "w_lru_proj", "w_out", "ffn_conv_w"),
                  ("w_up",), ("w_down",))
_GROUP_TWO_LEVEL = (False, True, True, True)
_SAME_CORE = (2, 4, 6)
_TWO_LEVEL = (1, 2, 4, 6)
_W_IN_PARTS = ((1,), (4,), (2,), (6,))
_W_IN_PART = 2
_VECTORS = ("g_mix", "b_gate", "pool_scale", "lru_conv_b", "b_a", "b_i", "lru_lambda", "g_mlp", "ffn_conv_b", "g_final")
_WEIGHTS = ("g_mix", "w_in", "b_gate", "w_pool", "pool_scale", "lru_conv_w", "lru_conv_b", "w_a", "b_a", "w_i", "b_i",
            "lru_lambda", "w_pool_proj", "w_lru_proj", "w_out", "g_mlp", "w_up", "ffn_conv_w", "ffn_conv_b", "w_down", "g_final")


def _full_shape(shape, kind):
    s = list(shape)
    s[{"col": 1, "row": 0, "mid": 1}[kind]] *= _N_DEV
    return tuple(s)


def _step(x, target, w, m, v):
    x, target = x[0], target[0]
    me = _blk(*_my_place())

    axis = {"col": 1, "row": 0, "mid": 1}
    kind = dict(_MATRICES, **{n: "col" for n in _CONVS})
    shard = {n: w[n].astype(bf16) for n in _MATRICES}
    shard.update({n: _pad_rows8(w[n]) for n in _CONVS})
    order = ["w_in"] + [n for grp in _GATHER_GROUPS for n in grp]
    index = {n: t for t, n in enumerate(order)}
    n_parts = len(_W_IN_PARTS)
    groups = [([0], ks) for ks in _W_IN_PARTS]
    groups += [([index[n] for n in grp], _TWO_LEVEL if two else _ALL_PEERS) for grp, two in zip(_GATHER_GROUPS, _GROUP_TWO_LEVEL)]
    g_routes = [_Route("gather", kind[n], shard[n].shape[axis[kind[n]]]) for n in order]
    lands = [lax.empty(_full_shape(shard[n].shape, kind[n]), shard[n].dtype) for n in order]
    g_sems, g_srcs, g_lands, _ = _send_start("gather_start", [shard[n] for n in order], lands, g_routes, groups)
    gathered, passing = {}, {}
    x_, y_, _c = _my_place()
    w_in_state = [[g_srcs[0]], [g_lands[0]]]

    def hint(name, after):
        if name == "w_in":
            srcs, got = w_in_state
            tok = None
            for part in range(1, n_parts):
                ks = _W_IN_PARTS[part]
                srcs, got = _send_wait(f"gather_wait_w_in_{part}", srcs, got, g_routes[:1], g_sems[part], after, ks)
                sems, got, tok = _forward_start(f"gather_pass_w_in_{part}", got, g_routes[:1], ks)
                passing[name, part] = sems
            w_in_state[:] = [srcs, got]
            return tok
        gi = next(i for i, grp in enumerate(_GATHER_GROUPS) if name in grp)
        if not _GROUP_TWO_LEVEL[gi] or gi in passing or _GATHER_GROUPS[gi][0] in gathered:
            return None
        ts = groups[n_parts + gi][0]
        routes = [g_routes[t] for t in ts]
        srcs, got = _send_wait(f"gather_wait_{gi}", [g_srcs[t] for t in ts], [g_lands[t] for t in ts], routes,
                               g_sems[n_parts + gi], after, _TWO_LEVEL)
        got = _place_own(f"gather_own_{gi}", srcs, got, routes)
        sems, got, tok = _forward_start(f"gather_pass_{gi}", got, routes, _SAME_CORE)
        passing[gi] = (sems, got, routes)
        return tok

    def get(name, after, part=None):
        if name == "w_in":
            ks = _W_IN_PARTS[part]
            if part == 0:
                srcs, got = _send_wait("gather_wait_w_in_0", *w_in_state, g_routes[:1], g_sems[0], after, ks)
                got = _place_own("gather_own_w_in", srcs, got, g_routes[:1])
                w_in_state[:] = [srcs, got]
            else:
                if (name, part) not in passing:
                    hint(name, after)
                got = _forward_wait(f"gather_got_w_in_{part}", w_in_state[1], g_routes[:1], passing[name, part], after, ks)
                w_in_state[1] = got
            k = ks[-1]
            base = 4 * (x_ ^ (k >> 2)) + 2 * (y_ ^ ((k >> 1) & 1))
            return got[0], base.astype(jnp.int32).reshape(1)
        if name not in gathered:
            gi = next(i for i, grp in enumerate(_GATHER_GROUPS) if name in grp)
            if _GROUP_TWO_LEVEL[gi]:
                hint(name, after)
                sems, got, routes = passing[gi]
                full = _forward_wait(f"gather_got_{gi}", got, routes, sems, after, _SAME_CORE)
            else:
                ts = groups[n_parts + gi][0]
                routes = [g_routes[t] for t in ts]
                srcs, got = _send_wait(f"gather_wait_{gi}", [g_srcs[t] for t in ts], [g_lands[t] for t in ts], routes,
                                       g_sems[n_parts + gi], after)
                full = _place_own(f"gather_own_{gi}", srcs, got, routes)
            gathered.update(zip(_GATHER_GROUPS[gi], full))
        return gathered[name]

    sent = {}

    def emit(name, grad):
        k = _MATRICES[name]
        size = w[name].shape[axis[k]]
        route = _Route("scatter", k, size)
        shp = list(grad.shape)
        shp[axis[k]] = size
        land = lax.empty((_N_DEV, *shp), grad.dtype)
        sems, srcs, lnds, token = _send_start("grad_start_" + name, [grad], [land], [route], [([0], _ALL_PEERS)])
        sent[name] = (srcs, lnds, [route], sems[0])
        return token

    p = {n: w[n].reshape(1, -1) for n in _VECTORS}
    loss_t, grad_x, gs = _local_step(x, target, p, get, emit, hint)
    loss = lax.psum(loss_t, ("x", "y", "c"))

    small_names = list(_VECTORS) + list(_CONVS)
    small_parts = _exchange([], [], [], [gs[n] for n in small_names])

    out = {}
    mats = list(_MATRICES)
    for n in mats:
        srcs, lnds, routes, sems = sent[n]
        srcs, got = _send_wait("grad_wait_" + n, srcs, lnds, routes, sems, grad_x)
        parts = _place_own("grad_own_" + n, srcs, got, routes)[0]
        shp = w[n].shape
        r2 = (-1, shp[-1])
        res = _adamw("adamw_" + n, w[n].reshape(r2), m[n].reshape(r2), v[n].reshape(r2),
                     parts.reshape((_N_DEV,) + w[n].reshape(r2).shape))
        out[n] = [a.reshape(shp) for a in res]
    two_d = lambda a: a.reshape(-1, a.shape[-1])
    res = _adamw_small(me.astype(jnp.int32).reshape(1), [two_d(w[n]) for n in small_names], [two_d(m[n]) for n in small_names],
                       [two_d(v[n]) for n in small_names], small_parts)
    for t, n in enumerate(small_names):
        out[n] = [a.reshape(w[n].shape) for a in res[4 * t:4 * t + 4]]
    return loss, grad_x[None], out


def kernel(x, g_mix, w_in, b_gate, w_pool, pool_scale, lru_conv_w, lru_conv_b, w_a, b_a, w_i, b_i, lru_lambda, w_pool_proj, w_lru_proj, w_out, g_mlp, w_up, ffn_conv_w, ffn_conv_b, w_down, g_final, loss_target, m_g_mix, m_w_in, m_b_gate, m_w_pool, m_pool_scale, m_lru_conv_w, m_lru_conv_b, m_w_a, m_b_a, m_w_i, m_b_i, m_lru_lambda, m_w_pool_proj, m_w_lru_proj, m_w_out, m_g_mlp, m_w_up, m_ffn_conv_w, m_ffn_conv_b, m_w_down, m_g_final, v_g_mix, v_w_in, v_b_gate, v_w_pool, v_pool_scale, v_lru_conv_w, v_lru_conv_b, v_w_a, v_b_a, v_w_i, v_b_i, v_lru_lambda, v_w_pool_proj, v_w_lru_proj, v_w_out, v_g_mlp, v_w_up, v_ffn_conv_w, v_ffn_conv_b, v_w_down, v_g_final):
    given = dict(locals())
    orig = {n: given[n].shape for n in _WEIGHTS}

    def squeeze(a, n):
        return a if n == "g_final" else a[0]

    w = {n: squeeze(given[n], n) for n in _WEIGHTS}
    m = {n: squeeze(given["m_" + n], n) for n in _WEIGHTS}
    v = {n: squeeze(given["v_" + n], n) for n in _WEIGHTS}
    for d in (w, m, v):
        d["g_final"] = d["g_final"].reshape(1, -1)
    loss, grad_x, out = _step(x, loss_target, w, m, v)
    res = [loss, grad_x]
    for k in range(4):
        res += [out[n][k].reshape(orig[n]) for n in _WEIGHTS]
    return tuple(res)
```

```python
import functools

import jax
import jax.numpy as jnp
from jax import lax
from jax.experimental import pallas as pl
from jax.experimental.pallas import tpu as pltpu

f32 = jnp.float32
bf16 = jnp.bfloat16

_EPS = 1e-6
_LRU_C = 8.0
_POOL_WINDOWS = (2, 4, 8, 16)
_POOL_HALO = 16
_CONV_HALO = 8
_GELU_C0 = 0.7978845608028654
_GELU_C1 = 0.044715
_ADAM_LR, _ADAM_B1, _ADAM_B2, _ADAM_EPS, _ADAM_WD, _ADAM_STEP = 0.001, 0.9, 0.999, 1e-08, 0.01, 10
_N_DEV = 8
_LANE = 128
_VMEM_LIMIT = 60 * 1024 * 1024

_TM = 512
_TM_SMALL = 256
_TC = 256
_TK_T = 1024
_TN_MAX = 1536
_CW = 1024
_TK_DOWN = 1536
_TK_UP = 2048


def _cparams(*sem):
    return pltpu.CompilerParams(dimension_semantics=tuple(sem), vmem_limit_bytes=_VMEM_LIMIT)


def _tile(n, want, mult=1):
    if n <= want:
        return n
    t = want - want % mult
    while n % t:
        t -= mult
    return t


def _gelu(x):
    t = jnp.tanh(x * (_GELU_C0 + (_GELU_C0 * _GELU_C1) * (x * x)))
    return x * (0.5 + 0.5 * t)


def _gelu_both(x):
    x2 = x * x
    t = jnp.tanh(x * (_GELU_C0 + (_GELU_C0 * _GELU_C1) * x2))
    h = 0.5 + 0.5 * t
    return x * h, h + (x * (1.0 - t * t)) * (0.5 * _GELU_C0 + (1.5 * _GELU_C0 * _GELU_C1) * x2)


def _fold8(x):
    out = x[0:8]
    for r in range(8, x.shape[0], 8):
        out = out + x[r:r + 8]
    return out


def _sigmoid(x):
    return jax.nn.sigmoid(x)


def _dot(a, b):
    return jnp.dot(a, b, preferred_element_type=f32)


def _dot_nt(a, b):
    return lax.dot_general(a, b, (((1,), (1,)), ((), ())), preferred_element_type=f32)


def _dot_tn(a, b):
    return lax.dot_general(a, b, (((0,), (0,)), ((), ())), preferred_element_type=f32)


def _rms_rows(x_ref, g_ref, h_ref, n_rows, n_cols):
    rr = _tile(n_rows, _SLAB_ROWS)
    cg = _tile(n_cols, _SLAB_COLS)
    for r0 in range(0, n_rows, rr):
        rows = pl.ds(r0, rr)
        x = x_ref[rows, :]
        r = lax.rsqrt(jnp.mean(x * x, axis=-1, keepdims=True) + _EPS)
        for c0 in range(0, n_cols, cg):
            cols = slice(c0, c0 + cg)
            h_ref[rows, cols] = (x_ref[rows, cols] * r * g_ref[:, cols]).astype(bf16)


def _matmul(name, mode, operands, in_specs, out_shape, out_spec, grid, acc_shape):
    dot = {"nn": _dot, "nt": _dot_nt, "tn": _dot_tn}[mode]
    nk = grid[2]

    def body_whole(a_ref, b_ref, o_ref):
        o_ref[...] = dot(a_ref[...], b_ref[...]).astype(o_ref.dtype)

    def body(a_ref, b_ref, o_ref, acc_ref):
        k = pl.program_id(2)

        @pl.when(k == 0)
        def _():
            acc_ref[...] = dot(a_ref[...], b_ref[...])

        @pl.when((k > 0) & (k < nk - 1))
        def _():
            acc_ref[...] += dot(a_ref[...], b_ref[...])

        @pl.when(k == nk - 1)
        def _():
            o_ref[...] = (acc_ref[...] + dot(a_ref[...], b_ref[...])).astype(o_ref.dtype)

    return pl.pallas_call(
        body_whole if nk == 1 else body, name=name, grid=grid, in_specs=in_specs, out_specs=out_spec, out_shape=out_shape,
        scratch_shapes=[] if nk == 1 else [pltpu.VMEM(acc_shape, f32)],
        compiler_params=_cparams("parallel", "parallel", "arbitrary"),
    )(*operands)


def _mm_nn(name, a, b, out_dtype, tm_want=None):
    M, K = a.shape
    N = b.shape[1]
    tm, tn = _tile(M, tm_want or _TM), _tile(N, _TN_MAX)
    return _matmul(
        name, "nn", (a, b),
        [pl.BlockSpec((tm, K), lambda i, j, k: (i, 0)), pl.BlockSpec((K, tn), lambda i, j, k: (0, j))],
        jax.ShapeDtypeStruct((M, N), out_dtype), pl.BlockSpec((tm, tn), lambda i, j, k: (i, j)),
        (M // tm, N // tn, 1), (tm, tn))


def _mm_nt(name, a, b, out_dtype):
    M, K = a.shape
    N = b.shape[0]
    tm, tn = _tile(M, 2 * _TM), _tile(N, _TN_MAX)
    return _matmul(
        name, "nt", (a, b),
        [pl.BlockSpec((tm, K), lambda i, j, k: (i, 0)), pl.BlockSpec((tn, K), lambda i, j, k: (j, 0))],
        jax.ShapeDtypeStruct((M, N), out_dtype), pl.BlockSpec((tm, tn), lambda i, j, k: (i, j)),
        (M // tm, N // tn, 1), (tm, tn))


def _mm_tn(name, a, b, out_dtype, tm_want=2048):
    T, M = a.shape
    N = b.shape[1]
    tm, tn, tk = _tile(M, tm_want), _tile(N, _TN_MAX), _tile(T, _TK_T)
    return _matmul(
        name, "tn", (a, b),
        [pl.BlockSpec((tk, tm), lambda i, j, k: (k, i)), pl.BlockSpec((tk, tn), lambda i, j, k: (k, j))],
        jax.ShapeDtypeStruct((M, N), out_dtype), pl.BlockSpec((tm, tn), lambda i, j, k: (i, j)),
        (M // tm, N // tn, T // tk), (tm, tn))


def _in_proj_first(x, g_mix, w_in, base, n_tiles):
    T, D = x.shape
    NI = w_in.shape[1]
    tm, tn = _tile(T, 2 * _TM), NI // _N_DEV

    def body(base_ref, x_ref, g_ref, w_ref, h_ref, o_ref):
        @pl.when(pl.program_id(1) == 0)
        def _():
            _rms_rows(x_ref, g_ref, h_ref, tm, D)

        o_ref[...] = _dot(h_ref[...], w_ref[...])

    return pl.pallas_call(
        body, name="in_proj_0",
        grid_spec=pltpu.PrefetchScalarGridSpec(
            num_scalar_prefetch=1, grid=(T // tm, n_tiles),
            in_specs=[pl.BlockSpec((tm, D), lambda i, j, b: (i, 0)), pl.BlockSpec((1, D), lambda i, j, b: (0, 0)),
                      pl.BlockSpec((D, tn), lambda i, j, b: (0, b[0] + j))],
            out_specs=[pl.BlockSpec((tm, D), lambda i, j, b: (i, 0)), pl.BlockSpec((tm, tn), lambda i, j, b: (i, b[0] + j))]),
        out_shape=[jax.ShapeDtypeStruct((T, D), bf16), jax.ShapeDtypeStruct((T, NI), f32)],
        compiler_params=_cparams("parallel", "arbitrary"),
    )(base, x, g_mix, w_in)


def _in_proj_more(name, h1, w_in, proj, base, n_tiles):
    T, D = h1.shape
    NI = w_in.shape[1]
    tm, tn = _tile(T, 2 * _TM), NI // _N_DEV

    def body(base_ref, h_ref, w_ref, proj_ref, o_ref):
        o_ref[...] = _dot(h_ref[...], w_ref[...])

    return pl.pallas_call(
        body, name=name,
        grid_spec=pltpu.PrefetchScalarGridSpec(
            num_scalar_prefetch=1, grid=(T // tm, n_tiles),
            in_specs=[pl.BlockSpec((tm, D), lambda i, j, b: (i, 0)), pl.BlockSpec((D, tn), lambda i, j, b: (0, b[0] + j)),
                      pl.BlockSpec(memory_space=pl.ANY)],
            out_specs=pl.BlockSpec((tm, tn), lambda i, j, b: (i, b[0] + j))),
        out_shape=jax.ShapeDtypeStruct((T, NI), f32),
        input_output_aliases={3: 0},
        compiler_params=_cparams("parallel", "arbitrary"),
    )(base, h1, w_in, proj)


def _window_means(ext_ref, row0, tc, gw):
    H = _POOL_HALO
    t_glob = row0 + lax.broadcasted_iota(jnp.int32, (tc, 1), 0)
    out = []
    for g, w in enumerate(_POOL_WINDOWS):
        s = ext_ref[:, g * gw:(g + 1) * gw]
        st = 1
        while st < w:
            s = s + pltpu.roll(s, st, 0)
            st *= 2
        cnt = jnp.minimum(t_glob + 1, w).astype(f32)
        out.append((s[H:, :] / cnt, ext_ref[pl.ds(H, tc), g * gw:(g + 1) * gw]))
    return out


def _pool_fwd(proj, w_pool, pool_scale):
    T = proj.shape[0]
    G, gw, _ = w_pool.shape
    PW = G * gw
    tc = _tile(T, _TC)
    H = _POOL_HALO

    def body(u_ref, w_ref, s_ref, y_ref, ext_ref):
        i = pl.program_id(0)

        @pl.when(i == 0)
        def _():
            ext_ref[pl.ds(0, H), :] = jnp.zeros((H, PW), f32)

        ext_ref[pl.ds(H, tc), :] = u_ref[...]
        for g, (m, u) in enumerate(_window_means(ext_ref, i * tc, tc, gw)):
            d = (m - u).astype(bf16)
            y = _dot(d, w_ref[g]) * s_ref[:, g * gw:(g + 1) * gw]
            y_ref[:, g * gw:(g + 1) * gw] = y.astype(bf16)
        ext_ref[pl.ds(0, H), :] = ext_ref[pl.ds(tc, H), :]

    return pl.pallas_call(
        body, name="pool_fwd", grid=(T // tc,),
        in_specs=[pl.BlockSpec((tc, PW), lambda i: (i, 0)), pl.BlockSpec((G, gw, gw), lambda i: (0, 0, 0)),
                  pl.BlockSpec((1, PW), lambda i: (0, 0))],
        out_specs=pl.BlockSpec((tc, PW), lambda i: (i, 0)),
        out_shape=jax.ShapeDtypeStruct((T, PW), bf16),
        scratch_shapes=[pltpu.VMEM((H + tc, PW), f32)],
        compiler_params=_cparams("arbitrary"),
    )(proj, w_pool, pool_scale)


def _softplus(z):
    return jnp.maximum(z, 0.0) + jnp.log1p(jnp.exp(-jnp.abs(z)))


def _causal_conv(ext_ref, cw_ref, cb_ref, n, K, cols=slice(None), r0=0):
    H = _CONV_HALO
    v = cb_ref[:, cols] + cw_ref[K - 1:K, cols] * ext_ref[pl.ds(H + r0, n), cols]
    for k in range(K - 1):
        v = v + cw_ref[k:k + 1, cols] * ext_ref[pl.ds(H + r0 - (K - 1 - k), n), cols]
    return v


_ROWS = 8
_SLAB_ROWS = 16
_SLAB_COLS = 512


def _slabs(n_rows, n_cols, reverse=False):
    cg = _tile(n_cols, _SLAB_COLS)
    rr = _tile(n_rows, _SLAB_ROWS)
    starts = range(0, n_rows, rr)
    for c0 in range(0, n_cols, cg):
        for r0 in (reversed(starts) if reverse else starts):
            yield slice(c0, c0 + cg), pl.ds(r0, rr)


def _scan_rows(a_ref, b_ref, out_ref, carry_ref, n_rows, n_cols, reverse=False):
    cg = _tile(n_cols, _SLAB_COLS)
    row = lax.broadcasted_iota(jnp.int32, (8, cg), 0)
    steps = [(8 - sh, row < 8 - sh) if reverse else (sh, row >= sh) for sh in (1, 2, 4)]
    tiles = list(range(0, n_rows, 8))
    for c0 in range(0, n_cols, cg):
        cols = slice(c0, c0 + cg)
        h_in = carry_ref[0:1, cols]
        for r0 in (reversed(tiles) if reverse else tiles):
            rows = pl.ds(r0, 8)
            a, b = a_ref[rows, cols], b_ref[rows, cols]
            for shift, inside in steps:
                b = jnp.where(inside, a * pltpu.roll(b, shift, 0) + b, b)
                a = jnp.where(inside, a * pltpu.roll(a, shift, 0), a)
            h = a * h_in + b
            out_ref[rows, cols] = h
            h_in = h[0:1, :] if reverse else h[7:8, :]
        carry_ref[0:1, cols] = h_in


def _slabs_with_sums(n_rows, n_cols, n_sums, visit, flush, reverse=False):
    cg = _tile(n_cols, _SLAB_COLS)
    rr = _tile(n_rows, _SLAB_ROWS)
    starts = list(range(0, n_rows, rr))
    for c0 in range(0, n_cols, cg):
        cols = slice(c0, c0 + cg)
        sums = [jnp.zeros((8, cg), f32) for _ in range(n_sums)]
        for r0 in (reversed(starts) if reverse else starts):
            sums = visit(cols, pl.ds(r0, rr), sums)
        flush(cols, [jnp.sum(s, axis=0, keepdims=True) for s in sums])


def _lru_fwd(proj, D, PW, conv_w, conv_b, w_a, b_a, w_i, b_i, lam):
    T = proj.shape[0]
    NB, bw, _ = w_a.shape
    K = 4
    tc = _tile(T, _TC)
    H = _CONV_HALO
    hb = D // 2
    assert PW == hb and conv_w.shape[0] == 8

    def body(u0_ref, u1_ref, g0_ref, g1_ref, cw_ref, cb_ref, wa_ref, ba_ref, wi_ref, bi_ref, lam_ref,
             y_ref, hs_ref, ext_ref, a_scr, b_scr, v_scr, hc_scr):
        c = pl.program_id(0)

        @pl.when(c == 0)
        def _():
            ext_ref[pl.ds(0, H), :] = jnp.zeros((H, D), f32)
            hc_scr[...] = jnp.zeros_like(hc_scr)

        ext_ref[pl.ds(H, tc), 0:hb] = u0_ref[...]
        ext_ref[pl.ds(H, tc), hb:D] = u1_ref[...]
        sp = _softplus(-lam_ref[...])
        for cols, rows in _slabs(tc, D):
            v_scr[rows, cols] = _causal_conv(ext_ref, cw_ref, cb_ref, rows.size, K, cols, rows.start)
        for b in range(NB):
            cols = slice(b * bw, (b + 1) * bw)
            vb = v_scr[:, cols].astype(bf16)
            a_scr[:, cols] = _dot(vb, wa_ref[b])
            b_scr[:, cols] = _dot(vb, wi_ref[b])
        for cols, rows in _slabs(tc, D):
            r = _sigmoid(a_scr[rows, cols] + ba_ref[:, cols])
            i = _sigmoid(b_scr[rows, cols] + bi_ref[:, cols])
            a = jnp.exp(-_LRU_C * r * sp[:, cols])
            a_scr[rows, cols] = a
            b_scr[rows, cols] = jnp.sqrt(1.0 - a * a) * (i * v_scr[rows, cols])

        _scan_rows(a_scr, b_scr, hs_ref, hc_scr, tc, D)
        for cols, rows in _slabs(tc, D):
            g_ref, gcols = (g0_ref, cols) if cols.start < hb else (g1_ref, slice(cols.start - hb, cols.stop - hb))
            y_ref[rows, cols] = (hs_ref[rows, cols] * _gelu(g_ref[rows, gcols])).astype(bf16)
        ext_ref[pl.ds(0, H), :] = ext_ref[pl.ds(tc, H), :]

    vec = pl.BlockSpec((1, D), lambda c: (0, 0))
    wspec = pl.BlockSpec((NB, bw, bw), lambda c: (0, 0, 0))
    return pl.pallas_call(
        body, name="lru_fwd", grid=(T // tc,),
        in_specs=[pl.BlockSpec((tc, hb), lambda c: (c, 1)), pl.BlockSpec((tc, hb), lambda c: (c, 2)),
                  pl.BlockSpec((tc, hb), lambda c: (c, 3)), pl.BlockSpec((tc, hb), lambda c: (c, 4)),
                  pl.BlockSpec((8, D), lambda c: (0, 0)), vec, wspec, vec, wspec, vec, vec],
        out_specs=[pl.BlockSpec((tc, D), lambda c: (c, 0)), pl.BlockSpec((tc, D), lambda c: (c, 0))],
        out_shape=[jax.ShapeDtypeStruct((T, D), bf16), jax.ShapeDtypeStruct((T, D), f32)],
        scratch_shapes=[pltpu.VMEM((H + tc, D), f32), pltpu.VMEM((tc, D), f32), pltpu.VMEM((tc, D), f32),
                        pltpu.VMEM((tc, D), f32), pltpu.VMEM((8, D), f32)],
        compiler_params=_cparams("arbitrary"),
    )(proj, proj, proj, proj, conv_w, conv_b, w_a, b_a, w_i, b_i, lam)


def _merge_fwd(y_pool, y_lru, w_pp, w_lp, proj, b_gate):
    T, PW = y_pool.shape
    D = y_lru.shape[1]
    tm, tn = _tile(T, _TM), _tile(D, PW)
    nj = D // tn
    off = (PW + 2 * D) // tn

    def body(yp_ref, yl_ref, wp_ref, wl_ref, l0_ref, l1_ref, b0_ref, b1_ref, p_ref, q_ref, m_ref, p_scr, q_scr):
        p_scr[...] = _dot(yp_ref[...], wp_ref[...])
        q_scr[...] = _dot(yl_ref[...], wl_ref[...])
        for cols, rows in _slabs(tm, tn):
            p, q = p_scr[rows, cols], q_scr[rows, cols]
            g0 = _sigmoid(l0_ref[rows, cols] + b0_ref[:, cols])
            g1 = _sigmoid(l1_ref[rows, cols] + b1_ref[:, cols])
            m_ref[rows, cols] = (g0 * p + g1 * q).astype(bf16)
            p_ref[rows, cols] = p.astype(bf16)
            q_ref[rows, cols] = q.astype(bf16)

    tile = pl.BlockSpec((tm, tn), lambda j, i: (i, j))
    return pl.pallas_call(
        body, name="merge_fwd", grid=(nj, T // tm),
        in_specs=[pl.BlockSpec((tm, PW), lambda j, i: (i, 0)), pl.BlockSpec((tm, D), lambda j, i: (i, 0)),
                  pl.BlockSpec((PW, tn), lambda j, i: (0, j)), pl.BlockSpec((D, tn), lambda j, i: (0, j)),
                  pl.BlockSpec((tm, tn), lambda j, i: (i, off + j)), pl.BlockSpec((tm, tn), lambda j, i: (i, off + nj + j)),
                  pl.BlockSpec((1, tn), lambda j, i: (0, j)), pl.BlockSpec((1, tn), lambda j, i: (0, nj + j))],
        out_specs=[tile, tile, tile],
        out_shape=[jax.ShapeDtypeStruct((T, D), bf16), jax.ShapeDtypeStruct((T, D), bf16), jax.ShapeDtypeStruct((T, D), bf16)],
        scratch_shapes=[pltpu.VMEM((tm, tn), f32), pltpu.VMEM((tm, tn), f32)],
        compiler_params=_cparams("parallel", "arbitrary"),
    )(y_pool, y_lru, w_pp, w_lp, proj, proj, b_gate, b_gate)


def _out_proj(merged, w_out, x, g_mlp):
    T, D = x.shape
    tm = _tile(T, _TM_SMALL)

    def body(m_ref, w_ref, x_ref, g_ref, x2_ref, h2_ref):
        x2_ref[...] = x_ref[...] + _dot(m_ref[...], w_ref[...])
        _rms_rows(x2_ref, g_ref, h2_ref, tm, D)

    row = pl.BlockSpec((tm, D), lambda i: (i, 0))
    return pl.pallas_call(
        body, name="out_proj", grid=(T // tm,),
        in_specs=[row, pl.BlockSpec((D, D), lambda i: (0, 0)), row, pl.BlockSpec((1, D), lambda i: (0, 0))],
        out_specs=[row, row],
        out_shape=[jax.ShapeDtypeStruct((T, D), f32), jax.ShapeDtypeStruct((T, D), bf16)],
        compiler_params=_cparams("parallel"),
    )(merged, w_out, x, g_mlp)


def _ffn_fwd(up, conv_w, conv_b):
    T, F2 = up.shape
    F = F2 // 2
    K = 3
    tc = _tile(T, 2 * _TC)
    cw = _tile(F, _CW)
    ns = F // cw
    H = _CONV_HALO

    def body(gp_ref, val_ref, cw_ref, cb_ref, z_ref, ext_ref):
        @pl.when(pl.program_id(1) == 0)
        def _():
            ext_ref[pl.ds(0, H), :] = jnp.zeros((H, cw), f32)

        ext_ref[pl.ds(H, tc), :] = gp_ref[...]
        for cols, rows in _slabs(tc, cw):
            c = _causal_conv(ext_ref, cw_ref, cb_ref, rows.size, K, cols, rows.start)
            z_ref[rows, cols] = (_gelu(c) * val_ref[rows, cols]).astype(bf16)
        ext_ref[pl.ds(0, H), :] = ext_ref[pl.ds(tc, H), :]

    return pl.pallas_call(
        body, name="ffn_fwd", grid=(ns, T // tc),
        in_specs=[pl.BlockSpec((tc, cw), lambda s, c: (c, s)), pl.BlockSpec((tc, cw), lambda s, c: (c, ns + s)),
                  pl.BlockSpec((8, cw), lambda s, c: (0, s)), pl.BlockSpec((1, cw), lambda s, c: (0, s))],
        out_specs=pl.BlockSpec((tc, cw), lambda s, c: (c, s)),
        out_shape=jax.ShapeDtypeStruct((T, F), bf16),
        scratch_shapes=[pltpu.VMEM((H + tc, cw), f32)],
        compiler_params=_cparams("parallel", "arbitrary"),
    )(up, up, conv_w, conv_b)


def _down_loss(z, w_down, x2, target, g_final):
    T, F = z.shape
    D = x2.shape[1]
    tm, tk = _tile(T, _TM), _tile(F, _TK_DOWN)
    nk = F // tk

    def body(z_ref, w_ref, x2_ref, t_ref, g_ref, dx_ref, dxb_ref, loss_ref, dg_ref, acc_ref):
        i, k = pl.program_id(0), pl.program_id(1)

        @pl.when(k == 0)
        def _():
            acc_ref[...] = x2_ref[...]

        @pl.when((i == 0) & (k == 0))
        def _():
            loss_ref[...] = jnp.zeros_like(loss_ref)
            dg_ref[...] = jnp.zeros_like(dg_ref)

        acc_ref[...] += _dot(z_ref[...], w_ref[...])

        @pl.when(k == nk - 1)
        def _():
            g = g_ref[...]
            sq = jnp.zeros((_ROWS, 1), f32)
            dgs = jnp.zeros((_ROWS, D), f32)
            for r0 in range(0, tm, _ROWS):
                rows = pl.ds(r0, _ROWS)
                x3 = acc_ref[rows, :]
                r = lax.rsqrt(jnp.mean(x3 * x3, axis=-1, keepdims=True) + _EPS)
                xr = x3 * r
                e = xr * g - t_ref[rows, :]
                sq = sq + jnp.sum(e * e, axis=-1, keepdims=True)
                dy = e * (1.0 / D)
                gy = dy * g
                dx = r * gy - x3 * ((r * r * r) * jnp.mean(x3 * gy, axis=-1, keepdims=True))
                dgs = dgs + dy * xr
                dx_ref[rows, :] = dx
                dxb_ref[rows, :] = dx.astype(bf16)
            loss_ref[...] += (0.5 / D) * jnp.sum(sq)
            dg_ref[...] += jnp.sum(dgs, axis=0, keepdims=True)

    row = pl.BlockSpec((tm, D), lambda i, k: (i, 0))
    vec = pl.BlockSpec((1, D), lambda i, k: (0, 0))
    return pl.pallas_call(
        body, name="down_loss", grid=(T // tm, nk),
        in_specs=[pl.BlockSpec((tm, tk), lambda i, k: (i, k)), pl.BlockSpec((tk, D), lambda i, k: (k, 0)), row, row, vec],
        out_specs=[row, row, pl.BlockSpec((8, _LANE), lambda i, k: (0, 0)), vec],
        out_shape=[jax.ShapeDtypeStruct((T, D), f32), jax.ShapeDtypeStruct((T, D), bf16),
                   jax.ShapeDtypeStruct((8, _LANE), f32), jax.ShapeDtypeStruct((1, D), f32)],
        scratch_shapes=[pltpu.VMEM((tm, D), f32)],
        compiler_params=_cparams("arbitrary", "arbitrary"),
    )(z, w_down, x2, target, g_final)


def _ffn_bwd(dz, up, conv_w, conv_b):
    T, F = dz.shape
    K = 3
    tc = _tile(T, 2 * _TC)
    cw = _tile(F, _CW)
    ns, nt = F // cw, T // tc
    H = _CONV_HALO

    def body(dz_ref, gp_ref, val_ref, gph_ref, cw_ref, cb_ref, dup_ref, dcw_ref, dcb_ref, ext_ref, dext_ref):
        j = pl.program_id(1)
        first = j == nt - 1

        @pl.when(j == 0)
        def _():
            dext_ref[pl.ds(tc, H), :] = jnp.zeros((H, cw), f32)
            dcw_ref[...] = jnp.zeros_like(dcw_ref)
            dcb_ref[...] = jnp.zeros_like(dcb_ref)

        ext_ref[pl.ds(0, H), :] = jnp.where(first, 0.0, gph_ref[...])
        ext_ref[pl.ds(H, tc), :] = gp_ref[...]
        def visit(cols, rows, sums):
            r0, n = rows.start, rows.size
            gp = ext_ref[pl.ds(H + r0, n), cols]
            c = _causal_conv(ext_ref, cw_ref, cb_ref, n, K, cols, r0)
            ge, gg = _gelu_both(c)
            dzv = dz_ref[rows, cols].astype(f32)
            dup_ref[1, rows, cols] = (dzv * ge).astype(bf16)
            dc = dzv * val_ref[rows, cols] * gg
            dext_ref[rows, cols] = dc
            dgp = cw_ref[K - 1:K, cols] * dc
            new = [None] * K + [sums[K] + _fold8(dc)]
            new[K - 1] = sums[K - 1] + _fold8(gp * dc)
            for k in range(K - 1):
                sh = dext_ref[pl.ds(r0 + K - 1 - k, n), cols]
                dgp = dgp + cw_ref[k:k + 1, cols] * sh
                new[k] = sums[k] + _fold8(gp * sh)
            dup_ref[0, rows, cols] = dgp.astype(bf16)
            return new

        def flush(cols, totals):
            for k in range(K):
                dcw_ref[k:k + 1, cols] += totals[k]
            dcb_ref[:, cols] += totals[K]

        _slabs_with_sums(tc, cw, K + 1, visit, flush, reverse=True)
        dext_ref[pl.ds(tc, H), :] = dext_ref[pl.ds(0, H), :]

    hblk = tc // H
    return pl.pallas_call(
        body, name="ffn_bwd", grid=(ns, nt),
        in_specs=[pl.BlockSpec((tc, cw), lambda s, j: (nt - 1 - j, s)),
                  pl.BlockSpec((tc, cw), lambda s, j: (nt - 1 - j, s)),
                  pl.BlockSpec((tc, cw), lambda s, j: (nt - 1 - j, ns + s)),
                  pl.BlockSpec((H, cw), lambda s, j: (jnp.maximum((nt - 1 - j) * hblk - 1, 0), s)),
                  pl.BlockSpec((8, cw), lambda s, j: (0, s)), pl.BlockSpec((1, cw), lambda s, j: (0, s))],
        out_specs=[pl.BlockSpec((2, tc, cw), lambda s, j: (0, nt - 1 - j, s)),
                   pl.BlockSpec((8, cw), lambda s, j: (0, s)), pl.BlockSpec((1, cw), lambda s, j: (0, s))],
        out_shape=[jax.ShapeDtypeStruct((2, T, F), bf16), jax.ShapeDtypeStruct((8, F), f32), jax.ShapeDtypeStruct((1, F), f32)],
        scratch_shapes=[pltpu.VMEM((H + tc, cw), f32), pltpu.VMEM((tc + H, cw), f32)],
        compiler_params=_cparams("parallel", "arbitrary"),
    )(dz, up, up, up, conv_w, conv_b)


def _norm_bwd_matmul(name, a, a_spec, nk, w, w_spec, x, g, dres):
    T, D = x.shape
    tm = a_spec.block_shape[-2]

    def body(a_ref, w_ref, x_ref, g_ref, dr_ref, dx_ref, dxb_ref, dg_ref, acc_ref):
        i, k = pl.program_id(0), pl.program_id(1)

        @pl.when((i == 0) & (k == 0))
        def _():
            dg_ref[...] = jnp.zeros_like(dg_ref)

        @pl.when(k == 0)
        def _():
            acc_ref[...] = _dot_nt(a_ref[...], w_ref[...])

        @pl.when(k > 0)
        def _():
            acc_ref[...] += _dot_nt(a_ref[...], w_ref[...])

        @pl.when(k == nk - 1)
        def _():
            g = g_ref[...]
            dgs = jnp.zeros((_ROWS, D), f32)
            for r0 in range(0, tm, _ROWS):
                rows = pl.ds(r0, _ROWS)
                x = x_ref[rows, :]
                dh = acc_ref[rows, :]
                r = lax.rsqrt(jnp.mean(x * x, axis=-1, keepdims=True) + _EPS)
                gy = dh * g
                dx = dr_ref[rows, :] + (r * gy - x * ((r * r * r) * jnp.mean(x * gy, axis=-1, keepdims=True)))
                dgs = dgs + dh * (x * r)
                dx_ref[rows, :] = dx
                dxb_ref[rows, :] = dx.astype(bf16)
            dg_ref[...] += jnp.sum(dgs, axis=0, keepdims=True)

    row = pl.BlockSpec((tm, D), lambda i, k: (i, 0))
    vec = pl.BlockSpec((1, D), lambda i, k: (0, 0))
    return pl.pallas_call(
        body, name=name, grid=(T // tm, nk),
        in_specs=[a_spec, w_spec, row, vec, row],
        out_specs=[row, row, vec],
        out_shape=[jax.ShapeDtypeStruct((T, D), f32), jax.ShapeDtypeStruct((T, D), bf16), jax.ShapeDtypeStruct((1, D), f32)],
        scratch_shapes=[pltpu.VMEM((tm, D), f32)],
        compiler_params=_cparams("arbitrary", "arbitrary"),
    )(a, w, x, g, dres)


def _merge_bwd(dx2b, w_out, p, q, proj, b_gate, PW):
    T, D = p.shape
    tm, tn = _tile(T, _TM), _tile(D, PW)
    nj = D // tn
    off = (PW + 2 * D) // tn

    def body(dx_ref, w_ref, p_ref, q_ref, l0_ref, l1_ref, b0_ref, b1_ref, dp_ref, dq_ref, dl0_ref, dl1_ref, db0_ref, db1_ref,
             dm_ref):
        @pl.when(pl.program_id(1) == 0)
        def _():
            db0_ref[...] = jnp.zeros_like(db0_ref)
            db1_ref[...] = jnp.zeros_like(db1_ref)

        dm_ref[...] = _dot_nt(dx_ref[...], w_ref[...])

        def visit(cols, rows, sums):
            dm = dm_ref[rows, cols]
            g0 = _sigmoid(l0_ref[rows, cols] + b0_ref[:, cols])
            g1 = _sigmoid(l1_ref[rows, cols] + b1_ref[:, cols])
            dp_ref[rows, cols] = (g0 * dm).astype(bf16)
            dq_ref[rows, cols] = (g1 * dm).astype(bf16)
            dl0 = dm * p_ref[rows, cols].astype(f32) * (g0 * (1.0 - g0))
            dl1 = dm * q_ref[rows, cols].astype(f32) * (g1 * (1.0 - g1))
            dl0_ref[rows, cols] = dl0.astype(bf16)
            dl1_ref[rows, cols] = dl1.astype(bf16)
            return [sums[0] + _fold8(dl0), sums[1] + _fold8(dl1)]

        def flush(cols, totals):
            db0_ref[:, cols] += totals[0]
            db1_ref[:, cols] += totals[1]

        _slabs_with_sums(tm, tn, 2, visit, flush)

    tile = pl.BlockSpec((tm, tn), lambda j, i: (i, j))
    vecj = pl.BlockSpec((1, tn), lambda j, i: (0, j))
    tb = jax.ShapeDtypeStruct((T, D), bf16)
    vb = jax.ShapeDtypeStruct((1, D), f32)
    return pl.pallas_call(
        body, name="merge_bwd", grid=(nj, T // tm),
        in_specs=[pl.BlockSpec((tm, D), lambda j, i: (i, 0)), pl.BlockSpec((tn, D), lambda j, i: (j, 0)), tile, tile,
                  pl.BlockSpec((tm, tn), lambda j, i: (i, off + j)), pl.BlockSpec((tm, tn), lambda j, i: (i, off + nj + j)),
                  vecj, pl.BlockSpec((1, tn), lambda j, i: (0, nj + j))],
        out_specs=[tile, tile, tile, tile, vecj, vecj],
        out_shape=[tb, tb, tb, tb, vb, vb],
        scratch_shapes=[pltpu.VMEM((tm, tn), f32)],
        compiler_params=_cparams("parallel", "arbitrary"),
    )(dx2b, w_out, p, q, proj, proj, b_gate, b_gate)


def _lru_bwd(dy, proj, hs, D, conv_w, conv_b, w_a, b_a, w_i, b_i, lam):
    T = dy.shape[0]
    NB, bw, _ = w_a.shape
    K = 4
    tc = _tile(T, _TC)
    nt = T // tc
    H = _CONV_HALO
    hb = D // 2

    def body(dy_ref, u0_ref, u1_ref, g0_ref, g1_ref, hs_ref, uh0_ref, uh1_ref, hh_ref,
             cw_ref, cb_ref, wa_ref, ba_ref, wi_ref, bi_ref, lam_ref,
             du_ref, dg_ref, dwa_ref, dwi_ref, dcw_ref, dvec_ref,
             ext_ref, hext_ref, dext_ref, q_ref, a_scr, r_scr, i_scr, v_scr, g_scr, car_scr):
        j = pl.program_id(0)
        first = j == nt - 1

        @pl.when(j == 0)
        def _():
            dext_ref[pl.ds(tc, H), :] = jnp.zeros((H, D), f32)
            car_scr[...] = jnp.zeros_like(car_scr)
            dwa_ref[...] = jnp.zeros_like(dwa_ref)
            dwi_ref[...] = jnp.zeros_like(dwi_ref)
            dcw_ref[...] = jnp.zeros_like(dcw_ref)
            dvec_ref[...] = jnp.zeros_like(dvec_ref)

        ext_ref[pl.ds(0, H), 0:hb] = jnp.where(first, 0.0, uh0_ref[...])
        ext_ref[pl.ds(0, H), hb:D] = jnp.where(first, 0.0, uh1_ref[...])
        ext_ref[pl.ds(H, tc), 0:hb] = u0_ref[...]
        ext_ref[pl.ds(H, tc), hb:D] = u1_ref[...]
        hext_ref[pl.ds(0, H), :] = jnp.where(first, 0.0, hh_ref[...])
        hext_ref[pl.ds(H, tc), :] = hs_ref[...]
        lamv = lam_ref[...]
        sp = _softplus(-lamv)

        for cols, rows in _slabs(tc, D):
            v_scr[rows, cols] = _causal_conv(ext_ref, cw_ref, cb_ref, rows.size, K, cols, rows.start)
        for b in range(NB):
            cols = slice(b * bw, (b + 1) * bw)
            vb = v_scr[:, cols].astype(bf16)
            r_scr[:, cols] = _dot(vb, wa_ref[b])
            i_scr[:, cols] = _dot(vb, wi_ref[b])
        for cols, rows in _slabs(tc, D):
            r = _sigmoid(r_scr[rows, cols] + ba_ref[:, cols])
            r_scr[rows, cols] = r
            i_scr[rows, cols] = _sigmoid(i_scr[rows, cols] + bi_ref[:, cols])
            a = jnp.exp(-_LRU_C * r * sp[:, cols])
            a_scr[rows, cols] = a
            g_ref, gcols = (g0_ref, cols) if cols.start < hb else (g1_ref, slice(cols.start - hb, cols.stop - hb))
            ge, gg = _gelu_both(g_ref[rows, gcols])
            dyv = dy_ref[rows, cols]
            dho = dyv * ge
            g_scr[rows, cols] = dho
            q_ref[rows, cols] = a * dho
            dg_ref[rows, cols] = (dyv * hs_ref[rows, cols] * gg).astype(bf16)

        q_ref[pl.ds(tc, 1), :] = car_scr[0:1, :]
        _scan_rows(a_scr, q_ref, q_ref, car_scr, tc, D, reverse=True)

        def gates(cols, rows, sums):
            g = g_scr[rows, cols] + q_ref[pl.ds(rows.start + 1, rows.size), cols]
            v, r, i, a = v_scr[rows, cols], r_scr[rows, cols], i_scr[rows, cols], a_scr[rows, cols]
            mult = jnp.sqrt(1.0 - a * a)
            h_prev = hext_ref[pl.ds(H - 1 + rows.start, rows.size), cols]
            gm = g * mult
            dext_ref[rows, cols] = gm * i
            dlog_a = (g * h_prev - g * (i * v) * (a / mult)) * a
            dpr = dlog_a * (-_LRU_C * sp[:, cols]) * (r * (1.0 - r))
            dpi = gm * v * (i * (1.0 - i))
            r_scr[rows, cols] = dpr
            i_scr[rows, cols] = dpi
            return [sums[0] + _fold8(dpr), sums[1] + _fold8(dpi), sums[2] + _fold8(dlog_a * (-_LRU_C * r))]

        def gates_flush(cols, totals):
            dvec_ref[1:2, cols] += totals[0]
            dvec_ref[2:3, cols] += totals[1]
            dvec_ref[3:4, cols] += totals[2] * (-_sigmoid(-lamv[:, cols]))

        _slabs_with_sums(tc, D, 3, gates, gates_flush)
        for b in range(NB):
            cols = slice(b * bw, (b + 1) * bw)
            dprb, dpib, vb = r_scr[:, cols].astype(bf16), i_scr[:, cols].astype(bf16), v_scr[:, cols].astype(bf16)
            dext_ref[pl.ds(0, tc), cols] += _dot_nt(dprb, wa_ref[b]) + _dot_nt(dpib, wi_ref[b])
            dwa_ref[b] += _dot_tn(vb, dprb)
            dwi_ref[b] += _dot_tn(vb, dpib)

        def conv_t(cols, rows, sums):
            r0, n = rows.start, rows.size
            u = ext_ref[pl.ds(H + r0, n), cols]
            dv = dext_ref[rows, cols]
            du = cw_ref[K - 1:K, cols] * dv
            new = [None] * K + [sums[K] + _fold8(dv)]
            new[K - 1] = sums[K - 1] + _fold8(u * dv)
            for k in range(K - 1):
                sh = dext_ref[pl.ds(r0 + K - 1 - k, n), cols]
                du = du + cw_ref[k:k + 1, cols] * sh
                new[k] = sums[k] + _fold8(u * sh)
            du_ref[rows, cols] = du.astype(bf16)
            return new

        def conv_t_flush(cols, totals):
            for k in range(K):
                dcw_ref[k:k + 1, cols] += totals[k]
            dvec_ref[0:1, cols] += totals[K]

        _slabs_with_sums(tc, D, K + 1, conv_t, conv_t_flush)
        dext_ref[pl.ds(tc, H), :] = dext_ref[pl.ds(0, H), :]

    hblk = tc // H
    rev = lambda j: nt - 1 - j
    halo = lambda j: jnp.maximum((nt - 1 - j) * hblk - 1, 0)
    vec = pl.BlockSpec((1, D), lambda j: (0, 0))
    wspec = pl.BlockSpec((NB, bw, bw), lambda j: (0, 0, 0))
    acc8 = pl.BlockSpec((8, D), lambda j: (0, 0))
    big = pltpu.VMEM((tc, D), f32)
    return pl.pallas_call(
        body, name="lru_bwd", grid=(nt,),
        in_specs=[pl.BlockSpec((tc, D), lambda j: (rev(j), 0)),
                  pl.BlockSpec((tc, hb), lambda j: (rev(j), 1)), pl.BlockSpec((tc, hb), lambda j: (rev(j), 2)),
                  pl.BlockSpec((tc, hb), lambda j: (rev(j), 3)), pl.BlockSpec((tc, hb), lambda j: (rev(j), 4)),
                  pl.BlockSpec((tc, D), lambda j: (rev(j), 0)),
                  pl.BlockSpec((H, hb), lambda j: (halo(j), 1)), pl.BlockSpec((H, hb), lambda j: (halo(j), 2)),
                  pl.BlockSpec((H, D), lambda j: (halo(j), 0)),
                  acc8, vec, wspec, vec, wspec, vec, vec],
        out_specs=[pl.BlockSpec((tc, D), lambda j: (rev(j), 0)), pl.BlockSpec((tc, D), lambda j: (rev(j), 0)),
                   wspec, wspec, acc8, acc8],
        out_shape=[jax.ShapeDtypeStruct((T, D), bf16), jax.ShapeDtypeStruct((T, D), bf16),
                   jax.ShapeDtypeStruct((NB, bw, bw), f32), jax.ShapeDtypeStruct((NB, bw, bw), f32),
                   jax.ShapeDtypeStruct((8, D), f32), jax.ShapeDtypeStruct((8, D), f32)],
        scratch_shapes=[pltpu.VMEM((H + tc, D), f32), pltpu.VMEM((H + tc, D), f32), pltpu.VMEM((tc + H, D), f32),
                        pltpu.VMEM((tc + H, D), f32), big, big, big, big, big, pltpu.VMEM((8, D), f32)],
        compiler_params=_cparams("arbitrary"),
    )(dy, proj, proj, proj, proj, hs, proj, proj, hs, conv_w, conv_b, w_a, b_a, w_i, b_i, lam)


def _pool_bwd(dy, proj, w_pool, pool_scale):
    T, PW = dy.shape
    G, gw, _ = w_pool.shape
    tc = _tile(T, _TC)
    nt = T // tc
    H = _POOL_HALO

    def body(dy_ref, u_ref, uh_ref, w_ref, s_ref, du_ref, dw_ref, ds_ref, ext_ref, eext_ref):
        j = pl.program_id(0)
        first = j == nt - 1
        row0 = (nt - 1 - j) * tc

        @pl.when(j == 0)
        def _():
            eext_ref[pl.ds(tc, H), :] = jnp.zeros((H, PW), f32)
            dw_ref[...] = jnp.zeros_like(dw_ref)
            ds_ref[...] = jnp.zeros_like(ds_ref)

        ext_ref[pl.ds(0, H), :] = jnp.where(first, 0.0, uh_ref[...])
        ext_ref[pl.ds(H, tc), :] = u_ref[...]
        t_glob = row0 + lax.broadcasted_iota(jnp.int32, (tc, 1), 0)
        dds = []
        for g, (m, u) in enumerate(_window_means(ext_ref, row0, tc, gw)):
            cols = slice(g * gw, (g + 1) * gw)
            d = (m - u).astype(bf16)
            yraw = _dot(d, w_ref[g])
            dyv = dy_ref[:, cols]
            ds_ref[:, cols] += jnp.sum(dyv * yraw, axis=0, keepdims=True)
            dyr = (dyv * s_ref[:, cols]).astype(bf16)
            dd = _dot_nt(dyr, w_ref[g])
            dw_ref[g] += _dot_tn(d, dyr)
            cnt = jnp.minimum(t_glob + 1, _POOL_WINDOWS[g]).astype(f32)
            eext_ref[pl.ds(0, tc), cols] = dd / cnt
            dds.append(dd)
        n = tc + H
        for g, w in enumerate(_POOL_WINDOWS):
            cols = slice(g * gw, (g + 1) * gw)
            s = eext_ref[:, cols]
            st = 1
            while st < w:
                s = s + pltpu.roll(s, n - st, 0)
                st *= 2
            du_ref[:, cols] = (s[0:tc, :] - dds[g]).astype(bf16)
        eext_ref[pl.ds(tc, H), :] = eext_ref[pl.ds(0, H), :]

    hblk = tc // H
    return pl.pallas_call(
        body, name="pool_bwd", grid=(nt,),
        in_specs=[pl.BlockSpec((tc, PW), lambda j: (nt - 1 - j, 0)), pl.BlockSpec((tc, PW), lambda j: (nt - 1 - j, 0)),
                  pl.BlockSpec((H, PW), lambda j: (jnp.maximum((nt - 1 - j) * hblk - 1, 0), 0)),
                  pl.BlockSpec((G, gw, gw), lambda j: (0, 0, 0)), pl.BlockSpec((1, PW), lambda j: (0, 0))],
        out_specs=[pl.BlockSpec((tc, PW), lambda j: (nt - 1 - j, 0)), pl.BlockSpec((G, gw, gw), lambda j: (0, 0, 0)),
                   pl.BlockSpec((1, PW), lambda j: (0, 0))],
        out_shape=[jax.ShapeDtypeStruct((T, PW), bf16), jax.ShapeDtypeStruct((G, gw, gw), f32), jax.ShapeDtypeStruct((1, PW), f32)],
        scratch_shapes=[pltpu.VMEM((H + tc, PW), f32), pltpu.VMEM((tc + H, PW), f32)],
        compiler_params=_cparams("arbitrary"),
    )(dy, proj, proj, w_pool, pool_scale)


_MESH = pl.DeviceIdType.MESH
_HBM = pl.BlockSpec(memory_space=pltpu.HBM)


def _slab(ref, kind, blk, n):
    start = blk * n
    if n % _LANE == 0:
        start = pl.multiple_of(start, _LANE)
    if kind == "col":
        return ref.at[:, pl.ds(start, n)]
    if kind == "row":
        return ref.at[pl.ds(start, n), :]
    if kind == "mid":
        return ref.at[:, pl.ds(start, n), :]
    raise ValueError(kind)


def _my_place():
    x, y, c = lax.axis_index("x"), lax.axis_index("y"), lax.axis_index("c")
    return x, y, c


def _blk(px, py, pc):
    return 4 * px + 2 * py + pc


_SEM = pl.BlockSpec(memory_space=pltpu.SEMAPHORE)
_EFFECT = pltpu.SideEffectType.DATAFLOW_SIDE_EFFECTING


_ALL_PEERS = (1, 2, 3, 4, 5, 6, 7)


def _peers(x, y, c):
    return [(k, (x ^ (k >> 2), y ^ ((k >> 1) & 1), c ^ (k & 1))) for k in range(1, 8)]


class _Route:
    def __init__(self, mode, kind, size):
        self.mode, self.kind, self.size = mode, kind, size

    def src(self, ref, peer_blk):
        return ref if self.mode == "gather" else _slab(ref, self.kind, peer_blk, self.size)

    def dst(self, ref, origin_blk):
        return _slab(ref, self.kind, origin_blk, self.size) if self.mode == "gather" else ref.at[origin_blk]


def _send_start(name, srcs, lands, routes, groups):
    nt, ng = len(srcs), len(groups)

    def body(*refs):
        src_refs, land_refs = refs[:nt], refs[nt:2 * nt]
        sems = refs[2 * nt:2 * nt + 2 * ng]
        token = refs[-1]
        x, y, c = _my_place()
        me = _blk(x, y, c)
        for gi, (grp, ks) in enumerate(groups):
            for pos, t in enumerate(grp):
                for k, peer in _peers(x, y, c):
                    if k in ks:
                        s = len(ks) * pos + ks.index(k)
                        pltpu.make_async_remote_copy(
                            src_ref=routes[t].src(src_refs[t], _blk(*peer)), dst_ref=routes[t].dst(land_refs[t], me),
                            send_sem=sems[2 * gi].at[s], recv_sem=sems[2 * gi + 1].at[s],
                            device_id=peer, device_id_type=_MESH).start()
        token[...] = jnp.zeros_like(token)

    hbm = lambda a: pltpu.HBM(a.shape, a.dtype)
    out_shape = []
    for grp, ks in groups:
        out_shape += [pltpu.SemaphoreType.DMA((len(ks) * len(grp),)), pltpu.SemaphoreType.DMA((len(ks) * len(grp),))]
    out_shape += [hbm(a) for a in srcs] + [hbm(a) for a in lands] + [jax.ShapeDtypeStruct((8, _LANE), f32)]
    res = pl.pallas_call(
        body, name=name, out_shape=out_shape,
        in_specs=[_HBM] * (2 * nt),
        out_specs=[_SEM] * (2 * ng) + [_HBM] * (2 * nt) + [pl.BlockSpec(memory_space=pltpu.VMEM)],
        input_output_aliases={t: 2 * ng + t for t in range(2 * nt)},
        compiler_params=pltpu.CompilerParams(has_side_effects=_EFFECT),
    )(*[pltpu.with_memory_space_constraint(a, pltpu.HBM) for a in list(srcs) + list(lands)])
    sems = [(res[2 * g], res[2 * g + 1]) for g in range(ng)]
    return sems, res[2 * ng:2 * ng + nt], res[2 * ng + nt:2 * ng + 2 * nt], res[-1]


def _send_wait(name, srcs, lands, routes, sems, after, ks=_ALL_PEERS):
    n = len(srcs)

    def body(*refs):
        src_refs, land_refs = refs[:n], refs[n:2 * n]
        send_sems, recv_sems = refs[2 * n], refs[2 * n + 1]
        x, y, c = _my_place()
        for pos in range(n):
            for k, peer in _peers(x, y, c):
                if k in ks:
                    pb = _blk(*peer)
                    s = len(ks) * pos + ks.index(k)
                    cp = pltpu.make_async_remote_copy(
                        src_ref=routes[pos].src(src_refs[pos], pb), dst_ref=routes[pos].dst(land_refs[pos], pb),
                        send_sem=send_sems.at[s], recv_sem=recv_sems.at[s], device_id=peer, device_id_type=_MESH)
                    cp.wait_send()
                    cp.wait_recv()

    hbm = lambda a: pltpu.HBM(a.shape, a.dtype)
    res = pl.pallas_call(
        body, name=name, out_shape=[hbm(a) for a in srcs] + [hbm(a) for a in lands],
        in_specs=[_HBM] * (2 * n) + [_SEM, _SEM, pl.BlockSpec(memory_space=pl.ANY)],
        out_specs=[_HBM] * (2 * n),
        input_output_aliases={t: t for t in range(2 * n)},
        compiler_params=pltpu.CompilerParams(has_side_effects=_EFFECT),
    )(*srcs, *lands, sems[0], sems[1], after)
    return res[:n], res[n:]


def _forward_start(name, lands, routes, ks):
    n = len(lands)

    def body(*refs):
        land_refs, send_sems, recv_sems, token = refs[:n], refs[n], refs[n + 1], refs[-1]
        x, y, c = _my_place()
        for pos in range(n):
            for i, k in enumerate(ks):
                part = routes[pos].dst(land_refs[pos], _blk(x ^ (k >> 2), y ^ ((k >> 1) & 1), c))
                pltpu.make_async_remote_copy(
                    src_ref=part, dst_ref=part, send_sem=send_sems.at[len(ks) * pos + i],
                    recv_sem=recv_sems.at[len(ks) * pos + i], device_id=(x, y, 1 - c), device_id_type=_MESH).start()
        token[...] = jnp.zeros_like(token)

    hbm = lambda a: pltpu.HBM(a.shape, a.dtype)
    sem = pltpu.SemaphoreType.DMA((len(ks) * n,))
    res = pl.pallas_call(
        body, name=name, out_shape=[sem, sem] + [hbm(a) for a in lands] + [jax.ShapeDtypeStruct((8, _LANE), f32)],
        in_specs=[_HBM] * n, out_specs=[_SEM, _SEM] + [_HBM] * n + [pl.BlockSpec(memory_space=pltpu.VMEM)],
        input_output_aliases={t: 2 + t for t in range(n)},
        compiler_params=pltpu.CompilerParams(has_side_effects=_EFFECT),
    )(*[pltpu.with_memory_space_constraint(a, pltpu.HBM) for a in lands])
    return (res[0], res[1]), res[2:2 + n], res[-1]


def _forward_wait(name, lands, routes, sems, after, ks):
    n = len(lands)

    def body(*refs):
        land_refs, send_sems, recv_sems = refs[:n], refs[n], refs[n + 1]
        x, y, c = _my_place()
        for pos in range(n):
            for i, k in enumerate(ks):
                px, py = x ^ (k >> 2), y ^ ((k >> 1) & 1)
                cp = pltpu.make_async_remote_copy(
                    src_ref=routes[pos].dst(land_refs[pos], _blk(px, py, c)),
                    dst_ref=routes[pos].dst(land_refs[pos], _blk(px, py, 1 - c)),
                    send_sem=send_sems.at[len(ks) * pos + i], recv_sem=recv_sems.at[len(ks) * pos + i],
                    device_id=(x, y, 1 - c), device_id_type=_MESH)
                cp.wait_send()
                cp.wait_recv()

    hbm = lambda a: pltpu.HBM(a.shape, a.dtype)
    return pl.pallas_call(
        body, name=name, out_shape=[hbm(a) for a in lands],
        in_specs=[_HBM] * n + [_SEM, _SEM, pl.BlockSpec(memory_space=pl.ANY)], out_specs=[_HBM] * n,
        input_output_aliases={t: t for t in range(n)},
        compiler_params=pltpu.CompilerParams(has_side_effects=_EFFECT),
    )(*lands, sems[0], sems[1], after)


def _copy_own(name, src, land, route, me):
    gather = route.mode == "gather"
    shard = src.shape if gather else land.shape[1:]
    lead = () if gather else (None,)

    if route.kind == "mid":
        grid = (1,)
        at_full = lambda i, me: (0, me[0], 0)
        at_shard = lambda i, me: (0, 0, 0)
        block = tuple(shard)
    else:
        rows, width = shard
        tr = _tile(rows, 512, 16)
        grid = (rows // tr,)
        block = (tr, width)
        if route.kind == "col":
            at_full = lambda i, me: (i, me[0])
        else:
            at_full = lambda i, me: (me[0] * grid[0] + i, 0)
        at_shard = lambda i, me: (i, 0)
    if gather:
        in_map, out_map = at_shard, at_full
    else:
        in_map, out_map = at_full, (lambda i, me: (me[0], *at_shard(i, me)))

    def body(me_ref, src_ref, land_ref, out_ref):
        out_ref[...] = src_ref[...]

    return pl.pallas_call(
        body, name=name, out_shape=jax.ShapeDtypeStruct(land.shape, land.dtype),
        grid_spec=pltpu.PrefetchScalarGridSpec(
            num_scalar_prefetch=1, grid=grid,
            in_specs=[pl.BlockSpec(block, in_map), pl.BlockSpec(memory_space=pl.ANY)],
            out_specs=pl.BlockSpec(lead + block, out_map)),
        input_output_aliases={2: 0},
        compiler_params=_cparams("arbitrary"),
    )(me, src, land)


def _place_own(name, srcs, lands, routes):
    me = _blk(*_my_place()).astype(jnp.int32).reshape(1)
    return [_copy_own(f"{name}_{t}", s, l, r, me) for t, (s, l, r) in enumerate(zip(srcs, lands, routes))]


def _exchange(fulls, kinds, sizes, whole):
    arrays = list(fulls) + list(whole)
    nt, nf = len(arrays), len(fulls)

    def shard_shape(t):
        s = list(arrays[t].shape)
        if t < nf:
            s[{"col": 1, "row": 0, "mid": 1}[kinds[t]]] = sizes[t]
        return tuple(s)

    def body(*refs):
        ins, outs = refs[:nt], refs[nt:2 * nt]
        send_sems, recv_sems, local_sems = refs[2 * nt:]
        x, y, c = _my_place()
        me = _blk(x, y, c)

        def src(t, blk):
            return _slab(ins[t], kinds[t], blk, sizes[t]) if t < nf else ins[t]

        mine = [pltpu.make_async_copy(src(t, me), outs[t].at[me], local_sems.at[t]) for t in range(nt)]
        for cp in mine:
            cp.start()
        sent = []
        for t in range(nt):
            for k in range(1, 8):
                peer = (x ^ (k >> 2), y ^ ((k >> 1) & 1), c ^ (k & 1))
                pb = _blk(*peer)
                cp = pltpu.make_async_remote_copy(
                    src_ref=src(t, pb), dst_ref=outs[t].at[me],
                    send_sem=send_sems.at[7 * t + k - 1], recv_sem=recv_sems.at[7 * t + k - 1],
                    device_id=peer, device_id_type=_MESH)
                cp.start()
                sent.append((cp, t, k, pb))
        for cp, t, k, pb in sent:
            pltpu.make_async_remote_copy(
                src_ref=src(t, pb), dst_ref=outs[t].at[pb],
                send_sem=send_sems.at[7 * t + k - 1], recv_sem=recv_sems.at[7 * t + k - 1],
                device_id=(x, y, c), device_id_type=_MESH).wait_recv()
        for cp, _, _, _ in sent:
            cp.wait_send()
        for cp in mine:
            cp.wait()

    return pl.pallas_call(
        body, name="exchange_grads",
        in_specs=[_HBM] * nt, out_specs=[_HBM] * nt,
        out_shape=[jax.ShapeDtypeStruct((_N_DEV,) + shard_shape(t), arrays[t].dtype) for t in range(nt)],
        scratch_shapes=[pltpu.SemaphoreType.DMA((7 * nt,)), pltpu.SemaphoreType.DMA((7 * nt,)), pltpu.SemaphoreType.DMA((nt,))],
        compiler_params=pltpu.CompilerParams(has_side_effects=True),
    )(*arrays)


def _adamw_update(w_ref, m_ref, v_ref, g, g_ref, d_ref, nm_ref, nv_ref):
    c1 = 1.0 - _ADAM_B1 ** _ADAM_STEP
    c2 = 1.0 - _ADAM_B2 ** _ADAM_STEP
    nm = _ADAM_B1 * m_ref[...] + (1.0 - _ADAM_B1) * g
    nv = _ADAM_B2 * v_ref[...] + (1.0 - _ADAM_B2) * (g * g)
    g_ref[...] = g
    nm_ref[...] = nm
    nv_ref[...] = nv
    d_ref[...] = -_ADAM_LR * ((nm / c1) / (jnp.sqrt(nv / c2) + _ADAM_EPS) + _ADAM_WD * w_ref[...])


def _adamw_small(me, ws, ms, vs, parts):
    n = len(ws)

    def body(me_ref, *refs):
        ins, outs = refs[:4 * n], refs[4 * n:]
        for t in range(n):
            w_ref, m_ref, v_ref, p_ref = ins[4 * t:4 * t + 4]
            r = w_ref.shape[0]
            g = p_ref[0, 0:r, :]
            for s in range(1, _N_DEV):
                g = g + p_ref[s, 0:r, :]
            _adamw_update(w_ref, m_ref, v_ref, g, *outs[4 * t:4 * t + 4])

    whole = lambda a: pl.BlockSpec(a.shape, lambda i, me, nd=a.ndim: (0,) * nd)
    in_specs, operands, out_specs, out_shape = [], [], [], []
    for w, m, v, p in zip(ws, ms, vs, parts):
        c = w.shape[1]
        mine = whole(p) if p.shape[2] == c else pl.BlockSpec((_N_DEV, p.shape[1], c), lambda i, me: (0, 0, me[0]))
        in_specs += [whole(w), whole(m), whole(v), mine]
        operands += [w, m, v, p]
        out_specs += [whole(w)] * 4
        out_shape += [jax.ShapeDtypeStruct(w.shape, f32)] * 4
    return pl.pallas_call(
        body, name="adamw_small", out_shape=out_shape,
        grid_spec=pltpu.PrefetchScalarGridSpec(num_scalar_prefetch=1, grid=(1,), in_specs=in_specs, out_specs=out_specs),
        compiler_params=_cparams("arbitrary"),
    )(me, *operands)


def _adamw(name, w, m, v, parts):
    R, C = w.shape
    n = parts.shape[0]
    tr = _tile(R, 256, 8)

    def body(w_ref, m_ref, v_ref, p_ref, g_ref, d_ref, nm_ref, nv_ref):
        g = p_ref[0].astype(f32)
        for s in range(1, n):
            g = g + p_ref[s].astype(f32)
        _adamw_update(w_ref, m_ref, v_ref, g, g_ref, d_ref, nm_ref, nv_ref)

    blk = pl.BlockSpec((tr, C), lambda i: (i, 0))
    sd = jax.ShapeDtypeStruct((R, C), f32)
    return pl.pallas_call(
        body, name=name, grid=(R // tr,),
        in_specs=[blk, blk, blk, pl.BlockSpec((n, tr, C), lambda i: (0, i, 0))],
        out_specs=[blk, blk, blk, blk], out_shape=[sd, sd, sd, sd],
        compiler_params=_cparams("parallel"),
    )(w, m, v, parts)


def _pad_rows8(a):
    return jnp.pad(a, ((0, 8 - a.shape[0]), (0, 0)))


def _local_step(x, target, p, get, emit, hint):
    T, D = x.shape

    def tie(a, *tokens):
        for tok in tokens:
            if tok is not None:
                a = a + tok[0, 0]
        return a

    n_parts = _N_DEV // _W_IN_PART
    w_in, base = get("w_in", x, 0)
    h1, proj = _in_proj_first(x, p["g_mix"], w_in, base, _W_IN_PART)
    hint("w_in", proj)
    for part in range(1, n_parts):
        w_in, base = get("w_in", proj, part)
        proj = _in_proj_more(f"in_proj_{part}", h1, w_in, proj, base, _W_IN_PART)
    w_pool = get("w_pool", proj)
    PW = w_pool.shape[0] * w_pool.shape[1]
    y_pool = _pool_fwd(proj, w_pool, p["pool_scale"])
    tok = hint("w_pool_proj", y_pool)
    lru_conv_w, w_a, w_i = get("lru_conv_w", proj), get("w_a", proj), get("w_i", proj)
    y_lru, hs = _lru_fwd(proj, D, PW, lru_conv_w, tie(p["lru_conv_b"], tok), w_a, p["b_a"], w_i, p["b_i"], p["lru_lambda"])
    tok = hint("w_up", y_lru)
    w_pp, w_lp = get("w_pool_proj", y_lru), get("w_lru_proj", y_lru)
    pp, qq, merged = _merge_fwd(y_pool, y_lru, w_pp, w_lp, proj, tie(p["b_gate"], tok))
    w_out = get("w_out", y_lru)
    x2, h2 = _out_proj(merged, w_out, x, p["g_mlp"])
    w_up = get("w_up", x2)
    up = _mm_nn("up_proj", h2, w_up, f32, tm_want=2 * _TM)
    tok = hint("w_down", up)
    ffn_conv_w = get("ffn_conv_w", y_lru)
    z = _ffn_fwd(up, ffn_conv_w, tie(p["ffn_conv_b"], tok))
    w_down = get("w_down", z)
    F = w_down.shape[0]
    dx3, dx3b, loss_t, dg_final = _down_loss(z, w_down, x2, target, p["g_final"])

    gs = {"g_final": dg_final}
    tok = emit("w_down", _mm_tn("dw_down", z, dx3b, bf16, tm_want=1536))
    dz = _mm_nt("dz", dx3b, w_down, bf16)
    dup, dcw_ffn, dcb_ffn = _ffn_bwd(dz, up, ffn_conv_w, tie(p["ffn_conv_b"], tok))
    gs["ffn_conv_w"] = dcw_ffn
    gs["ffn_conv_b"] = dcb_ffn

    tm = _tile(T, _TM)
    tk = _tile(F, _TN_MAX)
    nkh = F // tk
    tkt = _tile(T, _TK_T)
    tok = emit("w_up", _matmul(
        "dw_up", "tn", (h2, dup),
        [pl.BlockSpec((tkt, D), lambda i, j, k: (k, 0)), pl.BlockSpec((None, tkt, tk), lambda i, j, k: (j // nkh, k, j % nkh))],
        jax.ShapeDtypeStruct((D, 2 * F), bf16), pl.BlockSpec((D, tk), lambda i, j, k: (0, j)),
        (1, 2 * nkh, T // tkt), (D, tk)))

    tkc = _tile(F, _TK_UP)
    nkc = F // tkc
    dx2, dx2b, gs["g_mlp"] = _norm_bwd_matmul(
        "dh2", dup, pl.BlockSpec((None, tm, tkc), lambda i, k: (k // nkc, i, k % nkc)), 2 * nkc,
        w_up, pl.BlockSpec((D, tkc), lambda i, k: (0, k)), x2, tie(p["g_mlp"], tok), dx3)

    tok = emit("w_out", _mm_tn("dw_out", merged, dx2b, bf16))
    dP, dQ, dl0, dl1, db0, db1 = _merge_bwd(dx2b, w_out, pp, qq, proj, tie(p["b_gate"], tok), PW)
    gs["b_gate"] = jnp.concatenate([db0, db1], axis=1)
    tok = emit("w_pool_proj", _mm_tn("dw_pool_proj", y_pool, dP, bf16))
    tok2 = emit("w_lru_proj", _mm_tn("dw_lru_proj", y_lru, dQ, bf16))
    dy_pool = _mm_nt("dy_pool", dP, w_pp, f32)
    dy_lru = _mm_nt("dy_lru", dQ, w_lp, f32)

    du_lru, du_gelu, dwa, dwi, dcw_lru, dvec = _lru_bwd(
        dy_lru, proj, hs, D, lru_conv_w, tie(p["lru_conv_b"], tok, tok2), w_a, p["b_a"], w_i, p["b_i"], p["lru_lambda"])
    tok = emit("w_a", dwa.astype(bf16))
    tok2 = emit("w_i", dwi.astype(bf16))
    gs["lru_conv_w"] = dcw_lru
    gs["lru_conv_b"], gs["b_a"], gs["b_i"], gs["lru_lambda"] = dvec[0:1], dvec[1:2], dvec[2:3], dvec[3:4]
    du_pool, dwp, gs["pool_scale"] = _pool_bwd(dy_pool, proj, w_pool, tie(p["pool_scale"], tok, tok2))
    tok = emit("w_pool", dwp.astype(bf16))

    dproj = jnp.concatenate([du_pool, du_lru, du_gelu, dl0, dl1], axis=1)
    tok2 = emit("w_in", _mm_tn("dw_in", h1, dproj, bf16))
    NI = dproj.shape[1]
    tki = _tile(NI, _TN_MAX)
    grad_x, _, gs["g_mix"] = _norm_bwd_matmul(
        "dh1", dproj, pl.BlockSpec((tm, tki), lambda i, k: (i, k)), NI // tki,
        w_in, pl.BlockSpec((D, tki), lambda i, k: (0, k)), x, tie(p["g_mix"], tok, tok2), dx2)
    return loss_t[0, 0], grad_x, gs


_MATRICES = {"w_in": "col", "w_pool": "mid", "w_a": "mid", "w_i": "mid", "w_pool_proj": "col", "w_lru_proj": "row",
             "w_out": "row", "w_up": "col", "w_down": "row"}
_CONVS = ("lru_conv_w", "ffn_conv_w")
_GATHER_GROUPS = (("w_pool", "lru_conv_w", "w_a", "w_i"), ("w_pool_proj", "w_lru_proj", "w_out", "ffn_conv_w"),
                  ("w_up",), ("w_down",))
_GROUP_TWO_LEVEL = (False, True, True, True)
_SAME_CORE = (2, 4, 6)
_TWO_LEVEL = (1, 2, 4, 6)
_W_IN_PARTS = ((1,), (4,), (2,), (6,))
_W_IN_PART = 2
_VECTORS = ("g_mix", "b_gate", "pool_scale", "lru_conv_b", "b_a", "b_i", "lru_lambda", "g_mlp", "ffn_conv_b", "g_final")
_WEIGHTS = ("g_mix", "w_in", "b_gate", "w_pool", "pool_scale", "lru_conv_w", "lru_conv_b", "w_a", "b_a", "w_i", "b_i",
            "lru_lambda", "w_pool_proj", "w_lru_proj", "w_out", "g_mlp", "w_up", "ffn_conv_w", "ffn_conv_b", "w_down", "g_final")


def _full_shape(shape, kind):
    s = list(shape)
    s[{"col": 1, "row": 0, "mid": 1}[kind]] *= _N_DEV
    return tuple(s)


def _step(x, target, w, m, v):
    x, target = x[0], target[0]
    me = _blk(*_my_place())

    axis = {"col": 1, "row": 0, "mid": 1}
    kind = dict(_MATRICES, **{n: "col" for n in _CONVS})
    shard = {n: w[n].astype(bf16) for n in _MATRICES}
    shard.update({n: _pad_rows8(w[n]) for n in _CONVS})
    order = ["w_in"] + [n for grp in _GATHER_GROUPS for n in grp]
    index = {n: t for t, n in enumerate(order)}
    n_parts = len(_W_IN_PARTS)
    groups = [([0], ks) for ks in _W_IN_PARTS]
    groups += [([index[n] for n in grp], _TWO_LEVEL if two else _ALL_PEERS) for grp, two in zip(_GATHER_GROUPS, _GROUP_TWO_LEVEL)]
    g_routes = [_Route("gather", kind[n], shard[n].shape[axis[kind[n]]]) for n in order]
    lands = [lax.empty(_full_shape(shard[n].shape, kind[n]), shard[n].dtype) for n in order]
    g_sems, g_srcs, g_lands, _ = _send_start("gather_start", [shard[n] for n in order], lands, g_routes, groups)
    gathered, passing = {}, {}
    x_, y_, _c = _my_place()
    w_in_state = [[g_srcs[0]], [g_lands[0]]]

    def hint(name, after):
        if name == "w_in":
            srcs, got = w_in_state
            tok = None
            for part in range(1, n_parts):
                ks = _W_IN_PARTS[part]
                srcs, got = _send_wait(f"gather_wait_w_in_{part}", srcs, got, g_routes[:1], g_sems[part], after, ks)
                sems, got, tok = _forward_start(f"gather_pass_w_in_{part}", got, g_routes[:1], ks)
                passing[name, part] = sems
            w_in_state[:] = [srcs, got]
            return tok
        gi = next(i for i, grp in enumerate(_GATHER_GROUPS) if name in grp)
        if not _GROUP_TWO_LEVEL[gi] or gi in passing or _GATHER_GROUPS[gi][0] in gathered:
            return None
        ts = groups[n_parts + gi][0]
        routes = [g_routes[t] for t in ts]
        srcs, got = _send_wait(f"gather_wait_{gi}", [g_srcs[t] for t in ts], [g_lands[t] for t in ts], routes,
                               g_sems[n_parts + gi], after, _TWO_LEVEL)
        got = _place_own(f"gather_own_{gi}", srcs, got, routes)
        sems, got, tok = _forward_start(f"gather_pass_{gi}", got, routes, _SAME_CORE)
        passing[gi] = (sems, got, routes)
        return tok

    def get(name, after, part=None):
        if name == "w_in":
            ks = _W_IN_PARTS[part]
            if part == 0:
                srcs, got = _send_wait("gather_wait_w_in_0", *w_in_state, g_routes[:1], g_sems[0], after, ks)
                got = _place_own("gather_own_w_in", srcs, got, g_routes[:1])
                w_in_state[:] = [srcs, got]
            else:
                if (name, part) not in passing:
                    hint(name, after)
                got = _forward_wait(f"gather_got_w_in_{part}", w_in_state[1], g_routes[:1], passing[name, part], after, ks)
                w_in_state[1] = got
            k = ks[-1]
            base = 4 * (x_ ^ (k >> 2)) + 2 * (y_ ^ ((k >> 1) & 1))
            return got[0], base.astype(jnp.int32).reshape(1)
        if name not in gathered:
            gi = next(i for i, grp in enumerate(_GATHER_GROUPS) if name in grp)
            if _GROUP_TWO_LEVEL[gi]:
                hint(name, after)
                sems, got, routes = passing[gi]
                full = _forward_wait(f"gather_got_{gi}", got, routes, sems, after, _SAME_CORE)
            else:
                ts = groups[n_parts + gi][0]
                routes = [g_routes[t] for t in ts]
                srcs, got = _send_wait(f"gather_wait_{gi}", [g_srcs[t] for t in ts], [g_lands[t] for t in ts], routes,
                                       g_sems[n_parts + gi], after)
                full = _place_own(f"gather_own_{gi}", srcs, got, routes)
            gathered.update(zip(_GATHER_GROUPS[gi], full))
        return gathered[name]

    sent = {}

    def emit(name, grad):
        k = _MATRICES[name]
        size = w[name].shape[axis[k]]
        route = _Route("scatter", k, size)
        shp = list(grad.shape)
        shp[axis[k]] = size
        land = lax.empty((_N_DEV, *shp), grad.dtype)
        sems, srcs, lnds, token = _send_start("grad_start_" + name, [grad], [land], [route], [([0], _ALL_PEERS)])
        sent[name] = (srcs, lnds, [route], sems[0])
        return token

    p = {n: w[n].reshape(1, -1) for n in _VECTORS}
    loss_t, grad_x, gs = _local_step(x, target, p, get, emit, hint)
    loss = lax.psum(loss_t, ("x", "y", "c"))

    small_names = list(_VECTORS) + list(_CONVS)
    small_parts = _exchange([], [], [], [gs[n] for n in small_names])

    out = {}
    mats = list(_MATRICES)
    for n in mats:
        srcs, lnds, routes, sems = sent[n]
        srcs, got = _send_wait("grad_wait_" + n, srcs, lnds, routes, sems, grad_x)
        parts = _place_own("grad_own_" + n, srcs, got, routes)[0]
        shp = w[n].shape
        r2 = (-1, shp[-1])
        res = _adamw("adamw_" + n, w[n].reshape(r2), m[n].reshape(r2), v[n].reshape(r2),
                     parts.reshape((_N_DEV,) + w[n].reshape(r2).shape))
        out[n] = [a.reshape(shp) for a in res]
    two_d = lambda a: a.reshape(-1, a.shape[-1])
    res = _adamw_small(me.astype(jnp.int32).reshape(1), [two_d(w[n]) for n in small_names], [two_d(m[n]) for n in small_names],
                       [two_d(v[n]) for n in small_names], small_parts)
    for t, n in enumerate(small_names):
        out[n] = [a.reshape(w[n].shape) for a in res[4 * t:4 * t + 4]]
    return loss, grad_x[None], out


def kernel(x, g_mix, w_in, b_gate, w_pool, pool_scale, lru_conv_w, lru_conv_b, w_a, b_a, w_i, b_i, lru_lambda, w_pool_proj, w_lru_proj, w_out, g_mlp, w_up, ffn_conv_w, ffn_conv_b, w_down, g_final, loss_target, m_g_mix, m_w_in, m_b_gate, m_w_pool, m_pool_scale, m_lru_conv_w, m_lru_conv_b, m_w_a, m_b_a, m_w_i, m_b_i, m_lru_lambda, m_w_pool_proj, m_w_lru_proj, m_w_out, m_g_mlp, m_w_up, m_ffn_conv_w, m_ffn_conv_b, m_w_down, m_g_final, v_g_mix, v_w_in, v_b_gate, v_w_pool, v_pool_scale, v_lru_conv_w, v_lru_conv_b, v_w_a, v_b_a, v_w_i, v_b_i, v_lru_lambda, v_w_pool_proj, v_w_lru_proj, v_w_out, v_g_mlp, v_w_up, v_ffn_conv_w, v_ffn_conv_b, v_w_down, v_g_final):
    given = dict(locals())
    orig = {n: given[n].shape for n in _WEIGHTS}

    def squeeze(a, n):
        return a if n == "g_final" else a[0]

    w = {n: squeeze(given[n], n) for n in _WEIGHTS}
    m = {n: squeeze(given["m_" + n], n) for n in _WEIGHTS}
    v = {n: squeeze(given["v_" + n], n) for n in _WEIGHTS}
    for d in (w, m, v):
        d["g_final"] = d["g_final"].reshape(1, -1)
    loss, grad_x, out = _step(x, loss_target, w, m, v)
    res = [loss, grad_x]
    for k in range(4):
        res += [out[n][k].reshape(orig[n]) for n in _WEIGHTS]
    return tuple(res)
```

```python
import functools

import jax
import jax.numpy as jnp
from jax import lax
from jax.experimental import pallas as pl
from jax.experimental.pallas import tpu as pltpu

f32 = jnp.float32
bf16 = jnp.bfloat16

_EPS = 1e-6
_LRU_C = 8.0
_POOL_WINDOWS = (2, 4, 8, 16)
_POOL_HALO = 16
_CONV_HALO = 8
_GELU_C0 = 0.7978845608028654
_GELU_C1 = 0.044715
_ADAM_LR, _ADAM_B1, _ADAM_B2, _ADAM_EPS, _ADAM_WD, _ADAM_STEP = 0.001, 0.9, 0.999, 1e-08, 0.01, 10
_N_DEV = 8
_LANE = 128
_VMEM_LIMIT = 60 * 1024 * 1024

_TM = 512
_TM_SMALL = 512
_TC = 256
_TK_T = 1024
_TN_MAX = 1536
_CW = 1024
_TK_DOWN = 1536
_TK_IN = 2304
_TK_UP = 2048


def _cparams(*sem):
    return pltpu.CompilerParams(dimension_semantics=tuple(sem), vmem_limit_bytes=_VMEM_LIMIT)


def _tile(n, want, mult=1):
    if n <= want:
        return n
    t = want - want % mult
    while n % t:
        t -= mult
    return t


def _gelu(x):
    t = jnp.tanh(x * (_GELU_C0 + (_GELU_C0 * _GELU_C1) * (x * x)))
    return x * (0.5 + 0.5 * t)


def _gelu_both(x):
    x2 = x * x
    t = jnp.tanh(x * (_GELU_C0 + (_GELU_C0 * _GELU_C1) * x2))
    h = 0.5 + 0.5 * t
    return x * h, h + (x * (1.0 - t * t)) * (0.5 * _GELU_C0 + (1.5 * _GELU_C0 * _GELU_C1) * x2)


def _fold8(x):
    out = x[0:8]
    for r in range(8, x.shape[0], 8):
        out = out + x[r:r + 8]
    return out


def _sigmoid(x):
    return jax.nn.sigmoid(x)


def _dot(a, b):
    return jnp.dot(a, b, preferred_element_type=f32)


def _dot_nt(a, b):
    return lax.dot_general(a, b, (((1,), (1,)), ((), ())), preferred_element_type=f32)


def _dot_tn(a, b):
    return lax.dot_general(a, b, (((0,), (0,)), ((), ())), preferred_element_type=f32)


def _rms_rows(x_ref, g_ref, h_ref, n_rows, n_cols):
    rr = _tile(n_rows, _SLAB_ROWS)
    cg = _tile(n_cols, _SLAB_COLS)
    for r0 in range(0, n_rows, rr):
        rows = pl.ds(r0, rr)
        x = x_ref[rows, :]
        r = lax.rsqrt(jnp.mean(x * x, axis=-1, keepdims=True) + _EPS)
        for c0 in range(0, n_cols, cg):
            cols = slice(c0, c0 + cg)
            h_ref[rows, cols] = (x_ref[rows, cols] * r * g_ref[:, cols]).astype(bf16)


def _matmul(name, mode, operands, in_specs, out_shape, out_spec, grid, acc_shape):
    dot = {"nn": _dot, "nt": _dot_nt, "tn": _dot_tn}[mode]
    nk = grid[2]

    def body_whole(a_ref, b_ref, o_ref):
        o_ref[...] = dot(a_ref[...], b_ref[...]).astype(o_ref.dtype)

    def body(a_ref, b_ref, o_ref, acc_ref):
        k = pl.program_id(2)

        @pl.when(k == 0)
        def _():
            acc_ref[...] = dot(a_ref[...], b_ref[...])

        @pl.when((k > 0) & (k < nk - 1))
        def _():
            acc_ref[...] += dot(a_ref[...], b_ref[...])

        @pl.when(k == nk - 1)
        def _():
            o_ref[...] = (acc_ref[...] + dot(a_ref[...], b_ref[...])).astype(o_ref.dtype)

    return pl.pallas_call(
        body_whole if nk == 1 else body, name=name, grid=grid, in_specs=in_specs, out_specs=out_spec, out_shape=out_shape,
        scratch_shapes=[] if nk == 1 else [pltpu.VMEM(acc_shape, f32)],
        compiler_params=_cparams("parallel", "parallel", "arbitrary"),
    )(*operands)


def _mm_nn(name, a, b, out_dtype, tm_want=None):
    M, K = a.shape
    N = b.shape[1]
    tm, tn = _tile(M, tm_want or _TM), _tile(N, _TN_MAX)
    return _matmul(
        name, "nn", (a, b),
        [pl.BlockSpec((tm, K), lambda i, j, k: (i, 0)), pl.BlockSpec((K, tn), lambda i, j, k: (0, j))],
        jax.ShapeDtypeStruct((M, N), out_dtype), pl.BlockSpec((tm, tn), lambda i, j, k: (i, j)),
        (M // tm, N // tn, 1), (tm, tn))


def _mm_nt(name, a, b, out_dtype):
    M, K = a.shape
    N = b.shape[0]
    tm, tn = _tile(M, 2 * _TM), _tile(N, _TN_MAX)
    return _matmul(
        name, "nt", (a, b),
        [pl.BlockSpec((tm, K), lambda i, j, k: (i, 0)), pl.BlockSpec((tn, K), lambda i, j, k: (j, 0))],
        jax.ShapeDtypeStruct((M, N), out_dtype), pl.BlockSpec((tm, tn), lambda i, j, k: (i, j)),
        (M // tm, N // tn, 1), (tm, tn))


def _mm_tn(name, a, b, out_dtype, tm_want=2048):
    T, M = a.shape
    N = b.shape[1]
    tm, tn, tk = _tile(M, tm_want), _tile(N, _TN_MAX), _tile(T, _TK_T)
    return _matmul(
        name, "tn", (a, b),
        [pl.BlockSpec((tk, tm), lambda i, j, k: (k, i)), pl.BlockSpec((tk, tn), lambda i, j, k: (k, j))],
        jax.ShapeDtypeStruct((M, N), out_dtype), pl.BlockSpec((tm, tn), lambda i, j, k: (i, j)),
        (M // tm, N // tn, T // tk), (tm, tn))


def _in_proj_first(x, g_mix, w_in, base, n_tiles):
    T, D = x.shape
    NI = w_in.shape[1]
    tm, tn = _tile(T, 2 * _TM), NI // _N_DEV

    def body(base_ref, x_ref, g_ref, w_ref, h_ref, o_ref):
        @pl.when(pl.program_id(1) == 0)
        def _():
            _rms_rows(x_ref, g_ref, h_ref, tm, D)

        o_ref[...] = _dot(h_ref[...], w_ref[...])

    return pl.pallas_call(
        body, name="in_proj_0",
        grid_spec=pltpu.PrefetchScalarGridSpec(
            num_scalar_prefetch=1, grid=(T // tm, n_tiles),
            in_specs=[pl.BlockSpec((tm, D), lambda i, j, b: (i, 0)), pl.BlockSpec((1, D), lambda i, j, b: (0, 0)),
                      pl.BlockSpec((D, tn), lambda i, j, b: (0, b[0] + j))],
            out_specs=[pl.BlockSpec((tm, D), lambda i, j, b: (i, 0)), pl.BlockSpec((tm, tn), lambda i, j, b: (i, b[0] + j))]),
        out_shape=[jax.ShapeDtypeStruct((T, D), bf16), jax.ShapeDtypeStruct((T, NI), f32)],
        compiler_params=_cparams("parallel", "arbitrary"),
    )(base, x, g_mix, w_in)


def _in_proj_more(name, h1, w_in, proj, base, n_tiles):
    T, D = h1.shape
    NI = w_in.shape[1]
    tm, tn = _tile(T, 2 * _TM), NI // _N_DEV

    def body(base_ref, h_ref, w_ref, proj_ref, o_ref):
        o_ref[...] = _dot(h_ref[...], w_ref[...])

    return pl.pallas_call(
        body, name=name,
        grid_spec=pltpu.PrefetchScalarGridSpec(
            num_scalar_prefetch=1, grid=(T // tm, n_tiles),
            in_specs=[pl.BlockSpec((tm, D), lambda i, j, b: (i, 0)), pl.BlockSpec((D, tn), lambda i, j, b: (0, b[0] + j)),
                      pl.BlockSpec(memory_space=pl.ANY)],
            out_specs=pl.BlockSpec((tm, tn), lambda i, j, b: (i, b[0] + j))),
        out_shape=jax.ShapeDtypeStruct((T, NI), f32),
        input_output_aliases={3: 0},
        compiler_params=_cparams("parallel", "arbitrary"),
    )(base, h1, w_in, proj)


def _window_means(ext_ref, row0, tc, gw):
    H = _POOL_HALO
    t_glob = row0 + lax.broadcasted_iota(jnp.int32, (tc, 1), 0)
    out = []
    for g, w in enumerate(_POOL_WINDOWS):
        s = ext_ref[:, g * gw:(g + 1) * gw]
        st = 1
        while st < w:
            s = s + pltpu.roll(s, st, 0)
            st *= 2
        cnt = jnp.minimum(t_glob + 1, w).astype(f32)
        out.append((s[H:, :] / cnt, ext_ref[pl.ds(H, tc), g * gw:(g + 1) * gw]))
    return out


def _pool_fwd(proj, w_pool, pool_scale):
    T = proj.shape[0]
    G, gw, _ = w_pool.shape
    PW = G * gw
    tc = _tile(T, _TC)
    H = _POOL_HALO

    def body(u_ref, w_ref, s_ref, y_ref, ext_ref):
        i = pl.program_id(0)

        @pl.when(i == 0)
        def _():
            ext_ref[pl.ds(0, H), :] = jnp.zeros((H, PW), f32)

        ext_ref[pl.ds(H, tc), :] = u_ref[...]
        for g, (m, u) in enumerate(_window_means(ext_ref, i * tc, tc, gw)):
            d = (m - u).astype(bf16)
            y = _dot(d, w_ref[g]) * s_ref[:, g * gw:(g + 1) * gw]
            y_ref[:, g * gw:(g + 1) * gw] = y.astype(bf16)
        ext_ref[pl.ds(0, H), :] = ext_ref[pl.ds(tc, H), :]

    return pl.pallas_call(
        body, name="pool_fwd", grid=(T // tc,),
        in_specs=[pl.BlockSpec((tc, PW), lambda i: (i, 0)), pl.BlockSpec((G, gw, gw), lambda i: (0, 0, 0)),
                  pl.BlockSpec((1, PW), lambda i: (0, 0))],
        out_specs=pl.BlockSpec((tc, PW), lambda i: (i, 0)),
        out_shape=jax.ShapeDtypeStruct((T, PW), bf16),
        scratch_shapes=[pltpu.VMEM((H + tc, PW), f32)],
        compiler_params=_cparams("arbitrary"),
    )(proj, w_pool, pool_scale)


def _softplus(z):
    return jnp.maximum(z, 0.0) + jnp.log1p(jnp.exp(-jnp.abs(z)))


def _causal_conv(ext_ref, cw_ref, cb_ref, n, K, cols=slice(None), r0=0):
    H = _CONV_HALO
    v = cb_ref[:, cols] + cw_ref[K - 1:K, cols] * ext_ref[pl.ds(H + r0, n), cols]
    for k in range(K - 1):
        v = v + cw_ref[k:k + 1, cols] * ext_ref[pl.ds(H + r0 - (K - 1 - k), n), cols]
    return v


_ROWS = 8
_SLAB_ROWS = 16
_SLAB_COLS = 512


def _slabs(n_rows, n_cols, reverse=False):
    cg = _tile(n_cols, _SLAB_COLS)
    rr = _tile(n_rows, _SLAB_ROWS)
    starts = range(0, n_rows, rr)
    for c0 in range(0, n_cols, cg):
        for r0 in (reversed(starts) if reverse else starts):
            yield slice(c0, c0 + cg), pl.ds(r0, rr)


def _scan_rows(a_ref, b_ref, out_ref, carry_ref, n_rows, n_cols, reverse=False):
    cg = _tile(n_cols, _SLAB_COLS)
    row = lax.broadcasted_iota(jnp.int32, (8, cg), 0)
    steps = [(8 - sh, row < 8 - sh) if reverse else (sh, row >= sh) for sh in (1, 2, 4)]
    tiles = list(range(0, n_rows, 8))
    for c0 in range(0, n_cols, cg):
        cols = slice(c0, c0 + cg)
        h_in = carry_ref[0:1, cols]
        for r0 in (reversed(tiles) if reverse else tiles):
            rows = pl.ds(r0, 8)
            a, b = a_ref[rows, cols], b_ref[rows, cols]
            for shift, inside in steps:
                b = jnp.where(inside, a * pltpu.roll(b, shift, 0) + b, b)
                a = jnp.where(inside, a * pltpu.roll(a, shift, 0), a)
            h = a * h_in + b
            out_ref[rows, cols] = h
            h_in = h[0:1, :] if reverse else h[7:8, :]
        carry_ref[0:1, cols] = h_in


def _slabs_with_sums(n_rows, n_cols, n_sums, visit, flush, reverse=False):
    cg = _tile(n_cols, _SLAB_COLS)
    rr = _tile(n_rows, _SLAB_ROWS)
    starts = list(range(0, n_rows, rr))
    for c0 in range(0, n_cols, cg):
        cols = slice(c0, c0 + cg)
        sums = [jnp.zeros((8, cg), f32) for _ in range(n_sums)]
        for r0 in (reversed(starts) if reverse else starts):
            sums = visit(cols, pl.ds(r0, rr), sums)
        flush(cols, [jnp.sum(s, axis=0, keepdims=True) for s in sums])


def _lru_fwd(proj, D, PW, conv_w, conv_b, w_a, b_a, w_i, b_i, lam):
    T = proj.shape[0]
    NB, bw, _ = w_a.shape
    K = 4
    tc = _tile(T, _TC)
    H = _CONV_HALO
    hb = D // 2
    assert PW == hb and conv_w.shape[0] == 8

    def body(u0_ref, u1_ref, g0_ref, g1_ref, cw_ref, cb_ref, wa_ref, ba_ref, wi_ref, bi_ref, lam_ref,
             y_ref, hs_ref, ext_ref, a_scr, b_scr, v_scr, hc_scr):
        c = pl.program_id(0)

        @pl.when(c == 0)
        def _():
            ext_ref[pl.ds(0, H), :] = jnp.zeros((H, D), f32)
            hc_scr[...] = jnp.zeros_like(hc_scr)

        ext_ref[pl.ds(H, tc), 0:hb] = u0_ref[...]
        ext_ref[pl.ds(H, tc), hb:D] = u1_ref[...]
        sp = _softplus(-lam_ref[...])
        for cols, rows in _slabs(tc, D):
            v_scr[rows, cols] = _causal_conv(ext_ref, cw_ref, cb_ref, rows.size, K, cols, rows.start)
        for b in range(NB):
            cols = slice(b * bw, (b + 1) * bw)
            vb = v_scr[:, cols].astype(bf16)
            a_scr[:, cols] = _dot(vb, wa_ref[b])
            b_scr[:, cols] = _dot(vb, wi_ref[b])
        for cols, rows in _slabs(tc, D):
            r = _sigmoid(a_scr[rows, cols] + ba_ref[:, cols])
            i = _sigmoid(b_scr[rows, cols] + bi_ref[:, cols])
            a = jnp.exp(-_LRU_C * r * sp[:, cols])
            a_scr[rows, cols] = a
            b_scr[rows, cols] = jnp.sqrt(1.0 - a * a) * (i * v_scr[rows, cols])

        _scan_rows(a_scr, b_scr, hs_ref, hc_scr, tc, D)
        for cols, rows in _slabs(tc, D):
            g_ref, gcols = (g0_ref, cols) if cols.start < hb else (g1_ref, slice(cols.start - hb, cols.stop - hb))
            y_ref[rows, cols] = (hs_ref[rows, cols] * _gelu(g_ref[rows, gcols])).astype(bf16)
        ext_ref[pl.ds(0, H), :] = ext_ref[pl.ds(tc, H), :]

    vec = pl.BlockSpec((1, D), lambda c: (0, 0))
    wspec = pl.BlockSpec((NB, bw, bw), lambda c: (0, 0, 0))
    return pl.pallas_call(
        body, name="lru_fwd", grid=(T // tc,),
        in_specs=[pl.BlockSpec((tc, hb), lambda c: (c, 1)), pl.BlockSpec((tc, hb), lambda c: (c, 2)),
                  pl.BlockSpec((tc, hb), lambda c: (c, 3)), pl.BlockSpec((tc, hb), lambda c: (c, 4)),
                  pl.BlockSpec((8, D), lambda c: (0, 0)), vec, wspec, vec, wspec, vec, vec],
        out_specs=[pl.BlockSpec((tc, D), lambda c: (c, 0)), pl.BlockSpec((tc, D), lambda c: (c, 0))],
        out_shape=[jax.ShapeDtypeStruct((T, D), bf16), jax.ShapeDtypeStruct((T, D), f32)],
        scratch_shapes=[pltpu.VMEM((H + tc, D), f32), pltpu.VMEM((tc, D), f32), pltpu.VMEM((tc, D), f32),
                        pltpu.VMEM((tc, D), f32), pltpu.VMEM((8, D), f32)],
        compiler_params=_cparams("arbitrary"),
    )(proj, proj, proj, proj, conv_w, conv_b, w_a, b_a, w_i, b_i, lam)


def _merge_fwd(y_pool, y_lru, w_pp, w_lp, proj, b_gate):
    T, PW = y_pool.shape
    D = y_lru.shape[1]
    tm, tn = _tile(T, _TM), _tile(D, PW)
    nj = D // tn
    off = (PW + 2 * D) // tn

    def body(yp_ref, yl_ref, wp_ref, wl_ref, l0_ref, l1_ref, b0_ref, b1_ref, p_ref, q_ref, m_ref, p_scr, q_scr):
        p_scr[...] = _dot(yp_ref[...], wp_ref[...])
        q_scr[...] = _dot(yl_ref[...], wl_ref[...])
        for cols, rows in _slabs(tm, tn):
            p, q = p_scr[rows, cols], q_scr[rows, cols]
            g0 = _sigmoid(l0_ref[rows, cols] + b0_ref[:, cols])
            g1 = _sigmoid(l1_ref[rows, cols] + b1_ref[:, cols])
            m_ref[rows, cols] = (g0 * p + g1 * q).astype(bf16)
            p_ref[rows, cols] = p.astype(bf16)
            q_ref[rows, cols] = q.astype(bf16)

    tile = pl.BlockSpec((tm, tn), lambda j, i: (i, j))
    return pl.pallas_call(
        body, name="merge_fwd", grid=(nj, T // tm),
        in_specs=[pl.BlockSpec((tm, PW), lambda j, i: (i, 0)), pl.BlockSpec((tm, D), lambda j, i: (i, 0)),
                  pl.BlockSpec((PW, tn), lambda j, i: (0, j)), pl.BlockSpec((D, tn), lambda j, i: (0, j)),
                  pl.BlockSpec((tm, tn), lambda j, i: (i, off + j)), pl.BlockSpec((tm, tn), lambda j, i: (i, off + nj + j)),
                  pl.BlockSpec((1, tn), lambda j, i: (0, j)), pl.BlockSpec((1, tn), lambda j, i: (0, nj + j))],
        out_specs=[tile, tile, tile],
        out_shape=[jax.ShapeDtypeStruct((T, D), bf16), jax.ShapeDtypeStruct((T, D), bf16), jax.ShapeDtypeStruct((T, D), bf16)],
        scratch_shapes=[pltpu.VMEM((tm, tn), f32), pltpu.VMEM((tm, tn), f32)],
        compiler_params=_cparams("parallel", "arbitrary"),
    )(y_pool, y_lru, w_pp, w_lp, proj, proj, b_gate, b_gate)


def _out_proj(merged, w_out, x, g_mlp):
    T, D = x.shape
    tm = _tile(T, _TM_SMALL)

    def body(m_ref, w_ref, x_ref, g_ref, x2_ref, h2_ref):
        x2_ref[...] = x_ref[...] + _dot(m_ref[...], w_ref[...])
        _rms_rows(x2_ref, g_ref, h2_ref, tm, D)

    row = pl.BlockSpec((tm, D), lambda i: (i, 0))
    return pl.pallas_call(
        body, name="out_proj", grid=(T // tm,),
        in_specs=[row, pl.BlockSpec((D, D), lambda i: (0, 0)), row, pl.BlockSpec((1, D), lambda i: (0, 0))],
        out_specs=[row, row],
        out_shape=[jax.ShapeDtypeStruct((T, D), f32), jax.ShapeDtypeStruct((T, D), bf16)],
        compiler_params=_cparams("parallel"),
    )(merged, w_out, x, g_mlp)


def _ffn_fwd(up, conv_w, conv_b):
    T, F2 = up.shape
    F = F2 // 2
    K = 3
    tc = _tile(T, 4 * _TC)
    cw = _tile(F, _CW)
    ns = F // cw
    H = _CONV_HALO

    def body(gp_ref, val_ref, cw_ref, cb_ref, z_ref, ext_ref):
        @pl.when(pl.program_id(1) == 0)
        def _():
            ext_ref[pl.ds(0, H), :] = jnp.zeros((H, cw), f32)

        ext_ref[pl.ds(H, tc), :] = gp_ref[...]
        for cols, rows in _slabs(tc, cw):
            c = _causal_conv(ext_ref, cw_ref, cb_ref, rows.size, K, cols, rows.start)
            z_ref[rows, cols] = (_gelu(c) * val_ref[rows, cols]).astype(bf16)
        ext_ref[pl.ds(0, H), :] = ext_ref[pl.ds(tc, H), :]

    return pl.pallas_call(
        body, name="ffn_fwd", grid=(ns, T // tc),
        in_specs=[pl.BlockSpec((tc, cw), lambda s, c: (c, s)), pl.BlockSpec((tc, cw), lambda s, c: (c, ns + s)),
                  pl.BlockSpec((8, cw), lambda s, c: (0, s)), pl.BlockSpec((1, cw), lambda s, c: (0, s))],
        out_specs=pl.BlockSpec((tc, cw), lambda s, c: (c, s)),
        out_shape=jax.ShapeDtypeStruct((T, F), bf16),
        scratch_shapes=[pltpu.VMEM((H + tc, cw), f32)],
        compiler_params=_cparams("parallel", "arbitrary"),
    )(up, up, conv_w, conv_b)


def _down_loss(z, w_down, x2, target, g_final):
    T, F = z.shape
    D = x2.shape[1]
    tm, tk = _tile(T, _TM), _tile(F, _TK_DOWN)
    nk = F // tk

    def body(z_ref, w_ref, x2_ref, t_ref, g_ref, dx_ref, dxb_ref, loss_ref, dg_ref, acc_ref):
        i, k = pl.program_id(0), pl.program_id(1)

        @pl.when(k == 0)
        def _():
            acc_ref[...] = x2_ref[...]

        @pl.when((i == 0) & (k == 0))
        def _():
            loss_ref[...] = jnp.zeros_like(loss_ref)
            dg_ref[...] = jnp.zeros_like(dg_ref)

        acc_ref[...] += _dot(z_ref[...], w_ref[...])

        @pl.when(k == nk - 1)
        def _():
            g = g_ref[...]
            sq = jnp.zeros((_ROWS, 1), f32)
            dgs = jnp.zeros((_ROWS, D), f32)
            for r0 in range(0, tm, _ROWS):
                rows = pl.ds(r0, _ROWS)
                x3 = acc_ref[rows, :]
                r = lax.rsqrt(jnp.mean(x3 * x3, axis=-1, keepdims=True) + _EPS)
                xr = x3 * r
                e = xr * g - t_ref[rows, :]
                sq = sq + jnp.sum(e * e, axis=-1, keepdims=True)
                dy = e * (1.0 / D)
                gy = dy * g
                dx = r * gy - x3 * ((r * r * r) * jnp.mean(x3 * gy, axis=-1, keepdims=True))
                dgs = dgs + dy * xr
                dx_ref[rows, :] = dx
                dxb_ref[rows, :] = dx.astype(bf16)
            loss_ref[...] += (0.5 / D) * jnp.sum(sq)
            dg_ref[...] += jnp.sum(dgs, axis=0, keepdims=True)

    row = pl.BlockSpec((tm, D), lambda i, k: (i, 0))
    vec = pl.BlockSpec((1, D), lambda i, k: (0, 0))
    return pl.pallas_call(
        body, name="down_loss", grid=(T // tm, nk),
        in_specs=[pl.BlockSpec((tm, tk), lambda i, k: (i, k)), pl.BlockSpec((tk, D), lambda i, k: (k, 0)), row, row, vec],
        out_specs=[row, row, pl.BlockSpec((8, _LANE), lambda i, k: (0, 0)), vec],
        out_shape=[jax.ShapeDtypeStruct((T, D), f32), jax.ShapeDtypeStruct((T, D), bf16),
                   jax.ShapeDtypeStruct((8, _LANE), f32), jax.ShapeDtypeStruct((1, D), f32)],
        scratch_shapes=[pltpu.VMEM((tm, D), f32)],
        compiler_params=_cparams("arbitrary", "arbitrary"),
    )(z, w_down, x2, target, g_final)


def _ffn_bwd(dz, up, conv_w, conv_b):
    T, F = dz.shape
    K = 3
    tc = _tile(T, 4 * _TC)
    cw = _tile(F, _CW)
    ns, nt = F // cw, T // tc
    H = _CONV_HALO

    def body(dz_ref, gp_ref, val_ref, gph_ref, cw_ref, cb_ref, dup_ref, dcw_ref, dcb_ref, ext_ref, dext_ref):
        j = pl.program_id(1)
        first = j == nt - 1

        @pl.when(j == 0)
        def _():
            dext_ref[pl.ds(tc, H), :] = jnp.zeros((H, cw), f32)
            dcw_ref[...] = jnp.zeros_like(dcw_ref)
            dcb_ref[...] = jnp.zeros_like(dcb_ref)

        ext_ref[pl.ds(0, H), :] = jnp.where(first, 0.0, gph_ref[...])
        ext_ref[pl.ds(H, tc), :] = gp_ref[...]
        def visit(cols, rows, sums):
            r0, n = rows.start, rows.size
            gp = ext_ref[pl.ds(H + r0, n), cols]
            c = _causal_conv(ext_ref, cw_ref, cb_ref, n, K, cols, r0)
            ge, gg = _gelu_both(c)
            dzv = dz_ref[rows, cols].astype(f32)
            dup_ref[1, rows, cols] = (dzv * ge).astype(bf16)
            dc = dzv * val_ref[rows, cols] * gg
            dext_ref[rows, cols] = dc
            dgp = cw_ref[K - 1:K, cols] * dc
            new = [None] * K + [sums[K] + _fold8(dc)]
            new[K - 1] = sums[K - 1] + _fold8(gp * dc)
            for k in range(K - 1):
                sh = dext_ref[pl.ds(r0 + K - 1 - k, n), cols]
                dgp = dgp + cw_ref[k:k + 1, cols] * sh
                new[k] = sums[k] + _fold8(gp * sh)
            dup_ref[0, rows, cols] = dgp.astype(bf16)
            return new

        def flush(cols, totals):
            for k in range(K):
                dcw_ref[k:k + 1, cols] += totals[k]
            dcb_ref[:, cols] += totals[K]

        _slabs_with_sums(tc, cw, K + 1, visit, flush, reverse=True)
        dext_ref[pl.ds(tc, H), :] = dext_ref[pl.ds(0, H), :]

    hblk = tc // H
    return pl.pallas_call(
        body, name="ffn_bwd", grid=(ns, nt),
        in_specs=[pl.BlockSpec((tc, cw), lambda s, j: (nt - 1 - j, s)),
                  pl.BlockSpec((tc, cw), lambda s, j: (nt - 1 - j, s)),
                  pl.BlockSpec((tc, cw), lambda s, j: (nt - 1 - j, ns + s)),
                  pl.BlockSpec((H, cw), lambda s, j: (jnp.maximum((nt - 1 - j) * hblk - 1, 0), s)),
                  pl.BlockSpec((8, cw), lambda s, j: (0, s)), pl.BlockSpec((1, cw), lambda s, j: (0, s))],
        out_specs=[pl.BlockSpec((2, tc, cw), lambda s, j: (0, nt - 1 - j, s)),
                   pl.BlockSpec((8, cw), lambda s, j: (0, s)), pl.BlockSpec((1, cw), lambda s, j: (0, s))],
        out_shape=[jax.ShapeDtypeStruct((2, T, F), bf16), jax.ShapeDtypeStruct((8, F), f32), jax.ShapeDtypeStruct((1, F), f32)],
        scratch_shapes=[pltpu.VMEM((H + tc, cw), f32), pltpu.VMEM((tc + H, cw), f32)],
        compiler_params=_cparams("parallel", "arbitrary"),
    )(dz, up, up, up, conv_w, conv_b)


def _norm_bwd_matmul(name, a, a_spec, nk, w, w_spec, x, g, dres):
    T, D = x.shape
    tm = a_spec.block_shape[-2]

    def body(a_ref, w_ref, x_ref, g_ref, dr_ref, dx_ref, dxb_ref, dg_ref, acc_ref):
        i, k = pl.program_id(0), pl.program_id(1)

        @pl.when((i == 0) & (k == 0))
        def _():
            dg_ref[...] = jnp.zeros_like(dg_ref)

        @pl.when(k == 0)
        def _():
            acc_ref[...] = _dot_nt(a_ref[...], w_ref[...])

        @pl.when(k > 0)
        def _():
            acc_ref[...] += _dot_nt(a_ref[...], w_ref[...])

        @pl.when(k == nk - 1)
        def _():
            g = g_ref[...]
            dgs = jnp.zeros((_ROWS, D), f32)
            for r0 in range(0, tm, _ROWS):
                rows = pl.ds(r0, _ROWS)
                x = x_ref[rows, :]
                dh = acc_ref[rows, :]
                r = lax.rsqrt(jnp.mean(x * x, axis=-1, keepdims=True) + _EPS)
                gy = dh * g
                dx = dr_ref[rows, :] + (r * gy - x * ((r * r * r) * jnp.mean(x * gy, axis=-1, keepdims=True)))
                dgs = dgs + dh * (x * r)
                dx_ref[rows, :] = dx
                dxb_ref[rows, :] = dx.astype(bf16)
            dg_ref[...] += jnp.sum(dgs, axis=0, keepdims=True)

    row = pl.BlockSpec((tm, D), lambda i, k: (i, 0))
    vec = pl.BlockSpec((1, D), lambda i, k: (0, 0))
    return pl.pallas_call(
        body, name=name, grid=(T // tm, nk),
        in_specs=[a_spec, w_spec, row, vec, row],
        out_specs=[row, row, vec],
        out_shape=[jax.ShapeDtypeStruct((T, D), f32), jax.ShapeDtypeStruct((T, D), bf16), jax.ShapeDtypeStruct((1, D), f32)],
        scratch_shapes=[pltpu.VMEM((tm, D), f32)],
        compiler_params=_cparams("arbitrary", "arbitrary"),
    )(a, w, x, g, dres)


def _merge_bwd(dx2b, w_out, p, q, proj, b_gate, PW):
    T, D = p.shape
    tm, tn = _tile(T, _TM), _tile(D, PW)
    nj = D // tn
    off = (PW + 2 * D) // tn

    def body(dx_ref, w_ref, p_ref, q_ref, l0_ref, l1_ref, b0_ref, b1_ref, dp_ref, dq_ref, dl0_ref, dl1_ref, db0_ref, db1_ref,
             dm_ref):
        @pl.when(pl.program_id(1) == 0)
        def _():
            db0_ref[...] = jnp.zeros_like(db0_ref)
            db1_ref[...] = jnp.zeros_like(db1_ref)

        dm_ref[...] = _dot_nt(dx_ref[...], w_ref[...])

        def visit(cols, rows, sums):
            dm = dm_ref[rows, cols]
            g0 = _sigmoid(l0_ref[rows, cols] + b0_ref[:, cols])
            g1 = _sigmoid(l1_ref[rows, cols] + b1_ref[:, cols])
            dp_ref[rows, cols] = (g0 * dm).astype(bf16)
            dq_ref[rows, cols] = (g1 * dm).astype(bf16)
            dl0 = dm * p_ref[rows, cols].astype(f32) * (g0 * (1.0 - g0))
            dl1 = dm * q_ref[rows, cols].astype(f32) * (g1 * (1.0 - g1))
            dl0_ref[rows, cols] = dl0.astype(bf16)
            dl1_ref[rows, cols] = dl1.astype(bf16)
            return [sums[0] + _fold8(dl0), sums[1] + _fold8(dl1)]

        def flush(cols, totals):
            db0_ref[:, cols] += totals[0]
            db1_ref[:, cols] += totals[1]

        _slabs_with_sums(tm, tn, 2, visit, flush)

    tile = pl.BlockSpec((tm, tn), lambda j, i: (i, j))
    vecj = pl.BlockSpec((1, tn), lambda j, i: (0, j))
    tb = jax.ShapeDtypeStruct((T, D), bf16)
    vb = jax.ShapeDtypeStruct((1, D), f32)
    return pl.pallas_call(
        body, name="merge_bwd", grid=(nj, T // tm),
        in_specs=[pl.BlockSpec((tm, D), lambda j, i: (i, 0)), pl.BlockSpec((tn, D), lambda j, i: (j, 0)), tile, tile,
                  pl.BlockSpec((tm, tn), lambda j, i: (i, off + j)), pl.BlockSpec((tm, tn), lambda j, i: (i, off + nj + j)),
                  vecj, pl.BlockSpec((1, tn), lambda j, i: (0, nj + j))],
        out_specs=[tile, tile, tile, tile, vecj, vecj],
        out_shape=[tb, tb, tb, tb, vb, vb],
        scratch_shapes=[pltpu.VMEM((tm, tn), f32)],
        compiler_params=_cparams("parallel", "arbitrary"),
    )(dx2b, w_out, p, q, proj, proj, b_gate, b_gate)


def _lru_bwd(dy, proj, hs, D, conv_w, conv_b, w_a, b_a, w_i, b_i, lam):
    T = dy.shape[0]
    NB, bw, _ = w_a.shape
    K = 4
    tc = _tile(T, _TC)
    nt = T // tc
    H = _CONV_HALO
    hb = D // 2

    def body(dy_ref, u0_ref, u1_ref, g0_ref, g1_ref, hs_ref, uh0_ref, uh1_ref, hh_ref,
             cw_ref, cb_ref, wa_ref, ba_ref, wi_ref, bi_ref, lam_ref,
             du_ref, dg_ref, dwa_ref, dwi_ref, dcw_ref, dvec_ref,
             ext_ref, hext_ref, dext_ref, q_ref, a_scr, r_scr, i_scr, v_scr, g_scr, car_scr):
        j = pl.program_id(0)
        first = j == nt - 1

        @pl.when(j == 0)
        def _():
            dext_ref[pl.ds(tc, H), :] = jnp.zeros((H, D), f32)
            car_scr[...] = jnp.zeros_like(car_scr)
            dwa_ref[...] = jnp.zeros_like(dwa_ref)
            dwi_ref[...] = jnp.zeros_like(dwi_ref)
            dcw_ref[...] = jnp.zeros_like(dcw_ref)
            dvec_ref[...] = jnp.zeros_like(dvec_ref)

        ext_ref[pl.ds(0, H), 0:hb] = jnp.where(first, 0.0, uh0_ref[...])
        ext_ref[pl.ds(0, H), hb:D] = jnp.where(first, 0.0, uh1_ref[...])
        ext_ref[pl.ds(H, tc), 0:hb] = u0_ref[...]
        ext_ref[pl.ds(H, tc), hb:D] = u1_ref[...]
        hext_ref[pl.ds(0, H), :] = jnp.where(first, 0.0, hh_ref[...])
        hext_ref[pl.ds(H, tc), :] = hs_ref[...]
        lamv = lam_ref[...]
        sp = _softplus(-lamv)

        for cols, rows in _slabs(tc, D):
            v_scr[rows, cols] = _causal_conv(ext_ref, cw_ref, cb_ref, rows.size, K, cols, rows.start)
        for b in range(NB):
            cols = slice(b * bw, (b + 1) * bw)
            vb = v_scr[:, cols].astype(bf16)
            r_scr[:, cols] = _dot(vb, wa_ref[b])
            i_scr[:, cols] = _dot(vb, wi_ref[b])
        for cols, rows in _slabs(tc, D):
            r = _sigmoid(r_scr[rows, cols] + ba_ref[:, cols])
            r_scr[rows, cols] = r
            i_scr[rows, cols] = _sigmoid(i_scr[rows, cols] + bi_ref[:, cols])
            a = jnp.exp(-_LRU_C * r * sp[:, cols])
            a_scr[rows, cols] = a
            g_ref, gcols = (g0_ref, cols) if cols.start < hb else (g1_ref, slice(cols.start - hb, cols.stop - hb))
            ge, gg = _gelu_both(g_ref[rows, gcols])
            dyv = dy_ref[rows, cols]
            dho = dyv * ge
            g_scr[rows, cols] = dho
            q_ref[rows, cols] = a * dho
            dg_ref[rows, cols] = (dyv * hs_ref[rows, cols] * gg).astype(bf16)

        q_ref[pl.ds(tc, 1), :] = car_scr[0:1, :]
        _scan_rows(a_scr, q_ref, q_ref, car_scr, tc, D, reverse=True)

        def gates(cols, rows, sums):
            g = g_scr[rows, cols] + q_ref[pl.ds(rows.start + 1, rows.size), cols]
            v, r, i, a = v_scr[rows, cols], r_scr[rows, cols], i_scr[rows, cols], a_scr[rows, cols]
            mult = jnp.sqrt(1.0 - a * a)
            h_prev = hext_ref[pl.ds(H - 1 + rows.start, rows.size), cols]
            gm = g * mult
            dext_ref[rows, cols] = gm * i
            dlog_a = (g * h_prev - g * (i * v) * (a / mult)) * a
            dpr = dlog_a * (-_LRU_C * sp[:, cols]) * (r * (1.0 - r))
            dpi = gm * v * (i * (1.0 - i))
            r_scr[rows, cols] = dpr
            i_scr[rows, cols] = dpi
            return [sums[0] + _fold8(dpr), sums[1] + _fold8(dpi), sums[2] + _fold8(dlog_a * (-_LRU_C * r))]

        def gates_flush(cols, totals):
            dvec_ref[1:2, cols] += totals[0]
            dvec_ref[2:3, cols] += totals[1]
            dvec_ref[3:4, cols] += totals[2] * (-_sigmoid(-lamv[:, cols]))

        _slabs_with_sums(tc, D, 3, gates, gates_flush)
        for b in range(NB):
            cols = slice(b * bw, (b + 1) * bw)
            dprb, dpib, vb = r_scr[:, cols].astype(bf16), i_scr[:, cols].astype(bf16), v_scr[:, cols].astype(bf16)
            dext_ref[pl.ds(0, tc), cols] += _dot_nt(dprb, wa_ref[b]) + _dot_nt(dpib, wi_ref[b])
            dwa_ref[b] += _dot_tn(vb, dprb)
            dwi_ref[b] += _dot_tn(vb, dpib)

        def conv_t(cols, rows, sums):
            r0, n = rows.start, rows.size
            u = ext_ref[pl.ds(H + r0, n), cols]
            dv = dext_ref[rows, cols]
            du = cw_ref[K - 1:K, cols] * dv
            new = [None] * K + [sums[K] + _fold8(dv)]
            new[K - 1] = sums[K - 1] + _fold8(u * dv)
            for k in range(K - 1):
                sh = dext_ref[pl.ds(r0 + K - 1 - k, n), cols]
                du = du + cw_ref[k:k + 1, cols] * sh
                new[k] = sums[k] + _fold8(u * sh)
            du_ref[rows, cols] = du.astype(bf16)
            return new

        def conv_t_flush(cols, totals):
            for k in range(K):
                dcw_ref[k:k + 1, cols] += totals[k]
            dvec_ref[0:1, cols] += totals[K]

        _slabs_with_sums(tc, D, K + 1, conv_t, conv_t_flush)
        dext_ref[pl.ds(tc, H), :] = dext_ref[pl.ds(0, H), :]

    hblk = tc // H
    rev = lambda j: nt - 1 - j
    halo = lambda j: jnp.maximum((nt - 1 - j) * hblk - 1, 0)
    vec = pl.BlockSpec((1, D), lambda j: (0, 0))
    wspec = pl.BlockSpec((NB, bw, bw), lambda j: (0, 0, 0))
    acc8 = pl.BlockSpec((8, D), lambda j: (0, 0))
    big = pltpu.VMEM((tc, D), f32)
    return pl.pallas_call(
        body, name="lru_bwd", grid=(nt,),
        in_specs=[pl.BlockSpec((tc, D), lambda j: (rev(j), 0)),
                  pl.BlockSpec((tc, hb), lambda j: (rev(j), 1)), pl.BlockSpec((tc, hb), lambda j: (rev(j), 2)),
                  pl.BlockSpec((tc, hb), lambda j: (rev(j), 3)), pl.BlockSpec((tc, hb), lambda j: (rev(j), 4)),
                  pl.BlockSpec((tc, D), lambda j: (rev(j), 0)),
                  pl.BlockSpec((H, hb), lambda j: (halo(j), 1)), pl.BlockSpec((H, hb), lambda j: (halo(j), 2)),
                  pl.BlockSpec((H, D), lambda j: (halo(j), 0)),
                  acc8, vec, wspec, vec, wspec, vec, vec],
        out_specs=[pl.BlockSpec((tc, D), lambda j: (rev(j), 0)), pl.BlockSpec((tc, D), lambda j: (rev(j), 0)),
                   wspec, wspec, acc8, acc8],
        out_shape=[jax.ShapeDtypeStruct((T, D), bf16), jax.ShapeDtypeStruct((T, D), bf16),
                   jax.ShapeDtypeStruct((NB, bw, bw), f32), jax.ShapeDtypeStruct((NB, bw, bw), f32),
                   jax.ShapeDtypeStruct((8, D), f32), jax.ShapeDtypeStruct((8, D), f32)],
        scratch_shapes=[pltpu.VMEM((H + tc, D), f32), pltpu.VMEM((H + tc, D), f32), pltpu.VMEM((tc + H, D), f32),
                        pltpu.VMEM((tc + H, D), f32), big, big, big, big, big, pltpu.VMEM((8, D), f32)],
        compiler_params=_cparams("arbitrary"),
    )(dy, proj, proj, proj, proj, hs, proj, proj, hs, conv_w, conv_b, w_a, b_a, w_i, b_i, lam)


def _pool_bwd(dy, proj, w_pool, pool_scale):
    T, PW = dy.shape
    G, gw, _ = w_pool.shape
    tc = _tile(T, _TC)
    nt = T // tc
    H = _POOL_HALO

    def body(dy_ref, u_ref, uh_ref, w_ref, s_ref, du_ref, dw_ref, ds_ref, ext_ref, eext_ref):
        j = pl.program_id(0)
        first = j == nt - 1
        row0 = (nt - 1 - j) * tc

        @pl.when(j == 0)
        def _():
            eext_ref[pl.ds(tc, H), :] = jnp.zeros((H, PW), f32)
            dw_ref[...] = jnp.zeros_like(dw_ref)
            ds_ref[...] = jnp.zeros_like(ds_ref)

        ext_ref[pl.ds(0, H), :] = jnp.where(first, 0.0, uh_ref[...])
        ext_ref[pl.ds(H, tc), :] = u_ref[...]
        t_glob = row0 + lax.broadcasted_iota(jnp.int32, (tc, 1), 0)
        dds = []
        for g, (m, u) in enumerate(_window_means(ext_ref, row0, tc, gw)):
            cols = slice(g * gw, (g + 1) * gw)
            d = (m - u).astype(bf16)
            yraw = _dot(d, w_ref[g])
            dyv = dy_ref[:, cols]
            ds_ref[:, cols] += jnp.sum(dyv * yraw, axis=0, keepdims=True)
            dyr = (dyv * s_ref[:, cols]).astype(bf16)
            dd = _dot_nt(dyr, w_ref[g])
            dw_ref[g] += _dot_tn(d, dyr)
            cnt = jnp.minimum(t_glob + 1, _POOL_WINDOWS[g]).astype(f32)
            eext_ref[pl.ds(0, tc), cols] = dd / cnt
            dds.append(dd)
        n = tc + H
        for g, w in enumerate(_POOL_WINDOWS):
            cols = slice(g * gw, (g + 1) * gw)
            s = eext_ref[:, cols]
            st = 1
            while st < w:
                s = s + pltpu.roll(s, n - st, 0)
                st *= 2
            du_ref[:, cols] = (s[0:tc, :] - dds[g]).astype(bf16)
        eext_ref[pl.ds(tc, H), :] = eext_ref[pl.ds(0, H), :]

    hblk = tc // H
    return pl.pallas_call(
        body, name="pool_bwd", grid=(nt,),
        in_specs=[pl.BlockSpec((tc, PW), lambda j: (nt - 1 - j, 0)), pl.BlockSpec((tc, PW), lambda j: (nt - 1 - j, 0)),
                  pl.BlockSpec((H, PW), lambda j: (jnp.maximum((nt - 1 - j) * hblk - 1, 0), 0)),
                  pl.BlockSpec((G, gw, gw), lambda j: (0, 0, 0)), pl.BlockSpec((1, PW), lambda j: (0, 0))],
        out_specs=[pl.BlockSpec((tc, PW), lambda j: (nt - 1 - j, 0)), pl.BlockSpec((G, gw, gw), lambda j: (0, 0, 0)),
                   pl.BlockSpec((1, PW), lambda j: (0, 0))],
        out_shape=[jax.ShapeDtypeStruct((T, PW), bf16), jax.ShapeDtypeStruct((G, gw, gw), f32), jax.ShapeDtypeStruct((1, PW), f32)],
        scratch_shapes=[pltpu.VMEM((H + tc, PW), f32), pltpu.VMEM((tc + H, PW), f32)],
        compiler_params=_cparams("arbitrary"),
    )(dy, proj, proj, w_pool, pool_scale)


_MESH = pl.DeviceIdType.MESH
_HBM = pl.BlockSpec(memory_space=pltpu.HBM)


def _slab(ref, kind, blk, n):
    start = blk * n
    if n % _LANE == 0:
        start = pl.multiple_of(start, _LANE)
    if kind == "col":
        return ref.at[:, pl.ds(start, n)]
    if kind == "row":
        return ref.at[pl.ds(start, n), :]
    if kind == "mid":
        return ref.at[:, pl.ds(start, n), :]
    raise ValueError(kind)


def _my_place():
    x, y, c = lax.axis_index("x"), lax.axis_index("y"), lax.axis_index("c")
    return x, y, c


def _blk(px, py, pc):
    return 4 * px + 2 * py + pc


_SEM = pl.BlockSpec(memory_space=pltpu.SEMAPHORE)
_EFFECT = pltpu.SideEffectType.DATAFLOW_SIDE_EFFECTING


_ALL_PEERS = (1, 2, 3, 4, 5, 6, 7)


def _peers(x, y, c):
    return [(k, (x ^ (k >> 2), y ^ ((k >> 1) & 1), c ^ (k & 1))) for k in range(1, 8)]


class _Route:
    def __init__(self, mode, kind, size):
        self.mode, self.kind, self.size = mode, kind, size

    def src(self, ref, peer_blk):
        return ref if self.mode == "gather" else _slab(ref, self.kind, peer_blk, self.size)

    def dst(self, ref, origin_blk):
        return _slab(ref, self.kind, origin_blk, self.size) if self.mode == "gather" else ref.at[origin_blk]


def _send_start(name, srcs, lands, routes, groups):
    nt, ng = len(srcs), len(groups)

    def body(*refs):
        src_refs, land_refs = refs[:nt], refs[nt:2 * nt]
        sems = refs[2 * nt:2 * nt + 2 * ng]
        token = refs[-1]
        x, y, c = _my_place()
        me = _blk(x, y, c)
        for gi, (grp, ks) in enumerate(groups):
            for pos, t in enumerate(grp):
                for k, peer in _peers(x, y, c):
                    if k in ks:
                        s = len(ks) * pos + ks.index(k)
                        pltpu.make_async_remote_copy(
                            src_ref=routes[t].src(src_refs[t], _blk(*peer)), dst_ref=routes[t].dst(land_refs[t], me),
                            send_sem=sems[2 * gi].at[s], recv_sem=sems[2 * gi + 1].at[s],
                            device_id=peer, device_id_type=_MESH).start()
        token[...] = jnp.zeros_like(token)

    hbm = lambda a: pltpu.HBM(a.shape, a.dtype)
    out_shape = []
    for grp, ks in groups:
        out_shape += [pltpu.SemaphoreType.DMA((len(ks) * len(grp),)), pltpu.SemaphoreType.DMA((len(ks) * len(grp),))]
    out_shape += [hbm(a) for a in srcs] + [hbm(a) for a in lands] + [jax.ShapeDtypeStruct((8, _LANE), f32)]
    res = pl.pallas_call(
        body, name=name, out_shape=out_shape,
        in_specs=[_HBM] * (2 * nt),
        out_specs=[_SEM] * (2 * ng) + [_HBM] * (2 * nt) + [pl.BlockSpec(memory_space=pltpu.VMEM)],
        input_output_aliases={t: 2 * ng + t for t in range(2 * nt)},
        compiler_params=pltpu.CompilerParams(has_side_effects=_EFFECT),
    )(*[pltpu.with_memory_space_constraint(a, pltpu.HBM) for a in list(srcs) + list(lands)])
    sems = [(res[2 * g], res[2 * g + 1]) for g in range(ng)]
    return sems, res[2 * ng:2 * ng + nt], res[2 * ng + nt:2 * ng + 2 * nt], res[-1]


def _send_wait(name, srcs, lands, routes, sems, after, ks=_ALL_PEERS):
    n = len(srcs)

    def body(*refs):
        src_refs, land_refs = refs[:n], refs[n:2 * n]
        send_sems, recv_sems = refs[2 * n], refs[2 * n + 1]
        x, y, c = _my_place()
        for pos in range(n):
            for k, peer in _peers(x, y, c):
                if k in ks:
                    pb = _blk(*peer)
                    s = len(ks) * pos + ks.index(k)
                    cp = pltpu.make_async_remote_copy(
                        src_ref=routes[pos].src(src_refs[pos], pb), dst_ref=routes[pos].dst(land_refs[pos], pb),
                        send_sem=send_sems.at[s], recv_sem=recv_sems.at[s], device_id=peer, device_id_type=_MESH)
                    cp.wait_send()
                    cp.wait_recv()

    hbm = lambda a: pltpu.HBM(a.shape, a.dtype)
    res = pl.pallas_call(
        body, name=name, out_shape=[hbm(a) for a in srcs] + [hbm(a) for a in lands],
        in_specs=[_HBM] * (2 * n) + [_SEM, _SEM, pl.BlockSpec(memory_space=pl.ANY)],
        out_specs=[_HBM] * (2 * n),
        input_output_aliases={t: t for t in range(2 * n)},
        compiler_params=pltpu.CompilerParams(has_side_effects=_EFFECT),
    )(*srcs, *lands, sems[0], sems[1], after)
    return res[:n], res[n:]


def _forward_start(name, lands, routes, ks):
    n = len(lands)

    def body(*refs):
        land_refs, send_sems, recv_sems, token = refs[:n], refs[n], refs[n + 1], refs[-1]
        x, y, c = _my_place()
        for pos in range(n):
            for i, k in enumerate(ks):
                part = routes[pos].dst(land_refs[pos], _blk(x ^ (k >> 2), y ^ ((k >> 1) & 1), c))
                pltpu.make_async_remote_copy(
                    src_ref=part, dst_ref=part, send_sem=send_sems.at[len(ks) * pos + i],
                    recv_sem=recv_sems.at[len(ks) * pos + i], device_id=(x, y, 1 - c), device_id_type=_MESH).start()
        token[...] = jnp.zeros_like(token)

    hbm = lambda a: pltpu.HBM(a.shape, a.dtype)
    sem = pltpu.SemaphoreType.DMA((len(ks) * n,))
    res = pl.pallas_call(
        body, name=name, out_shape=[sem, sem] + [hbm(a) for a in lands] + [jax.ShapeDtypeStruct((8, _LANE), f32)],
        in_specs=[_HBM] * n, out_specs=[_SEM, _SEM] + [_HBM] * n + [pl.BlockSpec(memory_space=pltpu.VMEM)],
        input_output_aliases={t: 2 + t for t in range(n)},
        compiler_params=pltpu.CompilerParams(has_side_effects=_EFFECT),
    )(*[pltpu.with_memory_space_constraint(a, pltpu.HBM) for a in lands])
    return (res[0], res[1]), res[2:2 + n], res[-1]


def _forward_wait(name, lands, routes, sems, after, ks):
    n = len(lands)

    def body(*refs):
        land_refs, send_sems, recv_sems = refs[:n], refs[n], refs[n + 1]
        x, y, c = _my_place()
        for pos in range(n):
            for i, k in enumerate(ks):
                px, py = x ^ (k >> 2), y ^ ((k >> 1) & 1)
                cp = pltpu.make_async_remote_copy(
                    src_ref=routes[pos].dst(land_refs[pos], _blk(px, py, c)),
                    dst_ref=routes[pos].dst(land_refs[pos], _blk(px, py, 1 - c)),
                    send_sem=send_sems.at[len(ks) * pos + i], recv_sem=recv_sems.at[len(ks) * pos + i],
                    device_id=(x, y, 1 - c), device_id_type=_MESH)
                cp.wait_send()
                cp.wait_recv()

    hbm = lambda a: pltpu.HBM(a.shape, a.dtype)
    return pl.pallas_call(
        body, name=name, out_shape=[hbm(a) for a in lands],
        in_specs=[_HBM] * n + [_SEM, _SEM, pl.BlockSpec(memory_space=pl.ANY)], out_specs=[_HBM] * n,
        input_output_aliases={t: t for t in range(n)},
        compiler_params=pltpu.CompilerParams(has_side_effects=_EFFECT),
    )(*lands, sems[0], sems[1], after)


def _copy_own(name, src, land, route, me):
    gather = route.mode == "gather"
    shard = src.shape if gather else land.shape[1:]
    lead = () if gather else (None,)

    if route.kind == "mid":
        grid = (1,)
        at_full = lambda i, me: (0, me[0], 0)
        at_shard = lambda i, me: (0, 0, 0)
        block = tuple(shard)
    else:
        rows, width = shard
        tr = _tile(rows, 512, 16)
        grid = (rows // tr,)
        block = (tr, width)
        if route.kind == "col":
            at_full = lambda i, me: (i, me[0])
        else:
            at_full = lambda i, me: (me[0] * grid[0] + i, 0)
        at_shard = lambda i, me: (i, 0)
    if gather:
        in_map, out_map = at_shard, at_full
    else:
        in_map, out_map = at_full, (lambda i, me: (me[0], *at_shard(i, me)))

    def body(me_ref, src_ref, land_ref, out_ref):
        out_ref[...] = src_ref[...]

    return pl.pallas_call(
        body, name=name, out_shape=jax.ShapeDtypeStruct(land.shape, land.dtype),
        grid_spec=pltpu.PrefetchScalarGridSpec(
            num_scalar_prefetch=1, grid=grid,
            in_specs=[pl.BlockSpec(block, in_map), pl.BlockSpec(memory_space=pl.ANY)],
            out_specs=pl.BlockSpec(lead + block, out_map)),
        input_output_aliases={2: 0},
        compiler_params=_cparams("arbitrary"),
    )(me, src, land)


def _place_own(name, srcs, lands, routes):
    me = _blk(*_my_place()).astype(jnp.int32).reshape(1)
    return [_copy_own(f"{name}_{t}", s, l, r, me) for t, (s, l, r) in enumerate(zip(srcs, lands, routes))]


def _exchange(fulls, kinds, sizes, whole):
    arrays = list(fulls) + list(whole)
    nt, nf = len(arrays), len(fulls)

    def shard_shape(t):
        s = list(arrays[t].shape)
        if t < nf:
            s[{"col": 1, "row": 0, "mid": 1}[kinds[t]]] = sizes[t]
        return tuple(s)

    def body(*refs):
        ins, outs = refs[:nt], refs[nt:2 * nt]
        send_sems, recv_sems, local_sems = refs[2 * nt:]
        x, y, c = _my_place()
        me = _blk(x, y, c)

        def src(t, blk):
            return _slab(ins[t], kinds[t], blk, sizes[t]) if t < nf else ins[t]

        mine = [pltpu.make_async_copy(src(t, me), outs[t].at[me], local_sems.at[t]) for t in range(nt)]
        for cp in mine:
            cp.start()
        sent = []
        for t in range(nt):
            for k in range(1, 8):
                peer = (x ^ (k >> 2), y ^ ((k >> 1) & 1), c ^ (k & 1))
                pb = _blk(*peer)
                cp = pltpu.make_async_remote_copy(
                    src_ref=src(t, pb), dst_ref=outs[t].at[me],
                    send_sem=send_sems.at[7 * t + k - 1], recv_sem=recv_sems.at[7 * t + k - 1],
                    device_id=peer, device_id_type=_MESH)
                cp.start()
                sent.append((cp, t, k, pb))
        for cp, t, k, pb in sent:
            pltpu.make_async_remote_copy(
                src_ref=src(t, pb), dst_ref=outs[t].at[pb],
                send_sem=send_sems.at[7 * t + k - 1], recv_sem=recv_sems.at[7 * t + k - 1],
                device_id=(x, y, c), device_id_type=_MESH).wait_recv()
        for cp, _, _, _ in sent:
            cp.wait_send()
        for cp in mine:
            cp.wait()

    return pl.pallas_call(
        body, name="exchange_grads",
        in_specs=[_HBM] * nt, out_specs=[_HBM] * nt,
        out_shape=[jax.ShapeDtypeStruct((_N_DEV,) + shard_shape(t), arrays[t].dtype) for t in range(nt)],
        scratch_shapes=[pltpu.SemaphoreType.DMA((7 * nt,)), pltpu.SemaphoreType.DMA((7 * nt,)), pltpu.SemaphoreType.DMA((nt,))],
        compiler_params=pltpu.CompilerParams(has_side_effects=True),
    )(*arrays)


def _adamw_update(w_ref, m_ref, v_ref, g, g_ref, d_ref, nm_ref, nv_ref):
    c1 = 1.0 - _ADAM_B1 ** _ADAM_STEP
    c2 = 1.0 - _ADAM_B2 ** _ADAM_STEP
    nm = _ADAM_B1 * m_ref[...] + (1.0 - _ADAM_B1) * g
    nv = _ADAM_B2 * v_ref[...] + (1.0 - _ADAM_B2) * (g * g)
    g_ref[...] = g
    nm_ref[...] = nm
    nv_ref[...] = nv
    d_ref[...] = -_ADAM_LR * ((nm / c1) / (jnp.sqrt(nv / c2) + _ADAM_EPS) + _ADAM_WD * w_ref[...])


def _adamw_small(me, ws, ms, vs, parts):
    n = len(ws)

    def body(me_ref, *refs):
        ins, outs = refs[:4 * n], refs[4 * n:]
        for t in range(n):
            w_ref, m_ref, v_ref, p_ref = ins[4 * t:4 * t + 4]
            r = w_ref.shape[0]
            g = p_ref[0, 0:r, :]
            for s in range(1, _N_DEV):
                g = g + p_ref[s, 0:r, :]
            _adamw_update(w_ref, m_ref, v_ref, g, *outs[4 * t:4 * t + 4])

    whole = lambda a: pl.BlockSpec(a.shape, lambda i, me, nd=a.ndim: (0,) * nd)
    in_specs, operands, out_specs, out_shape = [], [], [], []
    for w, m, v, p in zip(ws, ms, vs, parts):
        c = w.shape[1]
        mine = whole(p) if p.shape[2] == c else pl.BlockSpec((_N_DEV, p.shape[1], c), lambda i, me: (0, 0, me[0]))
        in_specs += [whole(w), whole(m), whole(v), mine]
        operands += [w, m, v, p]
        out_specs += [whole(w)] * 4
        out_shape += [jax.ShapeDtypeStruct(w.shape, f32)] * 4
    return pl.pallas_call(
        body, name="adamw_small", out_shape=out_shape,
        grid_spec=pltpu.PrefetchScalarGridSpec(num_scalar_prefetch=1, grid=(1,), in_specs=in_specs, out_specs=out_specs),
        compiler_params=_cparams("arbitrary"),
    )(me, *operands)


def _adamw(name, w, m, v, parts):
    R, C = w.shape
    n = parts.shape[0]
    tr = _tile(R, 256, 8)

    def body(w_ref, m_ref, v_ref, p_ref, g_ref, d_ref, nm_ref, nv_ref):
        g = p_ref[0].astype(f32)
        for s in range(1, n):
            g = g + p_ref[s].astype(f32)
        _adamw_update(w_ref, m_ref, v_ref, g, g_ref, d_ref, nm_ref, nv_ref)

    blk = pl.BlockSpec((tr, C), lambda i: (i, 0))
    sd = jax.ShapeDtypeStruct((R, C), f32)
    return pl.pallas_call(
        body, name=name, grid=(R // tr,),
        in_specs=[blk, blk, blk, pl.BlockSpec((n, tr, C), lambda i: (0, i, 0))],
        out_specs=[blk, blk, blk, blk], out_shape=[sd, sd, sd, sd],
        compiler_params=_cparams("parallel"),
    )(w, m, v, parts)


def _pad_rows8(a):
    return jnp.pad(a, ((0, 8 - a.shape[0]), (0, 0)))


def _local_step(x, target, p, get, emit, hint):
    T, D = x.shape

    def tie(a, *tokens):
        for tok in tokens:
            if tok is not None:
                a = a + tok[0, 0]
        return a

    n_parts = _N_DEV // _W_IN_PART
    w_in, base = get("w_in", x, 0)
    h1, proj = _in_proj_first(x, p["g_mix"], w_in, base, _W_IN_PART)
    hint("w_in", proj)
    for part in range(1, n_parts):
        w_in, base = get("w_in", proj, part)
        proj = _in_proj_more(f"in_proj_{part}", h1, w_in, proj, base, _W_IN_PART)
    w_pool = get("w_pool", proj)
    PW = w_pool.shape[0] * w_pool.shape[1]
    y_pool = _pool_fwd(proj, w_pool, p["pool_scale"])
    tok = hint("w_pool_proj", y_pool)
    lru_conv_w, w_a, w_i = get("lru_conv_w", proj), get("w_a", proj), get("w_i", proj)
    y_lru, hs = _lru_fwd(proj, D, PW, lru_conv_w, tie(p["lru_conv_b"], tok), w_a, p["b_a"], w_i, p["b_i"], p["lru_lambda"])
    tok = hint("w_up", y_lru)
    w_pp, w_lp = get("w_pool_proj", y_lru), get("w_lru_proj", y_lru)
    pp, qq, merged = _merge_fwd(y_pool, y_lru, w_pp, w_lp, proj, tie(p["b_gate"], tok))
    w_out = get("w_out", y_lru)
    x2, h2 = _out_proj(merged, w_out, x, p["g_mlp"])
    w_up = get("w_up", x2)
    up = _mm_nn("up_proj", h2, w_up, f32, tm_want=2 * _TM)
    tok = hint("w_down", up)
    ffn_conv_w = get("ffn_conv_w", y_lru)
    z = _ffn_fwd(up, ffn_conv_w, tie(p["ffn_conv_b"], tok))
    w_down = get("w_down", z)
    F = w_down.shape[0]
    dx3, dx3b, loss_t, dg_final = _down_loss(z, w_down, x2, target, p["g_final"])

    gs = {"g_final": dg_final}
    tok = emit("w_down", _mm_tn("dw_down", z, dx3b, bf16, tm_want=1536))
    dz = _mm_nt("dz", dx3b, w_down, bf16)
    dup, dcw_ffn, dcb_ffn = _ffn_bwd(dz, up, ffn_conv_w, tie(p["ffn_conv_b"], tok))
    gs["ffn_conv_w"] = dcw_ffn
    gs["ffn_conv_b"] = dcb_ffn

    tm = _tile(T, _TM)
    tk = _tile(F, _TN_MAX)
    nkh = F // tk
    tkt = _tile(T, _TK_T)
    tok = emit("w_up", _matmul(
        "dw_up", "tn", (h2, dup),
        [pl.BlockSpec((tkt, D), lambda i, j, k: (k, 0)), pl.BlockSpec((None, tkt, tk), lambda i, j, k: (j // nkh, k, j % nkh))],
        jax.ShapeDtypeStruct((D, 2 * F), bf16), pl.BlockSpec((D, tk), lambda i, j, k: (0, j)),
        (1, 2 * nkh, T // tkt), (D, tk)))

    tkc = _tile(F, _TK_UP)
    nkc = F // tkc
    dx2, dx2b, gs["g_mlp"] = _norm_bwd_matmul(
        "dh2", dup, pl.BlockSpec((None, tm, tkc), lambda i, k: (k // nkc, i, k % nkc)), 2 * nkc,
        w_up, pl.BlockSpec((D, tkc), lambda i, k: (0, k)), x2, tie(p["g_mlp"], tok), dx3)

    tok = emit("w_out", _mm_tn("dw_out", merged, dx2b, bf16))
    dP, dQ, dl0, dl1, db0, db1 = _merge_bwd(dx2b, w_out, pp, qq, proj, tie(p["b_gate"], tok), PW)
    gs["b_gate"] = jnp.concatenate([db0, db1], axis=1)
    tok = emit("w_pool_proj", _mm_tn("dw_pool_proj", y_pool, dP, bf16))
    tok2 = emit("w_lru_proj", _mm_tn("dw_lru_proj", y_lru, dQ, bf16))
    dy_pool = _mm_nt("dy_pool", dP, w_pp, f32)
    dy_lru = _mm_nt("dy_lru", dQ, w_lp, f32)

    du_lru, du_gelu, dwa, dwi, dcw_lru, dvec = _lru_bwd(
        dy_lru, proj, hs, D, lru_conv_w, tie(p["lru_conv_b"], tok, tok2), w_a, p["b_a"], w_i, p["b_i"], p["lru_lambda"])
    tok = emit("w_a", dwa.astype(bf16))
    tok2 = emit("w_i", dwi.astype(bf16))
    gs["lru_conv_w"] = dcw_lru
    gs["lru_conv_b"], gs["b_a"], gs["b_i"], gs["lru_lambda"] = dvec[0:1], dvec[1:2], dvec[2:3], dvec[3:4]
    du_pool, dwp, gs["pool_scale"] = _pool_bwd(dy_pool, proj, w_pool, tie(p["pool_scale"], tok, tok2))
    tok = emit("w_pool", dwp.astype(bf16))

    dproj = jnp.concatenate([du_pool, du_lru, du_gelu, dl0, dl1], axis=1)
    tok2 = emit("w_in", _mm_tn("dw_in", h1, dproj, bf16))
    NI = dproj.shape[1]
    tki = _tile(NI, _TK_IN, _LANE)
    grad_x, _, gs["g_mix"] = _norm_bwd_matmul(
        "dh1", dproj, pl.BlockSpec((tm, tki), lambda i, k: (i, k)), NI // tki,
        w_in, pl.BlockSpec((D, tki), lambda i, k: (0, k)), x, tie(p["g_mix"], tok, tok2), dx2)
    return loss_t[0, 0], grad_x, gs


_MATRICES = {"w_in": "col", "w_pool": "mid", "w_a": "mid", "w_i": "mid", "w_pool_proj": "col", "w_lru_proj": "row",
             "w_out": "row", "w_up": "col", "w_down": "row"}
_CONVS = ("lru_conv_w", "ffn_conv_w")
_GATHER_GROUPS = (("w_pool", "lru_conv_w", "w_a", "w_i"), ("w_pool_proj", "w_lru_proj", "w_out", "ffn_conv_w"),
                  ("w_up",), ("w_down",))
_GROUP_TWO_LEVEL = (False, True, True, True)
_SAME_CORE = (2, 4, 6)
_TWO_LEVEL = (1, 2, 4, 6)
_W_IN_PARTS = ((1,), (4,), (2,), (6,))
_W_IN_PART = 2
_VECTORS = ("g_mix", "b_gate", "pool_scale", "lru_conv_b", "b_a", "b_i", "lru_lambda", "g_mlp", "ffn_conv_b", "g_final")
_WEIGHTS = ("g_mix", "w_in", "b_gate", "w_pool", "pool_scale", "lru_conv_w", "lru_conv_b", "w_a", "b_a", "w_i", "b_i",
            "lru_lambda", "w_pool_proj", "w_lru_proj", "w_out", "g_mlp", "w_up", "ffn_conv_w", "ffn_conv_b", "w_down", "g_final")


def _full_shape(shape, kind):
    s = list(shape)
    s[{"col": 1, "row": 0, "mid": 1}[kind]] *= _N_DEV
    return tuple(s)


def _step(x, target, w, m, v):
    x, target = x[0], target[0]
    me = _blk(*_my_place())

    axis = {"col": 1, "row": 0, "mid": 1}
    kind = dict(_MATRICES, **{n: "col" for n in _CONVS})
    shard = {n: w[n].astype(bf16) for n in _MATRICES}
    shard.update({n: _pad_rows8(w[n]) for n in _CONVS})
    order = ["w_in"] + [n for grp in _GATHER_GROUPS for n in grp]
    index = {n: t for t, n in enumerate(order)}
    n_parts = len(_W_IN_PARTS)
    groups = [([0], ks) for ks in _W_IN_PARTS]
    groups += [([index[n] for n in grp], _TWO_LEVEL if two else _ALL_PEERS) for grp, two in zip(_GATHER_GROUPS, _GROUP_TWO_LEVEL)]
    g_routes = [_Route("gather", kind[n], shard[n].shape[axis[kind[n]]]) for n in order]
    lands = [lax.empty(_full_shape(shard[n].shape, kind[n]), shard[n].dtype) for n in order]
    g_sems, g_srcs, g_lands, _ = _send_start("gather_start", [shard[n] for n in order], lands, g_routes, groups)
    gathered, passing = {}, {}
    x_, y_, _c = _my_place()
    w_in_state = [[g_srcs[0]], [g_lands[0]]]

    def hint(name, after):
        if name == "w_in":
            srcs, got = w_in_state
            tok = None
            for part in range(1, n_parts):
                ks = _W_IN_PARTS[part]
                srcs, got = _send_wait(f"gather_wait_w_in_{part}", srcs, got, g_routes[:1], g_sems[part], after, ks)
                sems, got, tok = _forward_start(f"gather_pass_w_in_{part}", got, g_routes[:1], ks)
                passing[name, part] = sems
            w_in_state[:] = [srcs, got]
            return tok
        gi = next(i for i, grp in enumerate(_GATHER_GROUPS) if name in grp)
        if not _GROUP_TWO_LEVEL[gi] or gi in passing or _GATHER_GROUPS[gi][0] in gathered:
            return None
        ts = groups[n_parts + gi][0]
        routes = [g_routes[t] for t in ts]
        srcs, got = _send_wait(f"gather_wait_{gi}", [g_srcs[t] for t in ts], [g_lands[t] for t in ts], routes,
                               g_sems[n_parts + gi], after, _TWO_LEVEL)
        got = _place_own(f"gather_own_{gi}", srcs, got, routes)
        sems, got, tok = _forward_start(f"gather_pass_{gi}", got, routes, _SAME_CORE)
        passing[gi] = (sems, got, routes)
        return tok

    def get(name, after, part=None):
        if name == "w_in":
            ks = _W_IN_PARTS[part]
            if part == 0:
                srcs, got = _send_wait("gather_wait_w_in_0", *w_in_state, g_routes[:1], g_sems[0], after, ks)
                got = _place_own("gather_own_w_in", srcs, got, g_routes[:1])
                w_in_state[:] = [srcs, got]
            else:
                if (name, part) not in passing:
                    hint(name, after)
                got = _forward_wait(f"gather_got_w_in_{part}", w_in_state[1], g_routes[:1], passing[name, part], after, ks)
                w_in_state[1] = got
            k = ks[-1]
            base = 4 * (x_ ^ (k >> 2)) + 2 * (y_ ^ ((k >> 1) & 1))
            return got[0], base.astype(jnp.int32).reshape(1)
        if name not in gathered:
            gi = next(i for i, grp in enumerate(_GATHER_GROUPS) if name in grp)
            if _GROUP_TWO_LEVEL[gi]:
                hint(name, after)
                sems, got, routes = passing[gi]
                full = _forward_wait(f"gather_got_{gi}", got, routes, sems, after, _SAME_CORE)
            else:
                ts = groups[n_parts + gi][0]
                routes = [g_routes[t] for t in ts]
                srcs, got = _send_wait(f"gather_wait_{gi}", [g_srcs[t] for t in ts], [g_lands[t] for t in ts], routes,
                                       g_sems[n_parts + gi], after)
                full = _place_own(f"gather_own_{gi}", srcs, got, routes)
            gathered.update(zip(_GATHER_GROUPS[gi], full))
        return gathered[name]

    sent = {}

    def emit(name, grad):
        k = _MATRICES[name]
        size = w[name].shape[axis[k]]
        route = _Route("scatter", k, size)
        shp = list(grad.shape)
        shp[axis[k]] = size
        land = lax.empty((_N_DEV, *shp), grad.dtype)
        sems, srcs, lnds, token = _send_start("grad_start_" + name, [grad], [land], [route], [([0], _ALL_PEERS)])
        sent[name] = (srcs, lnds, [route], sems[0])
        return token

    p = {n: w[n].reshape(1, -1) for n in _VECTORS}
    loss_t, grad_x, gs = _local_step(x, target, p, get, emit, hint)
    loss = lax.psum(loss_t, ("x", "y", "c"))

    small_names = list(_VECTORS) + list(_CONVS)
    small_parts = _exchange([], [], [], [gs[n] for n in small_names])

    out = {}
    mats = list(_MATRICES)
    for n in mats:
        srcs, lnds, routes, sems = sent[n]
        srcs, got = _send_wait("grad_wait_" + n, srcs, lnds, routes, sems, grad_x)
        parts = _place_own("grad_own_" + n, srcs, got, routes)[0]
        shp = w[n].shape
        r2 = (-1, shp[-1])
        res = _adamw("adamw_" + n, w[n].reshape(r2), m[n].reshape(r2), v[n].reshape(r2),
                     parts.reshape((_N_DEV,) + w[n].reshape(r2).shape))
        out[n] = [a.reshape(shp) for a in res]
    two_d = lambda a: a.reshape(-1, a.shape[-1])
    res = _adamw_small(me.astype(jnp.int32).reshape(1), [two_d(w[n]) for n in small_names], [two_d(m[n]) for n in small_names],
                       [two_d(v[n]) for n in small_names], small_parts)
    for t, n in enumerate(small_names):
        out[n] = [a.reshape(w[n].shape) for a in res[4 * t:4 * t + 4]]
    return loss, grad_x[None], out


def kernel(x, g_mix, w_in, b_gate, w_pool, pool_scale, lru_conv_w, lru_conv_b, w_a, b_a, w_i, b_i, lru_lambda, w_pool_proj, w_lru_proj, w_out, g_mlp, w_up, ffn_conv_w, ffn_conv_b, w_down, g_final, loss_target, m_g_mix, m_w_in, m_b_gate, m_w_pool, m_pool_scale, m_lru_conv_w, m_lru_conv_b, m_w_a, m_b_a, m_w_i, m_b_i, m_lru_lambda, m_w_pool_proj, m_w_lru_proj, m_w_out, m_g_mlp, m_w_up, m_ffn_conv_w, m_ffn_conv_b, m_w_down, m_g_final, v_g_mix, v_w_in, v_b_gate, v_w_pool, v_pool_scale, v_lru_conv_w, v_lru_conv_b, v_w_a, v_b_a, v_w_i, v_b_i, v_lru_lambda, v_w_pool_proj, v_w_lru_proj, v_w_out, v_g_mlp, v_w_up, v_ffn_conv_w, v_ffn_conv_b, v_w_down, v_g_final):
    given = dict(locals())
    orig = {n: given[n].shape for n in _WEIGHTS}

    def squeeze(a, n):
        return a if n == "g_final" else a[0]

    w = {n: squeeze(given[n], n) for n in _WEIGHTS}
    m = {n: squeeze(given["m_" + n], n) for n in _WEIGHTS}
    v = {n: squeeze(given["v_" + n], n) for n in _WEIGHTS}
    for d in (w, m, v):
        d["g_final"] = d["g_final"].reshape(1, -1)
    loss, grad_x, out = _step(x, loss_target, w, m, v)
    res = [loss, grad_x]
    for k in range(4):
        res += [out[n][k].reshape(orig[n]) for n in _WEIGHTS]
    return tuple(res)
```

```python
import functools

import jax
import jax.numpy as jnp
from jax import lax
from jax.experimental import pallas as pl
from jax.experimental.pallas import tpu as pltpu

f32 = jnp.float32
bf16 = jnp.bfloat16

_EPS = 1e-6
_LRU_C = 8.0
_POOL_WINDOWS = (2, 4, 8, 16)
_POOL_HALO = 16
_CONV_HALO = 8
_GELU_C0 = 0.7978845608028654
_GELU_C1 = 0.044715
_ADAM_LR, _ADAM_B1, _ADAM_B2, _ADAM_EPS, _ADAM_WD, _ADAM_STEP = 0.001, 0.9, 0.999, 1e-08, 0.01, 10
_N_DEV = 8
_LANE = 128
_VMEM_LIMIT = 60 * 1024 * 1024

_TM = 512
_TM_SMALL = 512
_TC = 256
_TK_T = 1024
_TN_VMEM_BUDGET = 44 * 1024 * 1024
_TN_MAX = 1536
_CW = 1024
_TK_DOWN = 1536
_TK_IN = 2304
_TK_UP = 2048


def _cparams(*sem):
    return pltpu.CompilerParams(dimension_semantics=tuple(sem), vmem_limit_bytes=_VMEM_LIMIT)


def _tile(n, want, mult=1):
    if n <= want:
        return n
    t = want - want % mult
    while n % t:
        t -= mult
    return t


def _gelu(x):
    t = jnp.tanh(x * (_GELU_C0 + (_GELU_C0 * _GELU_C1) * (x * x)))
    return x * (0.5 + 0.5 * t)


def _gelu_both(x):
    x2 = x * x
    t = jnp.tanh(x * (_GELU_C0 + (_GELU_C0 * _GELU_C1) * x2))
    h = 0.5 + 0.5 * t
    return x * h, h + (x * (1.0 - t * t)) * (0.5 * _GELU_C0 + (1.5 * _GELU_C0 * _GELU_C1) * x2)


def _fold8(x):
    out = x[0:8]
    for r in range(8, x.shape[0], 8):
        out = out + x[r:r + 8]
    return out


def _sigmoid(x):
    return jax.nn.sigmoid(x)


def _dot(a, b):
    return jnp.dot(a, b, preferred_element_type=f32)


def _dot_nt(a, b):
    return lax.dot_general(a, b, (((1,), (1,)), ((), ())), preferred_element_type=f32)


def _dot_tn(a, b):
    return lax.dot_general(a, b, (((0,), (0,)), ((), ())), preferred_element_type=f32)


def _rms_rows(x_ref, g_ref, h_ref, n_rows, n_cols):
    rr = _tile(n_rows, _SLAB_ROWS)
    cg = _tile(n_cols, _SLAB_COLS)
    for r0 in range(0, n_rows, rr):
        rows = pl.ds(r0, rr)
        x = x_ref[rows, :]
        r = lax.rsqrt(jnp.mean(x * x, axis=-1, keepdims=True) + _EPS)
        for c0 in range(0, n_cols, cg):
            cols = slice(c0, c0 + cg)
            h_ref[rows, cols] = (x_ref[rows, cols] * r * g_ref[:, cols]).astype(bf16)


def _matmul(name, mode, operands, in_specs, out_shape, out_spec, grid, acc_shape):
    dot = {"nn": _dot, "nt": _dot_nt, "tn": _dot_tn}[mode]
    nk = grid[2]

    def body_whole(a_ref, b_ref, o_ref):
        o_ref[...] = dot(a_ref[...], b_ref[...]).astype(o_ref.dtype)

    def body(a_ref, b_ref, o_ref, acc_ref):
        k = pl.program_id(2)

        @pl.when(k == 0)
        def _():
            acc_ref[...] = dot(a_ref[...], b_ref[...])

        @pl.when((k > 0) & (k < nk - 1))
        def _():
            acc_ref[...] += dot(a_ref[...], b_ref[...])

        @pl.when(k == nk - 1)
        def _():
            o_ref[...] = (acc_ref[...] + dot(a_ref[...], b_ref[...])).astype(o_ref.dtype)

    return pl.pallas_call(
        body_whole if nk == 1 else body, name=name, grid=grid, in_specs=in_specs, out_specs=out_spec, out_shape=out_shape,
        scratch_shapes=[] if nk == 1 else [pltpu.VMEM(acc_shape, f32)],
        compiler_params=_cparams("parallel", "parallel", "arbitrary"),
    )(*operands)


def _mm_nn(name, a, b, out_dtype, tm_want=None):
    M, K = a.shape
    N = b.shape[1]
    tm, tn = _tile(M, tm_want or _TM), _tile(N, _TN_MAX)
    return _matmul(
        name, "nn", (a, b),
        [pl.BlockSpec((tm, K), lambda i, j, k: (i, 0)), pl.BlockSpec((K, tn), lambda i, j, k: (0, j))],
        jax.ShapeDtypeStruct((M, N), out_dtype), pl.BlockSpec((tm, tn), lambda i, j, k: (i, j)),
        (M // tm, N // tn, 1), (tm, tn))


def _mm_nt(name, a, b, out_dtype):
    M, K = a.shape
    N = b.shape[0]
    tm, tn = _tile(M, 2 * _TM), _tile(N, _TN_MAX)
    return _matmul(
        name, "nt", (a, b),
        [pl.BlockSpec((tm, K), lambda i, j, k: (i, 0)), pl.BlockSpec((tn, K), lambda i, j, k: (j, 0))],
        jax.ShapeDtypeStruct((M, N), out_dtype), pl.BlockSpec((tm, tn), lambda i, j, k: (i, j)),
        (M // tm, N // tn, 1), (tm, tn))


def _mm_tn(name, a, b, out_dtype, tm_want=2048):
    T, M = a.shape
    N = b.shape[1]
    tm, tn, tk = _tile(M, tm_want), _tile(N, _TN_MAX), _tile(T, _TK_T)
    need = lambda tk: 4 * tk * (tm + tn) + 8 * tm * tn
    if T % (2 * tk) == 0 and need(2 * tk) <= _TN_VMEM_BUDGET:
        tk *= 2
    return _matmul(
        name, "tn", (a, b),
        [pl.BlockSpec((tk, tm), lambda i, j, k: (k, i)), pl.BlockSpec((tk, tn), lambda i, j, k: (k, j))],
        jax.ShapeDtypeStruct((M, N), out_dtype), pl.BlockSpec((tm, tn), lambda i, j, k: (i, j)),
        (M // tm, N // tn, T // tk), (tm, tn))


def _in_proj_first(x, g_mix, w_in, base, n_tiles):
    T, D = x.shape
    NI = w_in.shape[1]
    tm, tn = _tile(T, 2 * _TM), NI // _N_DEV

    def body(base_ref, x_ref, g_ref, w_ref, h_ref, o_ref):
        @pl.when(pl.program_id(1) == 0)
        def _():
            _rms_rows(x_ref, g_ref, h_ref, tm, D)

        o_ref[...] = _dot(h_ref[...], w_ref[...])

    return pl.pallas_call(
        body, name="in_proj_0",
        grid_spec=pltpu.PrefetchScalarGridSpec(
            num_scalar_prefetch=1, grid=(T // tm, n_tiles),
            in_specs=[pl.BlockSpec((tm, D), lambda i, j, b: (i, 0)), pl.BlockSpec((1, D), lambda i, j, b: (0, 0)),
                      pl.BlockSpec((D, tn), lambda i, j, b: (0, b[0] + j))],
            out_specs=[pl.BlockSpec((tm, D), lambda i, j, b: (i, 0)), pl.BlockSpec((tm, tn), lambda i, j, b: (i, b[0] + j))]),
        out_shape=[jax.ShapeDtypeStruct((T, D), bf16), jax.ShapeDtypeStruct((T, NI), f32)],
        compiler_params=_cparams("parallel", "arbitrary"),
    )(base, x, g_mix, w_in)


def _in_proj_more(name, h1, w_in, proj, base, n_tiles):
    T, D = h1.shape
    NI = w_in.shape[1]
    tm, tn = _tile(T, 2 * _TM), NI // _N_DEV

    def body(base_ref, h_ref, w_ref, proj_ref, o_ref):
        o_ref[...] = _dot(h_ref[...], w_ref[...])

    return pl.pallas_call(
        body, name=name,
        grid_spec=pltpu.PrefetchScalarGridSpec(
            num_scalar_prefetch=1, grid=(T // tm, n_tiles),
            in_specs=[pl.BlockSpec((tm, D), lambda i, j, b: (i, 0)), pl.BlockSpec((D, tn), lambda i, j, b: (0, b[0] + j)),
                      pl.BlockSpec(memory_space=pl.ANY)],
            out_specs=pl.BlockSpec((tm, tn), lambda i, j, b: (i, b[0] + j))),
        out_shape=jax.ShapeDtypeStruct((T, NI), f32),
        input_output_aliases={3: 0},
        compiler_params=_cparams("parallel", "arbitrary"),
    )(base, h1, w_in, proj)


def _window_means(ext_ref, row0, tc, gw):
    H = _POOL_HALO
    t_glob = row0 + lax.broadcasted_iota(jnp.int32, (tc, 1), 0)
    out = []
    for g, w in enumerate(_POOL_WINDOWS):
        s = ext_ref[:, g * gw:(g + 1) * gw]
        st = 1
        while st < w:
            s = s + pltpu.roll(s, st, 0)
            st *= 2
        cnt = jnp.minimum(t_glob + 1, w).astype(f32)
        out.append((s[H:, :] / cnt, ext_ref[pl.ds(H, tc), g * gw:(g + 1) * gw]))
    return out


def _pool_fwd(proj, w_pool, pool_scale):
    T = proj.shape[0]
    G, gw, _ = w_pool.shape
    PW = G * gw
    tc = _tile(T, _TC)
    H = _POOL_HALO

    def body(u_ref, w_ref, s_ref, y_ref, ext_ref):
        i = pl.program_id(0)

        @pl.when(i == 0)
        def _():
            ext_ref[pl.ds(0, H), :] = jnp.zeros((H, PW), f32)

        ext_ref[pl.ds(H, tc), :] = u_ref[...]
        for g, (m, u) in enumerate(_window_means(ext_ref, i * tc, tc, gw)):
            d = (m - u).astype(bf16)
            y = _dot(d, w_ref[g]) * s_ref[:, g * gw:(g + 1) * gw]
            y_ref[:, g * gw:(g + 1) * gw] = y.astype(bf16)
        ext_ref[pl.ds(0, H), :] = ext_ref[pl.ds(tc, H), :]

    return pl.pallas_call(
        body, name="pool_fwd", grid=(T // tc,),
        in_specs=[pl.BlockSpec((tc, PW), lambda i: (i, 0)), pl.BlockSpec((G, gw, gw), lambda i: (0, 0, 0)),
                  pl.BlockSpec((1, PW), lambda i: (0, 0))],
        out_specs=pl.BlockSpec((tc, PW), lambda i: (i, 0)),
        out_shape=jax.ShapeDtypeStruct((T, PW), bf16),
        scratch_shapes=[pltpu.VMEM((H + tc, PW), f32)],
        compiler_params=_cparams("arbitrary"),
    )(proj, w_pool, pool_scale)


def _softplus(z):
    return jnp.maximum(z, 0.0) + jnp.log1p(jnp.exp(-jnp.abs(z)))


def _causal_conv(ext_ref, cw_ref, cb_ref, n, K, cols=slice(None), r0=0):
    H = _CONV_HALO
    v = cb_ref[:, cols] + cw_ref[K - 1:K, cols] * ext_ref[pl.ds(H + r0, n), cols]
    for k in range(K - 1):
        v = v + cw_ref[k:k + 1, cols] * ext_ref[pl.ds(H + r0 - (K - 1 - k), n), cols]
    return v


_ROWS = 8
_SLAB_ROWS = 16
_SLAB_COLS = 512


def _slabs(n_rows, n_cols, reverse=False):
    cg = _tile(n_cols, _SLAB_COLS)
    rr = _tile(n_rows, _SLAB_ROWS)
    starts = range(0, n_rows, rr)
    for c0 in range(0, n_cols, cg):
        for r0 in (reversed(starts) if reverse else starts):
            yield slice(c0, c0 + cg), pl.ds(r0, rr)


def _scan_rows(a_ref, b_ref, out_ref, carry_ref, n_rows, n_cols, reverse=False):
    cg = _tile(n_cols, _SLAB_COLS)
    row = lax.broadcasted_iota(jnp.int32, (8, cg), 0)
    steps = [(8 - sh, row < 8 - sh) if reverse else (sh, row >= sh) for sh in (1, 2, 4)]
    tiles = list(range(0, n_rows, 8))
    for c0 in range(0, n_cols, cg):
        cols = slice(c0, c0 + cg)
        h_in = carry_ref[0:1, cols]
        for r0 in (reversed(tiles) if reverse else tiles):
            rows = pl.ds(r0, 8)
            a, b = a_ref[rows, cols], b_ref[rows, cols]
            for shift, inside in steps:
                b = jnp.where(inside, a * pltpu.roll(b, shift, 0) + b, b)
                a = jnp.where(inside, a * pltpu.roll(a, shift, 0), a)
            h = a * h_in + b
            out_ref[rows, cols] = h
            h_in = h[0:1, :] if reverse else h[7:8, :]
        carry_ref[0:1, cols] = h_in


def _slabs_with_sums(n_rows, n_cols, n_sums, visit, flush, reverse=False):
    cg = _tile(n_cols, _SLAB_COLS)
    rr = _tile(n_rows, _SLAB_ROWS)
    starts = list(range(0, n_rows, rr))
    for c0 in range(0, n_cols, cg):
        cols = slice(c0, c0 + cg)
        sums = [jnp.zeros((8, cg), f32) for _ in range(n_sums)]
        for r0 in (reversed(starts) if reverse else starts):
            sums = visit(cols, pl.ds(r0, rr), sums)
        flush(cols, [jnp.sum(s, axis=0, keepdims=True) for s in sums])


def _lru_fwd(proj, D, PW, conv_w, conv_b, w_a, b_a, w_i, b_i, lam):
    T = proj.shape[0]
    NB, bw, _ = w_a.shape
    K = 4
    tc = _tile(T, _TC)
    H = _CONV_HALO
    hb = D // 2
    assert PW == hb and conv_w.shape[0] == 8

    def body(u0_ref, u1_ref, g0_ref, g1_ref, cw_ref, cb_ref, wa_ref, ba_ref, wi_ref, bi_ref, lam_ref,
             y_ref, hs_ref, ext_ref, a_scr, b_scr, v_scr, hc_scr):
        c = pl.program_id(0)

        @pl.when(c == 0)
        def _():
            ext_ref[pl.ds(0, H), :] = jnp.zeros((H, D), f32)
            hc_scr[...] = jnp.zeros_like(hc_scr)

        ext_ref[pl.ds(H, tc), 0:hb] = u0_ref[...]
        ext_ref[pl.ds(H, tc), hb:D] = u1_ref[...]
        sp = _softplus(-lam_ref[...])
        for cols, rows in _slabs(tc, D):
            v_scr[rows, cols] = _causal_conv(ext_ref, cw_ref, cb_ref, rows.size, K, cols, rows.start)
        for b in range(NB):
            cols = slice(b * bw, (b + 1) * bw)
            vb = v_scr[:, cols].astype(bf16)
            a_scr[:, cols] = _dot(vb, wa_ref[b])
            b_scr[:, cols] = _dot(vb, wi_ref[b])
        for cols, rows in _slabs(tc, D):
            r = _sigmoid(a_scr[rows, cols] + ba_ref[:, cols])
            i = _sigmoid(b_scr[rows, cols] + bi_ref[:, cols])
            a = jnp.exp(-_LRU_C * r * sp[:, cols])
            a_scr[rows, cols] = a
            b_scr[rows, cols] = jnp.sqrt(1.0 - a * a) * (i * v_scr[rows, cols])

        _scan_rows(a_scr, b_scr, hs_ref, hc_scr, tc, D)
        for cols, rows in _slabs(tc, D):
            g_ref, gcols = (g0_ref, cols) if cols.start < hb else (g1_ref, slice(cols.start - hb, cols.stop - hb))
            y_ref[rows, cols] = (hs_ref[rows, cols] * _gelu(g_ref[rows, gcols])).astype(bf16)
        ext_ref[pl.ds(0, H), :] = ext_ref[pl.ds(tc, H), :]

    vec = pl.BlockSpec((1, D), lambda c: (0, 0))
    wspec = pl.BlockSpec((NB, bw, bw), lambda c: (0, 0, 0))
    return pl.pallas_call(
        body, name="lru_fwd", grid=(T // tc,),
        in_specs=[pl.BlockSpec((tc, hb), lambda c: (c, 1)), pl.BlockSpec((tc, hb), lambda c: (c, 2)),
                  pl.BlockSpec((tc, hb), lambda c: (c, 3)), pl.BlockSpec((tc, hb), lambda c: (c, 4)),
                  pl.BlockSpec((8, D), lambda c: (0, 0)), vec, wspec, vec, wspec, vec, vec],
        out_specs=[pl.BlockSpec((tc, D), lambda c: (c, 0)), pl.BlockSpec((tc, D), lambda c: (c, 0))],
        out_shape=[jax.ShapeDtypeStruct((T, D), bf16), jax.ShapeDtypeStruct((T, D), f32)],
        scratch_shapes=[pltpu.VMEM((H + tc, D), f32), pltpu.VMEM((tc, D), f32), pltpu.VMEM((tc, D), f32),
                        pltpu.VMEM((tc, D), f32), pltpu.VMEM((8, D), f32)],
        compiler_params=_cparams("arbitrary"),
    )(proj, proj, proj, proj, conv_w, conv_b, w_a, b_a, w_i, b_i, lam)


def _merge_fwd(y_pool, y_lru, w_pp, w_lp, proj, b_gate):
    T, PW = y_pool.shape
    D = y_lru.shape[1]
    tm, tn = _tile(T, _TM), _tile(D, PW)
    nj = D // tn
    off = (PW + 2 * D) // tn

    def body(yp_ref, yl_ref, wp_ref, wl_ref, l0_ref, l1_ref, b0_ref, b1_ref, p_ref, q_ref, m_ref, p_scr, q_scr):
        p_scr[...] = _dot(yp_ref[...], wp_ref[...])
        q_scr[...] = _dot(yl_ref[...], wl_ref[...])
        for cols, rows in _slabs(tm, tn):
            p, q = p_scr[rows, cols], q_scr[rows, cols]
            g0 = _sigmoid(l0_ref[rows, cols] + b0_ref[:, cols])
            g1 = _sigmoid(l1_ref[rows, cols] + b1_ref[:, cols])
            m_ref[rows, cols] = (g0 * p + g1 * q).astype(bf16)
            p_ref[rows, cols] = p.astype(bf16)
            q_ref[rows, cols] = q.astype(bf16)

    tile = pl.BlockSpec((tm, tn), lambda j, i: (i, j))
    return pl.pallas_call(
        body, name="merge_fwd", grid=(nj, T // tm),
        in_specs=[pl.BlockSpec((tm, PW), lambda j, i: (i, 0)), pl.BlockSpec((tm, D), lambda j, i: (i, 0)),
                  pl.BlockSpec((PW, tn), lambda j, i: (0, j)), pl.BlockSpec((D, tn), lambda j, i: (0, j)),
                  pl.BlockSpec((tm, tn), lambda j, i: (i, off + j)), pl.BlockSpec((tm, tn), lambda j, i: (i, off + nj + j)),
                  pl.BlockSpec((1, tn), lambda j, i: (0, j)), pl.BlockSpec((1, tn), lambda j, i: (0, nj + j))],
        out_specs=[tile, tile, tile],
        out_shape=[jax.ShapeDtypeStruct((T, D), bf16), jax.ShapeDtypeStruct((T, D), bf16), jax.ShapeDtypeStruct((T, D), bf16)],
        scratch_shapes=[pltpu.VMEM((tm, tn), f32), pltpu.VMEM((tm, tn), f32)],
        compiler_params=_cparams("parallel", "arbitrary"),
    )(y_pool, y_lru, w_pp, w_lp, proj, proj, b_gate, b_gate)


def _out_proj(merged, w_out, x, g_mlp):
    T, D = x.shape
    tm = _tile(T, _TM_SMALL)

    def body(m_ref, w_ref, x_ref, g_ref, x2_ref, h2_ref):
        x2_ref[...] = x_ref[...] + _dot(m_ref[...], w_ref[...])
        _rms_rows(x2_ref, g_ref, h2_ref, tm, D)

    row = pl.BlockSpec((tm, D), lambda i: (i, 0))
    return pl.pallas_call(
        body, name="out_proj", grid=(T // tm,),
        in_specs=[row, pl.BlockSpec((D, D), lambda i: (0, 0)), row, pl.BlockSpec((1, D), lambda i: (0, 0))],
        out_specs=[row, row],
        out_shape=[jax.ShapeDtypeStruct((T, D), f32), jax.ShapeDtypeStruct((T, D), bf16)],
        compiler_params=_cparams("parallel"),
    )(merged, w_out, x, g_mlp)


def _ffn_fwd(up, conv_w, conv_b):
    T, F2 = up.shape
    F = F2 // 2
    K = 3
    tc = _tile(T, 4 * _TC)
    cw = _tile(F, _CW)
    ns = F // cw
    H = _CONV_HALO

    def body(gp_ref, val_ref, cw_ref, cb_ref, z_ref, ext_ref):
        @pl.when(pl.program_id(1) == 0)
        def _():
            ext_ref[pl.ds(0, H), :] = jnp.zeros((H, cw), f32)

        ext_ref[pl.ds(H, tc), :] = gp_ref[...]
        for cols, rows in _slabs(tc, cw):
            c = _causal_conv(ext_ref, cw_ref, cb_ref, rows.size, K, cols, rows.start)
            z_ref[rows, cols] = (_gelu(c) * val_ref[rows, cols]).astype(bf16)
        ext_ref[pl.ds(0, H), :] = ext_ref[pl.ds(tc, H), :]

    return pl.pallas_call(
        body, name="ffn_fwd", grid=(ns, T // tc),
        in_specs=[pl.BlockSpec((tc, cw), lambda s, c: (c, s)), pl.BlockSpec((tc, cw), lambda s, c: (c, ns + s)),
                  pl.BlockSpec((8, cw), lambda s, c: (0, s)), pl.BlockSpec((1, cw), lambda s, c: (0, s))],
        out_specs=pl.BlockSpec((tc, cw), lambda s, c: (c, s)),
        out_shape=jax.ShapeDtypeStruct((T, F), bf16),
        scratch_shapes=[pltpu.VMEM((H + tc, cw), f32)],
        compiler_params=_cparams("parallel", "arbitrary"),
    )(up, up, conv_w, conv_b)


def _down_loss(z, w_down, x2, target, g_final):
    T, F = z.shape
    D = x2.shape[1]
    tm, tk = _tile(T, _TM), _tile(F, _TK_DOWN)
    nk = F // tk

    def body(z_ref, w_ref, x2_ref, t_ref, g_ref, dx_ref, dxb_ref, loss_ref, dg_ref, acc_ref):
        i, k = pl.program_id(0), pl.program_id(1)

        @pl.when(k == 0)
        def _():
            acc_ref[...] = x2_ref[...]

        @pl.when((i == 0) & (k == 0))
        def _():
            loss_ref[...] = jnp.zeros_like(loss_ref)
            dg_ref[...] = jnp.zeros_like(dg_ref)

        acc_ref[...] += _dot(z_ref[...], w_ref[...])

        @pl.when(k == nk - 1)
        def _():
            g = g_ref[...]
            sq = jnp.zeros((_ROWS, 1), f32)
            dgs = jnp.zeros((_ROWS, D), f32)
            for r0 in range(0, tm, _ROWS):
                rows = pl.ds(r0, _ROWS)
                x3 = acc_ref[rows, :]
                r = lax.rsqrt(jnp.mean(x3 * x3, axis=-1, keepdims=True) + _EPS)
                xr = x3 * r
                e = xr * g - t_ref[rows, :]
                sq = sq + jnp.sum(e * e, axis=-1, keepdims=True)
                dy = e * (1.0 / D)
                gy = dy * g
                dx = r * gy - x3 * ((r * r * r) * jnp.mean(x3 * gy, axis=-1, keepdims=True))
                dgs = dgs + dy * xr
                dx_ref[rows, :] = dx
                dxb_ref[rows, :] = dx.astype(bf16)
            loss_ref[...] += (0.5 / D) * jnp.sum(sq)
            dg_ref[...] += jnp.sum(dgs, axis=0, keepdims=True)

    row = pl.BlockSpec((tm, D), lambda i, k: (i, 0))
    vec = pl.BlockSpec((1, D), lambda i, k: (0, 0))
    return pl.pallas_call(
        body, name="down_loss", grid=(T // tm, nk),
        in_specs=[pl.BlockSpec((tm, tk), lambda i, k: (i, k)), pl.BlockSpec((tk, D), lambda i, k: (k, 0)), row, row, vec],
        out_specs=[row, row, pl.BlockSpec((8, _LANE), lambda i, k: (0, 0)), vec],
        out_shape=[jax.ShapeDtypeStruct((T, D), f32), jax.ShapeDtypeStruct((T, D), bf16),
                   jax.ShapeDtypeStruct((8, _LANE), f32), jax.ShapeDtypeStruct((1, D), f32)],
        scratch_shapes=[pltpu.VMEM((tm, D), f32)],
        compiler_params=_cparams("arbitrary", "arbitrary"),
    )(z, w_down, x2, target, g_final)


def _ffn_bwd(dz, up, conv_w, conv_b):
    T, F = dz.shape
    K = 3
    tc = _tile(T, 4 * _TC)
    cw = _tile(F, _CW)
    ns, nt = F // cw, T // tc
    H = _CONV_HALO

    def body(dz_ref, gp_ref, val_ref, gph_ref, cw_ref, cb_ref, dup_ref, dcw_ref, dcb_ref, ext_ref, dext_ref):
        j = pl.program_id(1)
        first = j == nt - 1

        @pl.when(j == 0)
        def _():
            dext_ref[pl.ds(tc, H), :] = jnp.zeros((H, cw), f32)
            dcw_ref[...] = jnp.zeros_like(dcw_ref)
            dcb_ref[...] = jnp.zeros_like(dcb_ref)

        ext_ref[pl.ds(0, H), :] = jnp.where(first, 0.0, gph_ref[...])
        ext_ref[pl.ds(H, tc), :] = gp_ref[...]
        def visit(cols, rows, sums):
            r0, n = rows.start, rows.size
            gp = ext_ref[pl.ds(H + r0, n), cols]
            c = _causal_conv(ext_ref, cw_ref, cb_ref, n, K, cols, r0)
            ge, gg = _gelu_both(c)
            dzv = dz_ref[rows, cols].astype(f32)
            dup_ref[1, rows, cols] = (dzv * ge).astype(bf16)
            dc = dzv * val_ref[rows, cols] * gg
            dext_ref[rows, cols] = dc
            dgp = cw_ref[K - 1:K, cols] * dc
            new = [None] * K + [sums[K] + _fold8(dc)]
            new[K - 1] = sums[K - 1] + _fold8(gp * dc)
            for k in range(K - 1):
                sh = dext_ref[pl.ds(r0 + K - 1 - k, n), cols]
                dgp = dgp + cw_ref[k:k + 1, cols] * sh
                new[k] = sums[k] + _fold8(gp * sh)
            dup_ref[0, rows, cols] = dgp.astype(bf16)
            return new

        def flush(cols, totals):
            for k in range(K):
                dcw_ref[k:k + 1, cols] += totals[k]
            dcb_ref[:, cols] += totals[K]

        _slabs_with_sums(tc, cw, K + 1, visit, flush, reverse=True)
        dext_ref[pl.ds(tc, H), :] = dext_ref[pl.ds(0, H), :]

    hblk = tc // H
    return pl.pallas_call(
        body, name="ffn_bwd", grid=(ns, nt),
        in_specs=[pl.BlockSpec((tc, cw), lambda s, j: (nt - 1 - j, s)),
                  pl.BlockSpec((tc, cw), lambda s, j: (nt - 1 - j, s)),
                  pl.BlockSpec((tc, cw), lambda s, j: (nt - 1 - j, ns + s)),
                  pl.BlockSpec((H, cw), lambda s, j: (jnp.maximum((nt - 1 - j) * hblk - 1, 0), s)),
                  pl.BlockSpec((8, cw), lambda s, j: (0, s)), pl.BlockSpec((1, cw), lambda s, j: (0, s))],
        out_specs=[pl.BlockSpec((2, tc, cw), lambda s, j: (0, nt - 1 - j, s)),
                   pl.BlockSpec((8, cw), lambda s, j: (0, s)), pl.BlockSpec((1, cw), lambda s, j: (0, s))],
        out_shape=[jax.ShapeDtypeStruct((2, T, F), bf16), jax.ShapeDtypeStruct((8, F), f32), jax.ShapeDtypeStruct((1, F), f32)],
        scratch_shapes=[pltpu.VMEM((H + tc, cw), f32), pltpu.VMEM((tc + H, cw), f32)],
        compiler_params=_cparams("parallel", "arbitrary"),
    )(dz, up, up, up, conv_w, conv_b)


def _norm_bwd_matmul(name, a, a_spec, nk, w, w_spec, x, g, dres):
    T, D = x.shape
    tm = a_spec.block_shape[-2]

    def body(a_ref, w_ref, x_ref, g_ref, dr_ref, dx_ref, dxb_ref, dg_ref, acc_ref):
        i, k = pl.program_id(0), pl.program_id(1)

        @pl.when((i == 0) & (k == 0))
        def _():
            dg_ref[...] = jnp.zeros_like(dg_ref)

        @pl.when(k == 0)
        def _():
            acc_ref[...] = _dot_nt(a_ref[...], w_ref[...])

        @pl.when(k > 0)
        def _():
            acc_ref[...] += _dot_nt(a_ref[...], w_ref[...])

        @pl.when(k == nk - 1)
        def _():
            g = g_ref[...]
            dgs = jnp.zeros((_ROWS, D), f32)
            for r0 in range(0, tm, _ROWS):
                rows = pl.ds(r0, _ROWS)
                x = x_ref[rows, :]
                dh = acc_ref[rows, :]
                r = lax.rsqrt(jnp.mean(x * x, axis=-1, keepdims=True) + _EPS)
                gy = dh * g
                dx = dr_ref[rows, :] + (r * gy - x * ((r * r * r) * jnp.mean(x * gy, axis=-1, keepdims=True)))
                dgs = dgs + dh * (x * r)
                dx_ref[rows, :] = dx
                dxb_ref[rows, :] = dx.astype(bf16)
            dg_ref[...] += jnp.sum(dgs, axis=0, keepdims=True)

    row = pl.BlockSpec((tm, D), lambda i, k: (i, 0))
    vec = pl.BlockSpec((1, D), lambda i, k: (0, 0))
    return pl.pallas_call(
        body, name=name, grid=(T // tm, nk),
        in_specs=[a_spec, w_spec, row, vec, row],
        out_specs=[row, row, vec],
        out_shape=[jax.ShapeDtypeStruct((T, D), f32), jax.ShapeDtypeStruct((T, D), bf16), jax.ShapeDtypeStruct((1, D), f32)],
        scratch_shapes=[pltpu.VMEM((tm, D), f32)],
        compiler_params=_cparams("arbitrary", "arbitrary"),
    )(a, w, x, g, dres)


def _merge_bwd(dx2b, w_out, p, q, proj, b_gate, PW):
    T, D = p.shape
    tm, tn = _tile(T, _TM), _tile(D, PW)
    nj = D // tn
    off = (PW + 2 * D) // tn

    def body(dx_ref, w_ref, p_ref, q_ref, l0_ref, l1_ref, b0_ref, b1_ref, dp_ref, dq_ref, dl0_ref, dl1_ref, db0_ref, db1_ref,
             dm_ref):
        @pl.when(pl.program_id(1) == 0)
        def _():
            db0_ref[...] = jnp.zeros_like(db0_ref)
            db1_ref[...] = jnp.zeros_like(db1_ref)

        dm_ref[...] = _dot_nt(dx_ref[...], w_ref[...])

        def visit(cols, rows, sums):
            dm = dm_ref[rows, cols]
            g0 = _sigmoid(l0_ref[rows, cols] + b0_ref[:, cols])
            g1 = _sigmoid(l1_ref[rows, cols] + b1_ref[:, cols])
            dp_ref[rows, cols] = (g0 * dm).astype(bf16)
            dq_ref[rows, cols] = (g1 * dm).astype(bf16)
            dl0 = dm * p_ref[rows, cols].astype(f32) * (g0 * (1.0 - g0))
            dl1 = dm * q_ref[rows, cols].astype(f32) * (g1 * (1.0 - g1))
            dl0_ref[rows, cols] = dl0.astype(bf16)
            dl1_ref[rows, cols] = dl1.astype(bf16)
            return [sums[0] + _fold8(dl0), sums[1] + _fold8(dl1)]

        def flush(cols, totals):
            db0_ref[:, cols] += totals[0]
            db1_ref[:, cols] += totals[1]

        _slabs_with_sums(tm, tn, 2, visit, flush)

    tile = pl.BlockSpec((tm, tn), lambda j, i: (i, j))
    vecj = pl.BlockSpec((1, tn), lambda j, i: (0, j))
    tb = jax.ShapeDtypeStruct((T, D), bf16)
    vb = jax.ShapeDtypeStruct((1, D), f32)
    return pl.pallas_call(
        body, name="merge_bwd", grid=(nj, T // tm),
        in_specs=[pl.BlockSpec((tm, D), lambda j, i: (i, 0)), pl.BlockSpec((tn, D), lambda j, i: (j, 0)), tile, tile,
                  pl.BlockSpec((tm, tn), lambda j, i: (i, off + j)), pl.BlockSpec((tm, tn), lambda j, i: (i, off + nj + j)),
                  vecj, pl.BlockSpec((1, tn), lambda j, i: (0, nj + j))],
        out_specs=[tile, tile, tile, tile, vecj, vecj],
        out_shape=[tb, tb, tb, tb, vb, vb],
        scratch_shapes=[pltpu.VMEM((tm, tn), f32)],
        compiler_params=_cparams("parallel", "arbitrary"),
    )(dx2b, w_out, p, q, proj, proj, b_gate, b_gate)


def _lru_bwd(dy, proj, hs, D, conv_w, conv_b, w_a, b_a, w_i, b_i, lam):
    T = dy.shape[0]
    NB, bw, _ = w_a.shape
    K = 4
    tc = _tile(T, _TC)
    nt = T // tc
    H = _CONV_HALO
    hb = D // 2

    def body(dy_ref, u0_ref, u1_ref, g0_ref, g1_ref, hs_ref, uh0_ref, uh1_ref, hh_ref,
             cw_ref, cb_ref, wa_ref, ba_ref, wi_ref, bi_ref, lam_ref,
             du_ref, dg_ref, dwa_ref, dwi_ref, dcw_ref, dvec_ref,
             ext_ref, hext_ref, dext_ref, q_ref, a_scr, r_scr, i_scr, v_scr, g_scr, car_scr):
        j = pl.program_id(0)
        first = j == nt - 1

        @pl.when(j == 0)
        def _():
            dext_ref[pl.ds(tc, H), :] = jnp.zeros((H, D), f32)
            car_scr[...] = jnp.zeros_like(car_scr)
            dwa_ref[...] = jnp.zeros_like(dwa_ref)
            dwi_ref[...] = jnp.zeros_like(dwi_ref)
            dcw_ref[...] = jnp.zeros_like(dcw_ref)
            dvec_ref[...] = jnp.zeros_like(dvec_ref)

        ext_ref[pl.ds(0, H), 0:hb] = jnp.where(first, 0.0, uh0_ref[...])
        ext_ref[pl.ds(0, H), hb:D] = jnp.where(first, 0.0, uh1_ref[...])
        ext_ref[pl.ds(H, tc), 0:hb] = u0_ref[...]
        ext_ref[pl.ds(H, tc), hb:D] = u1_ref[...]
        hext_ref[pl.ds(0, H), :] = jnp.where(first, 0.0, hh_ref[...])
        hext_ref[pl.ds(H, tc), :] = hs_ref[...]
        lamv = lam_ref[...]
        sp = _softplus(-lamv)

        for cols, rows in _slabs(tc, D):
            v_scr[rows, cols] = _causal_conv(ext_ref, cw_ref, cb_ref, rows.size, K, cols, rows.start)
        for b in range(NB):
            cols = slice(b * bw, (b + 1) * bw)
            vb = v_scr[:, cols].astype(bf16)
            r_scr[:, cols] = _dot(vb, wa_ref[b])
            i_scr[:, cols] = _dot(vb, wi_ref[b])
        for cols, rows in _slabs(tc, D):
            r = _sigmoid(r_scr[rows, cols] + ba_ref[:, cols])
            r_scr[rows, cols] = r
            i_scr[rows, cols] = _sigmoid(i_scr[rows, cols] + bi_ref[:, cols])
            a = jnp.exp(-_LRU_C * r * sp[:, cols])
            a_scr[rows, cols] = a
            g_ref, gcols = (g0_ref, cols) if cols.start < hb else (g1_ref, slice(cols.start - hb, cols.stop - hb))
            ge, gg = _gelu_both(g_ref[rows, gcols])
            dyv = dy_ref[rows, cols]
            dho = dyv * ge
            g_scr[rows, cols] = dho
            q_ref[rows, cols] = a * dho
            dg_ref[rows, cols] = (dyv * hs_ref[rows, cols] * gg).astype(bf16)

        q_ref[pl.ds(tc, 1), :] = car_scr[0:1, :]
        _scan_rows(a_scr, q_ref, q_ref, car_scr, tc, D, reverse=True)

        def gates(cols, rows, sums):
            g = g_scr[rows, cols] + q_ref[pl.ds(rows.start + 1, rows.size), cols]
            v, r, i, a = v_scr[rows, cols], r_scr[rows, cols], i_scr[rows, cols], a_scr[rows, cols]
            mult = jnp.sqrt(1.0 - a * a)
            h_prev = hext_ref[pl.ds(H - 1 + rows.start, rows.size), cols]
            gm = g * mult
            dext_ref[rows, cols] = gm * i
            dlog_a = (g * h_prev - g * (i * v) * (a / mult)) * a
            dpr = dlog_a * (-_LRU_C * sp[:, cols]) * (r * (1.0 - r))
            dpi = gm * v * (i * (1.0 - i))
            r_scr[rows, cols] = dpr
            i_scr[rows, cols] = dpi
            return [sums[0] + _fold8(dpr), sums[1] + _fold8(dpi), sums[2] + _fold8(dlog_a * (-_LRU_C * r))]

        def gates_flush(cols, totals):
            dvec_ref[1:2, cols] += totals[0]
            dvec_ref[2:3, cols] += totals[1]
            dvec_ref[3:4, cols] += totals[2] * (-_sigmoid(-lamv[:, cols]))

        _slabs_with_sums(tc, D, 3, gates, gates_flush)
        for b in range(NB):
            cols = slice(b * bw, (b + 1) * bw)
            dprb, dpib, vb = r_scr[:, cols].astype(bf16), i_scr[:, cols].astype(bf16), v_scr[:, cols].astype(bf16)
            dext_ref[pl.ds(0, tc), cols] += _dot_nt(dprb, wa_ref[b]) + _dot_nt(dpib, wi_ref[b])
            dwa_ref[b] += _dot_tn(vb, dprb)
            dwi_ref[b] += _dot_tn(vb, dpib)

        def conv_t(cols, rows, sums):
            r0, n = rows.start, rows.size
            u = ext_ref[pl.ds(H + r0, n), cols]
            dv = dext_ref[rows, cols]
            du = cw_ref[K - 1:K, cols] * dv
            new = [None] * K + [sums[K] + _fold8(dv)]
            new[K - 1] = sums[K - 1] + _fold8(u * dv)
            for k in range(K - 1):
                sh = dext_ref[pl.ds(r0 + K - 1 - k, n), cols]
                du = du + cw_ref[k:k + 1, cols] * sh
                new[k] = sums[k] + _fold8(u * sh)
            du_ref[rows, cols] = du.astype(bf16)
            return new

        def conv_t_flush(cols, totals):
            for k in range(K):
                dcw_ref[k:k + 1, cols] += totals[k]
            dvec_ref[0:1, cols] += totals[K]

        _slabs_with_sums(tc, D, K + 1, conv_t, conv_t_flush)
        dext_ref[pl.ds(tc, H), :] = dext_ref[pl.ds(0, H), :]

    hblk = tc // H
    rev = lambda j: nt - 1 - j
    halo = lambda j: jnp.maximum((nt - 1 - j) * hblk - 1, 0)
    vec = pl.BlockSpec((1, D), lambda j: (0, 0))
    wspec = pl.BlockSpec((NB, bw, bw), lambda j: (0, 0, 0))
    acc8 = pl.BlockSpec((8, D), lambda j: (0, 0))
    big = pltpu.VMEM((tc, D), f32)
    return pl.pallas_call(
        body, name="lru_bwd", grid=(nt,),
        in_specs=[pl.BlockSpec((tc, D), lambda j: (rev(j), 0)),
                  pl.BlockSpec((tc, hb), lambda j: (rev(j), 1)), pl.BlockSpec((tc, hb), lambda j: (rev(j), 2)),
                  pl.BlockSpec((tc, hb), lambda j: (rev(j), 3)), pl.BlockSpec((tc, hb), lambda j: (rev(j), 4)),
                  pl.BlockSpec((tc, D), lambda j: (rev(j), 0)),
                  pl.BlockSpec((H, hb), lambda j: (halo(j), 1)), pl.BlockSpec((H, hb), lambda j: (halo(j), 2)),
                  pl.BlockSpec((H, D), lambda j: (halo(j), 0)),
                  acc8, vec, wspec, vec, wspec, vec, vec],
        out_specs=[pl.BlockSpec((tc, D), lambda j: (rev(j), 0)), pl.BlockSpec((tc, D), lambda j: (rev(j), 0)),
                   wspec, wspec, acc8, acc8],
        out_shape=[jax.ShapeDtypeStruct((T, D), bf16), jax.ShapeDtypeStruct((T, D), bf16),
                   jax.ShapeDtypeStruct((NB, bw, bw), f32), jax.ShapeDtypeStruct((NB, bw, bw), f32),
                   jax.ShapeDtypeStruct((8, D), f32), jax.ShapeDtypeStruct((8, D), f32)],
        scratch_shapes=[pltpu.VMEM((H + tc, D), f32), pltpu.VMEM((H + tc, D), f32), pltpu.VMEM((tc + H, D), f32),
                        pltpu.VMEM((tc + H, D), f32), big, big, big, big, big, pltpu.VMEM((8, D), f32)],
        compiler_params=_cparams("arbitrary"),
    )(dy, proj, proj, proj, proj, hs, proj, proj, hs, conv_w, conv_b, w_a, b_a, w_i, b_i, lam)


def _pool_bwd(dy, proj, w_pool, pool_scale):
    T, PW = dy.shape
    G, gw, _ = w_pool.shape
    tc = _tile(T, _TC)
    nt = T // tc
    H = _POOL_HALO

    def body(dy_ref, u_ref, uh_ref, w_ref, s_ref, du_ref, dw_ref, ds_ref, ext_ref, eext_ref):
        j = pl.program_id(0)
        first = j == nt - 1
        row0 = (nt - 1 - j) * tc

        @pl.when(j == 0)
        def _():
            eext_ref[pl.ds(tc, H), :] = jnp.zeros((H, PW), f32)
            dw_ref[...] = jnp.zeros_like(dw_ref)
            ds_ref[...] = jnp.zeros_like(ds_ref)

        ext_ref[pl.ds(0, H), :] = jnp.where(first, 0.0, uh_ref[...])
        ext_ref[pl.ds(H, tc), :] = u_ref[...]
        t_glob = row0 + lax.broadcasted_iota(jnp.int32, (tc, 1), 0)
        dds = []
        for g, (m, u) in enumerate(_window_means(ext_ref, row0, tc, gw)):
            cols = slice(g * gw, (g + 1) * gw)
            d = (m - u).astype(bf16)
            yraw = _dot(d, w_ref[g])
            dyv = dy_ref[:, cols]
            ds_ref[:, cols] += jnp.sum(dyv * yraw, axis=0, keepdims=True)
            dyr = (dyv * s_ref[:, cols]).astype(bf16)
            dd = _dot_nt(dyr, w_ref[g])
            dw_ref[g] += _dot_tn(d, dyr)
            cnt = jnp.minimum(t_glob + 1, _POOL_WINDOWS[g]).astype(f32)
            eext_ref[pl.ds(0, tc), cols] = dd / cnt
            dds.append(dd)
        n = tc + H
        for g, w in enumerate(_POOL_WINDOWS):
            cols = slice(g * gw, (g + 1) * gw)
            s = eext_ref[:, cols]
            st = 1
            while st < w:
                s = s + pltpu.roll(s, n - st, 0)
                st *= 2
            du_ref[:, cols] = (s[0:tc, :] - dds[g]).astype(bf16)
        eext_ref[pl.ds(tc, H), :] = eext_ref[pl.ds(0, H), :]

    hblk = tc // H
    return pl.pallas_call(
        body, name="pool_bwd", grid=(nt,),
        in_specs=[pl.BlockSpec((tc, PW), lambda j: (nt - 1 - j, 0)), pl.BlockSpec((tc, PW), lambda j: (nt - 1 - j, 0)),
                  pl.BlockSpec((H, PW), lambda j: (jnp.maximum((nt - 1 - j) * hblk - 1, 0), 0)),
                  pl.BlockSpec((G, gw, gw), lambda j: (0, 0, 0)), pl.BlockSpec((1, PW), lambda j: (0, 0))],
        out_specs=[pl.BlockSpec((tc, PW), lambda j: (nt - 1 - j, 0)), pl.BlockSpec((G, gw, gw), lambda j: (0, 0, 0)),
                   pl.BlockSpec((1, PW), lambda j: (0, 0))],
        out_shape=[jax.ShapeDtypeStruct((T, PW), bf16), jax.ShapeDtypeStruct((G, gw, gw), f32), jax.ShapeDtypeStruct((1, PW), f32)],
        scratch_shapes=[pltpu.VMEM((H + tc, PW), f32), pltpu.VMEM((tc + H, PW), f32)],
        compiler_params=_cparams("arbitrary"),
    )(dy, proj, proj, w_pool, pool_scale)


_MESH = pl.DeviceIdType.MESH
_HBM = pl.BlockSpec(memory_space=pltpu.HBM)


def _slab(ref, kind, blk, n):
    start = blk * n
    if n % _LANE == 0:
        start = pl.multiple_of(start, _LANE)
    if kind == "col":
        return ref.at[:, pl.ds(start, n)]
    if kind == "row":
        return ref.at[pl.ds(start, n), :]
    if kind == "mid":
        return ref.at[:, pl.ds(start, n), :]
    raise ValueError(kind)


def _my_place():
    x, y, c = lax.axis_index("x"), lax.axis_index("y"), lax.axis_index("c")
    return x, y, c


def _blk(px, py, pc):
    return 4 * px + 2 * py + pc


_SEM = pl.BlockSpec(memory_space=pltpu.SEMAPHORE)
_EFFECT = pltpu.SideEffectType.DATAFLOW_SIDE_EFFECTING


_ALL_PEERS = (1, 2, 3, 4, 5, 6, 7)


def _peers(x, y, c):
    return [(k, (x ^ (k >> 2), y ^ ((k >> 1) & 1), c ^ (k & 1))) for k in range(1, 8)]


class _Route:
    def __init__(self, mode, kind, size):
        self.mode, self.kind, self.size = mode, kind, size

    def src(self, ref, peer_blk):
        return ref if self.mode == "gather" else _slab(ref, self.kind, peer_blk, self.size)

    def dst(self, ref, origin_blk):
        return _slab(ref, self.kind, origin_blk, self.size) if self.mode == "gather" else ref.at[origin_blk]


def _send_start(name, srcs, lands, routes, groups):
    nt, ng = len(srcs), len(groups)

    def body(*refs):
        src_refs, land_refs = refs[:nt], refs[nt:2 * nt]
        sems = refs[2 * nt:2 * nt + 2 * ng]
        token = refs[-1]
        x, y, c = _my_place()
        me = _blk(x, y, c)
        for gi, (grp, ks) in enumerate(groups):
            for pos, t in enumerate(grp):
                for k, peer in _peers(x, y, c):
                    if k in ks:
                        s = len(ks) * pos + ks.index(k)
                        pltpu.make_async_remote_copy(
                            src_ref=routes[t].src(src_refs[t], _blk(*peer)), dst_ref=routes[t].dst(land_refs[t], me),
                            send_sem=sems[2 * gi].at[s], recv_sem=sems[2 * gi + 1].at[s],
                            device_id=peer, device_id_type=_MESH).start()
        token[...] = jnp.zeros_like(token)

    hbm = lambda a: pltpu.HBM(a.shape, a.dtype)
    out_shape = []
    for grp, ks in groups:
        out_shape += [pltpu.SemaphoreType.DMA((len(ks) * len(grp),)), pltpu.SemaphoreType.DMA((len(ks) * len(grp),))]
    out_shape += [hbm(a) for a in srcs] + [hbm(a) for a in lands] + [jax.ShapeDtypeStruct((8, _LANE), f32)]
    res = pl.pallas_call(
        body, name=name, out_shape=out_shape,
        in_specs=[_HBM] * (2 * nt),
        out_specs=[_SEM] * (2 * ng) + [_HBM] * (2 * nt) + [pl.BlockSpec(memory_space=pltpu.VMEM)],
        input_output_aliases={t: 2 * ng + t for t in range(2 * nt)},
        compiler_params=pltpu.CompilerParams(has_side_effects=_EFFECT),
    )(*[pltpu.with_memory_space_constraint(a, pltpu.HBM) for a in list(srcs) + list(lands)])
    sems = [(res[2 * g], res[2 * g + 1]) for g in range(ng)]
    return sems, res[2 * ng:2 * ng + nt], res[2 * ng + nt:2 * ng + 2 * nt], res[-1]


def _send_wait(name, srcs, lands, routes, sems, after, ks=_ALL_PEERS):
    n = len(srcs)

    def body(*refs):
        src_refs, land_refs = refs[:n], refs[n:2 * n]
        send_sems, recv_sems = refs[2 * n], refs[2 * n + 1]
        x, y, c = _my_place()
        for pos in range(n):
            for k, peer in _peers(x, y, c):
                if k in ks:
                    pb = _blk(*peer)
                    s = len(ks) * pos + ks.index(k)
                    cp = pltpu.make_async_remote_copy(
                        src_ref=routes[pos].src(src_refs[pos], pb), dst_ref=routes[pos].dst(land_refs[pos], pb),
                        send_sem=send_sems.at[s], recv_sem=recv_sems.at[s], device_id=peer, device_id_type=_MESH)
                    cp.wait_send()
                    cp.wait_recv()

    hbm = lambda a: pltpu.HBM(a.shape, a.dtype)
    res = pl.pallas_call(
        body, name=name, out_shape=[hbm(a) for a in srcs] + [hbm(a) for a in lands],
        in_specs=[_HBM] * (2 * n) + [_SEM, _SEM, pl.BlockSpec(memory_space=pl.ANY)],
        out_specs=[_HBM] * (2 * n),
        input_output_aliases={t: t for t in range(2 * n)},
        compiler_params=pltpu.CompilerParams(has_side_effects=_EFFECT),
    )(*srcs, *lands, sems[0], sems[1], after)
    return res[:n], res[n:]


def _forward_start(name, lands, routes, ks):
    n = len(lands)

    def body(*refs):
        land_refs, send_sems, recv_sems, token = refs[:n], refs[n], refs[n + 1], refs[-1]
        x, y, c = _my_place()
        for pos in range(n):
            for i, k in enumerate(ks):
                part = routes[pos].dst(land_refs[pos], _blk(x ^ (k >> 2), y ^ ((k >> 1) & 1), c))
                pltpu.make_async_remote_copy(
                    src_ref=part, dst_ref=part, send_sem=send_sems.at[len(ks) * pos + i],
                    recv_sem=recv_sems.at[len(ks) * pos + i], device_id=(x, y, 1 - c), device_id_type=_MESH).start()
        token[...] = jnp.zeros_like(token)

    hbm = lambda a: pltpu.HBM(a.shape, a.dtype)
    sem = pltpu.SemaphoreType.DMA((len(ks) * n,))
    res = pl.pallas_call(
        body, name=name, out_shape=[sem, sem] + [hbm(a) for a in lands] + [jax.ShapeDtypeStruct((8, _LANE), f32)],
        in_specs=[_HBM] * n, out_specs=[_SEM, _SEM] + [_HBM] * n + [pl.BlockSpec(memory_space=pltpu.VMEM)],
        input_output_aliases={t: 2 + t for t in range(n)},
        compiler_params=pltpu.CompilerParams(has_side_effects=_EFFECT),
    )(*[pltpu.with_memory_space_constraint(a, pltpu.HBM) for a in lands])
    return (res[0], res[1]), res[2:2 + n], res[-1]


def _forward_wait(name, lands, routes, sems, after, ks):
    n = len(lands)

    def body(*refs):
        land_refs, send_sems, recv_sems = refs[:n], refs[n], refs[n + 1]
        x, y, c = _my_place()
        for pos in range(n):
            for i, k in enumerate(ks):
                px, py = x ^ (k >> 2), y ^ ((k >> 1) & 1)
                cp = pltpu.make_async_remote_copy(
                    src_ref=routes[pos].dst(land_refs[pos], _blk(px, py, c)),
                    dst_ref=routes[pos].dst(land_refs[pos], _blk(px, py, 1 - c)),
                    send_sem=send_sems.at[len(ks) * pos + i], recv_sem=recv_sems.at[len(ks) * pos + i],
                    device_id=(x, y, 1 - c), device_id_type=_MESH)
                cp.wait_send()
                cp.wait_recv()

    hbm = lambda a: pltpu.HBM(a.shape, a.dtype)
    return pl.pallas_call(
        body, name=name, out_shape=[hbm(a) for a in lands],
        in_specs=[_HBM] * n + [_SEM, _SEM, pl.BlockSpec(memory_space=pl.ANY)], out_specs=[_HBM] * n,
        input_output_aliases={t: t for t in range(n)},
        compiler_params=pltpu.CompilerParams(has_side_effects=_EFFECT),
    )(*lands, sems[0], sems[1], after)


def _copy_own(name, src, land, route, me):
    gather = route.mode == "gather"
    shard = src.shape if gather else land.shape[1:]
    lead = () if gather else (None,)

    if route.kind == "mid":
        grid = (1,)
        at_full = lambda i, me: (0, me[0], 0)
        at_shard = lambda i, me: (0, 0, 0)
        block = tuple(shard)
    else:
        rows, width = shard
        tr = _tile(rows, 512, 16)
        grid = (rows // tr,)
        block = (tr, width)
        if route.kind == "col":
            at_full = lambda i, me: (i, me[0])
        else:
            at_full = lambda i, me: (me[0] * grid[0] + i, 0)
        at_shard = lambda i, me: (i, 0)
    if gather:
        in_map, out_map = at_shard, at_full
    else:
        in_map, out_map = at_full, (lambda i, me: (me[0], *at_shard(i, me)))

    def body(me_ref, src_ref, land_ref, out_ref):
        out_ref[...] = src_ref[...]

    return pl.pallas_call(
        body, name=name, out_shape=jax.ShapeDtypeStruct(land.shape, land.dtype),
        grid_spec=pltpu.PrefetchScalarGridSpec(
            num_scalar_prefetch=1, grid=grid,
            in_specs=[pl.BlockSpec(block, in_map), pl.BlockSpec(memory_space=pl.ANY)],
            out_specs=pl.BlockSpec(lead + block, out_map)),
        input_output_aliases={2: 0},
        compiler_params=_cparams("arbitrary"),
    )(me, src, land)


def _place_own(name, srcs, lands, routes):
    me = _blk(*_my_place()).astype(jnp.int32).reshape(1)
    return [_copy_own(f"{name}_{t}", s, l, r, me) for t, (s, l, r) in enumerate(zip(srcs, lands, routes))]


def _exchange(fulls, kinds, sizes, whole):
    arrays = list(fulls) + list(whole)
    nt, nf = len(arrays), len(fulls)

    def shard_shape(t):
        s = list(arrays[t].shape)
        if t < nf:
            s[{"col": 1, "row": 0, "mid": 1}[kinds[t]]] = sizes[t]
        return tuple(s)

    def body(*refs):
        ins, outs = refs[:nt], refs[nt:2 * nt]
        send_sems, recv_sems, local_sems = refs[2 * nt:]
        x, y, c = _my_place()
        me = _blk(x, y, c)

        def src(t, blk):
            return _slab(ins[t], kinds[t], blk, sizes[t]) if t < nf else ins[t]

        mine = [pltpu.make_async_copy(src(t, me), outs[t].at[me], local_sems.at[t]) for t in range(nt)]
        for cp in mine:
            cp.start()
        sent = []
        for t in range(nt):
            for k in range(1, 8):
                peer = (x ^ (k >> 2), y ^ ((k >> 1) & 1), c ^ (k & 1))
                pb = _blk(*peer)
                cp = pltpu.make_async_remote_copy(
                    src_ref=src(t, pb), dst_ref=outs[t].at[me],
                    send_sem=send_sems.at[7 * t + k - 1], recv_sem=recv_sems.at[7 * t + k - 1],
                    device_id=peer, device_id_type=_MESH)
                cp.start()
                sent.append((cp, t, k, pb))
        for cp, t, k, pb in sent:
            pltpu.make_async_remote_copy(
                src_ref=src(t, pb), dst_ref=outs[t].at[pb],
                send_sem=send_sems.at[7 * t + k - 1], recv_sem=recv_sems.at[7 * t + k - 1],
                device_id=(x, y, c), device_id_type=_MESH).wait_recv()
        for cp, _, _, _ in sent:
            cp.wait_send()
        for cp in mine:
            cp.wait()

    return pl.pallas_call(
        body, name="exchange_grads",
        in_specs=[_HBM] * nt, out_specs=[_HBM] * nt,
        out_shape=[jax.ShapeDtypeStruct((_N_DEV,) + shard_shape(t), arrays[t].dtype) for t in range(nt)],
        scratch_shapes=[pltpu.SemaphoreType.DMA((7 * nt,)), pltpu.SemaphoreType.DMA((7 * nt,)), pltpu.SemaphoreType.DMA((nt,))],
        compiler_params=pltpu.CompilerParams(has_side_effects=True),
    )(*arrays)


def _adamw_update(w_ref, m_ref, v_ref, g, g_ref, d_ref, nm_ref, nv_ref):
    c1 = 1.0 - _ADAM_B1 ** _ADAM_STEP
    c2 = 1.0 - _ADAM_B2 ** _ADAM_STEP
    nm = _ADAM_B1 * m_ref[...] + (1.0 - _ADAM_B1) * g
    nv = _ADAM_B2 * v_ref[...] + (1.0 - _ADAM_B2) * (g * g)
    g_ref[...] = g
    nm_ref[...] = nm
    nv_ref[...] = nv
    d_ref[...] = -_ADAM_LR * ((nm / c1) / (jnp.sqrt(nv / c2) + _ADAM_EPS) + _ADAM_WD * w_ref[...])


def _adamw_small(me, ws, ms, vs, parts):
    n = len(ws)

    def body(me_ref, *refs):
        ins, outs = refs[:4 * n], refs[4 * n:]
        for t in range(n):
            w_ref, m_ref, v_ref, p_ref = ins[4 * t:4 * t + 4]
            r = w_ref.shape[0]
            g = p_ref[0, 0:r, :]
            for s in range(1, _N_DEV):
                g = g + p_ref[s, 0:r, :]
            _adamw_update(w_ref, m_ref, v_ref, g, *outs[4 * t:4 * t + 4])

    whole = lambda a: pl.BlockSpec(a.shape, lambda i, me, nd=a.ndim: (0,) * nd)
    in_specs, operands, out_specs, out_shape = [], [], [], []
    for w, m, v, p in zip(ws, ms, vs, parts):
        c = w.shape[1]
        mine = whole(p) if p.shape[2] == c else pl.BlockSpec((_N_DEV, p.shape[1], c), lambda i, me: (0, 0, me[0]))
        in_specs += [whole(w), whole(m), whole(v), mine]
        operands += [w, m, v, p]
        out_specs += [whole(w)] * 4
        out_shape += [jax.ShapeDtypeStruct(w.shape, f32)] * 4
    return pl.pallas_call(
        body, name="adamw_small", out_shape=out_shape,
        grid_spec=pltpu.PrefetchScalarGridSpec(num_scalar_prefetch=1, grid=(1,), in_specs=in_specs, out_specs=out_specs),
        compiler_params=_cparams("arbitrary"),
    )(me, *operands)


def _adamw(name, w, m, v, parts):
    R, C = w.shape
    n = parts.shape[0]
    tr = _tile(R, 256, 8)

    def body(w_ref, m_ref, v_ref, p_ref, g_ref, d_ref, nm_ref, nv_ref):
        g = p_ref[0].astype(f32)
        for s in range(1, n):
            g = g + p_ref[s].astype(f32)
        _adamw_update(w_ref, m_ref, v_ref, g, g_ref, d_ref, nm_ref, nv_ref)

    blk = pl.BlockSpec((tr, C), lambda i: (i, 0))
    sd = jax.ShapeDtypeStruct((R, C), f32)
    return pl.pallas_call(
        body, name=name, grid=(R // tr,),
        in_specs=[blk, blk, blk, pl.BlockSpec((n, tr, C), lambda i: (0, i, 0))],
        out_specs=[blk, blk, blk, blk], out_shape=[sd, sd, sd, sd],
        compiler_params=_cparams("parallel"),
    )(w, m, v, parts)


def _pad_rows8(a):
    return jnp.pad(a, ((0, 8 - a.shape[0]), (0, 0)))


def _local_step(x, target, p, get, emit, hint):
    T, D = x.shape

    def tie(a, *tokens):
        for tok in tokens:
            if tok is not None:
                a = a + tok[0, 0]
        return a

    n_parts = _N_DEV // _W_IN_PART
    w_in, base = get("w_in", x, 0)
    h1, proj = _in_proj_first(x, p["g_mix"], w_in, base, _W_IN_PART)
    hint("w_in", proj)
    for part in range(1, n_parts):
        w_in, base = get("w_in", proj, part)
        proj = _in_proj_more(f"in_proj_{part}", h1, w_in, proj, base, _W_IN_PART)
    w_pool = get("w_pool", proj)
    PW = w_pool.shape[0] * w_pool.shape[1]
    y_pool = _pool_fwd(proj, w_pool, p["pool_scale"])
    tok = hint("w_pool_proj", y_pool)
    lru_conv_w, w_a, w_i = get("lru_conv_w", proj), get("w_a", proj), get("w_i", proj)
    y_lru, hs = _lru_fwd(proj, D, PW, lru_conv_w, tie(p["lru_conv_b"], tok), w_a, p["b_a"], w_i, p["b_i"], p["lru_lambda"])
    tok = hint("w_up", y_lru)
    w_pp, w_lp = get("w_pool_proj", y_lru), get("w_lru_proj", y_lru)
    pp, qq, merged = _merge_fwd(y_pool, y_lru, w_pp, w_lp, proj, tie(p["b_gate"], tok))
    w_out = get("w_out", y_lru)
    x2, h2 = _out_proj(merged, w_out, x, p["g_mlp"])
    w_up = get("w_up", x2)
    up = _mm_nn("up_proj", h2, w_up, f32, tm_want=2 * _TM)
    tok = hint("w_down", up)
    ffn_conv_w = get("ffn_conv_w", y_lru)
    z = _ffn_fwd(up, ffn_conv_w, tie(p["ffn_conv_b"], tok))
    w_down = get("w_down", z)
    F = w_down.shape[0]
    dx3, dx3b, loss_t, dg_final = _down_loss(z, w_down, x2, target, p["g_final"])

    gs = {"g_final": dg_final}
    tok = emit("w_down", _mm_tn("dw_down", z, dx3b, bf16, tm_want=1536))
    dz = _mm_nt("dz", dx3b, w_down, bf16)
    dup, dcw_ffn, dcb_ffn = _ffn_bwd(dz, up, ffn_conv_w, tie(p["ffn_conv_b"], tok))
    gs["ffn_conv_w"] = dcw_ffn
    gs["ffn_conv_b"] = dcb_ffn

    tm = _tile(T, _TM)
    tk = _tile(F, _TN_MAX)
    nkh = F // tk
    tkt = _tile(T, _TK_T)
    tok = emit("w_up", _matmul(
        "dw_up", "tn", (h2, dup),
        [pl.BlockSpec((tkt, D), lambda i, j, k: (k, 0)), pl.BlockSpec((None, tkt, tk), lambda i, j, k: (j // nkh, k, j % nkh))],
        jax.ShapeDtypeStruct((D, 2 * F), bf16), pl.BlockSpec((D, tk), lambda i, j, k: (0, j)),
        (1, 2 * nkh, T // tkt), (D, tk)))

    tkc = _tile(F, _TK_UP)
    nkc = F // tkc
    dx2, dx2b, gs["g_mlp"] = _norm_bwd_matmul(
        "dh2", dup, pl.BlockSpec((None, tm, tkc), lambda i, k: (k // nkc, i, k % nkc)), 2 * nkc,
        w_up, pl.BlockSpec((D, tkc), lambda i, k: (0, k)), x2, tie(p["g_mlp"], tok), dx3)

    tok = emit("w_out", _mm_tn("dw_out", merged, dx2b, bf16))
    dP, dQ, dl0, dl1, db0, db1 = _merge_bwd(dx2b, w_out, pp, qq, proj, tie(p["b_gate"], tok), PW)
    gs["b_gate"] = jnp.concatenate([db0, db1], axis=1)
    tok = emit("w_pool_proj", _mm_tn("dw_pool_proj", y_pool, dP, bf16))
    tok2 = emit("w_lru_proj", _mm_tn("dw_lru_proj", y_lru, dQ, bf16))
    dy_pool = _mm_nt("dy_pool", dP, w_pp, f32)
    dy_lru = _mm_nt("dy_lru", dQ, w_lp, f32)

    du_lru, du_gelu, dwa, dwi, dcw_lru, dvec = _lru_bwd(
        dy_lru, proj, hs, D, lru_conv_w, tie(p["lru_conv_b"], tok, tok2), w_a, p["b_a"], w_i, p["b_i"], p["lru_lambda"])
    tok = emit("w_a", dwa.astype(bf16))
    tok2 = emit("w_i", dwi.astype(bf16))
    gs["lru_conv_w"] = dcw_lru
    gs["lru_conv_b"], gs["b_a"], gs["b_i"], gs["lru_lambda"] = dvec[0:1], dvec[1:2], dvec[2:3], dvec[3:4]
    du_pool, dwp, gs["pool_scale"] = _pool_bwd(dy_pool, proj, w_pool, tie(p["pool_scale"], tok, tok2))
    tok = emit("w_pool", dwp.astype(bf16))

    dproj = jnp.concatenate([du_pool, du_lru, du_gelu, dl0, dl1], axis=1)
    tok2 = emit("w_in", _mm_tn("dw_in", h1, dproj, bf16))
    NI = dproj.shape[1]
    tki = _tile(NI, _TK_IN, _LANE)
    grad_x, _, gs["g_mix"] = _norm_bwd_matmul(
        "dh1", dproj, pl.BlockSpec((tm, tki), lambda i, k: (i, k)), NI // tki,
        w_in, pl.BlockSpec((D, tki), lambda i, k: (0, k)), x, tie(p["g_mix"], tok, tok2), dx2)
    return loss_t[0, 0], grad_x, gs


_MATRICES = {"w_in": "col", "w_pool": "mid", "w_a": "mid", "w_i": "mid", "w_pool_proj": "col", "w_lru_proj": "row",
             "w_out": "row", "w_up": "col", "w_down": "row"}
_CONVS = ("lru_conv_w", "ffn_conv_w")
_GATHER_GROUPS = (("w_pool", "lru_conv_w", "w_a", "w_i"), ("w_pool_proj", "w_lru_proj", "w_out", "ffn_conv_w"),
                  ("w_up",), ("w_down",))
_GROUP_TWO_LEVEL = (False, True, True, True)
_SAME_CORE = (2, 4, 6)
_TWO_LEVEL = (1, 2, 4, 6)
_W_IN_PARTS = ((1,), (4,), (2,), (6,))
_W_IN_PART = 2
_VECTORS = ("g_mix", "b_gate", "pool_scale", "lru_conv_b", "b_a", "b_i", "lru_lambda", "g_mlp", "ffn_conv_b", "g_final")
_WEIGHTS = ("g_mix", "w_in", "b_gate", "w_pool", "pool_scale", "lru_conv_w", "lru_conv_b", "w_a", "b_a", "w_i", "b_i",
            "lru_lambda", "w_pool_proj", "w_lru_proj", "w_out", "g_mlp", "w_up", "ffn_conv_w", "ffn_conv_b", "w_down", "g_final")


def _full_shape(shape, kind):
    s = list(shape)
    s[{"col": 1, "row": 0, "mid": 1}[kind]] *= _N_DEV
    return tuple(s)


def _step(x, target, w, m, v):
    x, target = x[0], target[0]
    me = _blk(*_my_place())

    axis = {"col": 1, "row": 0, "mid": 1}
    kind = dict(_MATRICES, **{n: "col" for n in _CONVS})
    shard = {n: w[n].astype(bf16) for n in _MATRICES}
    shard.update({n: _pad_rows8(w[n]) for n in _CONVS})
    order = ["w_in"] + [n for grp in _GATHER_GROUPS for n in grp]
    index = {n: t for t, n in enumerate(order)}
    n_parts = len(_W_IN_PARTS)
    groups = [([0], ks) for ks in _W_IN_PARTS]
    groups += [([index[n] for n in grp], _TWO_LEVEL if two else _ALL_PEERS) for grp, two in zip(_GATHER_GROUPS, _GROUP_TWO_LEVEL)]
    g_routes = [_Route("gather", kind[n], shard[n].shape[axis[kind[n]]]) for n in order]
    lands = [lax.empty(_full_shape(shard[n].shape, kind[n]), shard[n].dtype) for n in order]
    g_sems, g_srcs, g_lands, _ = _send_start("gather_start", [shard[n] for n in order], lands, g_routes, groups)
    gathered, passing = {}, {}
    x_, y_, _c = _my_place()
    w_in_state = [[g_srcs[0]], [g_lands[0]]]

    def hint(name, after):
        if name == "w_in":
            srcs, got = w_in_state
            tok = None
            for part in range(1, n_parts):
                ks = _W_IN_PARTS[part]
                srcs, got = _send_wait(f"gather_wait_w_in_{part}", srcs, got, g_routes[:1], g_sems[part], after, ks)
                sems, got, tok = _forward_start(f"gather_pass_w_in_{part}", got, g_routes[:1], ks)
                passing[name, part] = sems
            w_in_state[:] = [srcs, got]
            return tok
        gi = next(i for i, grp in enumerate(_GATHER_GROUPS) if name in grp)
        if not _GROUP_TWO_LEVEL[gi] or gi in passing or _GATHER_GROUPS[gi][0] in gathered:
            return None
        ts = groups[n_parts + gi][0]
        routes = [g_routes[t] for t in ts]
        srcs, got = _send_wait(f"gather_wait_{gi}", [g_srcs[t] for t in ts], [g_lands[t] for t in ts], routes,
                               g_sems[n_parts + gi], after, _TWO_LEVEL)
        got = _place_own(f"gather_own_{gi}", srcs, got, routes)
        sems, got, tok = _forward_start(f"gather_pass_{gi}", got, routes, _SAME_CORE)
        passing[gi] = (sems, got, routes)
        return tok

    def get(name, after, part=None):
        if name == "w_in":
            ks = _W_IN_PARTS[part]
            if part == 0:
                srcs, got = _send_wait("gather_wait_w_in_0", *w_in_state, g_routes[:1], g_sems[0], after, ks)
                got = _place_own("gather_own_w_in", srcs, got, g_routes[:1])
                w_in_state[:] = [srcs, got]
            else:
                if (name, part) not in passing:
                    hint(name, after)
                got = _forward_wait(f"gather_got_w_in_{part}", w_in_state[1], g_routes[:1], passing[name, part], after, ks)
                w_in_state[1] = got
            k = ks[-1]
            base = 4 * (x_ ^ (k >> 2)) + 2 * (y_ ^ ((k >> 1) & 1))
            return got[0], base.astype(jnp.int32).reshape(1)
        if name not in gathered:
            gi = next(i for i, grp in enumerate(_GATHER_GROUPS) if name in grp)
            if _GROUP_TWO_LEVEL[gi]:
                hint(name, after)
                sems, got, routes = passing[gi]
                full = _forward_wait(f"gather_got_{gi}", got, routes, sems, after, _SAME_CORE)
            else:
                ts = groups[n_parts + gi][0]
                routes = [g_routes[t] for t in ts]
                srcs, got = _send_wait(f"gather_wait_{gi}", [g_srcs[t] for t in ts], [g_lands[t] for t in ts], routes,
                                       g_sems[n_parts + gi], after)
                full = _place_own(f"gather_own_{gi}", srcs, got, routes)
            gathered.update(zip(_GATHER_GROUPS[gi], full))
        return gathered[name]

    sent = {}

    def emit(name, grad):
        k = _MATRICES[name]
        size = w[name].shape[axis[k]]
        route = _Route("scatter", k, size)
        shp = list(grad.shape)
        shp[axis[k]] = size
        land = lax.empty((_N_DEV, *shp), grad.dtype)
        sems, srcs, lnds, token = _send_start("grad_start_" + name, [grad], [land], [route], [([0], _ALL_PEERS)])
        sent[name] = (srcs, lnds, [route], sems[0])
        return token

    p = {n: w[n].reshape(1, -1) for n in _VECTORS}
    loss_t, grad_x, gs = _local_step(x, target, p, get, emit, hint)
    loss = lax.psum(loss_t, ("x", "y", "c"))

    small_names = list(_VECTORS) + list(_CONVS)
    small_parts = _exchange([], [], [], [gs[n] for n in small_names])

    out = {}
    mats = list(_MATRICES)
    for n in mats:
        srcs, lnds, routes, sems = sent[n]
        srcs, got = _send_wait("grad_wait_" + n, srcs, lnds, routes, sems, grad_x)
        parts = _place_own("grad_own_" + n, srcs, got, routes)[0]
        shp = w[n].shape
        r2 = (-1, shp[-1])
        res = _adamw("adamw_" + n, w[n].reshape(r2), m[n].reshape(r2), v[n].reshape(r2),
                     parts.reshape((_N_DEV,) + w[n].reshape(r2).shape))
        out[n] = [a.reshape(shp) for a in res]
    two_d = lambda a: a.reshape(-1, a.shape[-1])
    res = _adamw_small(me.astype(jnp.int32).reshape(1), [two_d(w[n]) for n in small_names], [two_d(m[n]) for n in small_names],
                       [two_d(v[n]) for n in small_names], small_parts)
    for t, n in enumerate(small_names):
        out[n] = [a.reshape(w[n].shape) for a in res[4 * t:4 * t + 4]]
    return loss, grad_x[None], out


def kernel(x, g_mix, w_in, b_gate, w_pool, pool_scale, lru_conv_w, lru_conv_b, w_a, b_a, w_i, b_i, lru_lambda, w_pool_proj, w_lru_proj, w_out, g_mlp, w_up, ffn_conv_w, ffn_conv_b, w_down, g_final, loss_target, m_g_mix, m_w_in, m_b_gate, m_w_pool, m_pool_scale, m_lru_conv_w, m_lru_conv_b, m_w_a, m_b_a, m_w_i, m_b_i, m_lru_lambda, m_w_pool_proj, m_w_lru_proj, m_w_out, m_g_mlp, m_w_up, m_ffn_conv_w, m_ffn_conv_b, m_w_down, m_g_final, v_g_mix, v_w_in, v_b_gate, v_w_pool, v_pool_scale, v_lru_conv_w, v_lru_conv_b, v_w_a, v_b_a, v_w_i, v_b_i, v_lru_lambda, v_w_pool_proj, v_w_lru_proj, v_w_out, v_g_mlp, v_w_up, v_ffn_conv_w, v_ffn_conv_b, v_w_down, v_g_final):
    given = dict(locals())
    orig = {n: given[n].shape for n in _WEIGHTS}

    def squeeze(a, n):
        return a if n == "g_final" else a[0]

    w = {n: squeeze(given[n], n) for n in _WEIGHTS}
    m = {n: squeeze(given["m_" + n], n) for n in _WEIGHTS}
    v = {n: squeeze(given["v_" + n], n) for n in _WEIGHTS}
    for d in (w, m, v):
        d["g_final"] = d["g_final"].reshape(1, -1)
    loss, grad_x, out = _step(x, loss_target, w, m, v)
    res = [loss, grad_x]
    for k in range(4):
        res += [out[n][k].reshape(orig[n]) for n in _WEIGHTS]
    return tuple(res)
```

```python
import functools

import jax
import jax.numpy as jnp
from jax import lax
from jax.experimental import pallas as pl
from jax.experimental.pallas import tpu as pltpu

f32 = jnp.float32
bf16 = jnp.bfloat16

_EPS = 1e-6
_LRU_C = 8.0
_POOL_WINDOWS = (2, 4, 8, 16)
_POOL_HALO = 16
_CONV_HALO = 8
_GELU_C0 = 0.7978845608028654
_GELU_C1 = 0.044715
_ADAM_LR, _ADAM_B1, _ADAM_B2, _ADAM_EPS, _ADAM_WD, _ADAM_STEP = 0.001, 0.9, 0.999, 1e-08, 0.01, 10
_N_DEV = 8
_LANE = 128
_VMEM_LIMIT = 60 * 1024 * 1024

_TM = 512
_TM_SMALL = 512
_TC = 256
_TK_T = 1024
_TN_VMEM_BUDGET = 44 * 1024 * 1024
_TN_MAX = 1536
_CW = 1024
_TK_DOWN = 1536
_TK_IN = 2304
_TK_UP = 2048


def _cparams(*sem):
    return pltpu.CompilerParams(dimension_semantics=tuple(sem), vmem_limit_bytes=_VMEM_LIMIT)


def _tile(n, want, mult=1):
    if n <= want:
        return n
    t = want - want % mult
    while n % t:
        t -= mult
    return t


def _gelu(x):
    t = jnp.tanh(x * (_GELU_C0 + (_GELU_C0 * _GELU_C1) * (x * x)))
    return x * (0.5 + 0.5 * t)


def _gelu_both(x):
    x2 = x * x
    t = jnp.tanh(x * (_GELU_C0 + (_GELU_C0 * _GELU_C1) * x2))
    h = 0.5 + 0.5 * t
    return x * h, h + (x * (1.0 - t * t)) * (0.5 * _GELU_C0 + (1.5 * _GELU_C0 * _GELU_C1) * x2)


def _fold8(x):
    out = x[0:8]
    for r in range(8, x.shape[0], 8):
        out = out + x[r:r + 8]
    return out


def _sigmoid(x):
    return jax.nn.sigmoid(x)


def _dot(a, b):
    return jnp.dot(a, b, preferred_element_type=f32)


def _dot_nt(a, b):
    return lax.dot_general(a, b, (((1,), (1,)), ((), ())), preferred_element_type=f32)


def _dot_tn(a, b):
    return lax.dot_general(a, b, (((0,), (0,)), ((), ())), preferred_element_type=f32)


def _rms_rows(x_ref, g_ref, h_ref, n_rows, n_cols):
    rr = _tile(n_rows, _SLAB_ROWS)
    cg = _tile(n_cols, _SLAB_COLS)
    for r0 in range(0, n_rows, rr):
        rows = pl.ds(r0, rr)
        x = x_ref[rows, :]
        r = lax.rsqrt(jnp.mean(x * x, axis=-1, keepdims=True) + _EPS)
        for c0 in range(0, n_cols, cg):
            cols = slice(c0, c0 + cg)
            h_ref[rows, cols] = (x_ref[rows, cols] * r * g_ref[:, cols]).astype(bf16)


def _matmul(name, mode, operands, in_specs, out_shape, out_spec, grid, acc_shape):
    dot = {"nn": _dot, "nt": _dot_nt, "tn": _dot_tn}[mode]
    nk = grid[2]

    def body_whole(a_ref, b_ref, o_ref):
        o_ref[...] = dot(a_ref[...], b_ref[...]).astype(o_ref.dtype)

    def body(a_ref, b_ref, o_ref, acc_ref):
        k = pl.program_id(2)

        @pl.when(k == 0)
        def _():
            acc_ref[...] = dot(a_ref[...], b_ref[...])

        @pl.when((k > 0) & (k < nk - 1))
        def _():
            acc_ref[...] += dot(a_ref[...], b_ref[...])

        @pl.when(k == nk - 1)
        def _():
            o_ref[...] = (acc_ref[...] + dot(a_ref[...], b_ref[...])).astype(o_ref.dtype)

    return pl.pallas_call(
        body_whole if nk == 1 else body, name=name, grid=grid, in_specs=in_specs, out_specs=out_spec, out_shape=out_shape,
        scratch_shapes=[] if nk == 1 else [pltpu.VMEM(acc_shape, f32)],
        compiler_params=_cparams("parallel", "parallel", "arbitrary"),
    )(*operands)


def _mm_nn(name, a, b, out_dtype, tm_want=None):
    M, K = a.shape
    N = b.shape[1]
    tm, tn = _tile(M, tm_want or _TM), _tile(N, _TN_MAX)
    return _matmul(
        name, "nn", (a, b),
        [pl.BlockSpec((tm, K), lambda i, j, k: (i, 0)), pl.BlockSpec((K, tn), lambda i, j, k: (0, j))],
        jax.ShapeDtypeStruct((M, N), out_dtype), pl.BlockSpec((tm, tn), lambda i, j, k: (i, j)),
        (M // tm, N // tn, 1), (tm, tn))


def _mm_nt(name, a, b, out_dtype):
    M, K = a.shape
    N = b.shape[0]
    tm, tn = _tile(M, 2 * _TM), _tile(N, _TN_MAX)
    return _matmul(
        name, "nt", (a, b),
        [pl.BlockSpec((tm, K), lambda i, j, k: (i, 0)), pl.BlockSpec((tn, K), lambda i, j, k: (j, 0))],
        jax.ShapeDtypeStruct((M, N), out_dtype), pl.BlockSpec((tm, tn), lambda i, j, k: (i, j)),
        (M // tm, N // tn, 1), (tm, tn))


def _mm_tn(name, a, b, out_dtype, tm_want=2048):
    T, M = a.shape
    N = b.shape[1]
    tm, tn, tk = _tile(M, tm_want), _tile(N, _TN_MAX), _tile(T, _TK_T)
    need = lambda tk: 4 * tk * (tm + tn) + 8 * tm * tn
    if T % (2 * tk) == 0 and need(2 * tk) <= _TN_VMEM_BUDGET:
        tk *= 2
    return _matmul(
        name, "tn", (a, b),
        [pl.BlockSpec((tk, tm), lambda i, j, k: (k, i)), pl.BlockSpec((tk, tn), lambda i, j, k: (k, j))],
        jax.ShapeDtypeStruct((M, N), out_dtype), pl.BlockSpec((tm, tn), lambda i, j, k: (i, j)),
        (M // tm, N // tn, T // tk), (tm, tn))


def _in_proj_first(x, g_mix, w_in, base, n_tiles):
    T, D = x.shape
    NI = w_in.shape[1]
    tm, tn = _tile(T, 2 * _TM), NI // _N_DEV

    def body(base_ref, x_ref, g_ref, w_ref, h_ref, o_ref):
        @pl.when(pl.program_id(1) == 0)
        def _():
            _rms_rows(x_ref, g_ref, h_ref, tm, D)

        o_ref[...] = _dot(h_ref[...], w_ref[...])

    return pl.pallas_call(
        body, name="in_proj_0",
        grid_spec=pltpu.PrefetchScalarGridSpec(
            num_scalar_prefetch=1, grid=(T // tm, n_tiles),
            in_specs=[pl.BlockSpec((tm, D), lambda i, j, b: (i, 0)), pl.BlockSpec((1, D), lambda i, j, b: (0, 0)),
                      pl.BlockSpec((D, tn), lambda i, j, b: (0, b[0] + j))],
            out_specs=[pl.BlockSpec((tm, D), lambda i, j, b: (i, 0)), pl.BlockSpec((tm, tn), lambda i, j, b: (i, b[0] + j))]),
        out_shape=[jax.ShapeDtypeStruct((T, D), bf16), jax.ShapeDtypeStruct((T, NI), f32)],
        compiler_params=_cparams("parallel", "arbitrary"),
    )(base, x, g_mix, w_in)


def _in_proj_more(name, h1, w_in, proj, base, n_tiles):
    T, D = h1.shape
    NI = w_in.shape[1]
    tm, tn = _tile(T, 2 * _TM), NI // _N_DEV

    def body(base_ref, h_ref, w_ref, proj_ref, o_ref):
        o_ref[...] = _dot(h_ref[...], w_ref[...])

    return pl.pallas_call(
        body, name=name,
        grid_spec=pltpu.PrefetchScalarGridSpec(
            num_scalar_prefetch=1, grid=(T // tm, n_tiles),
            in_specs=[pl.BlockSpec((tm, D), lambda i, j, b: (i, 0)), pl.BlockSpec((D, tn), lambda i, j, b: (0, b[0] + j)),
                      pl.BlockSpec(memory_space=pl.ANY)],
            out_specs=pl.BlockSpec((tm, tn), lambda i, j, b: (i, b[0] + j))),
        out_shape=jax.ShapeDtypeStruct((T, NI), f32),
        input_output_aliases={3: 0},
        compiler_params=_cparams("parallel", "arbitrary"),
    )(base, h1, w_in, proj)


def _window_means(ext_ref, row0, tc, gw):
    H = _POOL_HALO
    t_glob = row0 + lax.broadcasted_iota(jnp.int32, (tc, 1), 0)
    out = []
    for g, w in enumerate(_POOL_WINDOWS):
        s = ext_ref[:, g * gw:(g + 1) * gw]
        st = 1
        while st < w:
            s = s + pltpu.roll(s, st, 0)
            st *= 2
        cnt = jnp.minimum(t_glob + 1, w).astype(f32)
        out.append((s[H:, :] / cnt, ext_ref[pl.ds(H, tc), g * gw:(g + 1) * gw]))
    return out


def _pool_fwd(proj, w_pool, pool_scale):
    T = proj.shape[0]
    G, gw, _ = w_pool.shape
    PW = G * gw
    tc = _tile(T, _TC)
    H = _POOL_HALO

    def body(u_ref, w_ref, s_ref, y_ref, ext_ref):
        i = pl.program_id(0)

        @pl.when(i == 0)
        def _():
            ext_ref[pl.ds(0, H), :] = jnp.zeros((H, PW), f32)

        ext_ref[pl.ds(H, tc), :] = u_ref[...]
        for g, (m, u) in enumerate(_window_means(ext_ref, i * tc, tc, gw)):
            d = (m - u).astype(bf16)
            y = _dot(d, w_ref[g]) * s_ref[:, g * gw:(g + 1) * gw]
            y_ref[:, g * gw:(g + 1) * gw] = y.astype(bf16)
        ext_ref[pl.ds(0, H), :] = ext_ref[pl.ds(tc, H), :]

    return pl.pallas_call(
        body, name="pool_fwd", grid=(T // tc,),
        in_specs=[pl.BlockSpec((tc, PW), lambda i: (i, 0)), pl.BlockSpec((G, gw, gw), lambda i: (0, 0, 0)),
                  pl.BlockSpec((1, PW), lambda i: (0, 0))],
        out_specs=pl.BlockSpec((tc, PW), lambda i: (i, 0)),
        out_shape=jax.ShapeDtypeStruct((T, PW), bf16),
        scratch_shapes=[pltpu.VMEM((H + tc, PW), f32)],
        compiler_params=_cparams("arbitrary"),
    )(proj, w_pool, pool_scale)


def _softplus(z):
    return jnp.maximum(z, 0.0) + jnp.log1p(jnp.exp(-jnp.abs(z)))


def _causal_conv(ext_ref, cw_ref, cb_ref, n, K, cols=slice(None), r0=0):
    H = _CONV_HALO
    v = cb_ref[:, cols] + cw_ref[K - 1:K, cols] * ext_ref[pl.ds(H + r0, n), cols]
    for k in range(K - 1):
        v = v + cw_ref[k:k + 1, cols] * ext_ref[pl.ds(H + r0 - (K - 1 - k), n), cols]
    return v


_ROWS = 8
_SLAB_ROWS = 16
_SLAB_COLS = 512


def _slabs(n_rows, n_cols, reverse=False):
    cg = _tile(n_cols, _SLAB_COLS)
    rr = _tile(n_rows, _SLAB_ROWS)
    starts = range(0, n_rows, rr)
    for c0 in range(0, n_cols, cg):
        for r0 in (reversed(starts) if reverse else starts):
            yield slice(c0, c0 + cg), pl.ds(r0, rr)


def _scan_rows(a_ref, b_ref, out_ref, carry_ref, n_rows, n_cols, reverse=False):
    cg = _tile(n_cols, _SLAB_COLS)
    row = lax.broadcasted_iota(jnp.int32, (8, cg), 0)
    steps = [(8 - sh, row < 8 - sh) if reverse else (sh, row >= sh) for sh in (1, 2, 4)]
    tiles = list(range(0, n_rows, 8))
    for c0 in range(0, n_cols, cg):
        cols = slice(c0, c0 + cg)
        h_in = carry_ref[0:1, cols]
        for r0 in (reversed(tiles) if reverse else tiles):
            rows = pl.ds(r0, 8)
            a, b = a_ref[rows, cols], b_ref[rows, cols]
            for shift, inside in steps:
                b = jnp.where(inside, a * pltpu.roll(b, shift, 0) + b, b)
                a = jnp.where(inside, a * pltpu.roll(a, shift, 0), a)
            h = a * h_in + b
            out_ref[rows, cols] = h
            h_in = h[0:1, :] if reverse else h[7:8, :]
        carry_ref[0:1, cols] = h_in


def _slabs_with_sums(n_rows, n_cols, n_sums, visit, flush, reverse=False):
    cg = _tile(n_cols, _SLAB_COLS)
    rr = _tile(n_rows, _SLAB_ROWS)
    starts = list(range(0, n_rows, rr))
    for c0 in range(0, n_cols, cg):
        cols = slice(c0, c0 + cg)
        sums = [jnp.zeros((8, cg), f32) for _ in range(n_sums)]
        for r0 in (reversed(starts) if reverse else starts):
            sums = visit(cols, pl.ds(r0, rr), sums)
        flush(cols, [jnp.sum(s, axis=0, keepdims=True) for s in sums])


def _lru_fwd(proj, D, PW, conv_w, conv_b, w_a, b_a, w_i, b_i, lam):
    T = proj.shape[0]
    NB, bw, _ = w_a.shape
    K = 4
    tc = _tile(T, _TC)
    H = _CONV_HALO
    hb = D // 2
    assert PW == hb and conv_w.shape[0] == 8

    def body(u0_ref, u1_ref, g0_ref, g1_ref, cw_ref, cb_ref, wa_ref, ba_ref, wi_ref, bi_ref, lam_ref,
             y_ref, hs_ref, ext_ref, a_scr, b_scr, v_scr, hc_scr):
        c = pl.program_id(0)

        @pl.when(c == 0)
        def _():
            ext_ref[pl.ds(0, H), :] = jnp.zeros((H, D), f32)
            hc_scr[...] = jnp.zeros_like(hc_scr)

        ext_ref[pl.ds(H, tc), 0:hb] = u0_ref[...]
        ext_ref[pl.ds(H, tc), hb:D] = u1_ref[...]
        sp = _softplus(-lam_ref[...])
        for cols, rows in _slabs(tc, D):
            v_scr[rows, cols] = _causal_conv(ext_ref, cw_ref, cb_ref, rows.size, K, cols, rows.start)
        for b in range(NB):
            cols = slice(b * bw, (b + 1) * bw)
            vb = v_scr[:, cols].astype(bf16)
            a_scr[:, cols] = _dot(vb, wa_ref[b])
            b_scr[:, cols] = _dot(vb, wi_ref[b])
        for cols, rows in _slabs(tc, D):
            r = _sigmoid(a_scr[rows, cols] + ba_ref[:, cols])
            i = _sigmoid(b_scr[rows, cols] + bi_ref[:, cols])
            a = jnp.exp(-_LRU_C * r * sp[:, cols])
            a_scr[rows, cols] = a
            b_scr[rows, cols] = jnp.sqrt(1.0 - a * a) * (i * v_scr[rows, cols])

        _scan_rows(a_scr, b_scr, hs_ref, hc_scr, tc, D)
        for cols, rows in _slabs(tc, D):
            g_ref, gcols = (g0_ref, cols) if cols.start < hb else (g1_ref, slice(cols.start - hb, cols.stop - hb))
            y_ref[rows, cols] = (hs_ref[rows, cols] * _gelu(g_ref[rows, gcols])).astype(bf16)
        ext_ref[pl.ds(0, H), :] = ext_ref[pl.ds(tc, H), :]

    vec = pl.BlockSpec((1, D), lambda c: (0, 0))
    wspec = pl.BlockSpec((NB, bw, bw), lambda c: (0, 0, 0))
    return pl.pallas_call(
        body, name="lru_fwd", grid=(T // tc,),
        in_specs=[pl.BlockSpec((tc, hb), lambda c: (c, 1)), pl.BlockSpec((tc, hb), lambda c: (c, 2)),
                  pl.BlockSpec((tc, hb), lambda c: (c, 3)), pl.BlockSpec((tc, hb), lambda c: (c, 4)),
                  pl.BlockSpec((8, D), lambda c: (0, 0)), vec, wspec, vec, wspec, vec, vec],
        out_specs=[pl.BlockSpec((tc, D), lambda c: (c, 0)), pl.BlockSpec((tc, D), lambda c: (c, 0))],
        out_shape=[jax.ShapeDtypeStruct((T, D), bf16), jax.ShapeDtypeStruct((T, D), f32)],
        scratch_shapes=[pltpu.VMEM((H + tc, D), f32), pltpu.VMEM((tc, D), f32), pltpu.VMEM((tc, D), f32),
                        pltpu.VMEM((tc, D), f32), pltpu.VMEM((8, D), f32)],
        compiler_params=_cparams("arbitrary"),
    )(proj, proj, proj, proj, conv_w, conv_b, w_a, b_a, w_i, b_i, lam)


def _merge_fwd(y_pool, y_lru, w_pp, w_lp, proj, b_gate):
    T, PW = y_pool.shape
    D = y_lru.shape[1]
    tm, tn = _tile(T, _TM), _tile(D, PW)
    nj = D // tn
    off = (PW + 2 * D) // tn

    def body(yp_ref, yl_ref, wp_ref, wl_ref, l0_ref, l1_ref, b0_ref, b1_ref, p_ref, q_ref, m_ref, p_scr, q_scr):
        p_scr[...] = _dot(yp_ref[...], wp_ref[...])
        q_scr[...] = _dot(yl_ref[...], wl_ref[...])
        for cols, rows in _slabs(tm, tn):
            p, q = p_scr[rows, cols], q_scr[rows, cols]
            g0 = _sigmoid(l0_ref[rows, cols] + b0_ref[:, cols])
            g1 = _sigmoid(l1_ref[rows, cols] + b1_ref[:, cols])
            m_ref[rows, cols] = (g0 * p + g1 * q).astype(bf16)
            p_ref[rows, cols] = p.astype(bf16)
            q_ref[rows, cols] = q.astype(bf16)

    tile = pl.BlockSpec((tm, tn), lambda j, i: (i, j))
    return pl.pallas_call(
        body, name="merge_fwd", grid=(nj, T // tm),
        in_specs=[pl.BlockSpec((tm, PW), lambda j, i: (i, 0)), pl.BlockSpec((tm, D), lambda j, i: (i, 0)),
                  pl.BlockSpec((PW, tn), lambda j, i: (0, j)), pl.BlockSpec((D, tn), lambda j, i: (0, j)),
                  pl.BlockSpec((tm, tn), lambda j, i: (i, off + j)), pl.BlockSpec((tm, tn), lambda j, i: (i, off + nj + j)),
                  pl.BlockSpec((1, tn), lambda j, i: (0, j)), pl.BlockSpec((1, tn), lambda j, i: (0, nj + j))],
        out_specs=[tile, tile, tile],
        out_shape=[jax.ShapeDtypeStruct((T, D), bf16), jax.ShapeDtypeStruct((T, D), bf16), jax.ShapeDtypeStruct((T, D), bf16)],
        scratch_shapes=[pltpu.VMEM((tm, tn), f32), pltpu.VMEM((tm, tn), f32)],
        compiler_params=_cparams("parallel", "arbitrary"),
    )(y_pool, y_lru, w_pp, w_lp, proj, proj, b_gate, b_gate)


def _out_proj(merged, w_out, x, g_mlp):
    T, D = x.shape
    tm = _tile(T, _TM_SMALL)

    def body(m_ref, w_ref, x_ref, g_ref, x2_ref, h2_ref):
        x2_ref[...] = x_ref[...] + _dot(m_ref[...], w_ref[...])
        _rms_rows(x2_ref, g_ref, h2_ref, tm, D)

    row = pl.BlockSpec((tm, D), lambda i: (i, 0))
    return pl.pallas_call(
        body, name="out_proj", grid=(T // tm,),
        in_specs=[row, pl.BlockSpec((D, D), lambda i: (0, 0)), row, pl.BlockSpec((1, D), lambda i: (0, 0))],
        out_specs=[row, row],
        out_shape=[jax.ShapeDtypeStruct((T, D), f32), jax.ShapeDtypeStruct((T, D), bf16)],
        compiler_params=_cparams("parallel"),
    )(merged, w_out, x, g_mlp)


def _ffn_fwd(up, conv_w, conv_b):
    T, F2 = up.shape
    F = F2 // 2
    K = 3
    tc = _tile(T, 4 * _TC)
    cw = _tile(F, _CW)
    ns = F // cw
    H = _CONV_HALO

    def body(gp_ref, val_ref, cw_ref, cb_ref, z_ref, ext_ref):
        @pl.when(pl.program_id(1) == 0)
        def _():
            ext_ref[pl.ds(0, H), :] = jnp.zeros((H, cw), f32)

        ext_ref[pl.ds(H, tc), :] = gp_ref[...]
        for cols, rows in _slabs(tc, cw):
            c = _causal_conv(ext_ref, cw_ref, cb_ref, rows.size, K, cols, rows.start)
            z_ref[rows, cols] = (_gelu(c) * val_ref[rows, cols]).astype(bf16)
        ext_ref[pl.ds(0, H), :] = ext_ref[pl.ds(tc, H), :]

    return pl.pallas_call(
        body, name="ffn_fwd", grid=(ns, T // tc),
        in_specs=[pl.BlockSpec((tc, cw), lambda s, c: (c, s)), pl.BlockSpec((tc, cw), lambda s, c: (c, ns + s)),
                  pl.BlockSpec((8, cw), lambda s, c: (0, s)), pl.BlockSpec((1, cw), lambda s, c: (0, s))],
        out_specs=pl.BlockSpec((tc, cw), lambda s, c: (c, s)),
        out_shape=jax.ShapeDtypeStruct((T, F), bf16),
        scratch_shapes=[pltpu.VMEM((H + tc, cw), f32)],
        compiler_params=_cparams("parallel", "arbitrary"),
    )(up, up, conv_w, conv_b)


def _down_loss(z, w_down, x2, target, g_final):
    T, F = z.shape
    D = x2.shape[1]
    tm, tk = _tile(T, _TM), _tile(F, _TK_DOWN)
    nk = F // tk

    def body(z_ref, w_ref, x2_ref, t_ref, g_ref, dx_ref, dxb_ref, loss_ref, dg_ref, acc_ref):
        i, k = pl.program_id(0), pl.program_id(1)

        @pl.when(k == 0)
        def _():
            acc_ref[...] = x2_ref[...]

        @pl.when((i == 0) & (k == 0))
        def _():
            loss_ref[...] = jnp.zeros_like(loss_ref)
            dg_ref[...] = jnp.zeros_like(dg_ref)

        acc_ref[...] += _dot(z_ref[...], w_ref[...])

        @pl.when(k == nk - 1)
        def _():
            g = g_ref[...]
            sq = jnp.zeros((_ROWS, 1), f32)
            dgs = jnp.zeros((_ROWS, D), f32)
            for r0 in range(0, tm, _ROWS):
                rows = pl.ds(r0, _ROWS)
                x3 = acc_ref[rows, :]
                r = lax.rsqrt(jnp.mean(x3 * x3, axis=-1, keepdims=True) + _EPS)
                xr = x3 * r
                e = xr * g - t_ref[rows, :]
                sq = sq + jnp.sum(e * e, axis=-1, keepdims=True)
                dy = e * (1.0 / D)
                gy = dy * g
                dx = r * gy - x3 * ((r * r * r) * jnp.mean(x3 * gy, axis=-1, keepdims=True))
                dgs = dgs + dy * xr
                dx_ref[rows, :] = dx
                dxb_ref[rows, :] = dx.astype(bf16)
            loss_ref[...] += (0.5 / D) * jnp.sum(sq)
            dg_ref[...] += jnp.sum(dgs, axis=0, keepdims=True)

    row = pl.BlockSpec((tm, D), lambda i, k: (i, 0))
    vec = pl.BlockSpec((1, D), lambda i, k: (0, 0))
    return pl.pallas_call(
        body, name="down_loss", grid=(T // tm, nk),
        in_specs=[pl.BlockSpec((tm, tk), lambda i, k: (i, k)), pl.BlockSpec((tk, D), lambda i, k: (k, 0)), row, row, vec],
        out_specs=[row, row, pl.BlockSpec((8, _LANE), lambda i, k: (0, 0)), vec],
        out_shape=[jax.ShapeDtypeStruct((T, D), f32), jax.ShapeDtypeStruct((T, D), bf16),
                   jax.ShapeDtypeStruct((8, _LANE), f32), jax.ShapeDtypeStruct((1, D), f32)],
        scratch_shapes=[pltpu.VMEM((tm, D), f32)],
        compiler_params=_cparams("arbitrary", "arbitrary"),
    )(z, w_down, x2, target, g_final)


def _ffn_bwd(dx3b, w_down, up, conv_w, conv_b):
    T, D = dx3b.shape
    F = w_down.shape[0]
    K = 3
    tc = _tile(T, 4 * _TC)
    cw = _tile(F, _CW)
    ns, nt = F // cw, T // tc
    H = _CONV_HALO

    def body(dx_ref, wd_ref, gp_ref, val_ref, gph_ref, cw_ref, cb_ref, dup_ref, dcw_ref, dcb_ref, ext_ref, dext_ref, dz_ref):
        j = pl.program_id(1)
        first = j == nt - 1

        @pl.when(j == 0)
        def _():
            dext_ref[pl.ds(tc, H), :] = jnp.zeros((H, cw), f32)
            dcw_ref[...] = jnp.zeros_like(dcw_ref)
            dcb_ref[...] = jnp.zeros_like(dcb_ref)

        dz_ref[...] = _dot_nt(dx_ref[...], wd_ref[...])
        ext_ref[pl.ds(0, H), :] = jnp.where(first, 0.0, gph_ref[...])
        ext_ref[pl.ds(H, tc), :] = gp_ref[...]
        def visit(cols, rows, sums):
            r0, n = rows.start, rows.size
            gp = ext_ref[pl.ds(H + r0, n), cols]
            c = _causal_conv(ext_ref, cw_ref, cb_ref, n, K, cols, r0)
            ge, gg = _gelu_both(c)
            dzv = dz_ref[rows, cols]
            dup_ref[1, rows, cols] = (dzv * ge).astype(bf16)
            dc = dzv * val_ref[rows, cols] * gg
            dext_ref[rows, cols] = dc
            dgp = cw_ref[K - 1:K, cols] * dc
            new = [None] * K + [sums[K] + _fold8(dc)]
            new[K - 1] = sums[K - 1] + _fold8(gp * dc)
            for k in range(K - 1):
                sh = dext_ref[pl.ds(r0 + K - 1 - k, n), cols]
                dgp = dgp + cw_ref[k:k + 1, cols] * sh
                new[k] = sums[k] + _fold8(gp * sh)
            dup_ref[0, rows, cols] = dgp.astype(bf16)
            return new

        def flush(cols, totals):
            for k in range(K):
                dcw_ref[k:k + 1, cols] += totals[k]
            dcb_ref[:, cols] += totals[K]

        _slabs_with_sums(tc, cw, K + 1, visit, flush, reverse=True)
        dext_ref[pl.ds(tc, H), :] = dext_ref[pl.ds(0, H), :]

    hblk = tc // H
    return pl.pallas_call(
        body, name="ffn_bwd", grid=(ns, nt),
        in_specs=[pl.BlockSpec((tc, D), lambda s, j: (nt - 1 - j, 0)),
                  pl.BlockSpec((cw, D), lambda s, j: (s, 0)),
                  pl.BlockSpec((tc, cw), lambda s, j: (nt - 1 - j, s)),
                  pl.BlockSpec((tc, cw), lambda s, j: (nt - 1 - j, ns + s)),
                  pl.BlockSpec((H, cw), lambda s, j: (jnp.maximum((nt - 1 - j) * hblk - 1, 0), s)),
                  pl.BlockSpec((8, cw), lambda s, j: (0, s)), pl.BlockSpec((1, cw), lambda s, j: (0, s))],
        out_specs=[pl.BlockSpec((2, tc, cw), lambda s, j: (0, nt - 1 - j, s)),
                   pl.BlockSpec((8, cw), lambda s, j: (0, s)), pl.BlockSpec((1, cw), lambda s, j: (0, s))],
        out_shape=[jax.ShapeDtypeStruct((2, T, F), bf16), jax.ShapeDtypeStruct((8, F), f32), jax.ShapeDtypeStruct((1, F), f32)],
        scratch_shapes=[pltpu.VMEM((H + tc, cw), f32), pltpu.VMEM((tc + H, cw), f32), pltpu.VMEM((tc, cw), f32)],
        compiler_params=_cparams("parallel", "arbitrary"),
    )(dx3b, w_down, up, up, up, conv_w, conv_b)


def _norm_bwd_matmul(name, a, a_spec, nk, w, w_spec, x, g, dres):
    T, D = x.shape
    tm = a_spec.block_shape[-2]

    def body(a_ref, w_ref, x_ref, g_ref, dr_ref, dx_ref, dxb_ref, dg_ref, acc_ref):
        i, k = pl.program_id(0), pl.program_id(1)

        @pl.when((i == 0) & (k == 0))
        def _():
            dg_ref[...] = jnp.zeros_like(dg_ref)

        @pl.when(k == 0)
        def _():
            acc_ref[...] = _dot_nt(a_ref[...], w_ref[...])

        @pl.when(k > 0)
        def _():
            acc_ref[...] += _dot_nt(a_ref[...], w_ref[...])

        @pl.when(k == nk - 1)
        def _():
            g = g_ref[...]
            dgs = jnp.zeros((_ROWS, D), f32)
            for r0 in range(0, tm, _ROWS):
                rows = pl.ds(r0, _ROWS)
                x = x_ref[rows, :]
                dh = acc_ref[rows, :]
                r = lax.rsqrt(jnp.mean(x * x, axis=-1, keepdims=True) + _EPS)
                gy = dh * g
                dx = dr_ref[rows, :] + (r * gy - x * ((r * r * r) * jnp.mean(x * gy, axis=-1, keepdims=True)))
                dgs = dgs + dh * (x * r)
                dx_ref[rows, :] = dx
                dxb_ref[rows, :] = dx.astype(bf16)
            dg_ref[...] += jnp.sum(dgs, axis=0, keepdims=True)

    row = pl.BlockSpec((tm, D), lambda i, k: (i, 0))
    vec = pl.BlockSpec((1, D), lambda i, k: (0, 0))
    return pl.pallas_call(
        body, name=name, grid=(T // tm, nk),
        in_specs=[a_spec, w_spec, row, vec, row],
        out_specs=[row, row, vec],
        out_shape=[jax.ShapeDtypeStruct((T, D), f32), jax.ShapeDtypeStruct((T, D), bf16), jax.ShapeDtypeStruct((1, D), f32)],
        scratch_shapes=[pltpu.VMEM((tm, D), f32)],
        compiler_params=_cparams("arbitrary", "arbitrary"),
    )(a, w, x, g, dres)


def _merge_bwd(dx2b, w_out, p, q, proj, b_gate, PW):
    T, D = p.shape
    tm, tn = _tile(T, _TM), _tile(D, PW)
    nj = D // tn
    off = (PW + 2 * D) // tn

    def body(dx_ref, w_ref, p_ref, q_ref, l0_ref, l1_ref, b0_ref, b1_ref, dp_ref, dq_ref, dl0_ref, dl1_ref, db0_ref, db1_ref,
             dm_ref):
        @pl.when(pl.program_id(1) == 0)
        def _():
            db0_ref[...] = jnp.zeros_like(db0_ref)
            db1_ref[...] = jnp.zeros_like(db1_ref)

        dm_ref[...] = _dot_nt(dx_ref[...], w_ref[...])

        def visit(cols, rows, sums):
            dm = dm_ref[rows, cols]
            g0 = _sigmoid(l0_ref[rows, cols] + b0_ref[:, cols])
            g1 = _sigmoid(l1_ref[rows, cols] + b1_ref[:, cols])
            dp_ref[rows, cols] = (g0 * dm).astype(bf16)
            dq_ref[rows, cols] = (g1 * dm).astype(bf16)
            dl0 = dm * p_ref[rows, cols].astype(f32) * (g0 * (1.0 - g0))
            dl1 = dm * q_ref[rows, cols].astype(f32) * (g1 * (1.0 - g1))
            dl0_ref[rows, cols] = dl0.astype(bf16)
            dl1_ref[rows, cols] = dl1.astype(bf16)
            return [sums[0] + _fold8(dl0), sums[1] + _fold8(dl1)]

        def flush(cols, totals):
            db0_ref[:, cols] += totals[0]
            db1_ref[:, cols] += totals[1]

        _slabs_with_sums(tm, tn, 2, visit, flush)

    tile = pl.BlockSpec((tm, tn), lambda j, i: (i, j))
    vecj = pl.BlockSpec((1, tn), lambda j, i: (0, j))
    tb = jax.ShapeDtypeStruct((T, D), bf16)
    vb = jax.ShapeDtypeStruct((1, D), f32)
    return pl.pallas_call(
        body, name="merge_bwd", grid=(nj, T // tm),
        in_specs=[pl.BlockSpec((tm, D), lambda j, i: (i, 0)), pl.BlockSpec((tn, D), lambda j, i: (j, 0)), tile, tile,
                  pl.BlockSpec((tm, tn), lambda j, i: (i, off + j)), pl.BlockSpec((tm, tn), lambda j, i: (i, off + nj + j)),
                  vecj, pl.BlockSpec((1, tn), lambda j, i: (0, nj + j))],
        out_specs=[tile, tile, tile, tile, vecj, vecj],
        out_shape=[tb, tb, tb, tb, vb, vb],
        scratch_shapes=[pltpu.VMEM((tm, tn), f32)],
        compiler_params=_cparams("parallel", "arbitrary"),
    )(dx2b, w_out, p, q, proj, proj, b_gate, b_gate)


def _lru_bwd(dy, proj, hs, D, conv_w, conv_b, w_a, b_a, w_i, b_i, lam):
    T = dy.shape[0]
    NB, bw, _ = w_a.shape
    K = 4
    tc = _tile(T, _TC)
    nt = T // tc
    H = _CONV_HALO
    hb = D // 2

    def body(dy_ref, u0_ref, u1_ref, g0_ref, g1_ref, hs_ref, uh0_ref, uh1_ref, hh_ref,
             cw_ref, cb_ref, wa_ref, ba_ref, wi_ref, bi_ref, lam_ref,
             du_ref, dg_ref, dwa_ref, dwi_ref, dcw_ref, dvec_ref,
             ext_ref, hext_ref, dext_ref, q_ref, a_scr, r_scr, i_scr, v_scr, g_scr, car_scr):
        j = pl.program_id(0)
        first = j == nt - 1

        @pl.when(j == 0)
        def _():
            dext_ref[pl.ds(tc, H), :] = jnp.zeros((H, D), f32)
            car_scr[...] = jnp.zeros_like(car_scr)
            dwa_ref[...] = jnp.zeros_like(dwa_ref)
            dwi_ref[...] = jnp.zeros_like(dwi_ref)
            dcw_ref[...] = jnp.zeros_like(dcw_ref)
            dvec_ref[...] = jnp.zeros_like(dvec_ref)

        ext_ref[pl.ds(0, H), 0:hb] = jnp.where(first, 0.0, uh0_ref[...])
        ext_ref[pl.ds(0, H), hb:D] = jnp.where(first, 0.0, uh1_ref[...])
        ext_ref[pl.ds(H, tc), 0:hb] = u0_ref[...]
        ext_ref[pl.ds(H, tc), hb:D] = u1_ref[...]
        hext_ref[pl.ds(0, H), :] = jnp.where(first, 0.0, hh_ref[...])
        hext_ref[pl.ds(H, tc), :] = hs_ref[...]
        lamv = lam_ref[...]
        sp = _softplus(-lamv)

        for cols, rows in _slabs(tc, D):
            v_scr[rows, cols] = _causal_conv(ext_ref, cw_ref, cb_ref, rows.size, K, cols, rows.start)
        for b in range(NB):
            cols = slice(b * bw, (b + 1) * bw)
            vb = v_scr[:, cols].astype(bf16)
            r_scr[:, cols] = _dot(vb, wa_ref[b])
            i_scr[:, cols] = _dot(vb, wi_ref[b])
        for cols, rows in _slabs(tc, D):
            r = _sigmoid(r_scr[rows, cols] + ba_ref[:, cols])
            r_scr[rows, cols] = r
            i_scr[rows, cols] = _sigmoid(i_scr[rows, cols] + bi_ref[:, cols])
            a = jnp.exp(-_LRU_C * r * sp[:, cols])
            a_scr[rows, cols] = a
            g_ref, gcols = (g0_ref, cols) if cols.start < hb else (g1_ref, slice(cols.start - hb, cols.stop - hb))
            ge, gg = _gelu_both(g_ref[rows, gcols])
            dyv = dy_ref[rows, cols]
            dho = dyv * ge
            g_scr[rows, cols] = dho
            q_ref[rows, cols] = a * dho
            dg_ref[rows, cols] = (dyv * hs_ref[rows, cols] * gg).astype(bf16)

        q_ref[pl.ds(tc, 1), :] = car_scr[0:1, :]
        _scan_rows(a_scr, q_ref, q_ref, car_scr, tc, D, reverse=True)

        def gates(cols, rows, sums):
            g = g_scr[rows, cols] + q_ref[pl.ds(rows.start + 1, rows.size), cols]
            v, r, i, a = v_scr[rows, cols], r_scr[rows, cols], i_scr[rows, cols], a_scr[rows, cols]
            mult = jnp.sqrt(1.0 - a * a)
            h_prev = hext_ref[pl.ds(H - 1 + rows.start, rows.size), cols]
            gm = g * mult
            dext_ref[rows, cols] = gm * i
            dlog_a = (g * h_prev - g * (i * v) * (a / mult)) * a
            dpr = dlog_a * (-_LRU_C * sp[:, cols]) * (r * (1.0 - r))
            dpi = gm * v * (i * (1.0 - i))
            r_scr[rows, cols] = dpr
            i_scr[rows, cols] = dpi
            return [sums[0] + _fold8(dpr), sums[1] + _fold8(dpi), sums[2] + _fold8(dlog_a * (-_LRU_C * r))]

        def gates_flush(cols, totals):
            dvec_ref[1:2, cols] += totals[0]
            dvec_ref[2:3, cols] += totals[1]
            dvec_ref[3:4, cols] += totals[2] * (-_sigmoid(-lamv[:, cols]))

        _slabs_with_sums(tc, D, 3, gates, gates_flush)
        for b in range(NB):
            cols = slice(b * bw, (b + 1) * bw)
            dprb, dpib, vb = r_scr[:, cols].astype(bf16), i_scr[:, cols].astype(bf16), v_scr[:, cols].astype(bf16)
            dext_ref[pl.ds(0, tc), cols] += _dot_nt(dprb, wa_ref[b]) + _dot_nt(dpib, wi_ref[b])
            dwa_ref[b] += _dot_tn(vb, dprb)
            dwi_ref[b] += _dot_tn(vb, dpib)

        def conv_t(cols, rows, sums):
            r0, n = rows.start, rows.size
            u = ext_ref[pl.ds(H + r0, n), cols]
            dv = dext_ref[rows, cols]
            du = cw_ref[K - 1:K, cols] * dv
            new = [None] * K + [sums[K] + _fold8(dv)]
            new[K - 1] = sums[K - 1] + _fold8(u * dv)
            for k in range(K - 1):
                sh = dext_ref[pl.ds(r0 + K - 1 - k, n), cols]
                du = du + cw_ref[k:k + 1, cols] * sh
                new[k] = sums[k] + _fold8(u * sh)
            du_ref[rows, cols] = du.astype(bf16)
            return new

        def conv_t_flush(cols, totals):
            for k in range(K):
                dcw_ref[k:k + 1, cols] += totals[k]
            dvec_ref[0:1, cols] += totals[K]

        _slabs_with_sums(tc, D, K + 1, conv_t, conv_t_flush)
        dext_ref[pl.ds(tc, H), :] = dext_ref[pl.ds(0, H), :]

    hblk = tc // H
    rev = lambda j: nt - 1 - j
    halo = lambda j: jnp.maximum((nt - 1 - j) * hblk - 1, 0)
    vec = pl.BlockSpec((1, D), lambda j: (0, 0))
    wspec = pl.BlockSpec((NB, bw, bw), lambda j: (0, 0, 0))
    acc8 = pl.BlockSpec((8, D), lambda j: (0, 0))
    big = pltpu.VMEM((tc, D), f32)
    return pl.pallas_call(
        body, name="lru_bwd", grid=(nt,),
        in_specs=[pl.BlockSpec((tc, D), lambda j: (rev(j), 0)),
                  pl.BlockSpec((tc, hb), lambda j: (rev(j), 1)), pl.BlockSpec((tc, hb), lambda j: (rev(j), 2)),
                  pl.BlockSpec((tc, hb), lambda j: (rev(j), 3)), pl.BlockSpec((tc, hb), lambda j: (rev(j), 4)),
                  pl.BlockSpec((tc, D), lambda j: (rev(j), 0)),
                  pl.BlockSpec((H, hb), lambda j: (halo(j), 1)), pl.BlockSpec((H, hb), lambda j: (halo(j), 2)),
                  pl.BlockSpec((H, D), lambda j: (halo(j), 0)),
                  acc8, vec, wspec, vec, wspec, vec, vec],
        out_specs=[pl.BlockSpec((tc, D), lambda j: (rev(j), 0)), pl.BlockSpec((tc, D), lambda j: (rev(j), 0)),
                   wspec, wspec, acc8, acc8],
        out_shape=[jax.ShapeDtypeStruct((T, D), bf16), jax.ShapeDtypeStruct((T, D), bf16),
                   jax.ShapeDtypeStruct((NB, bw, bw), f32), jax.ShapeDtypeStruct((NB, bw, bw), f32),
                   jax.ShapeDtypeStruct((8, D), f32), jax.ShapeDtypeStruct((8, D), f32)],
        scratch_shapes=[pltpu.VMEM((H + tc, D), f32), pltpu.VMEM((H + tc, D), f32), pltpu.VMEM((tc + H, D), f32),
                        pltpu.VMEM((tc + H, D), f32), big, big, big, big, big, pltpu.VMEM((8, D), f32)],
        compiler_params=_cparams("arbitrary"),
    )(dy, proj, proj, proj, proj, hs, proj, proj, hs, conv_w, conv_b, w_a, b_a, w_i, b_i, lam)


def _pool_bwd(dy, proj, w_pool, pool_scale):
    T, PW = dy.shape
    G, gw, _ = w_pool.shape
    tc = _tile(T, _TC)
    nt = T // tc
    H = _POOL_HALO

    def body(dy_ref, u_ref, uh_ref, w_ref, s_ref, du_ref, dw_ref, ds_ref, ext_ref, eext_ref):
        j = pl.program_id(0)
        first = j == nt - 1
        row0 = (nt - 1 - j) * tc

        @pl.when(j == 0)
        def _():
            eext_ref[pl.ds(tc, H), :] = jnp.zeros((H, PW), f32)
            dw_ref[...] = jnp.zeros_like(dw_ref)
            ds_ref[...] = jnp.zeros_like(ds_ref)

        ext_ref[pl.ds(0, H), :] = jnp.where(first, 0.0, uh_ref[...])
        ext_ref[pl.ds(H, tc), :] = u_ref[...]
        t_glob = row0 + lax.broadcasted_iota(jnp.int32, (tc, 1), 0)
        dds = []
        for g, (m, u) in enumerate(_window_means(ext_ref, row0, tc, gw)):
            cols = slice(g * gw, (g + 1) * gw)
            d = (m - u).astype(bf16)
            yraw = _dot(d, w_ref[g])
            dyv = dy_ref[:, cols]
            ds_ref[:, cols] += jnp.sum(dyv * yraw, axis=0, keepdims=True)
            dyr = (dyv * s_ref[:, cols]).astype(bf16)
            dd = _dot_nt(dyr, w_ref[g])
            dw_ref[g] += _dot_tn(d, dyr)
            cnt = jnp.minimum(t_glob + 1, _POOL_WINDOWS[g]).astype(f32)
            eext_ref[pl.ds(0, tc), cols] = dd / cnt
            dds.append(dd)
        n = tc + H
        for g, w in enumerate(_POOL_WINDOWS):
            cols = slice(g * gw, (g + 1) * gw)
            s = eext_ref[:, cols]
            st = 1
            while st < w:
                s = s + pltpu.roll(s, n - st, 0)
                st *= 2
            du_ref[:, cols] = (s[0:tc, :] - dds[g]).astype(bf16)
        eext_ref[pl.ds(tc, H), :] = eext_ref[pl.ds(0, H), :]

    hblk = tc // H
    return pl.pallas_call(
        body, name="pool_bwd", grid=(nt,),
        in_specs=[pl.BlockSpec((tc, PW), lambda j: (nt - 1 - j, 0)), pl.BlockSpec((tc, PW), lambda j: (nt - 1 - j, 0)),
                  pl.BlockSpec((H, PW), lambda j: (jnp.maximum((nt - 1 - j) * hblk - 1, 0), 0)),
                  pl.BlockSpec((G, gw, gw), lambda j: (0, 0, 0)), pl.BlockSpec((1, PW), lambda j: (0, 0))],
        out_specs=[pl.BlockSpec((tc, PW), lambda j: (nt - 1 - j, 0)), pl.BlockSpec((G, gw, gw), lambda j: (0, 0, 0)),
                   pl.BlockSpec((1, PW), lambda j: (0, 0))],
        out_shape=[jax.ShapeDtypeStruct((T, PW), bf16), jax.ShapeDtypeStruct((G, gw, gw), f32), jax.ShapeDtypeStruct((1, PW), f32)],
        scratch_shapes=[pltpu.VMEM((H + tc, PW), f32), pltpu.VMEM((tc + H, PW), f32)],
        compiler_params=_cparams("arbitrary"),
    )(dy, proj, proj, w_pool, pool_scale)


_MESH = pl.DeviceIdType.MESH
_HBM = pl.BlockSpec(memory_space=pltpu.HBM)


def _slab(ref, kind, blk, n):
    start = blk * n
    if n % _LANE == 0:
        start = pl.multiple_of(start, _LANE)
    if kind == "col":
        return ref.at[:, pl.ds(start, n)]
    if kind == "row":
        return ref.at[pl.ds(start, n), :]
    if kind == "mid":
        return ref.at[:, pl.ds(start, n), :]
    raise ValueError(kind)


def _my_place():
    x, y, c = lax.axis_index("x"), lax.axis_index("y"), lax.axis_index("c")
    return x, y, c


def _blk(px, py, pc):
    return 4 * px + 2 * py + pc


_SEM = pl.BlockSpec(memory_space=pltpu.SEMAPHORE)
_EFFECT = pltpu.SideEffectType.DATAFLOW_SIDE_EFFECTING


_ALL_PEERS = (1, 2, 3, 4, 5, 6, 7)


def _peers(x, y, c):
    return [(k, (x ^ (k >> 2), y ^ ((k >> 1) & 1), c ^ (k & 1))) for k in range(1, 8)]


class _Route:
    def __init__(self, mode, kind, size):
        self.mode, self.kind, self.size = mode, kind, size

    def src(self, ref, peer_blk):
        return ref if self.mode == "gather" else _slab(ref, self.kind, peer_blk, self.size)

    def dst(self, ref, origin_blk):
        return _slab(ref, self.kind, origin_blk, self.size) if self.mode == "gather" else ref.at[origin_blk]


def _send_start(name, srcs, lands, routes, groups):
    nt, ng = len(srcs), len(groups)

    def body(*refs):
        src_refs, land_refs = refs[:nt], refs[nt:2 * nt]
        sems = refs[2 * nt:2 * nt + 2 * ng]
        token = refs[-1]
        x, y, c = _my_place()
        me = _blk(x, y, c)
        for gi, (grp, ks) in enumerate(groups):
            for pos, t in enumerate(grp):
                for k, peer in _peers(x, y, c):
                    if k in ks:
                        s = len(ks) * pos + ks.index(k)
                        pltpu.make_async_remote_copy(
                            src_ref=routes[t].src(src_refs[t], _blk(*peer)), dst_ref=routes[t].dst(land_refs[t], me),
                            send_sem=sems[2 * gi].at[s], recv_sem=sems[2 * gi + 1].at[s],
                            device_id=peer, device_id_type=_MESH).start()
        token[...] = jnp.zeros_like(token)

    hbm = lambda a: pltpu.HBM(a.shape, a.dtype)
    out_shape = []
    for grp, ks in groups:
        out_shape += [pltpu.SemaphoreType.DMA((len(ks) * len(grp),)), pltpu.SemaphoreType.DMA((len(ks) * len(grp),))]
    out_shape += [hbm(a) for a in srcs] + [hbm(a) for a in lands] + [jax.ShapeDtypeStruct((8, _LANE), f32)]
    res = pl.pallas_call(
        body, name=name, out_shape=out_shape,
        in_specs=[_HBM] * (2 * nt),
        out_specs=[_SEM] * (2 * ng) + [_HBM] * (2 * nt) + [pl.BlockSpec(memory_space=pltpu.VMEM)],
        input_output_aliases={t: 2 * ng + t for t in range(2 * nt)},
        compiler_params=pltpu.CompilerParams(has_side_effects=_EFFECT),
    )(*[pltpu.with_memory_space_constraint(a, pltpu.HBM) for a in list(srcs) + list(lands)])
    sems = [(res[2 * g], res[2 * g + 1]) for g in range(ng)]
    return sems, res[2 * ng:2 * ng + nt], res[2 * ng + nt:2 * ng + 2 * nt], res[-1]


def _send_wait(name, srcs, lands, routes, sems, after, ks=_ALL_PEERS):
    n = len(srcs)

    def body(*refs):
        src_refs, land_refs = refs[:n], refs[n:2 * n]
        send_sems, recv_sems = refs[2 * n], refs[2 * n + 1]
        x, y, c = _my_place()
        for pos in range(n):
            for k, peer in _peers(x, y, c):
                if k in ks:
                    pb = _blk(*peer)
                    s = len(ks) * pos + ks.index(k)
                    cp = pltpu.make_async_remote_copy(
                        src_ref=routes[pos].src(src_refs[pos], pb), dst_ref=routes[pos].dst(land_refs[pos], pb),
                        send_sem=send_sems.at[s], recv_sem=recv_sems.at[s], device_id=peer, device_id_type=_MESH)
                    cp.wait_send()
                    cp.wait_recv()

    hbm = lambda a: pltpu.HBM(a.shape, a.dtype)
    res = pl.pallas_call(
        body, name=name, out_shape=[hbm(a) for a in srcs] + [hbm(a) for a in lands],
        in_specs=[_HBM] * (2 * n) + [_SEM, _SEM, pl.BlockSpec(memory_space=pl.ANY)],
        out_specs=[_HBM] * (2 * n),
        input_output_aliases={t: t for t in range(2 * n)},
        compiler_params=pltpu.CompilerParams(has_side_effects=_EFFECT),
    )(*srcs, *lands, sems[0], sems[1], after)
    return res[:n], res[n:]


def _forward_start(name, lands, routes, ks):
    n = len(lands)

    def body(*refs):
        land_refs, send_sems, recv_sems, token = refs[:n], refs[n], refs[n + 1], refs[-1]
        x, y, c = _my_place()
        for pos in range(n):
            for i, k in enumerate(ks):
                part = routes[pos].dst(land_refs[pos], _blk(x ^ (k >> 2), y ^ ((k >> 1) & 1), c))
                pltpu.make_async_remote_copy(
                    src_ref=part, dst_ref=part, send_sem=send_sems.at[len(ks) * pos + i],
                    recv_sem=recv_sems.at[len(ks) * pos + i], device_id=(x, y, 1 - c), device_id_type=_MESH).start()
        token[...] = jnp.zeros_like(token)

    hbm = lambda a: pltpu.HBM(a.shape, a.dtype)
    sem = pltpu.SemaphoreType.DMA((len(ks) * n,))
    res = pl.pallas_call(
        body, name=name, out_shape=[sem, sem] + [hbm(a) for a in lands] + [jax.ShapeDtypeStruct((8, _LANE), f32)],
        in_specs=[_HBM] * n, out_specs=[_SEM, _SEM] + [_HBM] * n + [pl.BlockSpec(memory_space=pltpu.VMEM)],
        input_output_aliases={t: 2 + t for t in range(n)},
        compiler_params=pltpu.CompilerParams(has_side_effects=_EFFECT),
    )(*[pltpu.with_memory_space_constraint(a, pltpu.HBM) for a in lands])
    return (res[0], res[1]), res[2:2 + n], res[-1]


def _forward_wait(name, lands, routes, sems, after, ks):
    n = len(lands)

    def body(*refs):
        land_refs, send_sems, recv_sems = refs[:n], refs[n], refs[n + 1]
        x, y, c = _my_place()
        for pos in range(n):
            for i, k in enumerate(ks):
                px, py = x ^ (k >> 2), y ^ ((k >> 1) & 1)
                cp = pltpu.make_async_remote_copy(
                    src_ref=routes[pos].dst(land_refs[pos], _blk(px, py, c)),
                    dst_ref=routes[pos].dst(land_refs[pos], _blk(px, py, 1 - c)),
                    send_sem=send_sems.at[len(ks) * pos + i], recv_sem=recv_sems.at[len(ks) * pos + i],
                    device_id=(x, y, 1 - c), device_id_type=_MESH)
                cp.wait_send()
                cp.wait_recv()

    hbm = lambda a: pltpu.HBM(a.shape, a.dtype)
    return pl.pallas_call(
        body, name=name, out_shape=[hbm(a) for a in lands],
        in_specs=[_HBM] * n + [_SEM, _SEM, pl.BlockSpec(memory_space=pl.ANY)], out_specs=[_HBM] * n,
        input_output_aliases={t: t for t in range(n)},
        compiler_params=pltpu.CompilerParams(has_side_effects=_EFFECT),
    )(*lands, sems[0], sems[1], after)


def _copy_own(name, src, land, route, me):
    gather = route.mode == "gather"
    shard = src.shape if gather else land.shape[1:]
    lead = () if gather else (None,)

    if route.kind == "mid":
        grid = (1,)
        at_full = lambda i, me: (0, me[0], 0)
        at_shard = lambda i, me: (0, 0, 0)
        block = tuple(shard)
    else:
        rows, width = shard
        tr = _tile(rows, 512, 16)
        grid = (rows // tr,)
        block = (tr, width)
        if route.kind == "col":
            at_full = lambda i, me: (i, me[0])
        else:
            at_full = lambda i, me: (me[0] * grid[0] + i, 0)
        at_shard = lambda i, me: (i, 0)
    if gather:
        in_map, out_map = at_shard, at_full
    else:
        in_map, out_map = at_full, (lambda i, me: (me[0], *at_shard(i, me)))

    def body(me_ref, src_ref, land_ref, out_ref):
        out_ref[...] = src_ref[...]

    return pl.pallas_call(
        body, name=name, out_shape=jax.ShapeDtypeStruct(land.shape, land.dtype),
        grid_spec=pltpu.PrefetchScalarGridSpec(
            num_scalar_prefetch=1, grid=grid,
            in_specs=[pl.BlockSpec(block, in_map), pl.BlockSpec(memory_space=pl.ANY)],
            out_specs=pl.BlockSpec(lead + block, out_map)),
        input_output_aliases={2: 0},
        compiler_params=_cparams("arbitrary"),
    )(me, src, land)


def _place_own(name, srcs, lands, routes):
    me = _blk(*_my_place()).astype(jnp.int32).reshape(1)
    return [_copy_own(f"{name}_{t}", s, l, r, me) for t, (s, l, r) in enumerate(zip(srcs, lands, routes))]


def _exchange(fulls, kinds, sizes, whole):
    arrays = list(fulls) + list(whole)
    nt, nf = len(arrays), len(fulls)

    def shard_shape(t):
        s = list(arrays[t].shape)
        if t < nf:
            s[{"col": 1, "row": 0, "mid": 1}[kinds[t]]] = sizes[t]
        return tuple(s)

    def body(*refs):
        ins, outs = refs[:nt], refs[nt:2 * nt]
        send_sems, recv_sems, local_sems = refs[2 * nt:]
        x, y, c = _my_place()
        me = _blk(x, y, c)

        def src(t, blk):
            return _slab(ins[t], kinds[t], blk, sizes[t]) if t < nf else ins[t]

        mine = [pltpu.make_async_copy(src(t, me), outs[t].at[me], local_sems.at[t]) for t in range(nt)]
        for cp in mine:
            cp.start()
        sent = []
        for t in range(nt):
            for k in range(1, 8):
                peer = (x ^ (k >> 2), y ^ ((k >> 1) & 1), c ^ (k & 1))
                pb = _blk(*peer)
                cp = pltpu.make_async_remote_copy(
                    src_ref=src(t, pb), dst_ref=outs[t].at[me],
                    send_sem=send_sems.at[7 * t + k - 1], recv_sem=recv_sems.at[7 * t + k - 1],
                    device_id=peer, device_id_type=_MESH)
                cp.start()
                sent.append((cp, t, k, pb))
        for cp, t, k, pb in sent:
            pltpu.make_async_remote_copy(
                src_ref=src(t, pb), dst_ref=outs[t].at[pb],
                send_sem=send_sems.at[7 * t + k - 1], recv_sem=recv_sems.at[7 * t + k - 1],
                device_id=(x, y, c), device_id_type=_MESH).wait_recv()
        for cp, _, _, _ in sent:
            cp.wait_send()
        for cp in mine:
            cp.wait()

    return pl.pallas_call(
        body, name="exchange_grads",
        in_specs=[_HBM] * nt, out_specs=[_HBM] * nt,
        out_shape=[jax.ShapeDtypeStruct((_N_DEV,) + shard_shape(t), arrays[t].dtype) for t in range(nt)],
        scratch_shapes=[pltpu.SemaphoreType.DMA((7 * nt,)), pltpu.SemaphoreType.DMA((7 * nt,)), pltpu.SemaphoreType.DMA((nt,))],
        compiler_params=pltpu.CompilerParams(has_side_effects=True),
    )(*arrays)


def _adamw_update(w_ref, m_ref, v_ref, g, g_ref, d_ref, nm_ref, nv_ref):
    c1 = 1.0 - _ADAM_B1 ** _ADAM_STEP
    c2 = 1.0 - _ADAM_B2 ** _ADAM_STEP
    nm = _ADAM_B1 * m_ref[...] + (1.0 - _ADAM_B1) * g
    nv = _ADAM_B2 * v_ref[...] + (1.0 - _ADAM_B2) * (g * g)
    g_ref[...] = g
    nm_ref[...] = nm
    nv_ref[...] = nv
    d_ref[...] = -_ADAM_LR * ((nm / c1) / (jnp.sqrt(nv / c2) + _ADAM_EPS) + _ADAM_WD * w_ref[...])


def _adamw_small(me, ws, ms, vs, parts):
    n = len(ws)

    def body(me_ref, *refs):
        ins, outs = refs[:4 * n], refs[4 * n:]
        for t in range(n):
            w_ref, m_ref, v_ref, p_ref = ins[4 * t:4 * t + 4]
            r = w_ref.shape[0]
            g = p_ref[0, 0:r, :]
            for s in range(1, _N_DEV):
                g = g + p_ref[s, 0:r, :]
            _adamw_update(w_ref, m_ref, v_ref, g, *outs[4 * t:4 * t + 4])

    whole = lambda a: pl.BlockSpec(a.shape, lambda i, me, nd=a.ndim: (0,) * nd)
    in_specs, operands, out_specs, out_shape = [], [], [], []
    for w, m, v, p in zip(ws, ms, vs, parts):
        c = w.shape[1]
        mine = whole(p) if p.shape[2] == c else pl.BlockSpec((_N_DEV, p.shape[1], c), lambda i, me: (0, 0, me[0]))
        in_specs += [whole(w), whole(m), whole(v), mine]
        operands += [w, m, v, p]
        out_specs += [whole(w)] * 4
        out_shape += [jax.ShapeDtypeStruct(w.shape, f32)] * 4
    return pl.pallas_call(
        body, name="adamw_small", out_shape=out_shape,
        grid_spec=pltpu.PrefetchScalarGridSpec(num_scalar_prefetch=1, grid=(1,), in_specs=in_specs, out_specs=out_specs),
        compiler_params=_cparams("arbitrary"),
    )(me, *operands)


def _adamw(name, w, m, v, parts):
    R, C = w.shape
    n = parts.shape[0]
    tr = _tile(R, 256, 8)

    def body(w_ref, m_ref, v_ref, p_ref, g_ref, d_ref, nm_ref, nv_ref):
        g = p_ref[0].astype(f32)
        for s in range(1, n):
            g = g + p_ref[s].astype(f32)
        _adamw_update(w_ref, m_ref, v_ref, g, g_ref, d_ref, nm_ref, nv_ref)

    blk = pl.BlockSpec((tr, C), lambda i: (i, 0))
    sd = jax.ShapeDtypeStruct((R, C), f32)
    return pl.pallas_call(
        body, name=name, grid=(R // tr,),
        in_specs=[blk, blk, blk, pl.BlockSpec((n, tr, C), lambda i: (0, i, 0))],
        out_specs=[blk, blk, blk, blk], out_shape=[sd, sd, sd, sd],
        compiler_params=_cparams("parallel"),
    )(w, m, v, parts)


def _pad_rows8(a):
    return jnp.pad(a, ((0, 8 - a.shape[0]), (0, 0)))


def _local_step(x, target, p, get, emit, hint):
    T, D = x.shape

    def tie(a, *tokens):
        for tok in tokens:
            if tok is not None:
                a = a + tok[0, 0]
        return a

    n_parts = _N_DEV // _W_IN_PART
    w_in, base = get("w_in", x, 0)
    h1, proj = _in_proj_first(x, p["g_mix"], w_in, base, _W_IN_PART)
    hint("w_in", proj)
    for part in range(1, n_parts):
        w_in, base = get("w_in", proj, part)
        proj = _in_proj_more(f"in_proj_{part}", h1, w_in, proj, base, _W_IN_PART)
    w_pool = get("w_pool", proj)
    PW = w_pool.shape[0] * w_pool.shape[1]
    y_pool = _pool_fwd(proj, w_pool, p["pool_scale"])
    tok = hint("w_pool_proj", y_pool)
    lru_conv_w, w_a, w_i = get("lru_conv_w", proj), get("w_a", proj), get("w_i", proj)
    y_lru, hs = _lru_fwd(proj, D, PW, lru_conv_w, tie(p["lru_conv_b"], tok), w_a, p["b_a"], w_i, p["b_i"], p["lru_lambda"])
    tok = hint("w_up", y_lru)
    w_pp, w_lp = get("w_pool_proj", y_lru), get("w_lru_proj", y_lru)
    pp, qq, merged = _merge_fwd(y_pool, y_lru, w_pp, w_lp, proj, tie(p["b_gate"], tok))
    w_out = get("w_out", y_lru)
    x2, h2 = _out_proj(merged, w_out, x, p["g_mlp"])
    w_up = get("w_up", x2)
    up = _mm_nn("up_proj", h2, w_up, f32, tm_want=2 * _TM)
    tok = hint("w_down", up)
    ffn_conv_w = get("ffn_conv_w", y_lru)
    z = _ffn_fwd(up, ffn_conv_w, tie(p["ffn_conv_b"], tok))
    w_down = get("w_down", z)
    F = w_down.shape[0]
    dx3, dx3b, loss_t, dg_final = _down_loss(z, w_down, x2, target, p["g_final"])

    gs = {"g_final": dg_final}
    tok = emit("w_down", _mm_tn("dw_down", z, dx3b, bf16, tm_want=1536))
    dup, dcw_ffn, dcb_ffn = _ffn_bwd(dx3b, w_down, up, ffn_conv_w, tie(p["ffn_conv_b"], tok))
    gs["ffn_conv_w"] = dcw_ffn
    gs["ffn_conv_b"] = dcb_ffn

    tm = _tile(T, _TM)
    tk = _tile(F, _TN_MAX)
    nkh = F // tk
    tkt = _tile(T, _TK_T)
    tok = emit("w_up", _matmul(
        "dw_up", "tn", (h2, dup),
        [pl.BlockSpec((tkt, D), lambda i, j, k: (k, 0)), pl.BlockSpec((None, tkt, tk), lambda i, j, k: (j // nkh, k, j % nkh))],
        jax.ShapeDtypeStruct((D, 2 * F), bf16), pl.BlockSpec((D, tk), lambda i, j, k: (0, j)),
        (1, 2 * nkh, T // tkt), (D, tk)))

    tkc = _tile(F, _TK_UP)
    nkc = F // tkc
    dx2, dx2b, gs["g_mlp"] = _norm_bwd_matmul(
        "dh2", dup, pl.BlockSpec((None, tm, tkc), lambda i, k: (k // nkc, i, k % nkc)), 2 * nkc,
        w_up, pl.BlockSpec((D, tkc), lambda i, k: (0, k)), x2, tie(p["g_mlp"], tok), dx3)

    tok = emit("w_out", _mm_tn("dw_out", merged, dx2b, bf16))
    dP, dQ, dl0, dl1, db0, db1 = _merge_bwd(dx2b, w_out, pp, qq, proj, tie(p["b_gate"], tok), PW)
    gs["b_gate"] = jnp.concatenate([db0, db1], axis=1)
    tok = emit("w_pool_proj", _mm_tn("dw_pool_proj", y_pool, dP, bf16))
    tok2 = emit("w_lru_proj", _mm_tn("dw_lru_proj", y_lru, dQ, bf16))
    dy_pool = _mm_nt("dy_pool", dP, w_pp, f32)
    dy_lru = _mm_nt("dy_lru", dQ, w_lp, f32)

    du_lru, du_gelu, dwa, dwi, dcw_lru, dvec = _lru_bwd(
        dy_lru, proj, hs, D, lru_conv_w, tie(p["lru_conv_b"], tok, tok2), w_a, p["b_a"], w_i, p["b_i"], p["lru_lambda"])
    tok = emit("w_a", dwa.astype(bf16))
    tok2 = emit("w_i", dwi.astype(bf16))
    gs["lru_conv_w"] = dcw_lru
    gs["lru_conv_b"], gs["b_a"], gs["b_i"], gs["lru_lambda"] = dvec[0:1], dvec[1:2], dvec[2:3], dvec[3:4]
    du_pool, dwp, gs["pool_scale"] = _pool_bwd(dy_pool, proj, w_pool, tie(p["pool_scale"], tok, tok2))
    tok = emit("w_pool", dwp.astype(bf16))

    dproj = jnp.concatenate([du_pool, du_lru, du_gelu, dl0, dl1], axis=1)
    tok2 = emit("w_in", _mm_tn("dw_in", h1, dproj, bf16))
    NI = dproj.shape[1]
    tki = _tile(NI, _TK_IN, _LANE)
    grad_x, _, gs["g_mix"] = _norm_bwd_matmul(
        "dh1", dproj, pl.BlockSpec((tm, tki), lambda i, k: (i, k)), NI // tki,
        w_in, pl.BlockSpec((D, tki), lambda i, k: (0, k)), x, tie(p["g_mix"], tok, tok2), dx2)
    return loss_t[0, 0], grad_x, gs


_MATRICES = {"w_in": "col", "w_pool": "mid", "w_a": "mid", "w_i": "mid", "w_pool_proj": "col", "w_lru_proj": "row",
             "w_out": "row", "w_up": "col", "w_down": "row"}
_CONVS = ("lru_conv_w", "ffn_conv_w")
_GATHER_GROUPS = (("w_pool", "lru_conv_w", "w_a", "w_i"), ("w_pool_proj", "w_lru_proj", "w_out", "ffn_conv_w"),
                  ("w_up",), ("w_down",))
_GROUP_TWO_LEVEL = (False, True, True, True)
_SAME_CORE = (2, 4, 6)
_TWO_LEVEL = (1, 2, 4, 6)
_W_IN_PARTS = ((1,), (4,), (2,), (6,))
_W_IN_PART = 2
_VECTORS = ("g_mix", "b_gate", "pool_scale", "lru_conv_b", "b_a", "b_i", "lru_lambda", "g_mlp", "ffn_conv_b", "g_final")
_WEIGHTS = ("g_mix", "w_in", "b_gate", "w_pool", "pool_scale", "lru_conv_w", "lru_conv_b", "w_a", "b_a", "w_i", "b_i",
            "lru_lambda", "w_pool_proj", "w_lru_proj", "w_out", "g_mlp", "w_up", "ffn_conv_w", "ffn_conv_b", "w_down", "g_final")


def _full_shape(shape, kind):
    s = list(shape)
    s[{"col": 1, "row": 0, "mid": 1}[kind]] *= _N_DEV
    return tuple(s)


def _step(x, target, w, m, v):
    x, target = x[0], target[0]
    me = _blk(*_my_place())

    axis = {"col": 1, "row": 0, "mid": 1}
    kind = dict(_MATRICES, **{n: "col" for n in _CONVS})
    shard = {n: w[n].astype(bf16) for n in _MATRICES}
    shard.update({n: _pad_rows8(w[n]) for n in _CONVS})
    order = ["w_in"] + [n for grp in _GATHER_GROUPS for n in grp]
    index = {n: t for t, n in enumerate(order)}
    n_parts = len(_W_IN_PARTS)
    groups = [([0], ks) for ks in _W_IN_PARTS]
    groups += [([index[n] for n in grp], _TWO_LEVEL if two else _ALL_PEERS) for grp, two in zip(_GATHER_GROUPS, _GROUP_TWO_LEVEL)]
    g_routes = [_Route("gather", kind[n], shard[n].shape[axis[kind[n]]]) for n in order]
    lands = [lax.empty(_full_shape(shard[n].shape, kind[n]), shard[n].dtype) for n in order]
    g_sems, g_srcs, g_lands, _ = _send_start("gather_start", [shard[n] for n in order], lands, g_routes, groups)
    gathered, passing = {}, {}
    x_, y_, _c = _my_place()
    w_in_state = [[g_srcs[0]], [g_lands[0]]]

    def hint(name, after):
        if name == "w_in":
            srcs, got = w_in_state
            tok = None
            for part in range(1, n_parts):
                ks = _W_IN_PARTS[part]
                srcs, got = _send_wait(f"gather_wait_w_in_{part}", srcs, got, g_routes[:1], g_sems[part], after, ks)
                sems, got, tok = _forward_start(f"gather_pass_w_in_{part}", got, g_routes[:1], ks)
                passing[name, part] = sems
            w_in_state[:] = [srcs, got]
            return tok
        gi = next(i for i, grp in enumerate(_GATHER_GROUPS) if name in grp)
        if not _GROUP_TWO_LEVEL[gi] or gi in passing or _GATHER_GROUPS[gi][0] in gathered:
            return None
        ts = groups[n_parts + gi][0]
        routes = [g_routes[t] for t in ts]
        srcs, got = _send_wait(f"gather_wait_{gi}", [g_srcs[t] for t in ts], [g_lands[t] for t in ts], routes,
                               g_sems[n_parts + gi], after, _TWO_LEVEL)
        got = _place_own(f"gather_own_{gi}", srcs, got, routes)
        sems, got, tok = _forward_start(f"gather_pass_{gi}", got, routes, _SAME_CORE)
        passing[gi] = (sems, got, routes)
        return tok

    def get(name, after, part=None):
        if name == "w_in":
            ks = _W_IN_PARTS[part]
            if part == 0:
                srcs, got = _send_wait("gather_wait_w_in_0", *w_in_state, g_routes[:1], g_sems[0], after, ks)
                got = _place_own("gather_own_w_in", srcs, got, g_routes[:1])
                w_in_state[:] = [srcs, got]
            else:
                if (name, part) not in passing:
                    hint(name, after)
                got = _forward_wait(f"gather_got_w_in_{part}", w_in_state[1], g_routes[:1], passing[name, part], after, ks)
                w_in_state[1] = got
            k = ks[-1]
            base = 4 * (x_ ^ (k >> 2)) + 2 * (y_ ^ ((k >> 1) & 1))
            return got[0], base.astype(jnp.int32).reshape(1)
        if name not in gathered:
            gi = next(i for i, grp in enumerate(_GATHER_GROUPS) if name in grp)
            if _GROUP_TWO_LEVEL[gi]:
                hint(name, after)
                sems, got, routes = passing[gi]
                full = _forward_wait(f"gather_got_{gi}", got, routes, sems, after, _SAME_CORE)
            else:
                ts = groups[n_parts + gi][0]
                routes = [g_routes[t] for t in ts]
                srcs, got = _send_wait(f"gather_wait_{gi}", [g_srcs[t] for t in ts], [g_lands[t] for t in ts], routes,
                                       g_sems[n_parts + gi], after)
                full = _place_own(f"gather_own_{gi}", srcs, got, routes)
            gathered.update(zip(_GATHER_GROUPS[gi], full))
        return gathered[name]

    sent = {}

    def emit(name, grad):
        k = _MATRICES[name]
        size = w[name].shape[axis[k]]
        route = _Route("scatter", k, size)
        shp = list(grad.shape)
        shp[axis[k]] = size
        land = lax.empty((_N_DEV, *shp), grad.dtype)
        sems, srcs, lnds, token = _send_start("grad_start_" + name, [grad], [land], [route], [([0], _ALL_PEERS)])
        sent[name] = (srcs, lnds, [route], sems[0])
        return token

    p = {n: w[n].reshape(1, -1) for n in _VECTORS}
    loss_t, grad_x, gs = _local_step(x, target, p, get, emit, hint)
    loss = lax.psum(loss_t, ("x", "y", "c"))

    small_names = list(_VECTORS) + list(_CONVS)
    small_parts = _exchange([], [], [], [gs[n] for n in small_names])

    out = {}
    mats = list(_MATRICES)
    for n in mats:
        srcs, lnds, routes, sems = sent[n]
        srcs, got = _send_wait("grad_wait_" + n, srcs, lnds, routes, sems, grad_x)
        parts = _place_own("grad_own_" + n, srcs, got, routes)[0]
        shp = w[n].shape
        r2 = (-1, shp[-1])
        res = _adamw("adamw_" + n, w[n].reshape(r2), m[n].reshape(r2), v[n].reshape(r2),
                     parts.reshape((_N_DEV,) + w[n].reshape(r2).shape))
        out[n] = [a.reshape(shp) for a in res]
    two_d = lambda a: a.reshape(-1, a.shape[-1])
    res = _adamw_small(me.astype(jnp.int32).reshape(1), [two_d(w[n]) for n in small_names], [two_d(m[n]) for n in small_names],
                       [two_d(v[n]) for n in small_names], small_parts)
    for t, n in enumerate(small_names):
        out[n] = [a.reshape(w[n].shape) for a in res[4 * t:4 * t + 4]]
    return loss, grad_x[None], out


def kernel(x, g_mix, w_in, b_gate, w_pool, pool_scale, lru_conv_w, lru_conv_b, w_a, b_a, w_i, b_i, lru_lambda, w_pool_proj, w_lru_proj, w_out, g_mlp, w_up, ffn_conv_w, ffn_conv_b, w_down, g_final, loss_target, m_g_mix, m_w_in, m_b_gate, m_w_pool, m_pool_scale, m_lru_conv_w, m_lru_conv_b, m_w_a, m_b_a, m_w_i, m_b_i, m_lru_lambda, m_w_pool_proj, m_w_lru_proj, m_w_out, m_g_mlp, m_w_up, m_ffn_conv_w, m_ffn_conv_b, m_w_down, m_g_final, v_g_mix, v_w_in, v_b_gate, v_w_pool, v_pool_scale, v_lru_conv_w, v_lru_conv_b, v_w_a, v_b_a, v_w_i, v_b_i, v_lru_lambda, v_w_pool_proj, v_w_lru_proj, v_w_out, v_g_mlp, v_w_up, v_ffn_conv_w, v_ffn_conv_b, v_w_down, v_g_final):
    given = dict(locals())
    orig = {n: given[n].shape for n in _WEIGHTS}

    def squeeze(a, n):
        return a if n == "g_final" else a[0]

    w = {n: squeeze(given[n], n) for n in _WEIGHTS}
    m = {n: squeeze(given["m_" + n], n) for n in _WEIGHTS}
    v = {n: squeeze(given["v_" + n], n) for n in _WEIGHTS}
    for d in (w, m, v):
        d["g_final"] = d["g_final"].reshape(1, -1)
    loss, grad_x, out = _step(x, loss_target, w, m, v)
    res = [loss, grad_x]
    for k in range(4):
        res += [out[n][k].reshape(orig[n]) for n in _WEIGHTS]
    return tuple(res)
```
